```python
import math
import jax, jax.numpy as jnp
from jax import lax
import numpy as np


D_MODEL = 1024
BATCH = 8
SEQ = 4096
DEPTH = 1

HEAD_DIM = 64
ATTN_WIDTH = D_MODEL // 2
N_ATTN_HEADS = ATTN_WIDTH // HEAD_DIM
REC_WIDTH = D_MODEL - ATTN_WIDTH
REC_BLOCKS = 8
REC_BLOCK = REC_WIDTH // REC_BLOCKS
MIX_WIDTH = ATTN_WIDTH + REC_WIDTH
IN_WIDTH = 3 * ATTN_WIDTH + 2 * REC_WIDTH
REC_CONV = 4
LRU_C = 8.0
D_FF = 3 * D_MODEL
FFN_CONV = 3
WINDOW_DILATIONS = ((128, 1), (512, 4), (2048, 16))
BLOCK = 128
ROPE_THETA = 10000.0
EPS = 1e-6
NEG_INF = -1e30

kernel_name = "hybrid_dilated_attn_rglru_convffn"


def rms_norm(x, g):
    xf = x.astype(jnp.float32)
    y = xf * lax.rsqrt(jnp.mean(xf * xf, axis=-1, keepdims=True) + EPS)
    return (y * g.astype(jnp.float32)).astype(x.dtype)


def rotary(x, positions):
    half = HEAD_DIM // 2
    inv_freq = ROPE_THETA ** (-jnp.arange(half, dtype=jnp.float32) / half)
    ang = positions.astype(jnp.float32)[..., None] * inv_freq
    cos = jnp.cos(ang)[:, :, None, :]
    sin = jnp.sin(ang)[:, :, None, :]
    xf = x.astype(jnp.float32)
    x1, x2 = xf[..., :half], xf[..., half:]
    return jnp.concatenate([x1 * cos - x2 * sin, x2 * cos + x1 * sin], axis=-1).astype(x.dtype)


def causal_depthwise_conv(x, w, b):
    k_width = w.shape[0]
    s = x.shape[1]
    xp = jnp.pad(x, ((0, 0), (k_width - 1, 0), (0, 0)))
    y = b
    for k in range(k_width):
        y = y + xp[:, k:k + s, :] * w[k]
    return y


def dilated_window_branch(q, k, v, window, dilation):
    bsz, s, h, d = q.shape
    length = s // dilation
    span = window // dilation
    assert span <= BLOCK
    nb = -(-length // BLOCK)
    lp = nb * BLOCK

    def regroup(t):
        return t.reshape(bsz, length, dilation, h, d).transpose(0, 2, 3, 1, 4)

    qs = jnp.pad(regroup(q), ((0, 0), (0, 0), (0, 0), (0, lp - length), (0, 0)))
    ks = jnp.pad(regroup(k), ((0, 0), (0, 0), (0, 0), (BLOCK, lp - length), (0, 0)))
    vs = jnp.pad(regroup(v), ((0, 0), (0, 0), (0, 0), (BLOCK, lp - length), (0, 0)))
    qb = qs.reshape(bsz, dilation, h, nb, BLOCK, d)
    kb = ks.reshape(bsz, dilation, h, nb + 1, BLOCK, d)
    vb = vs.reshape(bsz, dilation, h, nb + 1, BLOCK, d)
    kwin = jnp.concatenate([kb[:, :, :, :-1], kb[:, :, :, 1:]], axis=4)
    vwin = jnp.concatenate([vb[:, :, :, :-1], vb[:, :, :, 1:]], axis=4)

    scores = jnp.einsum('bchnqd,bchnkd->bchnqk', qb, kwin).astype(jnp.float32)
    qi = jnp.arange(BLOCK)[:, None]
    kj = jnp.arange(2 * BLOCK)[None, :]
    rel = qi - kj + BLOCK
    band = (rel >= 0) & (rel <= span)
    blk = jnp.arange(nb)[:, None, None]
    key_ok = (blk * BLOCK + kj[None] - BLOCK) >= 0
    mask = band[None] & key_ok
    scores = jnp.where(mask, scores, NEG_INF)
    m = jnp.max(scores, axis=-1, keepdims=True)
    p = jnp.exp(scores - m)
    l = jnp.sum(p, axis=-1, keepdims=True)
    o = jnp.einsum('bchnqk,bchnkd->bchnqd', p, vwin.astype(jnp.float32)) / l
    lse = (m + jnp.log(l))[..., 0]

    o = o.reshape(bsz, dilation, h, lp, d)[:, :, :, :length]
    lse = lse.reshape(bsz, dilation, h, lp)[:, :, :, :length]
    o = o.transpose(0, 3, 1, 2, 4).reshape(bsz, s, h, d)
    lse = lse.transpose(0, 3, 1, 2).reshape(bsz, s, h)
    return o, lse


def dilated_attention(q, k, v):
    outs, lses = [], []
    for window, dilation in WINDOW_DILATIONS:
        o, lse = dilated_window_branch(q, k, v, window, dilation)
        outs.append(o)
        lses.append(lse)
    wts = jax.nn.softmax(jnp.stack(lses, axis=0), axis=0)
    return jnp.einsum('gbsh,gbshd->bshd', wts, jnp.stack(outs, axis=0))


def lru_combine(left, right):
    a_l, b_l = left
    a_r, b_r = right
    return a_l * a_r, a_r * b_l + b_r


def rg_lru(xr, w_rg, b_rg, w_ig, b_ig, lru_lambda):
    bsz, s, _ = xr.shape
    xb = xr.reshape(bsz, s, REC_BLOCKS, REC_BLOCK)
    r = jax.nn.sigmoid(jnp.einsum('bsnc,ncd->bsnd', xb, w_rg) + b_rg).reshape(bsz, s, REC_WIDTH)
    i = jax.nn.sigmoid(jnp.einsum('bsnc,ncd->bsnd', xb, w_ig) + b_ig).reshape(bsz, s, REC_WIDTH)
    r = r.astype(jnp.float32)
    i = i.astype(jnp.float32)
    log_a = -LRU_C * r * jax.nn.softplus(-lru_lambda.astype(jnp.float32))
    a = jnp.exp(log_a)
    mult = jnp.sqrt(-jnp.expm1(2.0 * log_a))
    u = mult * (i * xr.astype(jnp.float32))
    _, hseq = lax.associative_scan(lru_combine, (a, u), axis=1)
    return hseq.astype(xr.dtype)


def _fwd_setup_inputs(seed: int = 0) -> dict:
    key = jax.random.key(seed)
    ks = jax.random.split(key, 24)
    f32 = jnp.float32

    def nrm(k, shape, scale):
        return jax.random.normal(k, shape, f32) * scale

    def gain(k, shape):
        return 1.0 + 0.01 * jax.random.normal(k, shape, f32)

    x = jax.random.normal(ks[0], (BATCH, SEQ, D_MODEL), f32)
    positions = jnp.broadcast_to(jnp.arange(SEQ, dtype=jnp.int32)[None, :], (BATCH, SEQ))
    a_c = jax.random.uniform(ks[11], (DEPTH, REC_WIDTH), f32, 0.9, 0.999)
    sig = a_c ** (1.0 / LRU_C)
    lru_lambda = jnp.log(sig) - jnp.log1p(-sig)
    return {
        "x": x,
        "positions": positions,
        "g_mix": gain(ks[1], (DEPTH, D_MODEL)),
        "w_in": nrm(ks[2], (DEPTH, D_MODEL, IN_WIDTH), D_MODEL ** -0.5),
        "q_norm_g": gain(ks[3], (DEPTH, HEAD_DIM)),
        "k_norm_g": gain(ks[4], (DEPTH, HEAD_DIM)),
        "rec_conv_w": nrm(ks[5], (DEPTH, REC_CONV, REC_WIDTH), REC_CONV ** -0.5),
        "rec_conv_b": nrm(ks[6], (DEPTH, REC_WIDTH), 0.01),
        "w_rg": nrm(ks[7], (DEPTH, REC_BLOCKS, REC_BLOCK, REC_BLOCK), REC_BLOCK ** -0.5),
        "b_rg": nrm(ks[8], (DEPTH, REC_BLOCKS, REC_BLOCK), 0.01),
        "w_ig": nrm(ks[9], (DEPTH, REC_BLOCKS, REC_BLOCK, REC_BLOCK), REC_BLOCK ** -0.5),
        "b_ig": nrm(ks[10], (DEPTH, REC_BLOCKS, REC_BLOCK), 0.01),
        "lru_lambda": lru_lambda,
        "g_attn_out": gain(ks[12], (DEPTH, ATTN_WIDTH)),
        "g_rec_out": gain(ks[13], (DEPTH, REC_WIDTH)),
        "w_out": nrm(ks[14], (DEPTH, MIX_WIDTH, D_MODEL), MIX_WIDTH ** -0.5),
        "g_ffn": gain(ks[15], (DEPTH, D_MODEL)),
        "w_up": nrm(ks[16], (DEPTH, D_MODEL, 2 * D_FF), D_MODEL ** -0.5),
        "ffn_conv_w": nrm(ks[17], (DEPTH, FFN_CONV, 2 * D_FF), FFN_CONV ** -0.5),
        "ffn_conv_b": nrm(ks[18], (DEPTH, 2 * D_FF), 0.01),
        "w_down": nrm(ks[19], (DEPTH, D_FF, D_MODEL), D_FF ** -0.5),
    }


def _fwd_reference(x, positions, g_mix, w_in, q_norm_g, k_norm_g, rec_conv_w, rec_conv_b,
              w_rg, b_rg, w_ig, b_ig, lru_lambda, g_attn_out, g_rec_out, w_out,
              g_ffn, w_up, ffn_conv_w, ffn_conv_b, w_down):
    bsz, s, _ = x.shape
    for layer in range(DEPTH):
        h = rms_norm(x, g_mix[layer])
        proj = h @ w_in[layer]
        q, k, v, xr, gr = jnp.split(
            proj, [ATTN_WIDTH, 2 * ATTN_WIDTH, 3 * ATTN_WIDTH, 3 * ATTN_WIDTH + REC_WIDTH], axis=-1)
        q = q.reshape(bsz, s, N_ATTN_HEADS, HEAD_DIM)
        k = k.reshape(bsz, s, N_ATTN_HEADS, HEAD_DIM)
        v = v.reshape(bsz, s, N_ATTN_HEADS, HEAD_DIM)
        q = rotary(rms_norm(q, q_norm_g[layer]), positions) * (HEAD_DIM ** -0.5)
        k = rotary(rms_norm(k, k_norm_g[layer]), positions)
        attn = dilated_attention(q, k, v).astype(x.dtype).reshape(bsz, s, ATTN_WIDTH)
        attn = rms_norm(attn, g_attn_out[layer])

        xr = causal_depthwise_conv(xr, rec_conv_w[layer], rec_conv_b[layer])
        rec = rg_lru(xr, w_rg[layer], b_rg[layer], w_ig[layer], b_ig[layer], lru_lambda[layer])
        rec = rms_norm(rec * jax.nn.gelu(gr), g_rec_out[layer])

        x = x + jnp.concatenate([attn, rec], axis=-1) @ w_out[layer]

        h = rms_norm(x, g_ffn[layer])
        u = causal_depthwise_conv(h @ w_up[layer], ffn_conv_w[layer], ffn_conv_b[layer])
        gate, up = jnp.split(u, 2, axis=-1)
        x = x + (jax.nn.gelu(gate) * up) @ w_down[layer]
    return x


import jax as _jax
import jax.numpy as _jnp

TWIN_FORMAT = 'train_step'
FWD_PARAMS = ['x', 'positions', 'g_mix', 'w_in', 'q_norm_g', 'k_norm_g', 'rec_conv_w', 'rec_conv_b', 'w_rg', 'b_rg', 'w_ig', 'b_ig', 'lru_lambda', 'g_attn_out', 'g_rec_out', 'w_out', 'g_ffn', 'w_up', 'ffn_conv_w', 'ffn_conv_b', 'w_down']
TWIN_WEIGHTS = ['g_mix', 'w_in', 'q_norm_g', 'k_norm_g', 'rec_conv_w', 'rec_conv_b', 'w_rg', 'b_rg', 'w_ig', 'b_ig', 'lru_lambda', 'g_attn_out', 'g_rec_out', 'w_out', 'g_ffn', 'w_up', 'ffn_conv_w', 'ffn_conv_b', 'w_down']
TWIN_DIFF_INPUT = 'x'
TWIN_INPUTS = ['x', 'positions', 'g_mix', 'w_in', 'q_norm_g', 'k_norm_g', 'rec_conv_w', 'rec_conv_b', 'w_rg', 'b_rg', 'w_ig', 'b_ig', 'lru_lambda', 'g_attn_out', 'g_rec_out', 'w_out', 'g_ffn', 'w_up', 'ffn_conv_w', 'ffn_conv_b', 'w_down', 'loss_target', 'm_g_mix', 'm_w_in', 'm_q_norm_g', 'm_k_norm_g', 'm_rec_conv_w', 'm_rec_conv_b', 'm_w_rg', 'm_b_rg', 'm_w_ig', 'm_b_ig', 'm_lru_lambda', 'm_g_attn_out', 'm_g_rec_out', 'm_w_out', 'm_g_ffn', 'm_w_up', 'm_ffn_conv_w', 'm_ffn_conv_b', 'm_w_down', 'v_g_mix', 'v_w_in', 'v_q_norm_g', 'v_k_norm_g', 'v_rec_conv_w', 'v_rec_conv_b', 'v_w_rg', 'v_b_rg', 'v_w_ig', 'v_b_ig', 'v_lru_lambda', 'v_g_attn_out', 'v_g_rec_out', 'v_w_out', 'v_g_ffn', 'v_w_up', 'v_ffn_conv_w', 'v_ffn_conv_b', 'v_w_down']
TWIN_OUTPUTS = ['loss', 'grad_x', 'grad_g_mix', 'grad_w_in', 'grad_q_norm_g', 'grad_k_norm_g', 'grad_rec_conv_w', 'grad_rec_conv_b', 'grad_w_rg', 'grad_b_rg', 'grad_w_ig', 'grad_b_ig', 'grad_lru_lambda', 'grad_g_attn_out', 'grad_g_rec_out', 'grad_w_out', 'grad_g_ffn', 'grad_w_up', 'grad_ffn_conv_w', 'grad_ffn_conv_b', 'grad_w_down', 'delta_g_mix', 'delta_w_in', 'delta_q_norm_g', 'delta_k_norm_g', 'delta_rec_conv_w', 'delta_rec_conv_b', 'delta_w_rg', 'delta_b_rg', 'delta_w_ig', 'delta_b_ig', 'delta_lru_lambda', 'delta_g_attn_out', 'delta_g_rec_out', 'delta_w_out', 'delta_g_ffn', 'delta_w_up', 'delta_ffn_conv_w', 'delta_ffn_conv_b', 'delta_w_down', 'new_m_g_mix', 'new_m_w_in', 'new_m_q_norm_g', 'new_m_k_norm_g', 'new_m_rec_conv_w', 'new_m_rec_conv_b', 'new_m_w_rg', 'new_m_b_rg', 'new_m_w_ig', 'new_m_b_ig', 'new_m_lru_lambda', 'new_m_g_attn_out', 'new_m_g_rec_out', 'new_m_w_out', 'new_m_g_ffn', 'new_m_w_up', 'new_m_ffn_conv_w', 'new_m_ffn_conv_b', 'new_m_w_down', 'new_v_g_mix', 'new_v_w_in', 'new_v_q_norm_g', 'new_v_k_norm_g', 'new_v_rec_conv_w', 'new_v_rec_conv_b', 'new_v_w_rg', 'new_v_b_rg', 'new_v_w_ig', 'new_v_b_ig', 'new_v_lru_lambda', 'new_v_g_attn_out', 'new_v_g_rec_out', 'new_v_w_out', 'new_v_g_ffn', 'new_v_w_up', 'new_v_ffn_conv_w', 'new_v_ffn_conv_b', 'new_v_w_down']
TWIN_LEAF_KINDS = {'loss': 'loss', 'grad_x': 'grad_x', 'grad_g_mix': 'grad_w', 'grad_w_in': 'grad_w', 'grad_q_norm_g': 'grad_w', 'grad_k_norm_g': 'grad_w', 'grad_rec_conv_w': 'grad_w', 'grad_rec_conv_b': 'grad_w', 'grad_w_rg': 'grad_w', 'grad_b_rg': 'grad_w', 'grad_w_ig': 'grad_w', 'grad_b_ig': 'grad_w', 'grad_lru_lambda': 'grad_w', 'grad_g_attn_out': 'grad_w', 'grad_g_rec_out': 'grad_w', 'grad_w_out': 'grad_w', 'grad_g_ffn': 'grad_w', 'grad_w_up': 'grad_w', 'grad_ffn_conv_w': 'grad_w', 'grad_ffn_conv_b': 'grad_w', 'grad_w_down': 'grad_w', 'delta_g_mix': 'delta_w', 'delta_w_in': 'delta_w', 'delta_q_norm_g': 'delta_w', 'delta_k_norm_g': 'delta_w', 'delta_rec_conv_w': 'delta_w', 'delta_rec_conv_b': 'delta_w', 'delta_w_rg': 'delta_w', 'delta_b_rg': 'delta_w', 'delta_w_ig': 'delta_w', 'delta_b_ig': 'delta_w', 'delta_lru_lambda': 'delta_w', 'delta_g_attn_out': 'delta_w', 'delta_g_rec_out': 'delta_w', 'delta_w_out': 'delta_w', 'delta_g_ffn': 'delta_w', 'delta_w_up': 'delta_w', 'delta_ffn_conv_w': 'delta_w', 'delta_ffn_conv_b': 'delta_w', 'delta_w_down': 'delta_w', 'new_m_g_mix': 'new_m', 'new_m_w_in': 'new_m', 'new_m_q_norm_g': 'new_m', 'new_m_k_norm_g': 'new_m', 'new_m_rec_conv_w': 'new_m', 'new_m_rec_conv_b': 'new_m', 'new_m_w_rg': 'new_m', 'new_m_b_rg': 'new_m', 'new_m_w_ig': 'new_m', 'new_m_b_ig': 'new_m', 'new_m_lru_lambda': 'new_m', 'new_m_g_attn_out': 'new_m', 'new_m_g_rec_out': 'new_m', 'new_m_w_out': 'new_m', 'new_m_g_ffn': 'new_m', 'new_m_w_up': 'new_m', 'new_m_ffn_conv_w': 'new_m', 'new_m_ffn_conv_b': 'new_m', 'new_m_w_down': 'new_m', 'new_v_g_mix': 'new_v', 'new_v_w_in': 'new_v', 'new_v_q_norm_g': 'new_v', 'new_v_k_norm_g': 'new_v', 'new_v_rec_conv_w': 'new_v', 'new_v_rec_conv_b': 'new_v', 'new_v_w_rg': 'new_v', 'new_v_b_rg': 'new_v', 'new_v_w_ig': 'new_v', 'new_v_b_ig': 'new_v', 'new_v_lru_lambda': 'new_v', 'new_v_g_attn_out': 'new_v', 'new_v_g_rec_out': 'new_v', 'new_v_w_out': 'new_v', 'new_v_g_ffn': 'new_v', 'new_v_w_up': 'new_v', 'new_v_ffn_conv_w': 'new_v', 'new_v_ffn_conv_b': 'new_v', 'new_v_w_down': 'new_v'}


def _forward(args):
    return _fwd_reference(*[args[k] for k in FWD_PARAMS])


def _output_shape():
    def fwd():
        inp = _fwd_setup_inputs(0)
        return _fwd_reference(*[inp[k] for k in FWD_PARAMS])
    out = _jax.eval_shape(fwd)
    return out.shape, out.dtype

N_MICROBATCH = 1
ADAM_LR = 0.001
ADAM_B1 = 0.9
ADAM_B2 = 0.999
ADAM_EPS = 1e-08
ADAM_WD = 0.01
ADAM_STEP = 10
PER_EXAMPLE_BATCH_AXIS = {'x': 0, 'positions': 0, 'loss_target': 0}
SHARED_INPUTS = []
_WEIGHT_DTYPES = {'g_mix': _jnp.float32, 'w_in': _jnp.float32, 'q_norm_g': _jnp.float32, 'k_norm_g': _jnp.float32, 'rec_conv_w': _jnp.float32, 'rec_conv_b': _jnp.float32, 'w_rg': _jnp.float32, 'b_rg': _jnp.float32, 'w_ig': _jnp.float32, 'b_ig': _jnp.float32, 'lru_lambda': _jnp.float32, 'g_attn_out': _jnp.float32, 'g_rec_out': _jnp.float32, 'w_out': _jnp.float32, 'g_ffn': _jnp.float32, 'w_up': _jnp.float32, 'ffn_conv_w': _jnp.float32, 'ffn_conv_b': _jnp.float32, 'w_down': _jnp.float32}
MOMENT_SCALE = {'g_mix': 1.026504e+00, 'w_in': 6.555289e-01, 'q_norm_g': 1.307805e+00, 'k_norm_g': 1.379489e+00, 'rec_conv_w': 2.566679e+00, 'rec_conv_b': 3.216991e+01, 'w_rg': 1.191739e+00, 'b_rg': 6.493428e-01, 'w_ig': 2.164206e+00, 'b_ig': 6.849333e-01, 'lru_lambda': 1.009858e+00, 'g_attn_out': 3.203683e+01, 'g_rec_out': 5.149291e+01, 'w_out': 2.283817e+00, 'g_ffn': 2.873694e+01, 'w_up': 6.719015e-01, 'ffn_conv_w': 3.796383e+00, 'ffn_conv_b': 3.592460e+00, 'w_down': 5.249727e-01}


def _to_microbatches(a, axis):
    t = _jnp.moveaxis(a, axis, 0)
    t = t.reshape((N_MICROBATCH, t.shape[0] // N_MICROBATCH) + t.shape[1:])
    return _jnp.moveaxis(t, 1, axis + 1)


def setup_inputs(seed: int = 0) -> dict:
    inp = _fwd_setup_inputs(seed)
    key = _jax.random.fold_in(_jax.random.key(seed), 7919)
    shape, _ = _output_shape()
    out = dict(inp)
    out["loss_target"] = _jax.random.normal(_jax.random.fold_in(key, 0), shape, _jnp.float32)
    for i, name in enumerate(TWIN_WEIGHTS):
        w = inp[name].astype(_jnp.float32)
        if MOMENT_SCALE is None:
            s = _jnp.sqrt(_jnp.mean(_jnp.square(w)) + 1e-30)
        else:
            s = MOMENT_SCALE[name]
        km, kv = _jax.random.split(_jax.random.fold_in(key, i + 1))
        out[name] = w
        out["m_" + name] = s * _jax.random.normal(km, w.shape, _jnp.float32)
        out["v_" + name] = (s * s) * _jax.random.uniform(kv, w.shape, _jnp.float32, 0.5, 1.5)
    if N_MICROBATCH > 1:
        for name, axis in PER_EXAMPLE_BATCH_AXIS.items():
            out[name] = _to_microbatches(out[name], axis)
    return {'x': out['x'], 'positions': out['positions'], 'g_mix': out['g_mix'], 'w_in': out['w_in'], 'q_norm_g': out['q_norm_g'], 'k_norm_g': out['k_norm_g'], 'rec_conv_w': out['rec_conv_w'], 'rec_conv_b': out['rec_conv_b'], 'w_rg': out['w_rg'], 'b_rg': out['b_rg'], 'w_ig': out['w_ig'], 'b_ig': out['b_ig'], 'lru_lambda': out['lru_lambda'], 'g_attn_out': out['g_attn_out'], 'g_rec_out': out['g_rec_out'], 'w_out': out['w_out'], 'g_ffn': out['g_ffn'], 'w_up': out['w_up'], 'ffn_conv_w': out['ffn_conv_w'], 'ffn_conv_b': out['ffn_conv_b'], 'w_down': out['w_down'], 'loss_target': out['loss_target'], 'm_g_mix': out['m_g_mix'], 'm_w_in': out['m_w_in'], 'm_q_norm_g': out['m_q_norm_g'], 'm_k_norm_g': out['m_k_norm_g'], 'm_rec_conv_w': out['m_rec_conv_w'], 'm_rec_conv_b': out['m_rec_conv_b'], 'm_w_rg': out['m_w_rg'], 'm_b_rg': out['m_b_rg'], 'm_w_ig': out['m_w_ig'], 'm_b_ig': out['m_b_ig'], 'm_lru_lambda': out['m_lru_lambda'], 'm_g_attn_out': out['m_g_attn_out'], 'm_g_rec_out': out['m_g_rec_out'], 'm_w_out': out['m_w_out'], 'm_g_ffn': out['m_g_ffn'], 'm_w_up': out['m_w_up'], 'm_ffn_conv_w': out['m_ffn_conv_w'], 'm_ffn_conv_b': out['m_ffn_conv_b'], 'm_w_down': out['m_w_down'], 'v_g_mix': out['v_g_mix'], 'v_w_in': out['v_w_in'], 'v_q_norm_g': out['v_q_norm_g'], 'v_k_norm_g': out['v_k_norm_g'], 'v_rec_conv_w': out['v_rec_conv_w'], 'v_rec_conv_b': out['v_rec_conv_b'], 'v_w_rg': out['v_w_rg'], 'v_b_rg': out['v_b_rg'], 'v_w_ig': out['v_w_ig'], 'v_b_ig': out['v_b_ig'], 'v_lru_lambda': out['v_lru_lambda'], 'v_g_attn_out': out['v_g_attn_out'], 'v_g_rec_out': out['v_g_rec_out'], 'v_w_out': out['v_w_out'], 'v_g_ffn': out['v_g_ffn'], 'v_w_up': out['v_w_up'], 'v_ffn_conv_w': out['v_ffn_conv_w'], 'v_ffn_conv_b': out['v_ffn_conv_b'], 'v_w_down': out['v_w_down']}


def _loss(weights, diff, rest, loss_target):
    with _jax.named_scope("forward"):
        args = {**rest, TWIN_DIFF_INPUT: diff, **{k: w.astype(_WEIGHT_DTYPES[k]) for k, w in weights.items()}}
        y = _forward(args)
    with _jax.named_scope("loss_head"):
        err = _jnp.square(y.astype(_jnp.float32) - loss_target)
        return 0.5 * _jnp.sum(_jnp.mean(err, axis=-1)) if err.ndim else 0.5 * err


def _adamw(w, g, m, v):
    m = ADAM_B1 * m + (1.0 - ADAM_B1) * g
    v = ADAM_B2 * v + (1.0 - ADAM_B2) * _jnp.square(g)
    m_hat = m / (1.0 - ADAM_B1 ** ADAM_STEP)
    v_hat = v / (1.0 - ADAM_B2 ** ADAM_STEP)
    delta = -ADAM_LR * (m_hat / (_jnp.sqrt(v_hat) + ADAM_EPS) + ADAM_WD * w)
    return delta, m, v


def reference(x, positions, g_mix, w_in, q_norm_g, k_norm_g, rec_conv_w, rec_conv_b, w_rg, b_rg, w_ig, b_ig, lru_lambda, g_attn_out, g_rec_out, w_out, g_ffn, w_up, ffn_conv_w, ffn_conv_b, w_down, loss_target, m_g_mix, m_w_in, m_q_norm_g, m_k_norm_g, m_rec_conv_w, m_rec_conv_b, m_w_rg, m_b_rg, m_w_ig, m_b_ig, m_lru_lambda, m_g_attn_out, m_g_rec_out, m_w_out, m_g_ffn, m_w_up, m_ffn_conv_w, m_ffn_conv_b, m_w_down, v_g_mix, v_w_in, v_q_norm_g, v_k_norm_g, v_rec_conv_w, v_rec_conv_b, v_w_rg, v_b_rg, v_w_ig, v_b_ig, v_lru_lambda, v_g_attn_out, v_g_rec_out, v_w_out, v_g_ffn, v_w_up, v_ffn_conv_w, v_ffn_conv_b, v_w_down):
    given = dict(x=x, positions=positions, g_mix=g_mix, w_in=w_in, q_norm_g=q_norm_g, k_norm_g=k_norm_g, rec_conv_w=rec_conv_w, rec_conv_b=rec_conv_b, w_rg=w_rg, b_rg=b_rg, w_ig=w_ig, b_ig=b_ig, lru_lambda=lru_lambda, g_attn_out=g_attn_out, g_rec_out=g_rec_out, w_out=w_out, g_ffn=g_ffn, w_up=w_up, ffn_conv_w=ffn_conv_w, ffn_conv_b=ffn_conv_b, w_down=w_down, loss_target=loss_target, m_g_mix=m_g_mix, m_w_in=m_w_in, m_q_norm_g=m_q_norm_g, m_k_norm_g=m_k_norm_g, m_rec_conv_w=m_rec_conv_w, m_rec_conv_b=m_rec_conv_b, m_w_rg=m_w_rg, m_b_rg=m_b_rg, m_w_ig=m_w_ig, m_b_ig=m_b_ig, m_lru_lambda=m_lru_lambda, m_g_attn_out=m_g_attn_out, m_g_rec_out=m_g_rec_out, m_w_out=m_w_out, m_g_ffn=m_g_ffn, m_w_up=m_w_up, m_ffn_conv_w=m_ffn_conv_w, m_ffn_conv_b=m_ffn_conv_b, m_w_down=m_w_down, v_g_mix=v_g_mix, v_w_in=v_w_in, v_q_norm_g=v_q_norm_g, v_k_norm_g=v_k_norm_g, v_rec_conv_w=v_rec_conv_w, v_rec_conv_b=v_rec_conv_b, v_w_rg=v_w_rg, v_b_rg=v_b_rg, v_w_ig=v_w_ig, v_b_ig=v_b_ig, v_lru_lambda=v_lru_lambda, v_g_attn_out=v_g_attn_out, v_g_rec_out=v_g_rec_out, v_w_out=v_w_out, v_g_ffn=v_g_ffn, v_w_up=v_w_up, v_ffn_conv_w=v_ffn_conv_w, v_ffn_conv_b=v_ffn_conv_b, v_w_down=v_w_down)
    weights = {n: given[n] for n in TWIN_WEIGHTS}
    shared = {n: given[n] for n in SHARED_INPUTS}
    per_example = {n: given[n] for n in ['x', 'positions']}
    grad_fn = _jax.value_and_grad(_loss, argnums=(0, 1))

    def one_microbatch(ex, loss_target):
        ex = dict(ex)
        diff = ex.pop(TWIN_DIFF_INPUT)
        return grad_fn(weights, diff, {**shared, **ex}, loss_target)

    if N_MICROBATCH == 1:
        loss, (grad_w, grad_x) = one_microbatch(per_example, given["loss_target"])
    else:
        def body(carry, xs):
            loss_sum, grad_sum = carry
            l_k, (gw_k, gx_k) = one_microbatch(xs[0], xs[1])
            with _jax.named_scope("update"):
                return (loss_sum + l_k, _jax.tree.map(_jnp.add, grad_sum, gw_k)), gx_k

        init = (_jnp.zeros((), _jnp.float32), _jax.tree.map(_jnp.zeros_like, weights))
        (loss, grad_w), grad_x = _jax.lax.scan(body, init, (per_example, given["loss_target"]))
    with _jax.named_scope("update"):
        delta_w, new_m, new_v = {}, {}, {}
        for n in TWIN_WEIGHTS:
            delta_w[n], new_m[n], new_v[n] = _adamw(weights[n], grad_w[n], given["m_" + n], given["v_" + n])
    return (loss, grad_x, *[grad_w[n] for n in TWIN_WEIGHTS], *[delta_w[n] for n in TWIN_WEIGHTS],
            *[new_m[n] for n in TWIN_WEIGHTS], *[new_v[n] for n in TWIN_WEIGHTS])
```

```python
import math

import jax
import jax.numpy as jnp
import numpy as np
from jax import lax
from jax.experimental import pallas as pl
from jax.experimental.pallas import tpu as pltpu

F32 = jnp.float32
BF16 = jnp.bfloat16

T = 4096
D = 1024
HD = 64
AW = 512
RW = 512
INW = 2560
DFF = 3072
NCHIP = 4
EPS = 1e-6
NEG = -1e30
LRU_C = 8.0
ROPE_THETA = 10000.0
BLK = 128
DILATIONS = (1, 4, 16)
ADAM_LR, ADAM_B1, ADAM_B2, ADAM_EPS, ADAM_WD, ADAM_STEP = 0.001, 0.9, 0.999, 1e-08, 0.01, 10
VMEM_LIMIT = 56 * 1024 * 1024
MESH = pl.DeviceIdType.MESH

NN = (((1,), (0,)), ((), ()))
NT = (((1,), (1,)), ((), ()))
TN = (((0,), (0,)), ((), ()))


def _cp(*sem):
    return pltpu.CompilerParams(dimension_semantics=sem, vmem_limit_bytes=VMEM_LIMIT)


def _bs(shape, fn):
    return pl.BlockSpec(shape, fn)


def _dot(a, b, dims=NN):
    return lax.dot_general(a, b, dims, preferred_element_type=F32)


_GC = math.sqrt(2.0 / math.pi)


def _gelu(x):
    return x * (0.5 * (1.0 + jnp.tanh(_GC * (x + 0.044715 * (x * x * x)))))


def _gelu_and_grad(x):
    x2 = x * x
    th = jnp.tanh(_GC * (x + 0.044715 * (x * x2)))
    cdf = 0.5 * (1.0 + th)
    dg = cdf + 0.5 * x * (1.0 - th * th) * (_GC * (1.0 + 3.0 * 0.044715 * x2))
    return x * cdf, dg


def _softplus(x):
    e = jnp.exp(-jnp.abs(x))
    u = 1.0 + e
    l1p = jnp.where(u == 1.0, e, jnp.log(u) * (e / (u - 1.0)))
    return jnp.maximum(x, 0.0) + l1p


def _segsum(z, e_bf16):
    hi = z.astype(BF16)
    lo = (z - hi.astype(F32)).astype(BF16)
    return _dot(hi, e_bf16) + _dot(lo, e_bf16)


def _mm(name, a, b, mode, tm, tn, out_dtype=F32, res=None, stack=0):
    if mode == "nn":
        (m, k), n = a.shape, (b.shape[1] if not stack else stack * b.shape[2])
        a_spec = _bs((tm, k), lambda j, i: (i, 0))
        if stack:
            per = b.shape[2] // tn
            b_spec = _bs((None, k, tn), lambda j, i: (j // per, 0, j % per))
        else:
            b_spec = _bs((k, tn), lambda j, i: (0, j))
    elif mode == "nt":
        (m, k), n = a.shape, (b.shape[0] if not stack else b.shape[1])
        a_spec = _bs((tm, k), lambda j, i: (i, 0))
        b_spec = _bs((stack, tn, k // stack), lambda j, i: (0, j, 0)) if stack else _bs((tn, k), lambda j, i: (j, 0))
    else:
        (k, m), n = a.shape, b.shape[1]
        a_spec, b_spec = _bs((k, tm), lambda j, i: (0, i)), _bs((k, tn), lambda j, i: (0, j))
    assert m % tm == 0 and n % tn == 0
    o_spec = _bs((tm, tn), lambda j, i: (i, j))
    o_shape = (m, n)
    if mode == "tn" and stack:
        per = n // stack // tn
        o_spec = _bs((None, tm, tn), lambda j, i: (j // per, i, j % per))
        o_shape = (stack, m, n // stack)
    dims = {"nn": NN, "nt": NT, "tn": TN}[mode]

    def product(a_ref, b_ref):
        if mode == "nt" and stack:
            cs = k // stack
            acc = _dot(a_ref[:, 0:cs], b_ref[0], NT)
            for s in range(1, stack):
                acc = acc + _dot(a_ref[:, s * cs:(s + 1) * cs], b_ref[s], NT)
            return acc
        return _dot(a_ref[...], b_ref[...], dims)

    if res is None:
        def body(a_ref, b_ref, o_ref):
            o_ref[...] = product(a_ref, b_ref).astype(out_dtype)
        ins, specs = (a, b), [a_spec, b_spec]
    else:
        def body(a_ref, b_ref, r_ref, o_ref):
            o_ref[...] = (r_ref[...] + product(a_ref, b_ref)).astype(out_dtype)
        ins, specs = (a, b, res), [a_spec, b_spec, o_spec]
    return pl.pallas_call(
        body, name=name, grid=(n // tn, m // tm), in_specs=specs, out_specs=o_spec,
        out_shape=jax.ShapeDtypeStruct(o_shape, out_dtype), compiler_params=_cp("parallel", "parallel"),
    )(*ins)


def _rms_fwd(name, x, g):
    tr = 512

    def body(x_ref, g_ref, o_ref):
        xv = x_ref[...]
        r = lax.rsqrt(jnp.mean(xv * xv, axis=-1, keepdims=True) + EPS)
        o_ref[...] = ((xv * r) * g_ref[...]).astype(BF16)

    return pl.pallas_call(
        body, name=name, grid=(T // tr,), in_specs=[_bs((tr, D), lambda i: (i, 0)), _bs((1, D), lambda i: (0, 0))],
        out_specs=_bs((tr, D), lambda i: (i, 0)), out_shape=jax.ShapeDtypeStruct((T, D), BF16),
        compiler_params=_cp("parallel"),
    )(x, g)


def _rms_bwd(name, x, g, dy, dres, want_bf16):
    tr = 256

    def body(x_ref, g_ref, dy_ref, dr_ref, dx_ref, *rest):
        dg_ref = rest[-1]
        xv, dyv = x_ref[...], dy_ref[...]
        r = lax.rsqrt(jnp.mean(xv * xv, axis=-1, keepdims=True) + EPS)
        gdy = g_ref[...] * dyv
        dx = r * gdy - xv * ((r * r * r) * jnp.mean(xv * gdy, axis=-1, keepdims=True)) + dr_ref[...]
        dx_ref[...] = dx
        if want_bf16:
            rest[0][...] = dx.astype(BF16)

        @pl.when(pl.program_id(0) == 0)
        def _():
            dg_ref[...] = jnp.zeros_like(dg_ref)

        dg_ref[...] += jnp.sum(dyv * (xv * r), axis=0, keepdims=True)

    row = _bs((tr, D), lambda i: (i, 0))
    vec = _bs((1, D), lambda i: (0, 0))
    outs = [jax.ShapeDtypeStruct((T, D), F32)] + ([jax.ShapeDtypeStruct((T, D), BF16)] if want_bf16 else [])
    return pl.pallas_call(
        body, name=name, grid=(T // tr,), in_specs=[row, vec, row, row],
        out_specs=[row] * len(outs) + [vec], out_shape=outs + [jax.ShapeDtypeStruct((1, D), F32)],
        compiler_params=_cp("arbitrary"),
    )(x, g, dy, dres)


def _head_ones():
    idx = np.arange(AW) // HD
    return jnp.asarray((idx[:, None] == idx[None, :]).astype(np.float32), dtype=BF16)


def _freq_row():
    half = HD // 2
    inv = ROPE_THETA ** (-(np.arange(half, dtype=np.float64)) / half)
    return jnp.asarray(np.tile(inv, 4)[None, :], dtype=F32)


def _rot_tables(pos_ref, f_ref):
    ang = pos_ref[...].astype(F32) * f_ref[...]
    c = jnp.tile(jnp.cos(ang), (1, 4))
    s = jnp.tile(jnp.sin(ang), (1, 4))
    lane = lax.broadcasted_iota(jnp.int32, (1, AW), 1)
    first = (lane & 32) == 0
    return c, jnp.where(first, -s, s), first


def _swap_halves(y, first):
    return jnp.where(first, pltpu.roll(y, AW - 32, 1), pltpu.roll(y, 32, 1))


def _qk_prep(proj, pos_col, qg, kg):
    tr = 512

    def body(q_ref, k_ref, v_ref, pos_ref, f_ref, qg_ref, kg_ref, e_ref, qo_ref, ko_ref, vo_ref):
        c, s_signed, first = _rot_tables(pos_ref, f_ref)
        e = e_ref[...]

        def norm_rot(xv, g, scale):
            r = lax.rsqrt(_segsum(xv * xv, e) * (1.0 / HD) + EPS)
            y = (xv * r) * g
            return (y * c + _swap_halves(y, first) * s_signed) * scale

        qo_ref[...] = norm_rot(q_ref[...], qg_ref[...], HD ** -0.5).astype(BF16)
        ko_ref[...] = norm_rot(k_ref[...], kg_ref[...], 1.0).astype(BF16)
        vo_ref[...] = v_ref[...].astype(BF16)

    col = lambda j: _bs((tr, AW), lambda i, j=j: (i, j))
    vec = _bs((1, AW), lambda i: (0, 0))
    out = jax.ShapeDtypeStruct((T, AW), BF16)
    return pl.pallas_call(
        body, name="qk_prep", grid=(T // tr,),
        in_specs=[col(0), col(1), col(2), _bs((tr, 1), lambda i: (i, 0)), _bs((1, 128), lambda i: (0, 0)), vec, vec,
                  _bs((AW, AW), lambda i: (0, 0))],
        out_specs=[col(0)] * 3, out_shape=[out] * 3, compiler_params=_cp("parallel"),
    )(proj, proj, proj, pos_col, _freq_row(), qg, kg, _head_ones())


def _qk_bwd(proj, pos_col, qg, kg, dqs, dks, dvs):
    tr = 256

    def body(q_ref, k_ref, pos_ref, f_ref, qg_ref, kg_ref, e_ref, a0, a1, a2, b0, b1, b2, c0, c1, c2,
             o_ref, dqg_ref, dkg_ref):
        i, j = pl.program_id(0), pl.program_id(1)

        @pl.when((i == 0) & (j == 0))
        def _():
            dqg_ref[...] = jnp.zeros_like(dqg_ref)
            dkg_ref[...] = jnp.zeros_like(dkg_ref)

        def norm_rot_bwd(x_ref, g_ref, dg_ref, d0, d1, d2, scale):
            c, s_signed, first = _rot_tables(pos_ref, f_ref)
            e = e_ref[...]
            dout = ((d0[...] + d1[...]) + d2[...]) * scale
            dy = dout * c + _swap_halves(dout * s_signed, first)
            xv, g = x_ref[...], g_ref[...]
            r = lax.rsqrt(_segsum(xv * xv, e) * (1.0 / HD) + EPS)
            gdy = g * dy
            dx = r * gdy - xv * ((r * r * r) * (_segsum(xv * gdy, e) * (1.0 / HD)))
            o_ref[...] = dx.astype(BF16)
            dg_ref[...] += jnp.sum(dy * (xv * r), axis=0, keepdims=True)

        @pl.when(j == 0)
        def _():
            norm_rot_bwd(q_ref, qg_ref, dqg_ref, a0, a1, a2, HD ** -0.5)

        @pl.when(j == 1)
        def _():
            norm_rot_bwd(k_ref, kg_ref, dkg_ref, b0, b1, b2, 1.0)

        @pl.when(j == 2)
        def _():
            o_ref[...] = ((c0[...] + c1[...]) + c2[...]).astype(BF16)

    col = lambda jj: _bs((tr, AW), lambda i, j, jj=jj: (i, jj))
    vec = _bs((1, AW), lambda i, j: (0, 0))
    piece = _bs((tr, AW), lambda i, j: (i, 0))
    return pl.pallas_call(
        body, name="qk_bwd", grid=(T // tr, 3),
        in_specs=[col(0), col(1), _bs((tr, 1), lambda i, j: (i, 0)), _bs((1, 128), lambda i, j: (0, 0)), vec, vec,
                  _bs((AW, AW), lambda i, j: (0, 0))] + [piece] * 9,
        out_specs=[_bs((tr, AW), lambda i, j: (i, j)), vec, vec],
        out_shape=[jax.ShapeDtypeStruct((T, INW), BF16), jax.ShapeDtypeStruct((1, AW), F32),
                   jax.ShapeDtypeStruct((1, AW), F32)],
        compiler_params=_cp("arbitrary", "arbitrary"),
    )(proj, proj, pos_col, _freq_row(), qg, kg, _head_ones(), *dqs, *dks, *dvs)


def _regroup(a, d):
    return a if d == 1 else a.reshape(T // d, d, AW).transpose(1, 0, 2).reshape(T, AW)


def _ungroup(a, d):
    return a if d == 1 else a.reshape(d, T // d, AW).transpose(1, 0, 2).reshape(T, AW)


def _band_masks():
    qi = lax.broadcasted_iota(jnp.int32, (BLK, 2 * BLK), 0)
    kj = lax.broadcasted_iota(jnp.int32, (BLK, 2 * BLK), 1)
    rel = qi - kj + BLK
    wide = (rel >= 0) & (rel <= BLK)
    qi1 = lax.broadcasted_iota(jnp.int32, (BLK, BLK), 0)
    kj1 = lax.broadcasted_iota(jnp.int32, (BLK, BLK), 1)
    return kj1 <= qi1, wide


def _attn_fwd(name, q, k, v, d):
    ln = T // d
    nb = ln // BLK

    def body(q_ref, k_ref, v_ref, o_ref, l_ref):
        first_mask, wide_mask = _band_masks()
        lane = lax.broadcasted_iota(jnp.int32, (1, 128), 1)
        h0 = lane < HD

        def head(qm, kk, vv, mask):
            s = jnp.where(mask, _dot(qm, kk, NT), NEG)
            m = jnp.max(s, axis=1, keepdims=True)
            p = jnp.exp(s - m)
            l = jnp.sum(p, axis=1, keepdims=True)
            return _dot(p.astype(BF16), vv) / l, m + jnp.log(l)

        def block(r0, k0, kn, mask):
            qv = q_ref[pl.ds(r0, BLK), :]
            kk = k_ref[pl.ds(k0, kn), :]
            vv = v_ref[pl.ds(k0, kn), :]
            zero = jnp.zeros_like(qv)
            o_a, l_a = head(jnp.where(h0, qv, zero), kk, vv, mask)
            o_b, l_b = head(jnp.where(h0, zero, qv), kk, vv, mask)
            o_ref[pl.ds(r0, BLK), :] = jnp.where(h0, o_a, o_b)
            l_ref[pl.ds(r0, BLK), :] = jnp.where(h0, l_a, l_b)

        block(0, 0, BLK, first_mask)

        def step(n, carry):
            r0 = pl.multiple_of(n * BLK, BLK)
            block(r0, pl.multiple_of(r0 - BLK, BLK), 2 * BLK, wide_mask)
            return carry

        lax.fori_loop(1, nb, step, 0)

    spec = _bs((ln, 128), lambda c, p: (c, p))
    out = jax.ShapeDtypeStruct((T, AW), F32)
    return pl.pallas_call(
        body, name=name, grid=(d, AW // 128), in_specs=[spec] * 3, out_specs=[spec] * 2, out_shape=[out] * 2,
        compiler_params=_cp("parallel", "parallel"),
    )(q, k, v)


def _attn_bwd(name, q, k, v, do, lse, delta, d):
    ln = T // d
    nb = ln // BLK

    def body(q_ref, k_ref, v_ref, do_ref, l_ref, dl_ref, dq_ref, dk_ref, dv_ref):
        first_mask, wide_mask = _band_masks()
        lane = lax.broadcasted_iota(jnp.int32, (1, 128), 1)
        h0 = lane < HD
        dk_ref[...] = jnp.zeros_like(dk_ref)
        dv_ref[...] = jnp.zeros_like(dv_ref)

        def head(qm, dom, kk, vv, lse_c, dl_c, mask):
            s = jnp.where(mask, _dot(qm, kk, NT), NEG)
            p = jnp.exp(s - lse_c)
            ds = p * (_dot(dom, vv, NT) - dl_c)
            pb, dsb = p.astype(BF16), ds.astype(BF16)
            return _dot(dsb, kk), _dot(dsb, qm, TN), _dot(pb, dom, TN)

        def block(r0, k0, kn, mask):
            qv = q_ref[pl.ds(r0, BLK), :]
            dov = do_ref[pl.ds(r0, BLK), :]
            lv = l_ref[pl.ds(r0, BLK), :]
            dlv = dl_ref[pl.ds(r0, BLK), :]
            kk = k_ref[pl.ds(k0, kn), :]
            vv = v_ref[pl.ds(k0, kn), :]
            zero = jnp.zeros_like(qv)
            dq_a, dk_a, dv_a = head(jnp.where(h0, qv, zero), jnp.where(h0, dov, zero), kk, vv,
                                    lv[:, 0:1], dlv[:, 0:1], mask)
            dq_b, dk_b, dv_b = head(jnp.where(h0, zero, qv), jnp.where(h0, zero, dov), kk, vv,
                                    lv[:, HD:HD + 1], dlv[:, HD:HD + 1], mask)
            dq_ref[pl.ds(r0, BLK), :] = jnp.where(h0, dq_a, dq_b)
            dk_ref[pl.ds(k0, kn), :] += dk_a + dk_b
            dv_ref[pl.ds(k0, kn), :] += dv_a + dv_b

        block(0, 0, BLK, first_mask)

        def step(n, carry):
            r0 = pl.multiple_of(n * BLK, BLK)
            block(r0, pl.multiple_of(r0 - BLK, BLK), 2 * BLK, wide_mask)
            return carry

        lax.fori_loop(1, nb, step, 0)

    spec = _bs((ln, 128), lambda c, p: (c, p))
    out = jax.ShapeDtypeStruct((T, AW), F32)
    return pl.pallas_call(
        body, name=name, grid=(d, AW // 128), in_specs=[spec] * 6, out_specs=[spec] * 3, out_shape=[out] * 3,
        compiler_params=_cp("parallel", "parallel"),
    )(q, k, v, do, lse, delta)


def _attn_merge(os_, ls_, g_attn):
    tr = 512

    def body(o0, o1, o2, l0, l1, l2, g_ref, a_ref, lse_ref, mix_ref):
        la, lb, lc = l0[...], l1[...], l2[...]
        m = jnp.maximum(jnp.maximum(la, lb), lc)
        ea, eb, ec = jnp.exp(la - m), jnp.exp(lb - m), jnp.exp(lc - m)
        z = (ea + eb) + ec
        attn = ((ea * o0[...] + eb * o1[...]) + ec * o2[...]) / z
        a_ref[...] = attn
        lse_ref[...] = m + jnp.log(z)
        r = lax.rsqrt(jnp.mean(attn * attn, axis=-1, keepdims=True) + EPS)
        mix_ref[...] = ((attn * r) * g_ref[...]).astype(BF16)

    row = _bs((tr, AW), lambda i: (i, 0))
    f = jax.ShapeDtypeStruct((T, AW), F32)
    return pl.pallas_call(
        body, name="attn_merge", grid=(T // tr,), in_specs=[row] * 6 + [_bs((1, AW), lambda i: (0, 0))],
        out_specs=[row, row, row], out_shape=[f, f, jax.ShapeDtypeStruct((T, D), BF16)],
        compiler_params=_cp("parallel"),
    )(*os_, *ls_, g_attn)


def _attn_out_bwd(attn, dmix, g_attn):
    tr = 256

    def body(a_ref, d_ref, g_ref, e_ref, do_ref, dl_ref, dg_ref):
        av, dyv = a_ref[...], d_ref[...]
        r = lax.rsqrt(jnp.mean(av * av, axis=-1, keepdims=True) + EPS)
        gdy = g_ref[...] * dyv
        da = r * gdy - av * ((r * r * r) * jnp.mean(av * gdy, axis=-1, keepdims=True))
        do_ref[...] = da.astype(BF16)
        dl_ref[...] = _segsum(da * av, e_ref[...])

        @pl.when(pl.program_id(0) == 0)
        def _():
            dg_ref[...] = jnp.zeros_like(dg_ref)

        dg_ref[...] += jnp.sum(dyv * (av * r), axis=0, keepdims=True)

    row = _bs((tr, AW), lambda i: (i, 0))
    vec = _bs((1, AW), lambda i: (0, 0))
    return pl.pallas_call(
        body, name="attn_out_bwd", grid=(T // tr,), in_specs=[row, row, vec, _bs((AW, AW), lambda i: (0, 0))],
        out_specs=[row, row, vec],
        out_shape=[jax.ShapeDtypeStruct((T, AW), BF16), jax.ShapeDtypeStruct((T, AW), F32),
                   jax.ShapeDtypeStruct((1, AW), F32)],
        compiler_params=_cp("arbitrary"),
    )(attn, dmix, g_attn, _head_ones())


TRR = 256


def _scan_fwd(a, u):
    n = a.shape[0]
    row = lax.broadcasted_iota(jnp.int32, (n, 1), 0)
    s = 1
    while s < n:
        keep = row >= s
        u = jnp.where(keep, a * pltpu.roll(u, s, 0) + u, u)
        a = jnp.where(keep, a * pltpu.roll(a, s, 0), a)
        s *= 2
    return a, u


def _scan_bwd(c, w):
    n = c.shape[0]
    row = lax.broadcasted_iota(jnp.int32, (n, 1), 0)
    s = 1
    while s < n:
        keep = row < n - s
        w = jnp.where(keep, c * pltpu.roll(w, n - s, 0) + w, w)
        c = jnp.where(keep, c * pltpu.roll(c, n - s, 0), c)
        s *= 2
    return w


def _gates(xc, wrg, wig, brg, big, sp):
    xcb = xc.astype(BF16)
    r = jax.nn.sigmoid(_dot(xcb, wrg) + brg)
    ig = jax.nn.sigmoid(_dot(xcb, wig) + big)
    la = (-LRU_C * r) * sp
    a = jnp.exp(la)
    mult = jnp.sqrt(-jnp.tanh(la) * (a * a + 1.0))
    return r, ig, a, mult


def _conv4(ext_ref, xr, cw_ref, cb_ref, n):
    y = cb_ref[...] + ext_ref[pl.ds(5, n), :] * cw_ref[0:1, :]
    y = y + ext_ref[pl.ds(6, n), :] * cw_ref[1:2, :]
    y = y + ext_ref[pl.ds(7, n), :] * cw_ref[2:3, :]
    return y + xr * cw_ref[3:4, :]


def _rec_fwd(proj, mix, cw, cb, wrg, wig, brg, big, lam, g_rec):
    n = TRR

    def body(xr_ref, gr_ref, cw_ref, cb_ref, wrg_ref, wig_ref, brg_ref, big_ref, lam_ref, g_ref, mix_in,
             mix_ref, h_ref, ext, hcar):
        del mix_in

        @pl.when(pl.program_id(0) == 0)
        def _():
            ext[0:8, :] = jnp.zeros((8, RW), F32)
            hcar[...] = jnp.zeros_like(hcar)

        xr = xr_ref[...]
        ext[8:, :] = xr
        xc = _conv4(ext, xr, cw_ref, cb_ref, n)
        ext[0:8, :] = xr[n - 8:, :]
        sp = _softplus(-lam_ref[...])
        _, ig, a, mult = _gates(xc, wrg_ref[...], wig_ref[...], brg_ref[...], big_ref[...], sp)
        a_s, u_s = _scan_fwd(a, mult * (ig * xc))
        h = u_s + a_s * hcar[7:8, :]
        h_ref[...] = h
        hcar[...] = h[n - 8:, :]
        pre = h * _gelu(gr_ref[...])
        r = lax.rsqrt(jnp.mean(pre * pre, axis=-1, keepdims=True) + EPS)
        mix_ref[...] = ((pre * r) * g_ref[...]).astype(BF16)

    vec = _bs((1, RW), lambda i: (0, 0))
    mat = _bs((RW, RW), lambda i: (0, 0))
    return pl.pallas_call(
        body, name="rec_fwd", grid=(T // n,),
        in_specs=[_bs((n, RW), lambda i: (i, 3)), _bs((n, RW), lambda i: (i, 4)), _bs((8, RW), lambda i: (0, 0)), vec,
                  mat, mat, vec, vec, vec, vec, pl.BlockSpec(memory_space=pl.ANY)],
        out_specs=[_bs((n, RW), lambda i: (i, 1)), _bs((n, RW), lambda i: (i, 0))],
        out_shape=[jax.ShapeDtypeStruct((T, D), BF16), jax.ShapeDtypeStruct((T, RW), F32)],
        scratch_shapes=[pltpu.VMEM((n + 8, RW), F32), pltpu.VMEM((8, RW), F32)],
        input_output_aliases={10: 0}, compiler_params=_cp("arbitrary"),
    )(proj, proj, cw, cb, wrg, wig, brg, big, lam, g_rec, mix)


def _rec_bwd(proj, h, dmix, dproj, cw, cb, wrg, wig, brg, big, lam, g_rec):
    n = TRR
    nt = T // n
    hb = n // 8

    def body(xr_ref, xh_ref, gr_ref, h_ref, hh_ref, dm_ref, cw_ref, cb_ref, wrg_ref, wig_ref, brg_ref, big_ref,
             lam_ref, g_ref, dp_in, dp_ref, xc_ref, dr_ref, di_ref, dcw_ref, dcb_ref, dbr_ref, dbi_ref, dsp_ref,
             dg_ref, ext, exth, extd, adh, dgr_s):
        del dp_in
        i, j = pl.program_id(0), pl.program_id(1)
        first_tile = i == nt - 1
        last_tile = i == 0

        @pl.when(j == 0)
        def _():
            @pl.when(last_tile)
            def _():
                for ref in (dcw_ref, dcb_ref, dbr_ref, dbi_ref, dsp_ref, dg_ref):
                    ref[...] = jnp.zeros_like(ref)
                extd[n:, :] = jnp.zeros((8, RW), F32)
                adh[...] = jnp.zeros_like(adh)

            row = lax.broadcasted_iota(jnp.int32, (n, 1), 0)
            xr = xr_ref[...]
            ext[0:8, :] = jnp.where(first_tile, 0.0, xh_ref[...])
            ext[8:, :] = xr
            xc = _conv4(ext, xr, cw_ref, cb_ref, n)
            sp = _softplus(-lam_ref[...])
            wrg, wig = wrg_ref[...], wig_ref[...]
            r, ig, a, mult = _gates(xc, wrg, wig, brg_ref[...], big_ref[...], sp)

            hv = h_ref[...]
            gl, dgl = _gelu_and_grad(gr_ref[...])
            pre = hv * gl
            dyv = dm_ref[...]
            rr = lax.rsqrt(jnp.mean(pre * pre, axis=-1, keepdims=True) + EPS)
            gdy = g_ref[...] * dyv
            dpre = rr * gdy - pre * ((rr * rr * rr) * jnp.mean(pre * gdy, axis=-1, keepdims=True))
            dg_ref[...] += jnp.sum(dyv * (pre * rr), axis=0, keepdims=True)
            dgr_s[...] = dpre * hv * dgl

            is_last_row = row == n - 1
            w = dpre * gl + jnp.where(is_last_row, adh[0:1, :], 0.0)
            c = jnp.where(is_last_row, 0.0, pltpu.roll(a, n - 1, 0))
            dh = _scan_bwd(c, w)
            adh[...] = (a * dh)[0:8, :]

            exth[0:8, :] = jnp.where(first_tile, 0.0, hh_ref[...])
            exth[8:, :] = hv
            da = dh * exth[pl.ds(7, n), :]
            ixc = ig * xc
            dmult = dh * ixc
            dla = da * a - dmult * ((a * a) / mult)
            dsp_ref[...] += jnp.sum(dla * (-LRU_C * r), axis=0, keepdims=True)
            dpr = (dla * (-LRU_C * sp)) * (r * (1.0 - r))
            dpi = (dh * (mult * xc)) * (ig * (1.0 - ig))
            dprb, dpib = dpr.astype(BF16), dpi.astype(BF16)
            dxc = dh * (mult * ig) + _dot(dprb, wrg, NT) + _dot(dpib, wig, NT)
            dbr_ref[...] += jnp.sum(dpr, axis=0, keepdims=True)
            dbi_ref[...] += jnp.sum(dpi, axis=0, keepdims=True)
            xc_ref[...] = xc.astype(BF16)
            dr_ref[...] = dprb
            di_ref[...] = dpib

            extd[0:n, :] = dxc
            dxr = dxc * cw_ref[3:4, :] + extd[pl.ds(1, n), :] * cw_ref[2:3, :]
            dxr = dxr + extd[pl.ds(2, n), :] * cw_ref[1:2, :] + extd[pl.ds(3, n), :] * cw_ref[0:1, :]
            extd[n:, :] = dxc[0:8, :]
            dcb_ref[...] += jnp.sum(dxc, axis=0, keepdims=True)
            for kk in range(4):
                dcw_ref[kk:kk + 1, :] += jnp.sum(dxc * ext[pl.ds(5 + kk, n), :], axis=0, keepdims=True)

            @pl.when(first_tile)
            def _():
                dsp_ref[...] = dsp_ref[...] * (-jax.nn.sigmoid(-lam_ref[...]))

            dp_ref[...] = dxr.astype(BF16)

        @pl.when(j == 1)
        def _():
            dp_ref[...] = dgr_s[...].astype(BF16)

    vec = _bs((1, RW), lambda i, j: (0, 0))
    mat = _bs((RW, RW), lambda i, j: (0, 0))
    tile = lambda cblk: _bs((n, RW), lambda i, j, cblk=cblk: (nt - 1 - i, cblk))
    halo = lambda cblk: _bs((8, RW), lambda i, j, cblk=cblk: (jnp.maximum((nt - 1 - i) * hb - 1, 0), cblk))
    bt = jax.ShapeDtypeStruct((T, RW), BF16)
    v = jax.ShapeDtypeStruct((1, RW), F32)
    return pl.pallas_call(
        body, name="rec_bwd", grid=(nt, 2),
        in_specs=[tile(3), halo(3), tile(4), tile(0), halo(0), tile(1), _bs((8, RW), lambda i, j: (0, 0)), vec,
                  mat, mat, vec, vec, vec, vec, pl.BlockSpec(memory_space=pl.ANY)],
        out_specs=[_bs((n, RW), lambda i, j: (nt - 1 - i, 3 + j)), tile(0), tile(0), tile(0),
                   _bs((8, RW), lambda i, j: (0, 0)), vec, vec, vec, vec, vec],
        out_shape=[jax.ShapeDtypeStruct((T, INW), BF16), bt, bt, bt, jax.ShapeDtypeStruct((8, RW), F32), v, v, v, v, v],
        scratch_shapes=[pltpu.VMEM((n + 8, RW), F32), pltpu.VMEM((n + 8, RW), F32), pltpu.VMEM((n + 8, RW), F32),
                        pltpu.VMEM((8, RW), F32), pltpu.VMEM((n, RW), F32)],
        input_output_aliases={14: 0}, compiler_params=_cp("arbitrary", "arbitrary"),
    )(proj, proj, proj, h, h, dmix, cw, cb, wrg, wig, brg, big, lam, g_rec, dproj)


FC = 1536
TRF = 256


def _conv3(ext_ref, w_ref, b_ref, n):
    y = b_ref[...] + ext_ref[pl.ds(6, n), :] * w_ref[0:1, :]
    y = y + ext_ref[pl.ds(7, n), :] * w_ref[1:2, :]
    return y + ext_ref[pl.ds(8, n), :] * w_ref[2:3, :]


def _ffn_act(up_pre, cw, cb):
    n = TRF
    hb = n // 8

    def body(g_ref, gh_ref, u_ref, uh_ref, wg_ref, wu_ref, bg_ref, bu_ref, o_ref, extg, extu):
        first = pl.program_id(1) == 0
        extg[0:8, :] = jnp.where(first, 0.0, gh_ref[...])
        extg[8:, :] = g_ref[...]
        extu[0:8, :] = jnp.where(first, 0.0, uh_ref[...])
        extu[8:, :] = u_ref[...]
        o_ref[...] = (_gelu(_conv3(extg, wg_ref, bg_ref, n)) * _conv3(extu, wu_ref, bu_ref, n)).astype(BF16)

    main = lambda o: _bs((n, FC), lambda j, i, o=o: (i, 2 * j + o))
    halo = lambda o: _bs((8, FC), lambda j, i, o=o: (jnp.maximum(i * hb - 1, 0), 2 * j + o))
    wsp = lambda o: _bs((None, 8, FC), lambda j, i, o=o: (2 * j + o, 0, 0))
    bsp = lambda o: _bs((1, FC), lambda j, i, o=o: (0, 2 * j + o))
    return pl.pallas_call(
        body, name="ffn_act", grid=(2, T // n),
        in_specs=[main(0), halo(0), main(1), halo(1), wsp(0), wsp(1), bsp(0), bsp(1)],
        out_specs=_bs((n, FC), lambda j, i: (i, j)), out_shape=jax.ShapeDtypeStruct((T, DFF), BF16),
        scratch_shapes=[pltpu.VMEM((n + 8, FC), F32)] * 2, compiler_params=_cp("parallel", "parallel"),
    )(up_pre, up_pre, up_pre, up_pre, cw, cw, cb, cb)


def _ffn_bwd(up_pre, dact, cw, cb):
    n = TRF
    hb = n // 8
    nt = T // n
    m = n + 8

    def body(g_ref, gp_ref, gn_ref, u_ref, up_ref, un_ref, d_ref, dn_ref, wg_ref, wu_ref, bg_ref, bu_ref,
             o_ref, dw_ref, db_ref, extg, extu, extd, dug_s, duu_s):
        i = pl.program_id(1)
        first, last = i == 0, i == nt - 1

        @pl.when(first)
        def _():
            dw_ref[...] = jnp.zeros_like(dw_ref)
            db_ref[...] = jnp.zeros_like(db_ref)

        extg[0:8, :] = jnp.where(first, 0.0, gp_ref[...])
        extg[8:n + 8, :] = g_ref[...]
        extg[n + 8:, :] = gn_ref[...]
        extu[0:8, :] = jnp.where(first, 0.0, up_ref[...])
        extu[8:n + 8, :] = u_ref[...]
        extu[n + 8:, :] = un_ref[...]
        extd[0:n, :] = d_ref[...]
        extd[n:, :] = jnp.where(last, 0.0, dn_ref[...])
        gl, dgl = _gelu_and_grad(_conv3(extg, wg_ref, bg_ref, m))
        uu = _conv3(extu, wu_ref, bu_ref, m)
        dv = extd[...]
        dug_s[...] = dv * uu * dgl
        duu_s[...] = dv * gl

        def conv_t(s_ref, w_ref):
            y = s_ref[pl.ds(0, n), :] * w_ref[2:3, :] + s_ref[pl.ds(1, n), :] * w_ref[1:2, :]
            return y + s_ref[pl.ds(2, n), :] * w_ref[0:1, :]

        o_ref[:, 0:FC] = conv_t(dug_s, wg_ref).astype(BF16)
        o_ref[:, FC:] = conv_t(duu_s, wu_ref).astype(BF16)
        dg0, du0 = dug_s[pl.ds(0, n), :], duu_s[pl.ds(0, n), :]
        db_ref[:, 0:FC] += jnp.sum(dg0, axis=0, keepdims=True)
        db_ref[:, FC:] += jnp.sum(du0, axis=0, keepdims=True)
        for kk in range(3):
            dw_ref[kk:kk + 1, 0:FC] += jnp.sum(dg0 * extg[pl.ds(6 + kk, n), :], axis=0, keepdims=True)
            dw_ref[kk:kk + 1, FC:] += jnp.sum(du0 * extu[pl.ds(6 + kk, n), :], axis=0, keepdims=True)

    main = lambda o: _bs((n, FC), lambda j, i, o=o: (i, 2 * j + o))
    prev = lambda o: _bs((8, FC), lambda j, i, o=o: (jnp.maximum(i * hb - 1, 0), 2 * j + o))
    nxt = lambda o: _bs((8, FC), lambda j, i, o=o: (jnp.minimum((i + 1) * hb, T // 8 - 1), 2 * j + o))
    wsp = lambda o: _bs((None, 8, FC), lambda j, i, o=o: (2 * j + o, 0, 0))
    bsp = lambda o: _bs((1, FC), lambda j, i, o=o: (0, 2 * j + o))
    return pl.pallas_call(
        body, name="ffn_bwd", grid=(2, nt),
        in_specs=[main(0), prev(0), nxt(0), main(1), prev(1), nxt(1), _bs((n, FC), lambda j, i: (i, j)),
                  _bs((8, FC), lambda j, i: (jnp.minimum((i + 1) * hb, T // 8 - 1), j)), wsp(0), wsp(1), bsp(0), bsp(1)],
        out_specs=[_bs((n, 2 * FC), lambda j, i: (i, j)), _bs((8, 2 * FC), lambda j, i: (0, j)),
                   _bs((1, 2 * FC), lambda j, i: (0, j))],
        out_shape=[jax.ShapeDtypeStruct((T, 2 * DFF), BF16), jax.ShapeDtypeStruct((8, 2 * DFF), F32),
                   jax.ShapeDtypeStruct((1, 2 * DFF), F32)],
        scratch_shapes=[pltpu.VMEM((n + 16, FC), F32), pltpu.VMEM((n + 16, FC), F32), pltpu.VMEM((m, FC), F32),
                        pltpu.VMEM((m, FC), F32), pltpu.VMEM((m, FC), F32)],
        compiler_params=_cp("parallel", "arbitrary"),
    )(up_pre, up_pre, up_pre, up_pre, up_pre, up_pre, dact, dact, cw, cw, cb, cb)


def _down_loss(act, w_down, x1, target):
    tm, tn = 512, 512

    def body(a_ref, b_ref, r_ref, t_ref, dy_ref, dyb_ref, l_ref):
        @pl.when((pl.program_id(0) == 0) & (pl.program_id(1) == 0))
        def _():
            l_ref[...] = jnp.zeros_like(l_ref)

        err = (r_ref[...] + _dot(a_ref[...], b_ref[...])) - t_ref[...]
        dy = err * (1.0 / D)
        dy_ref[...] = dy
        dyb_ref[...] = dy.astype(BF16)
        l_ref[...] += jnp.sum(0.5 * (err * err) * (1.0 / D))

    o_spec = _bs((tm, tn), lambda j, i: (i, j))
    return pl.pallas_call(
        body, name="down_loss", grid=(D // tn, T // tm),
        in_specs=[_bs((tm, DFF), lambda j, i: (i, 0)), _bs((DFF, tn), lambda j, i: (0, j)), o_spec, o_spec],
        out_specs=[o_spec, o_spec, _bs((8, 128), lambda j, i: (0, 0))],
        out_shape=[jax.ShapeDtypeStruct((T, D), F32), jax.ShapeDtypeStruct((T, D), BF16),
                   jax.ShapeDtypeStruct((8, 128), F32)],
        compiler_params=_cp("arbitrary", "arbitrary"),
    )(act, w_down, x1, target)


def _block_diag(w):
    eye = jnp.eye(8, dtype=w.dtype)
    return (w[:, :, None, :] * eye[:, None, :, None]).reshape(RW, RW).astype(BF16)


def _diag_blocks(m):
    return jnp.stack([m[HD * b:HD * (b + 1), HD * b:HD * (b + 1)] for b in range(8)])


def _local_step(x, pos_col, target, p):
    qg, kg = jnp.tile(p["q_norm_g"], (1, 8)), jnp.tile(p["k_norm_g"], (1, 8))
    wrg, wig = _block_diag(p["w_rg"]), _block_diag(p["w_ig"])
    brg, big = p["b_rg"].reshape(1, RW), p["b_ig"].reshape(1, RW)

    h1 = _rms_fwd("rms1", x, p["g_mix"])
    proj = _mm("mm_in", h1, p["w_in"], "nn", 512, 640, stack=NCHIP)
    q, k, v = _qk_prep(proj, pos_col, qg, kg)
    qs, ks, vs = ([_regroup(a, d) for d in DILATIONS] for a in (q, k, v))
    os_, ls_ = [], []
    for bi, d in enumerate(DILATIONS):
        o, l = _attn_fwd(f"attn_fwd{d}", qs[bi], ks[bi], vs[bi], d)
        os_.append(_ungroup(o, d))
        ls_.append(_ungroup(l, d))
    attn, lse, mix = _attn_merge(os_, ls_, p["g_attn_out"])
    mix, hseq = _rec_fwd(proj, mix, p["rec_conv_w"], p["rec_conv_b"], wrg, wig, brg, big, p["lru_lambda"], p["g_rec_out"])
    x1 = _mm("mm_out", mix, p["w_out"], "nn", 512, 512, res=x)
    h2 = _rms_fwd("rms2", x1, p["g_ffn"])
    up_pre = _mm("mm_up", h2, p["w_up"], "nn", 512, 768, stack=NCHIP)
    act = _ffn_act(up_pre, p["ffn_conv_w"], p["ffn_conv_b"])
    dy, dyb, loss_blk = _down_loss(act, p["w_down"], x1, target)

    g = {}
    g["w_down"] = _mm("wg_down", act, dyb, "tn", 512, 512)
    dact = _mm("dg_down", dyb, p["w_down"], "nt", 512, 512)
    dup, g["ffn_conv_w"], g["ffn_conv_b"] = _ffn_bwd(up_pre, dact, p["ffn_conv_w"], p["ffn_conv_b"])
    g["w_up"] = _mm("wg_up", h2, dup, "tn", 512, 768, stack=NCHIP)
    dh2 = _mm("dg_up", dup, p["w_up"], "nt", 512, 256, stack=NCHIP)
    dx1, dx1b, g["g_ffn"] = _rms_bwd("rms2_bwd", x1, p["g_ffn"], dh2, dy, True)
    g["w_out"] = _mm("wg_out", mix, dx1b, "tn", 512, 512)
    dmix = _mm("dg_out", dx1b, p["w_out"], "nt", 512, 512)
    do, delta, g["g_attn_out"] = _attn_out_bwd(attn, dmix, p["g_attn_out"])
    dqs, dks, dvs = [], [], []
    for bi, d in enumerate(DILATIONS):
        dq, dk, dv = _attn_bwd(f"attn_bwd{d}", qs[bi], ks[bi], vs[bi], _regroup(do, d), _regroup(lse, d),
                               _regroup(delta, d), d)
        dqs.append(_ungroup(dq, d))
        dks.append(_ungroup(dk, d))
        dvs.append(_ungroup(dv, d))
    dproj, dqg, dkg = _qk_bwd(proj, pos_col, qg, kg, dqs, dks, dvs)
    (dproj, xcb, dprb, dpib, g["rec_conv_w"], g["rec_conv_b"], dbr, dbi, dsp, g["g_rec_out"]) = _rec_bwd(
        proj, hseq, dmix, dproj, p["rec_conv_w"], p["rec_conv_b"], wrg, wig, brg, big, p["lru_lambda"], p["g_rec_out"])
    g["w_rg"] = _diag_blocks(_mm("wg_rg", xcb, dprb, "tn", 512, 512)).reshape(RW, HD)
    g["w_ig"] = _diag_blocks(_mm("wg_ig", xcb, dpib, "tn", 512, 512)).reshape(RW, HD)
    g["b_rg"], g["b_ig"] = dbr.reshape(8, HD), dbi.reshape(8, HD)
    g["lru_lambda"] = dsp
    g["q_norm_g"] = dqg.reshape(8, HD).sum(axis=0, keepdims=True)
    g["k_norm_g"] = dkg.reshape(8, HD).sum(axis=0, keepdims=True)
    g["w_in"] = _mm("wg_in", h1, dproj, "tn", 512, 640, stack=NCHIP)
    dh1 = _mm("dg_in", dproj, p["w_in"], "nt", 512, 512, stack=NCHIP)
    grad_x, g["g_mix"] = _rms_bwd("rms1_bwd", x, p["g_mix"], dh1, dx1, False)
    return loss_blk, grad_x, g


ANY = pl.BlockSpec(memory_space=pl.ANY)


def _mesh_pos():
    return lax.axis_index("x"), lax.axis_index("y"), lax.axis_index("c")


def _slot(px, py, perm):
    return 2 * py + px if perm else 2 * px + py


def _other_chips(x, y):
    return [(1 - x, y), (x, 1 - y), (1 - x, 1 - y)]


def _rcopy(src, dst, send, recv, k, to):
    return pltpu.make_async_remote_copy(src_ref=src, dst_ref=dst, send_sem=send.at[k], recv_sem=recv.at[k],
                                        device_id=to, device_id_type=MESH)


def _cast_bf16(name, w):
    r, c = w.shape
    tr = 128
    def body(w_ref, o_ref):
        o_ref[...] = w_ref[...].astype(BF16)
    return pl.pallas_call(
        body, name=name, grid=(r // tr,), in_specs=[_bs((tr, c), lambda i: (i, 0))],
        out_specs=_bs((tr, c), lambda i: (i, 0)), out_shape=jax.ShapeDtypeStruct((r, c), BF16),
        compiler_params=_cp("parallel"),
    )(w)


def _gather_weights(big, small):
    nb, ns = len(big), len(small)
    perms = [p for _, p in big] + [p for _, p in small]

    def body(*refs):
        ins, outs = refs[:nb + ns], refs[nb + ns:2 * (nb + ns)]
        send, recv, lsem = refs[2 * (nb + ns):]
        x, y, c = _mesh_pos()
        me, sib = (x, y, c), (x, y, 1 - c)
        chips = _other_chips(x, y)
        local = [pltpu.make_async_copy(ins[a], outs[a].at[_slot(x, y, perms[a])], lsem.at[a]) for a in range(nb + ns)]
        for cp in local:
            cp.start()
        first = []
        for a in range(nb):
            for j, (px, py) in enumerate(chips):
                first.append(_rcopy(ins[a].at[c], outs[a].at[_slot(x, y, perms[a]), c], send, recv, 3 * a + j, (px, py, c)))
        for t in range(ns):
            a = nb + t
            for j, (px, py) in enumerate(chips):
                first.append(_rcopy(ins[a], outs[a].at[_slot(x, y, perms[a])], send, recv, 6 * nb + 3 * t + j, (px, py, c)))
        for cp in first:
            cp.start()
        passed = []
        for a in range(nb):
            for j, (px, py) in enumerate(chips):
                got = outs[a].at[_slot(px, py, perms[a]), c]
                _rcopy(got, got, send, recv, 3 * a + j, me).wait_recv()
                fwd = _rcopy(got, got, send, recv, 3 * nb + 3 * a + j, sib)
                fwd.start()
                passed.append(fwd)
        for a in range(nb):
            for j, (px, py) in enumerate(chips):
                got = outs[a].at[_slot(px, py, perms[a]), 1 - c]
                _rcopy(got, got, send, recv, 3 * nb + 3 * a + j, me).wait_recv()
        for t in range(ns):
            a = nb + t
            for j, (px, py) in enumerate(chips):
                got = outs[a].at[_slot(px, py, perms[a])]
                _rcopy(got, got, send, recv, 6 * nb + 3 * t + j, me).wait_recv()
        for cp in first + passed:
            cp.wait_send()
        for cp in local:
            cp.wait()

    arrs = [a for a, _ in big] + [a for a, _ in small]
    nsem = 6 * nb + 3 * ns
    return pl.pallas_call(
        body, name="gather_weights", in_specs=[ANY] * (nb + ns), out_specs=[ANY] * (nb + ns),
        out_shape=[jax.ShapeDtypeStruct((NCHIP,) + a.shape, a.dtype) for a in arrs],
        scratch_shapes=[pltpu.SemaphoreType.DMA((nsem,)), pltpu.SemaphoreType.DMA((nsem,)),
                        pltpu.SemaphoreType.DMA((nb + ns,))],
    )(*arrs)


def _sibling_exchange(gs):
    na = len(gs)

    def body(*refs):
        ins, outs, (send, recv) = refs[:na], refs[na:2 * na], refs[2 * na:]
        x, y, c = _mesh_pos()
        cps = [_rcopy(ins[a].at[s, 1 - c], outs[a].at[s], send, recv, NCHIP * a + s, (x, y, 1 - c))
               for a in range(na) for s in range(NCHIP)]
        for cp in cps:
            cp.start()
        for cp in cps:
            cp.wait()

    return pl.pallas_call(
        body, name="rs_sibling", in_specs=[ANY] * na, out_specs=[ANY] * na,
        out_shape=[jax.ShapeDtypeStruct((NCHIP,) + g.shape[2:], F32) for g in gs],
        scratch_shapes=[pltpu.SemaphoreType.DMA((NCHIP * na,)), pltpu.SemaphoreType.DMA((NCHIP * na,))],
    )(*gs)


def _chip_exchange(ss, perms):
    na = len(ss)

    def body(*refs):
        ins, outs, (send, recv) = refs[:na], refs[na:2 * na], refs[2 * na:]
        x, y, c = _mesh_pos()
        cps = [_rcopy(ins[a].at[_slot(px, py, perms[a])], outs[a].at[j], send, recv, 3 * a + j, (px, py, c))
               for a in range(na) for j, (px, py) in enumerate(_other_chips(x, y))]
        for cp in cps:
            cp.start()
        for cp in cps:
            cp.wait()

    return pl.pallas_call(
        body, name="rs_chips", in_specs=[ANY] * na, out_specs=[ANY] * na,
        out_shape=[jax.ShapeDtypeStruct((3,) + s.shape[1:], F32) for s in ss],
        scratch_shapes=[pltpu.SemaphoreType.DMA((3 * na,)), pltpu.SemaphoreType.DMA((3 * na,))],
    )(*ss)


def _sibling_share(rs):
    na = len(rs)

    def body(*refs):
        ins, outs, (send, recv, lsem) = refs[:na], refs[na:2 * na], refs[2 * na:]
        x, y, c = _mesh_pos()
        local = [pltpu.make_async_copy(ins[a], outs[a].at[c], lsem.at[a]) for a in range(na)]
        cps = [_rcopy(ins[a], outs[a].at[c], send, recv, a, (x, y, 1 - c)) for a in range(na)]
        for cp in local + cps:
            cp.start()
        for a in range(na):
            got = outs[a].at[1 - c]
            _rcopy(got, got, send, recv, a, (x, y, c)).wait_recv()
        for cp in cps:
            cp.wait_send()
        for cp in local:
            cp.wait()

    return pl.pallas_call(
        body, name="rs_share", in_specs=[ANY] * na, out_specs=[ANY] * na,
        out_shape=[jax.ShapeDtypeStruct((2,) + r.shape, F32) for r in rs],
        scratch_shapes=[pltpu.SemaphoreType.DMA((na,)), pltpu.SemaphoreType.DMA((na,)), pltpu.SemaphoreType.DMA((na,))],
    )(*rs)


def _add_half(name, g, got, c_arr):
    _, _, r2, cc = g.shape
    tr = 128

    def body(c_ref, g_ref, r_ref, o_ref):
        del c_ref
        o_ref[...] = g_ref[...] + r_ref[...]

    return pl.pallas_call(
        body, name=name,
        grid_spec=pltpu.PrefetchScalarGridSpec(
            num_scalar_prefetch=1, grid=(NCHIP, r2 // tr),
            in_specs=[_bs((None, None, tr, cc), lambda s, i, c_ref: (s, c_ref[0], i, 0)),
                      _bs((None, tr, cc), lambda s, i, c_ref: (s, i, 0))],
            out_specs=_bs((None, tr, cc), lambda s, i, c_ref: (s, i, 0))),
        out_shape=jax.ShapeDtypeStruct((NCHIP, r2, cc), F32), compiler_params=_cp("parallel", "parallel"),
    )(c_arr, g, got)


def _add_chips(name, s, got, slot_arr):
    _, r2, cc = s.shape
    tr = 128

    def body(slot_ref, s_ref, r_ref, o_ref):
        del slot_ref
        o_ref[...] = ((s_ref[...] + r_ref[0]) + r_ref[1]) + r_ref[2]

    return pl.pallas_call(
        body, name=name,
        grid_spec=pltpu.PrefetchScalarGridSpec(
            num_scalar_prefetch=1, grid=(r2 // tr,),
            in_specs=[_bs((None, tr, cc), lambda i, slot_ref: (slot_ref[0], i, 0)),
                      _bs((3, tr, cc), lambda i, slot_ref: (0, i, 0))],
            out_specs=_bs((tr, cc), lambda i, slot_ref: (i, 0))),
        out_shape=jax.ShapeDtypeStruct((r2, cc), F32), compiler_params=_cp("parallel"),
    )(slot_arr, s, got)


def _adam_math(w, g, m, v):
    m = ADAM_B1 * m + (1.0 - ADAM_B1) * g
    v = ADAM_B2 * v + (1.0 - ADAM_B2) * (g * g)
    m_hat = m / (1.0 - ADAM_B1 ** ADAM_STEP)
    v_hat = v / (1.0 - ADAM_B2 ** ADAM_STEP)
    return -ADAM_LR * (m_hat / (jnp.sqrt(v_hat) + ADAM_EPS) + ADAM_WD * w), m, v


def _adam_big(name, w, g, m, v):
    r, c = w.shape
    tr = 128

    def body(w_ref, g_ref, m_ref, v_ref, d_ref, m2_ref, v2_ref):
        d_ref[...], m2_ref[...], v2_ref[...] = _adam_math(w_ref[...], g_ref[...], m_ref[...], v_ref[...])

    spec = _bs((tr, c), lambda i: (i, 0))
    out = jax.ShapeDtypeStruct((r, c), F32)
    return pl.pallas_call(
        body, name=name, grid=(r // tr,), in_specs=[spec] * 4, out_specs=[spec] * 3, out_shape=[out] * 3,
        compiler_params=_cp("parallel"),
    )(w, g, m, v)


_CLASS_SHAPE = {"a": (8, D), "b": (8, RW), "c": (8, 2 * DFF), "d": (1048, HD)}
_SMALL = (
    ("g_mix", "a", 0, 1, D), ("g_ffn", "a", 1, 1, D),
    ("rec_conv_w", "b", 0, 4, RW), ("rec_conv_b", "b", 4, 1, RW), ("lru_lambda", "b", 5, 1, RW),
    ("g_attn_out", "b", 6, 1, RW), ("g_rec_out", "b", 7, 1, RW),
    ("ffn_conv_w", "c", 0, 3, 2 * DFF), ("ffn_conv_b", "c", 3, 1, 2 * DFF),
    ("w_rg", "d", 0, RW, HD), ("w_ig", "d", RW, RW, HD), ("b_rg", "d", 2 * RW, 8, HD), ("b_ig", "d", 2 * RW + 8, 8, HD),
    ("q_norm_g", "d", 2 * RW + 16, 1, HD), ("k_norm_g", "d", 2 * RW + 17, 1, HD),
)
_LOSS_ROW = 2
_CLASSES = ("a", "b", "c", "d")


def _small_allreduce(g, loss_blk):
    names = [s[0] for s in _SMALL]
    nin = len(names) + 1

    def body(*refs):
        ins = dict(zip(names, refs[:len(names)]))
        loss_ref = refs[len(names)]
        outs = dict(zip(_CLASSES, refs[nin:nin + 4]))
        pair = dict(zip(_CLASSES, refs[nin + 4:nin + 8]))
        quad = dict(zip(_CLASSES, refs[nin + 8:nin + 12]))
        send, recv = refs[nin + 12:]
        x, y, c = _mesh_pos()
        chip = 2 * x + y
        pair["a"][c] = jnp.zeros(_CLASS_SHAPE["a"], F32)
        pair["b"][c] = ins["rec_conv_w"][...]
        pair["c"][c] = ins["ffn_conv_w"][...]
        pair["d"][c, 2 * RW + 16:, :] = jnp.zeros((8, HD), F32)
        for name, k, r0, nr, _ in _SMALL:
            if name in ("rec_conv_w", "ffn_conv_w"):
                continue
            pair[k][c, r0:r0 + nr, :] = ins[name][...]
        pair["a"][c, _LOSS_ROW:_LOSS_ROW + 1, :] = jnp.broadcast_to(loss_ref[0:1, 0:1], (1, D))
        cps = [_rcopy(pair[k].at[c], pair[k].at[c], send, recv, ki, (x, y, 1 - c)) for ki, k in enumerate(_CLASSES)]
        for cp in cps:
            cp.start()
        for ki, k in enumerate(_CLASSES):
            _rcopy(pair[k].at[1 - c], pair[k].at[1 - c], send, recv, ki, (x, y, c)).wait_recv()
            quad[k][chip] = pair[k][0] + pair[k][1]
        cps2 = []
        for ki, k in enumerate(_CLASSES):
            for j, (px, py) in enumerate(_other_chips(x, y)):
                cps2.append(_rcopy(quad[k].at[chip], quad[k].at[chip], send, recv, 4 + 3 * ki + j, (px, py, c)))
        for cp in cps2:
            cp.start()
        for ki, k in enumerate(_CLASSES):
            for j, (px, py) in enumerate(_other_chips(x, y)):
                got = quad[k].at[2 * px + py]
                _rcopy(got, got, send, recv, 4 + 3 * ki + j, (x, y, c)).wait_recv()
            outs[k][...] = ((quad[k][0] + quad[k][1]) + quad[k][2]) + quad[k][3]
        for cp in cps + cps2:
            cp.wait_send()

    vm = pl.BlockSpec(memory_space=pltpu.VMEM)
    return pl.pallas_call(
        body, name="small_allreduce", in_specs=[vm] * nin, out_specs=[vm] * 4,
        out_shape=[jax.ShapeDtypeStruct(_CLASS_SHAPE[k], F32) for k in _CLASSES],
        scratch_shapes=[pltpu.VMEM((2,) + _CLASS_SHAPE[k], F32) for k in _CLASSES]
        + [pltpu.VMEM((NCHIP,) + _CLASS_SHAPE[k], F32) for k in _CLASSES]
        + [pltpu.SemaphoreType.DMA((16,)), pltpu.SemaphoreType.DMA((16,))],
        compiler_params=pltpu.CompilerParams(vmem_limit_bytes=VMEM_LIMIT),
    )(*[g[n] for n in names], loss_blk)


def _adam_small(red, w, m, v):
    names = [s[0] for s in _SMALL]
    n = len(names)

    def body(*refs):
        red_refs = dict(zip(_CLASSES, refs[:4]))
        w_refs, m_refs, v_refs = refs[4:4 + n], refs[4 + n:4 + 2 * n], refs[4 + 2 * n:4 + 3 * n]
        loss_ref = refs[4 + 3 * n]
        out_refs = refs[5 + 3 * n:]
        x, y, _ = _mesh_pos()
        chip = 2 * x + y
        loss_ref[...] = jnp.broadcast_to(red_refs["a"][_LOSS_ROW:_LOSS_ROW + 1, 0:1], loss_ref.shape)
        for pi, (name, k, r0, nr, width) in enumerate(_SMALL):
            gfull = red_refs[k][r0:r0 + nr, :]
            if name == "rec_conv_w":
                parts = [gfull[:, 128 * s:128 * (s + 1)] for s in range(NCHIP)]
                g = jnp.where(chip == 0, parts[0], jnp.where(chip == 1, parts[1], jnp.where(chip == 2, parts[2], parts[3])))
            elif name == "ffn_conv_w":
                parts = [gfull[:, FC * s:FC * (s + 1)] for s in range(NCHIP)]
                g = jnp.where(chip == 0, parts[0], jnp.where(chip == 1, parts[2], jnp.where(chip == 2, parts[1], parts[3])))
            elif name == "ffn_conv_b":
                g = jnp.concatenate([gfull[:, FC * s:FC * (s + 1)] for s in (0, 2, 1, 3)], axis=1)
            else:
                g = gfull
            d, m2, v2 = _adam_math(w_refs[pi][...], g, m_refs[pi][...], v_refs[pi][...])
            o = out_refs[4 * pi:4 * pi + 4]
            o[0][...], o[1][...], o[2][...], o[3][...] = g, d, m2, v2

    vm = pl.BlockSpec(memory_space=pltpu.VMEM)
    outs = [jax.ShapeDtypeStruct((1, 128), F32)]
    for name in names:
        outs += [jax.ShapeDtypeStruct(w[name].shape, F32)] * 4
    res = pl.pallas_call(
        body, name="adam_small", in_specs=[vm] * (4 + 3 * n), out_specs=[vm] * len(outs), out_shape=outs,
        compiler_params=pltpu.CompilerParams(vmem_limit_bytes=VMEM_LIMIT),
    )(*red, *[w[k] for k in names], *[m[k] for k in names], *[v[k] for k in names])
    return res[0], {name: res[1 + 4 * i:5 + 4 * i] for i, name in enumerate(names)}


_WEIGHTS = ("g_mix", "w_in", "q_norm_g", "k_norm_g", "rec_conv_w", "rec_conv_b", "w_rg", "b_rg", "w_ig", "b_ig",
            "lru_lambda", "g_attn_out", "g_rec_out", "w_out", "g_ffn", "w_up", "ffn_conv_w", "ffn_conv_b", "w_down")
_BIG = ("w_in", "w_out", "w_up", "w_down")
_BIG_PERM = {"w_in": False, "w_out": False, "w_up": True, "w_down": False}
_SMALL_2D = {"w_rg": (RW, HD), "w_ig": (RW, HD), "b_rg": (8, HD), "b_ig": (8, HD), "rec_conv_w": (4, 128),
             "ffn_conv_w": (3, FC)}


def _halves(a):
    r, c = a.shape
    return a.reshape(2, r // 2, c)


def kernel(x, positions, g_mix, w_in, q_norm_g, k_norm_g, rec_conv_w, rec_conv_b, w_rg, b_rg, w_ig, b_ig, lru_lambda, g_attn_out, g_rec_out, w_out, g_ffn, w_up, ffn_conv_w, ffn_conv_b, w_down, loss_target, m_g_mix, m_w_in, m_q_norm_g, m_k_norm_g, m_rec_conv_w, m_rec_conv_b, m_w_rg, m_b_rg, m_w_ig, m_b_ig, m_lru_lambda, m_g_attn_out, m_g_rec_out, m_w_out, m_g_ffn, m_w_up, m_ffn_conv_w, m_ffn_conv_b, m_w_down, v_g_mix, v_w_in, v_q_norm_g, v_k_norm_g, v_rec_conv_w, v_rec_conv_b, v_w_rg, v_b_rg, v_w_ig, v_b_ig, v_lru_lambda, v_g_attn_out, v_g_rec_out, v_w_out, v_g_ffn, v_w_up, v_ffn_conv_w, v_ffn_conv_b, v_w_down):
    given = dict(g_mix=g_mix, w_in=w_in, q_norm_g=q_norm_g, k_norm_g=k_norm_g, rec_conv_w=rec_conv_w, rec_conv_b=rec_conv_b, w_rg=w_rg, b_rg=b_rg, w_ig=w_ig, b_ig=b_ig, lru_lambda=lru_lambda, g_attn_out=g_attn_out, g_rec_out=g_rec_out, w_out=w_out, g_ffn=g_ffn, w_up=w_up, ffn_conv_w=ffn_conv_w, ffn_conv_b=ffn_conv_b, w_down=w_down)
    given_m = dict(g_mix=m_g_mix, w_in=m_w_in, q_norm_g=m_q_norm_g, k_norm_g=m_k_norm_g, rec_conv_w=m_rec_conv_w, rec_conv_b=m_rec_conv_b, w_rg=m_w_rg, b_rg=m_b_rg, w_ig=m_w_ig, b_ig=m_b_ig, lru_lambda=m_lru_lambda, g_attn_out=m_g_attn_out, g_rec_out=m_g_rec_out, w_out=m_w_out, g_ffn=m_g_ffn, w_up=m_w_up, ffn_conv_w=m_ffn_conv_w, ffn_conv_b=m_ffn_conv_b, w_down=m_w_down)
    given_v = dict(g_mix=v_g_mix, w_in=v_w_in, q_norm_g=v_q_norm_g, k_norm_g=v_k_norm_g, rec_conv_w=v_rec_conv_w, rec_conv_b=v_rec_conv_b, w_rg=v_w_rg, b_rg=v_b_rg, w_ig=v_w_ig, b_ig=v_b_ig, lru_lambda=v_lru_lambda, g_attn_out=v_g_attn_out, g_rec_out=v_g_rec_out, w_out=v_w_out, g_ffn=v_g_ffn, w_up=v_w_up, ffn_conv_w=v_ffn_conv_w, ffn_conv_b=v_ffn_conv_b, w_down=v_w_down)
    shapes = {n: a.shape for n, a in given.items()}

    def two_d(n, a):
        a = a[0]
        return a.reshape(_SMALL_2D[n]) if n in _SMALL_2D else (a if a.ndim == 2 else a[None])

    w = {n: two_d(n, a) for n, a in given.items()}
    m = {n: two_d(n, a) for n, a in given_m.items()}
    v = {n: two_d(n, a) for n, a in given_v.items()}
    c_arr = lax.axis_index("c").astype(jnp.int32).reshape(1)
    cx, cy = lax.axis_index("x"), lax.axis_index("y")

    big = [(_halves(_cast_bf16(f"cast_{n}", w[n])), _BIG_PERM[n]) for n in _BIG]
    small = [(jnp.pad(w["ffn_conv_w"], ((0, 5), (0, 0))), True), (jnp.pad(w["rec_conv_w"], ((0, 4), (0, 0))), False)]
    f_in, f_out, f_up, f_down, f_fcw, f_rcw = _gather_weights(big, small)
    p = {n: w[n] for n in ("g_mix", "g_ffn", "q_norm_g", "k_norm_g", "rec_conv_b", "lru_lambda", "g_attn_out", "g_rec_out")}
    p.update(w_rg=w["w_rg"].reshape(8, HD, HD), w_ig=w["w_ig"].reshape(8, HD, HD), b_rg=w["b_rg"], b_ig=w["b_ig"],
             w_in=f_in.reshape(NCHIP, D, INW // NCHIP), w_up=f_up.reshape(NCHIP, D, FC), w_out=f_out.reshape(D, D),
             w_down=f_down.reshape(DFF, D), ffn_conv_w=f_fcw,
             ffn_conv_b=jnp.concatenate([w["ffn_conv_b"][:, FC * s:FC * (s + 1)] for s in (0, 2, 1, 3)], axis=1),
             rec_conv_w=f_rcw.transpose(1, 0, 2).reshape(8, RW))

    loss_blk, grad_x, g = _local_step(x[0], positions.reshape(T, 1), loss_target[0], p)

    gs = [g["w_in"].reshape(NCHIP, 2, D // 2, INW // NCHIP), g["w_out"].reshape(NCHIP, 2, D // 8, D),
          g["w_up"].reshape(NCHIP, 2, D // 2, FC), g["w_down"].reshape(NCHIP, 2, DFF // 8, D)]
    perms = [_BIG_PERM[n] for n in _BIG]
    from_sib = _sibling_exchange(gs)
    chip_sums = [_add_half(f"rs_add1_{n}", gs[i], from_sib[i], c_arr) for i, n in enumerate(_BIG)]
    from_chips = _chip_exchange(chip_sums, perms)
    slot = {False: (2 * cx + cy).astype(jnp.int32).reshape(1), True: (2 * cy + cx).astype(jnp.int32).reshape(1)}
    mine = [_add_chips(f"rs_add2_{n}", chip_sums[i], from_chips[i], slot[perms[i]]) for i, n in enumerate(_BIG)]
    reduced = _sibling_share(mine)

    out_g, out_d, out_m, out_v = {}, {}, {}, {}
    for i, n in enumerate(_BIG):
        gn = reduced[i].reshape(w[n].shape)
        out_g[n] = gn
        out_d[n], out_m[n], out_v[n] = _adam_big(f"adam_{n}", w[n], gn, m[n], v[n])

    red = _small_allreduce(g, loss_blk)
    loss_row, small_out = _adam_small(red, w, m, v)
    for n, (gn, dn, mn, vn) in small_out.items():
        out_g[n], out_d[n], out_m[n], out_v[n] = gn, dn, mn, vn

    outs = [loss_row[0, 0], grad_x[None]]
    for group in (out_g, out_d, out_m, out_v):
        outs += [group[n].reshape(shapes[n]) for n in _WEIGHTS]
    return tuple(outs)
```

```python
import math

import jax
import jax.numpy as jnp
import numpy as np
from jax import lax
from jax.experimental import pallas as pl
from jax.experimental.pallas import tpu as pltpu

F32 = jnp.float32
BF16 = jnp.bfloat16

T = 4096
D = 1024
HD = 64
AW = 512
RW = 512
INW = 2560
DFF = 3072
NCHIP = 4
EPS = 1e-6
NEG = -1e30
LRU_C = 8.0
ROPE_THETA = 10000.0
BLK = 128
DILATIONS = (1, 4, 16)
ADAM_LR, ADAM_B1, ADAM_B2, ADAM_EPS, ADAM_WD, ADAM_STEP = 0.001, 0.9, 0.999, 1e-08, 0.01, 10
VMEM_LIMIT = 56 * 1024 * 1024
MESH = pl.DeviceIdType.MESH

NN = (((1,), (0,)), ((), ()))
NT = (((1,), (1,)), ((), ()))
TN = (((0,), (0,)), ((), ()))


def _cp(*sem):
    return pltpu.CompilerParams(dimension_semantics=sem, vmem_limit_bytes=VMEM_LIMIT)


def _bs(shape, fn):
    return pl.BlockSpec(shape, fn)


def _dot(a, b, dims=NN):
    return lax.dot_general(a, b, dims, preferred_element_type=F32)


_GC = math.sqrt(2.0 / math.pi)


def _gelu(x):
    return x * (0.5 * (1.0 + jnp.tanh(_GC * (x + 0.044715 * (x * x * x)))))


def _gelu_and_grad(x):
    x2 = x * x
    th = jnp.tanh(_GC * (x + 0.044715 * (x * x2)))
    cdf = 0.5 * (1.0 + th)
    dg = cdf + 0.5 * x * (1.0 - th * th) * (_GC * (1.0 + 3.0 * 0.044715 * x2))
    return x * cdf, dg


def _softplus(x):
    e = jnp.exp(-jnp.abs(x))
    u = 1.0 + e
    l1p = jnp.where(u == 1.0, e, jnp.log(u) * (e / (u - 1.0)))
    return jnp.maximum(x, 0.0) + l1p


def _segsum(z, e_bf16):
    hi = z.astype(BF16)
    lo = (z - hi.astype(F32)).astype(BF16)
    return _dot(hi, e_bf16) + _dot(lo, e_bf16)


def _mm(name, a, b, mode, tm, tn, out_dtype=F32, res=None, stack=0, twin_bf16=False, after=()):
    if mode == "nn":
        (m, k), n = a.shape, (b.shape[1] if not stack else stack * b.shape[2])
        a_spec = _bs((tm, k), lambda j, i: (i, 0))
        if stack:
            per = b.shape[2] // tn
            b_spec = _bs((None, k, tn), lambda j, i: (j // per, 0, j % per))
        else:
            b_spec = _bs((k, tn), lambda j, i: (0, j))
    elif mode == "nt":
        (m, k), n = a.shape, (b.shape[0] if not stack else b.shape[1])
        a_spec = _bs((tm, k), lambda j, i: (i, 0))
        b_spec = _bs((stack, tn, k // stack), lambda j, i: (0, j, 0)) if stack else _bs((tn, k), lambda j, i: (j, 0))
    else:
        (k, m), n = a.shape, b.shape[1]
        a_spec, b_spec = _bs((k, tm), lambda j, i: (0, i)), _bs((k, tn), lambda j, i: (0, j))
    assert m % tm == 0 and n % tn == 0
    o_spec = _bs((tm, tn), lambda j, i: (i, j))
    o_shape = (m, n)
    if mode == "tn" and stack:
        per = n // stack // tn
        o_spec = _bs((None, tm, tn), lambda j, i: (j // per, i, j % per))
        o_shape = (stack, m, n // stack)
    dims = {"nn": NN, "nt": NT, "tn": TN}[mode]

    def product(a_ref, b_ref):
        if mode == "nt" and stack:
            cs = k // stack
            acc = _dot(a_ref[:, 0:cs], b_ref[0], NT)
            for s in range(1, stack):
                acc = acc + _dot(a_ref[:, s * cs:(s + 1) * cs], b_ref[s], NT)
            return acc
        return _dot(a_ref[...], b_ref[...], dims)

    nres = 0 if res is None else 1

    def body(a_ref, b_ref, *rest):
        acc = product(a_ref, b_ref)
        if nres:
            acc = rest[0][...] + acc
        outs = rest[nres + len(after):]
        outs[0][...] = acc.astype(out_dtype)
        if twin_bf16:
            outs[1][...] = acc.astype(BF16)

    ins = (a, b) + ((res,) if nres else ()) + tuple(after)
    specs = [a_spec, b_spec] + ([o_spec] if nres else []) + [pl.BlockSpec(memory_space=pl.ANY)] * len(after)
    shapes = [jax.ShapeDtypeStruct(o_shape, out_dtype)] + ([jax.ShapeDtypeStruct(o_shape, BF16)] if twin_bf16 else [])
    out = pl.pallas_call(
        body, name=name, grid=(n // tn, m // tm), in_specs=specs, out_specs=[o_spec] * len(shapes),
        out_shape=shapes, compiler_params=_cp("parallel", "parallel"),
    )(*ins)
    return tuple(out) if twin_bf16 else out[0]


def _rms_fwd(name, x, g):
    tr = 512

    def body(x_ref, g_ref, o_ref):
        xv = x_ref[...]
        r = lax.rsqrt(jnp.mean(xv * xv, axis=-1, keepdims=True) + EPS)
        o_ref[...] = ((xv * r) * g_ref[...]).astype(BF16)

    return pl.pallas_call(
        body, name=name, grid=(T // tr,), in_specs=[_bs((tr, D), lambda i: (i, 0)), _bs((1, D), lambda i: (0, 0))],
        out_specs=_bs((tr, D), lambda i: (i, 0)), out_shape=jax.ShapeDtypeStruct((T, D), BF16),
        compiler_params=_cp("parallel"),
    )(x, g)


def _rms_bwd(name, x, g, dy, dres, want_bf16):
    tr = 256

    def body(x_ref, g_ref, dy_ref, dr_ref, dx_ref, *rest):
        dg_ref = rest[-1]
        xv, dyv = x_ref[...], dy_ref[...]
        r = lax.rsqrt(jnp.mean(xv * xv, axis=-1, keepdims=True) + EPS)
        gdy = g_ref[...] * dyv
        dx = r * gdy - xv * ((r * r * r) * jnp.mean(xv * gdy, axis=-1, keepdims=True)) + dr_ref[...]
        dx_ref[...] = dx
        if want_bf16:
            rest[0][...] = dx.astype(BF16)

        @pl.when(pl.program_id(0) == 0)
        def _():
            dg_ref[...] = jnp.zeros_like(dg_ref)

        dg_ref[...] += jnp.sum(dyv * (xv * r), axis=0, keepdims=True)

    row = _bs((tr, D), lambda i: (i, 0))
    vec = _bs((1, D), lambda i: (0, 0))
    outs = [jax.ShapeDtypeStruct((T, D), F32)] + ([jax.ShapeDtypeStruct((T, D), BF16)] if want_bf16 else [])
    return pl.pallas_call(
        body, name=name, grid=(T // tr,), in_specs=[row, vec, row, row],
        out_specs=[row] * len(outs) + [vec], out_shape=outs + [jax.ShapeDtypeStruct((1, D), F32)],
        compiler_params=_cp("arbitrary"),
    )(x, g, dy, dres)


def _head_ones():
    idx = np.arange(AW) // HD
    return jnp.asarray((idx[:, None] == idx[None, :]).astype(np.float32), dtype=BF16)


def _freq_row():
    half = HD // 2
    inv = ROPE_THETA ** (-(np.arange(half, dtype=np.float64)) / half)
    return jnp.asarray(np.tile(inv, 4)[None, :], dtype=F32)


def _rot_tables(pos_ref, f_ref):
    ang = pos_ref[...].astype(F32) * f_ref[...]
    c = jnp.tile(jnp.cos(ang), (1, 4))
    s = jnp.tile(jnp.sin(ang), (1, 4))
    lane = lax.broadcasted_iota(jnp.int32, (1, AW), 1)
    first = (lane & 32) == 0
    return c, jnp.where(first, -s, s), first


def _swap_halves(y, first):
    return jnp.where(first, pltpu.roll(y, AW - 32, 1), pltpu.roll(y, 32, 1))


def _qk_prep(proj, pos_col, qg, kg):
    tr = 512

    def body(q_ref, k_ref, v_ref, pos_ref, f_ref, qg_ref, kg_ref, e_ref, qo_ref, ko_ref, vo_ref):
        c, s_signed, first = _rot_tables(pos_ref, f_ref)
        e = e_ref[...]

        def norm_rot(xv, g, scale):
            r = lax.rsqrt(_segsum(xv * xv, e) * (1.0 / HD) + EPS)
            y = (xv * r) * g
            return (y * c + _swap_halves(y, first) * s_signed) * scale

        qo_ref[...] = norm_rot(q_ref[...], qg_ref[...], HD ** -0.5).astype(BF16)
        ko_ref[...] = norm_rot(k_ref[...], kg_ref[...], 1.0).astype(BF16)
        vo_ref[...] = v_ref[...].astype(BF16)

    col = lambda j: _bs((tr, AW), lambda i, j=j: (i, j))
    vec = _bs((1, AW), lambda i: (0, 0))
    out = jax.ShapeDtypeStruct((T, AW), BF16)
    return pl.pallas_call(
        body, name="qk_prep", grid=(T // tr,),
        in_specs=[col(0), col(1), col(2), _bs((tr, 1), lambda i: (i, 0)), _bs((1, 128), lambda i: (0, 0)), vec, vec,
                  _bs((AW, AW), lambda i: (0, 0))],
        out_specs=[col(0)] * 3, out_shape=[out] * 3, compiler_params=_cp("parallel"),
    )(proj, proj, proj, pos_col, _freq_row(), qg, kg, _head_ones())


def _qk_bwd(proj, pos_col, qg, kg, dqs, dks, dvs):
    tr = 256

    def body(q_ref, k_ref, pos_ref, f_ref, qg_ref, kg_ref, e_ref, a0, a1, a2, b0, b1, b2, c0, c1, c2,
             o_ref, dqg_ref, dkg_ref):
        i, j = pl.program_id(0), pl.program_id(1)

        @pl.when((i == 0) & (j == 0))
        def _():
            dqg_ref[...] = jnp.zeros_like(dqg_ref)
            dkg_ref[...] = jnp.zeros_like(dkg_ref)

        def norm_rot_bwd(x_ref, g_ref, dg_ref, d0, d1, d2, scale):
            c, s_signed, first = _rot_tables(pos_ref, f_ref)
            e = e_ref[...]
            dout = ((d0[...] + d1[...]) + d2[...]) * scale
            dy = dout * c + _swap_halves(dout * s_signed, first)
            xv, g = x_ref[...], g_ref[...]
            r = lax.rsqrt(_segsum(xv * xv, e) * (1.0 / HD) + EPS)
            gdy = g * dy
            dx = r * gdy - xv * ((r * r * r) * (_segsum(xv * gdy, e) * (1.0 / HD)))
            o_ref[...] = dx.astype(BF16)
            dg_ref[...] += jnp.sum(dy * (xv * r), axis=0, keepdims=True)

        @pl.when(j == 0)
        def _():
            norm_rot_bwd(q_ref, qg_ref, dqg_ref, a0, a1, a2, HD ** -0.5)

        @pl.when(j == 1)
        def _():
            norm_rot_bwd(k_ref, kg_ref, dkg_ref, b0, b1, b2, 1.0)

        @pl.when(j == 2)
        def _():
            o_ref[...] = ((c0[...] + c1[...]) + c2[...]).astype(BF16)

    col = lambda jj: _bs((tr, AW), lambda i, j, jj=jj: (i, jj))
    vec = _bs((1, AW), lambda i, j: (0, 0))
    piece = _bs((tr, AW), lambda i, j: (i, 0))
    return pl.pallas_call(
        body, name="qk_bwd", grid=(T // tr, 3),
        in_specs=[col(0), col(1), _bs((tr, 1), lambda i, j: (i, 0)), _bs((1, 128), lambda i, j: (0, 0)), vec, vec,
                  _bs((AW, AW), lambda i, j: (0, 0))] + [piece] * 9,
        out_specs=[_bs((tr, AW), lambda i, j: (i, j)), vec, vec],
        out_shape=[jax.ShapeDtypeStruct((T, INW), BF16), jax.ShapeDtypeStruct((1, AW), F32),
                   jax.ShapeDtypeStruct((1, AW), F32)],
        compiler_params=_cp("arbitrary", "arbitrary"),
    )(proj, proj, pos_col, _freq_row(), qg, kg, _head_ones(), *dqs, *dks, *dvs)


def _regroup(a, d):
    return a if d == 1 else a.reshape(T // d, d, AW).transpose(1, 0, 2).reshape(T, AW)


def _ungroup(a, d):
    return a if d == 1 else a.reshape(d, T // d, AW).transpose(1, 0, 2).reshape(T, AW)


def _band_masks():
    qi = lax.broadcasted_iota(jnp.int32, (BLK, 2 * BLK), 0)
    kj = lax.broadcasted_iota(jnp.int32, (BLK, 2 * BLK), 1)
    rel = qi - kj + BLK
    wide = (rel >= 0) & (rel <= BLK)
    qi1 = lax.broadcasted_iota(jnp.int32, (BLK, BLK), 0)
    kj1 = lax.broadcasted_iota(jnp.int32, (BLK, BLK), 1)
    return kj1 <= qi1, wide


def _attn_fwd(name, q, k, v, d):
    ln = T // d
    nb = ln // BLK

    def body(q_ref, k_ref, v_ref, o_ref, l_ref):
        first_mask, wide_mask = _band_masks()
        lane = lax.broadcasted_iota(jnp.int32, (1, 128), 1)
        h0 = lane < HD

        def head(qm, kk, vv, mask):
            s = jnp.where(mask, _dot(qm, kk, NT), NEG)
            m = jnp.max(s, axis=1, keepdims=True)
            p = jnp.exp(s - m)
            l = jnp.sum(p, axis=1, keepdims=True)
            return _dot(p.astype(BF16), vv) / l, m + jnp.log(l)

        def block(r0, k0, kn, mask):
            qv = q_ref[pl.ds(r0, BLK), :]
            kk = k_ref[pl.ds(k0, kn), :]
            vv = v_ref[pl.ds(k0, kn), :]
            zero = jnp.zeros_like(qv)
            o_a, l_a = head(jnp.where(h0, qv, zero), kk, vv, mask)
            o_b, l_b = head(jnp.where(h0, zero, qv), kk, vv, mask)
            o_ref[pl.ds(r0, BLK), :] = jnp.where(h0, o_a, o_b)
            l_ref[pl.ds(r0, BLK), :] = jnp.where(h0, l_a, l_b)

        block(0, 0, BLK, first_mask)

        def step(n, carry):
            r0 = pl.multiple_of(n * BLK, BLK)
            block(r0, pl.multiple_of(r0 - BLK, BLK), 2 * BLK, wide_mask)
            return carry

        lax.fori_loop(1, nb, step, 0)

    spec = _bs((ln, 128), lambda c, p: (c, p))
    out = jax.ShapeDtypeStruct((T, AW), F32)
    return pl.pallas_call(
        body, name=name, grid=(d, AW // 128), in_specs=[spec] * 3, out_specs=[spec] * 2, out_shape=[out] * 2,
        compiler_params=_cp("parallel", "parallel"),
    )(q, k, v)


def _attn_bwd(name, q, k, v, do, lse, delta, d):
    ln = T // d
    nb = ln // BLK

    def body(q_ref, k_ref, v_ref, do_ref, l_ref, dl_ref, dq_ref, dk_ref, dv_ref):
        first_mask, wide_mask = _band_masks()
        lane = lax.broadcasted_iota(jnp.int32, (1, 128), 1)
        h0 = lane < HD
        dk_ref[...] = jnp.zeros_like(dk_ref)
        dv_ref[...] = jnp.zeros_like(dv_ref)

        def head(qm, dom, kk, vv, lse_c, dl_c, mask):
            s = jnp.where(mask, _dot(qm, kk, NT), NEG)
            p = jnp.exp(s - lse_c)
            ds = p * (_dot(dom, vv, NT) - dl_c)
            pb, dsb = p.astype(BF16), ds.astype(BF16)
            return _dot(dsb, kk), _dot(dsb, qm, TN), _dot(pb, dom, TN)

        def block(r0, k0, kn, mask):
            qv = q_ref[pl.ds(r0, BLK), :]
            dov = do_ref[pl.ds(r0, BLK), :]
            lv = l_ref[pl.ds(r0, BLK), :]
            dlv = dl_ref[pl.ds(r0, BLK), :]
            kk = k_ref[pl.ds(k0, kn), :]
            vv = v_ref[pl.ds(k0, kn), :]
            zero = jnp.zeros_like(qv)
            dq_a, dk_a, dv_a = head(jnp.where(h0, qv, zero), jnp.where(h0, dov, zero), kk, vv,
                                    lv[:, 0:1], dlv[:, 0:1], mask)
            dq_b, dk_b, dv_b = head(jnp.where(h0, zero, qv), jnp.where(h0, zero, dov), kk, vv,
                                    lv[:, HD:HD + 1], dlv[:, HD:HD + 1], mask)
            dq_ref[pl.ds(r0, BLK), :] = jnp.where(h0, dq_a, dq_b)
            dk_ref[pl.ds(k0, kn), :] += dk_a + dk_b
            dv_ref[pl.ds(k0, kn), :] += dv_a + dv_b

        block(0, 0, BLK, first_mask)

        def step(n, carry):
            r0 = pl.multiple_of(n * BLK, BLK)
            block(r0, pl.multiple_of(r0 - BLK, BLK), 2 * BLK, wide_mask)
            return carry

        lax.fori_loop(1, nb, step, 0)

    spec = _bs((ln, 128), lambda c, p: (c, p))
    out = jax.ShapeDtypeStruct((T, AW), F32)
    return pl.pallas_call(
        body, name=name, grid=(d, AW // 128), in_specs=[spec] * 6, out_specs=[spec] * 3, out_shape=[out] * 3,
        compiler_params=_cp("parallel", "parallel"),
    )(q, k, v, do, lse, delta)


def _attn_merge(os_, ls_, g_attn):
    tr = 512

    def body(o0, o1, o2, l0, l1, l2, g_ref, a_ref, lse_ref, mix_ref):
        la, lb, lc = l0[...], l1[...], l2[...]
        m = jnp.maximum(jnp.maximum(la, lb), lc)
        ea, eb, ec = jnp.exp(la - m), jnp.exp(lb - m), jnp.exp(lc - m)
        z = (ea + eb) + ec
        attn = ((ea * o0[...] + eb * o1[...]) + ec * o2[...]) / z
        a_ref[...] = attn
        lse_ref[...] = m + jnp.log(z)
        r = lax.rsqrt(jnp.mean(attn * attn, axis=-1, keepdims=True) + EPS)
        mix_ref[...] = ((attn * r) * g_ref[...]).astype(BF16)

    row = _bs((tr, AW), lambda i: (i, 0))
    f = jax.ShapeDtypeStruct((T, AW), F32)
    return pl.pallas_call(
        body, name="attn_merge", grid=(T // tr,), in_specs=[row] * 6 + [_bs((1, AW), lambda i: (0, 0))],
        out_specs=[row, row, row], out_shape=[f, f, jax.ShapeDtypeStruct((T, D), BF16)],
        compiler_params=_cp("parallel"),
    )(*os_, *ls_, g_attn)


def _attn_out_bwd(attn, dmix, g_attn):
    tr = 256

    def body(a_ref, d_ref, g_ref, e_ref, do_ref, dl_ref, dg_ref):
        av, dyv = a_ref[...], d_ref[...]
        r = lax.rsqrt(jnp.mean(av * av, axis=-1, keepdims=True) + EPS)
        gdy = g_ref[...] * dyv
        da = r * gdy - av * ((r * r * r) * jnp.mean(av * gdy, axis=-1, keepdims=True))
        do_ref[...] = da.astype(BF16)
        dl_ref[...] = _segsum(da * av, e_ref[...])

        @pl.when(pl.program_id(0) == 0)
        def _():
            dg_ref[...] = jnp.zeros_like(dg_ref)

        dg_ref[...] += jnp.sum(dyv * (av * r), axis=0, keepdims=True)

    row = _bs((tr, AW), lambda i: (i, 0))
    vec = _bs((1, AW), lambda i: (0, 0))
    return pl.pallas_call(
        body, name="attn_out_bwd", grid=(T // tr,), in_specs=[row, row, vec, _bs((AW, AW), lambda i: (0, 0))],
        out_specs=[row, row, vec],
        out_shape=[jax.ShapeDtypeStruct((T, AW), BF16), jax.ShapeDtypeStruct((T, AW), F32),
                   jax.ShapeDtypeStruct((1, AW), F32)],
        compiler_params=_cp("arbitrary"),
    )(attn, dmix, g_attn, _head_ones())


TRR = 256


def _scan_fwd(a, u):
    n = a.shape[0]
    row = lax.broadcasted_iota(jnp.int32, (n, 1), 0)
    s = 1
    while s < n:
        keep = row >= s
        u = jnp.where(keep, a * pltpu.roll(u, s, 0) + u, u)
        a = jnp.where(keep, a * pltpu.roll(a, s, 0), a)
        s *= 2
    return a, u


def _scan_bwd(c, w):
    n = c.shape[0]
    row = lax.broadcasted_iota(jnp.int32, (n, 1), 0)
    s = 1
    while s < n:
        keep = row < n - s
        w = jnp.where(keep, c * pltpu.roll(w, n - s, 0) + w, w)
        c = jnp.where(keep, c * pltpu.roll(c, n - s, 0), c)
        s *= 2
    return w


def _gates(xc, wrg, wig, brg, big, sp):
    xcb = xc.astype(BF16)
    r = jax.nn.sigmoid(_dot(xcb, wrg) + brg)
    ig = jax.nn.sigmoid(_dot(xcb, wig) + big)
    la = (-LRU_C * r) * sp
    a = jnp.exp(la)
    mult = jnp.sqrt(-jnp.tanh(la) * (a * a + 1.0))
    return r, ig, a, mult


def _conv4(ext_ref, xr, cw_ref, cb_ref, n):
    y = cb_ref[...] + ext_ref[pl.ds(5, n), :] * cw_ref[0:1, :]
    y = y + ext_ref[pl.ds(6, n), :] * cw_ref[1:2, :]
    y = y + ext_ref[pl.ds(7, n), :] * cw_ref[2:3, :]
    return y + xr * cw_ref[3:4, :]


def _rec_fwd(proj, mix, cw, cb, wrg, wig, brg, big, lam, g_rec):
    n = TRR

    def body(xr_ref, gr_ref, cw_ref, cb_ref, wrg_ref, wig_ref, brg_ref, big_ref, lam_ref, g_ref, mix_in,
             mix_ref, h_ref, ext, hcar):
        del mix_in

        @pl.when(pl.program_id(0) == 0)
        def _():
            ext[0:8, :] = jnp.zeros((8, RW), F32)
            hcar[...] = jnp.zeros_like(hcar)

        xr = xr_ref[...]
        ext[8:, :] = xr
        xc = _conv4(ext, xr, cw_ref, cb_ref, n)
        ext[0:8, :] = xr[n - 8:, :]
        sp = _softplus(-lam_ref[...])
        _, ig, a, mult = _gates(xc, wrg_ref[...], wig_ref[...], brg_ref[...], big_ref[...], sp)
        a_s, u_s = _scan_fwd(a, mult * (ig * xc))
        h = u_s + a_s * hcar[7:8, :]
        h_ref[...] = h
        hcar[...] = h[n - 8:, :]
        pre = h * _gelu(gr_ref[...])
        r = lax.rsqrt(jnp.mean(pre * pre, axis=-1, keepdims=True) + EPS)
        mix_ref[...] = ((pre * r) * g_ref[...]).astype(BF16)

    vec = _bs((1, RW), lambda i: (0, 0))
    mat = _bs((RW, RW), lambda i: (0, 0))
    return pl.pallas_call(
        body, name="rec_fwd", grid=(T // n,),
        in_specs=[_bs((n, RW), lambda i: (i, 3)), _bs((n, RW), lambda i: (i, 4)), _bs((8, RW), lambda i: (0, 0)), vec,
                  mat, mat, vec, vec, vec, vec, pl.BlockSpec(memory_space=pl.ANY)],
        out_specs=[_bs((n, RW), lambda i: (i, 1)), _bs((n, RW), lambda i: (i, 0))],
        out_shape=[jax.ShapeDtypeStruct((T, D), BF16), jax.ShapeDtypeStruct((T, RW), F32)],
        scratch_shapes=[pltpu.VMEM((n + 8, RW), F32), pltpu.VMEM((8, RW), F32)],
        input_output_aliases={10: 0}, compiler_params=_cp("arbitrary"),
    )(proj, proj, cw, cb, wrg, wig, brg, big, lam, g_rec, mix)


def _rec_bwd(proj, h, dmix, dproj, cw, cb, wrg, wig, brg, big, lam, g_rec):
    n = TRR
    nt = T // n
    hb = n // 8

    def body(xr_ref, xh_ref, gr_ref, h_ref, hh_ref, dm_ref, cw_ref, cb_ref, wrg_ref, wig_ref, brg_ref, big_ref,
             lam_ref, g_ref, dp_in, dp_ref, xc_ref, dr_ref, di_ref, dcw_ref, dcb_ref, dbr_ref, dbi_ref, dsp_ref,
             dg_ref, ext, exth, extd, adh, dgr_s):
        del dp_in
        i, j = pl.program_id(0), pl.program_id(1)
        first_tile = i == nt - 1
        last_tile = i == 0

        @pl.when(j == 0)
        def _():
            @pl.when(last_tile)
            def _():
                for ref in (dcw_ref, dcb_ref, dbr_ref, dbi_ref, dsp_ref, dg_ref):
                    ref[...] = jnp.zeros_like(ref)
                extd[n:, :] = jnp.zeros((8, RW), F32)
                adh[...] = jnp.zeros_like(adh)

            row = lax.broadcasted_iota(jnp.int32, (n, 1), 0)
            xr = xr_ref[...]
            ext[0:8, :] = jnp.where(first_tile, 0.0, xh_ref[...])
            ext[8:, :] = xr
            xc = _conv4(ext, xr, cw_ref, cb_ref, n)
            sp = _softplus(-lam_ref[...])
            wrg, wig = wrg_ref[...], wig_ref[...]
            r, ig, a, mult = _gates(xc, wrg, wig, brg_ref[...], big_ref[...], sp)

            hv = h_ref[...]
            gl, dgl = _gelu_and_grad(gr_ref[...])
            pre = hv * gl
            dyv = dm_ref[...]
            rr = lax.rsqrt(jnp.mean(pre * pre, axis=-1, keepdims=True) + EPS)
            gdy = g_ref[...] * dyv
            dpre = rr * gdy - pre * ((rr * rr * rr) * jnp.mean(pre * gdy, axis=-1, keepdims=True))
            dg_ref[...] += jnp.sum(dyv * (pre * rr), axis=0, keepdims=True)
            dgr_s[...] = dpre * hv * dgl

            is_last_row = row == n - 1
            w = dpre * gl + jnp.where(is_last_row, adh[0:1, :], 0.0)
            c = jnp.where(is_last_row, 0.0, pltpu.roll(a, n - 1, 0))
            dh = _scan_bwd(c, w)
            adh[...] = (a * dh)[0:8, :]

            exth[0:8, :] = jnp.where(first_tile, 0.0, hh_ref[...])
            exth[8:, :] = hv
            da = dh * exth[pl.ds(7, n), :]
            ixc = ig * xc
            dmult = dh * ixc
            dla = da * a - dmult * ((a * a) / mult)
            dsp_ref[...] += jnp.sum(dla * (-LRU_C * r), axis=0, keepdims=True)
            dpr = (dla * (-LRU_C * sp)) * (r * (1.0 - r))
            dpi = (dh * (mult * xc)) * (ig * (1.0 - ig))
            dprb, dpib = dpr.astype(BF16), dpi.astype(BF16)
            dxc = dh * (mult * ig) + _dot(dprb, wrg, NT) + _dot(dpib, wig, NT)
            dbr_ref[...] += jnp.sum(dpr, axis=0, keepdims=True)
            dbi_ref[...] += jnp.sum(dpi, axis=0, keepdims=True)
            xc_ref[...] = xc.astype(BF16)
            dr_ref[...] = dprb
            di_ref[...] = dpib

            extd[0:n, :] = dxc
            dxr = dxc * cw_ref[3:4, :] + extd[pl.ds(1, n), :] * cw_ref[2:3, :]
            dxr = dxr + extd[pl.ds(2, n), :] * cw_ref[1:2, :] + extd[pl.ds(3, n), :] * cw_ref[0:1, :]
            extd[n:, :] = dxc[0:8, :]
            dcb_ref[...] += jnp.sum(dxc, axis=0, keepdims=True)
            for kk in range(4):
                dcw_ref[kk:kk + 1, :] += jnp.sum(dxc * ext[pl.ds(5 + kk, n), :], axis=0, keepdims=True)

            @pl.when(first_tile)
            def _():
                dsp_ref[...] = dsp_ref[...] * (-jax.nn.sigmoid(-lam_ref[...]))

            dp_ref[...] = dxr.astype(BF16)

        @pl.when(j == 1)
        def _():
            dp_ref[...] = dgr_s[...].astype(BF16)

    vec = _bs((1, RW), lambda i, j: (0, 0))
    mat = _bs((RW, RW), lambda i, j: (0, 0))
    tile = lambda cblk: _bs((n, RW), lambda i, j, cblk=cblk: (nt - 1 - i, cblk))
    halo = lambda cblk: _bs((8, RW), lambda i, j, cblk=cblk: (jnp.maximum((nt - 1 - i) * hb - 1, 0), cblk))
    bt = jax.ShapeDtypeStruct((T, RW), BF16)
    v = jax.ShapeDtypeStruct((1, RW), F32)
    return pl.pallas_call(
        body, name="rec_bwd", grid=(nt, 2),
        in_specs=[tile(3), halo(3), tile(4), tile(0), halo(0), tile(1), _bs((8, RW), lambda i, j: (0, 0)), vec,
                  mat, mat, vec, vec, vec, vec, pl.BlockSpec(memory_space=pl.ANY)],
        out_specs=[_bs((n, RW), lambda i, j: (nt - 1 - i, 3 + j)), tile(0), tile(0), tile(0),
                   _bs((8, RW), lambda i, j: (0, 0)), vec, vec, vec, vec, vec],
        out_shape=[jax.ShapeDtypeStruct((T, INW), BF16), bt, bt, bt, jax.ShapeDtypeStruct((8, RW), F32), v, v, v, v, v],
        scratch_shapes=[pltpu.VMEM((n + 8, RW), F32), pltpu.VMEM((n + 8, RW), F32), pltpu.VMEM((n + 8, RW), F32),
                        pltpu.VMEM((8, RW), F32), pltpu.VMEM((n, RW), F32)],
        input_output_aliases={14: 0}, compiler_params=_cp("arbitrary", "arbitrary"),
    )(proj, proj, proj, h, h, dmix, cw, cb, wrg, wig, brg, big, lam, g_rec, dproj)


FC = 1536
TRF = 256


def _conv3(ext_ref, w_ref, b_ref, n):
    y = b_ref[...] + ext_ref[pl.ds(6, n), :] * w_ref[0:1, :]
    y = y + ext_ref[pl.ds(7, n), :] * w_ref[1:2, :]
    return y + ext_ref[pl.ds(8, n), :] * w_ref[2:3, :]


def _ffn_act(up_pre, cw, cb):
    n = TRF
    hb = n // 8

    def body(g_ref, gh_ref, u_ref, uh_ref, wg_ref, wu_ref, bg_ref, bu_ref, o_ref, extg, extu):
        first = pl.program_id(1) == 0
        extg[0:8, :] = jnp.where(first, 0.0, gh_ref[...])
        extg[8:, :] = g_ref[...]
        extu[0:8, :] = jnp.where(first, 0.0, uh_ref[...])
        extu[8:, :] = u_ref[...]
        o_ref[...] = (_gelu(_conv3(extg, wg_ref, bg_ref, n)) * _conv3(extu, wu_ref, bu_ref, n)).astype(BF16)

    main = lambda o: _bs((n, FC), lambda j, i, o=o: (i, 2 * j + o))
    halo = lambda o: _bs((8, FC), lambda j, i, o=o: (jnp.maximum(i * hb - 1, 0), 2 * j + o))
    wsp = lambda o: _bs((None, 8, FC), lambda j, i, o=o: (2 * j + o, 0, 0))
    bsp = lambda o: _bs((1, FC), lambda j, i, o=o: (0, 2 * j + o))
    return pl.pallas_call(
        body, name="ffn_act", grid=(2, T // n),
        in_specs=[main(0), halo(0), main(1), halo(1), wsp(0), wsp(1), bsp(0), bsp(1)],
        out_specs=_bs((n, FC), lambda j, i: (i, j)), out_shape=jax.ShapeDtypeStruct((T, DFF), BF16),
        scratch_shapes=[pltpu.VMEM((n + 8, FC), F32)] * 2, compiler_params=_cp("parallel", "parallel"),
    )(up_pre, up_pre, up_pre, up_pre, cw, cw, cb, cb)


def _ffn_bwd(up_pre, dact, cw, cb):
    n = TRF
    hb = n // 8
    nt = T // n
    m = n + 8

    def body(g_ref, gp_ref, gn_ref, u_ref, up_ref, un_ref, d_ref, dn_ref, wg_ref, wu_ref, bg_ref, bu_ref,
             o_ref, dw_ref, db_ref, extg, extu, extd, dug_s, duu_s):
        i = pl.program_id(1)
        first, last = i == 0, i == nt - 1

        @pl.when(first)
        def _():
            dw_ref[...] = jnp.zeros_like(dw_ref)
            db_ref[...] = jnp.zeros_like(db_ref)

        extg[0:8, :] = jnp.where(first, 0.0, gp_ref[...])
        extg[8:n + 8, :] = g_ref[...]
        extg[n + 8:, :] = gn_ref[...]
        extu[0:8, :] = jnp.where(first, 0.0, up_ref[...])
        extu[8:n + 8, :] = u_ref[...]
        extu[n + 8:, :] = un_ref[...]
        extd[0:n, :] = d_ref[...]
        extd[n:, :] = jnp.where(last, 0.0, dn_ref[...])
        gl, dgl = _gelu_and_grad(_conv3(extg, wg_ref, bg_ref, m))
        uu = _conv3(extu, wu_ref, bu_ref, m)
        dv = extd[...]
        dug_s[...] = dv * uu * dgl
        duu_s[...] = dv * gl

        def conv_t(s_ref, w_ref):
            y = s_ref[pl.ds(0, n), :] * w_ref[2:3, :] + s_ref[pl.ds(1, n), :] * w_ref[1:2, :]
            return y + s_ref[pl.ds(2, n), :] * w_ref[0:1, :]

        o_ref[:, 0:FC] = conv_t(dug_s, wg_ref).astype(BF16)
        o_ref[:, FC:] = conv_t(duu_s, wu_ref).astype(BF16)
        dg0, du0 = dug_s[pl.ds(0, n), :], duu_s[pl.ds(0, n), :]
        db_ref[:, 0:FC] += jnp.sum(dg0, axis=0, keepdims=True)
        db_ref[:, FC:] += jnp.sum(du0, axis=0, keepdims=True)
        for kk in range(3):
            dw_ref[kk:kk + 1, 0:FC] += jnp.sum(dg0 * extg[pl.ds(6 + kk, n), :], axis=0, keepdims=True)
            dw_ref[kk:kk + 1, FC:] += jnp.sum(du0 * extu[pl.ds(6 + kk, n), :], axis=0, keepdims=True)

    main = lambda o: _bs((n, FC), lambda j, i, o=o: (i, 2 * j + o))
    prev = lambda o: _bs((8, FC), lambda j, i, o=o: (jnp.maximum(i * hb - 1, 0), 2 * j + o))
    nxt = lambda o: _bs((8, FC), lambda j, i, o=o: (jnp.minimum((i + 1) * hb, T // 8 - 1), 2 * j + o))
    wsp = lambda o: _bs((None, 8, FC), lambda j, i, o=o: (2 * j + o, 0, 0))
    bsp = lambda o: _bs((1, FC), lambda j, i, o=o: (0, 2 * j + o))
    return pl.pallas_call(
        body, name="ffn_bwd", grid=(2, nt),
        in_specs=[main(0), prev(0), nxt(0), main(1), prev(1), nxt(1), _bs((n, FC), lambda j, i: (i, j)),
                  _bs((8, FC), lambda j, i: (jnp.minimum((i + 1) * hb, T // 8 - 1), j)), wsp(0), wsp(1), bsp(0), bsp(1)],
        out_specs=[_bs((n, 2 * FC), lambda j, i: (i, j)), _bs((8, 2 * FC), lambda j, i: (0, j)),
                   _bs((1, 2 * FC), lambda j, i: (0, j))],
        out_shape=[jax.ShapeDtypeStruct((T, 2 * DFF), BF16), jax.ShapeDtypeStruct((8, 2 * DFF), F32),
                   jax.ShapeDtypeStruct((1, 2 * DFF), F32)],
        scratch_shapes=[pltpu.VMEM((n + 16, FC), F32), pltpu.VMEM((n + 16, FC), F32), pltpu.VMEM((m, FC), F32),
                        pltpu.VMEM((m, FC), F32), pltpu.VMEM((m, FC), F32)],
        compiler_params=_cp("parallel", "arbitrary"),
    )(up_pre, up_pre, up_pre, up_pre, up_pre, up_pre, dact, dact, cw, cw, cb, cb)


def _down_loss(act, w_down, x1, target):
    tm, tn = 512, 512

    def body(a_ref, b_ref, r_ref, t_ref, dy_ref, dyb_ref, l_ref):
        @pl.when((pl.program_id(0) == 0) & (pl.program_id(1) == 0))
        def _():
            l_ref[...] = jnp.zeros_like(l_ref)

        err = (r_ref[...] + _dot(a_ref[...], b_ref[...])) - t_ref[...]
        dy = err * (1.0 / D)
        dy_ref[...] = dy
        dyb_ref[...] = dy.astype(BF16)
        l_ref[...] += jnp.sum(0.5 * (err * err) * (1.0 / D))

    o_spec = _bs((tm, tn), lambda j, i: (i, j))
    return pl.pallas_call(
        body, name="down_loss", grid=(D // tn, T // tm),
        in_specs=[_bs((tm, DFF), lambda j, i: (i, 0)), _bs((DFF, tn), lambda j, i: (0, j)), o_spec, o_spec],
        out_specs=[o_spec, o_spec, _bs((8, 128), lambda j, i: (0, 0))],
        out_shape=[jax.ShapeDtypeStruct((T, D), F32), jax.ShapeDtypeStruct((T, D), BF16),
                   jax.ShapeDtypeStruct((8, 128), F32)],
        compiler_params=_cp("arbitrary", "arbitrary"),
    )(act, w_down, x1, target)


def _block_diag(w):
    eye = jnp.eye(8, dtype=w.dtype)
    return (w[:, :, None, :] * eye[:, None, :, None]).reshape(RW, RW).astype(BF16)


def _diag_blocks(m):
    return jnp.stack([m[HD * b:HD * (b + 1), HD * b:HD * (b + 1)] for b in range(8)])


def _local_step(x, pos_col, target, p, exch):
    qg, kg = jnp.tile(p["q_norm_g"], (1, 8)), jnp.tile(p["k_norm_g"], (1, 8))
    wrg, wig = _block_diag(p["w_rg"]), _block_diag(p["w_ig"])
    brg, big = p["b_rg"].reshape(1, RW), p["b_ig"].reshape(1, RW)

    h1 = _rms_fwd("rms1", x, p["g_mix"])
    proj = _mm("mm_in", h1, p["w_in"], "nn", 512, 640, stack=NCHIP, after=exch.start_rest())
    q, k, v = _qk_prep(proj, pos_col, qg, kg)
    qs, ks, vs = ([_regroup(a, d) for d in DILATIONS] for a in (q, k, v))
    os_, ls_ = [], []
    for bi, d in enumerate(DILATIONS):
        o, l = _attn_fwd(f"attn_fwd{d}", qs[bi], ks[bi], vs[bi], d)
        os_.append(_ungroup(o, d))
        ls_.append(_ungroup(l, d))
    attn, lse, mix = _attn_merge(os_, ls_, p["g_attn_out"])
    mix, hseq = _rec_fwd(proj, mix, p["rec_conv_w"], p["rec_conv_b"], wrg, wig, brg, big, p["lru_lambda"], p["g_rec_out"])
    rest = exch.wait_rest(mix)
    x1 = _mm("mm_out", mix, rest["w_out"], "nn", 512, 512, res=x)
    h2 = _rms_fwd("rms2", x1, p["g_ffn"])
    up_pre = _mm("mm_up", h2, rest["w_up"], "nn", 512, 768, stack=NCHIP)
    act = _ffn_act(up_pre, p["ffn_conv_w"], p["ffn_conv_b"])
    dy, dyb, loss_blk = _down_loss(act, rest["w_down"], x1, target)

    g = {}
    tok = exch.reduce_start("w_down", *_mm("wg_down", act, dyb, "tn", 512, 512, twin_bf16=True))
    dact = _mm("dg_down", dyb, rest["w_down"], "nt", 512, 512, after=tok)
    dup, g["ffn_conv_w"], g["ffn_conv_b"] = _ffn_bwd(up_pre, dact, p["ffn_conv_w"], p["ffn_conv_b"])
    tok = exch.reduce_start("w_up", *_mm("wg_up", h2, dup, "tn", 512, 768, stack=NCHIP, twin_bf16=True))
    dh2 = _mm("dg_up", dup, rest["w_up"], "nt", 512, 256, stack=NCHIP, after=tok)
    dx1, dx1b, g["g_ffn"] = _rms_bwd("rms2_bwd", x1, p["g_ffn"], dh2, dy, True)
    tok = exch.reduce_start("w_out", *_mm("wg_out", mix, dx1b, "tn", 512, 512, twin_bf16=True))
    dmix = _mm("dg_out", dx1b, rest["w_out"], "nt", 512, 512, after=tok)
    do, delta, g["g_attn_out"] = _attn_out_bwd(attn, dmix, p["g_attn_out"])
    dqs, dks, dvs = [], [], []
    for bi, d in enumerate(DILATIONS):
        dq, dk, dv = _attn_bwd(f"attn_bwd{d}", qs[bi], ks[bi], vs[bi], _regroup(do, d), _regroup(lse, d),
                               _regroup(delta, d), d)
        dqs.append(_ungroup(dq, d))
        dks.append(_ungroup(dk, d))
        dvs.append(_ungroup(dv, d))
    dproj, dqg, dkg = _qk_bwd(proj, pos_col, qg, kg, dqs, dks, dvs)
    (dproj, xcb, dprb, dpib, g["rec_conv_w"], g["rec_conv_b"], dbr, dbi, dsp, g["g_rec_out"]) = _rec_bwd(
        proj, hseq, dmix, dproj, p["rec_conv_w"], p["rec_conv_b"], wrg, wig, brg, big, p["lru_lambda"], p["g_rec_out"])
    g["w_rg"] = _diag_blocks(_mm("wg_rg", xcb, dprb, "tn", 512, 512)).reshape(RW, HD)
    g["w_ig"] = _diag_blocks(_mm("wg_ig", xcb, dpib, "tn", 512, 512)).reshape(RW, HD)
    g["b_rg"], g["b_ig"] = dbr.reshape(8, HD), dbi.reshape(8, HD)
    g["lru_lambda"] = dsp
    g["q_norm_g"] = dqg.reshape(8, HD).sum(axis=0, keepdims=True)
    g["k_norm_g"] = dkg.reshape(8, HD).sum(axis=0, keepdims=True)
    tok = exch.reduce_start("w_in", *_mm("wg_in", h1, dproj, "tn", 512, 640, stack=NCHIP, twin_bf16=True))
    dh1 = _mm("dg_in", dproj, p["w_in"], "nt", 512, 512, stack=NCHIP, after=tok)
    grad_x, g["g_mix"] = _rms_bwd("rms1_bwd", x, p["g_mix"], dh1, dx1, False)
    return loss_blk, grad_x, g


ANY = pl.BlockSpec(memory_space=pl.ANY)


def _mesh_pos():
    return lax.axis_index("x"), lax.axis_index("y"), lax.axis_index("c")


def _slot(px, py, perm):
    return 2 * py + px if perm else 2 * px + py


def _other_chips(x, y):
    return [(1 - x, y), (x, 1 - y), (1 - x, 1 - y)]


def _rcopy(src, dst, send, recv, k, to, kr=None):
    return pltpu.make_async_remote_copy(src_ref=src, dst_ref=dst, send_sem=send.at[k],
                                        recv_sem=recv.at[k if kr is None else kr], device_id=to, device_id_type=MESH)


def _cast_bf16(name, w):
    r, c = w.shape
    tr = 128
    def body(w_ref, o_ref):
        o_ref[...] = w_ref[...].astype(BF16)
    return pl.pallas_call(
        body, name=name, grid=(r // tr,), in_specs=[_bs((tr, c), lambda i: (i, 0))],
        out_specs=_bs((tr, c), lambda i: (i, 0)), out_shape=jax.ShapeDtypeStruct((r, c), BF16),
        compiler_params=_cp("parallel"),
    )(w)


def _gather_weights(big, small):
    nb, ns = len(big), len(small)
    perms = [p for _, p in big] + [p for _, p in small]

    def body(*refs):
        ins, outs = refs[:nb + ns], refs[nb + ns:2 * (nb + ns)]
        send, recv, lsem = refs[2 * (nb + ns):]
        x, y, c = _mesh_pos()
        me, sib = (x, y, c), (x, y, 1 - c)
        chips = _other_chips(x, y)
        local = [pltpu.make_async_copy(ins[a], outs[a].at[_slot(x, y, perms[a])], lsem.at[a]) for a in range(nb + ns)]
        for cp in local:
            cp.start()
        first = []
        for a in range(nb):
            for j, (px, py) in enumerate(chips):
                first.append(_rcopy(ins[a].at[c], outs[a].at[_slot(x, y, perms[a]), c], send, recv, 3 * a + j, (px, py, c)))
        for t in range(ns):
            a = nb + t
            for j, (px, py) in enumerate(chips):
                first.append(_rcopy(ins[a], outs[a].at[_slot(x, y, perms[a])], send, recv, 6 * nb + 3 * t + j, (px, py, c)))
        for cp in first:
            cp.start()
        passed = []
        for a in range(nb):
            for j, (px, py) in enumerate(chips):
                got = outs[a].at[_slot(px, py, perms[a]), c]
                _rcopy(got, got, send, recv, 3 * a + j, me).wait_recv()
                fwd = _rcopy(got, got, send, recv, 3 * nb + 3 * a + j, sib)
                fwd.start()
                passed.append(fwd)
        for a in range(nb):
            for j, (px, py) in enumerate(chips):
                got = outs[a].at[_slot(px, py, perms[a]), 1 - c]
                _rcopy(got, got, send, recv, 3 * nb + 3 * a + j, me).wait_recv()
        for t in range(ns):
            a = nb + t
            for j, (px, py) in enumerate(chips):
                got = outs[a].at[_slot(px, py, perms[a])]
                _rcopy(got, got, send, recv, 6 * nb + 3 * t + j, me).wait_recv()
        for cp in first + passed:
            cp.wait_send()
        for cp in local:
            cp.wait()

    arrs = [a for a, _ in big] + [a for a, _ in small]
    nsem = 6 * nb + 3 * ns
    return pl.pallas_call(
        body, name="gather_weights", in_specs=[ANY] * (nb + ns), out_specs=[ANY] * (nb + ns),
        out_shape=[jax.ShapeDtypeStruct((NCHIP,) + a.shape, a.dtype) for a in arrs],
        scratch_shapes=[pltpu.SemaphoreType.DMA((nsem,)), pltpu.SemaphoreType.DMA((nsem,)),
                        pltpu.SemaphoreType.DMA((nb + ns,))],
    )(*arrs)


HBM = pl.BlockSpec(memory_space=pltpu.HBM)
SEM = pl.BlockSpec(memory_space=pltpu.SEMAPHORE)
EFFECT = pltpu.SideEffectType.DATAFLOW_SIDE_EFFECTING


def _split_start(name, srcs, lands, plan, nsem):
    ns, nl = len(srcs), len(lands)

    def body(*refs):
        send, recv = refs[ns + nl], refs[ns + nl + 1]
        sends, _ = plan(refs[:ns], refs[ns:ns + nl], send, recv)
        for cp in sends:
            cp.start()
        refs[-1][...] = jnp.zeros((8, 128), F32)

    arrs = list(srcs) + list(lands)
    out = pl.pallas_call(
        body, name=name, in_specs=[HBM] * (ns + nl),
        out_specs=[SEM, SEM] + [HBM] * (ns + nl) + [pl.BlockSpec(memory_space=pltpu.VMEM)],
        out_shape=[pltpu.SemaphoreType.DMA((nsem,)), pltpu.SemaphoreType.DMA((nsem,))]
        + [pltpu.HBM(a.shape, a.dtype) for a in arrs] + [jax.ShapeDtypeStruct((8, 128), F32)],
        input_output_aliases={i: 2 + i for i in range(ns + nl)},
        compiler_params=pltpu.CompilerParams(has_side_effects=EFFECT),
    )(*[pltpu.with_memory_space_constraint(a, pltpu.HBM) for a in arrs])
    return out[0], out[1], out[2:2 + ns], out[2 + ns:2 + ns + nl], out[-1]


def _split_wait(name, send, recv, srcs, lands, plan, after):
    ns, nl = len(srcs), len(lands)

    def body(*refs):
        sends, recvs = plan(refs[:ns], refs[ns:ns + nl], refs[ns + nl], refs[ns + nl + 1])
        for cp in sends:
            cp.wait_send()
        for cp in recvs:
            cp.wait_recv()

    arrs = list(srcs) + list(lands)
    out = pl.pallas_call(
        body, name=name, in_specs=[HBM] * (ns + nl) + [SEM, SEM, ANY], out_specs=[HBM] * (ns + nl),
        out_shape=[pltpu.HBM(a.shape, a.dtype) for a in arrs],
        input_output_aliases={i: i for i in range(ns + nl)},
        compiler_params=pltpu.CompilerParams(has_side_effects=EFFECT),
    )(*arrs, send, recv, after)
    return out[ns:]


def _gather_plan(perms):
    def plan(srcs, lands, send, recv):
        x, y, c = _mesh_pos()
        sends, recvs = [], []
        for a, perm in enumerate(perms):
            for j, (px, py) in enumerate(_other_chips(x, y)):
                for cc in (0, 1):
                    k = 6 * a + 2 * j + cc
                    sends.append(_rcopy(srcs[a].at[c], lands[a].at[_slot(x, y, perm), c], send, recv, k, (px, py, cc),
                                        kr=6 * a + 2 * j + c))
                    got = lands[a].at[_slot(px, py, perm), cc]
                    recvs.append(_rcopy(got, got, send, recv, k, (x, y, c)))
        return sends, recvs
    return plan


def _reduce_plan(perm):
    def plan(srcs, lands, send, recv):
        x, y, c = _mesh_pos()
        src, land = srcs[0], lands[0]
        sends = []
        for j, (px, py) in enumerate(_other_chips(x, y)):
            for hf in (0, 1):
                sends.append(_rcopy(src.at[_slot(px, py, perm), hf], land.at[2 * j + c], send, recv, 2 * j + hf,
                                    (px, py, hf), kr=2 * j + c))
        sends.append(_rcopy(src.at[_slot(x, y, perm), 1 - c], land.at[6], send, recv, 6, (x, y, 1 - c)))
        recvs = [_rcopy(land.at[i], land.at[i], send, recv, i, (x, y, c)) for i in range(7)]
        return sends, recvs
    return plan


def _sibling_share(rs):
    na = len(rs)

    def body(*refs):
        ins, outs, (send, recv, lsem) = refs[:na], refs[na:2 * na], refs[2 * na:]
        x, y, c = _mesh_pos()
        local = [pltpu.make_async_copy(ins[a], outs[a].at[c], lsem.at[a]) for a in range(na)]
        cps = [_rcopy(ins[a], outs[a].at[c], send, recv, a, (x, y, 1 - c)) for a in range(na)]
        for cp in local + cps:
            cp.start()
        for a in range(na):
            got = outs[a].at[1 - c]
            _rcopy(got, got, send, recv, a, (x, y, c)).wait_recv()
        for cp in cps:
            cp.wait_send()
        for cp in local:
            cp.wait()

    return pl.pallas_call(
        body, name="rs_share", in_specs=[ANY] * na, out_specs=[ANY] * na,
        out_shape=[jax.ShapeDtypeStruct((2,) + r.shape, F32) for r in rs],
        scratch_shapes=[pltpu.SemaphoreType.DMA((na,)), pltpu.SemaphoreType.DMA((na,)), pltpu.SemaphoreType.DMA((na,))],
    )(*rs)


def _add_pieces(name, g, got, where):
    _, _, r2, cc = g.shape
    tr = 128

    def body(w_ref, g_ref, r_ref, o_ref):
        del w_ref
        acc = g_ref[...]
        for i in range(7):
            acc = acc + r_ref[i].astype(F32)
        o_ref[...] = acc

    return pl.pallas_call(
        body, name=name,
        grid_spec=pltpu.PrefetchScalarGridSpec(
            num_scalar_prefetch=1, grid=(r2 // tr,),
            in_specs=[_bs((None, None, tr, cc), lambda i, w_ref: (w_ref[0], w_ref[1], i, 0)),
                      _bs((7, tr, cc), lambda i, w_ref: (0, i, 0))],
            out_specs=_bs((tr, cc), lambda i, w_ref: (i, 0))),
        out_shape=jax.ShapeDtypeStruct((r2, cc), F32), compiler_params=_cp("parallel"),
    )(where, g, got)


def _adam_math(w, g, m, v):
    m = ADAM_B1 * m + (1.0 - ADAM_B1) * g
    v = ADAM_B2 * v + (1.0 - ADAM_B2) * (g * g)
    m_hat = m / (1.0 - ADAM_B1 ** ADAM_STEP)
    v_hat = v / (1.0 - ADAM_B2 ** ADAM_STEP)
    return -ADAM_LR * (m_hat / (jnp.sqrt(v_hat) + ADAM_EPS) + ADAM_WD * w), m, v


def _adam_big(name, w, g, m, v):
    r, c = w.shape
    tr = 128

    def body(w_ref, g_ref, m_ref, v_ref, d_ref, m2_ref, v2_ref):
        d_ref[...], m2_ref[...], v2_ref[...] = _adam_math(w_ref[...], g_ref[...], m_ref[...], v_ref[...])

    spec = _bs((tr, c), lambda i: (i, 0))
    out = jax.ShapeDtypeStruct((r, c), F32)
    return pl.pallas_call(
        body, name=name, grid=(r // tr,), in_specs=[spec] * 4, out_specs=[spec] * 3, out_shape=[out] * 3,
        compiler_params=_cp("parallel"),
    )(w, g, m, v)


_CLASS_SHAPE = {"a": (8, D), "b": (8, RW), "c": (8, 2 * DFF), "d": (1048, HD)}
_SMALL = (
    ("g_mix", "a", 0, 1, D), ("g_ffn", "a", 1, 1, D),
    ("rec_conv_w", "b", 0, 4, RW), ("rec_conv_b", "b", 4, 1, RW), ("lru_lambda", "b", 5, 1, RW),
    ("g_attn_out", "b", 6, 1, RW), ("g_rec_out", "b", 7, 1, RW),
    ("ffn_conv_w", "c", 0, 3, 2 * DFF), ("ffn_conv_b", "c", 3, 1, 2 * DFF),
    ("w_rg", "d", 0, RW, HD), ("w_ig", "d", RW, RW, HD), ("b_rg", "d", 2 * RW, 8, HD), ("b_ig", "d", 2 * RW + 8, 8, HD),
    ("q_norm_g", "d", 2 * RW + 16, 1, HD), ("k_norm_g", "d", 2 * RW + 17, 1, HD),
)
_LOSS_ROW = 2
_CLASSES = ("a", "b", "c", "d")


def _small_allreduce(g, loss_blk):
    names = [s[0] for s in _SMALL]
    nin = len(names) + 1

    def body(*refs):
        ins = dict(zip(names, refs[:len(names)]))
        loss_ref = refs[len(names)]
        outs = dict(zip(_CLASSES, refs[nin:nin + 4]))
        pair = dict(zip(_CLASSES, refs[nin + 4:nin + 8]))
        quad = dict(zip(_CLASSES, refs[nin + 8:nin + 12]))
        send, recv = refs[nin + 12:]
        x, y, c = _mesh_pos()
        chip = 2 * x + y
        pair["a"][c] = jnp.zeros(_CLASS_SHAPE["a"], F32)
        pair["b"][c] = ins["rec_conv_w"][...]
        pair["c"][c] = ins["ffn_conv_w"][...]
        pair["d"][c, 2 * RW + 16:, :] = jnp.zeros((8, HD), F32)
        for name, k, r0, nr, _ in _SMALL:
            if name in ("rec_conv_w", "ffn_conv_w"):
                continue
            pair[k][c, r0:r0 + nr, :] = ins[name][...]
        pair["a"][c, _LOSS_ROW:_LOSS_ROW + 1, :] = jnp.broadcast_to(loss_ref[0:1, 0:1], (1, D))
        cps = [_rcopy(pair[k].at[c], pair[k].at[c], send, recv, ki, (x, y, 1 - c)) for ki, k in enumerate(_CLASSES)]
        for cp in cps:
            cp.start()
        for ki, k in enumerate(_CLASSES):
            _rcopy(pair[k].at[1 - c], pair[k].at[1 - c], send, recv, ki, (x, y, c)).wait_recv()
            quad[k][chip] = pair[k][0] + pair[k][1]
        cps2 = []
        for ki, k in enumerate(_CLASSES):
            for j, (px, py) in enumerate(_other_chips(x, y)):
                cps2.append(_rcopy(quad[k].at[chip], quad[k].at[chip], send, recv, 4 + 3 * ki + j, (px, py, c)))
        for cp in cps2:
            cp.start()
        for ki, k in enumerate(_CLASSES):
            for j, (px, py) in enumerate(_other_chips(x, y)):
                got = quad[k].at[2 * px + py]
                _rcopy(got, got, send, recv, 4 + 3 * ki + j, (x, y, c)).wait_recv()
            outs[k][...] = ((quad[k][0] + quad[k][1]) + quad[k][2]) + quad[k][3]
        for cp in cps + cps2:
            cp.wait_send()

    vm = pl.BlockSpec(memory_space=pltpu.VMEM)
    return pl.pallas_call(
        body, name="small_allreduce", in_specs=[vm] * nin, out_specs=[vm] * 4,
        out_shape=[jax.ShapeDtypeStruct(_CLASS_SHAPE[k], F32) for k in _CLASSES],
        scratch_shapes=[pltpu.VMEM((2,) + _CLASS_SHAPE[k], F32) for k in _CLASSES]
        + [pltpu.VMEM((NCHIP,) + _CLASS_SHAPE[k], F32) for k in _CLASSES]
        + [pltpu.SemaphoreType.DMA((16,)), pltpu.SemaphoreType.DMA((16,))],
        compiler_params=pltpu.CompilerParams(vmem_limit_bytes=VMEM_LIMIT),
    )(*[g[n] for n in names], loss_blk)


def _adam_small(red, w, m, v):
    names = [s[0] for s in _SMALL]
    n = len(names)

    def body(*refs):
        red_refs = dict(zip(_CLASSES, refs[:4]))
        w_refs, m_refs, v_refs = refs[4:4 + n], refs[4 + n:4 + 2 * n], refs[4 + 2 * n:4 + 3 * n]
        loss_ref = refs[4 + 3 * n]
        out_refs = refs[5 + 3 * n:]
        x, y, _ = _mesh_pos()
        chip = 2 * x + y
        loss_ref[...] = jnp.broadcast_to(red_refs["a"][_LOSS_ROW:_LOSS_ROW + 1, 0:1], loss_ref.shape)
        for pi, (name, k, r0, nr, width) in enumerate(_SMALL):
            gfull = red_refs[k][r0:r0 + nr, :]
            if name == "rec_conv_w":
                parts = [gfull[:, 128 * s:128 * (s + 1)] for s in range(NCHIP)]
                g = jnp.where(chip == 0, parts[0], jnp.where(chip == 1, parts[1], jnp.where(chip == 2, parts[2], parts[3])))
            elif name == "ffn_conv_w":
                parts = [gfull[:, FC * s:FC * (s + 1)] for s in range(NCHIP)]
                g = jnp.where(chip == 0, parts[0], jnp.where(chip == 1, parts[2], jnp.where(chip == 2, parts[1], parts[3])))
            elif name == "ffn_conv_b":
                g = jnp.concatenate([gfull[:, FC * s:FC * (s + 1)] for s in (0, 2, 1, 3)], axis=1)
            else:
                g = gfull
            d, m2, v2 = _adam_math(w_refs[pi][...], g, m_refs[pi][...], v_refs[pi][...])
            o = out_refs[4 * pi:4 * pi + 4]
            o[0][...], o[1][...], o[2][...], o[3][...] = g, d, m2, v2

    vm = pl.BlockSpec(memory_space=pltpu.VMEM)
    outs = [jax.ShapeDtypeStruct((1, 128), F32)]
    for name in names:
        outs += [jax.ShapeDtypeStruct(w[name].shape, F32)] * 4
    res = pl.pallas_call(
        body, name="adam_small", in_specs=[vm] * (4 + 3 * n), out_specs=[vm] * len(outs), out_shape=outs,
        compiler_params=pltpu.CompilerParams(vmem_limit_bytes=VMEM_LIMIT),
    )(*red, *[w[k] for k in names], *[m[k] for k in names], *[v[k] for k in names])
    return res[0], {name: res[1 + 4 * i:5 + 4 * i] for i, name in enumerate(names)}


_WEIGHTS = ("g_mix", "w_in", "q_norm_g", "k_norm_g", "rec_conv_w", "rec_conv_b", "w_rg", "b_rg", "w_ig", "b_ig",
            "lru_lambda", "g_attn_out", "g_rec_out", "w_out", "g_ffn", "w_up", "ffn_conv_w", "ffn_conv_b", "w_down")
_BIG = ("w_in", "w_out", "w_up", "w_down")
_BIG_PERM = {"w_in": False, "w_out": False, "w_up": True, "w_down": False}
_SMALL_2D = {"w_rg": (RW, HD), "w_ig": (RW, HD), "b_rg": (8, HD), "b_ig": (8, HD), "rec_conv_w": (4, 128),
             "ffn_conv_w": (3, FC)}


def _halves(a):
    r, c = a.shape
    return a.reshape(2, r // 2, c)


def kernel(x, positions, g_mix, w_in, q_norm_g, k_norm_g, rec_conv_w, rec_conv_b, w_rg, b_rg, w_ig, b_ig, lru_lambda, g_attn_out, g_rec_out, w_out, g_ffn, w_up, ffn_conv_w, ffn_conv_b, w_down, loss_target, m_g_mix, m_w_in, m_q_norm_g, m_k_norm_g, m_rec_conv_w, m_rec_conv_b, m_w_rg, m_b_rg, m_w_ig, m_b_ig, m_lru_lambda, m_g_attn_out, m_g_rec_out, m_w_out, m_g_ffn, m_w_up, m_ffn_conv_w, m_ffn_conv_b, m_w_down, v_g_mix, v_w_in, v_q_norm_g, v_k_norm_g, v_rec_conv_w, v_rec_conv_b, v_w_rg, v_b_rg, v_w_ig, v_b_ig, v_lru_lambda, v_g_attn_out, v_g_rec_out, v_w_out, v_g_ffn, v_w_up, v_ffn_conv_w, v_ffn_conv_b, v_w_down):
    given = dict(g_mix=g_mix, w_in=w_in, q_norm_g=q_norm_g, k_norm_g=k_norm_g, rec_conv_w=rec_conv_w, rec_conv_b=rec_conv_b, w_rg=w_rg, b_rg=b_rg, w_ig=w_ig, b_ig=b_ig, lru_lambda=lru_lambda, g_attn_out=g_attn_out, g_rec_out=g_rec_out, w_out=w_out, g_ffn=g_ffn, w_up=w_up, ffn_conv_w=ffn_conv_w, ffn_conv_b=ffn_conv_b, w_down=w_down)
    given_m = dict(g_mix=m_g_mix, w_in=m_w_in, q_norm_g=m_q_norm_g, k_norm_g=m_k_norm_g, rec_conv_w=m_rec_conv_w, rec_conv_b=m_rec_conv_b, w_rg=m_w_rg, b_rg=m_b_rg, w_ig=m_w_ig, b_ig=m_b_ig, lru_lambda=m_lru_lambda, g_attn_out=m_g_attn_out, g_rec_out=m_g_rec_out, w_out=m_w_out, g_ffn=m_g_ffn, w_up=m_w_up, ffn_conv_w=m_ffn_conv_w, ffn_conv_b=m_ffn_conv_b, w_down=m_w_down)
    given_v = dict(g_mix=v_g_mix, w_in=v_w_in, q_norm_g=v_q_norm_g, k_norm_g=v_k_norm_g, rec_conv_w=v_rec_conv_w, rec_conv_b=v_rec_conv_b, w_rg=v_w_rg, b_rg=v_b_rg, w_ig=v_w_ig, b_ig=v_b_ig, lru_lambda=v_lru_lambda, g_attn_out=v_g_attn_out, g_rec_out=v_g_rec_out, w_out=v_w_out, g_ffn=v_g_ffn, w_up=v_w_up, ffn_conv_w=v_ffn_conv_w, ffn_conv_b=v_ffn_conv_b, w_down=v_w_down)
    shapes = {n: a.shape for n, a in given.items()}

    def two_d(n, a):
        a = a[0]
        return a.reshape(_SMALL_2D[n]) if n in _SMALL_2D else (a if a.ndim == 2 else a[None])

    w = {n: two_d(n, a) for n, a in given.items()}
    m = {n: two_d(n, a) for n, a in given_m.items()}
    v = {n: two_d(n, a) for n, a in given_v.items()}
    cc = lax.axis_index("c").astype(jnp.int32)
    cx, cy = lax.axis_index("x").astype(jnp.int32), lax.axis_index("y").astype(jnp.int32)
    slot = {False: 2 * cx + cy, True: 2 * cy + cx}

    shards = {n: _halves(_cast_bf16(f"cast_{n}", w[n])) for n in _BIG}
    small = [(jnp.pad(w["ffn_conv_w"], ((0, 5), (0, 0))), True), (jnp.pad(w["rec_conv_w"], ((0, 4), (0, 0))), False)]
    f_in, f_fcw, f_rcw = _gather_weights([(shards["w_in"], False)], small)
    p = {n: w[n] for n in ("g_mix", "g_ffn", "q_norm_g", "k_norm_g", "rec_conv_b", "lru_lambda", "g_attn_out", "g_rec_out")}
    p.update(w_rg=w["w_rg"].reshape(8, HD, HD), w_ig=w["w_ig"].reshape(8, HD, HD), b_rg=w["b_rg"], b_ig=w["b_ig"],
             w_in=f_in.reshape(NCHIP, D, INW // NCHIP), ffn_conv_w=f_fcw,
             ffn_conv_b=jnp.concatenate([w["ffn_conv_b"][:, FC * s:FC * (s + 1)] for s in (0, 2, 1, 3)], axis=1),
             rec_conv_w=f_rcw.transpose(1, 0, 2).reshape(8, RW))

    class Exchange:
        rest = ("w_out", "w_up", "w_down")
        order = []
        flight = {}

        def start_rest(self):
            srcs = [shards[n] for n in self.rest]
            lands = [lax.dynamic_update_slice(lax.empty((NCHIP,) + s.shape, BF16), s[None], (slot[_BIG_PERM[n]], 0, 0, 0))
                     for n, s in zip(self.rest, srcs)]
            plan = _gather_plan([_BIG_PERM[n] for n in self.rest])
            send, recv, srcs, lands, token = _split_start("gather_rest_start", srcs, lands, plan, 6 * len(srcs))
            self.flight["rest"] = (send, recv, srcs, lands, plan)
            return (token,)

        def wait_rest(self, after):
            send, recv, srcs, lands, plan = self.flight.pop("rest")
            f_out, f_up, f_down = _split_wait("gather_rest_wait", send, recv, srcs, lands, plan, after)
            return dict(w_out=f_out.reshape(D, D), w_up=f_up.reshape(NCHIP, D, FC), w_down=f_down.reshape(DFF, D))

        def reduce_start(self, name, g32, g16):
            r2, cols = shards[name].shape[1:]
            plan = _reduce_plan(_BIG_PERM[name])
            send, recv, srcs, lands, token = _split_start(
                f"reduce_{name}_start", [g16.reshape(NCHIP, 2, r2, cols)], [lax.empty((7, r2, cols), BF16)], plan, 7)
            self.flight[name] = (send, recv, srcs, lands, plan, g32.reshape(NCHIP, 2, r2, cols))
            self.order.append(name)
            return (token,)

        def finish(self, after):
            mine = {}
            for name in self.order:
                send, recv, srcs, lands, plan, g32 = self.flight.pop(name)
                (got,) = _split_wait(f"reduce_{name}_wait", send, recv, srcs, lands, plan, after)
                where = jnp.stack([slot[_BIG_PERM[name]], cc])
                mine[name] = after = _add_pieces(f"reduce_{name}_add", g32, got, where)
            return dict(zip(_BIG, _sibling_share([mine[n] for n in _BIG])))

    exch = Exchange()

    loss_blk, grad_x, g = _local_step(x[0], positions.reshape(T, 1), loss_target[0], p, exch)

    out_g, out_d, out_m, out_v = {}, {}, {}, {}
    red = _small_allreduce(g, loss_blk)
    loss_row, small_out = _adam_small(red, w, m, v)
    for n, (gn, dn, mn, vn) in small_out.items():
        out_g[n], out_d[n], out_m[n], out_v[n] = gn, dn, mn, vn

    reduced = exch.finish(red[0])
    for n in _BIG:
        gn = reduced[n].reshape(w[n].shape)
        out_g[n] = gn
        out_d[n], out_m[n], out_v[n] = _adam_big(f"adam_{n}", w[n], gn, m[n], v[n])

    outs = [loss_row[0, 0], grad_x[None]]
    for group in (out_g, out_d, out_m, out_v):
        outs += [group[n].reshape(shapes[n]) for n in _WEIGHTS]
    return tuple(outs)
```

```python
import math

import jax
import jax.numpy as jnp
import numpy as np
from jax import lax
from jax.experimental import pallas as pl
from jax.experimental.pallas import tpu as pltpu

F32 = jnp.float32
BF16 = jnp.bfloat16

T = 4096
D = 1024
HD = 64
AW = 512
RW = 512
INW = 2560
DFF = 3072
NCHIP = 4
EPS = 1e-6
NEG = -1e30
LRU_C = 8.0
ROPE_THETA = 10000.0
BLK = 128
DILATIONS = (1, 4, 16)
ADAM_LR, ADAM_B1, ADAM_B2, ADAM_EPS, ADAM_WD, ADAM_STEP = 0.001, 0.9, 0.999, 1e-08, 0.01, 10
VMEM_LIMIT = 56 * 1024 * 1024
MESH = pl.DeviceIdType.MESH

NN = (((1,), (0,)), ((), ()))
NT = (((1,), (1,)), ((), ()))
TN = (((0,), (0,)), ((), ()))


def _cp(*sem):
    return pltpu.CompilerParams(dimension_semantics=sem, vmem_limit_bytes=VMEM_LIMIT)


def _bs(shape, fn):
    return pl.BlockSpec(shape, fn)


def _dot(a, b, dims=NN):
    return lax.dot_general(a, b, dims, preferred_element_type=F32)


_GC = math.sqrt(2.0 / math.pi)


def _gelu(x):
    return x * (0.5 * (1.0 + jnp.tanh(_GC * (x + 0.044715 * (x * x * x)))))


def _gelu_and_grad(x):
    x2 = x * x
    th = jnp.tanh(_GC * (x + 0.044715 * (x * x2)))
    cdf = 0.5 * (1.0 + th)
    dg = cdf + 0.5 * x * (1.0 - th * th) * (_GC * (1.0 + 3.0 * 0.044715 * x2))
    return x * cdf, dg


def _softplus(x):
    e = jnp.exp(-jnp.abs(x))
    u = 1.0 + e
    l1p = jnp.where(u == 1.0, e, jnp.log(u) * (e / (u - 1.0)))
    return jnp.maximum(x, 0.0) + l1p


def _segsum(z, e_bf16):
    hi = z.astype(BF16)
    lo = (z - hi.astype(F32)).astype(BF16)
    return _dot(hi, e_bf16) + _dot(lo, e_bf16)


def _mm(name, a, b, mode, tm, tn, out_dtype=F32, res=None, stack=0, twin_bf16=False, after=()):
    if mode == "nn":
        (m, k), n = a.shape, (b.shape[1] if not stack else stack * b.shape[2])
        a_spec = _bs((tm, k), lambda j, i: (i, 0))
        if stack:
            per = b.shape[2] // tn
            b_spec = _bs((None, k, tn), lambda j, i: (j // per, 0, j % per))
        else:
            b_spec = _bs((k, tn), lambda j, i: (0, j))
    elif mode == "nt":
        (m, k), n = a.shape, (b.shape[0] if not stack else b.shape[1])
        a_spec = _bs((tm, k), lambda j, i: (i, 0))
        b_spec = _bs((stack, tn, k // stack), lambda j, i: (0, j, 0)) if stack else _bs((tn, k), lambda j, i: (j, 0))
    else:
        (k, m), n = a.shape, b.shape[1]
        a_spec, b_spec = _bs((k, tm), lambda j, i: (0, i)), _bs((k, tn), lambda j, i: (0, j))
    assert m % tm == 0 and n % tn == 0
    o_spec = _bs((tm, tn), lambda j, i: (i, j))
    o_shape = (m, n)
    if mode == "tn" and stack:
        per = n // stack // tn
        o_spec = _bs((None, tm, tn), lambda j, i: (j // per, i, j % per))
        o_shape = (stack, m, n // stack)
    dims = {"nn": NN, "nt": NT, "tn": TN}[mode]

    def product(a_ref, b_ref):
        if mode == "nt" and stack:
            cs = k // stack
            acc = _dot(a_ref[:, 0:cs], b_ref[0], NT)
            for s in range(1, stack):
                acc = acc + _dot(a_ref[:, s * cs:(s + 1) * cs], b_ref[s], NT)
            return acc
        return _dot(a_ref[...], b_ref[...], dims)

    nres = 0 if res is None else 1

    def body(a_ref, b_ref, *rest):
        acc = product(a_ref, b_ref)
        if nres:
            acc = rest[0][...] + acc
        outs = rest[nres + len(after):]
        outs[0][...] = acc.astype(out_dtype)
        if twin_bf16:
            outs[1][...] = acc.astype(BF16)

    ins = (a, b) + ((res,) if nres else ()) + tuple(after)
    specs = [a_spec, b_spec] + ([o_spec] if nres else []) + [pl.BlockSpec(memory_space=pl.ANY)] * len(after)
    shapes = [jax.ShapeDtypeStruct(o_shape, out_dtype)] + ([jax.ShapeDtypeStruct(o_shape, BF16)] if twin_bf16 else [])
    out = pl.pallas_call(
        body, name=name, grid=(n // tn, m // tm), in_specs=specs, out_specs=[o_spec] * len(shapes),
        out_shape=shapes, compiler_params=_cp("parallel", "parallel"),
    )(*ins)
    return tuple(out) if twin_bf16 else out[0]


def _rms_fwd(name, x, g):
    tr = 512

    def body(x_ref, g_ref, o_ref):
        xv = x_ref[...]
        r = lax.rsqrt(jnp.mean(xv * xv, axis=-1, keepdims=True) + EPS)
        o_ref[...] = ((xv * r) * g_ref[...]).astype(BF16)

    return pl.pallas_call(
        body, name=name, grid=(T // tr,), in_specs=[_bs((tr, D), lambda i: (i, 0)), _bs((1, D), lambda i: (0, 0))],
        out_specs=_bs((tr, D), lambda i: (i, 0)), out_shape=jax.ShapeDtypeStruct((T, D), BF16),
        compiler_params=_cp("parallel"),
    )(x, g)


def _rms_bwd(name, x, g, dy, dres, want_bf16):
    tr = 256

    def body(x_ref, g_ref, dy_ref, dr_ref, dx_ref, *rest):
        dg_ref = rest[-1]
        xv, dyv = x_ref[...], dy_ref[...]
        r = lax.rsqrt(jnp.mean(xv * xv, axis=-1, keepdims=True) + EPS)
        gdy = g_ref[...] * dyv
        dx = r * gdy - xv * ((r * r * r) * jnp.mean(xv * gdy, axis=-1, keepdims=True)) + dr_ref[...]
        dx_ref[...] = dx
        if want_bf16:
            rest[0][...] = dx.astype(BF16)

        @pl.when(pl.program_id(0) == 0)
        def _():
            dg_ref[...] = jnp.zeros_like(dg_ref)

        dg_ref[...] += jnp.sum(dyv * (xv * r), axis=0, keepdims=True)

    row = _bs((tr, D), lambda i: (i, 0))
    vec = _bs((1, D), lambda i: (0, 0))
    outs = [jax.ShapeDtypeStruct((T, D), F32)] + ([jax.ShapeDtypeStruct((T, D), BF16)] if want_bf16 else [])
    return pl.pallas_call(
        body, name=name, grid=(T // tr,), in_specs=[row, vec, row, row],
        out_specs=[row] * len(outs) + [vec], out_shape=outs + [jax.ShapeDtypeStruct((1, D), F32)],
        compiler_params=_cp("arbitrary"),
    )(x, g, dy, dres)


def _head_ones():
    idx = np.arange(AW) // HD
    return jnp.asarray((idx[:, None] == idx[None, :]).astype(np.float32), dtype=BF16)


def _freq_row():
    half = HD // 2
    inv = ROPE_THETA ** (-(np.arange(half, dtype=np.float64)) / half)
    return jnp.asarray(np.tile(inv, 4)[None, :], dtype=F32)


def _rot_tables(pos_ref, f_ref):
    ang = pos_ref[...].astype(F32) * f_ref[...]
    c = jnp.tile(jnp.cos(ang), (1, 4))
    s = jnp.tile(jnp.sin(ang), (1, 4))
    lane = lax.broadcasted_iota(jnp.int32, (1, AW), 1)
    first = (lane & 32) == 0
    return c, jnp.where(first, -s, s), first


def _swap_halves(y, first):
    return jnp.where(first, pltpu.roll(y, AW - 32, 1), pltpu.roll(y, 32, 1))


def _qk_prep(proj, pos_col, qg, kg):
    tr = 512

    def body(q_ref, k_ref, v_ref, pos_ref, f_ref, qg_ref, kg_ref, e_ref, qo_ref, ko_ref, vo_ref):
        c, s_signed, first = _rot_tables(pos_ref, f_ref)
        e = e_ref[...]

        def norm_rot(xv, g, scale):
            r = lax.rsqrt(_segsum(xv * xv, e) * (1.0 / HD) + EPS)
            y = (xv * r) * g
            return (y * c + _swap_halves(y, first) * s_signed) * scale

        qo_ref[...] = norm_rot(q_ref[...], qg_ref[...], HD ** -0.5).astype(BF16)
        ko_ref[...] = norm_rot(k_ref[...], kg_ref[...], 1.0).astype(BF16)
        vo_ref[...] = v_ref[...].astype(BF16)

    col = lambda j: _bs((tr, AW), lambda i, j=j: (i, j))
    vec = _bs((1, AW), lambda i: (0, 0))
    out = jax.ShapeDtypeStruct((T, AW), BF16)
    return pl.pallas_call(
        body, name="qk_prep", grid=(T // tr,),
        in_specs=[col(0), col(1), col(2), _bs((tr, 1), lambda i: (i, 0)), _bs((1, 128), lambda i: (0, 0)), vec, vec,
                  _bs((AW, AW), lambda i: (0, 0))],
        out_specs=[col(0)] * 3, out_shape=[out] * 3, compiler_params=_cp("parallel"),
    )(proj, proj, proj, pos_col, _freq_row(), qg, kg, _head_ones())


def _qk_bwd(proj, pos_col, qg, kg, dqs, dks, dvs):
    tr = 256

    def body(q_ref, k_ref, pos_ref, f_ref, qg_ref, kg_ref, e_ref, a0, a1, a2, b0, b1, b2, c0, c1, c2,
             o_ref, dqg_ref, dkg_ref):
        i, j = pl.program_id(0), pl.program_id(1)

        @pl.when((i == 0) & (j == 0))
        def _():
            dqg_ref[...] = jnp.zeros_like(dqg_ref)
            dkg_ref[...] = jnp.zeros_like(dkg_ref)

        def norm_rot_bwd(x_ref, g_ref, dg_ref, d0, d1, d2, scale):
            c, s_signed, first = _rot_tables(pos_ref, f_ref)
            e = e_ref[...]
            dout = ((d0[...] + d1[...]) + d2[...]) * scale
            dy = dout * c + _swap_halves(dout * s_signed, first)
            xv, g = x_ref[...], g_ref[...]
            r = lax.rsqrt(_segsum(xv * xv, e) * (1.0 / HD) + EPS)
            gdy = g * dy
            dx = r * gdy - xv * ((r * r * r) * (_segsum(xv * gdy, e) * (1.0 / HD)))
            o_ref[...] = dx.astype(BF16)
            dg_ref[...] += jnp.sum(dy * (xv * r), axis=0, keepdims=True)

        @pl.when(j == 0)
        def _():
            norm_rot_bwd(q_ref, qg_ref, dqg_ref, a0, a1, a2, HD ** -0.5)

        @pl.when(j == 1)
        def _():
            norm_rot_bwd(k_ref, kg_ref, dkg_ref, b0, b1, b2, 1.0)

        @pl.when(j == 2)
        def _():
            o_ref[...] = ((c0[...] + c1[...]) + c2[...]).astype(BF16)

    col = lambda jj: _bs((tr, AW), lambda i, j, jj=jj: (i, jj))
    vec = _bs((1, AW), lambda i, j: (0, 0))
    piece = _bs((tr, AW), lambda i, j: (i, 0))
    return pl.pallas_call(
        body, name="qk_bwd", grid=(T // tr, 3),
        in_specs=[col(0), col(1), _bs((tr, 1), lambda i, j: (i, 0)), _bs((1, 128), lambda i, j: (0, 0)), vec, vec,
                  _bs((AW, AW), lambda i, j: (0, 0))] + [piece] * 9,
        out_specs=[_bs((tr, AW), lambda i, j: (i, j)), vec, vec],
        out_shape=[jax.ShapeDtypeStruct((T, INW), BF16), jax.ShapeDtypeStruct((1, AW), F32),
                   jax.ShapeDtypeStruct((1, AW), F32)],
        compiler_params=_cp("arbitrary", "arbitrary"),
    )(proj, proj, pos_col, _freq_row(), qg, kg, _head_ones(), *dqs, *dks, *dvs)


def _regroup(a, d):
    return a if d == 1 else a.reshape(T // d, d, AW).transpose(1, 0, 2).reshape(T, AW)


def _ungroup(a, d):
    return a if d == 1 else a.reshape(d, T // d, AW).transpose(1, 0, 2).reshape(T, AW)


def _band_masks():
    qi = lax.broadcasted_iota(jnp.int32, (BLK, 2 * BLK), 0)
    kj = lax.broadcasted_iota(jnp.int32, (BLK, 2 * BLK), 1)
    rel = qi - kj + BLK
    wide = (rel >= 0) & (rel <= BLK)
    qi1 = lax.broadcasted_iota(jnp.int32, (BLK, BLK), 0)
    kj1 = lax.broadcasted_iota(jnp.int32, (BLK, BLK), 1)
    return kj1 <= qi1, wide


def _attn_fwd(name, q, k, v, d):
    ln = T // d
    nb = ln // BLK

    def body(q_ref, k_ref, v_ref, o_ref, l_ref):
        first_mask, wide_mask = _band_masks()
        lane = lax.broadcasted_iota(jnp.int32, (1, 128), 1)
        h0 = lane < HD

        def head(qm, kk, vv, mask):
            s = jnp.where(mask, _dot(qm, kk, NT), NEG)
            m = jnp.max(s, axis=1, keepdims=True)
            p = jnp.exp(s - m)
            l = jnp.sum(p, axis=1, keepdims=True)
            return _dot(p.astype(BF16), vv) / l, m + jnp.log(l)

        def block(r0, k0, kn, mask):
            qv = q_ref[pl.ds(r0, BLK), :]
            kk = k_ref[pl.ds(k0, kn), :]
            vv = v_ref[pl.ds(k0, kn), :]
            zero = jnp.zeros_like(qv)
            o_a, l_a = head(jnp.where(h0, qv, zero), kk, vv, mask)
            o_b, l_b = head(jnp.where(h0, zero, qv), kk, vv, mask)
            o_ref[pl.ds(r0, BLK), :] = jnp.where(h0, o_a, o_b)
            l_ref[pl.ds(r0, BLK), :] = jnp.where(h0, l_a, l_b)

        block(0, 0, BLK, first_mask)

        def step(n, carry):
            r0 = pl.multiple_of(n * BLK, BLK)
            block(r0, pl.multiple_of(r0 - BLK, BLK), 2 * BLK, wide_mask)
            return carry

        lax.fori_loop(1, nb, step, 0)

    spec = _bs((ln, 128), lambda c, p: (c, p))
    out = jax.ShapeDtypeStruct((T, AW), F32)
    return pl.pallas_call(
        body, name=name, grid=(d, AW // 128), in_specs=[spec] * 3, out_specs=[spec] * 2, out_shape=[out] * 2,
        compiler_params=_cp("parallel", "parallel"),
    )(q, k, v)


def _attn_bwd(name, q, k, v, do, lse, delta, d):
    ln = T // d
    nb = ln // BLK

    def body(q_ref, k_ref, v_ref, do_ref, l_ref, dl_ref, dq_ref, dk_ref, dv_ref):
        first_mask, wide_mask = _band_masks()
        lane = lax.broadcasted_iota(jnp.int32, (1, 128), 1)
        h0 = lane < HD
        dk_ref[...] = jnp.zeros_like(dk_ref)
        dv_ref[...] = jnp.zeros_like(dv_ref)

        def head(qm, dom, kk, vv, lse_c, dl_c, mask):
            s = jnp.where(mask, _dot(qm, kk, NT), NEG)
            p = jnp.exp(s - lse_c)
            ds = p * (_dot(dom, vv, NT) - dl_c)
            pb, dsb = p.astype(BF16), ds.astype(BF16)
            return _dot(dsb, kk), _dot(dsb, qm, TN), _dot(pb, dom, TN)

        def block(r0, k0, kn, mask):
            qv = q_ref[pl.ds(r0, BLK), :]
            dov = do_ref[pl.ds(r0, BLK), :]
            lv = l_ref[pl.ds(r0, BLK), :]
            dlv = dl_ref[pl.ds(r0, BLK), :]
            kk = k_ref[pl.ds(k0, kn), :]
            vv = v_ref[pl.ds(k0, kn), :]
            zero = jnp.zeros_like(qv)
            dq_a, dk_a, dv_a = head(jnp.where(h0, qv, zero), jnp.where(h0, dov, zero), kk, vv,
                                    lv[:, 0:1], dlv[:, 0:1], mask)
            dq_b, dk_b, dv_b = head(jnp.where(h0, zero, qv), jnp.where(h0, zero, dov), kk, vv,
                                    lv[:, HD:HD + 1], dlv[:, HD:HD + 1], mask)
            dq_ref[pl.ds(r0, BLK), :] = jnp.where(h0, dq_a, dq_b)
            dk_ref[pl.ds(k0, kn), :] += dk_a + dk_b
            dv_ref[pl.ds(k0, kn), :] += dv_a + dv_b

        block(0, 0, BLK, first_mask)

        def step(n, carry):
            r0 = pl.multiple_of(n * BLK, BLK)
            block(r0, pl.multiple_of(r0 - BLK, BLK), 2 * BLK, wide_mask)
            return carry

        lax.fori_loop(1, nb, step, 0)

    spec = _bs((ln, 128), lambda c, p: (c, p))
    out = jax.ShapeDtypeStruct((T, AW), F32)
    return pl.pallas_call(
        body, name=name, grid=(d, AW // 128), in_specs=[spec] * 6, out_specs=[spec] * 3, out_shape=[out] * 3,
        compiler_params=_cp("parallel", "parallel"),
    )(q, k, v, do, lse, delta)


def _attn_merge(os_, ls_, g_attn):
    tr = 512

    def body(o0, o1, o2, l0, l1, l2, g_ref, a_ref, lse_ref, mix_ref):
        la, lb, lc = l0[...], l1[...], l2[...]
        m = jnp.maximum(jnp.maximum(la, lb), lc)
        ea, eb, ec = jnp.exp(la - m), jnp.exp(lb - m), jnp.exp(lc - m)
        z = (ea + eb) + ec
        attn = ((ea * o0[...] + eb * o1[...]) + ec * o2[...]) / z
        a_ref[...] = attn
        lse_ref[...] = m + jnp.log(z)
        r = lax.rsqrt(jnp.mean(attn * attn, axis=-1, keepdims=True) + EPS)
        mix_ref[...] = ((attn * r) * g_ref[...]).astype(BF16)

    row = _bs((tr, AW), lambda i: (i, 0))
    f = jax.ShapeDtypeStruct((T, AW), F32)
    return pl.pallas_call(
        body, name="attn_merge", grid=(T // tr,), in_specs=[row] * 6 + [_bs((1, AW), lambda i: (0, 0))],
        out_specs=[row, row, row], out_shape=[f, f, jax.ShapeDtypeStruct((T, D), BF16)],
        compiler_params=_cp("parallel"),
    )(*os_, *ls_, g_attn)


def _attn_out_bwd(attn, dmix, g_attn):
    tr = 256

    def body(a_ref, d_ref, g_ref, e_ref, do_ref, dl_ref, dg_ref):
        av, dyv = a_ref[...], d_ref[...]
        r = lax.rsqrt(jnp.mean(av * av, axis=-1, keepdims=True) + EPS)
        gdy = g_ref[...] * dyv
        da = r * gdy - av * ((r * r * r) * jnp.mean(av * gdy, axis=-1, keepdims=True))
        do_ref[...] = da.astype(BF16)
        dl_ref[...] = _segsum(da * av, e_ref[...])

        @pl.when(pl.program_id(0) == 0)
        def _():
            dg_ref[...] = jnp.zeros_like(dg_ref)

        dg_ref[...] += jnp.sum(dyv * (av * r), axis=0, keepdims=True)

    row = _bs((tr, AW), lambda i: (i, 0))
    vec = _bs((1, AW), lambda i: (0, 0))
    return pl.pallas_call(
        body, name="attn_out_bwd", grid=(T // tr,), in_specs=[row, row, vec, _bs((AW, AW), lambda i: (0, 0))],
        out_specs=[row, row, vec],
        out_shape=[jax.ShapeDtypeStruct((T, AW), BF16), jax.ShapeDtypeStruct((T, AW), F32),
                   jax.ShapeDtypeStruct((1, AW), F32)],
        compiler_params=_cp("arbitrary"),
    )(attn, dmix, g_attn, _head_ones())


TRR = 256


def _scan_fwd(a, u):
    n = a.shape[0]
    row = lax.broadcasted_iota(jnp.int32, (n, 1), 0)
    s = 1
    while s < n:
        keep = row >= s
        u = jnp.where(keep, a * pltpu.roll(u, s, 0) + u, u)
        a = jnp.where(keep, a * pltpu.roll(a, s, 0), a)
        s *= 2
    return a, u


def _scan_bwd(c, w):
    n = c.shape[0]
    row = lax.broadcasted_iota(jnp.int32, (n, 1), 0)
    s = 1
    while s < n:
        keep = row < n - s
        w = jnp.where(keep, c * pltpu.roll(w, n - s, 0) + w, w)
        c = jnp.where(keep, c * pltpu.roll(c, n - s, 0), c)
        s *= 2
    return w


def _gates(xc, wrg, wig, brg, big, sp):
    xcb = xc.astype(BF16)
    r = jax.nn.sigmoid(_dot(xcb, wrg) + brg)
    ig = jax.nn.sigmoid(_dot(xcb, wig) + big)
    la = (-LRU_C * r) * sp
    a = jnp.exp(la)
    mult = jnp.sqrt(-jnp.tanh(la) * (a * a + 1.0))
    return r, ig, a, mult


def _conv4(ext_ref, xr, cw_ref, cb_ref, n):
    y = cb_ref[...] + ext_ref[pl.ds(5, n), :] * cw_ref[0:1, :]
    y = y + ext_ref[pl.ds(6, n), :] * cw_ref[1:2, :]
    y = y + ext_ref[pl.ds(7, n), :] * cw_ref[2:3, :]
    return y + xr * cw_ref[3:4, :]


def _rec_fwd(proj, mix, cw, cb, wrg, wig, brg, big, lam, g_rec):
    n = TRR

    def body(xr_ref, gr_ref, cw_ref, cb_ref, wrg_ref, wig_ref, brg_ref, big_ref, lam_ref, g_ref, mix_in,
             mix_ref, h_ref, ext, hcar):
        del mix_in

        @pl.when(pl.program_id(0) == 0)
        def _():
            ext[0:8, :] = jnp.zeros((8, RW), F32)
            hcar[...] = jnp.zeros_like(hcar)

        xr = xr_ref[...]
        ext[8:, :] = xr
        xc = _conv4(ext, xr, cw_ref, cb_ref, n)
        ext[0:8, :] = xr[n - 8:, :]
        sp = _softplus(-lam_ref[...])
        _, ig, a, mult = _gates(xc, wrg_ref[...], wig_ref[...], brg_ref[...], big_ref[...], sp)
        a_s, u_s = _scan_fwd(a, mult * (ig * xc))
        h = u_s + a_s * hcar[7:8, :]
        h_ref[...] = h
        hcar[...] = h[n - 8:, :]
        pre = h * _gelu(gr_ref[...])
        r = lax.rsqrt(jnp.mean(pre * pre, axis=-1, keepdims=True) + EPS)
        mix_ref[...] = ((pre * r) * g_ref[...]).astype(BF16)

    vec = _bs((1, RW), lambda i: (0, 0))
    mat = _bs((RW, RW), lambda i: (0, 0))
    return pl.pallas_call(
        body, name="rec_fwd", grid=(T // n,),
        in_specs=[_bs((n, RW), lambda i: (i, 3)), _bs((n, RW), lambda i: (i, 4)), _bs((8, RW), lambda i: (0, 0)), vec,
                  mat, mat, vec, vec, vec, vec, pl.BlockSpec(memory_space=pl.ANY)],
        out_specs=[_bs((n, RW), lambda i: (i, 1)), _bs((n, RW), lambda i: (i, 0))],
        out_shape=[jax.ShapeDtypeStruct((T, D), BF16), jax.ShapeDtypeStruct((T, RW), F32)],
        scratch_shapes=[pltpu.VMEM((n + 8, RW), F32), pltpu.VMEM((8, RW), F32)],
        input_output_aliases={10: 0}, compiler_params=_cp("arbitrary"),
    )(proj, proj, cw, cb, wrg, wig, brg, big, lam, g_rec, mix)


def _rec_bwd(proj, h, dmix, dproj, cw, cb, wrg, wig, brg, big, lam, g_rec):
    n = TRR
    nt = T // n
    hb = n // 8

    def body(xr_ref, xh_ref, gr_ref, h_ref, hh_ref, dm_ref, cw_ref, cb_ref, wrg_ref, wig_ref, brg_ref, big_ref,
             lam_ref, g_ref, dp_in, dp_ref, xc_ref, dr_ref, di_ref, dcw_ref, dcb_ref, dbr_ref, dbi_ref, dsp_ref,
             dg_ref, ext, exth, extd, adh, dgr_s):
        del dp_in
        i, j = pl.program_id(0), pl.program_id(1)
        first_tile = i == nt - 1
        last_tile = i == 0

        @pl.when(j == 0)
        def _():
            @pl.when(last_tile)
            def _():
                for ref in (dcw_ref, dcb_ref, dbr_ref, dbi_ref, dsp_ref, dg_ref):
                    ref[...] = jnp.zeros_like(ref)
                extd[n:, :] = jnp.zeros((8, RW), F32)
                adh[...] = jnp.zeros_like(adh)

            row = lax.broadcasted_iota(jnp.int32, (n, 1), 0)
            xr = xr_ref[...]
            ext[0:8, :] = jnp.where(first_tile, 0.0, xh_ref[...])
            ext[8:, :] = xr
            xc = _conv4(ext, xr, cw_ref, cb_ref, n)
            sp = _softplus(-lam_ref[...])
            wrg, wig = wrg_ref[...], wig_ref[...]
            r, ig, a, mult = _gates(xc, wrg, wig, brg_ref[...], big_ref[...], sp)

            hv = h_ref[...]
            gl, dgl = _gelu_and_grad(gr_ref[...])
            pre = hv * gl
            dyv = dm_ref[...]
            rr = lax.rsqrt(jnp.mean(pre * pre, axis=-1, keepdims=True) + EPS)
            gdy = g_ref[...] * dyv
            dpre = rr * gdy - pre * ((rr * rr * rr) * jnp.mean(pre * gdy, axis=-1, keepdims=True))
            dg_ref[...] += jnp.sum(dyv * (pre * rr), axis=0, keepdims=True)
            dgr_s[...] = dpre * hv * dgl

            is_last_row = row == n - 1
            w = dpre * gl + jnp.where(is_last_row, adh[0:1, :], 0.0)
            c = jnp.where(is_last_row, 0.0, pltpu.roll(a, n - 1, 0))
            dh = _scan_bwd(c, w)
            adh[...] = (a * dh)[0:8, :]

            exth[0:8, :] = jnp.where(first_tile, 0.0, hh_ref[...])
            exth[8:, :] = hv
            da = dh * exth[pl.ds(7, n), :]
            ixc = ig * xc
            dmult = dh * ixc
            dla = da * a - dmult * ((a * a) / mult)
            dsp_ref[...] += jnp.sum(dla * (-LRU_C * r), axis=0, keepdims=True)
            dpr = (dla * (-LRU_C * sp)) * (r * (1.0 - r))
            dpi = (dh * (mult * xc)) * (ig * (1.0 - ig))
            dprb, dpib = dpr.astype(BF16), dpi.astype(BF16)
            dxc = dh * (mult * ig) + _dot(dprb, wrg, NT) + _dot(dpib, wig, NT)
            dbr_ref[...] += jnp.sum(dpr, axis=0, keepdims=True)
            dbi_ref[...] += jnp.sum(dpi, axis=0, keepdims=True)
            xc_ref[...] = xc.astype(BF16)
            dr_ref[...] = dprb
            di_ref[...] = dpib

            extd[0:n, :] = dxc
            dxr = dxc * cw_ref[3:4, :] + extd[pl.ds(1, n), :] * cw_ref[2:3, :]
            dxr = dxr + extd[pl.ds(2, n), :] * cw_ref[1:2, :] + extd[pl.ds(3, n), :] * cw_ref[0:1, :]
            extd[n:, :] = dxc[0:8, :]
            dcb_ref[...] += jnp.sum(dxc, axis=0, keepdims=True)
            for kk in range(4):
                dcw_ref[kk:kk + 1, :] += jnp.sum(dxc * ext[pl.ds(5 + kk, n), :], axis=0, keepdims=True)

            @pl.when(first_tile)
            def _():
                dsp_ref[...] = dsp_ref[...] * (-jax.nn.sigmoid(-lam_ref[...]))

            dp_ref[...] = dxr.astype(BF16)

        @pl.when(j == 1)
        def _():
            dp_ref[...] = dgr_s[...].astype(BF16)

    vec = _bs((1, RW), lambda i, j: (0, 0))
    mat = _bs((RW, RW), lambda i, j: (0, 0))
    tile = lambda cblk: _bs((n, RW), lambda i, j, cblk=cblk: (nt - 1 - i, cblk))
    halo = lambda cblk: _bs((8, RW), lambda i, j, cblk=cblk: (jnp.maximum((nt - 1 - i) * hb - 1, 0), cblk))
    bt = jax.ShapeDtypeStruct((T, RW), BF16)
    v = jax.ShapeDtypeStruct((1, RW), F32)
    return pl.pallas_call(
        body, name="rec_bwd", grid=(nt, 2),
        in_specs=[tile(3), halo(3), tile(4), tile(0), halo(0), tile(1), _bs((8, RW), lambda i, j: (0, 0)), vec,
                  mat, mat, vec, vec, vec, vec, pl.BlockSpec(memory_space=pl.ANY)],
        out_specs=[_bs((n, RW), lambda i, j: (nt - 1 - i, 3 + j)), tile(0), tile(0), tile(0),
                   _bs((8, RW), lambda i, j: (0, 0)), vec, vec, vec, vec, vec],
        out_shape=[jax.ShapeDtypeStruct((T, INW), BF16), bt, bt, bt, jax.ShapeDtypeStruct((8, RW), F32), v, v, v, v, v],
        scratch_shapes=[pltpu.VMEM((n + 8, RW), F32), pltpu.VMEM((n + 8, RW), F32), pltpu.VMEM((n + 8, RW), F32),
                        pltpu.VMEM((8, RW), F32), pltpu.VMEM((n, RW), F32)],
        input_output_aliases={14: 0}, compiler_params=_cp("arbitrary", "arbitrary"),
    )(proj, proj, proj, h, h, dmix, cw, cb, wrg, wig, brg, big, lam, g_rec, dproj)


FC = 1536
TRF = 256


def _conv3(ext_ref, w_ref, b_ref, n):
    y = b_ref[...] + ext_ref[pl.ds(6, n), :] * w_ref[0:1, :]
    y = y + ext_ref[pl.ds(7, n), :] * w_ref[1:2, :]
    return y + ext_ref[pl.ds(8, n), :] * w_ref[2:3, :]


def _ffn_act(up_pre, cw, cb):
    n = TRF
    hb = n // 8

    def body(g_ref, gh_ref, u_ref, uh_ref, wg_ref, wu_ref, bg_ref, bu_ref, o_ref, extg, extu):
        first = pl.program_id(1) == 0
        extg[0:8, :] = jnp.where(first, 0.0, gh_ref[...])
        extg[8:, :] = g_ref[...]
        extu[0:8, :] = jnp.where(first, 0.0, uh_ref[...])
        extu[8:, :] = u_ref[...]
        o_ref[...] = (_gelu(_conv3(extg, wg_ref, bg_ref, n)) * _conv3(extu, wu_ref, bu_ref, n)).astype(BF16)

    main = lambda o: _bs((n, FC), lambda j, i, o=o: (i, 2 * j + o))
    halo = lambda o: _bs((8, FC), lambda j, i, o=o: (jnp.maximum(i * hb - 1, 0), 2 * j + o))
    wsp = lambda o: _bs((None, 8, FC), lambda j, i, o=o: (2 * j + o, 0, 0))
    bsp = lambda o: _bs((1, FC), lambda j, i, o=o: (0, 2 * j + o))
    return pl.pallas_call(
        body, name="ffn_act", grid=(2, T // n),
        in_specs=[main(0), halo(0), main(1), halo(1), wsp(0), wsp(1), bsp(0), bsp(1)],
        out_specs=_bs((n, FC), lambda j, i: (i, j)), out_shape=jax.ShapeDtypeStruct((T, DFF), BF16),
        scratch_shapes=[pltpu.VMEM((n + 8, FC), F32)] * 2, compiler_params=_cp("parallel", "parallel"),
    )(up_pre, up_pre, up_pre, up_pre, cw, cw, cb, cb)


def _ffn_bwd(up_pre, dact, cw, cb):
    n = TRF
    hb = n // 8
    nt = T // n
    m = n + 8

    def body(g_ref, gp_ref, gn_ref, u_ref, up_ref, un_ref, d_ref, dn_ref, wg_ref, wu_ref, bg_ref, bu_ref,
             o_ref, dw_ref, db_ref, extg, extu, extd, dug_s, duu_s):
        i = pl.program_id(1)
        first, last = i == 0, i == nt - 1

        @pl.when(first)
        def _():
            dw_ref[...] = jnp.zeros_like(dw_ref)
            db_ref[...] = jnp.zeros_like(db_ref)

        extg[0:8, :] = jnp.where(first, 0.0, gp_ref[...])
        extg[8:n + 8, :] = g_ref[...]
        extg[n + 8:, :] = gn_ref[...]
        extu[0:8, :] = jnp.where(first, 0.0, up_ref[...])
        extu[8:n + 8, :] = u_ref[...]
        extu[n + 8:, :] = un_ref[...]
        extd[0:n, :] = d_ref[...]
        extd[n:, :] = jnp.where(last, 0.0, dn_ref[...])
        gl, dgl = _gelu_and_grad(_conv3(extg, wg_ref, bg_ref, m))
        uu = _conv3(extu, wu_ref, bu_ref, m)
        dv = extd[...]
        dug_s[...] = dv * uu * dgl
        duu_s[...] = dv * gl

        def conv_t(s_ref, w_ref):
            y = s_ref[pl.ds(0, n), :] * w_ref[2:3, :] + s_ref[pl.ds(1, n), :] * w_ref[1:2, :]
            return y + s_ref[pl.ds(2, n), :] * w_ref[0:1, :]

        o_ref[:, 0:FC] = conv_t(dug_s, wg_ref).astype(BF16)
        o_ref[:, FC:] = conv_t(duu_s, wu_ref).astype(BF16)
        dg0, du0 = dug_s[pl.ds(0, n), :], duu_s[pl.ds(0, n), :]
        db_ref[:, 0:FC] += jnp.sum(dg0, axis=0, keepdims=True)
        db_ref[:, FC:] += jnp.sum(du0, axis=0, keepdims=True)
        for kk in range(3):
            dw_ref[kk:kk + 1, 0:FC] += jnp.sum(dg0 * extg[pl.ds(6 + kk, n), :], axis=0, keepdims=True)
            dw_ref[kk:kk + 1, FC:] += jnp.sum(du0 * extu[pl.ds(6 + kk, n), :], axis=0, keepdims=True)

    main = lambda o: _bs((n, FC), lambda j, i, o=o: (i, 2 * j + o))
    prev = lambda o: _bs((8, FC), lambda j, i, o=o: (jnp.maximum(i * hb - 1, 0), 2 * j + o))
    nxt = lambda o: _bs((8, FC), lambda j, i, o=o: (jnp.minimum((i + 1) * hb, T // 8 - 1), 2 * j + o))
    wsp = lambda o: _bs((None, 8, FC), lambda j, i, o=o: (2 * j + o, 0, 0))
    bsp = lambda o: _bs((1, FC), lambda j, i, o=o: (0, 2 * j + o))
    return pl.pallas_call(
        body, name="ffn_bwd", grid=(2, nt),
        in_specs=[main(0), prev(0), nxt(0), main(1), prev(1), nxt(1), _bs((n, FC), lambda j, i: (i, j)),
                  _bs((8, FC), lambda j, i: (jnp.minimum((i + 1) * hb, T // 8 - 1), j)), wsp(0), wsp(1), bsp(0), bsp(1)],
        out_specs=[_bs((n, 2 * FC), lambda j, i: (i, j)), _bs((8, 2 * FC), lambda j, i: (0, j)),
                   _bs((1, 2 * FC), lambda j, i: (0, j))],
        out_shape=[jax.ShapeDtypeStruct((T, 2 * DFF), BF16), jax.ShapeDtypeStruct((8, 2 * DFF), F32),
                   jax.ShapeDtypeStruct((1, 2 * DFF), F32)],
        scratch_shapes=[pltpu.VMEM((n + 16, FC), F32), pltpu.VMEM((n + 16, FC), F32), pltpu.VMEM((m, FC), F32),
                        pltpu.VMEM((m, FC), F32), pltpu.VMEM((m, FC), F32)],
        compiler_params=_cp("parallel", "arbitrary"),
    )(up_pre, up_pre, up_pre, up_pre, up_pre, up_pre, dact, dact, cw, cw, cb, cb)


def _down_loss(act, w_down, x1, target):
    tm, tn = 512, 512

    def body(a_ref, b_ref, r_ref, t_ref, dy_ref, dyb_ref, l_ref):
        @pl.when((pl.program_id(0) == 0) & (pl.program_id(1) == 0))
        def _():
            l_ref[...] = jnp.zeros_like(l_ref)

        err = (r_ref[...] + _dot(a_ref[...], b_ref[...])) - t_ref[...]
        dy = err * (1.0 / D)
        dy_ref[...] = dy
        dyb_ref[...] = dy.astype(BF16)
        l_ref[...] += jnp.sum(0.5 * (err * err) * (1.0 / D))

    o_spec = _bs((tm, tn), lambda j, i: (i, j))
    return pl.pallas_call(
        body, name="down_loss", grid=(D // tn, T // tm),
        in_specs=[_bs((tm, DFF), lambda j, i: (i, 0)), _bs((DFF, tn), lambda j, i: (0, j)), o_spec, o_spec],
        out_specs=[o_spec, o_spec, _bs((8, 128), lambda j, i: (0, 0))],
        out_shape=[jax.ShapeDtypeStruct((T, D), F32), jax.ShapeDtypeStruct((T, D), BF16),
                   jax.ShapeDtypeStruct((8, 128), F32)],
        compiler_params=_cp("arbitrary", "arbitrary"),
    )(act, w_down, x1, target)


def _block_diag(w):
    eye = jnp.eye(8, dtype=w.dtype)
    return (w[:, :, None, :] * eye[:, None, :, None]).reshape(RW, RW).astype(BF16)


def _diag_blocks(m):
    return jnp.stack([m[HD * b:HD * (b + 1), HD * b:HD * (b + 1)] for b in range(8)])


def _local_step(x, pos_col, target, p, exch):
    qg, kg = jnp.tile(p["q_norm_g"], (1, 8)), jnp.tile(p["k_norm_g"], (1, 8))
    wrg, wig = _block_diag(p["w_rg"]), _block_diag(p["w_ig"])
    brg, big = p["b_rg"].reshape(1, RW), p["b_ig"].reshape(1, RW)

    h1 = _rms_fwd("rms1", x, p["g_mix"])
    proj = _mm("mm_in", h1, p["w_in"], "nn", 512, 640, stack=NCHIP, after=exch.start_rest())
    q, k, v = _qk_prep(proj, pos_col, qg, kg)
    qs, ks, vs = ([_regroup(a, d) for d in DILATIONS] for a in (q, k, v))
    os_, ls_ = [], []
    for bi, d in enumerate(DILATIONS):
        o, l = _attn_fwd(f"attn_fwd{d}", qs[bi], ks[bi], vs[bi], d)
        os_.append(_ungroup(o, d))
        ls_.append(_ungroup(l, d))
    attn, lse, mix = _attn_merge(os_, ls_, p["g_attn_out"])
    mix, hseq = _rec_fwd(proj, mix, p["rec_conv_w"], p["rec_conv_b"], wrg, wig, brg, big, p["lru_lambda"], p["g_rec_out"])
    rest = exch.wait_rest(mix)
    x1 = _mm("mm_out", mix, rest["w_out"], "nn", 512, 512, res=x)
    h2 = _rms_fwd("rms2", x1, p["g_ffn"])
    up_pre = _mm("mm_up", h2, rest["w_up"], "nn", 512, 768, stack=NCHIP)
    act = _ffn_act(up_pre, p["ffn_conv_w"], p["ffn_conv_b"])
    dy, dyb, loss_blk = _down_loss(act, rest["w_down"], x1, target)

    g = {}
    tok = exch.reduce_start("w_down", *_mm("wg_down", act, dyb, "tn", 512, 512, twin_bf16=True))
    dact = _mm("dg_down", dyb, rest["w_down"], "nt", 512, 512, after=tok)
    dup, g["ffn_conv_w"], g["ffn_conv_b"] = _ffn_bwd(up_pre, dact, p["ffn_conv_w"], p["ffn_conv_b"])
    tok = exch.reduce_start("w_up", *_mm("wg_up", h2, dup, "tn", 512, 768, stack=NCHIP, twin_bf16=True))
    dh2 = _mm("dg_up", dup, rest["w_up"], "nt", 512, 256, stack=NCHIP, after=tok)
    dx1, dx1b, g["g_ffn"] = _rms_bwd("rms2_bwd", x1, p["g_ffn"], dh2, dy, True)
    tok = exch.reduce_start("w_out", *_mm("wg_out", mix, dx1b, "tn", 512, 512, twin_bf16=True))
    dmix = _mm("dg_out", dx1b, rest["w_out"], "nt", 512, 512, after=tok)
    do, delta, g["g_attn_out"] = _attn_out_bwd(attn, dmix, p["g_attn_out"])
    dqs, dks, dvs = [], [], []
    for bi, d in enumerate(DILATIONS):
        dq, dk, dv = _attn_bwd(f"attn_bwd{d}", qs[bi], ks[bi], vs[bi], _regroup(do, d), _regroup(lse, d),
                               _regroup(delta, d), d)
        dqs.append(_ungroup(dq, d))
        dks.append(_ungroup(dk, d))
        dvs.append(_ungroup(dv, d))
    dproj, dqg, dkg = _qk_bwd(proj, pos_col, qg, kg, dqs, dks, dvs)
    (dproj, xcb, dprb, dpib, g["rec_conv_w"], g["rec_conv_b"], dbr, dbi, dsp, g["g_rec_out"]) = _rec_bwd(
        proj, hseq, dmix, dproj, p["rec_conv_w"], p["rec_conv_b"], wrg, wig, brg, big, p["lru_lambda"], p["g_rec_out"])
    g["w_rg"] = _diag_blocks(_mm("wg_rg", xcb, dprb, "tn", 512, 512)).reshape(RW, HD)
    g["w_ig"] = _diag_blocks(_mm("wg_ig", xcb, dpib, "tn", 512, 512)).reshape(RW, HD)
    g["b_rg"], g["b_ig"] = dbr.reshape(8, HD), dbi.reshape(8, HD)
    g["lru_lambda"] = dsp
    g["q_norm_g"] = dqg.reshape(8, HD).sum(axis=0, keepdims=True)
    g["k_norm_g"] = dkg.reshape(8, HD).sum(axis=0, keepdims=True)
    tok = exch.reduce_start("w_in", *_mm("wg_in", h1, dproj, "tn", 512, 640, stack=NCHIP, twin_bf16=True))
    dh1 = _mm("dg_in", dproj, p["w_in"], "nt", 512, 512, stack=NCHIP, after=tok)
    grad_x, g["g_mix"] = _rms_bwd("rms1_bwd", x, p["g_mix"], dh1, dx1, False)
    return loss_blk, grad_x, g


ANY = pl.BlockSpec(memory_space=pl.ANY)


def _mesh_pos():
    return lax.axis_index("x"), lax.axis_index("y"), lax.axis_index("c")


def _slot(px, py, perm):
    return 2 * py + px if perm else 2 * px + py


def _other_chips(x, y):
    return [(1 - x, y), (x, 1 - y), (1 - x, 1 - y)]


def _rcopy(src, dst, send, recv, k, to, kr=None):
    return pltpu.make_async_remote_copy(src_ref=src, dst_ref=dst, send_sem=send.at[k],
                                        recv_sem=recv.at[k if kr is None else kr], device_id=to, device_id_type=MESH)


def _cast_bf16(name, w):
    r, c = w.shape
    tr = 128
    def body(w_ref, o_ref):
        o_ref[...] = w_ref[...].astype(BF16)
    return pl.pallas_call(
        body, name=name, grid=(r // tr,), in_specs=[_bs((tr, c), lambda i: (i, 0))],
        out_specs=_bs((tr, c), lambda i: (i, 0)), out_shape=jax.ShapeDtypeStruct((r, c), BF16),
        compiler_params=_cp("parallel"),
    )(w)


def _gather_weights(big, small, slot):
    nb, ns = len(big), len(small)
    perms = [p for _, p in big] + [p for _, p in small]

    def body(*refs):
        ins, outs = refs[:nb + ns], refs[2 * (nb + ns):3 * (nb + ns)]
        send, recv = refs[3 * (nb + ns):]
        x, y, c = _mesh_pos()
        me, sib = (x, y, c), (x, y, 1 - c)
        chips = _other_chips(x, y)
        first = []
        for a in range(nb):
            for j, (px, py) in enumerate(chips):
                first.append(_rcopy(ins[a].at[c], outs[a].at[_slot(x, y, perms[a]), c], send, recv, 3 * a + j, (px, py, c)))
        for t in range(ns):
            a = nb + t
            for j, (px, py) in enumerate(chips):
                first.append(_rcopy(ins[a], outs[a].at[_slot(x, y, perms[a])], send, recv, 6 * nb + 3 * t + j, (px, py, c)))
        for cp in first:
            cp.start()
        passed = []
        for a in range(nb):
            for j, (px, py) in enumerate(chips):
                got = outs[a].at[_slot(px, py, perms[a]), c]
                _rcopy(got, got, send, recv, 3 * a + j, me).wait_recv()
                fwd = _rcopy(got, got, send, recv, 3 * nb + 3 * a + j, sib)
                fwd.start()
                passed.append(fwd)
        for a in range(nb):
            for j, (px, py) in enumerate(chips):
                got = outs[a].at[_slot(px, py, perms[a]), 1 - c]
                _rcopy(got, got, send, recv, 3 * nb + 3 * a + j, me).wait_recv()
        for t in range(ns):
            a = nb + t
            for j, (px, py) in enumerate(chips):
                got = outs[a].at[_slot(px, py, perms[a])]
                _rcopy(got, got, send, recv, 6 * nb + 3 * t + j, me).wait_recv()
        for cp in first + passed:
            cp.wait_send()

    arrs = [a for a, _ in big] + [a for a, _ in small]
    lands = [lax.dynamic_update_slice(lax.empty((NCHIP,) + a.shape, a.dtype), a[None], (slot[p],) + (0,) * a.ndim)
             for a, p in zip(arrs, perms)]
    nsem = 6 * nb + 3 * ns
    return pl.pallas_call(
        body, name="gather_weights", in_specs=[ANY] * (2 * (nb + ns)), out_specs=[ANY] * (nb + ns),
        out_shape=[jax.ShapeDtypeStruct(a.shape, a.dtype) for a in lands],
        input_output_aliases={nb + ns + i: i for i in range(nb + ns)},
        scratch_shapes=[pltpu.SemaphoreType.DMA((nsem,)), pltpu.SemaphoreType.DMA((nsem,))],
    )(*arrs, *lands)


HBM = pl.BlockSpec(memory_space=pltpu.HBM)
SEM = pl.BlockSpec(memory_space=pltpu.SEMAPHORE)
EFFECT = pltpu.SideEffectType.DATAFLOW_SIDE_EFFECTING


def _split_start(name, srcs, lands, plan, nsem):
    ns, nl = len(srcs), len(lands)

    def body(*refs):
        send, recv = refs[ns + nl], refs[ns + nl + 1]
        sends, _ = plan(refs[:ns], refs[ns:ns + nl], send, recv)
        for cp in sends:
            cp.start()
        refs[-1][...] = jnp.zeros((8, 128), F32)

    arrs = list(srcs) + list(lands)
    out = pl.pallas_call(
        body, name=name, in_specs=[HBM] * (ns + nl),
        out_specs=[SEM, SEM] + [HBM] * (ns + nl) + [pl.BlockSpec(memory_space=pltpu.VMEM)],
        out_shape=[pltpu.SemaphoreType.DMA((nsem,)), pltpu.SemaphoreType.DMA((nsem,))]
        + [pltpu.HBM(a.shape, a.dtype) for a in arrs] + [jax.ShapeDtypeStruct((8, 128), F32)],
        input_output_aliases={i: 2 + i for i in range(ns + nl)},
        compiler_params=pltpu.CompilerParams(has_side_effects=EFFECT),
    )(*[pltpu.with_memory_space_constraint(a, pltpu.HBM) for a in arrs])
    return out[0], out[1], out[2:2 + ns], out[2 + ns:2 + ns + nl], out[-1]


def _split_wait(name, send, recv, srcs, lands, plan, after):
    ns, nl = len(srcs), len(lands)

    def body(*refs):
        sends, recvs = plan(refs[:ns], refs[ns:ns + nl], refs[ns + nl], refs[ns + nl + 1])
        for cp in sends:
            cp.wait_send()
        for cp in recvs:
            cp.wait_recv()

    arrs = list(srcs) + list(lands)
    out = pl.pallas_call(
        body, name=name, in_specs=[HBM] * (ns + nl) + [SEM, SEM, ANY], out_specs=[HBM] * (ns + nl),
        out_shape=[pltpu.HBM(a.shape, a.dtype) for a in arrs],
        input_output_aliases={i: i for i in range(ns + nl)},
        compiler_params=pltpu.CompilerParams(has_side_effects=EFFECT),
    )(*arrs, send, recv, after)
    return out[ns:]


def _gather_plan(perms):
    def plan(srcs, lands, send, recv):
        x, y, c = _mesh_pos()
        sends, recvs = [], []
        for a, perm in enumerate(perms):
            for j, (px, py) in enumerate(_other_chips(x, y)):
                for cc in (0, 1):
                    k = 6 * a + 2 * j + cc
                    sends.append(_rcopy(srcs[a].at[c], lands[a].at[_slot(x, y, perm), c], send, recv, k, (px, py, cc),
                                        kr=6 * a + 2 * j + c))
                    got = lands[a].at[_slot(px, py, perm), cc]
                    recvs.append(_rcopy(got, got, send, recv, k, (x, y, c)))
        return sends, recvs
    return plan


def _reduce_plan(perm):
    def plan(srcs, lands, send, recv):
        x, y, c = _mesh_pos()
        src, land = srcs[0], lands[0]
        sends = []
        for j, (px, py) in enumerate(_other_chips(x, y)):
            for hf in (0, 1):
                sends.append(_rcopy(src.at[_slot(px, py, perm), hf], land.at[2 * j + c], send, recv, 2 * j + hf,
                                    (px, py, hf), kr=2 * j + c))
        sends.append(_rcopy(src.at[_slot(x, y, perm), 1 - c], land.at[6], send, recv, 6, (x, y, 1 - c)))
        recvs = [_rcopy(land.at[i], land.at[i], send, recv, i, (x, y, c)) for i in range(7)]
        return sends, recvs
    return plan


def _sibling_share(rs):
    na = len(rs)

    def body(*refs):
        ins, outs, (send, recv) = refs[:na], refs[na:2 * na], refs[2 * na:]
        x, y, c = _mesh_pos()
        cps = [_rcopy(ins[a], outs[a], send, recv, a, (x, y, 1 - c)) for a in range(na)]
        for cp in cps:
            cp.start()
        for cp in cps:
            cp.wait()

    return pl.pallas_call(
        body, name="rs_share", in_specs=[ANY] * na, out_specs=[ANY] * na,
        out_shape=[jax.ShapeDtypeStruct(r.shape, F32) for r in rs],
        scratch_shapes=[pltpu.SemaphoreType.DMA((na,)), pltpu.SemaphoreType.DMA((na,))],
    )(*rs)


def _add_pieces(name, g, got, where):
    _, _, r2, cc = g.shape
    tr = 128

    def body(w_ref, g_ref, r_ref, o_ref):
        del w_ref
        acc = g_ref[...]
        for i in range(7):
            acc = acc + r_ref[i].astype(F32)
        o_ref[...] = acc

    return pl.pallas_call(
        body, name=name,
        grid_spec=pltpu.PrefetchScalarGridSpec(
            num_scalar_prefetch=1, grid=(r2 // tr,),
            in_specs=[_bs((None, None, tr, cc), lambda i, w_ref: (w_ref[0], w_ref[1], i, 0)),
                      _bs((7, tr, cc), lambda i, w_ref: (0, i, 0))],
            out_specs=_bs((tr, cc), lambda i, w_ref: (i, 0))),
        out_shape=jax.ShapeDtypeStruct((r2, cc), F32), compiler_params=_cp("parallel"),
    )(where, g, got)


def _adam_math(w, g, m, v):
    m = ADAM_B1 * m + (1.0 - ADAM_B1) * g
    v = ADAM_B2 * v + (1.0 - ADAM_B2) * (g * g)
    m_hat = m / (1.0 - ADAM_B1 ** ADAM_STEP)
    v_hat = v / (1.0 - ADAM_B2 ** ADAM_STEP)
    return -ADAM_LR * (m_hat / (jnp.sqrt(v_hat) + ADAM_EPS) + ADAM_WD * w), m, v


def _adam_big(name, w, g_mine, g_sib, m, v, c_arr):
    r, cols = w.shape
    tr = 128
    per = r // 2 // tr

    def body(c_ref, w_ref, a_ref, b_ref, m_ref, v_ref, g_ref, d_ref, m2_ref, v2_ref):
        g = jnp.where(pl.program_id(0) == c_ref[0], a_ref[...], b_ref[...])
        g_ref[...] = g
        d_ref[...], m2_ref[...], v2_ref[...] = _adam_math(w_ref[...], g, m_ref[...], v_ref[...])

    spec = _bs((tr, cols), lambda h, i, c_ref: (h * per + i, 0))
    half = _bs((tr, cols), lambda h, i, c_ref: (i, 0))
    out = jax.ShapeDtypeStruct((r, cols), F32)
    return pl.pallas_call(
        body, name=name,
        grid_spec=pltpu.PrefetchScalarGridSpec(
            num_scalar_prefetch=1, grid=(2, per), in_specs=[spec, half, half, spec, spec], out_specs=[spec] * 4),
        out_shape=[out] * 4, compiler_params=_cp("parallel", "parallel"),
    )(c_arr, w, g_mine, g_sib, m, v)


_CLASS_SHAPE = {"a": (8, D), "b": (8, RW), "c": (8, 2 * DFF), "d": (1048, HD)}
_SMALL = (
    ("g_mix", "a", 0, 1, D), ("g_ffn", "a", 1, 1, D),
    ("rec_conv_w", "b", 0, 4, RW), ("rec_conv_b", "b", 4, 1, RW), ("lru_lambda", "b", 5, 1, RW),
    ("g_attn_out", "b", 6, 1, RW), ("g_rec_out", "b", 7, 1, RW),
    ("ffn_conv_w", "c", 0, 3, 2 * DFF), ("ffn_conv_b", "c", 3, 1, 2 * DFF),
    ("w_rg", "d", 0, RW, HD), ("w_ig", "d", RW, RW, HD), ("b_rg", "d", 2 * RW, 8, HD), ("b_ig", "d", 2 * RW + 8, 8, HD),
    ("q_norm_g", "d", 2 * RW + 16, 1, HD), ("k_norm_g", "d", 2 * RW + 17, 1, HD),
)
_LOSS_ROW = 2
_CLASSES = ("a", "b", "c", "d")


def _small_allreduce(g, loss_blk):
    names = [s[0] for s in _SMALL]
    nin = len(names) + 1

    def body(*refs):
        ins = dict(zip(names, refs[:len(names)]))
        loss_ref = refs[len(names)]
        outs = dict(zip(_CLASSES, refs[nin:nin + 4]))
        pair = dict(zip(_CLASSES, refs[nin + 4:nin + 8]))
        quad = dict(zip(_CLASSES, refs[nin + 8:nin + 12]))
        send, recv = refs[nin + 12:]
        x, y, c = _mesh_pos()
        chip = 2 * x + y
        pair["a"][c] = jnp.zeros(_CLASS_SHAPE["a"], F32)
        pair["b"][c] = ins["rec_conv_w"][...]
        pair["c"][c] = ins["ffn_conv_w"][...]
        pair["d"][c, 2 * RW + 16:, :] = jnp.zeros((8, HD), F32)
        for name, k, r0, nr, _ in _SMALL:
            if name in ("rec_conv_w", "ffn_conv_w"):
                continue
            pair[k][c, r0:r0 + nr, :] = ins[name][...]
        pair["a"][c, _LOSS_ROW:_LOSS_ROW + 1, :] = jnp.broadcast_to(loss_ref[0:1, 0:1], (1, D))
        cps = [_rcopy(pair[k].at[c], pair[k].at[c], send, recv, ki, (x, y, 1 - c)) for ki, k in enumerate(_CLASSES)]
        for cp in cps:
            cp.start()
        for ki, k in enumerate(_CLASSES):
            _rcopy(pair[k].at[1 - c], pair[k].at[1 - c], send, recv, ki, (x, y, c)).wait_recv()
            quad[k][chip] = pair[k][0] + pair[k][1]
        cps2 = []
        for ki, k in enumerate(_CLASSES):
            for j, (px, py) in enumerate(_other_chips(x, y)):
                cps2.append(_rcopy(quad[k].at[chip], quad[k].at[chip], send, recv, 4 + 3 * ki + j, (px, py, c)))
        for cp in cps2:
            cp.start()
        for ki, k in enumerate(_CLASSES):
            for j, (px, py) in enumerate(_other_chips(x, y)):
                got = quad[k].at[2 * px + py]
                _rcopy(got, got, send, recv, 4 + 3 * ki + j, (x, y, c)).wait_recv()
            outs[k][...] = ((quad[k][0] + quad[k][1]) + quad[k][2]) + quad[k][3]
        for cp in cps + cps2:
            cp.wait_send()

    vm = pl.BlockSpec(memory_space=pltpu.VMEM)
    return pl.pallas_call(
        body, name="small_allreduce", in_specs=[vm] * nin, out_specs=[vm] * 4,
        out_shape=[jax.ShapeDtypeStruct(_CLASS_SHAPE[k], F32) for k in _CLASSES],
        scratch_shapes=[pltpu.VMEM((2,) + _CLASS_SHAPE[k], F32) for k in _CLASSES]
        + [pltpu.VMEM((NCHIP,) + _CLASS_SHAPE[k], F32) for k in _CLASSES]
        + [pltpu.SemaphoreType.DMA((16,)), pltpu.SemaphoreType.DMA((16,))],
        compiler_params=pltpu.CompilerParams(vmem_limit_bytes=VMEM_LIMIT),
    )(*[g[n] for n in names], loss_blk)


def _adam_small(red, w, m, v):
    names = [s[0] for s in _SMALL]
    n = len(names)

    def body(*refs):
        red_refs = dict(zip(_CLASSES, refs[:4]))
        w_refs, m_refs, v_refs = refs[4:4 + n], refs[4 + n:4 + 2 * n], refs[4 + 2 * n:4 + 3 * n]
        loss_ref = refs[4 + 3 * n]
        out_refs = refs[5 + 3 * n:]
        x, y, _ = _mesh_pos()
        chip = 2 * x + y
        loss_ref[...] = jnp.broadcast_to(red_refs["a"][_LOSS_ROW:_LOSS_ROW + 1, 0:1], loss_ref.shape)
        for pi, (name, k, r0, nr, width) in enumerate(_SMALL):
            gfull = red_refs[k][r0:r0 + nr, :]
            if name == "rec_conv_w":
                parts = [gfull[:, 128 * s:128 * (s + 1)] for s in range(NCHIP)]
                g = jnp.where(chip == 0, parts[0], jnp.where(chip == 1, parts[1], jnp.where(chip == 2, parts[2], parts[3])))
            elif name == "ffn_conv_w":
                parts = [gfull[:, FC * s:FC * (s + 1)] for s in range(NCHIP)]
                g = jnp.where(chip == 0, parts[0], jnp.where(chip == 1, parts[2], jnp.where(chip == 2, parts[1], parts[3])))
            elif name == "ffn_conv_b":
                g = jnp.concatenate([gfull[:, FC * s:FC * (s + 1)] for s in (0, 2, 1, 3)], axis=1)
            else:
                g = gfull
            d, m2, v2 = _adam_math(w_refs[pi][...], g, m_refs[pi][...], v_refs[pi][...])
            o = out_refs[4 * pi:4 * pi + 4]
            o[0][...], o[1][...], o[2][...], o[3][...] = g, d, m2, v2

    vm = pl.BlockSpec(memory_space=pltpu.VMEM)
    outs = [jax.ShapeDtypeStruct((1, 128), F32)]
    for name in names:
        outs += [jax.ShapeDtypeStruct(w[name].shape, F32)] * 4
    res = pl.pallas_call(
        body, name="adam_small", in_specs=[vm] * (4 + 3 * n), out_specs=[vm] * len(outs), out_shape=outs,
        compiler_params=pltpu.CompilerParams(vmem_limit_bytes=VMEM_LIMIT),
    )(*red, *[w[k] for k in names], *[m[k] for k in names], *[v[k] for k in names])
    return res[0], {name: res[1 + 4 * i:5 + 4 * i] for i, name in enumerate(names)}


_WEIGHTS = ("g_mix", "w_in", "q_norm_g", "k_norm_g", "rec_conv_w", "rec_conv_b", "w_rg", "b_rg", "w_ig", "b_ig",
            "lru_lambda", "g_attn_out", "g_rec_out", "w_out", "g_ffn", "w_up", "ffn_conv_w", "ffn_conv_b", "w_down")
_BIG = ("w_in", "w_out", "w_up", "w_down")
_BIG_PERM = {"w_in": False, "w_out": False, "w_up": True, "w_down": False}
_SMALL_2D = {"w_rg": (RW, HD), "w_ig": (RW, HD), "b_rg": (8, HD), "b_ig": (8, HD), "rec_conv_w": (4, 128),
             "ffn_conv_w": (3, FC)}


def _halves(a):
    r, c = a.shape
    return a.reshape(2, r // 2, c)


def kernel(x, positions, g_mix, w_in, q_norm_g, k_norm_g, rec_conv_w, rec_conv_b, w_rg, b_rg, w_ig, b_ig, lru_lambda, g_attn_out, g_rec_out, w_out, g_ffn, w_up, ffn_conv_w, ffn_conv_b, w_down, loss_target, m_g_mix, m_w_in, m_q_norm_g, m_k_norm_g, m_rec_conv_w, m_rec_conv_b, m_w_rg, m_b_rg, m_w_ig, m_b_ig, m_lru_lambda, m_g_attn_out, m_g_rec_out, m_w_out, m_g_ffn, m_w_up, m_ffn_conv_w, m_ffn_conv_b, m_w_down, v_g_mix, v_w_in, v_q_norm_g, v_k_norm_g, v_rec_conv_w, v_rec_conv_b, v_w_rg, v_b_rg, v_w_ig, v_b_ig, v_lru_lambda, v_g_attn_out, v_g_rec_out, v_w_out, v_g_ffn, v_w_up, v_ffn_conv_w, v_ffn_conv_b, v_w_down):
    given = dict(g_mix=g_mix, w_in=w_in, q_norm_g=q_norm_g, k_norm_g=k_norm_g, rec_conv_w=rec_conv_w, rec_conv_b=rec_conv_b, w_rg=w_rg, b_rg=b_rg, w_ig=w_ig, b_ig=b_ig, lru_lambda=lru_lambda, g_attn_out=g_attn_out, g_rec_out=g_rec_out, w_out=w_out, g_ffn=g_ffn, w_up=w_up, ffn_conv_w=ffn_conv_w, ffn_conv_b=ffn_conv_b, w_down=w_down)
    given_m = dict(g_mix=m_g_mix, w_in=m_w_in, q_norm_g=m_q_norm_g, k_norm_g=m_k_norm_g, rec_conv_w=m_rec_conv_w, rec_conv_b=m_rec_conv_b, w_rg=m_w_rg, b_rg=m_b_rg, w_ig=m_w_ig, b_ig=m_b_ig, lru_lambda=m_lru_lambda, g_attn_out=m_g_attn_out, g_rec_out=m_g_rec_out, w_out=m_w_out, g_ffn=m_g_ffn, w_up=m_w_up, ffn_conv_w=m_ffn_conv_w, ffn_conv_b=m_ffn_conv_b, w_down=m_w_down)
    given_v = dict(g_mix=v_g_mix, w_in=v_w_in, q_norm_g=v_q_norm_g, k_norm_g=v_k_norm_g, rec_conv_w=v_rec_conv_w, rec_conv_b=v_rec_conv_b, w_rg=v_w_rg, b_rg=v_b_rg, w_ig=v_w_ig, b_ig=v_b_ig, lru_lambda=v_lru_lambda, g_attn_out=v_g_attn_out, g_rec_out=v_g_rec_out, w_out=v_w_out, g_ffn=v_g_ffn, w_up=v_w_up, ffn_conv_w=v_ffn_conv_w, ffn_conv_b=v_ffn_conv_b, w_down=v_w_down)
    shapes = {n: a.shape for n, a in given.items()}

    def two_d(n, a):
        a = a[0]
        return a.reshape(_SMALL_2D[n]) if n in _SMALL_2D else (a if a.ndim == 2 else a[None])

    w = {n: two_d(n, a) for n, a in given.items()}
    m = {n: two_d(n, a) for n, a in given_m.items()}
    v = {n: two_d(n, a) for n, a in given_v.items()}
    cc = lax.axis_index("c").astype(jnp.int32)
    cx, cy = lax.axis_index("x").astype(jnp.int32), lax.axis_index("y").astype(jnp.int32)
    slot = {False: 2 * cx + cy, True: 2 * cy + cx}

    shards = {n: _halves(_cast_bf16(f"cast_{n}", w[n])) for n in _BIG}
    small = [(jnp.pad(w["ffn_conv_w"], ((0, 5), (0, 0))), True), (jnp.pad(w["rec_conv_w"], ((0, 4), (0, 0))), False)]
    f_in, f_fcw, f_rcw = _gather_weights([(shards["w_in"], False)], small, slot)
    p = {n: w[n] for n in ("g_mix", "g_ffn", "q_norm_g", "k_norm_g", "rec_conv_b", "lru_lambda", "g_attn_out", "g_rec_out")}
    p.update(w_rg=w["w_rg"].reshape(8, HD, HD), w_ig=w["w_ig"].reshape(8, HD, HD), b_rg=w["b_rg"], b_ig=w["b_ig"],
             w_in=f_in.reshape(NCHIP, D, INW // NCHIP), ffn_conv_w=f_fcw,
             ffn_conv_b=jnp.concatenate([w["ffn_conv_b"][:, FC * s:FC * (s + 1)] for s in (0, 2, 1, 3)], axis=1),
             rec_conv_w=f_rcw.transpose(1, 0, 2).reshape(8, RW))

    class Exchange:
        rest = ("w_out", "w_up", "w_down")
        order = []
        flight = {}

        def start_rest(self):
            srcs = [shards[n] for n in self.rest]
            lands = [lax.dynamic_update_slice(lax.empty((NCHIP,) + s.shape, BF16), s[None], (slot[_BIG_PERM[n]], 0, 0, 0))
                     for n, s in zip(self.rest, srcs)]
            plan = _gather_plan([_BIG_PERM[n] for n in self.rest])
            send, recv, srcs, lands, token = _split_start("gather_rest_start", srcs, lands, plan, 6 * len(srcs))
            self.flight["rest"] = (send, recv, srcs, lands, plan)
            return (token,)

        def wait_rest(self, after):
            send, recv, srcs, lands, plan = self.flight.pop("rest")
            f_out, f_up, f_down = _split_wait("gather_rest_wait", send, recv, srcs, lands, plan, after)
            return dict(w_out=f_out.reshape(D, D), w_up=f_up.reshape(NCHIP, D, FC), w_down=f_down.reshape(DFF, D))

        def reduce_start(self, name, g32, g16):
            r2, cols = shards[name].shape[1:]
            plan = _reduce_plan(_BIG_PERM[name])
            send, recv, srcs, lands, token = _split_start(
                f"reduce_{name}_start", [g16.reshape(NCHIP, 2, r2, cols)], [lax.empty((7, r2, cols), BF16)], plan, 7)
            self.flight[name] = (send, recv, srcs, lands, plan, g32.reshape(NCHIP, 2, r2, cols))
            self.order.append(name)
            return (token,)

        def finish(self, after):
            mine = {}
            for name in self.order:
                send, recv, srcs, lands, plan, g32 = self.flight.pop(name)
                (got,) = _split_wait(f"reduce_{name}_wait", send, recv, srcs, lands, plan, after)
                where = jnp.stack([slot[_BIG_PERM[name]], cc])
                mine[name] = after = _add_pieces(f"reduce_{name}_add", g32, got, where)
            theirs = dict(zip(_BIG, _sibling_share([mine[n] for n in _BIG])))
            return mine, theirs

    exch = Exchange()

    loss_blk, grad_x, g = _local_step(x[0], positions.reshape(T, 1), loss_target[0], p, exch)

    out_g, out_d, out_m, out_v = {}, {}, {}, {}
    red = _small_allreduce(g, loss_blk)
    loss_row, small_out = _adam_small(red, w, m, v)
    for n, (gn, dn, mn, vn) in small_out.items():
        out_g[n], out_d[n], out_m[n], out_v[n] = gn, dn, mn, vn

    mine, theirs = exch.finish(red[0])
    for n in _BIG:
        out_g[n], out_d[n], out_m[n], out_v[n] = _adam_big(f"adam_{n}", w[n], mine[n], theirs[n], m[n], v[n], cc.reshape(1))

    outs = [loss_row[0, 0], grad_x[None]]
    for group in (out_g, out_d, out_m, out_v):
        outs += [group[n].reshape(shapes[n]) for n in _WEIGHTS]
    return tuple(outs)
```

```python
import math

import jax
import jax.numpy as jnp
import numpy as np
from jax import lax
from jax.experimental import pallas as pl
from jax.experimental.pallas import tpu as pltpu

F32 = jnp.float32
BF16 = jnp.bfloat16

T = 4096
D = 1024
HD = 64
AW = 512
RW = 512
INW = 2560
DFF = 3072
NCHIP = 4
EPS = 1e-6
NEG = -1e30
LRU_C = 8.0
ROPE_THETA = 10000.0
BLK = 128
DILATIONS = (1, 4, 16)
ADAM_LR, ADAM_B1, ADAM_B2, ADAM_EPS, ADAM_WD, ADAM_STEP = 0.001, 0.9, 0.999, 1e-08, 0.01, 10
VMEM_LIMIT = 56 * 1024 * 1024
MESH = pl.DeviceIdType.MESH

NN = (((1,), (0,)), ((), ()))
NT = (((1,), (1,)), ((), ()))
TN = (((0,), (0,)), ((), ()))


def _cp(*sem):
    return pltpu.CompilerParams(dimension_semantics=sem, vmem_limit_bytes=VMEM_LIMIT)


def _bs(shape, fn):
    return pl.BlockSpec(shape, fn)


def _dot(a, b, dims=NN):
    return lax.dot_general(a, b, dims, preferred_element_type=F32)


_GC = math.sqrt(2.0 / math.pi)


def _gelu(x):
    return x * (0.5 * (1.0 + jnp.tanh(_GC * (x + 0.044715 * (x * x * x)))))


def _gelu_and_grad(x):
    x2 = x * x
    th = jnp.tanh(_GC * (x + 0.044715 * (x * x2)))
    cdf = 0.5 * (1.0 + th)
    dg = cdf + 0.5 * x * (1.0 - th * th) * (_GC * (1.0 + 3.0 * 0.044715 * x2))
    return x * cdf, dg


def _softplus(x):
    e = jnp.exp(-jnp.abs(x))
    u = 1.0 + e
    l1p = jnp.where(u == 1.0, e, jnp.log(u) * (e / (u - 1.0)))
    return jnp.maximum(x, 0.0) + l1p


def _segsum(z, e_bf16):
    hi = z.astype(BF16)
    lo = (z - hi.astype(F32)).astype(BF16)
    return _dot(hi, e_bf16) + _dot(lo, e_bf16)


def _mm(name, a, b, mode, tm, tn, out_dtype=F32, res=None, stack=0, twin_bf16=False, after=()):
    if mode == "nn":
        (m, k), n = a.shape, (b.shape[1] if not stack else stack * b.shape[2])
        a_spec = _bs((tm, k), lambda j, i: (i, 0))
        if stack:
            per = b.shape[2] // tn
            b_spec = _bs((None, k, tn), lambda j, i: (j // per, 0, j % per))
        else:
            b_spec = _bs((k, tn), lambda j, i: (0, j))
    elif mode == "nt":
        (m, k), n = a.shape, (b.shape[0] if not stack else b.shape[1])
        a_spec = _bs((tm, k), lambda j, i: (i, 0))
        b_spec = _bs((stack, tn, k // stack), lambda j, i: (0, j, 0)) if stack else _bs((tn, k), lambda j, i: (j, 0))
    else:
        (k, m), n = a.shape, b.shape[1]
        a_spec, b_spec = _bs((k, tm), lambda j, i: (0, i)), _bs((k, tn), lambda j, i: (0, j))
    assert m % tm == 0 and n % tn == 0
    o_spec = _bs((tm, tn), lambda j, i: (i, j))
    o_shape = (m, n)
    if mode == "tn" and stack:
        per = n // stack // tn
        o_spec = _bs((None, tm, tn), lambda j, i: (j // per, i, j % per))
        o_shape = (stack, m, n // stack)
    dims = {"nn": NN, "nt": NT, "tn": TN}[mode]

    def product(a_ref, b_ref):
        if mode == "nt" and stack:
            cs = k // stack
            acc = _dot(a_ref[:, 0:cs], b_ref[0], NT)
            for s in range(1, stack):
                acc = acc + _dot(a_ref[:, s * cs:(s + 1) * cs], b_ref[s], NT)
            return acc
        return _dot(a_ref[...], b_ref[...], dims)

    nres = 0 if res is None else 1

    def body(a_ref, b_ref, *rest):
        acc = product(a_ref, b_ref)
        if nres:
            acc = rest[0][...] + acc
        outs = rest[nres + len(after):]
        outs[0][...] = acc.astype(out_dtype)
        if twin_bf16:
            outs[1][...] = acc.astype(BF16)

    ins = (a, b) + ((res,) if nres else ()) + tuple(after)
    specs = [a_spec, b_spec] + ([o_spec] if nres else []) + [pl.BlockSpec(memory_space=pl.ANY)] * len(after)
    shapes = [jax.ShapeDtypeStruct(o_shape, out_dtype)] + ([jax.ShapeDtypeStruct(o_shape, BF16)] if twin_bf16 else [])
    out = pl.pallas_call(
        body, name=name, grid=(n // tn, m // tm), in_specs=specs, out_specs=[o_spec] * len(shapes),
        out_shape=shapes, compiler_params=_cp("parallel", "parallel"),
    )(*ins)
    return tuple(out) if twin_bf16 else out[0]


def _rms_fwd(name, x, g):
    tr = 512

    def body(x_ref, g_ref, o_ref):
        xv = x_ref[...]
        r = lax.rsqrt(jnp.mean(xv * xv, axis=-1, keepdims=True) + EPS)
        o_ref[...] = ((xv * r) * g_ref[...]).astype(BF16)

    return pl.pallas_call(
        body, name=name, grid=(T // tr,), in_specs=[_bs((tr, D), lambda i: (i, 0)), _bs((1, D), lambda i: (0, 0))],
        out_specs=_bs((tr, D), lambda i: (i, 0)), out_shape=jax.ShapeDtypeStruct((T, D), BF16),
        compiler_params=_cp("parallel"),
    )(x, g)


def _rms_bwd(name, x, g, dy, dres, want_bf16):
    tr = 256

    def body(x_ref, g_ref, dy_ref, dr_ref, dx_ref, *rest):
        dg_ref = rest[-1]
        xv, dyv = x_ref[...], dy_ref[...]
        r = lax.rsqrt(jnp.mean(xv * xv, axis=-1, keepdims=True) + EPS)
        gdy = g_ref[...] * dyv
        dx = r * gdy - xv * ((r * r * r) * jnp.mean(xv * gdy, axis=-1, keepdims=True)) + dr_ref[...]
        dx_ref[...] = dx
        if want_bf16:
            rest[0][...] = dx.astype(BF16)

        @pl.when(pl.program_id(0) == 0)
        def _():
            dg_ref[...] = jnp.zeros_like(dg_ref)

        dg_ref[...] += jnp.sum(dyv * (xv * r), axis=0, keepdims=True)

    row = _bs((tr, D), lambda i: (i, 0))
    vec = _bs((1, D), lambda i: (0, 0))
    outs = [jax.ShapeDtypeStruct((T, D), F32)] + ([jax.ShapeDtypeStruct((T, D), BF16)] if want_bf16 else [])
    return pl.pallas_call(
        body, name=name, grid=(T // tr,), in_specs=[row, vec, row, row],
        out_specs=[row] * len(outs) + [vec], out_shape=outs + [jax.ShapeDtypeStruct((1, D), F32)],
        compiler_params=_cp("arbitrary"),
    )(x, g, dy, dres)


def _head_ones():
    idx = np.arange(AW) // HD
    return jnp.asarray((idx[:, None] == idx[None, :]).astype(np.float32), dtype=BF16)


def _freq_row():
    half = HD // 2
    inv = ROPE_THETA ** (-(np.arange(half, dtype=np.float64)) / half)
    return jnp.asarray(np.tile(inv, 4)[None, :], dtype=F32)


def _rot_tables(pos_ref, f_ref):
    ang = pos_ref[...].astype(F32) * f_ref[...]
    c = jnp.tile(jnp.cos(ang), (1, 4))
    s = jnp.tile(jnp.sin(ang), (1, 4))
    lane = lax.broadcasted_iota(jnp.int32, (1, AW), 1)
    first = (lane & 32) == 0
    return c, jnp.where(first, -s, s), first


def _swap_halves(y, first):
    return jnp.where(first, pltpu.roll(y, AW - 32, 1), pltpu.roll(y, 32, 1))


def _qk_prep(proj, pos_col, qg, kg):
    tr = 512

    def body(q_ref, k_ref, pos_ref, f_ref, qg_ref, kg_ref, e_ref, qo_ref, ko_ref):
        c, s_signed, first = _rot_tables(pos_ref, f_ref)
        e = e_ref[...]

        def norm_rot(xv, g, scale):
            r = lax.rsqrt(_segsum(xv * xv, e) * (1.0 / HD) + EPS)
            y = (xv * r) * g
            return (y * c + _swap_halves(y, first) * s_signed) * scale

        qo_ref[...] = norm_rot(q_ref[...], qg_ref[...], HD ** -0.5)
        ko_ref[...] = norm_rot(k_ref[...], kg_ref[...], 1.0)

    col = lambda j: _bs((tr, AW), lambda i, j=j: (i, j))
    vec = _bs((1, AW), lambda i: (0, 0))
    out = jax.ShapeDtypeStruct((T, AW), F32)
    return pl.pallas_call(
        body, name="qk_prep", grid=(T // tr,),
        in_specs=[col(0), col(1), _bs((tr, 1), lambda i: (i, 0)), _bs((1, 128), lambda i: (0, 0)), vec, vec,
                  _bs((AW, AW), lambda i: (0, 0))],
        out_specs=[col(0)] * 2, out_shape=[out] * 2, compiler_params=_cp("parallel"),
    )(proj, proj, pos_col, _freq_row(), qg, kg, _head_ones())


def _qk_bwd(proj, pos_col, qg, kg, dq, dk, dv):
    tr = 256

    def body(q_ref, k_ref, pos_ref, f_ref, qg_ref, kg_ref, e_ref, dq_ref, dk_ref, dv_ref, o_ref, dqg_ref, dkg_ref):
        i, j = pl.program_id(0), pl.program_id(1)

        @pl.when((i == 0) & (j == 0))
        def _():
            dqg_ref[...] = jnp.zeros_like(dqg_ref)
            dkg_ref[...] = jnp.zeros_like(dkg_ref)

        def norm_rot_bwd(x_ref, g_ref, dg_ref, d_ref, scale):
            c, s_signed, first = _rot_tables(pos_ref, f_ref)
            e = e_ref[...]
            dout = d_ref[...] * scale
            dy = dout * c + _swap_halves(dout * s_signed, first)
            xv, g = x_ref[...], g_ref[...]
            r = lax.rsqrt(_segsum(xv * xv, e) * (1.0 / HD) + EPS)
            gdy = g * dy
            dx = r * gdy - xv * ((r * r * r) * (_segsum(xv * gdy, e) * (1.0 / HD)))
            o_ref[...] = dx.astype(BF16)
            dg_ref[...] += jnp.sum(dy * (xv * r), axis=0, keepdims=True)

        @pl.when(j == 0)
        def _():
            norm_rot_bwd(q_ref, qg_ref, dqg_ref, dq_ref, HD ** -0.5)

        @pl.when(j == 1)
        def _():
            norm_rot_bwd(k_ref, kg_ref, dkg_ref, dk_ref, 1.0)

        @pl.when(j == 2)
        def _():
            o_ref[...] = dv_ref[...].astype(BF16)

    col = lambda jj: _bs((tr, AW), lambda i, j, jj=jj: (i, jj))
    vec = _bs((1, AW), lambda i, j: (0, 0))
    piece = _bs((tr, AW), lambda i, j: (i, 0))
    return pl.pallas_call(
        body, name="qk_bwd", grid=(T // tr, 3),
        in_specs=[col(0), col(1), _bs((tr, 1), lambda i, j: (i, 0)), _bs((1, 128), lambda i, j: (0, 0)), vec, vec,
                  _bs((AW, AW), lambda i, j: (0, 0))] + [piece] * 3,
        out_specs=[_bs((tr, AW), lambda i, j: (i, j)), vec, vec],
        out_shape=[jax.ShapeDtypeStruct((T, INW), BF16), jax.ShapeDtypeStruct((1, AW), F32),
                   jax.ShapeDtypeStruct((1, AW), F32)],
        compiler_params=_cp("arbitrary", "arbitrary"),
    )(proj, proj, pos_col, _freq_row(), qg, kg, _head_ones(), dq, dk, dv)


RG = 256


def _stacked_band_mask():
    qi = lax.broadcasted_iota(jnp.int32, (2 * BLK, 2 * BLK), 0) & (BLK - 1)
    kj = lax.broadcasted_iota(jnp.int32, (2 * BLK, 2 * BLK), 1)
    rel = qi - kj + BLK
    return (rel >= 0) & (rel <= BLK), kj >= BLK


def _natural_rows(r0, n_rows, d):
    if d == 1:
        return pl.ds(r0, n_rows)
    ln = T // d
    return pl.ds(r0 // ln + d * (r0 % ln), n_rows, stride=d)


def _regroup_into(dst, src_ref, d, pad, cast=True):
    def step(j, carry):
        r0 = pl.multiple_of(j * RG, RG)
        val = src_ref[_natural_rows(r0, RG, d), :]
        dst[pl.ds(pad + r0, RG), :] = val.astype(dst.dtype) if cast else val
        return carry
    lax.fori_loop(0, T // RG, step, 0)


def _stack_heads(x, h0):
    zero = jnp.zeros_like(x)
    return jnp.concatenate([jnp.where(h0, x, zero), jnp.where(h0, zero, x)], axis=0)


def _attn_fwd(q, k, proj):
    nblk = T // BLK

    def body(q_ref, k_ref, v_ref, a_ref, lse_ref, qs, ks, vs, o0, o1, o2, l0, l1, l2):
        band, cur_half = _stacked_band_mask()
        h0 = lax.broadcasted_iota(jnp.int32, (1, 128), 1) < HD
        ks[0:BLK, :] = jnp.zeros((BLK, 128), BF16)
        vs[0:BLK, :] = jnp.zeros((BLK, 128), BF16)
        for d, o_s, l_s in zip(DILATIONS, (o0, o1, o2), (l0, l1, l2)):
            nb = T // d // BLK
            _regroup_into(qs, q_ref, d, 0)
            _regroup_into(ks, k_ref, d, BLK)
            _regroup_into(vs, v_ref, d, BLK)

            def step(b, carry, d=d, nb=nb, o_s=o_s, l_s=l_s):
                r0 = pl.multiple_of(b * BLK, BLK)
                mask = band & (cur_half | ((b & (nb - 1)) > 0))
                kk = ks[pl.ds(r0, 2 * BLK), :]
                vv = vs[pl.ds(r0, 2 * BLK), :]
                s = jnp.where(mask, _dot(_stack_heads(qs[pl.ds(r0, BLK), :], h0), kk, NT), NEG)
                m = jnp.max(s, axis=1, keepdims=True)
                p = jnp.exp(s - m)
                l = jnp.sum(p, axis=1, keepdims=True)
                o = _dot(p.astype(BF16), vv) / l
                lse = m + jnp.log(l)
                rows = _natural_rows(r0, BLK, d)
                o_s[rows, :] = jnp.where(h0, o[0:BLK, :], o[BLK:, :])
                l_s[rows, :] = jnp.where(h0, lse[0:BLK, :], lse[BLK:, :])
                return carry

            lax.fori_loop(0, nblk, step, 0, unroll=2)

        def merge(i, carry):
            r = pl.ds(pl.multiple_of(i * RG, RG), RG)
            la, lb, lc = l0[r, :], l1[r, :], l2[r, :]
            m = jnp.maximum(jnp.maximum(la, lb), lc)
            ea, eb, ec = jnp.exp(la - m), jnp.exp(lb - m), jnp.exp(lc - m)
            z = (ea + eb) + ec
            a_ref[r, :] = ((ea * o0[r, :] + eb * o1[r, :]) + ec * o2[r, :]) / z
            lse_ref[r, :] = m + jnp.log(z)
            return carry

        lax.fori_loop(0, T // RG, merge, 0)

    spec = lambda cb: _bs((T, 128), lambda p, cb=cb: (0, cb + p))
    out = jax.ShapeDtypeStruct((T, AW), F32)
    return pl.pallas_call(
        body, name="attn_fwd", grid=(AW // 128,), in_specs=[spec(0), spec(0), spec(8)], out_specs=[spec(0)] * 2,
        out_shape=[out] * 2,
        scratch_shapes=[pltpu.VMEM((T, 128), BF16), pltpu.VMEM((T + BLK, 128), BF16), pltpu.VMEM((T + BLK, 128), BF16)]
        + [pltpu.VMEM((T, 128), F32)] * 6,
        compiler_params=_cp("parallel"),
    )(q, k, proj)


def _attn_bwd(q, k, proj, do, lse, delta):
    nblk = T // BLK

    def body(q_ref, k_ref, v_ref, do_ref, l_ref, dl_ref, dq_ref, dk_ref, dv_ref, qs, dos, ks, vs, ls, dls, dks, dvs):
        band, cur_half = _stacked_band_mask()
        h0 = lax.broadcasted_iota(jnp.int32, (1, 128), 1) < HD
        ks[0:BLK, :] = jnp.zeros((BLK, 128), BF16)
        vs[0:BLK, :] = jnp.zeros((BLK, 128), BF16)
        for d in DILATIONS:
            nb = T // d // BLK
            _regroup_into(qs, q_ref, d, 0)
            _regroup_into(dos, do_ref, d, 0)
            _regroup_into(ks, k_ref, d, BLK)
            _regroup_into(vs, v_ref, d, BLK)
            _regroup_into(ls, l_ref, d, 0, cast=False)
            _regroup_into(dls, dl_ref, d, 0, cast=False)
            dks[...] = jnp.zeros_like(dks)
            dvs[...] = jnp.zeros_like(dvs)

            def step(b, carry, d=d, nb=nb):
                r0 = pl.multiple_of(b * BLK, BLK)
                mask = band & (cur_half | ((b & (nb - 1)) > 0))
                win = pl.ds(r0, 2 * BLK)
                kk, vv = ks[win, :], vs[win, :]
                q2 = _stack_heads(qs[pl.ds(r0, BLK), :], h0)
                do2 = _stack_heads(dos[pl.ds(r0, BLK), :], h0)
                lv, dlv = ls[pl.ds(r0, BLK), :], dls[pl.ds(r0, BLK), :]
                lse2 = jnp.concatenate([lv[:, 0:1], lv[:, HD:HD + 1]], axis=0)
                dl2 = jnp.concatenate([dlv[:, 0:1], dlv[:, HD:HD + 1]], axis=0)
                s = jnp.where(mask, _dot(q2, kk, NT), NEG)
                p = jnp.exp(s - lse2)
                ds = p * (_dot(do2, vv, NT) - dl2)
                pb, dsb = p.astype(BF16), ds.astype(BF16)
                dq2 = _dot(dsb, kk)
                dks[win, :] += _dot(dsb, q2, TN)
                dvs[win, :] += _dot(pb, do2, TN)
                rows = _natural_rows(r0, BLK, d)
                dq = jnp.where(h0, dq2[0:BLK, :], dq2[BLK:, :])
                dq_ref[rows, :] = dq if d == 1 else dq_ref[rows, :] + dq
                return carry

            lax.fori_loop(0, nblk, step, 0, unroll=2)

            def back(j, carry, d=d):
                r0 = pl.multiple_of(j * RG, RG)
                rows = _natural_rows(r0, RG, d)
                src = pl.ds(BLK + r0, RG)
                dk_ref[rows, :] = dks[src, :] if d == 1 else dk_ref[rows, :] + dks[src, :]
                dv_ref[rows, :] = dvs[src, :] if d == 1 else dv_ref[rows, :] + dvs[src, :]
                return carry

            lax.fori_loop(0, T // RG, back, 0)

    spec = lambda cb: pl.BlockSpec((T, 128), lambda p, cb=cb: (0, cb + p), pipeline_mode=pl.Buffered(1))
    ospec = _bs((T, 128), lambda p: (0, p))
    out = jax.ShapeDtypeStruct((T, AW), F32)
    return pl.pallas_call(
        body, name="attn_bwd", grid=(AW // 128,), in_specs=[spec(0), spec(0), spec(8), spec(0), spec(0), spec(0)],
        out_specs=[ospec] * 3, out_shape=[out] * 3,
        scratch_shapes=[pltpu.VMEM((T, 128), BF16), pltpu.VMEM((T, 128), BF16), pltpu.VMEM((T + BLK, 128), BF16),
                        pltpu.VMEM((T + BLK, 128), BF16), pltpu.VMEM((T, 128), F32), pltpu.VMEM((T, 128), F32),
                        pltpu.VMEM((T + BLK, 128), F32), pltpu.VMEM((T + BLK, 128), F32)],
        compiler_params=_cp("parallel"),
    )(q, k, proj, do, lse, delta)


def _attn_norm(attn, g_attn):
    tr = 512

    def body(a_ref, g_ref, mix_ref):
        attn = a_ref[...]
        r = lax.rsqrt(jnp.mean(attn * attn, axis=-1, keepdims=True) + EPS)
        mix_ref[...] = ((attn * r) * g_ref[...]).astype(BF16)

    row = _bs((tr, AW), lambda i: (i, 0))
    return pl.pallas_call(
        body, name="attn_norm", grid=(T // tr,), in_specs=[row, _bs((1, AW), lambda i: (0, 0))],
        out_specs=row, out_shape=jax.ShapeDtypeStruct((T, D), BF16), compiler_params=_cp("parallel"),
    )(attn, g_attn)


def _attn_out_bwd(attn, dmix, g_attn):
    tr = 256

    def body(a_ref, d_ref, g_ref, e_ref, do_ref, dl_ref, dg_ref):
        av, dyv = a_ref[...], d_ref[...]
        r = lax.rsqrt(jnp.mean(av * av, axis=-1, keepdims=True) + EPS)
        gdy = g_ref[...] * dyv
        da = r * gdy - av * ((r * r * r) * jnp.mean(av * gdy, axis=-1, keepdims=True))
        do_ref[...] = da
        dl_ref[...] = _segsum(da * av, e_ref[...])

        @pl.when(pl.program_id(0) == 0)
        def _():
            dg_ref[...] = jnp.zeros_like(dg_ref)

        dg_ref[...] += jnp.sum(dyv * (av * r), axis=0, keepdims=True)

    row = _bs((tr, AW), lambda i: (i, 0))
    vec = _bs((1, AW), lambda i: (0, 0))
    return pl.pallas_call(
        body, name="attn_out_bwd", grid=(T // tr,), in_specs=[row, row, vec, _bs((AW, AW), lambda i: (0, 0))],
        out_specs=[row, row, vec],
        out_shape=[jax.ShapeDtypeStruct((T, AW), F32), jax.ShapeDtypeStruct((T, AW), F32),
                   jax.ShapeDtypeStruct((1, AW), F32)],
        compiler_params=_cp("arbitrary"),
    )(attn, dmix, g_attn, _head_ones())


TRR = 256


def _scan_fwd(a, u):
    n = a.shape[0]
    row = lax.broadcasted_iota(jnp.int32, (n, 1), 0)
    s = 1
    while s < n:
        keep = row >= s
        u = jnp.where(keep, a * pltpu.roll(u, s, 0) + u, u)
        a = jnp.where(keep, a * pltpu.roll(a, s, 0), a)
        s *= 2
    return a, u


def _scan_bwd(c, w):
    n = c.shape[0]
    row = lax.broadcasted_iota(jnp.int32, (n, 1), 0)
    s = 1
    while s < n:
        keep = row < n - s
        w = jnp.where(keep, c * pltpu.roll(w, n - s, 0) + w, w)
        c = jnp.where(keep, c * pltpu.roll(c, n - s, 0), c)
        s *= 2
    return w


def _gates(xc, wrg, wig, brg, big, sp):
    xcb = xc.astype(BF16)
    r = jax.nn.sigmoid(_dot(xcb, wrg) + brg)
    ig = jax.nn.sigmoid(_dot(xcb, wig) + big)
    la = (-LRU_C * r) * sp
    a = jnp.exp(la)
    mult = jnp.sqrt(-jnp.tanh(la) * (a * a + 1.0))
    return r, ig, a, mult


def _conv4(ext_ref, xr, cw_ref, cb_ref, n):
    y = cb_ref[...] + ext_ref[pl.ds(5, n), :] * cw_ref[0:1, :]
    y = y + ext_ref[pl.ds(6, n), :] * cw_ref[1:2, :]
    y = y + ext_ref[pl.ds(7, n), :] * cw_ref[2:3, :]
    return y + xr * cw_ref[3:4, :]


def _rec_fwd(proj, mix, cw, cb, wrg, wig, brg, big, lam, g_rec):
    n = TRR

    def body(xr_ref, gr_ref, cw_ref, cb_ref, wrg_ref, wig_ref, brg_ref, big_ref, lam_ref, g_ref, mix_in,
             mix_ref, h_ref, ext, hcar):
        del mix_in

        @pl.when(pl.program_id(0) == 0)
        def _():
            ext[0:8, :] = jnp.zeros((8, RW), F32)
            hcar[...] = jnp.zeros_like(hcar)

        xr = xr_ref[...]
        ext[8:, :] = xr
        xc = _conv4(ext, xr, cw_ref, cb_ref, n)
        ext[0:8, :] = xr[n - 8:, :]
        sp = _softplus(-lam_ref[...])
        _, ig, a, mult = _gates(xc, wrg_ref[...], wig_ref[...], brg_ref[...], big_ref[...], sp)
        a_s, u_s = _scan_fwd(a, mult * (ig * xc))
        h = u_s + a_s * hcar[7:8, :]
        h_ref[...] = h
        hcar[...] = h[n - 8:, :]
        pre = h * _gelu(gr_ref[...])
        r = lax.rsqrt(jnp.mean(pre * pre, axis=-1, keepdims=True) + EPS)
        mix_ref[...] = ((pre * r) * g_ref[...]).astype(BF16)

    vec = _bs((1, RW), lambda i: (0, 0))
    mat = _bs((RW, RW), lambda i: (0, 0))
    return pl.pallas_call(
        body, name="rec_fwd", grid=(T // n,),
        in_specs=[_bs((n, RW), lambda i: (i, 3)), _bs((n, RW), lambda i: (i, 4)), _bs((8, RW), lambda i: (0, 0)), vec,
                  mat, mat, vec, vec, vec, vec, pl.BlockSpec(memory_space=pl.ANY)],
        out_specs=[_bs((n, RW), lambda i: (i, 1)), _bs((n, RW), lambda i: (i, 0))],
        out_shape=[jax.ShapeDtypeStruct((T, D), BF16), jax.ShapeDtypeStruct((T, RW), F32)],
        scratch_shapes=[pltpu.VMEM((n + 8, RW), F32), pltpu.VMEM((8, RW), F32)],
        input_output_aliases={10: 0}, compiler_params=_cp("arbitrary"),
    )(proj, proj, cw, cb, wrg, wig, brg, big, lam, g_rec, mix)


def _rec_bwd(proj, h, dmix, dproj, cw, cb, wrg, wig, brg, big, lam, g_rec):
    n = TRR
    nt = T // n
    hb = n // 8

    def body(xr_ref, xh_ref, gr_ref, h_ref, hh_ref, dm_ref, cw_ref, cb_ref, wrg_ref, wig_ref, brg_ref, big_ref,
             lam_ref, g_ref, dp_in, dp_ref, xc_ref, dr_ref, di_ref, dcw_ref, dcb_ref, dbr_ref, dbi_ref, dsp_ref,
             dg_ref, ext, exth, extd, adh, dgr_s):
        del dp_in
        i, j = pl.program_id(0), pl.program_id(1)
        first_tile = i == nt - 1
        last_tile = i == 0

        @pl.when(j == 0)
        def _():
            @pl.when(last_tile)
            def _():
                for ref in (dcw_ref, dcb_ref, dbr_ref, dbi_ref, dsp_ref, dg_ref):
                    ref[...] = jnp.zeros_like(ref)
                extd[n:, :] = jnp.zeros((8, RW), F32)
                adh[...] = jnp.zeros_like(adh)

            row = lax.broadcasted_iota(jnp.int32, (n, 1), 0)
            xr = xr_ref[...]
            ext[0:8, :] = jnp.where(first_tile, 0.0, xh_ref[...])
            ext[8:, :] = xr
            xc = _conv4(ext, xr, cw_ref, cb_ref, n)
            sp = _softplus(-lam_ref[...])
            wrg, wig = wrg_ref[...], wig_ref[...]
            r, ig, a, mult = _gates(xc, wrg, wig, brg_ref[...], big_ref[...], sp)

            hv = h_ref[...]
            gl, dgl = _gelu_and_grad(gr_ref[...])
            pre = hv * gl
            dyv = dm_ref[...]
            rr = lax.rsqrt(jnp.mean(pre * pre, axis=-1, keepdims=True) + EPS)
            gdy = g_ref[...] * dyv
            dpre = rr * gdy - pre * ((rr * rr * rr) * jnp.mean(pre * gdy, axis=-1, keepdims=True))
            dg_ref[...] += jnp.sum(dyv * (pre * rr), axis=0, keepdims=True)
            dgr_s[...] = dpre * hv * dgl

            is_last_row = row == n - 1
            w = dpre * gl + jnp.where(is_last_row, adh[0:1, :], 0.0)
            c = jnp.where(is_last_row, 0.0, pltpu.roll(a, n - 1, 0))
            dh = _scan_bwd(c, w)
            adh[...] = (a * dh)[0:8, :]

            exth[0:8, :] = jnp.where(first_tile, 0.0, hh_ref[...])
            exth[8:, :] = hv
            da = dh * exth[pl.ds(7, n), :]
            ixc = ig * xc
            dmult = dh * ixc
            dla = da * a - dmult * ((a * a) / mult)
            dsp_ref[...] += jnp.sum(dla * (-LRU_C * r), axis=0, keepdims=True)
            dpr = (dla * (-LRU_C * sp)) * (r * (1.0 - r))
            dpi = (dh * (mult * xc)) * (ig * (1.0 - ig))
            dprb, dpib = dpr.astype(BF16), dpi.astype(BF16)
            dxc = dh * (mult * ig) + _dot(dprb, wrg, NT) + _dot(dpib, wig, NT)
            dbr_ref[...] += jnp.sum(dpr, axis=0, keepdims=True)
            dbi_ref[...] += jnp.sum(dpi, axis=0, keepdims=True)
            xc_ref[...] = xc.astype(BF16)
            dr_ref[...] = dprb
            di_ref[...] = dpib

            extd[0:n, :] = dxc
            dxr = dxc * cw_ref[3:4, :] + extd[pl.ds(1, n), :] * cw_ref[2:3, :]
            dxr = dxr + extd[pl.ds(2, n), :] * cw_ref[1:2, :] + extd[pl.ds(3, n), :] * cw_ref[0:1, :]
            extd[n:, :] = dxc[0:8, :]
            dcb_ref[...] += jnp.sum(dxc, axis=0, keepdims=True)
            for kk in range(4):
                dcw_ref[kk:kk + 1, :] += jnp.sum(dxc * ext[pl.ds(5 + kk, n), :], axis=0, keepdims=True)

            @pl.when(first_tile)
            def _():
                dsp_ref[...] = dsp_ref[...] * (-jax.nn.sigmoid(-lam_ref[...]))

            dp_ref[...] = dxr.astype(BF16)

        @pl.when(j == 1)
        def _():
            dp_ref[...] = dgr_s[...].astype(BF16)

    vec = _bs((1, RW), lambda i, j: (0, 0))
    mat = _bs((RW, RW), lambda i, j: (0, 0))
    tile = lambda cblk: _bs((n, RW), lambda i, j, cblk=cblk: (nt - 1 - i, cblk))
    halo = lambda cblk: _bs((8, RW), lambda i, j, cblk=cblk: (jnp.maximum((nt - 1 - i) * hb - 1, 0), cblk))
    bt = jax.ShapeDtypeStruct((T, RW), BF16)
    v = jax.ShapeDtypeStruct((1, RW), F32)
    return pl.pallas_call(
        body, name="rec_bwd", grid=(nt, 2),
        in_specs=[tile(3), halo(3), tile(4), tile(0), halo(0), tile(1), _bs((8, RW), lambda i, j: (0, 0)), vec,
                  mat, mat, vec, vec, vec, vec, pl.BlockSpec(memory_space=pl.ANY)],
        out_specs=[_bs((n, RW), lambda i, j: (nt - 1 - i, 3 + j)), tile(0), tile(0), tile(0),
                   _bs((8, RW), lambda i, j: (0, 0)), vec, vec, vec, vec, vec],
        out_shape=[jax.ShapeDtypeStruct((T, INW), BF16), bt, bt, bt, jax.ShapeDtypeStruct((8, RW), F32), v, v, v, v, v],
        scratch_shapes=[pltpu.VMEM((n + 8, RW), F32), pltpu.VMEM((n + 8, RW), F32), pltpu.VMEM((n + 8, RW), F32),
                        pltpu.VMEM((8, RW), F32), pltpu.VMEM((n, RW), F32)],
        input_output_aliases={14: 0}, compiler_params=_cp("arbitrary", "arbitrary"),
    )(proj, proj, proj, h, h, dmix, cw, cb, wrg, wig, brg, big, lam, g_rec, dproj)


FC = 1536
TRF = 256


def _conv3(ext_ref, w_ref, b_ref, n):
    y = b_ref[...] + ext_ref[pl.ds(6, n), :] * w_ref[0:1, :]
    y = y + ext_ref[pl.ds(7, n), :] * w_ref[1:2, :]
    return y + ext_ref[pl.ds(8, n), :] * w_ref[2:3, :]


def _ffn_act(up_pre, cw, cb):
    n = TRF
    hb = n // 8

    def body(g_ref, gh_ref, u_ref, uh_ref, wg_ref, wu_ref, bg_ref, bu_ref, o_ref, extg, extu):
        first = pl.program_id(1) == 0
        extg[0:8, :] = jnp.where(first, 0.0, gh_ref[...])
        extg[8:, :] = g_ref[...]
        extu[0:8, :] = jnp.where(first, 0.0, uh_ref[...])
        extu[8:, :] = u_ref[...]
        o_ref[...] = (_gelu(_conv3(extg, wg_ref, bg_ref, n)) * _conv3(extu, wu_ref, bu_ref, n)).astype(BF16)

    main = lambda o: _bs((n, FC), lambda j, i, o=o: (i, 2 * j + o))
    halo = lambda o: _bs((8, FC), lambda j, i, o=o: (jnp.maximum(i * hb - 1, 0), 2 * j + o))
    wsp = lambda o: _bs((None, 8, FC), lambda j, i, o=o: (2 * j + o, 0, 0))
    bsp = lambda o: _bs((1, FC), lambda j, i, o=o: (0, 2 * j + o))
    return pl.pallas_call(
        body, name="ffn_act", grid=(2, T // n),
        in_specs=[main(0), halo(0), main(1), halo(1), wsp(0), wsp(1), bsp(0), bsp(1)],
        out_specs=_bs((n, FC), lambda j, i: (i, j)), out_shape=jax.ShapeDtypeStruct((T, DFF), BF16),
        scratch_shapes=[pltpu.VMEM((n + 8, FC), F32)] * 2, compiler_params=_cp("parallel", "parallel"),
    )(up_pre, up_pre, up_pre, up_pre, cw, cw, cb, cb)


def _ffn_bwd(up_pre, dact, cw, cb):
    n = TRF
    hb = n // 8
    nt = T // n
    m = n + 8

    def body(g_ref, gp_ref, gn_ref, u_ref, up_ref, un_ref, d_ref, dn_ref, wg_ref, wu_ref, bg_ref, bu_ref,
             o_ref, dw_ref, db_ref, extg, extu, extd, dug_s, duu_s):
        i = pl.program_id(1)
        first, last = i == 0, i == nt - 1

        @pl.when(first)
        def _():
            dw_ref[...] = jnp.zeros_like(dw_ref)
            db_ref[...] = jnp.zeros_like(db_ref)

        extg[0:8, :] = jnp.where(first, 0.0, gp_ref[...])
        extg[8:n + 8, :] = g_ref[...]
        extg[n + 8:, :] = gn_ref[...]
        extu[0:8, :] = jnp.where(first, 0.0, up_ref[...])
        extu[8:n + 8, :] = u_ref[...]
        extu[n + 8:, :] = un_ref[...]
        extd[0:n, :] = d_ref[...]
        extd[n:, :] = jnp.where(last, 0.0, dn_ref[...])
        gl, dgl = _gelu_and_grad(_conv3(extg, wg_ref, bg_ref, m))
        uu = _conv3(extu, wu_ref, bu_ref, m)
        dv = extd[...]
        dug_s[...] = dv * uu * dgl
        duu_s[...] = dv * gl

        def conv_t(s_ref, w_ref):
            y = s_ref[pl.ds(0, n), :] * w_ref[2:3, :] + s_ref[pl.ds(1, n), :] * w_ref[1:2, :]
            return y + s_ref[pl.ds(2, n), :] * w_ref[0:1, :]

        o_ref[:, 0:FC] = conv_t(dug_s, wg_ref).astype(BF16)
        o_ref[:, FC:] = conv_t(duu_s, wu_ref).astype(BF16)
        dg0, du0 = dug_s[pl.ds(0, n), :], duu_s[pl.ds(0, n), :]
        db_ref[:, 0:FC] += jnp.sum(dg0, axis=0, keepdims=True)
        db_ref[:, FC:] += jnp.sum(du0, axis=0, keepdims=True)
        for kk in range(3):
            dw_ref[kk:kk + 1, 0:FC] += jnp.sum(dg0 * extg[pl.ds(6 + kk, n), :], axis=0, keepdims=True)
            dw_ref[kk:kk + 1, FC:] += jnp.sum(du0 * extu[pl.ds(6 + kk, n), :], axis=0, keepdims=True)

    main = lambda o: _bs((n, FC), lambda j, i, o=o: (i, 2 * j + o))
    prev = lambda o: _bs((8, FC), lambda j, i, o=o: (jnp.maximum(i * hb - 1, 0), 2 * j + o))
    nxt = lambda o: _bs((8, FC), lambda j, i, o=o: (jnp.minimum((i + 1) * hb, T // 8 - 1), 2 * j + o))
    wsp = lambda o: _bs((None, 8, FC), lambda j, i, o=o: (2 * j + o, 0, 0))
    bsp = lambda o: _bs((1, FC), lambda j, i, o=o: (0, 2 * j + o))
    return pl.pallas_call(
        body, name="ffn_bwd", grid=(2, nt),
        in_specs=[main(0), prev(0), nxt(0), main(1), prev(1), nxt(1), _bs((n, FC), lambda j, i: (i, j)),
                  _bs((8, FC), lambda j, i: (jnp.minimum((i + 1) * hb, T // 8 - 1), j)), wsp(0), wsp(1), bsp(0), bsp(1)],
        out_specs=[_bs((n, 2 * FC), lambda j, i: (i, j)), _bs((8, 2 * FC), lambda j, i: (0, j)),
                   _bs((1, 2 * FC), lambda j, i: (0, j))],
        out_shape=[jax.ShapeDtypeStruct((T, 2 * DFF), BF16), jax.ShapeDtypeStruct((8, 2 * DFF), F32),
                   jax.ShapeDtypeStruct((1, 2 * DFF), F32)],
        scratch_shapes=[pltpu.VMEM((n + 16, FC), F32), pltpu.VMEM((n + 16, FC), F32), pltpu.VMEM((m, FC), F32),
                        pltpu.VMEM((m, FC), F32), pltpu.VMEM((m, FC), F32)],
        compiler_params=_cp("parallel", "arbitrary"),
    )(up_pre, up_pre, up_pre, up_pre, up_pre, up_pre, dact, dact, cw, cw, cb, cb)


def _down_loss(act, w_down, x1, target):
    tm, tn = 512, 512

    def body(a_ref, b_ref, r_ref, t_ref, dy_ref, dyb_ref, l_ref):
        @pl.when((pl.program_id(0) == 0) & (pl.program_id(1) == 0))
        def _():
            l_ref[...] = jnp.zeros_like(l_ref)

        err = (r_ref[...] + _dot(a_ref[...], b_ref[...])) - t_ref[...]
        dy = err * (1.0 / D)
        dy_ref[...] = dy
        dyb_ref[...] = dy.astype(BF16)
        l_ref[...] += jnp.sum(0.5 * (err * err) * (1.0 / D))

    o_spec = _bs((tm, tn), lambda j, i: (i, j))
    return pl.pallas_call(
        body, name="down_loss", grid=(D // tn, T // tm),
        in_specs=[_bs((tm, DFF), lambda j, i: (i, 0)), _bs((DFF, tn), lambda j, i: (0, j)), o_spec, o_spec],
        out_specs=[o_spec, o_spec, _bs((8, 128), lambda j, i: (0, 0))],
        out_shape=[jax.ShapeDtypeStruct((T, D), F32), jax.ShapeDtypeStruct((T, D), BF16),
                   jax.ShapeDtypeStruct((8, 128), F32)],
        compiler_params=_cp("arbitrary", "arbitrary"),
    )(act, w_down, x1, target)


def _block_diag(w):
    eye = jnp.eye(8, dtype=w.dtype)
    return (w[:, :, None, :] * eye[:, None, :, None]).reshape(RW, RW).astype(BF16)


def _diag_blocks(m):
    return jnp.stack([m[HD * b:HD * (b + 1), HD * b:HD * (b + 1)] for b in range(8)])


def _local_step(x, pos_col, target, p, exch):
    qg, kg = jnp.tile(p["q_norm_g"], (1, 8)), jnp.tile(p["k_norm_g"], (1, 8))
    wrg, wig = _block_diag(p["w_rg"]), _block_diag(p["w_ig"])
    brg, big = p["b_rg"].reshape(1, RW), p["b_ig"].reshape(1, RW)

    h1 = _rms_fwd("rms1", x, p["g_mix"])
    proj = _mm("mm_in", h1, p["w_in"], "nn", 512, 640, stack=NCHIP, after=exch.start_rest())
    q, k = _qk_prep(proj, pos_col, qg, kg)
    attn, lse = _attn_fwd(q, k, proj)
    mix = _attn_norm(attn, p["g_attn_out"])
    mix, hseq = _rec_fwd(proj, mix, p["rec_conv_w"], p["rec_conv_b"], wrg, wig, brg, big, p["lru_lambda"], p["g_rec_out"])
    rest = exch.wait_rest(mix)
    x1 = _mm("mm_out", mix, rest["w_out"], "nn", 512, 512, res=x)
    h2 = _rms_fwd("rms2", x1, p["g_ffn"])
    up_pre = _mm("mm_up", h2, rest["w_up"], "nn", 512, 768, stack=NCHIP)
    act = _ffn_act(up_pre, p["ffn_conv_w"], p["ffn_conv_b"])
    dy, dyb, loss_blk = _down_loss(act, rest["w_down"], x1, target)

    g = {}
    tok = exch.reduce_start("w_down", *_mm("wg_down", act, dyb, "tn", 512, 512, twin_bf16=True))
    dact = _mm("dg_down", dyb, rest["w_down"], "nt", 512, 512, after=tok)
    dup, g["ffn_conv_w"], g["ffn_conv_b"] = _ffn_bwd(up_pre, dact, p["ffn_conv_w"], p["ffn_conv_b"])
    tok = exch.reduce_start("w_up", *_mm("wg_up", h2, dup, "tn", 512, 768, stack=NCHIP, twin_bf16=True))
    dh2 = _mm("dg_up", dup, rest["w_up"], "nt", 512, 256, stack=NCHIP, after=tok)
    dx1, dx1b, g["g_ffn"] = _rms_bwd("rms2_bwd", x1, p["g_ffn"], dh2, dy, True)
    tok = exch.reduce_start("w_out", *_mm("wg_out", mix, dx1b, "tn", 512, 512, twin_bf16=True))
    dmix = _mm("dg_out", dx1b, rest["w_out"], "nt", 512, 512, after=tok)
    do, delta, g["g_attn_out"] = _attn_out_bwd(attn, dmix, p["g_attn_out"])
    dq, dk, dv = _attn_bwd(q, k, proj, do, lse, delta)
    dproj, dqg, dkg = _qk_bwd(proj, pos_col, qg, kg, dq, dk, dv)
    (dproj, xcb, dprb, dpib, g["rec_conv_w"], g["rec_conv_b"], dbr, dbi, dsp, g["g_rec_out"]) = _rec_bwd(
        proj, hseq, dmix, dproj, p["rec_conv_w"], p["rec_conv_b"], wrg, wig, brg, big, p["lru_lambda"], p["g_rec_out"])
    g["w_rg"] = _diag_blocks(_mm("wg_rg", xcb, dprb, "tn", 512, 512)).reshape(RW, HD)
    g["w_ig"] = _diag_blocks(_mm("wg_ig", xcb, dpib, "tn", 512, 512)).reshape(RW, HD)
    g["b_rg"], g["b_ig"] = dbr.reshape(8, HD), dbi.reshape(8, HD)
    g["lru_lambda"] = dsp
    g["q_norm_g"] = dqg.reshape(8, HD).sum(axis=0, keepdims=True)
    g["k_norm_g"] = dkg.reshape(8, HD).sum(axis=0, keepdims=True)
    tok = exch.reduce_start("w_in", *_mm("wg_in", h1, dproj, "tn", 512, 640, stack=NCHIP, twin_bf16=True))
    dh1 = _mm("dg_in", dproj, p["w_in"], "nt", 512, 512, stack=NCHIP, after=tok)
    grad_x, g["g_mix"] = _rms_bwd("rms1_bwd", x, p["g_mix"], dh1, dx1, False)
    return loss_blk, grad_x, g


ANY = pl.BlockSpec(memory_space=pl.ANY)


def _mesh_pos():
    return lax.axis_index("x"), lax.axis_index("y"), lax.axis_index("c")


def _slot(px, py, perm):
    return 2 * py + px if perm else 2 * px + py


def _other_chips(x, y):
    return [(1 - x, y), (x, 1 - y), (1 - x, 1 - y)]


def _rcopy(src, dst, send, recv, k, to, kr=None):
    return pltpu.make_async_remote_copy(src_ref=src, dst_ref=dst, send_sem=send.at[k],
                                        recv_sem=recv.at[k if kr is None else kr], device_id=to, device_id_type=MESH)


def _cast_bf16(name, w):
    r, c = w.shape
    tr = 128
    def body(w_ref, o_ref):
        o_ref[...] = w_ref[...].astype(BF16)
    return pl.pallas_call(
        body, name=name, grid=(r // tr,), in_specs=[_bs((tr, c), lambda i: (i, 0))],
        out_specs=_bs((tr, c), lambda i: (i, 0)), out_shape=jax.ShapeDtypeStruct((r, c), BF16),
        compiler_params=_cp("parallel"),
    )(w)


def _gather_weights(big, small, slot):
    nb, ns = len(big), len(small)
    perms = [p for _, p in big] + [p for _, p in small]

    def body(*refs):
        ins, outs = refs[:nb + ns], refs[2 * (nb + ns):3 * (nb + ns)]
        send, recv = refs[3 * (nb + ns):]
        x, y, c = _mesh_pos()
        me, sib = (x, y, c), (x, y, 1 - c)
        chips = _other_chips(x, y)
        first = []
        for a in range(nb):
            for j, (px, py) in enumerate(chips):
                first.append(_rcopy(ins[a].at[c], outs[a].at[_slot(x, y, perms[a]), c], send, recv, 3 * a + j, (px, py, c)))
        for t in range(ns):
            a = nb + t
            for j, (px, py) in enumerate(chips):
                first.append(_rcopy(ins[a], outs[a].at[_slot(x, y, perms[a])], send, recv, 6 * nb + 3 * t + j, (px, py, c)))
        for cp in first:
            cp.start()
        passed = []
        for a in range(nb):
            for j, (px, py) in enumerate(chips):
                got = outs[a].at[_slot(px, py, perms[a]), c]
                _rcopy(got, got, send, recv, 3 * a + j, me).wait_recv()
                fwd = _rcopy(got, got, send, recv, 3 * nb + 3 * a + j, sib)
                fwd.start()
                passed.append(fwd)
        for a in range(nb):
            for j, (px, py) in enumerate(chips):
                got = outs[a].at[_slot(px, py, perms[a]), 1 - c]
                _rcopy(got, got, send, recv, 3 * nb + 3 * a + j, me).wait_recv()
        for t in range(ns):
            a = nb + t
            for j, (px, py) in enumerate(chips):
                got = outs[a].at[_slot(px, py, perms[a])]
                _rcopy(got, got, send, recv, 6 * nb + 3 * t + j, me).wait_recv()
        for cp in first + passed:
            cp.wait_send()

    arrs = [a for a, _ in big] + [a for a, _ in small]
    lands = [lax.dynamic_update_slice(lax.empty((NCHIP,) + a.shape, a.dtype), a[None], (slot[p],) + (0,) * a.ndim)
             for a, p in zip(arrs, perms)]
    nsem = 6 * nb + 3 * ns
    return pl.pallas_call(
        body, name="gather_weights", in_specs=[ANY] * (2 * (nb + ns)), out_specs=[ANY] * (nb + ns),
        out_shape=[jax.ShapeDtypeStruct(a.shape, a.dtype) for a in lands],
        input_output_aliases={nb + ns + i: i for i in range(nb + ns)},
        scratch_shapes=[pltpu.SemaphoreType.DMA((nsem,)), pltpu.SemaphoreType.DMA((nsem,))],
    )(*arrs, *lands)


HBM = pl.BlockSpec(memory_space=pltpu.HBM)
SEM = pl.BlockSpec(memory_space=pltpu.SEMAPHORE)
EFFECT = pltpu.SideEffectType.DATAFLOW_SIDE_EFFECTING


def _split_start(name, srcs, lands, plan, nsem):
    ns, nl = len(srcs), len(lands)

    def body(*refs):
        send, recv = refs[ns + nl], refs[ns + nl + 1]
        sends, _ = plan(refs[:ns], refs[ns:ns + nl], send, recv)
        for cp in sends:
            cp.start()
        refs[-1][...] = jnp.zeros((8, 128), F32)

    arrs = list(srcs) + list(lands)
    out = pl.pallas_call(
        body, name=name, in_specs=[HBM] * (ns + nl),
        out_specs=[SEM, SEM] + [HBM] * (ns + nl) + [pl.BlockSpec(memory_space=pltpu.VMEM)],
        out_shape=[pltpu.SemaphoreType.DMA((nsem,)), pltpu.SemaphoreType.DMA((nsem,))]
        + [pltpu.HBM(a.shape, a.dtype) for a in arrs] + [jax.ShapeDtypeStruct((8, 128), F32)],
        input_output_aliases={i: 2 + i for i in range(ns + nl)},
        compiler_params=pltpu.CompilerParams(has_side_effects=EFFECT),
    )(*[pltpu.with_memory_space_constraint(a, pltpu.HBM) for a in arrs])
    return out[0], out[1], out[2:2 + ns], out[2 + ns:2 + ns + nl], out[-1]


def _split_wait(name, send, recv, srcs, lands, plan, after):
    ns, nl = len(srcs), len(lands)

    def body(*refs):
        sends, recvs = plan(refs[:ns], refs[ns:ns + nl], refs[ns + nl], refs[ns + nl + 1])
        for cp in sends:
            cp.wait_send()
        for cp in recvs:
            cp.wait_recv()

    arrs = list(srcs) + list(lands)
    out = pl.pallas_call(
        body, name=name, in_specs=[HBM] * (ns + nl) + [SEM, SEM, ANY], out_specs=[HBM] * (ns + nl),
        out_shape=[pltpu.HBM(a.shape, a.dtype) for a in arrs],
        input_output_aliases={i: i for i in range(ns + nl)},
        compiler_params=pltpu.CompilerParams(has_side_effects=EFFECT),
    )(*arrs, send, recv, after)
    return out[ns:]


def _gather_plan(perms):
    def plan(srcs, lands, send, recv):
        x, y, c = _mesh_pos()
        sends, recvs = [], []
        for a, perm in enumerate(perms):
            for j, (px, py) in enumerate(_other_chips(x, y)):
                for cc in (0, 1):
                    k = 6 * a + 2 * j + cc
                    sends.append(_rcopy(srcs[a].at[c], lands[a].at[_slot(x, y, perm), c], send, recv, k, (px, py, cc),
                                        kr=6 * a + 2 * j + c))
                    got = lands[a].at[_slot(px, py, perm), cc]
                    recvs.append(_rcopy(got, got, send, recv, k, (x, y, c)))
        return sends, recvs
    return plan


def _reduce_plan(perm):
    def plan(srcs, lands, send, recv):
        x, y, c = _mesh_pos()
        src, land = srcs[0], lands[0]
        sends = []
        for j, (px, py) in enumerate(_other_chips(x, y)):
            for hf in (0, 1):
                sends.append(_rcopy(src.at[_slot(px, py, perm), hf], land.at[2 * j + c], send, recv, 2 * j + hf,
                                    (px, py, hf), kr=2 * j + c))
        sends.append(_rcopy(src.at[_slot(x, y, perm), 1 - c], land.at[6], send, recv, 6, (x, y, 1 - c)))
        recvs = [_rcopy(land.at[i], land.at[i], send, recv, i, (x, y, c)) for i in range(7)]
        return sends, recvs
    return plan


def _sibling_share(rs):
    na = len(rs)

    def body(*refs):
        ins, outs, (send, recv) = refs[:na], refs[na:2 * na], refs[2 * na:]
        x, y, c = _mesh_pos()
        cps = [_rcopy(ins[a], outs[a], send, recv, a, (x, y, 1 - c)) for a in range(na)]
        for cp in cps:
            cp.start()
        for cp in cps:
            cp.wait()

    return pl.pallas_call(
        body, name="rs_share", in_specs=[ANY] * na, out_specs=[ANY] * na,
        out_shape=[jax.ShapeDtypeStruct(r.shape, F32) for r in rs],
        scratch_shapes=[pltpu.SemaphoreType.DMA((na,)), pltpu.SemaphoreType.DMA((na,))],
    )(*rs)


def _add_pieces(name, g, got, where):
    _, _, r2, cc = g.shape
    tr = 128

    def body(w_ref, g_ref, r_ref, o_ref):
        del w_ref
        acc = g_ref[...]
        for i in range(7):
            acc = acc + r_ref[i].astype(F32)
        o_ref[...] = acc

    return pl.pallas_call(
        body, name=name,
        grid_spec=pltpu.PrefetchScalarGridSpec(
            num_scalar_prefetch=1, grid=(r2 // tr,),
            in_specs=[_bs((None, None, tr, cc), lambda i, w_ref: (w_ref[0], w_ref[1], i, 0)),
                      _bs((7, tr, cc), lambda i, w_ref: (0, i, 0))],
            out_specs=_bs((tr, cc), lambda i, w_ref: (i, 0))),
        out_shape=jax.ShapeDtypeStruct((r2, cc), F32), compiler_params=_cp("parallel"),
    )(where, g, got)


def _adam_math(w, g, m, v):
    m = ADAM_B1 * m + (1.0 - ADAM_B1) * g
    v = ADAM_B2 * v + (1.0 - ADAM_B2) * (g * g)
    m_hat = m / (1.0 - ADAM_B1 ** ADAM_STEP)
    v_hat = v / (1.0 - ADAM_B2 ** ADAM_STEP)
    return -ADAM_LR * (m_hat / (jnp.sqrt(v_hat) + ADAM_EPS) + ADAM_WD * w), m, v


def _adam_big(name, w, g_mine, g_sib, m, v, c_arr):
    r, cols = w.shape
    tr = 128
    per = r // 2 // tr

    def body(c_ref, w_ref, a_ref, b_ref, m_ref, v_ref, g_ref, d_ref, m2_ref, v2_ref):
        g = jnp.where(pl.program_id(0) == c_ref[0], a_ref[...], b_ref[...])
        g_ref[...] = g
        d_ref[...], m2_ref[...], v2_ref[...] = _adam_math(w_ref[...], g, m_ref[...], v_ref[...])

    spec = _bs((tr, cols), lambda h, i, c_ref: (h * per + i, 0))
    half = _bs((tr, cols), lambda h, i, c_ref: (i, 0))
    out = jax.ShapeDtypeStruct((r, cols), F32)
    return pl.pallas_call(
        body, name=name,
        grid_spec=pltpu.PrefetchScalarGridSpec(
            num_scalar_prefetch=1, grid=(2, per), in_specs=[spec, half, half, spec, spec], out_specs=[spec] * 4),
        out_shape=[out] * 4, compiler_params=_cp("parallel", "parallel"),
    )(c_arr, w, g_mine, g_sib, m, v)


_CLASS_SHAPE = {"a": (8, D), "b": (8, RW), "c": (8, 2 * DFF), "d": (1048, HD)}
_SMALL = (
    ("g_mix", "a", 0, 1, D), ("g_ffn", "a", 1, 1, D),
    ("rec_conv_w", "b", 0, 4, RW), ("rec_conv_b", "b", 4, 1, RW), ("lru_lambda", "b", 5, 1, RW),
    ("g_attn_out", "b", 6, 1, RW), ("g_rec_out", "b", 7, 1, RW),
    ("ffn_conv_w", "c", 0, 3, 2 * DFF), ("ffn_conv_b", "c", 3, 1, 2 * DFF),
    ("w_rg", "d", 0, RW, HD), ("w_ig", "d", RW, RW, HD), ("b_rg", "d", 2 * RW, 8, HD), ("b_ig", "d", 2 * RW + 8, 8, HD),
    ("q_norm_g", "d", 2 * RW + 16, 1, HD), ("k_norm_g", "d", 2 * RW + 17, 1, HD),
)
_LOSS_ROW = 2
_CLASSES = ("a", "b", "c", "d")


def _small_allreduce(g, loss_blk):
    names = [s[0] for s in _SMALL]
    nin = len(names) + 1

    def body(*refs):
        ins = dict(zip(names, refs[:len(names)]))
        loss_ref = refs[len(names)]
        outs = dict(zip(_CLASSES, refs[nin:nin + 4]))
        pair = dict(zip(_CLASSES, refs[nin + 4:nin + 8]))
        quad = dict(zip(_CLASSES, refs[nin + 8:nin + 12]))
        send, recv = refs[nin + 12:]
        x, y, c = _mesh_pos()
        chip = 2 * x + y
        pair["a"][c] = jnp.zeros(_CLASS_SHAPE["a"], F32)
        pair["b"][c] = ins["rec_conv_w"][...]
        pair["c"][c] = ins["ffn_conv_w"][...]
        pair["d"][c, 2 * RW + 16:, :] = jnp.zeros((8, HD), F32)
        for name, k, r0, nr, _ in _SMALL:
            if name in ("rec_conv_w", "ffn_conv_w"):
                continue
            pair[k][c, r0:r0 + nr, :] = ins[name][...]
        pair["a"][c, _LOSS_ROW:_LOSS_ROW + 1, :] = jnp.broadcast_to(loss_ref[0:1, 0:1], (1, D))
        cps = [_rcopy(pair[k].at[c], pair[k].at[c], send, recv, ki, (x, y, 1 - c)) for ki, k in enumerate(_CLASSES)]
        for cp in cps:
            cp.start()
        for ki, k in enumerate(_CLASSES):
            _rcopy(pair[k].at[1 - c], pair[k].at[1 - c], send, recv, ki, (x, y, c)).wait_recv()
            quad[k][chip] = pair[k][0] + pair[k][1]
        cps2 = []
        for ki, k in enumerate(_CLASSES):
            for j, (px, py) in enumerate(_other_chips(x, y)):
                cps2.append(_rcopy(quad[k].at[chip], quad[k].at[chip], send, recv, 4 + 3 * ki + j, (px, py, c)))
        for cp in cps2:
            cp.start()
        for ki, k in enumerate(_CLASSES):
            for j, (px, py) in enumerate(_other_chips(x, y)):
                got = quad[k].at[2 * px + py]
                _rcopy(got, got, send, recv, 4 + 3 * ki + j, (x, y, c)).wait_recv()
            outs[k][...] = ((quad[k][0] + quad[k][1]) + quad[k][2]) + quad[k][3]
        for cp in cps + cps2:
            cp.wait_send()

    vm = pl.BlockSpec(memory_space=pltpu.VMEM)
    return pl.pallas_call(
        body, name="small_allreduce", in_specs=[vm] * nin, out_specs=[vm] * 4,
        out_shape=[jax.ShapeDtypeStruct(_CLASS_SHAPE[k], F32) for k in _CLASSES],
        scratch_shapes=[pltpu.VMEM((2,) + _CLASS_SHAPE[k], F32) for k in _CLASSES]
        + [pltpu.VMEM((NCHIP,) + _CLASS_SHAPE[k], F32) for k in _CLASSES]
        + [pltpu.SemaphoreType.DMA((16,)), pltpu.SemaphoreType.DMA((16,))],
        compiler_params=pltpu.CompilerParams(vmem_limit_bytes=VMEM_LIMIT),
    )(*[g[n] for n in names], loss_blk)


def _adam_small(red, w, m, v):
    names = [s[0] for s in _SMALL]
    n = len(names)

    def body(*refs):
        red_refs = dict(zip(_CLASSES, refs[:4]))
        w_refs, m_refs, v_refs = refs[4:4 + n], refs[4 + n:4 + 2 * n], refs[4 + 2 * n:4 + 3 * n]
        loss_ref = refs[4 + 3 * n]
        out_refs = refs[5 + 3 * n:]
        x, y, _ = _mesh_pos()
        chip = 2 * x + y
        loss_ref[...] = jnp.broadcast_to(red_refs["a"][_LOSS_ROW:_LOSS_ROW + 1, 0:1], loss_ref.shape)
        for pi, (name, k, r0, nr, width) in enumerate(_SMALL):
            gfull = red_refs[k][r0:r0 + nr, :]
            if name == "rec_conv_w":
                parts = [gfull[:, 128 * s:128 * (s + 1)] for s in range(NCHIP)]
                g = jnp.where(chip == 0, parts[0], jnp.where(chip == 1, parts[1], jnp.where(chip == 2, parts[2], parts[3])))
            elif name == "ffn_conv_w":
                parts = [gfull[:, FC * s:FC * (s + 1)] for s in range(NCHIP)]
                g = jnp.where(chip == 0, parts[0], jnp.where(chip == 1, parts[2], jnp.where(chip == 2, parts[1], parts[3])))
            elif name == "ffn_conv_b":
                g = jnp.concatenate([gfull[:, FC * s:FC * (s + 1)] for s in (0, 2, 1, 3)], axis=1)
            else:
                g = gfull
            d, m2, v2 = _adam_math(w_refs[pi][...], g, m_refs[pi][...], v_refs[pi][...])
            o = out_refs[4 * pi:4 * pi + 4]
            o[0][...], o[1][...], o[2][...], o[3][...] = g, d, m2, v2

    vm = pl.BlockSpec(memory_space=pltpu.VMEM)
    outs = [jax.ShapeDtypeStruct((1, 128), F32)]
    for name in names:
        outs += [jax.ShapeDtypeStruct(w[name].shape, F32)] * 4
    res = pl.pallas_call(
        body, name="adam_small", in_specs=[vm] * (4 + 3 * n), out_specs=[vm] * len(outs), out_shape=outs,
        compiler_params=pltpu.CompilerParams(vmem_limit_bytes=VMEM_LIMIT),
    )(*red, *[w[k] for k in names], *[m[k] for k in names], *[v[k] for k in names])
    return res[0], {name: res[1 + 4 * i:5 + 4 * i] for i, name in enumerate(names)}


_WEIGHTS = ("g_mix", "w_in", "q_norm_g", "k_norm_g", "rec_conv_w", "rec_conv_b", "w_rg", "b_rg", "w_ig", "b_ig",
            "lru_lambda", "g_attn_out", "g_rec_out", "w_out", "g_ffn", "w_up", "ffn_conv_w", "ffn_conv_b", "w_down")
_BIG = ("w_in", "w_out", "w_up", "w_down")
_BIG_PERM = {"w_in": False, "w_out": False, "w_up": True, "w_down": False}
_SMALL_2D = {"w_rg": (RW, HD), "w_ig": (RW, HD), "b_rg": (8, HD), "b_ig": (8, HD), "rec_conv_w": (4, 128),
             "ffn_conv_w": (3, FC)}


def _halves(a):
    r, c = a.shape
    return a.reshape(2, r // 2, c)


def kernel(x, positions, g_mix, w_in, q_norm_g, k_norm_g, rec_conv_w, rec_conv_b, w_rg, b_rg, w_ig, b_ig, lru_lambda, g_attn_out, g_rec_out, w_out, g_ffn, w_up, ffn_conv_w, ffn_conv_b, w_down, loss_target, m_g_mix, m_w_in, m_q_norm_g, m_k_norm_g, m_rec_conv_w, m_rec_conv_b, m_w_rg, m_b_rg, m_w_ig, m_b_ig, m_lru_lambda, m_g_attn_out, m_g_rec_out, m_w_out, m_g_ffn, m_w_up, m_ffn_conv_w, m_ffn_conv_b, m_w_down, v_g_mix, v_w_in, v_q_norm_g, v_k_norm_g, v_rec_conv_w, v_rec_conv_b, v_w_rg, v_b_rg, v_w_ig, v_b_ig, v_lru_lambda, v_g_attn_out, v_g_rec_out, v_w_out, v_g_ffn, v_w_up, v_ffn_conv_w, v_ffn_conv_b, v_w_down):
    given = dict(g_mix=g_mix, w_in=w_in, q_norm_g=q_norm_g, k_norm_g=k_norm_g, rec_conv_w=rec_conv_w, rec_conv_b=rec_conv_b, w_rg=w_rg, b_rg=b_rg, w_ig=w_ig, b_ig=b_ig, lru_lambda=lru_lambda, g_attn_out=g_attn_out, g_rec_out=g_rec_out, w_out=w_out, g_ffn=g_ffn, w_up=w_up, ffn_conv_w=ffn_conv_w, ffn_conv_b=ffn_conv_b, w_down=w_down)
    given_m = dict(g_mix=m_g_mix, w_in=m_w_in, q_norm_g=m_q_norm_g, k_norm_g=m_k_norm_g, rec_conv_w=m_rec_conv_w, rec_conv_b=m_rec_conv_b, w_rg=m_w_rg, b_rg=m_b_rg, w_ig=m_w_ig, b_ig=m_b_ig, lru_lambda=m_lru_lambda, g_attn_out=m_g_attn_out, g_rec_out=m_g_rec_out, w_out=m_w_out, g_ffn=m_g_ffn, w_up=m_w_up, ffn_conv_w=m_ffn_conv_w, ffn_conv_b=m_ffn_conv_b, w_down=m_w_down)
    given_v = dict(g_mix=v_g_mix, w_in=v_w_in, q_norm_g=v_q_norm_g, k_norm_g=v_k_norm_g, rec_conv_w=v_rec_conv_w, rec_conv_b=v_rec_conv_b, w_rg=v_w_rg, b_rg=v_b_rg, w_ig=v_w_ig, b_ig=v_b_ig, lru_lambda=v_lru_lambda, g_attn_out=v_g_attn_out, g_rec_out=v_g_rec_out, w_out=v_w_out, g_ffn=v_g_ffn, w_up=v_w_up, ffn_conv_w=v_ffn_conv_w, ffn_conv_b=v_ffn_conv_b, w_down=v_w_down)
    shapes = {n: a.shape for n, a in given.items()}

    def two_d(n, a):
        a = a[0]
        return a.reshape(_SMALL_2D[n]) if n in _SMALL_2D else (a if a.ndim == 2 else a[None])

    w = {n: two_d(n, a) for n, a in given.items()}
    m = {n: two_d(n, a) for n, a in given_m.items()}
    v = {n: two_d(n, a) for n, a in given_v.items()}
    cc = lax.axis_index("c").astype(jnp.int32)
    cx, cy = lax.axis_index("x").astype(jnp.int32), lax.axis_index("y").astype(jnp.int32)
    slot = {False: 2 * cx + cy, True: 2 * cy + cx}

    shards = {n: _halves(_cast_bf16(f"cast_{n}", w[n])) for n in _BIG}
    small = [(jnp.pad(w["ffn_conv_w"], ((0, 5), (0, 0))), True), (jnp.pad(w["rec_conv_w"], ((0, 4), (0, 0))), False)]
    f_in, f_fcw, f_rcw = _gather_weights([(shards["w_in"], False)], small, slot)
    p = {n: w[n] for n in ("g_mix", "g_ffn", "q_norm_g", "k_norm_g", "rec_conv_b", "lru_lambda", "g_attn_out", "g_rec_out")}
    p.update(w_rg=w["w_rg"].reshape(8, HD, HD), w_ig=w["w_ig"].reshape(8, HD, HD), b_rg=w["b_rg"], b_ig=w["b_ig"],
             w_in=f_in.reshape(NCHIP, D, INW // NCHIP), ffn_conv_w=f_fcw,
             ffn_conv_b=jnp.concatenate([w["ffn_conv_b"][:, FC * s:FC * (s + 1)] for s in (0, 2, 1, 3)], axis=1),
             rec_conv_w=f_rcw.transpose(1, 0, 2).reshape(8, RW))

    class Exchange:
        rest = ("w_out", "w_up", "w_down")
        order = []
        flight = {}

        def start_rest(self):
            srcs = [shards[n] for n in self.rest]
            lands = [lax.dynamic_update_slice(lax.empty((NCHIP,) + s.shape, BF16), s[None], (slot[_BIG_PERM[n]], 0, 0, 0))
                     for n, s in zip(self.rest, srcs)]
            plan = _gather_plan([_BIG_PERM[n] for n in self.rest])
            send, recv, srcs, lands, token = _split_start("gather_rest_start", srcs, lands, plan, 6 * len(srcs))
            self.flight["rest"] = (send, recv, srcs, lands, plan)
            return (token,)

        def wait_rest(self, after):
            send, recv, srcs, lands, plan = self.flight.pop("rest")
            f_out, f_up, f_down = _split_wait("gather_rest_wait", send, recv, srcs, lands, plan, after)
            return dict(w_out=f_out.reshape(D, D), w_up=f_up.reshape(NCHIP, D, FC), w_down=f_down.reshape(DFF, D))

        def reduce_start(self, name, g32, g16):
            r2, cols = shards[name].shape[1:]
            plan = _reduce_plan(_BIG_PERM[name])
            send, recv, srcs, lands, token = _split_start(
                f"reduce_{name}_start", [g16.reshape(NCHIP, 2, r2, cols)], [lax.empty((7, r2, cols), BF16)], plan, 7)
            self.flight[name] = (send, recv, srcs, lands, plan, g32.reshape(NCHIP, 2, r2, cols))
            self.order.append(name)
            return (token,)

        def finish(self, after):
            mine = {}
            for name in self.order:
                send, recv, srcs, lands, plan, g32 = self.flight.pop(name)
                (got,) = _split_wait(f"reduce_{name}_wait", send, recv, srcs, lands, plan, after)
                where = jnp.stack([slot[_BIG_PERM[name]], cc])
                mine[name] = after = _add_pieces(f"reduce_{name}_add", g32, got, where)
            theirs = dict(zip(_BIG, _sibling_share([mine[n] for n in _BIG])))
            return mine, theirs

    exch = Exchange()

    loss_blk, grad_x, g = _local_step(x[0], positions.reshape(T, 1), loss_target[0], p, exch)

    out_g, out_d, out_m, out_v = {}, {}, {}, {}
    red = _small_allreduce(g, loss_blk)
    loss_row, small_out = _adam_small(red, w, m, v)
    for n, (gn, dn, mn, vn) in small_out.items():
        out_g[n], out_d[n], out_m[n], out_v[n] = gn, dn, mn, vn

    mine, theirs = exch.finish(red[0])
    for n in _BIG:
        out_g[n], out_d[n], out_m[n], out_v[n] = _adam_big(f"adam_{n}", w[n], mine[n], theirs[n], m[n], v[n], cc.reshape(1))

    outs = [loss_row[0, 0], grad_x[None]]
    for group in (out_g, out_d, out_m, out_v):
        outs += [group[n].reshape(shapes[n]) for n in _WEIGHTS]
    return tuple(outs)
```

```python
import math

import jax
import jax.numpy as jnp
import numpy as np
from jax import lax
from jax.experimental import pallas as pl
from jax.experimental.pallas import tpu as pltpu

F32 = jnp.float32
BF16 = jnp.bfloat16

T = 4096
D = 1024
HD = 64
AW = 512
RW = 512
INW = 2560
DFF = 3072
NCHIP = 4
EPS = 1e-6
NEG = -1e30
LRU_C = 8.0
ROPE_THETA = 10000.0
BLK = 128
DILATIONS = (1, 4, 16)
ADAM_LR, ADAM_B1, ADAM_B2, ADAM_EPS, ADAM_WD, ADAM_STEP = 0.001, 0.9, 0.999, 1e-08, 0.01, 10
VMEM_LIMIT = 56 * 1024 * 1024
MESH = pl.DeviceIdType.MESH

NN = (((1,), (0,)), ((), ()))
NT = (((1,), (1,)), ((), ()))
TN = (((0,), (0,)), ((), ()))


def _cp(*sem):
    return pltpu.CompilerParams(dimension_semantics=sem, vmem_limit_bytes=VMEM_LIMIT)


def _bs(shape, fn):
    return pl.BlockSpec(shape, fn)


def _dot(a, b, dims=NN):
    return lax.dot_general(a, b, dims, preferred_element_type=F32)


_GC = math.sqrt(2.0 / math.pi)


def _gelu(x):
    return x * (0.5 * (1.0 + jnp.tanh(_GC * (x + 0.044715 * (x * x * x)))))


def _gelu_and_grad(x):
    x2 = x * x
    th = jnp.tanh(_GC * (x + 0.044715 * (x * x2)))
    cdf = 0.5 * (1.0 + th)
    dg = cdf + 0.5 * x * (1.0 - th * th) * (_GC * (1.0 + 3.0 * 0.044715 * x2))
    return x * cdf, dg


def _softplus(x):
    e = jnp.exp(-jnp.abs(x))
    u = 1.0 + e
    l1p = jnp.where(u == 1.0, e, jnp.log(u) * (e / (u - 1.0)))
    return jnp.maximum(x, 0.0) + l1p


def _segsum(z, e_bf16):
    hi = z.astype(BF16)
    lo = (z - hi.astype(F32)).astype(BF16)
    return _dot(hi, e_bf16) + _dot(lo, e_bf16)


def _mm(name, a, b, mode, tm, tn, out_dtype=F32, res=None, stack=0, twin_bf16=False, after=()):
    if mode == "nn":
        (m, k), n = a.shape, (b.shape[1] if not stack else stack * b.shape[2])
        a_spec = _bs((tm, k), lambda j, i: (i, 0))
        if stack:
            per = b.shape[2] // tn
            b_spec = _bs((None, k, tn), lambda j, i: (j // per, 0, j % per))
        else:
            b_spec = _bs((k, tn), lambda j, i: (0, j))
    elif mode == "nt":
        (m, k), n = a.shape, (b.shape[0] if not stack else b.shape[1])
        a_spec = _bs((tm, k), lambda j, i: (i, 0))
        b_spec = _bs((stack, tn, k // stack), lambda j, i: (0, j, 0)) if stack else _bs((tn, k), lambda j, i: (j, 0))
    else:
        (k, m), n = a.shape, b.shape[1]
        a_spec, b_spec = _bs((k, tm), lambda j, i: (0, i)), _bs((k, tn), lambda j, i: (0, j))
    assert m % tm == 0 and n % tn == 0
    o_spec = _bs((tm, tn), lambda j, i: (i, j))
    o_shape = (m, n)
    if mode == "tn" and stack:
        per = n // stack // tn
        o_spec = _bs((None, tm, tn), lambda j, i: (j // per, i, j % per))
        o_shape = (stack, m, n // stack)
    dims = {"nn": NN, "nt": NT, "tn": TN}[mode]

    def product(a_ref, b_ref):
        if mode == "nt" and stack:
            cs = k // stack
            acc = _dot(a_ref[:, 0:cs], b_ref[0], NT)
            for s in range(1, stack):
                acc = acc + _dot(a_ref[:, s * cs:(s + 1) * cs], b_ref[s], NT)
            return acc
        return _dot(a_ref[...], b_ref[...], dims)

    nres = 0 if res is None else 1

    def body(a_ref, b_ref, *rest):
        acc = product(a_ref, b_ref)
        if nres:
            acc = rest[0][...] + acc
        outs = rest[nres + len(after):]
        outs[0][...] = acc.astype(out_dtype)
        if twin_bf16:
            outs[1][...] = acc.astype(BF16)

    ins = (a, b) + ((res,) if nres else ()) + tuple(after)
    specs = [a_spec, b_spec] + ([o_spec] if nres else []) + [pl.BlockSpec(memory_space=pl.ANY)] * len(after)
    shapes = [jax.ShapeDtypeStruct(o_shape, out_dtype)] + ([jax.ShapeDtypeStruct(o_shape, BF16)] if twin_bf16 else [])
    out = pl.pallas_call(
        body, name=name, grid=(n // tn, m // tm), in_specs=specs, out_specs=[o_spec] * len(shapes),
        out_shape=shapes, compiler_params=_cp("parallel", "parallel"),
    )(*ins)
    return tuple(out) if twin_bf16 else out[0]


def _rms_fwd(name, x, g):
    tr = 512

    def body(x_ref, g_ref, o_ref):
        xv = x_ref[...]
        r = lax.rsqrt(jnp.mean(xv * xv, axis=-1, keepdims=True) + EPS)
        o_ref[...] = ((xv * r) * g_ref[...]).astype(BF16)

    return pl.pallas_call(
        body, name=name, grid=(T // tr,), in_specs=[_bs((tr, D), lambda i: (i, 0)), _bs((1, D), lambda i: (0, 0))],
        out_specs=_bs((tr, D), lambda i: (i, 0)), out_shape=jax.ShapeDtypeStruct((T, D), BF16),
        compiler_params=_cp("parallel"),
    )(x, g)


def _rms_bwd(name, x, g, dy, dres, want_bf16):
    tr = 256

    def body(x_ref, g_ref, dy_ref, dr_ref, dx_ref, *rest):
        dg_ref = rest[-1]
        xv, dyv = x_ref[...], dy_ref[...]
        r = lax.rsqrt(jnp.mean(xv * xv, axis=-1, keepdims=True) + EPS)
        gdy = g_ref[...] * dyv
        dx = r * gdy - xv * ((r * r * r) * jnp.mean(xv * gdy, axis=-1, keepdims=True)) + dr_ref[...]
        dx_ref[...] = dx
        if want_bf16:
            rest[0][...] = dx.astype(BF16)

        @pl.when(pl.program_id(0) == 0)
        def _():
            dg_ref[...] = jnp.zeros_like(dg_ref)

        dg_ref[...] += jnp.sum(dyv * (xv * r), axis=0, keepdims=True)

    row = _bs((tr, D), lambda i: (i, 0))
    vec = _bs((1, D), lambda i: (0, 0))
    outs = [jax.ShapeDtypeStruct((T, D), F32)] + ([jax.ShapeDtypeStruct((T, D), BF16)] if want_bf16 else [])
    return pl.pallas_call(
        body, name=name, grid=(T // tr,), in_specs=[row, vec, row, row],
        out_specs=[row] * len(outs) + [vec], out_shape=outs + [jax.ShapeDtypeStruct((1, D), F32)],
        compiler_params=_cp("arbitrary"),
    )(x, g, dy, dres)


def _head_ones():
    idx = np.arange(AW) // HD
    return jnp.asarray((idx[:, None] == idx[None, :]).astype(np.float32), dtype=BF16)


def _freq_row():
    half = HD // 2
    inv = ROPE_THETA ** (-(np.arange(half, dtype=np.float64)) / half)
    return jnp.asarray(np.tile(inv, 4)[None, :], dtype=F32)


def _rot_tables(pos_ref, f_ref):
    ang = pos_ref[...].astype(F32) * f_ref[...]
    c = jnp.tile(jnp.cos(ang), (1, 4))
    s = jnp.tile(jnp.sin(ang), (1, 4))
    lane = lax.broadcasted_iota(jnp.int32, (1, AW), 1)
    first = (lane & 32) == 0
    return c, jnp.where(first, -s, s), first


def _swap_halves(y, first):
    return jnp.where(first, pltpu.roll(y, AW - 32, 1), pltpu.roll(y, 32, 1))


def _qk_prep(proj, pos_col, qg, kg):
    tr = 512

    def body(q_ref, k_ref, pos_ref, f_ref, qg_ref, kg_ref, e_ref, qo_ref, ko_ref):
        c, s_signed, first = _rot_tables(pos_ref, f_ref)
        e = e_ref[...]

        def norm_rot(xv, g, scale):
            r = lax.rsqrt(_segsum(xv * xv, e) * (1.0 / HD) + EPS)
            y = (xv * r) * g
            return (y * c + _swap_halves(y, first) * s_signed) * scale

        qo_ref[...] = norm_rot(q_ref[...], qg_ref[...], HD ** -0.5)
        ko_ref[...] = norm_rot(k_ref[...], kg_ref[...], 1.0)

    col = lambda j: _bs((tr, AW), lambda i, j=j: (i, j))
    vec = _bs((1, AW), lambda i: (0, 0))
    out = jax.ShapeDtypeStruct((T, AW), F32)
    return pl.pallas_call(
        body, name="qk_prep", grid=(T // tr,),
        in_specs=[col(0), col(1), _bs((tr, 1), lambda i: (i, 0)), _bs((1, 128), lambda i: (0, 0)), vec, vec,
                  _bs((AW, AW), lambda i: (0, 0))],
        out_specs=[col(0)] * 2, out_shape=[out] * 2, compiler_params=_cp("parallel"),
    )(proj, proj, pos_col, _freq_row(), qg, kg, _head_ones())


def _qk_bwd(proj, pos_col, qg, kg, dq, dk, dv):
    tr = 256

    def body(q_ref, k_ref, pos_ref, f_ref, qg_ref, kg_ref, e_ref, dq_ref, dk_ref, dv_ref, o_ref, dqg_ref, dkg_ref):
        i, j = pl.program_id(0), pl.program_id(1)

        @pl.when((i == 0) & (j == 0))
        def _():
            dqg_ref[...] = jnp.zeros_like(dqg_ref)
            dkg_ref[...] = jnp.zeros_like(dkg_ref)

        def norm_rot_bwd(x_ref, g_ref, dg_ref, d_ref, scale):
            c, s_signed, first = _rot_tables(pos_ref, f_ref)
            e = e_ref[...]
            dout = d_ref[...] * scale
            dy = dout * c + _swap_halves(dout * s_signed, first)
            xv, g = x_ref[...], g_ref[...]
            r = lax.rsqrt(_segsum(xv * xv, e) * (1.0 / HD) + EPS)
            gdy = g * dy
            dx = r * gdy - xv * ((r * r * r) * (_segsum(xv * gdy, e) * (1.0 / HD)))
            o_ref[...] = dx.astype(BF16)
            dg_ref[...] += jnp.sum(dy * (xv * r), axis=0, keepdims=True)

        @pl.when(j == 0)
        def _():
            norm_rot_bwd(q_ref, qg_ref, dqg_ref, dq_ref, HD ** -0.5)

        @pl.when(j == 1)
        def _():
            norm_rot_bwd(k_ref, kg_ref, dkg_ref, dk_ref, 1.0)

        @pl.when(j == 2)
        def _():
            o_ref[...] = dv_ref[...].astype(BF16)

    col = lambda jj: _bs((tr, AW), lambda i, j, jj=jj: (i, jj))
    vec = _bs((1, AW), lambda i, j: (0, 0))
    piece = _bs((tr, AW), lambda i, j: (i, 0))
    return pl.pallas_call(
        body, name="qk_bwd", grid=(T // tr, 3),
        in_specs=[col(0), col(1), _bs((tr, 1), lambda i, j: (i, 0)), _bs((1, 128), lambda i, j: (0, 0)), vec, vec,
                  _bs((AW, AW), lambda i, j: (0, 0))] + [piece] * 3,
        out_specs=[_bs((tr, AW), lambda i, j: (i, j)), vec, vec],
        out_shape=[jax.ShapeDtypeStruct((T, INW), BF16), jax.ShapeDtypeStruct((1, AW), F32),
                   jax.ShapeDtypeStruct((1, AW), F32)],
        compiler_params=_cp("arbitrary", "arbitrary"),
    )(proj, proj, pos_col, _freq_row(), qg, kg, _head_ones(), dq, dk, dv)


RG = 256
QC = 64


def _stacked_band_mask(rows=2 * BLK, q0=0):
    qi = (lax.broadcasted_iota(jnp.int32, (rows, 2 * BLK), 0) + q0) & (BLK - 1)
    kj = lax.broadcasted_iota(jnp.int32, (rows, 2 * BLK), 1)
    rel = qi - kj + BLK
    return (rel >= 0) & (rel <= BLK), lax.broadcasted_iota(jnp.int32, (1, 2 * BLK), 1) >= BLK


def _natural_rows(r0, n_rows, d):
    if d == 1:
        return pl.ds(r0, n_rows)
    ln = T // d
    return pl.ds(r0 // ln + d * (r0 % ln), n_rows, stride=d)


def _regroup_into(dst, src_ref, d, pad, cast=True):
    def step(j, carry):
        r0 = pl.multiple_of(j * RG, RG)
        val = src_ref[_natural_rows(r0, RG, d), :]
        dst[pl.ds(pad + r0, RG), :] = val.astype(dst.dtype) if cast else val
        return carry
    lax.fori_loop(0, T // RG, step, 0)


def _stack_heads(x, h0):
    zero = jnp.zeros_like(x)
    return jnp.concatenate([jnp.where(h0, x, zero), jnp.where(h0, zero, x)], axis=0)


def _attn_fwd(q, k, proj):
    nblk = T // BLK

    def body(q_ref, k_ref, v_ref, a_ref, lse_ref, qs, ks, vs, o0, o1, o2, l0, l1, l2):
        band, cur_half = _stacked_band_mask()
        h0 = lax.broadcasted_iota(jnp.int32, (1, 128), 1) < HD
        ks[0:BLK, :] = jnp.zeros((BLK, 128), BF16)
        vs[0:BLK, :] = jnp.zeros((BLK, 128), BF16)
        for d, o_s, l_s in zip(DILATIONS, (o0, o1, o2), (l0, l1, l2)):
            nb = T // d // BLK
            _regroup_into(qs, q_ref, d, 0)
            _regroup_into(ks, k_ref, d, BLK)
            _regroup_into(vs, v_ref, d, BLK)

            def step(b, carry, d=d, nb=nb, o_s=o_s, l_s=l_s):
                r0 = pl.multiple_of(b * BLK, BLK)
                mask = band & (cur_half | ((b & (nb - 1)) > 0))
                kk = ks[pl.ds(r0, 2 * BLK), :]
                vv = vs[pl.ds(r0, 2 * BLK), :]
                s = jnp.where(mask, _dot(_stack_heads(qs[pl.ds(r0, BLK), :], h0), kk, NT), NEG)
                m = jnp.max(s, axis=1, keepdims=True)
                p = jnp.exp(s - m)
                l = jnp.sum(p, axis=1, keepdims=True)
                o = _dot(p.astype(BF16), vv) / l
                lse = m + jnp.log(l)
                rows = _natural_rows(r0, BLK, d)
                o_s[rows, :] = jnp.where(h0, o[0:BLK, :], o[BLK:, :])
                l_s[rows, :] = jnp.where(h0, lse[0:BLK, :], lse[BLK:, :])
                return carry

            lax.fori_loop(0, nblk, step, 0, unroll=2)

        def merge(i, carry):
            r = pl.ds(pl.multiple_of(i * RG, RG), RG)
            la, lb, lc = l0[r, :], l1[r, :], l2[r, :]
            m = jnp.maximum(jnp.maximum(la, lb), lc)
            ea, eb, ec = jnp.exp(la - m), jnp.exp(lb - m), jnp.exp(lc - m)
            z = (ea + eb) + ec
            a_ref[r, :] = ((ea * o0[r, :] + eb * o1[r, :]) + ec * o2[r, :]) / z
            lse_ref[r, :] = m + jnp.log(z)
            return carry

        lax.fori_loop(0, T // RG, merge, 0)

    spec = lambda cb: _bs((T, 128), lambda p, cb=cb: (0, cb + p))
    out = jax.ShapeDtypeStruct((T, AW), F32)
    return pl.pallas_call(
        body, name="attn_fwd", grid=(AW // 128,), in_specs=[spec(0), spec(0), spec(8)], out_specs=[spec(0)] * 2,
        out_shape=[out] * 2,
        scratch_shapes=[pltpu.VMEM((T, 128), BF16), pltpu.VMEM((T + BLK, 128), BF16), pltpu.VMEM((T + BLK, 128), BF16)]
        + [pltpu.VMEM((T, 128), F32)] * 6,
        compiler_params=_cp("parallel"),
    )(q, k, proj)


def _attn_bwd(q, k, proj, do, lse, delta):
    nblk = T // BLK

    def body(q_ref, k_ref, v_ref, do_ref, l_ref, dl_ref, dq_ref, dk_ref, dv_ref, qs, dos, ks, vs, ls, dls, dks, dvs):
        band, cur_half = _stacked_band_mask()
        h0 = lax.broadcasted_iota(jnp.int32, (1, 128), 1) < HD
        ks[0:BLK, :] = jnp.zeros((BLK, 128), BF16)
        vs[0:BLK, :] = jnp.zeros((BLK, 128), BF16)
        for d in DILATIONS:
            nb = T // d // BLK
            _regroup_into(qs, q_ref, d, 0)
            _regroup_into(dos, do_ref, d, 0)
            _regroup_into(ks, k_ref, d, BLK)
            _regroup_into(vs, v_ref, d, BLK)
            _regroup_into(ls, l_ref, d, 0, cast=False)
            _regroup_into(dls, dl_ref, d, 0, cast=False)
            dks[...] = jnp.zeros_like(dks)
            dvs[...] = jnp.zeros_like(dvs)

            def step(b, carry, d=d, nb=nb):
                r0 = pl.multiple_of(b * BLK, BLK)
                mask = band & (cur_half | ((b & (nb - 1)) > 0))
                win = pl.ds(r0, 2 * BLK)
                kk, vv = ks[win, :], vs[win, :]
                q2 = _stack_heads(qs[pl.ds(r0, BLK), :], h0)
                do2 = _stack_heads(dos[pl.ds(r0, BLK), :], h0)
                lv, dlv = ls[pl.ds(r0, BLK), :], dls[pl.ds(r0, BLK), :]
                lse2 = jnp.concatenate([lv[:, 0:1], lv[:, HD:HD + 1]], axis=0)
                dl2 = jnp.concatenate([dlv[:, 0:1], dlv[:, HD:HD + 1]], axis=0)
                s = jnp.where(mask, _dot(q2, kk, NT), NEG)
                p = jnp.exp(s - lse2)
                ds = p * (_dot(do2, vv, NT) - dl2)
                pb, dsb = p.astype(BF16), ds.astype(BF16)
                dq2 = _dot(dsb, kk)
                dks[win, :] += _dot(dsb, q2, TN)
                dvs[win, :] += _dot(pb, do2, TN)
                rows = _natural_rows(r0, BLK, d)
                dq = jnp.where(h0, dq2[0:BLK, :], dq2[BLK:, :])
                dq_ref[rows, :] = dq if d == 1 else dq_ref[rows, :] + dq
                return carry

            lax.fori_loop(0, nblk, step, 0, unroll=2)

            def back(j, carry, d=d):
                r0 = pl.multiple_of(j * RG, RG)
                rows = _natural_rows(r0, RG, d)
                src = pl.ds(BLK + r0, RG)
                dk_ref[rows, :] = dks[src, :] if d == 1 else dk_ref[rows, :] + dks[src, :]
                dv_ref[rows, :] = dvs[src, :] if d == 1 else dv_ref[rows, :] + dvs[src, :]
                return carry

            lax.fori_loop(0, T // RG, back, 0)

    spec = lambda cb: pl.BlockSpec((T, 128), lambda p, cb=cb: (0, cb + p), pipeline_mode=pl.Buffered(1))
    ospec = _bs((T, 128), lambda p: (0, p))
    out = jax.ShapeDtypeStruct((T, AW), F32)
    return pl.pallas_call(
        body, name="attn_bwd", grid=(AW // 128,), in_specs=[spec(0), spec(0), spec(8), spec(0), spec(0), spec(0)],
        out_specs=[ospec] * 3, out_shape=[out] * 3,
        scratch_shapes=[pltpu.VMEM((T, 128), BF16), pltpu.VMEM((T, 128), BF16), pltpu.VMEM((T + BLK, 128), BF16),
                        pltpu.VMEM((T + BLK, 128), BF16), pltpu.VMEM((T, 128), F32), pltpu.VMEM((T, 128), F32),
                        pltpu.VMEM((T + BLK, 128), F32), pltpu.VMEM((T + BLK, 128), F32)],
        compiler_params=_cp("parallel"),
    )(q, k, proj, do, lse, delta)


def _attn_norm(attn, g_attn):
    tr = 512

    def body(a_ref, g_ref, mix_ref):
        attn = a_ref[...]
        r = lax.rsqrt(jnp.mean(attn * attn, axis=-1, keepdims=True) + EPS)
        mix_ref[...] = ((attn * r) * g_ref[...]).astype(BF16)

    row = _bs((tr, AW), lambda i: (i, 0))
    return pl.pallas_call(
        body, name="attn_norm", grid=(T // tr,), in_specs=[row, _bs((1, AW), lambda i: (0, 0))],
        out_specs=row, out_shape=jax.ShapeDtypeStruct((T, D), BF16), compiler_params=_cp("parallel"),
    )(attn, g_attn)


def _attn_out_bwd(attn, dmix, g_attn):
    tr = 256

    def body(a_ref, d_ref, g_ref, e_ref, do_ref, dl_ref, dg_ref):
        av, dyv = a_ref[...], d_ref[...]
        r = lax.rsqrt(jnp.mean(av * av, axis=-1, keepdims=True) + EPS)
        gdy = g_ref[...] * dyv
        da = r * gdy - av * ((r * r * r) * jnp.mean(av * gdy, axis=-1, keepdims=True))
        do_ref[...] = da
        dl_ref[...] = _segsum(da * av, e_ref[...])

        @pl.when(pl.program_id(0) == 0)
        def _():
            dg_ref[...] = jnp.zeros_like(dg_ref)

        dg_ref[...] += jnp.sum(dyv * (av * r), axis=0, keepdims=True)

    row = _bs((tr, AW), lambda i: (i, 0))
    vec = _bs((1, AW), lambda i: (0, 0))
    return pl.pallas_call(
        body, name="attn_out_bwd", grid=(T // tr,), in_specs=[row, row, vec, _bs((AW, AW), lambda i: (0, 0))],
        out_specs=[row, row, vec],
        out_shape=[jax.ShapeDtypeStruct((T, AW), F32), jax.ShapeDtypeStruct((T, AW), F32),
                   jax.ShapeDtypeStruct((1, AW), F32)],
        compiler_params=_cp("arbitrary"),
    )(attn, dmix, g_attn, _head_ones())


TRR = 256


def _scan_fwd(a, u):
    n = a.shape[0]
    row = lax.broadcasted_iota(jnp.int32, (n, 1), 0)
    s = 1
    while s < n:
        keep = row >= s
        u = jnp.where(keep, a * pltpu.roll(u, s, 0) + u, u)
        a = jnp.where(keep, a * pltpu.roll(a, s, 0), a)
        s *= 2
    return a, u


def _scan_bwd(c, w):
    n = c.shape[0]
    row = lax.broadcasted_iota(jnp.int32, (n, 1), 0)
    s = 1
    while s < n:
        keep = row < n - s
        w = jnp.where(keep, c * pltpu.roll(w, n - s, 0) + w, w)
        c = jnp.where(keep, c * pltpu.roll(c, n - s, 0), c)
        s *= 2
    return w


def _gates(xc, wrg, wig, brg, big, sp):
    xcb = xc.astype(BF16)
    r = jax.nn.sigmoid(_dot(xcb, wrg) + brg)
    ig = jax.nn.sigmoid(_dot(xcb, wig) + big)
    la = (-LRU_C * r) * sp
    a = jnp.exp(la)
    mult = jnp.sqrt(-jnp.tanh(la) * (a * a + 1.0))
    return r, ig, a, mult


def _conv4(ext_ref, xr, cw_ref, cb_ref, n):
    y = cb_ref[...] + ext_ref[pl.ds(5, n), :] * cw_ref[0:1, :]
    y = y + ext_ref[pl.ds(6, n), :] * cw_ref[1:2, :]
    y = y + ext_ref[pl.ds(7, n), :] * cw_ref[2:3, :]
    return y + xr * cw_ref[3:4, :]


def _rec_fwd(proj, mix, cw, cb, wrg, wig, brg, big, lam, g_rec):
    n = TRR

    def body(xr_ref, gr_ref, cw_ref, cb_ref, wrg_ref, wig_ref, brg_ref, big_ref, lam_ref, g_ref, mix_in,
             mix_ref, h_ref, ext, hcar):
        del mix_in

        @pl.when(pl.program_id(0) == 0)
        def _():
            ext[0:8, :] = jnp.zeros((8, RW), F32)
            hcar[...] = jnp.zeros_like(hcar)

        xr = xr_ref[...]
        ext[8:, :] = xr
        xc = _conv4(ext, xr, cw_ref, cb_ref, n)
        ext[0:8, :] = xr[n - 8:, :]
        sp = _softplus(-lam_ref[...])
        _, ig, a, mult = _gates(xc, wrg_ref[...], wig_ref[...], brg_ref[...], big_ref[...], sp)
        a_s, u_s = _scan_fwd(a, mult * (ig * xc))
        h = u_s + a_s * hcar[7:8, :]
        h_ref[...] = h
        hcar[...] = h[n - 8:, :]
        pre = h * _gelu(gr_ref[...])
        r = lax.rsqrt(jnp.mean(pre * pre, axis=-1, keepdims=True) + EPS)
        mix_ref[...] = ((pre * r) * g_ref[...]).astype(BF16)

    vec = _bs((1, RW), lambda i: (0, 0))
    mat = _bs((RW, RW), lambda i: (0, 0))
    return pl.pallas_call(
        body, name="rec_fwd", grid=(T // n,),
        in_specs=[_bs((n, RW), lambda i: (i, 3)), _bs((n, RW), lambda i: (i, 4)), _bs((8, RW), lambda i: (0, 0)), vec,
                  mat, mat, vec, vec, vec, vec, pl.BlockSpec(memory_space=pl.ANY)],
        out_specs=[_bs((n, RW), lambda i: (i, 1)), _bs((n, RW), lambda i: (i, 0))],
        out_shape=[jax.ShapeDtypeStruct((T, D), BF16), jax.ShapeDtypeStruct((T, RW), F32)],
        scratch_shapes=[pltpu.VMEM((n + 8, RW), F32), pltpu.VMEM((8, RW), F32)],
        input_output_aliases={10: 0}, compiler_params=_cp("arbitrary"),
    )(proj, proj, cw, cb, wrg, wig, brg, big, lam, g_rec, mix)


def _rec_bwd(proj, h, dmix, dproj, cw, cb, wrg, wig, brg, big, lam, g_rec):
    n = TRR
    nt = T // n
    hb = n // 8

    def body(xr_ref, xh_ref, gr_ref, h_ref, hh_ref, dm_ref, cw_ref, cb_ref, wrg_ref, wig_ref, brg_ref, big_ref,
             lam_ref, g_ref, dp_in, dp_ref, xc_ref, dr_ref, di_ref, dcw_ref, dcb_ref, dbr_ref, dbi_ref, dsp_ref,
             dg_ref, ext, exth, extd, adh, dgr_s):
        del dp_in
        i, j = pl.program_id(0), pl.program_id(1)
        first_tile = i == nt - 1
        last_tile = i == 0

        @pl.when(j == 0)
        def _():
            @pl.when(last_tile)
            def _():
                for ref in (dcw_ref, dcb_ref, dbr_ref, dbi_ref, dsp_ref, dg_ref):
                    ref[...] = jnp.zeros_like(ref)
                extd[n:, :] = jnp.zeros((8, RW), F32)
                adh[...] = jnp.zeros_like(adh)

            row = lax.broadcasted_iota(jnp.int32, (n, 1), 0)
            xr = xr_ref[...]
            ext[0:8, :] = jnp.where(first_tile, 0.0, xh_ref[...])
            ext[8:, :] = xr
            xc = _conv4(ext, xr, cw_ref, cb_ref, n)
            sp = _softplus(-lam_ref[...])
            wrg, wig = wrg_ref[...], wig_ref[...]
            r, ig, a, mult = _gates(xc, wrg, wig, brg_ref[...], big_ref[...], sp)

            hv = h_ref[...]
            gl, dgl = _gelu_and_grad(gr_ref[...])
            pre = hv * gl
            dyv = dm_ref[...]
            rr = lax.rsqrt(jnp.mean(pre * pre, axis=-1, keepdims=True) + EPS)
            gdy = g_ref[...] * dyv
            dpre = rr * gdy - pre * ((rr * rr * rr) * jnp.mean(pre * gdy, axis=-1, keepdims=True))
            dg_ref[...] += jnp.sum(dyv * (pre * rr), axis=0, keepdims=True)
            dgr_s[...] = dpre * hv * dgl

            is_last_row = row == n - 1
            w = dpre * gl + jnp.where(is_last_row, adh[0:1, :], 0.0)
            c = jnp.where(is_last_row, 0.0, pltpu.roll(a, n - 1, 0))
            dh = _scan_bwd(c, w)
            adh[...] = (a * dh)[0:8, :]

            exth[0:8, :] = jnp.where(first_tile, 0.0, hh_ref[...])
            exth[8:, :] = hv
            da = dh * exth[pl.ds(7, n), :]
            ixc = ig * xc
            dmult = dh * ixc
            dla = da * a - dmult * ((a * a) / mult)
            dsp_ref[...] += jnp.sum(dla * (-LRU_C * r), axis=0, keepdims=True)
            dpr = (dla * (-LRU_C * sp)) * (r * (1.0 - r))
            dpi = (dh * (mult * xc)) * (ig * (1.0 - ig))
            dprb, dpib = dpr.astype(BF16), dpi.astype(BF16)
            dxc = dh * (mult * ig) + _dot(dprb, wrg, NT) + _dot(dpib, wig, NT)
            dbr_ref[...] += jnp.sum(dpr, axis=0, keepdims=True)
            dbi_ref[...] += jnp.sum(dpi, axis=0, keepdims=True)
            xc_ref[...] = xc.astype(BF16)
            dr_ref[...] = dprb
            di_ref[...] = dpib

            extd[0:n, :] = dxc
            dxr = dxc * cw_ref[3:4, :] + extd[pl.ds(1, n), :] * cw_ref[2:3, :]
            dxr = dxr + extd[pl.ds(2, n), :] * cw_ref[1:2, :] + extd[pl.ds(3, n), :] * cw_ref[0:1, :]
            extd[n:, :] = dxc[0:8, :]
            dcb_ref[...] += jnp.sum(dxc, axis=0, keepdims=True)
            for kk in range(4):
                dcw_ref[kk:kk + 1, :] += jnp.sum(dxc * ext[pl.ds(5 + kk, n), :], axis=0, keepdims=True)

            @pl.when(first_tile)
            def _():
                dsp_ref[...] = dsp_ref[...] * (-jax.nn.sigmoid(-lam_ref[...]))

            dp_ref[...] = dxr.astype(BF16)

        @pl.when(j == 1)
        def _():
            dp_ref[...] = dgr_s[...].astype(BF16)

    vec = _bs((1, RW), lambda i, j: (0, 0))
    mat = _bs((RW, RW), lambda i, j: (0, 0))
    tile = lambda cblk: _bs((n, RW), lambda i, j, cblk=cblk: (nt - 1 - i, cblk))
    halo = lambda cblk: _bs((8, RW), lambda i, j, cblk=cblk: (jnp.maximum((nt - 1 - i) * hb - 1, 0), cblk))
    bt = jax.ShapeDtypeStruct((T, RW), BF16)
    v = jax.ShapeDtypeStruct((1, RW), F32)
    return pl.pallas_call(
        body, name="rec_bwd", grid=(nt, 2),
        in_specs=[tile(3), halo(3), tile(4), tile(0), halo(0), tile(1), _bs((8, RW), lambda i, j: (0, 0)), vec,
                  mat, mat, vec, vec, vec, vec, pl.BlockSpec(memory_space=pl.ANY)],
        out_specs=[_bs((n, RW), lambda i, j: (nt - 1 - i, 3 + j)), tile(0), tile(0), tile(0),
                   _bs((8, RW), lambda i, j: (0, 0)), vec, vec, vec, vec, vec],
        out_shape=[jax.ShapeDtypeStruct((T, INW), BF16), bt, bt, bt, jax.ShapeDtypeStruct((8, RW), F32), v, v, v, v, v],
        scratch_shapes=[pltpu.VMEM((n + 8, RW), F32), pltpu.VMEM((n + 8, RW), F32), pltpu.VMEM((n + 8, RW), F32),
                        pltpu.VMEM((8, RW), F32), pltpu.VMEM((n, RW), F32)],
        input_output_aliases={14: 0}, compiler_params=_cp("arbitrary", "arbitrary"),
    )(proj, proj, proj, h, h, dmix, cw, cb, wrg, wig, brg, big, lam, g_rec, dproj)


FC = 1536
TRF = 256


LC = 128


def _taps(x_ref, edge, cols, r):
    if r == 0:
        return edge[pl.ds(6, 8), cols], edge[pl.ds(7, 8), cols], edge[pl.ds(8, 8), cols]
    return x_ref[pl.ds(r - 2, 8), cols], x_ref[pl.ds(r - 1, 8), cols], x_ref[pl.ds(r, 8), cols]


def _ffn_act(up_pre, cw, cb):
    n = TRF
    hb = n // 8

    def body(g_ref, gh_ref, u_ref, uh_ref, wg_ref, wu_ref, bg_ref, bu_ref, o_ref, eg, eu):
        first = pl.program_id(1) == 0
        eg[0:8, :] = jnp.where(first, 0.0, gh_ref[...])
        eg[8:, :] = g_ref[0:8, :]
        eu[0:8, :] = jnp.where(first, 0.0, uh_ref[...])
        eu[8:, :] = u_ref[0:8, :]

        def column(ci, carry):
            cols = pl.ds(pl.multiple_of(ci * LC, LC), LC)
            wg = [wg_ref[kk:kk + 1, cols] for kk in range(3)]
            wu = [wu_ref[kk:kk + 1, cols] for kk in range(3)]
            bg, bu = bg_ref[:, cols], bu_ref[:, cols]
            for r in range(0, n, 16):
                res = []
                for rr in (r, r + 8):
                    g0, g1, g2 = _taps(g_ref, eg, cols, rr)
                    u0, u1, u2 = _taps(u_ref, eu, cols, rr)
                    ug = ((bg + g0 * wg[0]) + g1 * wg[1]) + g2 * wg[2]
                    uu = ((bu + u0 * wu[0]) + u1 * wu[1]) + u2 * wu[2]
                    res.append(_gelu(ug) * uu)
                o_ref[pl.ds(r, 16), cols] = jnp.concatenate(res, axis=0).astype(BF16)
            return carry

        lax.fori_loop(0, FC // LC, column, 0)

    main = lambda o: _bs((n, FC), lambda j, i, o=o: (i, 2 * j + o))
    halo = lambda o: _bs((8, FC), lambda j, i, o=o: (jnp.maximum(i * hb - 1, 0), 2 * j + o))
    wsp = lambda o: _bs((None, 8, FC), lambda j, i, o=o: (2 * j + o, 0, 0))
    bsp = lambda o: _bs((1, FC), lambda j, i, o=o: (0, 2 * j + o))
    return pl.pallas_call(
        body, name="ffn_act", grid=(2, T // n),
        in_specs=[main(0), halo(0), main(1), halo(1), wsp(0), wsp(1), bsp(0), bsp(1)],
        out_specs=_bs((n, FC), lambda j, i: (i, j)), out_shape=jax.ShapeDtypeStruct((T, DFF), BF16),
        scratch_shapes=[pltpu.VMEM((16, FC), F32)] * 2, compiler_params=_cp("parallel", "parallel"),
    )(up_pre, up_pre, up_pre, up_pre, cw, cw, cb, cb)


def _ffn_bwd(up_pre, dact, cw, cb):
    n = TRF
    hb = n // 8
    nt = T // n
    m = n + 8

    def body(g_ref, gp_ref, gn_ref, u_ref, up_ref, un_ref, d_ref, dn_ref, wg_ref, wu_ref, bg_ref, bu_ref,
             o_ref, dw_ref, db_ref, eg0, eg1, eu0, eu1, dug_s, duu_s):
        i = pl.program_id(1)
        first, last = i == 0, i == nt - 1

        @pl.when(first)
        def _():
            dw_ref[...] = jnp.zeros_like(dw_ref)
            db_ref[...] = jnp.zeros_like(db_ref)

        eg0[0:8, :] = jnp.where(first, 0.0, gp_ref[...])
        eg0[8:, :] = g_ref[0:8, :]
        eg1[0:8, :] = g_ref[n - 8:, :]
        eg1[8:, :] = gn_ref[...]
        eu0[0:8, :] = jnp.where(first, 0.0, up_ref[...])
        eu0[8:, :] = u_ref[0:8, :]
        eu1[0:8, :] = u_ref[n - 8:, :]
        eu1[8:, :] = un_ref[...]

        def column(ci, carry):
            cols = pl.ds(pl.multiple_of(ci * LC, LC), LC)
            ucols = pl.ds(pl.multiple_of(FC + ci * LC, LC), LC)
            wg = [wg_ref[kk:kk + 1, cols] for kk in range(3)]
            wu = [wu_ref[kk:kk + 1, cols] for kk in range(3)]
            bg, bu = bg_ref[:, cols], bu_ref[:, cols]
            zero = jnp.zeros((8, LC), F32)
            acc = [zero] * 8
            for r in range(0, n + 8, 8):
                if r == n:
                    gt = (eg1[pl.ds(6, 8), cols], eg1[pl.ds(7, 8), cols], eg1[pl.ds(8, 8), cols])
                    ut = (eu1[pl.ds(6, 8), cols], eu1[pl.ds(7, 8), cols], eu1[pl.ds(8, 8), cols])
                    dv = jnp.where(last, 0.0, dn_ref[:, cols])
                else:
                    gt, ut = _taps(g_ref, eg0, cols, r), _taps(u_ref, eu0, cols, r)
                    dv = d_ref[pl.ds(r, 8), cols]
                gl, dgl = _gelu_and_grad(((bg + gt[0] * wg[0]) + gt[1] * wg[1]) + gt[2] * wg[2])
                uu = ((bu + ut[0] * wu[0]) + ut[1] * wu[1]) + ut[2] * wu[2]
                dug, duu = dv * uu * dgl, dv * gl
                dug_s[pl.ds(r, 8), :] = dug
                duu_s[pl.ds(r, 8), :] = duu
                if r < n:
                    acc = [acc[0] + dug * gt[0], acc[1] + dug * gt[1], acc[2] + dug * gt[2],
                           acc[3] + duu * ut[0], acc[4] + duu * ut[1], acc[5] + duu * ut[2], acc[6] + dug, acc[7] + duu]
            for r in range(0, n, 16):
                og, ou = [], []
                for rr in (r, r + 8):
                    og.append((dug_s[pl.ds(rr, 8), :] * wg[2] + dug_s[pl.ds(rr + 1, 8), :] * wg[1])
                              + dug_s[pl.ds(rr + 2, 8), :] * wg[0])
                    ou.append((duu_s[pl.ds(rr, 8), :] * wu[2] + duu_s[pl.ds(rr + 1, 8), :] * wu[1])
                              + duu_s[pl.ds(rr + 2, 8), :] * wu[0])
                o_ref[pl.ds(r, 16), cols] = jnp.concatenate(og, axis=0).astype(BF16)
                o_ref[pl.ds(r, 16), ucols] = jnp.concatenate(ou, axis=0).astype(BF16)
            for kk in range(3):
                dw_ref[kk:kk + 1, cols] += jnp.sum(acc[kk], axis=0, keepdims=True)
                dw_ref[kk:kk + 1, ucols] += jnp.sum(acc[3 + kk], axis=0, keepdims=True)
            db_ref[:, cols] += jnp.sum(acc[6], axis=0, keepdims=True)
            db_ref[:, ucols] += jnp.sum(acc[7], axis=0, keepdims=True)
            return carry

        lax.fori_loop(0, FC // LC, column, 0)

    main = lambda o: _bs((n, FC), lambda j, i, o=o: (i, 2 * j + o))
    prev = lambda o: _bs((8, FC), lambda j, i, o=o: (jnp.maximum(i * hb - 1, 0), 2 * j + o))
    nxt = lambda o: _bs((8, FC), lambda j, i, o=o: (jnp.minimum((i + 1) * hb, T // 8 - 1), 2 * j + o))
    wsp = lambda o: _bs((None, 8, FC), lambda j, i, o=o: (2 * j + o, 0, 0))
    bsp = lambda o: _bs((1, FC), lambda j, i, o=o: (0, 2 * j + o))
    return pl.pallas_call(
        body, name="ffn_bwd", grid=(2, nt),
        in_specs=[main(0), prev(0), nxt(0), main(1), prev(1), nxt(1), _bs((n, FC), lambda j, i: (i, j)),
                  _bs((8, FC), lambda j, i: (jnp.minimum((i + 1) * hb, T // 8 - 1), j)), wsp(0), wsp(1), bsp(0), bsp(1)],
        out_specs=[_bs((n, 2 * FC), lambda j, i: (i, j)), _bs((8, 2 * FC), lambda j, i: (0, j)),
                   _bs((1, 2 * FC), lambda j, i: (0, j))],
        out_shape=[jax.ShapeDtypeStruct((T, 2 * DFF), BF16), jax.ShapeDtypeStruct((8, 2 * DFF), F32),
                   jax.ShapeDtypeStruct((1, 2 * DFF), F32)],
        scratch_shapes=[pltpu.VMEM((16, FC), F32)] * 4 + [pltpu.VMEM((m, LC), F32)] * 2,
        compiler_params=_cp("parallel", "arbitrary"),
    )(up_pre, up_pre, up_pre, up_pre, up_pre, up_pre, dact, dact, cw, cw, cb, cb)


def _down_loss(act, w_down, x1, target):
    tm, tn = 512, 512

    def body(a_ref, b_ref, r_ref, t_ref, dy_ref, dyb_ref, l_ref):
        @pl.when((pl.program_id(0) == 0) & (pl.program_id(1) == 0))
        def _():
            l_ref[...] = jnp.zeros_like(l_ref)

        err = (r_ref[...] + _dot(a_ref[...], b_ref[...])) - t_ref[...]
        dy = err * (1.0 / D)
        dy_ref[...] = dy
        dyb_ref[...] = dy.astype(BF16)
        l_ref[...] += jnp.sum(0.5 * (err * err) * (1.0 / D))

    o_spec = _bs((tm, tn), lambda j, i: (i, j))
    return pl.pallas_call(
        body, name="down_loss", grid=(D // tn, T // tm),
        in_specs=[_bs((tm, DFF), lambda j, i: (i, 0)), _bs((DFF, tn), lambda j, i: (0, j)), o_spec, o_spec],
        out_specs=[o_spec, o_spec, _bs((8, 128), lambda j, i: (0, 0))],
        out_shape=[jax.ShapeDtypeStruct((T, D), F32), jax.ShapeDtypeStruct((T, D), BF16),
                   jax.ShapeDtypeStruct((8, 128), F32)],
        compiler_params=_cp("arbitrary", "arbitrary"),
    )(act, w_down, x1, target)


def _block_diag(w):
    eye = jnp.eye(8, dtype=w.dtype)
    return (w[:, :, None, :] * eye[:, None, :, None]).reshape(RW, RW).astype(BF16)


def _diag_blocks(m):
    return jnp.stack([m[HD * b:HD * (b + 1), HD * b:HD * (b + 1)] for b in range(8)])


def _local_step(x, pos_col, target, p, exch):
    qg, kg = jnp.tile(p["q_norm_g"], (1, 8)), jnp.tile(p["k_norm_g"], (1, 8))
    wrg, wig = _block_diag(p["w_rg"]), _block_diag(p["w_ig"])
    brg, big = p["b_rg"].reshape(1, RW), p["b_ig"].reshape(1, RW)

    h1 = _rms_fwd("rms1", x, p["g_mix"])
    proj = _mm("mm_in", h1, p["w_in"], "nn", 512, 640, stack=NCHIP, after=exch.start_rest())
    q, k = _qk_prep(proj, pos_col, qg, kg)
    attn, lse = _attn_fwd(q, k, proj)
    mix = _attn_norm(attn, p["g_attn_out"])
    mix, hseq = _rec_fwd(proj, mix, p["rec_conv_w"], p["rec_conv_b"], wrg, wig, brg, big, p["lru_lambda"], p["g_rec_out"])
    rest = exch.wait_rest(mix)
    x1 = _mm("mm_out", mix, rest["w_out"], "nn", 512, 512, res=x)
    h2 = _rms_fwd("rms2", x1, p["g_ffn"])
    up_pre = _mm("mm_up", h2, rest["w_up"], "nn", 512, 768, stack=NCHIP)
    act = _ffn_act(up_pre, p["ffn_conv_w"], p["ffn_conv_b"])
    dy, dyb, loss_blk = _down_loss(act, rest["w_down"], x1, target)

    g = {}
    tok = exch.reduce_start("w_down", *_mm("wg_down", act, dyb, "tn", 512, 512, twin_bf16=True))
    dact = _mm("dg_down", dyb, rest["w_down"], "nt", 512, 512, after=tok)
    dup, g["ffn_conv_w"], g["ffn_conv_b"] = _ffn_bwd(up_pre, dact, p["ffn_conv_w"], p["ffn_conv_b"])
    tok = exch.reduce_start("w_up", *_mm("wg_up", h2, dup, "tn", 512, 768, stack=NCHIP, twin_bf16=True))
    dh2 = _mm("dg_up", dup, rest["w_up"], "nt", 512, 256, stack=NCHIP, after=tok)
    dx1, dx1b, g["g_ffn"] = _rms_bwd("rms2_bwd", x1, p["g_ffn"], dh2, dy, True)
    tok = exch.reduce_start("w_out", *_mm("wg_out", mix, dx1b, "tn", 512, 512, twin_bf16=True))
    dmix = _mm("dg_out", dx1b, rest["w_out"], "nt", 512, 512, after=tok)
    do, delta, g["g_attn_out"] = _attn_out_bwd(attn, dmix, p["g_attn_out"])
    dq, dk, dv = _attn_bwd(q, k, proj, do, lse, delta)
    dproj, dqg, dkg = _qk_bwd(proj, pos_col, qg, kg, dq, dk, dv)
    (dproj, xcb, dprb, dpib, g["rec_conv_w"], g["rec_conv_b"], dbr, dbi, dsp, g["g_rec_out"]) = _rec_bwd(
        proj, hseq, dmix, dproj, p["rec_conv_w"], p["rec_conv_b"], wrg, wig, brg, big, p["lru_lambda"], p["g_rec_out"])
    g["w_rg"] = _diag_blocks(_mm("wg_rg", xcb, dprb, "tn", 512, 512)).reshape(RW, HD)
    g["w_ig"] = _diag_blocks(_mm("wg_ig", xcb, dpib, "tn", 512, 512)).reshape(RW, HD)
    g["b_rg"], g["b_ig"] = dbr.reshape(8, HD), dbi.reshape(8, HD)
    g["lru_lambda"] = dsp
    g["q_norm_g"] = dqg.reshape(8, HD).sum(axis=0, keepdims=True)
    g["k_norm_g"] = dkg.reshape(8, HD).sum(axis=0, keepdims=True)
    tok = exch.reduce_start("w_in", *_mm("wg_in", h1, dproj, "tn", 512, 640, stack=NCHIP, twin_bf16=True))
    dh1 = _mm("dg_in", dproj, p["w_in"], "nt", 512, 512, stack=NCHIP, after=tok)
    grad_x, g["g_mix"] = _rms_bwd("rms1_bwd", x, p["g_mix"], dh1, dx1, False)
    return loss_blk, grad_x, g


ANY = pl.BlockSpec(memory_space=pl.ANY)


def _mesh_pos():
    return lax.axis_index("x"), lax.axis_index("y"), lax.axis_index("c")


def _slot(px, py, perm):
    return 2 * py + px if perm else 2 * px + py


def _other_chips(x, y):
    return [(1 - x, y), (x, 1 - y), (1 - x, 1 - y)]


def _rcopy(src, dst, send, recv, k, to, kr=None):
    return pltpu.make_async_remote_copy(src_ref=src, dst_ref=dst, send_sem=send.at[k],
                                        recv_sem=recv.at[k if kr is None else kr], device_id=to, device_id_type=MESH)


def _cast_bf16(name, w):
    r, c = w.shape
    tr = 128
    def body(w_ref, o_ref):
        o_ref[...] = w_ref[...].astype(BF16)
    return pl.pallas_call(
        body, name=name, grid=(r // tr,), in_specs=[_bs((tr, c), lambda i: (i, 0))],
        out_specs=_bs((tr, c), lambda i: (i, 0)), out_shape=jax.ShapeDtypeStruct((r, c), BF16),
        compiler_params=_cp("parallel"),
    )(w)


def _gather_weights(big, small, slot):
    nb, ns = len(big), len(small)
    perms = [p for _, p in big] + [p for _, p in small]

    def body(*refs):
        ins, outs = refs[:nb + ns], refs[2 * (nb + ns):3 * (nb + ns)]
        send, recv = refs[3 * (nb + ns):]
        x, y, c = _mesh_pos()
        me, sib = (x, y, c), (x, y, 1 - c)
        chips = _other_chips(x, y)
        first = []
        for a in range(nb):
            for j, (px, py) in enumerate(chips):
                first.append(_rcopy(ins[a].at[c], outs[a].at[_slot(x, y, perms[a]), c], send, recv, 3 * a + j, (px, py, c)))
        for t in range(ns):
            a = nb + t
            for j, (px, py) in enumerate(chips):
                first.append(_rcopy(ins[a], outs[a].at[_slot(x, y, perms[a])], send, recv, 6 * nb + 3 * t + j, (px, py, c)))
        for cp in first:
            cp.start()
        passed = []
        for a in range(nb):
            for j, (px, py) in enumerate(chips):
                got = outs[a].at[_slot(px, py, perms[a]), c]
                _rcopy(got, got, send, recv, 3 * a + j, me).wait_recv()
                fwd = _rcopy(got, got, send, recv, 3 * nb + 3 * a + j, sib)
                fwd.start()
                passed.append(fwd)
        for a in range(nb):
            for j, (px, py) in enumerate(chips):
                got = outs[a].at[_slot(px, py, perms[a]), 1 - c]
                _rcopy(got, got, send, recv, 3 * nb + 3 * a + j, me).wait_recv()
        for t in range(ns):
            a = nb + t
            for j, (px, py) in enumerate(chips):
                got = outs[a].at[_slot(px, py, perms[a])]
                _rcopy(got, got, send, recv, 6 * nb + 3 * t + j, me).wait_recv()
        for cp in first + passed:
            cp.wait_send()

    arrs = [a for a, _ in big] + [a for a, _ in small]
    lands = [lax.dynamic_update_slice(lax.empty((NCHIP,) + a.shape, a.dtype), a[None], (slot[p],) + (0,) * a.ndim)
             for a, p in zip(arrs, perms)]
    nsem = 6 * nb + 3 * ns
    return pl.pallas_call(
        body, name="gather_weights", in_specs=[ANY] * (2 * (nb + ns)), out_specs=[ANY] * (nb + ns),
        out_shape=[jax.ShapeDtypeStruct(a.shape, a.dtype) for a in lands],
        input_output_aliases={nb + ns + i: i for i in range(nb + ns)},
        scratch_shapes=[pltpu.SemaphoreType.DMA((nsem,)), pltpu.SemaphoreType.DMA((nsem,))],
    )(*arrs, *lands)


HBM = pl.BlockSpec(memory_space=pltpu.HBM)
SEM = pl.BlockSpec(memory_space=pltpu.SEMAPHORE)
EFFECT = pltpu.SideEffectType.DATAFLOW_SIDE_EFFECTING


def _split_start(name, srcs, lands, plan, nsem):
    ns, nl = len(srcs), len(lands)

    def body(*refs):
        send, recv = refs[ns + nl], refs[ns + nl + 1]
        sends, _ = plan(refs[:ns], refs[ns:ns + nl], send, recv)
        for cp in sends:
            cp.start()
        refs[-1][...] = jnp.zeros((8, 128), F32)

    arrs = list(srcs) + list(lands)
    out = pl.pallas_call(
        body, name=name, in_specs=[HBM] * (ns + nl),
        out_specs=[SEM, SEM] + [HBM] * (ns + nl) + [pl.BlockSpec(memory_space=pltpu.VMEM)],
        out_shape=[pltpu.SemaphoreType.DMA((nsem,)), pltpu.SemaphoreType.DMA((nsem,))]
        + [pltpu.HBM(a.shape, a.dtype) for a in arrs] + [jax.ShapeDtypeStruct((8, 128), F32)],
        input_output_aliases={i: 2 + i for i in range(ns + nl)},
        compiler_params=pltpu.CompilerParams(has_side_effects=EFFECT),
    )(*[pltpu.with_memory_space_constraint(a, pltpu.HBM) for a in arrs])
    return out[0], out[1], out[2:2 + ns], out[2 + ns:2 + ns + nl], out[-1]


def _split_wait(name, send, recv, srcs, lands, plan, after):
    ns, nl = len(srcs), len(lands)

    def body(*refs):
        sends, recvs = plan(refs[:ns], refs[ns:ns + nl], refs[ns + nl], refs[ns + nl + 1])
        for cp in sends:
            cp.wait_send()
        for cp in recvs:
            cp.wait_recv()

    arrs = list(srcs) + list(lands)
    out = pl.pallas_call(
        body, name=name, in_specs=[HBM] * (ns + nl) + [SEM, SEM, ANY], out_specs=[HBM] * (ns + nl),
        out_shape=[pltpu.HBM(a.shape, a.dtype) for a in arrs],
        input_output_aliases={i: i for i in range(ns + nl)},
        compiler_params=pltpu.CompilerParams(has_side_effects=EFFECT),
    )(*arrs, send, recv, after)
    return out[ns:]


def _gather_plan(perms):
    def plan(srcs, lands, send, recv):
        x, y, c = _mesh_pos()
        sends, recvs = [], []
        for a, perm in enumerate(perms):
            for j, (px, py) in enumerate(_other_chips(x, y)):
                for cc in (0, 1):
                    k = 6 * a + 2 * j + cc
                    sends.append(_rcopy(srcs[a].at[c], lands[a].at[_slot(x, y, perm), c], send, recv, k, (px, py, cc),
                                        kr=6 * a + 2 * j + c))
                    got = lands[a].at[_slot(px, py, perm), cc]
                    recvs.append(_rcopy(got, got, send, recv, k, (x, y, c)))
        return sends, recvs
    return plan


def _reduce_plan(perm):
    def plan(srcs, lands, send, recv):
        x, y, c = _mesh_pos()
        src, land = srcs[0], lands[0]
        sends = []
        for j, (px, py) in enumerate(_other_chips(x, y)):
            for hf in (0, 1):
                sends.append(_rcopy(src.at[_slot(px, py, perm), hf], land.at[2 * j + c], send, recv, 2 * j + hf,
                                    (px, py, hf), kr=2 * j + c))
        sends.append(_rcopy(src.at[_slot(x, y, perm), 1 - c], land.at[6], send, recv, 6, (x, y, 1 - c)))
        recvs = [_rcopy(land.at[i], land.at[i], send, recv, i, (x, y, c)) for i in range(7)]
        return sends, recvs
    return plan


def _sibling_share(rs):
    na = len(rs)

    def body(*refs):
        ins, outs, (send, recv) = refs[:na], refs[na:2 * na], refs[2 * na:]
        x, y, c = _mesh_pos()
        cps = [_rcopy(ins[a], outs[a], send, recv, a, (x, y, 1 - c)) for a in range(na)]
        for cp in cps:
            cp.start()
        for cp in cps:
            cp.wait()

    return pl.pallas_call(
        body, name="rs_share", in_specs=[ANY] * na, out_specs=[ANY] * na,
        out_shape=[jax.ShapeDtypeStruct(r.shape, F32) for r in rs],
        scratch_shapes=[pltpu.SemaphoreType.DMA((na,)), pltpu.SemaphoreType.DMA((na,))],
    )(*rs)


def _add_pieces(name, g, got, where):
    _, _, r2, cc = g.shape
    tr = 128

    def body(w_ref, g_ref, r_ref, o_ref):
        del w_ref
        acc = g_ref[...]
        for i in range(7):
            acc = acc + r_ref[i].astype(F32)
        o_ref[...] = acc

    return pl.pallas_call(
        body, name=name,
        grid_spec=pltpu.PrefetchScalarGridSpec(
            num_scalar_prefetch=1, grid=(r2 // tr,),
            in_specs=[_bs((None, None, tr, cc), lambda i, w_ref: (w_ref[0], w_ref[1], i, 0)),
                      _bs((7, tr, cc), lambda i, w_ref: (0, i, 0))],
            out_specs=_bs((tr, cc), lambda i, w_ref: (i, 0))),
        out_shape=jax.ShapeDtypeStruct((r2, cc), F32), compiler_params=_cp("parallel"),
    )(where, g, got)


def _adam_math(w, g, m, v):
    m = ADAM_B1 * m + (1.0 - ADAM_B1) * g
    v = ADAM_B2 * v + (1.0 - ADAM_B2) * (g * g)
    m_hat = m / (1.0 - ADAM_B1 ** ADAM_STEP)
    v_hat = v / (1.0 - ADAM_B2 ** ADAM_STEP)
    return -ADAM_LR * (m_hat / (jnp.sqrt(v_hat) + ADAM_EPS) + ADAM_WD * w), m, v


def _adam_big(name, w, g_mine, g_sib, m, v, c_arr):
    r, cols = w.shape
    tr = 128
    per = r // 2 // tr

    def body(c_ref, w_ref, a_ref, b_ref, m_ref, v_ref, g_ref, d_ref, m2_ref, v2_ref):
        g = jnp.where(pl.program_id(0) == c_ref[0], a_ref[...], b_ref[...])
        g_ref[...] = g
        d_ref[...], m2_ref[...], v2_ref[...] = _adam_math(w_ref[...], g, m_ref[...], v_ref[...])

    spec = _bs((tr, cols), lambda h, i, c_ref: (h * per + i, 0))
    half = _bs((tr, cols), lambda h, i, c_ref: (i, 0))
    out = jax.ShapeDtypeStruct((r, cols), F32)
    return pl.pallas_call(
        body, name=name,
        grid_spec=pltpu.PrefetchScalarGridSpec(
            num_scalar_prefetch=1, grid=(2, per), in_specs=[spec, half, half, spec, spec], out_specs=[spec] * 4),
        out_shape=[out] * 4, compiler_params=_cp("parallel", "parallel"),
    )(c_arr, w, g_mine, g_sib, m, v)


_CLASS_SHAPE = {"a": (8, D), "b": (8, RW), "c": (8, 2 * DFF), "d": (1048, HD)}
_SMALL = (
    ("g_mix", "a", 0, 1, D), ("g_ffn", "a", 1, 1, D),
    ("rec_conv_w", "b", 0, 4, RW), ("rec_conv_b", "b", 4, 1, RW), ("lru_lambda", "b", 5, 1, RW),
    ("g_attn_out", "b", 6, 1, RW), ("g_rec_out", "b", 7, 1, RW),
    ("ffn_conv_w", "c", 0, 3, 2 * DFF), ("ffn_conv_b", "c", 3, 1, 2 * DFF),
    ("w_rg", "d", 0, RW, HD), ("w_ig", "d", RW, RW, HD), ("b_rg", "d", 2 * RW, 8, HD), ("b_ig", "d", 2 * RW + 8, 8, HD),
    ("q_norm_g", "d", 2 * RW + 16, 1, HD), ("k_norm_g", "d", 2 * RW + 17, 1, HD),
)
_LOSS_ROW = 2
_CLASSES = ("a", "b", "c", "d")


def _small_allreduce(g, loss_blk):
    names = [s[0] for s in _SMALL]
    nin = len(names) + 1

    def body(*refs):
        ins = dict(zip(names, refs[:len(names)]))
        loss_ref = refs[len(names)]
        outs = dict(zip(_CLASSES, refs[nin:nin + 4]))
        pair = dict(zip(_CLASSES, refs[nin + 4:nin + 8]))
        quad = dict(zip(_CLASSES, refs[nin + 8:nin + 12]))
        send, recv = refs[nin + 12:]
        x, y, c = _mesh_pos()
        chip = 2 * x + y
        pair["a"][c] = jnp.zeros(_CLASS_SHAPE["a"], F32)
        pair["b"][c] = ins["rec_conv_w"][...]
        pair["c"][c] = ins["ffn_conv_w"][...]
        pair["d"][c, 2 * RW + 16:, :] = jnp.zeros((8, HD), F32)
        for name, k, r0, nr, _ in _SMALL:
            if name in ("rec_conv_w", "ffn_conv_w"):
                continue
            pair[k][c, r0:r0 + nr, :] = ins[name][...]
        pair["a"][c, _LOSS_ROW:_LOSS_ROW + 1, :] = jnp.broadcast_to(loss_ref[0:1, 0:1], (1, D))
        cps = [_rcopy(pair[k].at[c], pair[k].at[c], send, recv, ki, (x, y, 1 - c)) for ki, k in enumerate(_CLASSES)]
        for cp in cps:
            cp.start()
        for ki, k in enumerate(_CLASSES):
            _rcopy(pair[k].at[1 - c], pair[k].at[1 - c], send, recv, ki, (x, y, c)).wait_recv()
            quad[k][chip] = pair[k][0] + pair[k][1]
        cps2 = []
        for ki, k in enumerate(_CLASSES):
            for j, (px, py) in enumerate(_other_chips(x, y)):
                cps2.append(_rcopy(quad[k].at[chip], quad[k].at[chip], send, recv, 4 + 3 * ki + j, (px, py, c)))
        for cp in cps2:
            cp.start()
        for ki, k in enumerate(_CLASSES):
            for j, (px, py) in enumerate(_other_chips(x, y)):
                got = quad[k].at[2 * px + py]
                _rcopy(got, got, send, recv, 4 + 3 * ki + j, (x, y, c)).wait_recv()
            outs[k][...] = ((quad[k][0] + quad[k][1]) + quad[k][2]) + quad[k][3]
        for cp in cps + cps2:
            cp.wait_send()

    vm = pl.BlockSpec(memory_space=pltpu.VMEM)
    return pl.pallas_call(
        body, name="small_allreduce", in_specs=[vm] * nin, out_specs=[vm] * 4,
        out_shape=[jax.ShapeDtypeStruct(_CLASS_SHAPE[k], F32) for k in _CLASSES],
        scratch_shapes=[pltpu.VMEM((2,) + _CLASS_SHAPE[k], F32) for k in _CLASSES]
        + [pltpu.VMEM((NCHIP,) + _CLASS_SHAPE[k], F32) for k in _CLASSES]
        + [pltpu.SemaphoreType.DMA((16,)), pltpu.SemaphoreType.DMA((16,))],
        compiler_params=pltpu.CompilerParams(vmem_limit_bytes=VMEM_LIMIT),
    )(*[g[n] for n in names], loss_blk)


def _adam_small(red, w, m, v):
    names = [s[0] for s in _SMALL]
    n = len(names)

    def body(*refs):
        red_refs = dict(zip(_CLASSES, refs[:4]))
        w_refs, m_refs, v_refs = refs[4:4 + n], refs[4 + n:4 + 2 * n], refs[4 + 2 * n:4 + 3 * n]
        loss_ref = refs[4 + 3 * n]
        out_refs = refs[5 + 3 * n:]
        x, y, _ = _mesh_pos()
        chip = 2 * x + y
        loss_ref[...] = jnp.broadcast_to(red_refs["a"][_LOSS_ROW:_LOSS_ROW + 1, 0:1], loss_ref.shape)
        for pi, (name, k, r0, nr, width) in enumerate(_SMALL):
            gfull = red_refs[k][r0:r0 + nr, :]
            if name == "rec_conv_w":
                parts = [gfull[:, 128 * s:128 * (s + 1)] for s in range(NCHIP)]
                g = jnp.where(chip == 0, parts[0], jnp.where(chip == 1, parts[1], jnp.where(chip == 2, parts[2], parts[3])))
            elif name == "ffn_conv_w":
                parts = [gfull[:, FC * s:FC * (s + 1)] for s in range(NCHIP)]
                g = jnp.where(chip == 0, parts[0], jnp.where(chip == 1, parts[2], jnp.where(chip == 2, parts[1], parts[3])))
            elif name == "ffn_conv_b":
                g = jnp.concatenate([gfull[:, FC * s:FC * (s + 1)] for s in (0, 2, 1, 3)], axis=1)
            else:
                g = gfull
            d, m2, v2 = _adam_math(w_refs[pi][...], g, m_refs[pi][...], v_refs[pi][...])
            o = out_refs[4 * pi:4 * pi + 4]
            o[0][...], o[1][...], o[2][...], o[3][...] = g, d, m2, v2

    vm = pl.BlockSpec(memory_space=pltpu.VMEM)
    outs = [jax.ShapeDtypeStruct((1, 128), F32)]
    for name in names:
        outs += [jax.ShapeDtypeStruct(w[name].shape, F32)] * 4
    res = pl.pallas_call(
        body, name="adam_small", in_specs=[vm] * (4 + 3 * n), out_specs=[vm] * len(outs), out_shape=outs,
        compiler_params=pltpu.CompilerParams(vmem_limit_bytes=VMEM_LIMIT),
    )(*red, *[w[k] for k in names], *[m[k] for k in names], *[v[k] for k in names])
    return res[0], {name: res[1 + 4 * i:5 + 4 * i] for i, name in enumerate(names)}


_WEIGHTS = ("g_mix", "w_in", "q_norm_g", "k_norm_g", "rec_conv_w", "rec_conv_b", "w_rg", "b_rg", "w_ig", "b_ig",
            "lru_lambda", "g_attn_out", "g_rec_out", "w_out", "g_ffn", "w_up", "ffn_conv_w", "ffn_conv_b", "w_down")
_BIG = ("w_in", "w_out", "w_up", "w_down")
_BIG_PERM = {"w_in": False, "w_out": False, "w_up": True, "w_down": False}
_SMALL_2D = {"w_rg": (RW, HD), "w_ig": (RW, HD), "b_rg": (8, HD), "b_ig": (8, HD), "rec_conv_w": (4, 128),
             "ffn_conv_w": (3, FC)}


def _halves(a):
    r, c = a.shape
    return a.reshape(2, r // 2, c)


def kernel(x, positions, g_mix, w_in, q_norm_g, k_norm_g, rec_conv_w, rec_conv_b, w_rg, b_rg, w_ig, b_ig, lru_lambda, g_attn_out, g_rec_out, w_out, g_ffn, w_up, ffn_conv_w, ffn_conv_b, w_down, loss_target, m_g_mix, m_w_in, m_q_norm_g, m_k_norm_g, m_rec_conv_w, m_rec_conv_b, m_w_rg, m_b_rg, m_w_ig, m_b_ig, m_lru_lambda, m_g_attn_out, m_g_rec_out, m_w_out, m_g_ffn, m_w_up, m_ffn_conv_w, m_ffn_conv_b, m_w_down, v_g_mix, v_w_in, v_q_norm_g, v_k_norm_g, v_rec_conv_w, v_rec_conv_b, v_w_rg, v_b_rg, v_w_ig, v_b_ig, v_lru_lambda, v_g_attn_out, v_g_rec_out, v_w_out, v_g_ffn, v_w_up, v_ffn_conv_w, v_ffn_conv_b, v_w_down):
    given = dict(g_mix=g_mix, w_in=w_in, q_norm_g=q_norm_g, k_norm_g=k_norm_g, rec_conv_w=rec_conv_w, rec_conv_b=rec_conv_b, w_rg=w_rg, b_rg=b_rg, w_ig=w_ig, b_ig=b_ig, lru_lambda=lru_lambda, g_attn_out=g_attn_out, g_rec_out=g_rec_out, w_out=w_out, g_ffn=g_ffn, w_up=w_up, ffn_conv_w=ffn_conv_w, ffn_conv_b=ffn_conv_b, w_down=w_down)
    given_m = dict(g_mix=m_g_mix, w_in=m_w_in, q_norm_g=m_q_norm_g, k_norm_g=m_k_norm_g, rec_conv_w=m_rec_conv_w, rec_conv_b=m_rec_conv_b, w_rg=m_w_rg, b_rg=m_b_rg, w_ig=m_w_ig, b_ig=m_b_ig, lru_lambda=m_lru_lambda, g_attn_out=m_g_attn_out, g_rec_out=m_g_rec_out, w_out=m_w_out, g_ffn=m_g_ffn, w_up=m_w_up, ffn_conv_w=m_ffn_conv_w, ffn_conv_b=m_ffn_conv_b, w_down=m_w_down)
    given_v = dict(g_mix=v_g_mix, w_in=v_w_in, q_norm_g=v_q_norm_g, k_norm_g=v_k_norm_g, rec_conv_w=v_rec_conv_w, rec_conv_b=v_rec_conv_b, w_rg=v_w_rg, b_rg=v_b_rg, w_ig=v_w_ig, b_ig=v_b_ig, lru_lambda=v_lru_lambda, g_attn_out=v_g_attn_out, g_rec_out=v_g_rec_out, w_out=v_w_out, g_ffn=v_g_ffn, w_up=v_w_up, ffn_conv_w=v_ffn_conv_w, ffn_conv_b=v_ffn_conv_b, w_down=v_w_down)
    shapes = {n: a.shape for n, a in given.items()}

    def two_d(n, a):
        a = a[0]
        return a.reshape(_SMALL_2D[n]) if n in _SMALL_2D else (a if a.ndim == 2 else a[None])

    w = {n: two_d(n, a) for n, a in given.items()}
    m = {n: two_d(n, a) for n, a in given_m.items()}
    v = {n: two_d(n, a) for n, a in given_v.items()}
    cc = lax.axis_index("c").astype(jnp.int32)
    cx, cy = lax.axis_index("x").astype(jnp.int32), lax.axis_index("y").astype(jnp.int32)
    slot = {False: 2 * cx + cy, True: 2 * cy + cx}

    shards = {n: _halves(_cast_bf16(f"cast_{n}", w[n])) for n in _BIG}
    small = [(jnp.pad(w["ffn_conv_w"], ((0, 5), (0, 0))), True), (jnp.pad(w["rec_conv_w"], ((0, 4), (0, 0))), False)]
    f_in, f_fcw, f_rcw = _gather_weights([(shards["w_in"], False)], small, slot)
    p = {n: w[n] for n in ("g_mix", "g_ffn", "q_norm_g", "k_norm_g", "rec_conv_b", "lru_lambda", "g_attn_out", "g_rec_out")}
    p.update(w_rg=w["w_rg"].reshape(8, HD, HD), w_ig=w["w_ig"].reshape(8, HD, HD), b_rg=w["b_rg"], b_ig=w["b_ig"],
             w_in=f_in.reshape(NCHIP, D, INW // NCHIP), ffn_conv_w=f_fcw,
             ffn_conv_b=jnp.concatenate([w["ffn_conv_b"][:, FC * s:FC * (s + 1)] for s in (0, 2, 1, 3)], axis=1),
             rec_conv_w=f_rcw.transpose(1, 0, 2).reshape(8, RW))

    class Exchange:
        rest = ("w_out", "w_up", "w_down")
        order = []
        flight = {}

        def start_rest(self):
            srcs = [shards[n] for n in self.rest]
            lands = [lax.dynamic_update_slice(lax.empty((NCHIP,) + s.shape, BF16), s[None], (slot[_BIG_PERM[n]], 0, 0, 0))
                     for n, s in zip(self.rest, srcs)]
            plan = _gather_plan([_BIG_PERM[n] for n in self.rest])
            send, recv, srcs, lands, token = _split_start("gather_rest_start", srcs, lands, plan, 6 * len(srcs))
            self.flight["rest"] = (send, recv, srcs, lands, plan)
            return (token,)

        def wait_rest(self, after):
            send, recv, srcs, lands, plan = self.flight.pop("rest")
            f_out, f_up, f_down = _split_wait("gather_rest_wait", send, recv, srcs, lands, plan, after)
            return dict(w_out=f_out.reshape(D, D), w_up=f_up.reshape(NCHIP, D, FC), w_down=f_down.reshape(DFF, D))

        def reduce_start(self, name, g32, g16):
            r2, cols = shards[name].shape[1:]
            plan = _reduce_plan(_BIG_PERM[name])
            send, recv, srcs, lands, token = _split_start(
                f"reduce_{name}_start", [g16.reshape(NCHIP, 2, r2, cols)], [lax.empty((7, r2, cols), BF16)], plan, 7)
            self.flight[name] = (send, recv, srcs, lands, plan, g32.reshape(NCHIP, 2, r2, cols))
            self.order.append(name)
            return (token,)

        def finish(self, after):
            mine = {}
            for name in self.order:
                send, recv, srcs, lands, plan, g32 = self.flight.pop(name)
                (got,) = _split_wait(f"reduce_{name}_wait", send, recv, srcs, lands, plan, after)
                where = jnp.stack([slot[_BIG_PERM[name]], cc])
                mine[name] = after = _add_pieces(f"reduce_{name}_add", g32, got, where)
            theirs = dict(zip(_BIG, _sibling_share([mine[n] for n in _BIG])))
            return mine, theirs

    exch = Exchange()

    loss_blk, grad_x, g = _local_step(x[0], positions.reshape(T, 1), loss_target[0], p, exch)

    out_g, out_d, out_m, out_v = {}, {}, {}, {}
    red = _small_allreduce(g, loss_blk)
    loss_row, small_out = _adam_small(red, w, m, v)
    for n, (gn, dn, mn, vn) in small_out.items():
        out_g[n], out_d[n], out_m[n], out_v[n] = gn, dn, mn, vn

    mine, theirs = exch.finish(red[0])
    for n in _BIG:
        out_g[n], out_d[n], out_m[n], out_v[n] = _adam_big(f"adam_{n}", w[n], mine[n], theirs[n], m[n], v[n], cc.reshape(1))

    outs = [loss_row[0, 0], grad_x[None]]
    for group in (out_g, out_d, out_m, out_v):
        outs += [group[n].reshape(shapes[n]) for n in _WEIGHTS]
    return tuple(outs)
```

```python
import math

import jax
import jax.numpy as jnp
import numpy as np
from jax import lax
from jax.experimental import pallas as pl
from jax.experimental.pallas import tpu as pltpu

F32 = jnp.float32
BF16 = jnp.bfloat16

T = 4096
D = 1024
HD = 64
AW = 512
RW = 512
INW = 2560
DFF = 3072
NCHIP = 4
EPS = 1e-6
NEG = -1e30
LRU_C = 8.0
ROPE_THETA = 10000.0
BLK = 128
DILATIONS = (1, 4, 16)
ADAM_LR, ADAM_B1, ADAM_B2, ADAM_EPS, ADAM_WD, ADAM_STEP = 0.001, 0.9, 0.999, 1e-08, 0.01, 10
VMEM_LIMIT = 56 * 1024 * 1024
MESH = pl.DeviceIdType.MESH

NN = (((1,), (0,)), ((), ()))
NT = (((1,), (1,)), ((), ()))
TN = (((0,), (0,)), ((), ()))


def _cp(*sem):
    return pltpu.CompilerParams(dimension_semantics=sem, vmem_limit_bytes=VMEM_LIMIT)


def _bs(shape, fn):
    return pl.BlockSpec(shape, fn)


def _dot(a, b, dims=NN):
    return lax.dot_general(a, b, dims, preferred_element_type=F32)


_GC = math.sqrt(2.0 / math.pi)


def _gelu(x):
    return x * (0.5 * (1.0 + jnp.tanh(_GC * (x + 0.044715 * (x * x * x)))))


def _gelu_and_grad(x):
    x2 = x * x
    th = jnp.tanh(_GC * (x + 0.044715 * (x * x2)))
    cdf = 0.5 * (1.0 + th)
    dg = cdf + 0.5 * x * (1.0 - th * th) * (_GC * (1.0 + 3.0 * 0.044715 * x2))
    return x * cdf, dg


def _softplus(x):
    e = jnp.exp(-jnp.abs(x))
    u = 1.0 + e
    l1p = jnp.where(u == 1.0, e, jnp.log(u) * (e / (u - 1.0)))
    return jnp.maximum(x, 0.0) + l1p


def _segsum(z, e_bf16):
    hi = z.astype(BF16)
    lo = (z - hi.astype(F32)).astype(BF16)
    return _dot(hi, e_bf16) + _dot(lo, e_bf16)


def _mm(name, a, b, mode, tm, tn, out_dtype=F32, res=None, stack=0, twin_bf16=False, after=()):
    if mode == "nn":
        (m, k), n = a.shape, (b.shape[1] if not stack else stack * b.shape[2])
        a_spec = _bs((tm, k), lambda j, i: (i, 0))
        if stack:
            per = b.shape[2] // tn
            b_spec = _bs((None, k, tn), lambda j, i: (j // per, 0, j % per))
        else:
            b_spec = _bs((k, tn), lambda j, i: (0, j))
    elif mode == "nt":
        (m, k), n = a.shape, (b.shape[0] if not stack else b.shape[1])
        a_spec = _bs((tm, k), lambda j, i: (i, 0))
        b_spec = _bs((stack, tn, k // stack), lambda j, i: (0, j, 0)) if stack else _bs((tn, k), lambda j, i: (j, 0))
    else:
        (k, m), n = a.shape, b.shape[1]
        a_spec, b_spec = _bs((k, tm), lambda j, i: (0, i)), _bs((k, tn), lambda j, i: (0, j))
    assert m % tm == 0 and n % tn == 0
    o_spec = _bs((tm, tn), lambda j, i: (i, j))
    o_shape = (m, n)
    if mode == "tn" and stack:
        per = n // stack // tn
        o_spec = _bs((None, tm, tn), lambda j, i: (j // per, i, j % per))
        o_shape = (stack, m, n // stack)
    dims = {"nn": NN, "nt": NT, "tn": TN}[mode]

    def product(a_ref, b_ref):
        if mode == "nt" and stack:
            cs = k // stack
            acc = _dot(a_ref[:, 0:cs], b_ref[0], NT)
            for s in range(1, stack):
                acc = acc + _dot(a_ref[:, s * cs:(s + 1) * cs], b_ref[s], NT)
            return acc
        return _dot(a_ref[...], b_ref[...], dims)

    nres = 0 if res is None else 1

    def body(a_ref, b_ref, *rest):
        acc = product(a_ref, b_ref)
        if nres:
            acc = rest[0][...] + acc
        outs = rest[nres + len(after):]
        outs[0][...] = acc.astype(out_dtype)
        if twin_bf16:
            outs[1][...] = acc.astype(BF16)

    ins = (a, b) + ((res,) if nres else ()) + tuple(after)
    specs = [a_spec, b_spec] + ([o_spec] if nres else []) + [pl.BlockSpec(memory_space=pl.ANY)] * len(after)
    shapes = [jax.ShapeDtypeStruct(o_shape, out_dtype)] + ([jax.ShapeDtypeStruct(o_shape, BF16)] if twin_bf16 else [])
    out = pl.pallas_call(
        body, name=name, grid=(n // tn, m // tm), in_specs=specs, out_specs=[o_spec] * len(shapes),
        out_shape=shapes, compiler_params=_cp("parallel", "parallel"),
    )(*ins)
    return tuple(out) if twin_bf16 else out[0]


def _rms_fwd(name, x, g):
    tr = 512

    def body(x_ref, g_ref, o_ref):
        xv = x_ref[...]
        r = lax.rsqrt(jnp.mean(xv * xv, axis=-1, keepdims=True) + EPS)
        o_ref[...] = ((xv * r) * g_ref[...]).astype(BF16)

    return pl.pallas_call(
        body, name=name, grid=(T // tr,), in_specs=[_bs((tr, D), lambda i: (i, 0)), _bs((1, D), lambda i: (0, 0))],
        out_specs=_bs((tr, D), lambda i: (i, 0)), out_shape=jax.ShapeDtypeStruct((T, D), BF16),
        compiler_params=_cp("parallel"),
    )(x, g)


def _rms_bwd(name, x, g, dy, dres, want_bf16):
    tr = 256

    def body(x_ref, g_ref, dy_ref, dr_ref, dx_ref, *rest):
        dg_ref = rest[-1]
        xv, dyv = x_ref[...], dy_ref[...]
        r = lax.rsqrt(jnp.mean(xv * xv, axis=-1, keepdims=True) + EPS)
        gdy = g_ref[...] * dyv
        dx = r * gdy - xv * ((r * r * r) * jnp.mean(xv * gdy, axis=-1, keepdims=True)) + dr_ref[...]
        dx_ref[...] = dx
        if want_bf16:
            rest[0][...] = dx.astype(BF16)

        @pl.when(pl.program_id(0) == 0)
        def _():
            dg_ref[...] = jnp.zeros_like(dg_ref)

        dg_ref[...] += jnp.sum(dyv * (xv * r), axis=0, keepdims=True)

    row = _bs((tr, D), lambda i: (i, 0))
    vec = _bs((1, D), lambda i: (0, 0))
    outs = [jax.ShapeDtypeStruct((T, D), F32)] + ([jax.ShapeDtypeStruct((T, D), BF16)] if want_bf16 else [])
    return pl.pallas_call(
        body, name=name, grid=(T // tr,), in_specs=[row, vec, row, row],
        out_specs=[row] * len(outs) + [vec], out_shape=outs + [jax.ShapeDtypeStruct((1, D), F32)],
        compiler_params=_cp("arbitrary"),
    )(x, g, dy, dres)


def _head_ones():
    idx = np.arange(AW) // HD
    return jnp.asarray((idx[:, None] == idx[None, :]).astype(np.float32), dtype=BF16)


def _freq_row():
    half = HD // 2
    inv = ROPE_THETA ** (-(np.arange(half, dtype=np.float64)) / half)
    return jnp.asarray(np.tile(inv, 4)[None, :], dtype=F32)


def _rot_tables(cos128, sin128):
    c = jnp.tile(cos128, (1, 4))
    s = jnp.tile(sin128, (1, 4))
    lane = lax.broadcasted_iota(jnp.int32, (1, AW), 1)
    first = (lane & 32) == 0
    return c, jnp.where(first, -s, s), first


def _swap_halves(y, first):
    return jnp.where(first, pltpu.roll(y, AW - 32, 1), pltpu.roll(y, 32, 1))


def _qk_prep(proj, pos_col, qg, kg):
    tr = 512

    def body(q_ref, k_ref, pos_ref, f_ref, qg_ref, kg_ref, e_ref, qo_ref, ko_ref, cos_ref, sin_ref):
        ang = pos_ref[...].astype(F32) * f_ref[...]
        cos_ref[...] = jnp.cos(ang)
        sin_ref[...] = jnp.sin(ang)
        c, s_signed, first = _rot_tables(cos_ref[...], sin_ref[...])
        e = e_ref[...]

        def norm_rot(xv, g, scale):
            r = lax.rsqrt(_segsum(xv * xv, e) * (1.0 / HD) + EPS)
            y = (xv * r) * g
            return (y * c + _swap_halves(y, first) * s_signed) * scale

        qo_ref[...] = norm_rot(q_ref[...], qg_ref[...], HD ** -0.5)
        ko_ref[...] = norm_rot(k_ref[...], kg_ref[...], 1.0)

    col = lambda j: _bs((tr, AW), lambda i, j=j: (i, j))
    vec = _bs((1, AW), lambda i: (0, 0))
    out = jax.ShapeDtypeStruct((T, AW), F32)
    tab = jax.ShapeDtypeStruct((T, 128), F32)
    tspec = _bs((tr, 128), lambda i: (i, 0))
    return pl.pallas_call(
        body, name="qk_prep", grid=(T // tr,),
        in_specs=[col(0), col(1), _bs((tr, 1), lambda i: (i, 0)), _bs((1, 128), lambda i: (0, 0)), vec, vec,
                  _bs((AW, AW), lambda i: (0, 0))],
        out_specs=[col(0)] * 2 + [tspec] * 2, out_shape=[out, out, tab, tab], compiler_params=_cp("parallel"),
    )(proj, proj, pos_col, _freq_row(), qg, kg, _head_ones())


def _qk_bwd(proj, cos_t, sin_t, qg, kg, dq, dk, dv):
    tr = 256

    def body(q_ref, k_ref, cos_ref, sin_ref, qg_ref, kg_ref, e_ref, dq_ref, dk_ref, dv_ref, o_ref, dqg_ref, dkg_ref):
        i, j = pl.program_id(0), pl.program_id(1)

        @pl.when((i == 0) & (j == 0))
        def _():
            dqg_ref[...] = jnp.zeros_like(dqg_ref)
            dkg_ref[...] = jnp.zeros_like(dkg_ref)

        def norm_rot_bwd(x_ref, g_ref, dg_ref, d_ref, scale):
            c, s_signed, first = _rot_tables(cos_ref[...], sin_ref[...])
            e = e_ref[...]
            dout = d_ref[...] * scale
            dy = dout * c + _swap_halves(dout * s_signed, first)
            xv, g = x_ref[...], g_ref[...]
            r = lax.rsqrt(_segsum(xv * xv, e) * (1.0 / HD) + EPS)
            gdy = g * dy
            dx = r * gdy - xv * ((r * r * r) * (_segsum(xv * gdy, e) * (1.0 / HD)))
            o_ref[...] = dx.astype(BF16)
            dg_ref[...] += jnp.sum(dy * (xv * r), axis=0, keepdims=True)

        @pl.when(j == 0)
        def _():
            norm_rot_bwd(q_ref, qg_ref, dqg_ref, dq_ref, HD ** -0.5)

        @pl.when(j == 1)
        def _():
            norm_rot_bwd(k_ref, kg_ref, dkg_ref, dk_ref, 1.0)

        @pl.when(j == 2)
        def _():
            o_ref[...] = dv_ref[...].astype(BF16)

    col = lambda jj: _bs((tr, AW), lambda i, j, jj=jj: (i, jj))
    vec = _bs((1, AW), lambda i, j: (0, 0))
    piece = _bs((tr, AW), lambda i, j: (i, 0))
    return pl.pallas_call(
        body, name="qk_bwd", grid=(T // tr, 3),
        in_specs=[col(0), col(1), _bs((tr, 128), lambda i, j: (i, 0)), _bs((tr, 128), lambda i, j: (i, 0)), vec, vec,
                  _bs((AW, AW), lambda i, j: (0, 0))] + [piece] * 3,
        out_specs=[_bs((tr, AW), lambda i, j: (i, j)), vec, vec],
        out_shape=[jax.ShapeDtypeStruct((T, INW), BF16), jax.ShapeDtypeStruct((1, AW), F32),
                   jax.ShapeDtypeStruct((1, AW), F32)],
        compiler_params=_cp("arbitrary", "arbitrary"),
    )(proj, proj, cos_t, sin_t, qg, kg, _head_ones(), dq, dk, dv)


RG = 256
QC = 64


def _stacked_band_mask(rows=2 * BLK, q0=0):
    qi = (lax.broadcasted_iota(jnp.int32, (rows, 2 * BLK), 0) + q0) & (BLK - 1)
    kj = lax.broadcasted_iota(jnp.int32, (rows, 2 * BLK), 1)
    rel = qi - kj + BLK
    return (rel >= 0) & (rel <= BLK), lax.broadcasted_iota(jnp.int32, (1, 2 * BLK), 1) >= BLK


def _natural_rows(r0, n_rows, d):
    if d == 1:
        return pl.ds(r0, n_rows)
    ln = T // d
    return pl.ds(r0 // ln + d * (r0 % ln), n_rows, stride=d)


def _regroup_into(dst, src_ref, d, pad, cast=True):
    def step(j, carry):
        r0 = pl.multiple_of(j * RG, RG)
        val = src_ref[_natural_rows(r0, RG, d), :]
        dst[pl.ds(pad + r0, RG), :] = val.astype(dst.dtype) if cast else val
        return carry
    lax.fori_loop(0, T // RG, step, 0)


def _stack_heads(x, h0):
    zero = jnp.zeros_like(x)
    return jnp.concatenate([jnp.where(h0, x, zero), jnp.where(h0, zero, x)], axis=0)


def _attn_fwd(q, k, proj):
    nblk = T // BLK

    def body(q_ref, k_ref, v_ref, a_ref, lse_ref, qs, ks, vs, o0, o1, o2, l0, l1, l2, sb0, sb1):
        band, cur_half = _stacked_band_mask()
        h0 = lax.broadcasted_iota(jnp.int32, (1, 128), 1) < HD
        ks[0:BLK, :] = jnp.zeros((BLK, 128), BF16)
        vs[0:BLK, :] = jnp.zeros((BLK, 128), BF16)
        for d, o_s, l_s in zip(DILATIONS, (o0, o1, o2), (l0, l1, l2)):
            nb = T // d // BLK
            _regroup_into(qs, q_ref, d, 0)
            _regroup_into(ks, k_ref, d, BLK)
            _regroup_into(vs, v_ref, d, BLK)

            def scores(b):
                r0 = pl.multiple_of(b * BLK, BLK)
                return _dot(_stack_heads(qs[pl.ds(r0, BLK), :], h0), ks[pl.ds(r0, 2 * BLK), :], NT)

            def finish(b, s_raw, d=d, nb=nb, o_s=o_s, l_s=l_s):
                r0 = pl.multiple_of(b * BLK, BLK)
                mask = band & (cur_half | ((b & (nb - 1)) > 0))
                s = jnp.where(mask, s_raw, NEG)
                m = jnp.max(s, axis=1, keepdims=True)
                p = jnp.exp(s - m)
                l = jnp.sum(p, axis=1, keepdims=True)
                o = _dot(p.astype(BF16), vs[pl.ds(r0, 2 * BLK), :]) / l
                lse = m + jnp.log(l)
                rows = _natural_rows(r0, BLK, d)
                o_s[rows, :] = jnp.where(h0, o[0:BLK, :], o[BLK:, :])
                l_s[rows, :] = jnp.where(h0, lse[0:BLK, :], lse[BLK:, :])

            sb0[...] = scores(0)

            def step(i, carry):
                b = 2 * i
                sb1[...] = scores(b + 1)
                finish(b, sb0[...])
                sb0[...] = scores(jnp.minimum(b + 2, nblk - 1))
                finish(b + 1, sb1[...])
                return carry

            lax.fori_loop(0, nblk // 2, step, 0)

        def merge(i, carry):
            r = pl.ds(pl.multiple_of(i * RG, RG), RG)
            la, lb, lc = l0[r, :], l1[r, :], l2[r, :]
            m = jnp.maximum(jnp.maximum(la, lb), lc)
            ea, eb, ec = jnp.exp(la - m), jnp.exp(lb - m), jnp.exp(lc - m)
            z = (ea + eb) + ec
            a_ref[r, :] = ((ea * o0[r, :] + eb * o1[r, :]) + ec * o2[r, :]) / z
            lse_ref[r, :] = m + jnp.log(z)
            return carry

        lax.fori_loop(0, T // RG, merge, 0)

    spec = lambda cb: _bs((T, 128), lambda p, cb=cb: (0, cb + p))
    out = jax.ShapeDtypeStruct((T, AW), F32)
    return pl.pallas_call(
        body, name="attn_fwd", grid=(AW // 128,), in_specs=[spec(0), spec(0), spec(8)], out_specs=[spec(0)] * 2,
        out_shape=[out] * 2,
        scratch_shapes=[pltpu.VMEM((T, 128), BF16), pltpu.VMEM((T + BLK, 128), BF16), pltpu.VMEM((T + BLK, 128), BF16)]
        + [pltpu.VMEM((T, 128), F32)] * 6 + [pltpu.VMEM((2 * BLK, 2 * BLK), F32)] * 2,
        compiler_params=_cp("parallel"),
    )(q, k, proj)


def _attn_bwd(q, k, proj, do, lse, delta):
    nblk = T // BLK

    def body(q_ref, k_ref, v_ref, do_ref, l_ref, dl_ref, dq_ref, dk_ref, dv_ref, qs, dos, ks, vs, ls, dls, dks, dvs,
             sa0, sa1, da0, da1):
        band, cur_half = _stacked_band_mask()
        h0 = lax.broadcasted_iota(jnp.int32, (1, 128), 1) < HD
        ks[0:BLK, :] = jnp.zeros((BLK, 128), BF16)
        vs[0:BLK, :] = jnp.zeros((BLK, 128), BF16)
        for d in DILATIONS:
            nb = T // d // BLK
            _regroup_into(qs, q_ref, d, 0)
            _regroup_into(dos, do_ref, d, 0)
            _regroup_into(ks, k_ref, d, BLK)
            _regroup_into(vs, v_ref, d, BLK)
            _regroup_into(ls, l_ref, d, 0, cast=False)
            _regroup_into(dls, dl_ref, d, 0, cast=False)
            dks[...] = jnp.zeros_like(dks)
            dvs[...] = jnp.zeros_like(dvs)

            def scores(b, s_buf, dp_buf):
                r0 = pl.multiple_of(b * BLK, BLK)
                win = pl.ds(r0, 2 * BLK)
                s_buf[...] = _dot(_stack_heads(qs[pl.ds(r0, BLK), :], h0), ks[win, :], NT)
                dp_buf[...] = _dot(_stack_heads(dos[pl.ds(r0, BLK), :], h0), vs[win, :], NT)

            def finish(b, s_buf, dp_buf, d=d, nb=nb):
                r0 = pl.multiple_of(b * BLK, BLK)
                mask = band & (cur_half | ((b & (nb - 1)) > 0))
                win = pl.ds(r0, 2 * BLK)
                lv, dlv = ls[pl.ds(r0, BLK), :], dls[pl.ds(r0, BLK), :]
                lse2 = jnp.concatenate([lv[:, 0:1], lv[:, HD:HD + 1]], axis=0)
                dl2 = jnp.concatenate([dlv[:, 0:1], dlv[:, HD:HD + 1]], axis=0)
                p = jnp.exp(jnp.where(mask, s_buf[...], NEG) - lse2)
                ds = p * (dp_buf[...] - dl2)
                pb, dsb = p.astype(BF16), ds.astype(BF16)
                dq2 = _dot(dsb, ks[win, :])
                dks[win, :] += _dot(dsb, _stack_heads(qs[pl.ds(r0, BLK), :], h0), TN)
                dvs[win, :] += _dot(pb, _stack_heads(dos[pl.ds(r0, BLK), :], h0), TN)
                rows = _natural_rows(r0, BLK, d)
                dq = jnp.where(h0, dq2[0:BLK, :], dq2[BLK:, :])
                dq_ref[rows, :] = dq if d == 1 else dq_ref[rows, :] + dq

            scores(0, sa0, da0)

            def step(i, carry):
                b = 2 * i
                scores(b + 1, sa1, da1)
                finish(b, sa0, da0)
                scores(jnp.minimum(b + 2, nblk - 1), sa0, da0)
                finish(b + 1, sa1, da1)
                return carry

            lax.fori_loop(0, nblk // 2, step, 0)

            def back(j, carry, d=d):
                r0 = pl.multiple_of(j * RG, RG)
                rows = _natural_rows(r0, RG, d)
                src = pl.ds(BLK + r0, RG)
                dk_ref[rows, :] = dks[src, :] if d == 1 else dk_ref[rows, :] + dks[src, :]
                dv_ref[rows, :] = dvs[src, :] if d == 1 else dv_ref[rows, :] + dvs[src, :]
                return carry

            lax.fori_loop(0, T // RG, back, 0)

    spec = lambda cb: pl.BlockSpec((T, 128), lambda p, cb=cb: (0, cb + p), pipeline_mode=pl.Buffered(1))
    ospec = _bs((T, 128), lambda p: (0, p))
    out = jax.ShapeDtypeStruct((T, AW), F32)
    return pl.pallas_call(
        body, name="attn_bwd", grid=(AW // 128,), in_specs=[spec(0), spec(0), spec(8), spec(0), spec(0), spec(0)],
        out_specs=[ospec] * 3, out_shape=[out] * 3,
        scratch_shapes=[pltpu.VMEM((T, 128), BF16), pltpu.VMEM((T, 128), BF16), pltpu.VMEM((T + BLK, 128), BF16),
                        pltpu.VMEM((T + BLK, 128), BF16), pltpu.VMEM((T, 128), F32), pltpu.VMEM((T, 128), F32),
                        pltpu.VMEM((T + BLK, 128), F32), pltpu.VMEM((T + BLK, 128), F32)]
        + [pltpu.VMEM((2 * BLK, 2 * BLK), F32)] * 4,
        compiler_params=_cp("parallel"),
    )(q, k, proj, do, lse, delta)


def _attn_norm(attn, g_attn):
    tr = 512

    def body(a_ref, g_ref, mix_ref):
        attn = a_ref[...]
        r = lax.rsqrt(jnp.mean(attn * attn, axis=-1, keepdims=True) + EPS)
        mix_ref[...] = ((attn * r) * g_ref[...]).astype(BF16)

    row = _bs((tr, AW), lambda i: (i, 0))
    return pl.pallas_call(
        body, name="attn_norm", grid=(T // tr,), in_specs=[row, _bs((1, AW), lambda i: (0, 0))],
        out_specs=row, out_shape=jax.ShapeDtypeStruct((T, D), BF16), compiler_params=_cp("parallel"),
    )(attn, g_attn)


def _attn_out_bwd(attn, dmix, g_attn):
    tr = 256

    def body(a_ref, d_ref, g_ref, e_ref, do_ref, dl_ref, dg_ref):
        av, dyv = a_ref[...], d_ref[...]
        r = lax.rsqrt(jnp.mean(av * av, axis=-1, keepdims=True) + EPS)
        gdy = g_ref[...] * dyv
        da = r * gdy - av * ((r * r * r) * jnp.mean(av * gdy, axis=-1, keepdims=True))
        do_ref[...] = da
        dl_ref[...] = _segsum(da * av, e_ref[...])

        @pl.when(pl.program_id(0) == 0)
        def _():
            dg_ref[...] = jnp.zeros_like(dg_ref)

        dg_ref[...] += jnp.sum(dyv * (av * r), axis=0, keepdims=True)

    row = _bs((tr, AW), lambda i: (i, 0))
    vec = _bs((1, AW), lambda i: (0, 0))
    return pl.pallas_call(
        body, name="attn_out_bwd", grid=(T // tr,), in_specs=[row, row, vec, _bs((AW, AW), lambda i: (0, 0))],
        out_specs=[row, row, vec],
        out_shape=[jax.ShapeDtypeStruct((T, AW), F32), jax.ShapeDtypeStruct((T, AW), F32),
                   jax.ShapeDtypeStruct((1, AW), F32)],
        compiler_params=_cp("arbitrary"),
    )(attn, dmix, g_attn, _head_ones())


TRR = 256


def _scan_fwd(a, u):
    n = a.shape[0]
    row = lax.broadcasted_iota(jnp.int32, (n, 1), 0)
    s = 1
    while s < n:
        keep = row >= s
        u = jnp.where(keep, a * pltpu.roll(u, s, 0) + u, u)
        a = jnp.where(keep, a * pltpu.roll(a, s, 0), a)
        s *= 2
    return a, u


def _scan_bwd(c, w):
    n = c.shape[0]
    row = lax.broadcasted_iota(jnp.int32, (n, 1), 0)
    s = 1
    while s < n:
        keep = row < n - s
        w = jnp.where(keep, c * pltpu.roll(w, n - s, 0) + w, w)
        c = jnp.where(keep, c * pltpu.roll(c, n - s, 0), c)
        s *= 2
    return w


def _gates(xc, wrg, wig, brg, big, sp):
    xcb = xc.astype(BF16)
    r = jax.nn.sigmoid(_dot(xcb, wrg) + brg)
    ig = jax.nn.sigmoid(_dot(xcb, wig) + big)
    la = (-LRU_C * r) * sp
    a = jnp.exp(la)
    mult = jnp.sqrt(-jnp.tanh(la) * (a * a + 1.0))
    return r, ig, a, mult


def _conv4(ext_ref, xr, cw_ref, cb_ref, n):
    y = cb_ref[...] + ext_ref[pl.ds(5, n), :] * cw_ref[0:1, :]
    y = y + ext_ref[pl.ds(6, n), :] * cw_ref[1:2, :]
    y = y + ext_ref[pl.ds(7, n), :] * cw_ref[2:3, :]
    return y + xr * cw_ref[3:4, :]


def _rec_fwd(proj, mix, cw, cb, wrg, wig, brg, big, lam, g_rec):
    n = TRR

    def body(xr_ref, gr_ref, cw_ref, cb_ref, wrg_ref, wig_ref, brg_ref, big_ref, lam_ref, g_ref, mix_in,
             mix_ref, h_ref, ext, hcar):
        del mix_in

        @pl.when(pl.program_id(0) == 0)
        def _():
            ext[0:8, :] = jnp.zeros((8, RW), F32)
            hcar[...] = jnp.zeros_like(hcar)

        xr = xr_ref[...]
        ext[8:, :] = xr
        xc = _conv4(ext, xr, cw_ref, cb_ref, n)
        ext[0:8, :] = xr[n - 8:, :]
        sp = _softplus(-lam_ref[...])
        _, ig, a, mult = _gates(xc, wrg_ref[...], wig_ref[...], brg_ref[...], big_ref[...], sp)
        a_s, u_s = _scan_fwd(a, mult * (ig * xc))
        h = u_s + a_s * hcar[7:8, :]
        h_ref[...] = h
        hcar[...] = h[n - 8:, :]
        pre = h * _gelu(gr_ref[...])
        r = lax.rsqrt(jnp.mean(pre * pre, axis=-1, keepdims=True) + EPS)
        mix_ref[...] = ((pre * r) * g_ref[...]).astype(BF16)

    vec = _bs((1, RW), lambda i: (0, 0))
    mat = _bs((RW, RW), lambda i: (0, 0))
    return pl.pallas_call(
        body, name="rec_fwd", grid=(T // n,),
        in_specs=[_bs((n, RW), lambda i: (i, 3)), _bs((n, RW), lambda i: (i, 4)), _bs((8, RW), lambda i: (0, 0)), vec,
                  mat, mat, vec, vec, vec, vec, pl.BlockSpec(memory_space=pl.ANY)],
        out_specs=[_bs((n, RW), lambda i: (i, 1)), _bs((n, RW), lambda i: (i, 0))],
        out_shape=[jax.ShapeDtypeStruct((T, D), BF16), jax.ShapeDtypeStruct((T, RW), F32)],
        scratch_shapes=[pltpu.VMEM((n + 8, RW), F32), pltpu.VMEM((8, RW), F32)],
        input_output_aliases={10: 0}, compiler_params=_cp("arbitrary"),
    )(proj, proj, cw, cb, wrg, wig, brg, big, lam, g_rec, mix)


def _rec_bwd(proj, h, dmix, dproj, cw, cb, wrg, wig, brg, big, lam, g_rec):
    n = TRR
    nt = T // n
    hb = n // 8

    def body(xr_ref, xh_ref, gr_ref, h_ref, hh_ref, dm_ref, cw_ref, cb_ref, wrg_ref, wig_ref, brg_ref, big_ref,
             lam_ref, g_ref, dp_in, dp_ref, xc_ref, dr_ref, di_ref, dcw_ref, dcb_ref, dbr_ref, dbi_ref, dsp_ref,
             dg_ref, ext, exth, extd, adh, dgr_s):
        del dp_in
        i, j = pl.program_id(0), pl.program_id(1)
        first_tile = i == nt - 1
        last_tile = i == 0

        @pl.when(j == 0)
        def _():
            @pl.when(last_tile)
            def _():
                for ref in (dcw_ref, dcb_ref, dbr_ref, dbi_ref, dsp_ref, dg_ref):
                    ref[...] = jnp.zeros_like(ref)
                extd[n:, :] = jnp.zeros((8, RW), F32)
                adh[...] = jnp.zeros_like(adh)

            row = lax.broadcasted_iota(jnp.int32, (n, 1), 0)
            xr = xr_ref[...]
            ext[0:8, :] = jnp.where(first_tile, 0.0, xh_ref[...])
            ext[8:, :] = xr
            xc = _conv4(ext, xr, cw_ref, cb_ref, n)
            sp = _softplus(-lam_ref[...])
            wrg, wig = wrg_ref[...], wig_ref[...]
            r, ig, a, mult = _gates(xc, wrg, wig, brg_ref[...], big_ref[...], sp)

            hv = h_ref[...]
            gl, dgl = _gelu_and_grad(gr_ref[...])
            pre = hv * gl
            dyv = dm_ref[...]
            rr = lax.rsqrt(jnp.mean(pre * pre, axis=-1, keepdims=True) + EPS)
            gdy = g_ref[...] * dyv
            dpre = rr * gdy - pre * ((rr * rr * rr) * jnp.mean(pre * gdy, axis=-1, keepdims=True))
            dg_ref[...] += jnp.sum(dyv * (pre * rr), axis=0, keepdims=True)
            dgr_s[...] = dpre * hv * dgl

            is_last_row = row == n - 1
            w = dpre * gl + jnp.where(is_last_row, adh[0:1, :], 0.0)
            c = jnp.where(is_last_row, 0.0, pltpu.roll(a, n - 1, 0))
            dh = _scan_bwd(c, w)
            adh[...] = (a * dh)[0:8, :]

            exth[0:8, :] = jnp.where(first_tile, 0.0, hh_ref[...])
            exth[8:, :] = hv
            da = dh * exth[pl.ds(7, n), :]
            ixc = ig * xc
            dmult = dh * ixc
            dla = da * a - dmult * ((a * a) / mult)
            dsp_ref[...] += jnp.sum(dla * (-LRU_C * r), axis=0, keepdims=True)
            dpr = (dla * (-LRU_C * sp)) * (r * (1.0 - r))
            dpi = (dh * (mult * xc)) * (ig * (1.0 - ig))
            dprb, dpib = dpr.astype(BF16), dpi.astype(BF16)
            dxc = dh * (mult * ig) + _dot(dprb, wrg, NT) + _dot(dpib, wig, NT)
            dbr_ref[...] += jnp.sum(dpr, axis=0, keepdims=True)
            dbi_ref[...] += jnp.sum(dpi, axis=0, keepdims=True)
            xc_ref[...] = xc.astype(BF16)
            dr_ref[...] = dprb
            di_ref[...] = dpib

            extd[0:n, :] = dxc
            dxr = dxc * cw_ref[3:4, :] + extd[pl.ds(1, n), :] * cw_ref[2:3, :]
            dxr = dxr + extd[pl.ds(2, n), :] * cw_ref[1:2, :] + extd[pl.ds(3, n), :] * cw_ref[0:1, :]
            extd[n:, :] = dxc[0:8, :]
            dcb_ref[...] += jnp.sum(dxc, axis=0, keepdims=True)
            for kk in range(4):
                dcw_ref[kk:kk + 1, :] += jnp.sum(dxc * ext[pl.ds(5 + kk, n), :], axis=0, keepdims=True)

            @pl.when(first_tile)
            def _():
                dsp_ref[...] = dsp_ref[...] * (-jax.nn.sigmoid(-lam_ref[...]))

            dp_ref[...] = dxr.astype(BF16)

        @pl.when(j == 1)
        def _():
            dp_ref[...] = dgr_s[...].astype(BF16)

    vec = _bs((1, RW), lambda i, j: (0, 0))
    mat = _bs((RW, RW), lambda i, j: (0, 0))
    tile = lambda cblk: _bs((n, RW), lambda i, j, cblk=cblk: (nt - 1 - i, cblk))
    halo = lambda cblk: _bs((8, RW), lambda i, j, cblk=cblk: (jnp.maximum((nt - 1 - i) * hb - 1, 0), cblk))
    bt = jax.ShapeDtypeStruct((T, RW), BF16)
    v = jax.ShapeDtypeStruct((1, RW), F32)
    return pl.pallas_call(
        body, name="rec_bwd", grid=(nt, 2),
        in_specs=[tile(3), halo(3), tile(4), tile(0), halo(0), tile(1), _bs((8, RW), lambda i, j: (0, 0)), vec,
                  mat, mat, vec, vec, vec, vec, pl.BlockSpec(memory_space=pl.ANY)],
        out_specs=[_bs((n, RW), lambda i, j: (nt - 1 - i, 3 + j)), tile(0), tile(0), tile(0),
                   _bs((8, RW), lambda i, j: (0, 0)), vec, vec, vec, vec, vec],
        out_shape=[jax.ShapeDtypeStruct((T, INW), BF16), bt, bt, bt, jax.ShapeDtypeStruct((8, RW), F32), v, v, v, v, v],
        scratch_shapes=[pltpu.VMEM((n + 8, RW), F32), pltpu.VMEM((n + 8, RW), F32), pltpu.VMEM((n + 8, RW), F32),
                        pltpu.VMEM((8, RW), F32), pltpu.VMEM((n, RW), F32)],
        input_output_aliases={14: 0}, compiler_params=_cp("arbitrary", "arbitrary"),
    )(proj, proj, proj, h, h, dmix, cw, cb, wrg, wig, brg, big, lam, g_rec, dproj)


FC = 1536
TRF = 256


LC = 128


def _taps(x_ref, edge, cols, r):
    if r == 0:
        return edge[pl.ds(6, 8), cols], edge[pl.ds(7, 8), cols], edge[pl.ds(8, 8), cols]
    return x_ref[pl.ds(r - 2, 8), cols], x_ref[pl.ds(r - 1, 8), cols], x_ref[pl.ds(r, 8), cols]


def _ffn_act(up_pre, cw, cb):
    n = TRF
    hb = n // 8

    def body(g_ref, gh_ref, u_ref, uh_ref, wg_ref, wu_ref, bg_ref, bu_ref, o_ref, eg, eu):
        first = pl.program_id(1) == 0
        eg[0:8, :] = jnp.where(first, 0.0, gh_ref[...])
        eg[8:, :] = g_ref[0:8, :]
        eu[0:8, :] = jnp.where(first, 0.0, uh_ref[...])
        eu[8:, :] = u_ref[0:8, :]

        def column(ci, carry):
            cols = pl.ds(pl.multiple_of(ci * LC, LC), LC)
            wg = [wg_ref[kk:kk + 1, cols] for kk in range(3)]
            wu = [wu_ref[kk:kk + 1, cols] for kk in range(3)]
            bg, bu = bg_ref[:, cols], bu_ref[:, cols]
            for r in range(0, n, 16):
                res = []
                for rr in (r, r + 8):
                    g0, g1, g2 = _taps(g_ref, eg, cols, rr)
                    u0, u1, u2 = _taps(u_ref, eu, cols, rr)
                    ug = ((bg + g0 * wg[0]) + g1 * wg[1]) + g2 * wg[2]
                    uu = ((bu + u0 * wu[0]) + u1 * wu[1]) + u2 * wu[2]
                    res.append(_gelu(ug) * uu)
                o_ref[pl.ds(r, 16), cols] = jnp.concatenate(res, axis=0).astype(BF16)
            return carry

        lax.fori_loop(0, FC // LC, column, 0)

    main = lambda o: _bs((n, FC), lambda j, i, o=o: (i, 2 * j + o))
    halo = lambda o: _bs((8, FC), lambda j, i, o=o: (jnp.maximum(i * hb - 1, 0), 2 * j + o))
    wsp = lambda o: _bs((None, 8, FC), lambda j, i, o=o: (2 * j + o, 0, 0))
    bsp = lambda o: _bs((1, FC), lambda j, i, o=o: (0, 2 * j + o))
    return pl.pallas_call(
        body, name="ffn_act", grid=(2, T // n),
        in_specs=[main(0), halo(0), main(1), halo(1), wsp(0), wsp(1), bsp(0), bsp(1)],
        out_specs=_bs((n, FC), lambda j, i: (i, j)), out_shape=jax.ShapeDtypeStruct((T, DFF), BF16),
        scratch_shapes=[pltpu.VMEM((16, FC), F32)] * 2, compiler_params=_cp("parallel", "parallel"),
    )(up_pre, up_pre, up_pre, up_pre, cw, cw, cb, cb)


def _ffn_bwd(up_pre, dact, cw, cb):
    n = TRF
    hb = n // 8
    nt = T // n
    m = n + 8

    def body(g_ref, gp_ref, gn_ref, u_ref, up_ref, un_ref, d_ref, dn_ref, wg_ref, wu_ref, bg_ref, bu_ref,
             o_ref, dw_ref, db_ref, eg0, eg1, eu0, eu1, dug_s, duu_s):
        i = pl.program_id(1)
        first, last = i == 0, i == nt - 1

        @pl.when(first)
        def _():
            dw_ref[...] = jnp.zeros_like(dw_ref)
            db_ref[...] = jnp.zeros_like(db_ref)

        eg0[0:8, :] = jnp.where(first, 0.0, gp_ref[...])
        eg0[8:, :] = g_ref[0:8, :]
        eg1[0:8, :] = g_ref[n - 8:, :]
        eg1[8:, :] = gn_ref[...]
        eu0[0:8, :] = jnp.where(first, 0.0, up_ref[...])
        eu0[8:, :] = u_ref[0:8, :]
        eu1[0:8, :] = u_ref[n - 8:, :]
        eu1[8:, :] = un_ref[...]

        def column(ci, carry):
            cols = pl.ds(pl.multiple_of(ci * LC, LC), LC)
            ucols = pl.ds(pl.multiple_of(FC + ci * LC, LC), LC)
            wg = [wg_ref[kk:kk + 1, cols] for kk in range(3)]
            wu = [wu_ref[kk:kk + 1, cols] for kk in range(3)]
            bg, bu = bg_ref[:, cols], bu_ref[:, cols]
            zero = jnp.zeros((8, LC), F32)
            acc = [zero] * 8
            for r in range(0, n + 8, 8):
                if r == n:
                    gt = (eg1[pl.ds(6, 8), cols], eg1[pl.ds(7, 8), cols], eg1[pl.ds(8, 8), cols])
                    ut = (eu1[pl.ds(6, 8), cols], eu1[pl.ds(7, 8), cols], eu1[pl.ds(8, 8), cols])
                    dv = jnp.where(last, 0.0, dn_ref[:, cols])
                else:
                    gt, ut = _taps(g_ref, eg0, cols, r), _taps(u_ref, eu0, cols, r)
                    dv = d_ref[pl.ds(r, 8), cols]
                gl, dgl = _gelu_and_grad(((bg + gt[0] * wg[0]) + gt[1] * wg[1]) + gt[2] * wg[2])
                uu = ((bu + ut[0] * wu[0]) + ut[1] * wu[1]) + ut[2] * wu[2]
                dug, duu = dv * uu * dgl, dv * gl
                dug_s[pl.ds(r, 8), :] = dug
                duu_s[pl.ds(r, 8), :] = duu
                if r < n:
                    acc = [acc[0] + dug * gt[0], acc[1] + dug * gt[1], acc[2] + dug * gt[2],
                           acc[3] + duu * ut[0], acc[4] + duu * ut[1], acc[5] + duu * ut[2], acc[6] + dug, acc[7] + duu]
            for r in range(0, n, 16):
                og, ou = [], []
                for rr in (r, r + 8):
                    og.append((dug_s[pl.ds(rr, 8), :] * wg[2] + dug_s[pl.ds(rr + 1, 8), :] * wg[1])
                              + dug_s[pl.ds(rr + 2, 8), :] * wg[0])
                    ou.append((duu_s[pl.ds(rr, 8), :] * wu[2] + duu_s[pl.ds(rr + 1, 8), :] * wu[1])
                              + duu_s[pl.ds(rr + 2, 8), :] * wu[0])
                o_ref[pl.ds(r, 16), cols] = jnp.concatenate(og, axis=0).astype(BF16)
                o_ref[pl.ds(r, 16), ucols] = jnp.concatenate(ou, axis=0).astype(BF16)
            for kk in range(3):
                dw_ref[kk:kk + 1, cols] += jnp.sum(acc[kk], axis=0, keepdims=True)
                dw_ref[kk:kk + 1, ucols] += jnp.sum(acc[3 + kk], axis=0, keepdims=True)
            db_ref[:, cols] += jnp.sum(acc[6], axis=0, keepdims=True)
            db_ref[:, ucols] += jnp.sum(acc[7], axis=0, keepdims=True)
            return carry

        lax.fori_loop(0, FC // LC, column, 0)

    main = lambda o: _bs((n, FC), lambda j, i, o=o: (i, 2 * j + o))
    prev = lambda o: _bs((8, FC), lambda j, i, o=o: (jnp.maximum(i * hb - 1, 0), 2 * j + o))
    nxt = lambda o: _bs((8, FC), lambda j, i, o=o: (jnp.minimum((i + 1) * hb, T // 8 - 1), 2 * j + o))
    wsp = lambda o: _bs((None, 8, FC), lambda j, i, o=o: (2 * j + o, 0, 0))
    bsp = lambda o: _bs((1, FC), lambda j, i, o=o: (0, 2 * j + o))
    return pl.pallas_call(
        body, name="ffn_bwd", grid=(2, nt),
        in_specs=[main(0), prev(0), nxt(0), main(1), prev(1), nxt(1), _bs((n, FC), lambda j, i: (i, j)),
                  _bs((8, FC), lambda j, i: (jnp.minimum((i + 1) * hb, T // 8 - 1), j)), wsp(0), wsp(1), bsp(0), bsp(1)],
        out_specs=[_bs((n, 2 * FC), lambda j, i: (i, j)), _bs((8, 2 * FC), lambda j, i: (0, j)),
                   _bs((1, 2 * FC), lambda j, i: (0, j))],
        out_shape=[jax.ShapeDtypeStruct((T, 2 * DFF), BF16), jax.ShapeDtypeStruct((8, 2 * DFF), F32),
                   jax.ShapeDtypeStruct((1, 2 * DFF), F32)],
        scratch_shapes=[pltpu.VMEM((16, FC), F32)] * 4 + [pltpu.VMEM((m, LC), F32)] * 2,
        compiler_params=_cp("parallel", "arbitrary"),
    )(up_pre, up_pre, up_pre, up_pre, up_pre, up_pre, dact, dact, cw, cw, cb, cb)


def _down_loss(act, w_down, x1, target):
    tm, tn = 512, 512

    def body(a_ref, b_ref, r_ref, t_ref, dy_ref, dyb_ref, l_ref):
        @pl.when((pl.program_id(0) == 0) & (pl.program_id(1) == 0))
        def _():
            l_ref[...] = jnp.zeros_like(l_ref)

        err = (r_ref[...] + _dot(a_ref[...], b_ref[...])) - t_ref[...]
        dy = err * (1.0 / D)
        dy_ref[...] = dy
        dyb_ref[...] = dy.astype(BF16)
        l_ref[...] += jnp.sum(0.5 * (err * err) * (1.0 / D))

    o_spec = _bs((tm, tn), lambda j, i: (i, j))
    return pl.pallas_call(
        body, name="down_loss", grid=(D // tn, T // tm),
        in_specs=[_bs((tm, DFF), lambda j, i: (i, 0)), _bs((DFF, tn), lambda j, i: (0, j)), o_spec, o_spec],
        out_specs=[o_spec, o_spec, _bs((8, 128), lambda j, i: (0, 0))],
        out_shape=[jax.ShapeDtypeStruct((T, D), F32), jax.ShapeDtypeStruct((T, D), BF16),
                   jax.ShapeDtypeStruct((8, 128), F32)],
        compiler_params=_cp("arbitrary", "arbitrary"),
    )(act, w_down, x1, target)


def _block_diag(w):
    eye = jnp.eye(8, dtype=w.dtype)
    return (w[:, :, None, :] * eye[:, None, :, None]).reshape(RW, RW).astype(BF16)


def _diag_blocks(m):
    return jnp.stack([m[HD * b:HD * (b + 1), HD * b:HD * (b + 1)] for b in range(8)])


def _local_step(x, pos_col, target, p, exch):
    qg, kg = jnp.tile(p["q_norm_g"], (1, 8)), jnp.tile(p["k_norm_g"], (1, 8))
    wrg, wig = _block_diag(p["w_rg"]), _block_diag(p["w_ig"])
    brg, big = p["b_rg"].reshape(1, RW), p["b_ig"].reshape(1, RW)

    h1 = _rms_fwd("rms1", x, p["g_mix"])
    proj = _mm("mm_in", h1, p["w_in"], "nn", 512, 640, stack=NCHIP, after=exch.start_rest())
    q, k, cos_t, sin_t = _qk_prep(proj, pos_col, qg, kg)
    attn, lse = _attn_fwd(q, k, proj)
    mix = _attn_norm(attn, p["g_attn_out"])
    mix, hseq = _rec_fwd(proj, mix, p["rec_conv_w"], p["rec_conv_b"], wrg, wig, brg, big, p["lru_lambda"], p["g_rec_out"])
    rest = exch.wait_rest(mix)
    x1 = _mm("mm_out", mix, rest["w_out"], "nn", 512, 512, res=x)
    h2 = _rms_fwd("rms2", x1, p["g_ffn"])
    up_pre = _mm("mm_up", h2, rest["w_up"], "nn", 512, 768, stack=NCHIP)
    act = _ffn_act(up_pre, p["ffn_conv_w"], p["ffn_conv_b"])
    dy, dyb, loss_blk = _down_loss(act, rest["w_down"], x1, target)

    g = {}
    tok = exch.reduce_start("w_down", *_mm("wg_down", act, dyb, "tn", 512, 512, twin_bf16=True))
    dact = _mm("dg_down", dyb, rest["w_down"], "nt", 512, 512, after=tok)
    dup, g["ffn_conv_w"], g["ffn_conv_b"] = _ffn_bwd(up_pre, dact, p["ffn_conv_w"], p["ffn_conv_b"])
    tok = exch.reduce_start("w_up", *_mm("wg_up", h2, dup, "tn", 512, 768, stack=NCHIP, twin_bf16=True))
    dh2 = _mm("dg_up", dup, rest["w_up"], "nt", 512, 256, stack=NCHIP, after=tok)
    dx1, dx1b, g["g_ffn"] = _rms_bwd("rms2_bwd", x1, p["g_ffn"], dh2, dy, True)
    tok = exch.reduce_start("w_out", *_mm("wg_out", mix, dx1b, "tn", 512, 512, twin_bf16=True))
    dmix = _mm("dg_out", dx1b, rest["w_out"], "nt", 512, 512, after=tok)
    do, delta, g["g_attn_out"] = _attn_out_bwd(attn, dmix, p["g_attn_out"])
    dq, dk, dv = _attn_bwd(q, k, proj, do, lse, delta)
    dproj, dqg, dkg = _qk_bwd(proj, cos_t, sin_t, qg, kg, dq, dk, dv)
    (dproj, xcb, dprb, dpib, g["rec_conv_w"], g["rec_conv_b"], dbr, dbi, dsp, g["g_rec_out"]) = _rec_bwd(
        proj, hseq, dmix, dproj, p["rec_conv_w"], p["rec_conv_b"], wrg, wig, brg, big, p["lru_lambda"], p["g_rec_out"])
    g["w_rg"] = _diag_blocks(_mm("wg_rg", xcb, dprb, "tn", 512, 512)).reshape(RW, HD)
    g["w_ig"] = _diag_blocks(_mm("wg_ig", xcb, dpib, "tn", 512, 512)).reshape(RW, HD)
    g["b_rg"], g["b_ig"] = dbr.reshape(8, HD), dbi.reshape(8, HD)
    g["lru_lambda"] = dsp
    g["q_norm_g"] = dqg.reshape(8, HD).sum(axis=0, keepdims=True)
    g["k_norm_g"] = dkg.reshape(8, HD).sum(axis=0, keepdims=True)
    tok = exch.reduce_start("w_in", *_mm("wg_in", h1, dproj, "tn", 512, 640, stack=NCHIP, twin_bf16=True))
    dh1 = _mm("dg_in", dproj, p["w_in"], "nt", 512, 512, stack=NCHIP, after=tok)
    grad_x, g["g_mix"] = _rms_bwd("rms1_bwd", x, p["g_mix"], dh1, dx1, False)
    return loss_blk, grad_x, g


ANY = pl.BlockSpec(memory_space=pl.ANY)


def _mesh_pos():
    return lax.axis_index("x"), lax.axis_index("y"), lax.axis_index("c")


def _slot(px, py, perm):
    return 2 * py + px if perm else 2 * px + py


def _other_chips(x, y):
    return [(1 - x, y), (x, 1 - y), (1 - x, 1 - y)]


def _rcopy(src, dst, send, recv, k, to, kr=None):
    return pltpu.make_async_remote_copy(src_ref=src, dst_ref=dst, send_sem=send.at[k],
                                        recv_sem=recv.at[k if kr is None else kr], device_id=to, device_id_type=MESH)


def _cast_bf16(name, w):
    r, c = w.shape
    tr = 128
    def body(w_ref, o_ref):
        o_ref[...] = w_ref[...].astype(BF16)
    return pl.pallas_call(
        body, name=name, grid=(r // tr,), in_specs=[_bs((tr, c), lambda i: (i, 0))],
        out_specs=_bs((tr, c), lambda i: (i, 0)), out_shape=jax.ShapeDtypeStruct((r, c), BF16),
        compiler_params=_cp("parallel"),
    )(w)


def _gather_weights(big, small, slot):
    nb, ns = len(big), len(small)
    perms = [p for _, p in big] + [p for _, p in small]

    def body(*refs):
        ins, outs = refs[:nb + ns], refs[2 * (nb + ns):3 * (nb + ns)]
        send, recv = refs[3 * (nb + ns):]
        x, y, c = _mesh_pos()
        me, sib = (x, y, c), (x, y, 1 - c)
        chips = _other_chips(x, y)
        first = []
        for a in range(nb):
            for j, (px, py) in enumerate(chips):
                first.append(_rcopy(ins[a].at[c], outs[a].at[_slot(x, y, perms[a]), c], send, recv, 3 * a + j, (px, py, c)))
        for t in range(ns):
            a = nb + t
            for j, (px, py) in enumerate(chips):
                first.append(_rcopy(ins[a], outs[a].at[_slot(x, y, perms[a])], send, recv, 6 * nb + 3 * t + j, (px, py, c)))
        for cp in first:
            cp.start()
        passed = []
        for a in range(nb):
            for j, (px, py) in enumerate(chips):
                got = outs[a].at[_slot(px, py, perms[a]), c]
                _rcopy(got, got, send, recv, 3 * a + j, me).wait_recv()
                fwd = _rcopy(got, got, send, recv, 3 * nb + 3 * a + j, sib)
                fwd.start()
                passed.append(fwd)
        for a in range(nb):
            for j, (px, py) in enumerate(chips):
                got = outs[a].at[_slot(px, py, perms[a]), 1 - c]
                _rcopy(got, got, send, recv, 3 * nb + 3 * a + j, me).wait_recv()
        for t in range(ns):
            a = nb + t
            for j, (px, py) in enumerate(chips):
                got = outs[a].at[_slot(px, py, perms[a])]
                _rcopy(got, got, send, recv, 6 * nb + 3 * t + j, me).wait_recv()
        for cp in first + passed:
            cp.wait_send()

    arrs = [a for a, _ in big] + [a for a, _ in small]
    lands = [lax.dynamic_update_slice(lax.empty((NCHIP,) + a.shape, a.dtype), a[None], (slot[p],) + (0,) * a.ndim)
             for a, p in zip(arrs, perms)]
    nsem = 6 * nb + 3 * ns
    return pl.pallas_call(
        body, name="gather_weights", in_specs=[ANY] * (2 * (nb + ns)), out_specs=[ANY] * (nb + ns),
        out_shape=[jax.ShapeDtypeStruct(a.shape, a.dtype) for a in lands],
        input_output_aliases={nb + ns + i: i for i in range(nb + ns)},
        scratch_shapes=[pltpu.SemaphoreType.DMA((nsem,)), pltpu.SemaphoreType.DMA((nsem,))],
    )(*arrs, *lands)


HBM = pl.BlockSpec(memory_space=pltpu.HBM)
SEM = pl.BlockSpec(memory_space=pltpu.SEMAPHORE)
EFFECT = pltpu.SideEffectType.DATAFLOW_SIDE_EFFECTING


def _split_start(name, srcs, lands, plan, nsem):
    ns, nl = len(srcs), len(lands)

    def body(*refs):
        send, recv = refs[ns + nl], refs[ns + nl + 1]
        sends, _ = plan(refs[:ns], refs[ns:ns + nl], send, recv)
        for cp in sends:
            cp.start()
        refs[-1][...] = jnp.zeros((8, 128), F32)

    arrs = list(srcs) + list(lands)
    out = pl.pallas_call(
        body, name=name, in_specs=[HBM] * (ns + nl),
        out_specs=[SEM, SEM] + [HBM] * (ns + nl) + [pl.BlockSpec(memory_space=pltpu.VMEM)],
        out_shape=[pltpu.SemaphoreType.DMA((nsem,)), pltpu.SemaphoreType.DMA((nsem,))]
        + [pltpu.HBM(a.shape, a.dtype) for a in arrs] + [jax.ShapeDtypeStruct((8, 128), F32)],
        input_output_aliases={i: 2 + i for i in range(ns + nl)},
        compiler_params=pltpu.CompilerParams(has_side_effects=EFFECT),
    )(*[pltpu.with_memory_space_constraint(a, pltpu.HBM) for a in arrs])
    return out[0], out[1], out[2:2 + ns], out[2 + ns:2 + ns + nl], out[-1]


def _split_wait(name, send, recv, srcs, lands, plan, after):
    ns, nl = len(srcs), len(lands)

    def body(*refs):
        sends, recvs = plan(refs[:ns], refs[ns:ns + nl], refs[ns + nl], refs[ns + nl + 1])
        for cp in sends:
            cp.wait_send()
        for cp in recvs:
            cp.wait_recv()

    arrs = list(srcs) + list(lands)
    out = pl.pallas_call(
        body, name=name, in_specs=[HBM] * (ns + nl) + [SEM, SEM, ANY], out_specs=[HBM] * (ns + nl),
        out_shape=[pltpu.HBM(a.shape, a.dtype) for a in arrs],
        input_output_aliases={i: i for i in range(ns + nl)},
        compiler_params=pltpu.CompilerParams(has_side_effects=EFFECT),
    )(*arrs, send, recv, after)
    return out[ns:]


def _gather_plan(perms):
    def plan(srcs, lands, send, recv):
        x, y, c = _mesh_pos()
        sends, recvs = [], []
        for a, perm in enumerate(perms):
            for j, (px, py) in enumerate(_other_chips(x, y)):
                for cc in (0, 1):
                    k = 6 * a + 2 * j + cc
                    sends.append(_rcopy(srcs[a].at[c], lands[a].at[_slot(x, y, perm), c], send, recv, k, (px, py, cc),
                                        kr=6 * a + 2 * j + c))
                    got = lands[a].at[_slot(px, py, perm), cc]
                    recvs.append(_rcopy(got, got, send, recv, k, (x, y, c)))
        return sends, recvs
    return plan


def _reduce_plan(perm):
    def plan(srcs, lands, send, recv):
        x, y, c = _mesh_pos()
        src, land = srcs[0], lands[0]
        sends = []
        for j, (px, py) in enumerate(_other_chips(x, y)):
            for hf in (0, 1):
                sends.append(_rcopy(src.at[_slot(px, py, perm), hf], land.at[2 * j + c], send, recv, 2 * j + hf,
                                    (px, py, hf), kr=2 * j + c))
        sends.append(_rcopy(src.at[_slot(x, y, perm), 1 - c], land.at[6], send, recv, 6, (x, y, 1 - c)))
        recvs = [_rcopy(land.at[i], land.at[i], send, recv, i, (x, y, c)) for i in range(7)]
        return sends, recvs
    return plan


def _sibling_share(rs):
    na = len(rs)

    def body(*refs):
        ins, outs, (send, recv) = refs[:na], refs[na:2 * na], refs[2 * na:]
        x, y, c = _mesh_pos()
        cps = [_rcopy(ins[a], outs[a], send, recv, a, (x, y, 1 - c)) for a in range(na)]
        for cp in cps:
            cp.start()
        for cp in cps:
            cp.wait()

    return pl.pallas_call(
        body, name="rs_share", in_specs=[ANY] * na, out_specs=[ANY] * na,
        out_shape=[jax.ShapeDtypeStruct(r.shape, F32) for r in rs],
        scratch_shapes=[pltpu.SemaphoreType.DMA((na,)), pltpu.SemaphoreType.DMA((na,))],
    )(*rs)


def _add_pieces(name, g, got, where):
    _, _, r2, cc = g.shape
    tr = 128

    def body(w_ref, g_ref, r_ref, o_ref):
        del w_ref
        acc = g_ref[...]
        for i in range(7):
            acc = acc + r_ref[i].astype(F32)
        o_ref[...] = acc

    return pl.pallas_call(
        body, name=name,
        grid_spec=pltpu.PrefetchScalarGridSpec(
            num_scalar_prefetch=1, grid=(r2 // tr,),
            in_specs=[_bs((None, None, tr, cc), lambda i, w_ref: (w_ref[0], w_ref[1], i, 0)),
                      _bs((7, tr, cc), lambda i, w_ref: (0, i, 0))],
            out_specs=_bs((tr, cc), lambda i, w_ref: (i, 0))),
        out_shape=jax.ShapeDtypeStruct((r2, cc), F32), compiler_params=_cp("parallel"),
    )(where, g, got)


def _adam_math(w, g, m, v):
    m = ADAM_B1 * m + (1.0 - ADAM_B1) * g
    v = ADAM_B2 * v + (1.0 - ADAM_B2) * (g * g)
    m_hat = m / (1.0 - ADAM_B1 ** ADAM_STEP)
    v_hat = v / (1.0 - ADAM_B2 ** ADAM_STEP)
    return -ADAM_LR * (m_hat / (jnp.sqrt(v_hat) + ADAM_EPS) + ADAM_WD * w), m, v


def _adam_big(name, w, g_mine, g_sib, m, v, c_arr):
    r, cols = w.shape
    tr = 128
    per = r // 2 // tr

    def body(c_ref, w_ref, a_ref, b_ref, m_ref, v_ref, g_ref, d_ref, m2_ref, v2_ref):
        g = jnp.where(pl.program_id(0) == c_ref[0], a_ref[...], b_ref[...])
        g_ref[...] = g
        d_ref[...], m2_ref[...], v2_ref[...] = _adam_math(w_ref[...], g, m_ref[...], v_ref[...])

    spec = _bs((tr, cols), lambda h, i, c_ref: (h * per + i, 0))
    half = _bs((tr, cols), lambda h, i, c_ref: (i, 0))
    out = jax.ShapeDtypeStruct((r, cols), F32)
    return pl.pallas_call(
        body, name=name,
        grid_spec=pltpu.PrefetchScalarGridSpec(
            num_scalar_prefetch=1, grid=(2, per), in_specs=[spec, half, half, spec, spec], out_specs=[spec] * 4),
        out_shape=[out] * 4, compiler_params=_cp("parallel", "parallel"),
    )(c_arr, w, g_mine, g_sib, m, v)


_CLASS_SHAPE = {"a": (8, D), "b": (8, RW), "c": (8, 2 * DFF), "d": (1048, HD)}
_SMALL = (
    ("g_mix", "a", 0, 1, D), ("g_ffn", "a", 1, 1, D),
    ("rec_conv_w", "b", 0, 4, RW), ("rec_conv_b", "b", 4, 1, RW), ("lru_lambda", "b", 5, 1, RW),
    ("g_attn_out", "b", 6, 1, RW), ("g_rec_out", "b", 7, 1, RW),
    ("ffn_conv_w", "c", 0, 3, 2 * DFF), ("ffn_conv_b", "c", 3, 1, 2 * DFF),
    ("w_rg", "d", 0, RW, HD), ("w_ig", "d", RW, RW, HD), ("b_rg", "d", 2 * RW, 8, HD), ("b_ig", "d", 2 * RW + 8, 8, HD),
    ("q_norm_g", "d", 2 * RW + 16, 1, HD), ("k_norm_g", "d", 2 * RW + 17, 1, HD),
)
_LOSS_ROW = 2
_CLASSES = ("a", "b", "c", "d")


def _small_allreduce(g, loss_blk):
    names = [s[0] for s in _SMALL]
    nin = len(names) + 1

    def body(*refs):
        ins = dict(zip(names, refs[:len(names)]))
        loss_ref = refs[len(names)]
        outs = dict(zip(_CLASSES, refs[nin:nin + 4]))
        pair = dict(zip(_CLASSES, refs[nin + 4:nin + 8]))
        quad = dict(zip(_CLASSES, refs[nin + 8:nin + 12]))
        send, recv = refs[nin + 12:]
        x, y, c = _mesh_pos()
        chip = 2 * x + y
        pair["a"][c] = jnp.zeros(_CLASS_SHAPE["a"], F32)
        pair["b"][c] = ins["rec_conv_w"][...]
        pair["c"][c] = ins["ffn_conv_w"][...]
        pair["d"][c, 2 * RW + 16:, :] = jnp.zeros((8, HD), F32)
        for name, k, r0, nr, _ in _SMALL:
            if name in ("rec_conv_w", "ffn_conv_w"):
                continue
            pair[k][c, r0:r0 + nr, :] = ins[name][...]
        pair["a"][c, _LOSS_ROW:_LOSS_ROW + 1, :] = jnp.broadcast_to(loss_ref[0:1, 0:1], (1, D))
        cps = [_rcopy(pair[k].at[c], pair[k].at[c], send, recv, ki, (x, y, 1 - c)) for ki, k in enumerate(_CLASSES)]
        for cp in cps:
            cp.start()
        for ki, k in enumerate(_CLASSES):
            _rcopy(pair[k].at[1 - c], pair[k].at[1 - c], send, recv, ki, (x, y, c)).wait_recv()
            quad[k][chip] = pair[k][0] + pair[k][1]
        cps2 = []
        for ki, k in enumerate(_CLASSES):
            for j, (px, py) in enumerate(_other_chips(x, y)):
                cps2.append(_rcopy(quad[k].at[chip], quad[k].at[chip], send, recv, 4 + 3 * ki + j, (px, py, c)))
        for cp in cps2:
            cp.start()
        for ki, k in enumerate(_CLASSES):
            for j, (px, py) in enumerate(_other_chips(x, y)):
                got = quad[k].at[2 * px + py]
                _rcopy(got, got, send, recv, 4 + 3 * ki + j, (x, y, c)).wait_recv()
            outs[k][...] = ((quad[k][0] + quad[k][1]) + quad[k][2]) + quad[k][3]
        for cp in cps + cps2:
            cp.wait_send()

    vm = pl.BlockSpec(memory_space=pltpu.VMEM)
    return pl.pallas_call(
        body, name="small_allreduce", in_specs=[vm] * nin, out_specs=[vm] * 4,
        out_shape=[jax.ShapeDtypeStruct(_CLASS_SHAPE[k], F32) for k in _CLASSES],
        scratch_shapes=[pltpu.VMEM((2,) + _CLASS_SHAPE[k], F32) for k in _CLASSES]
        + [pltpu.VMEM((NCHIP,) + _CLASS_SHAPE[k], F32) for k in _CLASSES]
        + [pltpu.SemaphoreType.DMA((16,)), pltpu.SemaphoreType.DMA((16,))],
        compiler_params=pltpu.CompilerParams(vmem_limit_bytes=VMEM_LIMIT),
    )(*[g[n] for n in names], loss_blk)


def _adam_small(red, w, m, v):
    names = [s[0] for s in _SMALL]
    n = len(names)

    def body(*refs):
        red_refs = dict(zip(_CLASSES, refs[:4]))
        w_refs, m_refs, v_refs = refs[4:4 + n], refs[4 + n:4 + 2 * n], refs[4 + 2 * n:4 + 3 * n]
        loss_ref = refs[4 + 3 * n]
        out_refs = refs[5 + 3 * n:]
        x, y, _ = _mesh_pos()
        chip = 2 * x + y
        loss_ref[...] = jnp.broadcast_to(red_refs["a"][_LOSS_ROW:_LOSS_ROW + 1, 0:1], loss_ref.shape)
        for pi, (name, k, r0, nr, width) in enumerate(_SMALL):
            gfull = red_refs[k][r0:r0 + nr, :]
            if name == "rec_conv_w":
                parts = [gfull[:, 128 * s:128 * (s + 1)] for s in range(NCHIP)]
                g = jnp.where(chip == 0, parts[0], jnp.where(chip == 1, parts[1], jnp.where(chip == 2, parts[2], parts[3])))
            elif name == "ffn_conv_w":
                parts = [gfull[:, FC * s:FC * (s + 1)] for s in range(NCHIP)]
                g = jnp.where(chip == 0, parts[0], jnp.where(chip == 1, parts[2], jnp.where(chip == 2, parts[1], parts[3])))
            elif name == "ffn_conv_b":
                g = jnp.concatenate([gfull[:, FC * s:FC * (s + 1)] for s in (0, 2, 1, 3)], axis=1)
            else:
                g = gfull
            d, m2, v2 = _adam_math(w_refs[pi][...], g, m_refs[pi][...], v_refs[pi][...])
            o = out_refs[4 * pi:4 * pi + 4]
            o[0][...], o[1][...], o[2][...], o[3][...] = g, d, m2, v2

    vm = pl.BlockSpec(memory_space=pltpu.VMEM)
    outs = [jax.ShapeDtypeStruct((1, 128), F32)]
    for name in names:
        outs += [jax.ShapeDtypeStruct(w[name].shape, F32)] * 4
    res = pl.pallas_call(
        body, name="adam_small", in_specs=[vm] * (4 + 3 * n), out_specs=[vm] * len(outs), out_shape=outs,
        compiler_params=pltpu.CompilerParams(vmem_limit_bytes=VMEM_LIMIT),
    )(*red, *[w[k] for k in names], *[m[k] for k in names], *[v[k] for k in names])
    return res[0], {name: res[1 + 4 * i:5 + 4 * i] for i, name in enumerate(names)}


_WEIGHTS = ("g_mix", "w_in", "q_norm_g", "k_norm_g", "rec_conv_w", "rec_conv_b", "w_rg", "b_rg", "w_ig", "b_ig",
            "lru_lambda", "g_attn_out", "g_rec_out", "w_out", "g_ffn", "w_up", "ffn_conv_w", "ffn_conv_b", "w_down")
_BIG = ("w_in", "w_out", "w_up", "w_down")
_BIG_PERM = {"w_in": False, "w_out": False, "w_up": True, "w_down": False}
_SMALL_2D = {"w_rg": (RW, HD), "w_ig": (RW, HD), "b_rg": (8, HD), "b_ig": (8, HD), "rec_conv_w": (4, 128),
             "ffn_conv_w": (3, FC)}


def _halves(a):
    r, c = a.shape
    return a.reshape(2, r // 2, c)


def kernel(x, positions, g_mix, w_in, q_norm_g, k_norm_g, rec_conv_w, rec_conv_b, w_rg, b_rg, w_ig, b_ig, lru_lambda, g_attn_out, g_rec_out, w_out, g_ffn, w_up, ffn_conv_w, ffn_conv_b, w_down, loss_target, m_g_mix, m_w_in, m_q_norm_g, m_k_norm_g, m_rec_conv_w, m_rec_conv_b, m_w_rg, m_b_rg, m_w_ig, m_b_ig, m_lru_lambda, m_g_attn_out, m_g_rec_out, m_w_out, m_g_ffn, m_w_up, m_ffn_conv_w, m_ffn_conv_b, m_w_down, v_g_mix, v_w_in, v_q_norm_g, v_k_norm_g, v_rec_conv_w, v_rec_conv_b, v_w_rg, v_b_rg, v_w_ig, v_b_ig, v_lru_lambda, v_g_attn_out, v_g_rec_out, v_w_out, v_g_ffn, v_w_up, v_ffn_conv_w, v_ffn_conv_b, v_w_down):
    given = dict(g_mix=g_mix, w_in=w_in, q_norm_g=q_norm_g, k_norm_g=k_norm_g, rec_conv_w=rec_conv_w, rec_conv_b=rec_conv_b, w_rg=w_rg, b_rg=b_rg, w_ig=w_ig, b_ig=b_ig, lru_lambda=lru_lambda, g_attn_out=g_attn_out, g_rec_out=g_rec_out, w_out=w_out, g_ffn=g_ffn, w_up=w_up, ffn_conv_w=ffn_conv_w, ffn_conv_b=ffn_conv_b, w_down=w_down)
    given_m = dict(g_mix=m_g_mix, w_in=m_w_in, q_norm_g=m_q_norm_g, k_norm_g=m_k_norm_g, rec_conv_w=m_rec_conv_w, rec_conv_b=m_rec_conv_b, w_rg=m_w_rg, b_rg=m_b_rg, w_ig=m_w_ig, b_ig=m_b_ig, lru_lambda=m_lru_lambda, g_attn_out=m_g_attn_out, g_rec_out=m_g_rec_out, w_out=m_w_out, g_ffn=m_g_ffn, w_up=m_w_up, ffn_conv_w=m_ffn_conv_w, ffn_conv_b=m_ffn_conv_b, w_down=m_w_down)
    given_v = dict(g_mix=v_g_mix, w_in=v_w_in, q_norm_g=v_q_norm_g, k_norm_g=v_k_norm_g, rec_conv_w=v_rec_conv_w, rec_conv_b=v_rec_conv_b, w_rg=v_w_rg, b_rg=v_b_rg, w_ig=v_w_ig, b_ig=v_b_ig, lru_lambda=v_lru_lambda, g_attn_out=v_g_attn_out, g_rec_out=v_g_rec_out, w_out=v_w_out, g_ffn=v_g_ffn, w_up=v_w_up, ffn_conv_w=v_ffn_conv_w, ffn_conv_b=v_ffn_conv_b, w_down=v_w_down)
    shapes = {n: a.shape for n, a in given.items()}

    def two_d(n, a):
        a = a[0]
        return a.reshape(_SMALL_2D[n]) if n in _SMALL_2D else (a if a.ndim == 2 else a[None])

    w = {n: two_d(n, a) for n, a in given.items()}
    m = {n: two_d(n, a) for n, a in given_m.items()}
    v = {n: two_d(n, a) for n, a in given_v.items()}
    cc = lax.axis_index("c").astype(jnp.int32)
    cx, cy = lax.axis_index("x").astype(jnp.int32), lax.axis_index("y").astype(jnp.int32)
    slot = {False: 2 * cx + cy, True: 2 * cy + cx}

    shards = {n: _halves(_cast_bf16(f"cast_{n}", w[n])) for n in _BIG}
    small = [(jnp.pad(w["ffn_conv_w"], ((0, 5), (0, 0))), True), (jnp.pad(w["rec_conv_w"], ((0, 4), (0, 0))), False)]
    f_in, f_fcw, f_rcw = _gather_weights([(shards["w_in"], False)], small, slot)
    p = {n: w[n] for n in ("g_mix", "g_ffn", "q_norm_g", "k_norm_g", "rec_conv_b", "lru_lambda", "g_attn_out", "g_rec_out")}
    p.update(w_rg=w["w_rg"].reshape(8, HD, HD), w_ig=w["w_ig"].reshape(8, HD, HD), b_rg=w["b_rg"], b_ig=w["b_ig"],
             w_in=f_in.reshape(NCHIP, D, INW // NCHIP), ffn_conv_w=f_fcw,
             ffn_conv_b=jnp.concatenate([w["ffn_conv_b"][:, FC * s:FC * (s + 1)] for s in (0, 2, 1, 3)], axis=1),
             rec_conv_w=f_rcw.transpose(1, 0, 2).reshape(8, RW))

    class Exchange:
        rest = ("w_out", "w_up", "w_down")
        order = []
        flight = {}

        def start_rest(self):
            srcs = [shards[n] for n in self.rest]
            lands = [lax.dynamic_update_slice(lax.empty((NCHIP,) + s.shape, BF16), s[None], (slot[_BIG_PERM[n]], 0, 0, 0))
                     for n, s in zip(self.rest, srcs)]
            plan = _gather_plan([_BIG_PERM[n] for n in self.rest])
            send, recv, srcs, lands, token = _split_start("gather_rest_start", srcs, lands, plan, 6 * len(srcs))
            self.flight["rest"] = (send, recv, srcs, lands, plan)
            return (token,)

        def wait_rest(self, after):
            send, recv, srcs, lands, plan = self.flight.pop("rest")
            f_out, f_up, f_down = _split_wait("gather_rest_wait", send, recv, srcs, lands, plan, after)
            return dict(w_out=f_out.reshape(D, D), w_up=f_up.reshape(NCHIP, D, FC), w_down=f_down.reshape(DFF, D))

        def reduce_start(self, name, g32, g16):
            r2, cols = shards[name].shape[1:]
            plan = _reduce_plan(_BIG_PERM[name])
            send, recv, srcs, lands, token = _split_start(
                f"reduce_{name}_start", [g16.reshape(NCHIP, 2, r2, cols)], [lax.empty((7, r2, cols), BF16)], plan, 7)
            self.flight[name] = (send, recv, srcs, lands, plan, g32.reshape(NCHIP, 2, r2, cols))
            self.order.append(name)
            return (token,)

        def finish(self, after):
            mine = {}
            for name in self.order:
                send, recv, srcs, lands, plan, g32 = self.flight.pop(name)
                (got,) = _split_wait(f"reduce_{name}_wait", send, recv, srcs, lands, plan, after)
                where = jnp.stack([slot[_BIG_PERM[name]], cc])
                mine[name] = after = _add_pieces(f"reduce_{name}_add", g32, got, where)
            theirs = dict(zip(_BIG, _sibling_share([mine[n] for n in _BIG])))
            return mine, theirs

    exch = Exchange()

    loss_blk, grad_x, g = _local_step(x[0], positions.reshape(T, 1), loss_target[0], p, exch)

    out_g, out_d, out_m, out_v = {}, {}, {}, {}
    red = _small_allreduce(g, loss_blk)
    loss_row, small_out = _adam_small(red, w, m, v)
    for n, (gn, dn, mn, vn) in small_out.items():
        out_g[n], out_d[n], out_m[n], out_v[n] = gn, dn, mn, vn

    mine, theirs = exch.finish(red[0])
    for n in _BIG:
        out_g[n], out_d[n], out_m[n], out_v[n] = _adam_big(f"adam_{n}", w[n], mine[n], theirs[n], m[n], v[n], cc.reshape(1))

    outs = [loss_row[0, 0], grad_x[None]]
    for group in (out_g, out_d, out_m, out_v):
        outs += [group[n].reshape(shapes[n]) for n in _WEIGHTS]
    return tuple(outs)
```

```python
import math

import jax
import jax.numpy as jnp
import numpy as np
from jax import lax
from jax.experimental import pallas as pl
from jax.experimental.pallas import tpu as pltpu

F32 = jnp.float32
BF16 = jnp.bfloat16

T = 4096
D = 1024
HD = 64
AW = 512
RW = 512
INW = 2560
DFF = 3072
NCHIP = 4
EPS = 1e-6
NEG = -1e30
LRU_C = 8.0
ROPE_THETA = 10000.0
BLK = 128
DILATIONS = (1, 4, 16)
ADAM_LR, ADAM_B1, ADAM_B2, ADAM_EPS, ADAM_WD, ADAM_STEP = 0.001, 0.9, 0.999, 1e-08, 0.01, 10
VMEM_LIMIT = 56 * 1024 * 1024
MESH = pl.DeviceIdType.MESH

NN = (((1,), (0,)), ((), ()))
NT = (((1,), (1,)), ((), ()))
TN = (((0,), (0,)), ((), ()))


def _cp(*sem):
    return pltpu.CompilerParams(dimension_semantics=sem, vmem_limit_bytes=VMEM_LIMIT)


def _bs(shape, fn):
    return pl.BlockSpec(shape, fn)


def _dot(a, b, dims=NN):
    return lax.dot_general(a, b, dims, preferred_element_type=F32)


_GC = math.sqrt(2.0 / math.pi)


def _gelu(x):
    return x * (0.5 * (1.0 + jnp.tanh(_GC * (x + 0.044715 * (x * x * x)))))


def _gelu_and_grad(x):
    x2 = x * x
    th = jnp.tanh(_GC * (x + 0.044715 * (x * x2)))
    cdf = 0.5 * (1.0 + th)
    dg = cdf + 0.5 * x * (1.0 - th * th) * (_GC * (1.0 + 3.0 * 0.044715 * x2))
    return x * cdf, dg


def _softplus(x):
    e = jnp.exp(-jnp.abs(x))
    u = 1.0 + e
    l1p = jnp.where(u == 1.0, e, jnp.log(u) * (e / (u - 1.0)))
    return jnp.maximum(x, 0.0) + l1p


def _segsum(z, e_bf16):
    hi = z.astype(BF16)
    lo = (z - hi.astype(F32)).astype(BF16)
    return _dot(hi, e_bf16) + _dot(lo, e_bf16)


def _mm(name, a, b, mode, tm, tn, out_dtype=F32, res=None, stack=0, twin_bf16=False, after=(), a_full=False,
        b_full=False):
    if mode == "nn":
        (m, k), n = a.shape, (b.shape[1] if not stack else stack * b.shape[2])
        a_spec = _bs((tm, k), lambda j, i: (i, 0))
        if stack:
            per = b.shape[2] // tn
            b_spec = _bs((None, k, tn), lambda j, i: (j // per, 0, j % per))
        else:
            b_spec = _bs((k, tn), lambda j, i: (0, j))
    elif mode == "nt":
        (m, k), n = a.shape, (b.shape[0] if not stack else b.shape[1])
        a_spec = _bs((tm, k), lambda j, i: (i, 0))
        b_spec = _bs((stack, tn, k // stack), lambda j, i: (0, j, 0)) if stack else _bs((tn, k), lambda j, i: (j, 0))
    else:
        (k, m), n = a.shape, b.shape[1]
        a_spec, b_spec = _bs((k, tm), lambda j, i: (0, i)), _bs((k, tn), lambda j, i: (0, j))
    assert m % tm == 0 and n % tn == 0
    o_spec = _bs((tm, tn), lambda j, i: (i, j))
    o_shape = (m, n)
    if mode == "tn" and stack:
        per = n // stack // tn
        o_spec = _bs((None, tm, tn), lambda j, i: (j // per, i, j % per))
        o_shape = (stack, m, n // stack)
    dims = {"nn": NN, "nt": NT, "tn": TN}[mode]
    once = pl.Buffered(1)
    if a_full:
        a_spec = pl.BlockSpec(a.shape, lambda j, i: (0, 0), pipeline_mode=once)
    if b_full:
        assert n == tn
        b_spec = pl.BlockSpec(b_spec.block_shape, b_spec.index_map, pipeline_mode=once)

    def product(a_ref, b_ref):
        if a_full:
            mine = pl.ds(pl.multiple_of(pl.program_id(1) * tm, tm), tm)
            take = (lambda cols: a_ref[:, mine]) if mode == "tn" else (lambda cols: a_ref[mine, cols])
        else:
            take = lambda cols: a_ref[:, cols]
        if mode == "nt" and stack:
            cs = k // stack
            acc = _dot(take(pl.ds(0, cs)), b_ref[0], NT)
            for s in range(1, stack):
                acc = acc + _dot(take(pl.ds(s * cs, cs)), b_ref[s], NT)
            return acc
        return _dot(take(slice(None)), b_ref[...], dims)

    nres = 0 if res is None else 1

    def body(a_ref, b_ref, *rest):
        acc = product(a_ref, b_ref)
        if nres:
            acc = rest[0][...] + acc
        outs = rest[nres + len(after):]
        outs[0][...] = acc.astype(out_dtype)
        if twin_bf16:
            outs[1][...] = acc.astype(BF16)

    ins = (a, b) + ((res,) if nres else ()) + tuple(after)
    specs = [a_spec, b_spec] + ([o_spec] if nres else []) + [pl.BlockSpec(memory_space=pl.ANY)] * len(after)
    shapes = [jax.ShapeDtypeStruct(o_shape, out_dtype)] + ([jax.ShapeDtypeStruct(o_shape, BF16)] if twin_bf16 else [])
    out = pl.pallas_call(
        body, name=name, grid=(n // tn, m // tm), in_specs=specs, out_specs=[o_spec] * len(shapes),
        out_shape=shapes, compiler_params=_cp("parallel", "parallel"),
    )(*ins)
    return tuple(out) if twin_bf16 else out[0]


def _rms_fwd(name, x, g):
    tr = 512

    def body(x_ref, g_ref, o_ref):
        xv = x_ref[...]
        r = lax.rsqrt(jnp.mean(xv * xv, axis=-1, keepdims=True) + EPS)
        o_ref[...] = ((xv * r) * g_ref[...]).astype(BF16)

    return pl.pallas_call(
        body, name=name, grid=(T // tr,), in_specs=[_bs((tr, D), lambda i: (i, 0)), _bs((1, D), lambda i: (0, 0))],
        out_specs=_bs((tr, D), lambda i: (i, 0)), out_shape=jax.ShapeDtypeStruct((T, D), BF16),
        compiler_params=_cp("parallel"),
    )(x, g)


def _rms_bwd(name, x, g, dy, dres, want_bf16):
    tr = 256

    def body(x_ref, g_ref, dy_ref, dr_ref, dx_ref, *rest):
        dg_ref = rest[-1]
        xv, dyv = x_ref[...], dy_ref[...]
        r = lax.rsqrt(jnp.mean(xv * xv, axis=-1, keepdims=True) + EPS)
        gdy = g_ref[...] * dyv
        dx = r * gdy - xv * ((r * r * r) * jnp.mean(xv * gdy, axis=-1, keepdims=True)) + dr_ref[...]
        dx_ref[...] = dx
        if want_bf16:
            rest[0][...] = dx.astype(BF16)

        @pl.when(pl.program_id(0) == 0)
        def _():
            dg_ref[...] = jnp.zeros_like(dg_ref)

        dg_ref[...] += jnp.sum(dyv * (xv * r), axis=0, keepdims=True)

    row = _bs((tr, D), lambda i: (i, 0))
    vec = _bs((1, D), lambda i: (0, 0))
    outs = [jax.ShapeDtypeStruct((T, D), F32)] + ([jax.ShapeDtypeStruct((T, D), BF16)] if want_bf16 else [])
    return pl.pallas_call(
        body, name=name, grid=(T // tr,), in_specs=[row, vec, row, row],
        out_specs=[row] * len(outs) + [vec], out_shape=outs + [jax.ShapeDtypeStruct((1, D), F32)],
        compiler_params=_cp("arbitrary"),
    )(x, g, dy, dres)


def _head_ones():
    idx = np.arange(AW) // HD
    return jnp.asarray((idx[:, None] == idx[None, :]).astype(np.float32), dtype=BF16)


def _freq_row():
    half = HD // 2
    inv = ROPE_THETA ** (-(np.arange(half, dtype=np.float64)) / half)
    return jnp.asarray(np.tile(inv, 4)[None, :], dtype=F32)


def _rot_tables(cos128, sin128):
    c = jnp.tile(cos128, (1, 4))
    s = jnp.tile(sin128, (1, 4))
    lane = lax.broadcasted_iota(jnp.int32, (1, AW), 1)
    first = (lane & 32) == 0
    return c, jnp.where(first, -s, s), first


def _swap_halves(y, first):
    return jnp.where(first, pltpu.roll(y, AW - 32, 1), pltpu.roll(y, 32, 1))


def _qk_prep(proj, pos_col, qg, kg):
    tr = 512

    def body(q_ref, k_ref, pos_ref, f_ref, qg_ref, kg_ref, e_ref, qo_ref, ko_ref, cos_ref, sin_ref):
        ang = pos_ref[...].astype(F32) * f_ref[...]
        cos_ref[...] = jnp.cos(ang)
        sin_ref[...] = jnp.sin(ang)
        c, s_signed, first = _rot_tables(cos_ref[...], sin_ref[...])
        e = e_ref[...]

        def norm_rot(xv, g, scale):
            r = lax.rsqrt(_segsum(xv * xv, e) * (1.0 / HD) + EPS)
            y = (xv * r) * g
            return (y * c + _swap_halves(y, first) * s_signed) * scale

        qo_ref[...] = norm_rot(q_ref[...], qg_ref[...], HD ** -0.5)
        ko_ref[...] = norm_rot(k_ref[...], kg_ref[...], 1.0)

    col = lambda j: _bs((tr, AW), lambda i, j=j: (i, j))
    vec = _bs((1, AW), lambda i: (0, 0))
    out = jax.ShapeDtypeStruct((T, AW), F32)
    tab = jax.ShapeDtypeStruct((T, 128), F32)
    tspec = _bs((tr, 128), lambda i: (i, 0))
    return pl.pallas_call(
        body, name="qk_prep", grid=(T // tr,),
        in_specs=[col(0), col(1), _bs((tr, 1), lambda i: (i, 0)), _bs((1, 128), lambda i: (0, 0)), vec, vec,
                  _bs((AW, AW), lambda i: (0, 0))],
        out_specs=[col(0)] * 2 + [tspec] * 2, out_shape=[out, out, tab, tab], compiler_params=_cp("parallel"),
    )(proj, proj, pos_col, _freq_row(), qg, kg, _head_ones())


def _qk_bwd(proj, cos_t, sin_t, qg, kg, dq, dk, dv):
    tr = 256

    def body(q_ref, k_ref, cos_ref, sin_ref, qg_ref, kg_ref, e_ref, dq_ref, dk_ref, dv_ref, o_ref, dqg_ref, dkg_ref):
        i, j = pl.program_id(0), pl.program_id(1)

        @pl.when((i == 0) & (j == 0))
        def _():
            dqg_ref[...] = jnp.zeros_like(dqg_ref)
            dkg_ref[...] = jnp.zeros_like(dkg_ref)

        def norm_rot_bwd(x_ref, g_ref, dg_ref, d_ref, scale):
            c, s_signed, first = _rot_tables(cos_ref[...], sin_ref[...])
            e = e_ref[...]
            dout = d_ref[...] * scale
            dy = dout * c + _swap_halves(dout * s_signed, first)
            xv, g = x_ref[...], g_ref[...]
            r = lax.rsqrt(_segsum(xv * xv, e) * (1.0 / HD) + EPS)
            gdy = g * dy
            dx = r * gdy - xv * ((r * r * r) * (_segsum(xv * gdy, e) * (1.0 / HD)))
            o_ref[...] = dx.astype(BF16)
            dg_ref[...] += jnp.sum(dy * (xv * r), axis=0, keepdims=True)

        @pl.when(j == 0)
        def _():
            norm_rot_bwd(q_ref, qg_ref, dqg_ref, dq_ref, HD ** -0.5)

        @pl.when(j == 1)
        def _():
            norm_rot_bwd(k_ref, kg_ref, dkg_ref, dk_ref, 1.0)

        @pl.when(j == 2)
        def _():
            o_ref[...] = dv_ref[...].astype(BF16)

    col = lambda jj: _bs((tr, AW), lambda i, j, jj=jj: (i, jj))
    vec = _bs((1, AW), lambda i, j: (0, 0))
    piece = _bs((tr, AW), lambda i, j: (i, 0))
    return pl.pallas_call(
        body, name="qk_bwd", grid=(T // tr, 3),
        in_specs=[col(0), col(1), _bs((tr, 128), lambda i, j: (i, 0)), _bs((tr, 128), lambda i, j: (i, 0)), vec, vec,
                  _bs((AW, AW), lambda i, j: (0, 0))] + [piece] * 3,
        out_specs=[_bs((tr, AW), lambda i, j: (i, j)), vec, vec],
        out_shape=[jax.ShapeDtypeStruct((T, INW), BF16), jax.ShapeDtypeStruct((1, AW), F32),
                   jax.ShapeDtypeStruct((1, AW), F32)],
        compiler_params=_cp("arbitrary", "arbitrary"),
    )(proj, proj, cos_t, sin_t, qg, kg, _head_ones(), dq, dk, dv)


RG = 256
QC = 64


def _stacked_band_mask(rows=2 * BLK, q0=0):
    qi = (lax.broadcasted_iota(jnp.int32, (rows, 2 * BLK), 0) + q0) & (BLK - 1)
    kj = lax.broadcasted_iota(jnp.int32, (rows, 2 * BLK), 1)
    rel = qi - kj + BLK
    return (rel >= 0) & (rel <= BLK), lax.broadcasted_iota(jnp.int32, (1, 2 * BLK), 1) >= BLK


def _natural_rows(r0, n_rows, d):
    if d == 1:
        return pl.ds(r0, n_rows)
    ln = T // d
    return pl.ds(r0 // ln + d * (r0 % ln), n_rows, stride=d)


def _regroup_into(dst, src_ref, d, pad, cast=True):
    def step(j, carry):
        r0 = pl.multiple_of(j * RG, RG)
        val = src_ref[_natural_rows(r0, RG, d), :]
        dst[pl.ds(pad + r0, RG), :] = val.astype(dst.dtype) if cast else val
        return carry
    lax.fori_loop(0, T // RG, step, 0)


def _stack_heads(x, h0):
    zero = jnp.zeros_like(x)
    return jnp.concatenate([jnp.where(h0, x, zero), jnp.where(h0, zero, x)], axis=0)


def _attn_fwd(q, k, proj):
    nblk = T // BLK

    def body(q_ref, k_ref, v_ref, a_ref, lse_ref, qs, ks, vs, o0, o1, o2, l0, l1, l2, sb0, sb1):
        band, cur_half = _stacked_band_mask()
        h0 = lax.broadcasted_iota(jnp.int32, (1, 128), 1) < HD
        ks[0:BLK, :] = jnp.zeros((BLK, 128), BF16)
        vs[0:BLK, :] = jnp.zeros((BLK, 128), BF16)
        for d, o_s, l_s in zip(DILATIONS, (o0, o1, o2), (l0, l1, l2)):
            nb = T // d // BLK
            _regroup_into(qs, q_ref, d, 0)
            _regroup_into(ks, k_ref, d, BLK)
            _regroup_into(vs, v_ref, d, BLK)

            def scores(b):
                r0 = pl.multiple_of(b * BLK, BLK)
                return _dot(_stack_heads(qs[pl.ds(r0, BLK), :], h0), ks[pl.ds(r0, 2 * BLK), :], NT)

            def finish(b, s_raw, d=d, nb=nb, o_s=o_s, l_s=l_s):
                r0 = pl.multiple_of(b * BLK, BLK)
                mask = band & (cur_half | ((b & (nb - 1)) > 0))
                s = jnp.where(mask, s_raw, NEG)
                m = jnp.max(s, axis=1, keepdims=True)
                p = jnp.exp(s - m)
                l = jnp.sum(p, axis=1, keepdims=True)
                o = _dot(p.astype(BF16), vs[pl.ds(r0, 2 * BLK), :]) / l
                lse = m + jnp.log(l)
                rows = _natural_rows(r0, BLK, d)
                o_s[rows, :] = jnp.where(h0, o[0:BLK, :], o[BLK:, :])
                l_s[rows, :] = jnp.where(h0, lse[0:BLK, :], lse[BLK:, :])

            sb0[...] = scores(0)

            def step(i, carry):
                b = 2 * i
                sb1[...] = scores(b + 1)
                finish(b, sb0[...])
                sb0[...] = scores(jnp.minimum(b + 2, nblk - 1))
                finish(b + 1, sb1[...])
                return carry

            lax.fori_loop(0, nblk // 2, step, 0)

        def merge(i, carry):
            r = pl.ds(pl.multiple_of(i * RG, RG), RG)
            la, lb, lc = l0[r, :], l1[r, :], l2[r, :]
            m = jnp.maximum(jnp.maximum(la, lb), lc)
            ea, eb, ec = jnp.exp(la - m), jnp.exp(lb - m), jnp.exp(lc - m)
            z = (ea + eb) + ec
            a_ref[r, :] = ((ea * o0[r, :] + eb * o1[r, :]) + ec * o2[r, :]) / z
            lse_ref[r, :] = m + jnp.log(z)
            return carry

        lax.fori_loop(0, T // RG, merge, 0)

    spec = lambda cb: _bs((T, 128), lambda p, cb=cb: (0, cb + p))
    out = jax.ShapeDtypeStruct((T, AW), F32)
    return pl.pallas_call(
        body, name="attn_fwd", grid=(AW // 128,), in_specs=[spec(0), spec(0), spec(8)], out_specs=[spec(0)] * 2,
        out_shape=[out] * 2,
        scratch_shapes=[pltpu.VMEM((T, 128), BF16), pltpu.VMEM((T + BLK, 128), BF16), pltpu.VMEM((T + BLK, 128), BF16)]
        + [pltpu.VMEM((T, 128), F32)] * 6 + [pltpu.VMEM((2 * BLK, 2 * BLK), F32)] * 2,
        compiler_params=_cp("parallel"),
    )(q, k, proj)


def _attn_bwd(q, k, proj, do, lse, delta):
    nblk = T // BLK

    def body(q_ref, k_ref, v_ref, do_ref, l_ref, dl_ref, dq_ref, dk_ref, dv_ref, qs, dos, ks, vs, ls, dls, dks, dvs,
             sa0, sa1, da0, da1):
        band, cur_half = _stacked_band_mask()
        h0 = lax.broadcasted_iota(jnp.int32, (1, 128), 1) < HD
        ks[0:BLK, :] = jnp.zeros((BLK, 128), BF16)
        vs[0:BLK, :] = jnp.zeros((BLK, 128), BF16)
        for d in DILATIONS:
            nb = T // d // BLK
            _regroup_into(qs, q_ref, d, 0)
            _regroup_into(dos, do_ref, d, 0)
            _regroup_into(ks, k_ref, d, BLK)
            _regroup_into(vs, v_ref, d, BLK)
            _regroup_into(ls, l_ref, d, 0, cast=False)
            _regroup_into(dls, dl_ref, d, 0, cast=False)
            dks[...] = jnp.zeros_like(dks)
            dvs[...] = jnp.zeros_like(dvs)

            def scores(b, s_buf, dp_buf):
                r0 = pl.multiple_of(b * BLK, BLK)
                win = pl.ds(r0, 2 * BLK)
                s_buf[...] = _dot(_stack_heads(qs[pl.ds(r0, BLK), :], h0), ks[win, :], NT)
                dp_buf[...] = _dot(_stack_heads(dos[pl.ds(r0, BLK), :], h0), vs[win, :], NT)

            def finish(b, s_buf, dp_buf, d=d, nb=nb):
                r0 = pl.multiple_of(b * BLK, BLK)
                mask = band & (cur_half | ((b & (nb - 1)) > 0))
                win = pl.ds(r0, 2 * BLK)
                lv, dlv = ls[pl.ds(r0, BLK), :], dls[pl.ds(r0, BLK), :]
                lse2 = jnp.concatenate([lv[:, 0:1], lv[:, HD:HD + 1]], axis=0)
                dl2 = jnp.concatenate([dlv[:, 0:1], dlv[:, HD:HD + 1]], axis=0)
                p = jnp.exp(jnp.where(mask, s_buf[...], NEG) - lse2)
                ds = p * (dp_buf[...] - dl2)
                pb, dsb = p.astype(BF16), ds.astype(BF16)
                dq2 = _dot(dsb, ks[win, :])
                dks[win, :] += _dot(dsb, _stack_heads(qs[pl.ds(r0, BLK), :], h0), TN)
                dvs[win, :] += _dot(pb, _stack_heads(dos[pl.ds(r0, BLK), :], h0), TN)
                rows = _natural_rows(r0, BLK, d)
                dq = jnp.where(h0, dq2[0:BLK, :], dq2[BLK:, :])
                dq_ref[rows, :] = dq if d == 1 else dq_ref[rows, :] + dq

            scores(0, sa0, da0)

            def step(i, carry):
                b = 2 * i
                scores(b + 1, sa1, da1)
                finish(b, sa0, da0)
                scores(jnp.minimum(b + 2, nblk - 1), sa0, da0)
                finish(b + 1, sa1, da1)
                return carry

            lax.fori_loop(0, nblk // 2, step, 0)

            def back(j, carry, d=d):
                r0 = pl.multiple_of(j * RG, RG)
                rows = _natural_rows(r0, RG, d)
                src = pl.ds(BLK + r0, RG)
                dk_ref[rows, :] = dks[src, :] if d == 1 else dk_ref[rows, :] + dks[src, :]
                dv_ref[rows, :] = dvs[src, :] if d == 1 else dv_ref[rows, :] + dvs[src, :]
                return carry

            lax.fori_loop(0, T // RG, back, 0)

    spec = lambda cb: _bs((T, 128), lambda p, cb=cb: (0, cb + p))
    ospec = _bs((T, 128), lambda p: (0, p))
    out = jax.ShapeDtypeStruct((T, AW), F32)
    return pl.pallas_call(
        body, name="attn_bwd", grid=(AW // 128,), in_specs=[spec(0), spec(0), spec(8), spec(0), spec(0), spec(0)],
        out_specs=[ospec] * 3, out_shape=[out] * 3,
        scratch_shapes=[pltpu.VMEM((T, 128), BF16), pltpu.VMEM((T, 128), BF16), pltpu.VMEM((T + BLK, 128), BF16),
                        pltpu.VMEM((T + BLK, 128), BF16), pltpu.VMEM((T, 128), F32), pltpu.VMEM((T, 128), F32),
                        pltpu.VMEM((T + BLK, 128), F32), pltpu.VMEM((T + BLK, 128), F32)]
        + [pltpu.VMEM((2 * BLK, 2 * BLK), F32)] * 4,
        compiler_params=_cp("parallel"),
    )(q, k, proj, do, lse, delta)


def _attn_norm(attn, g_attn):
    tr = 512

    def body(a_ref, g_ref, mix_ref):
        attn = a_ref[...]
        r = lax.rsqrt(jnp.mean(attn * attn, axis=-1, keepdims=True) + EPS)
        mix_ref[...] = ((attn * r) * g_ref[...]).astype(BF16)

    row = _bs((tr, AW), lambda i: (i, 0))
    return pl.pallas_call(
        body, name="attn_norm", grid=(T // tr,), in_specs=[row, _bs((1, AW), lambda i: (0, 0))],
        out_specs=row, out_shape=jax.ShapeDtypeStruct((T, D), BF16), compiler_params=_cp("parallel"),
    )(attn, g_attn)


def _attn_out_bwd(attn, dmix, g_attn):
    tr = 256

    def body(a_ref, d_ref, g_ref, e_ref, do_ref, dl_ref, dg_ref):
        av, dyv = a_ref[...], d_ref[...]
        r = lax.rsqrt(jnp.mean(av * av, axis=-1, keepdims=True) + EPS)
        gdy = g_ref[...] * dyv
        da = r * gdy - av * ((r * r * r) * jnp.mean(av * gdy, axis=-1, keepdims=True))
        do_ref[...] = da
        dl_ref[...] = _segsum(da * av, e_ref[...])

        @pl.when(pl.program_id(0) == 0)
        def _():
            dg_ref[...] = jnp.zeros_like(dg_ref)

        dg_ref[...] += jnp.sum(dyv * (av * r), axis=0, keepdims=True)

    row = _bs((tr, AW), lambda i: (i, 0))
    vec = _bs((1, AW), lambda i: (0, 0))
    return pl.pallas_call(
        body, name="attn_out_bwd", grid=(T // tr,), in_specs=[row, row, vec, _bs((AW, AW), lambda i: (0, 0))],
        out_specs=[row, row, vec],
        out_shape=[jax.ShapeDtypeStruct((T, AW), F32), jax.ShapeDtypeStruct((T, AW), F32),
                   jax.ShapeDtypeStruct((1, AW), F32)],
        compiler_params=_cp("arbitrary"),
    )(attn, dmix, g_attn, _head_ones())


TRR = 256


def _scan_fwd(a, u):
    n = a.shape[0]
    row = lax.broadcasted_iota(jnp.int32, (n, 1), 0)
    s = 1
    while s < n:
        keep = row >= s
        u = jnp.where(keep, a * pltpu.roll(u, s, 0) + u, u)
        a = jnp.where(keep, a * pltpu.roll(a, s, 0), a)
        s *= 2
    return a, u


def _scan_bwd(c, w):
    n = c.shape[0]
    row = lax.broadcasted_iota(jnp.int32, (n, 1), 0)
    s = 1
    while s < n:
        keep = row < n - s
        w = jnp.where(keep, c * pltpu.roll(w, n - s, 0) + w, w)
        c = jnp.where(keep, c * pltpu.roll(c, n - s, 0), c)
        s *= 2
    return w


def _gates(xc, wrg, wig, brg, big, sp):
    xcb = xc.astype(BF16)
    r = jax.nn.sigmoid(_dot(xcb, wrg) + brg)
    ig = jax.nn.sigmoid(_dot(xcb, wig) + big)
    la = (-LRU_C * r) * sp
    a = jnp.exp(la)
    mult = jnp.sqrt(-jnp.tanh(la) * (a * a + 1.0))
    return r, ig, a, mult


def _conv4(ext_ref, xr, cw_ref, cb_ref, n):
    y = cb_ref[...] + ext_ref[pl.ds(5, n), :] * cw_ref[0:1, :]
    y = y + ext_ref[pl.ds(6, n), :] * cw_ref[1:2, :]
    y = y + ext_ref[pl.ds(7, n), :] * cw_ref[2:3, :]
    return y + xr * cw_ref[3:4, :]


def _rec_fwd(proj, mix, cw, cb, wrg, wig, brg, big, lam, g_rec):
    n = TRR

    def body(xr_ref, gr_ref, cw_ref, cb_ref, wrg_ref, wig_ref, brg_ref, big_ref, lam_ref, g_ref, mix_in,
             mix_ref, h_ref, ext, hcar):
        del mix_in

        @pl.when(pl.program_id(0) == 0)
        def _():
            ext[0:8, :] = jnp.zeros((8, RW), F32)
            hcar[...] = jnp.zeros_like(hcar)

        xr = xr_ref[...]
        ext[8:, :] = xr
        xc = _conv4(ext, xr, cw_ref, cb_ref, n)
        ext[0:8, :] = xr[n - 8:, :]
        sp = _softplus(-lam_ref[...])
        _, ig, a, mult = _gates(xc, wrg_ref[...], wig_ref[...], brg_ref[...], big_ref[...], sp)
        a_s, u_s = _scan_fwd(a, mult * (ig * xc))
        h = u_s + a_s * hcar[7:8, :]
        h_ref[...] = h
        hcar[...] = h[n - 8:, :]
        pre = h * _gelu(gr_ref[...])
        r = lax.rsqrt(jnp.mean(pre * pre, axis=-1, keepdims=True) + EPS)
        mix_ref[...] = ((pre * r) * g_ref[...]).astype(BF16)

    vec = _bs((1, RW), lambda i: (0, 0))
    mat = _bs((RW, RW), lambda i: (0, 0))
    return pl.pallas_call(
        body, name="rec_fwd", grid=(T // n,),
        in_specs=[_bs((n, RW), lambda i: (i, 3)), _bs((n, RW), lambda i: (i, 4)), _bs((8, RW), lambda i: (0, 0)), vec,
                  mat, mat, vec, vec, vec, vec, pl.BlockSpec(memory_space=pl.ANY)],
        out_specs=[_bs((n, RW), lambda i: (i, 1)), _bs((n, RW), lambda i: (i, 0))],
        out_shape=[jax.ShapeDtypeStruct((T, D), BF16), jax.ShapeDtypeStruct((T, RW), F32)],
        scratch_shapes=[pltpu.VMEM((n + 8, RW), F32), pltpu.VMEM((8, RW), F32)],
        input_output_aliases={10: 0}, compiler_params=_cp("arbitrary"),
    )(proj, proj, cw, cb, wrg, wig, brg, big, lam, g_rec, mix)


def _rec_bwd(proj, h, dmix, dproj, cw, cb, wrg, wig, brg, big, lam, g_rec):
    n = TRR
    nt = T // n
    hb = n // 8

    def body(xr_ref, xh_ref, gr_ref, h_ref, hh_ref, dm_ref, cw_ref, cb_ref, wrg_ref, wig_ref, brg_ref, big_ref,
             lam_ref, g_ref, dp_in, dp_ref, xc_ref, dr_ref, di_ref, dcw_ref, dcb_ref, dbr_ref, dbi_ref, dsp_ref,
             dg_ref, ext, exth, extd, adh, dgr_s):
        del dp_in
        i, j = pl.program_id(0), pl.program_id(1)
        first_tile = i == nt - 1
        last_tile = i == 0

        @pl.when(j == 0)
        def _():
            @pl.when(last_tile)
            def _():
                for ref in (dcw_ref, dcb_ref, dbr_ref, dbi_ref, dsp_ref, dg_ref):
                    ref[...] = jnp.zeros_like(ref)
                extd[n:, :] = jnp.zeros((8, RW), F32)
                adh[...] = jnp.zeros_like(adh)

            row = lax.broadcasted_iota(jnp.int32, (n, 1), 0)
            xr = xr_ref[...]
            ext[0:8, :] = jnp.where(first_tile, 0.0, xh_ref[...])
            ext[8:, :] = xr
            xc = _conv4(ext, xr, cw_ref, cb_ref, n)
            sp = _softplus(-lam_ref[...])
            wrg, wig = wrg_ref[...], wig_ref[...]
            r, ig, a, mult = _gates(xc, wrg, wig, brg_ref[...], big_ref[...], sp)

            hv = h_ref[...]
            gl, dgl = _gelu_and_grad(gr_ref[...])
            pre = hv * gl
            dyv = dm_ref[...]
            rr = lax.rsqrt(jnp.mean(pre * pre, axis=-1, keepdims=True) + EPS)
            gdy = g_ref[...] * dyv
            dpre = rr * gdy - pre * ((rr * rr * rr) * jnp.mean(pre * gdy, axis=-1, keepdims=True))
            dg_ref[...] += jnp.sum(dyv * (pre * rr), axis=0, keepdims=True)
            dgr_s[...] = dpre * hv * dgl

            is_last_row = row == n - 1
            w = dpre * gl + jnp.where(is_last_row, adh[0:1, :], 0.0)
            c = jnp.where(is_last_row, 0.0, pltpu.roll(a, n - 1, 0))
            dh = _scan_bwd(c, w)
            adh[...] = (a * dh)[0:8, :]

            exth[0:8, :] = jnp.where(first_tile, 0.0, hh_ref[...])
            exth[8:, :] = hv
            da = dh * exth[pl.ds(7, n), :]
            ixc = ig * xc
            dmult = dh * ixc
            dla = da * a - dmult * ((a * a) / mult)
            dsp_ref[...] += jnp.sum(dla * (-LRU_C * r), axis=0, keepdims=True)
            dpr = (dla * (-LRU_C * sp)) * (r * (1.0 - r))
            dpi = (dh * (mult * xc)) * (ig * (1.0 - ig))
            dprb, dpib = dpr.astype(BF16), dpi.astype(BF16)
            dxc = dh * (mult * ig) + _dot(dprb, wrg, NT) + _dot(dpib, wig, NT)
            dbr_ref[...] += jnp.sum(dpr, axis=0, keepdims=True)
            dbi_ref[...] += jnp.sum(dpi, axis=0, keepdims=True)
            xc_ref[...] = xc.astype(BF16)
            dr_ref[...] = dprb
            di_ref[...] = dpib

            extd[0:n, :] = dxc
            dxr = dxc * cw_ref[3:4, :] + extd[pl.ds(1, n), :] * cw_ref[2:3, :]
            dxr = dxr + extd[pl.ds(2, n), :] * cw_ref[1:2, :] + extd[pl.ds(3, n), :] * cw_ref[0:1, :]
            extd[n:, :] = dxc[0:8, :]
            dcb_ref[...] += jnp.sum(dxc, axis=0, keepdims=True)
            for kk in range(4):
                dcw_ref[kk:kk + 1, :] += jnp.sum(dxc * ext[pl.ds(5 + kk, n), :], axis=0, keepdims=True)

            @pl.when(first_tile)
            def _():
                dsp_ref[...] = dsp_ref[...] * (-jax.nn.sigmoid(-lam_ref[...]))

            dp_ref[...] = dxr.astype(BF16)

        @pl.when(j == 1)
        def _():
            dp_ref[...] = dgr_s[...].astype(BF16)

    vec = _bs((1, RW), lambda i, j: (0, 0))
    mat = _bs((RW, RW), lambda i, j: (0, 0))
    tile = lambda cblk: _bs((n, RW), lambda i, j, cblk=cblk: (nt - 1 - i, cblk))
    halo = lambda cblk: _bs((8, RW), lambda i, j, cblk=cblk: (jnp.maximum((nt - 1 - i) * hb - 1, 0), cblk))
    bt = jax.ShapeDtypeStruct((T, RW), BF16)
    v = jax.ShapeDtypeStruct((1, RW), F32)
    return pl.pallas_call(
        body, name="rec_bwd", grid=(nt, 2),
        in_specs=[tile(3), halo(3), tile(4), tile(0), halo(0), tile(1), _bs((8, RW), lambda i, j: (0, 0)), vec,
                  mat, mat, vec, vec, vec, vec, pl.BlockSpec(memory_space=pl.ANY)],
        out_specs=[_bs((n, RW), lambda i, j: (nt - 1 - i, 3 + j)), tile(0), tile(0), tile(0),
                   _bs((8, RW), lambda i, j: (0, 0)), vec, vec, vec, vec, vec],
        out_shape=[jax.ShapeDtypeStruct((T, INW), BF16), bt, bt, bt, jax.ShapeDtypeStruct((8, RW), F32), v, v, v, v, v],
        scratch_shapes=[pltpu.VMEM((n + 8, RW), F32), pltpu.VMEM((n + 8, RW), F32), pltpu.VMEM((n + 8, RW), F32),
                        pltpu.VMEM((8, RW), F32), pltpu.VMEM((n, RW), F32)],
        input_output_aliases={14: 0}, compiler_params=_cp("arbitrary", "arbitrary"),
    )(proj, proj, proj, h, h, dmix, cw, cb, wrg, wig, brg, big, lam, g_rec, dproj)


FC = 1536
TRF = 256


LC = 128


def _taps(x_ref, edge, cols, r):
    if r == 0:
        return edge[pl.ds(6, 8), cols], edge[pl.ds(7, 8), cols], edge[pl.ds(8, 8), cols]
    return x_ref[pl.ds(r - 2, 8), cols], x_ref[pl.ds(r - 1, 8), cols], x_ref[pl.ds(r, 8), cols]


def _ffn_act(up_pre, cw, cb):
    n = TRF
    hb = n // 8

    def body(g_ref, gh_ref, u_ref, uh_ref, wg_ref, wu_ref, bg_ref, bu_ref, o_ref, eg, eu):
        first = pl.program_id(1) == 0
        eg[0:8, :] = jnp.where(first, 0.0, gh_ref[...])
        eg[8:, :] = g_ref[0:8, :]
        eu[0:8, :] = jnp.where(first, 0.0, uh_ref[...])
        eu[8:, :] = u_ref[0:8, :]

        def column(ci, carry):
            cols = pl.ds(pl.multiple_of(ci * LC, LC), LC)
            wg = [wg_ref[kk:kk + 1, cols] for kk in range(3)]
            wu = [wu_ref[kk:kk + 1, cols] for kk in range(3)]
            bg, bu = bg_ref[:, cols], bu_ref[:, cols]
            for r in range(0, n, 16):
                res = []
                for rr in (r, r + 8):
                    g0, g1, g2 = _taps(g_ref, eg, cols, rr)
                    u0, u1, u2 = _taps(u_ref, eu, cols, rr)
                    ug = ((bg + g0 * wg[0]) + g1 * wg[1]) + g2 * wg[2]
                    uu = ((bu + u0 * wu[0]) + u1 * wu[1]) + u2 * wu[2]
                    res.append(_gelu(ug) * uu)
                o_ref[pl.ds(r, 16), cols] = jnp.concatenate(res, axis=0).astype(BF16)
            return carry

        lax.fori_loop(0, FC // LC, column, 0)

    main = lambda o: _bs((n, FC), lambda j, i, o=o: (i, 2 * j + o))
    halo = lambda o: _bs((8, FC), lambda j, i, o=o: (jnp.maximum(i * hb - 1, 0), 2 * j + o))
    wsp = lambda o: _bs((None, 8, FC), lambda j, i, o=o: (2 * j + o, 0, 0))
    bsp = lambda o: _bs((1, FC), lambda j, i, o=o: (0, 2 * j + o))
    return pl.pallas_call(
        body, name="ffn_act", grid=(2, T // n),
        in_specs=[main(0), halo(0), main(1), halo(1), wsp(0), wsp(1), bsp(0), bsp(1)],
        out_specs=_bs((n, FC), lambda j, i: (i, j)), out_shape=jax.ShapeDtypeStruct((T, DFF), BF16),
        scratch_shapes=[pltpu.VMEM((16, FC), F32)] * 2, compiler_params=_cp("parallel", "parallel"),
    )(up_pre, up_pre, up_pre, up_pre, cw, cw, cb, cb)


def _ffn_bwd(up_pre, dact, cw, cb):
    n = TRF
    hb = n // 8
    nt = T // n
    m = n + 8

    def body(g_ref, gp_ref, gn_ref, u_ref, up_ref, un_ref, d_ref, dn_ref, wg_ref, wu_ref, bg_ref, bu_ref,
             o_ref, dw_ref, db_ref, eg0, eg1, eu0, eu1, dug_s, duu_s):
        i = pl.program_id(1)
        first, last = i == 0, i == nt - 1

        @pl.when(first)
        def _():
            dw_ref[...] = jnp.zeros_like(dw_ref)
            db_ref[...] = jnp.zeros_like(db_ref)

        eg0[0:8, :] = jnp.where(first, 0.0, gp_ref[...])
        eg0[8:, :] = g_ref[0:8, :]
        eg1[0:8, :] = g_ref[n - 8:, :]
        eg1[8:, :] = gn_ref[...]
        eu0[0:8, :] = jnp.where(first, 0.0, up_ref[...])
        eu0[8:, :] = u_ref[0:8, :]
        eu1[0:8, :] = u_ref[n - 8:, :]
        eu1[8:, :] = un_ref[...]

        def column(ci, carry):
            cols = pl.ds(pl.multiple_of(ci * LC, LC), LC)
            ucols = pl.ds(pl.multiple_of(FC + ci * LC, LC), LC)
            wg = [wg_ref[kk:kk + 1, cols] for kk in range(3)]
            wu = [wu_ref[kk:kk + 1, cols] for kk in range(3)]
            bg, bu = bg_ref[:, cols], bu_ref[:, cols]
            zero = jnp.zeros((8, LC), F32)
            acc = [zero] * 8
            for r in range(0, n + 8, 8):
                if r == n:
                    gt = (eg1[pl.ds(6, 8), cols], eg1[pl.ds(7, 8), cols], eg1[pl.ds(8, 8), cols])
                    ut = (eu1[pl.ds(6, 8), cols], eu1[pl.ds(7, 8), cols], eu1[pl.ds(8, 8), cols])
                    dv = jnp.where(last, 0.0, dn_ref[:, cols])
                else:
                    gt, ut = _taps(g_ref, eg0, cols, r), _taps(u_ref, eu0, cols, r)
                    dv = d_ref[pl.ds(r, 8), cols]
                gl, dgl = _gelu_and_grad(((bg + gt[0] * wg[0]) + gt[1] * wg[1]) + gt[2] * wg[2])
                uu = ((bu + ut[0] * wu[0]) + ut[1] * wu[1]) + ut[2] * wu[2]
                dug, duu = dv * uu * dgl, dv * gl
                dug_s[pl.ds(r, 8), :] = dug
                duu_s[pl.ds(r, 8), :] = duu
                if r < n:
                    acc = [acc[0] + dug * gt[0], acc[1] + dug * gt[1], acc[2] + dug * gt[2],
                           acc[3] + duu * ut[0], acc[4] + duu * ut[1], acc[5] + duu * ut[2], acc[6] + dug, acc[7] + duu]
            for r in range(0, n, 16):
                og, ou = [], []
                for rr in (r, r + 8):
                    og.append((dug_s[pl.ds(rr, 8), :] * wg[2] + dug_s[pl.ds(rr + 1, 8), :] * wg[1])
                              + dug_s[pl.ds(rr + 2, 8), :] * wg[0])
                    ou.append((duu_s[pl.ds(rr, 8), :] * wu[2] + duu_s[pl.ds(rr + 1, 8), :] * wu[1])
                              + duu_s[pl.ds(rr + 2, 8), :] * wu[0])
                o_ref[pl.ds(r, 16), cols] = jnp.concatenate(og, axis=0).astype(BF16)
                o_ref[pl.ds(r, 16), ucols] = jnp.concatenate(ou, axis=0).astype(BF16)
            for kk in range(3):
                dw_ref[kk:kk + 1, cols] += jnp.sum(acc[kk], axis=0, keepdims=True)
                dw_ref[kk:kk + 1, ucols] += jnp.sum(acc[3 + kk], axis=0, keepdims=True)
            db_ref[:, cols] += jnp.sum(acc[6], axis=0, keepdims=True)
            db_ref[:, ucols] += jnp.sum(acc[7], axis=0, keepdims=True)
            return carry

        lax.fori_loop(0, FC // LC, column, 0)

    main = lambda o: _bs((n, FC), lambda j, i, o=o: (i, 2 * j + o))
    prev = lambda o: _bs((8, FC), lambda j, i, o=o: (jnp.maximum(i * hb - 1, 0), 2 * j + o))
    nxt = lambda o: _bs((8, FC), lambda j, i, o=o: (jnp.minimum((i + 1) * hb, T // 8 - 1), 2 * j + o))
    wsp = lambda o: _bs((None, 8, FC), lambda j, i, o=o: (2 * j + o, 0, 0))
    bsp = lambda o: _bs((1, FC), lambda j, i, o=o: (0, 2 * j + o))
    return pl.pallas_call(
        body, name="ffn_bwd", grid=(2, nt),
        in_specs=[main(0), prev(0), nxt(0), main(1), prev(1), nxt(1), _bs((n, FC), lambda j, i: (i, j)),
                  _bs((8, FC), lambda j, i: (jnp.minimum((i + 1) * hb, T // 8 - 1), j)), wsp(0), wsp(1), bsp(0), bsp(1)],
        out_specs=[_bs((n, 2 * FC), lambda j, i: (i, j)), _bs((8, 2 * FC), lambda j, i: (0, j)),
                   _bs((1, 2 * FC), lambda j, i: (0, j))],
        out_shape=[jax.ShapeDtypeStruct((T, 2 * DFF), BF16), jax.ShapeDtypeStruct((8, 2 * DFF), F32),
                   jax.ShapeDtypeStruct((1, 2 * DFF), F32)],
        scratch_shapes=[pltpu.VMEM((16, FC), F32)] * 4 + [pltpu.VMEM((m, LC), F32)] * 2,
        compiler_params=_cp("parallel", "arbitrary"),
    )(up_pre, up_pre, up_pre, up_pre, up_pre, up_pre, dact, dact, cw, cw, cb, cb)


def _down_loss(act, w_down, x1, target):
    tm, tn = 512, 512

    def body(a_ref, b_ref, r_ref, t_ref, dy_ref, dyb_ref, l_ref):
        @pl.when((pl.program_id(0) == 0) & (pl.program_id(1) == 0))
        def _():
            l_ref[...] = jnp.zeros_like(l_ref)

        err = (r_ref[...] + _dot(a_ref[...], b_ref[...])) - t_ref[...]
        dy = err * (1.0 / D)
        dy_ref[...] = dy
        dyb_ref[...] = dy.astype(BF16)
        l_ref[...] += jnp.sum(0.5 * (err * err) * (1.0 / D))

    o_spec = _bs((tm, tn), lambda j, i: (i, j))
    return pl.pallas_call(
        body, name="down_loss", grid=(D // tn, T // tm),
        in_specs=[_bs((tm, DFF), lambda j, i: (i, 0)), _bs((DFF, tn), lambda j, i: (0, j)), o_spec, o_spec],
        out_specs=[o_spec, o_spec, _bs((8, 128), lambda j, i: (0, 0))],
        out_shape=[jax.ShapeDtypeStruct((T, D), F32), jax.ShapeDtypeStruct((T, D), BF16),
                   jax.ShapeDtypeStruct((8, 128), F32)],
        compiler_params=_cp("arbitrary", "arbitrary"),
    )(act, w_down, x1, target)


def _block_diag(w):
    eye = jnp.eye(8, dtype=w.dtype)
    return (w[:, :, None, :] * eye[:, None, :, None]).reshape(RW, RW).astype(BF16)


def _diag_blocks(m):
    eye = jnp.eye(8, dtype=m.dtype)
    return (m.reshape(8, HD, 8, HD) * eye[:, None, :, None]).sum(axis=2)


def _local_step(x, pos_col, target, p, exch):
    qg, kg = jnp.tile(p["q_norm_g"], (1, 8)), jnp.tile(p["k_norm_g"], (1, 8))
    wrg, wig = _block_diag(p["w_rg"]), _block_diag(p["w_ig"])
    brg, big = p["b_rg"].reshape(1, RW), p["b_ig"].reshape(1, RW)

    h1 = _rms_fwd("rms1", x, p["g_mix"])
    proj = _mm("mm_in", h1, p["w_in"], "nn", 512, 640, stack=NCHIP, after=exch.start_rest(), a_full=True)
    q, k, cos_t, sin_t = _qk_prep(proj, pos_col, qg, kg)
    attn, lse = _attn_fwd(q, k, proj)
    mix = _attn_norm(attn, p["g_attn_out"])
    mix, hseq = _rec_fwd(proj, mix, p["rec_conv_w"], p["rec_conv_b"], wrg, wig, brg, big, p["lru_lambda"], p["g_rec_out"])
    rest = exch.wait_rest(mix)
    x1 = _mm("mm_out", mix, rest["w_out"], "nn", 512, 512, res=x, a_full=True)
    h2 = _rms_fwd("rms2", x1, p["g_ffn"])
    up_pre = _mm("mm_up", h2, rest["w_up"], "nn", 512, 768, stack=NCHIP, a_full=True)
    act = _ffn_act(up_pre, p["ffn_conv_w"], p["ffn_conv_b"])
    dy, dyb, loss_blk = _down_loss(act, rest["w_down"], x1, target)

    g = {}
    tok = exch.reduce_start("w_down", *_mm("wg_down", act, dyb, "tn", 512, 512, twin_bf16=True, a_full=True))
    dact = _mm("dg_down", dyb, rest["w_down"], "nt", 512, 512, after=tok, a_full=True)
    dup, g["ffn_conv_w"], g["ffn_conv_b"] = _ffn_bwd(up_pre, dact, p["ffn_conv_w"], p["ffn_conv_b"])
    tok = exch.reduce_start("w_up", *_mm("wg_up", h2, dup, "tn", 512, 768, stack=NCHIP, twin_bf16=True, a_full=True))
    dh2 = _mm("dg_up", dup, rest["w_up"], "nt", 512, D, stack=NCHIP, after=tok, b_full=True)
    dx1, dx1b, g["g_ffn"] = _rms_bwd("rms2_bwd", x1, p["g_ffn"], dh2, dy, True)
    tok = exch.reduce_start("w_out", *_mm("wg_out", mix, dx1b, "tn", 512, 512, twin_bf16=True, a_full=True))
    dmix = _mm("dg_out", dx1b, rest["w_out"], "nt", 512, 512, after=tok, a_full=True)
    do, delta, g["g_attn_out"] = _attn_out_bwd(attn, dmix, p["g_attn_out"])
    dq, dk, dv = _attn_bwd(q, k, proj, do, lse, delta)
    dproj, dqg, dkg = _qk_bwd(proj, cos_t, sin_t, qg, kg, dq, dk, dv)
    (dproj, xcb, dprb, dpib, g["rec_conv_w"], g["rec_conv_b"], dbr, dbi, dsp, g["g_rec_out"]) = _rec_bwd(
        proj, hseq, dmix, dproj, p["rec_conv_w"], p["rec_conv_b"], wrg, wig, brg, big, p["lru_lambda"], p["g_rec_out"])
    g["w_rg"] = _diag_blocks(_mm("wg_rg", xcb, dprb, "tn", 512, 512)).reshape(RW, HD)
    g["w_ig"] = _diag_blocks(_mm("wg_ig", xcb, dpib, "tn", 512, 512)).reshape(RW, HD)
    g["b_rg"], g["b_ig"] = dbr.reshape(8, HD), dbi.reshape(8, HD)
    g["lru_lambda"] = dsp
    g["q_norm_g"] = dqg.reshape(8, HD).sum(axis=0, keepdims=True)
    g["k_norm_g"] = dkg.reshape(8, HD).sum(axis=0, keepdims=True)
    tok = exch.reduce_start("w_in", *_mm("wg_in", h1, dproj, "tn", 512, 640, stack=NCHIP, twin_bf16=True, a_full=True))
    dh1 = _mm("dg_in", dproj, p["w_in"], "nt", 512, 512, stack=NCHIP, after=tok, a_full=True)
    grad_x, g["g_mix"] = _rms_bwd("rms1_bwd", x, p["g_mix"], dh1, dx1, False)
    return loss_blk, grad_x, g


ANY = pl.BlockSpec(memory_space=pl.ANY)


def _mesh_pos():
    return lax.axis_index("x"), lax.axis_index("y"), lax.axis_index("c")


def _slot(px, py, perm):
    return 2 * py + px if perm else 2 * px + py


def _other_chips(x, y):
    return [(1 - x, y), (x, 1 - y), (1 - x, 1 - y)]


def _rcopy(src, dst, send, recv, k, to, kr=None):
    return pltpu.make_async_remote_copy(src_ref=src, dst_ref=dst, send_sem=send.at[k],
                                        recv_sem=recv.at[k if kr is None else kr], device_id=to, device_id_type=MESH)


def _cast_bf16(name, w):
    r, c = w.shape
    tr = 128
    def body(w_ref, o_ref):
        o_ref[...] = w_ref[...].astype(BF16)
    return pl.pallas_call(
        body, name=name, grid=(r // tr,), in_specs=[_bs((tr, c), lambda i: (i, 0))],
        out_specs=_bs((tr, c), lambda i: (i, 0)), out_shape=jax.ShapeDtypeStruct((r, c), BF16),
        compiler_params=_cp("parallel"),
    )(w)


def _gather_weights(big, small, slot):
    nb, ns = len(big), len(small)
    perms = [p for _, p in big] + [p for _, p in small]

    def body(*refs):
        ins, outs = refs[:nb + ns], refs[2 * (nb + ns):3 * (nb + ns)]
        send, recv = refs[3 * (nb + ns):]
        x, y, c = _mesh_pos()
        me, sib = (x, y, c), (x, y, 1 - c)
        chips = _other_chips(x, y)
        first = []
        for a in range(nb):
            for j, (px, py) in enumerate(chips):
                first.append(_rcopy(ins[a].at[c], outs[a].at[_slot(x, y, perms[a]), c], send, recv, 3 * a + j, (px, py, c)))
        for t in range(ns):
            a = nb + t
            for j, (px, py) in enumerate(chips):
                first.append(_rcopy(ins[a], outs[a].at[_slot(x, y, perms[a])], send, recv, 6 * nb + 3 * t + j, (px, py, c)))
        for cp in first:
            cp.start()
        passed = []
        for a in range(nb):
            for j, (px, py) in enumerate(chips):
                got = outs[a].at[_slot(px, py, perms[a]), c]
                _rcopy(got, got, send, recv, 3 * a + j, me).wait_recv()
                fwd = _rcopy(got, got, send, recv, 3 * nb + 3 * a + j, sib)
                fwd.start()
                passed.append(fwd)
        for a in range(nb):
            for j, (px, py) in enumerate(chips):
                got = outs[a].at[_slot(px, py, perms[a]), 1 - c]
                _rcopy(got, got, send, recv, 3 * nb + 3 * a + j, me).wait_recv()
        for t in range(ns):
            a = nb + t
            for j, (px, py) in enumerate(chips):
                got = outs[a].at[_slot(px, py, perms[a])]
                _rcopy(got, got, send, recv, 6 * nb + 3 * t + j, me).wait_recv()
        for cp in first + passed:
            cp.wait_send()

    arrs = [a for a, _ in big] + [a for a, _ in small]
    lands = [lax.dynamic_update_slice(lax.empty((NCHIP,) + a.shape, a.dtype), a[None], (slot[p],) + (0,) * a.ndim)
             for a, p in zip(arrs, perms)]
    nsem = 6 * nb + 3 * ns
    return pl.pallas_call(
        body, name="gather_weights", in_specs=[ANY] * (2 * (nb + ns)), out_specs=[ANY] * (nb + ns),
        out_shape=[jax.ShapeDtypeStruct(a.shape, a.dtype) for a in lands],
        input_output_aliases={nb + ns + i: i for i in range(nb + ns)},
        scratch_shapes=[pltpu.SemaphoreType.DMA((nsem,)), pltpu.SemaphoreType.DMA((nsem,))],
    )(*arrs, *lands)


HBM = pl.BlockSpec(memory_space=pltpu.HBM)
SEM = pl.BlockSpec(memory_space=pltpu.SEMAPHORE)
EFFECT = pltpu.SideEffectType.DATAFLOW_SIDE_EFFECTING


def _split_start(name, srcs, lands, plan, nsem):
    ns, nl = len(srcs), len(lands)

    def body(*refs):
        send, recv = refs[ns + nl], refs[ns + nl + 1]
        sends, _ = plan(refs[:ns], refs[ns:ns + nl], send, recv)
        for cp in sends:
            cp.start()
        refs[-1][...] = jnp.zeros((8, 128), F32)

    arrs = list(srcs) + list(lands)
    out = pl.pallas_call(
        body, name=name, in_specs=[HBM] * (ns + nl),
        out_specs=[SEM, SEM] + [HBM] * (ns + nl) + [pl.BlockSpec(memory_space=pltpu.VMEM)],
        out_shape=[pltpu.SemaphoreType.DMA((nsem,)), pltpu.SemaphoreType.DMA((nsem,))]
        + [pltpu.HBM(a.shape, a.dtype) for a in arrs] + [jax.ShapeDtypeStruct((8, 128), F32)],
        input_output_aliases={i: 2 + i for i in range(ns + nl)},
        compiler_params=pltpu.CompilerParams(has_side_effects=EFFECT),
    )(*[pltpu.with_memory_space_constraint(a, pltpu.HBM) for a in arrs])
    return out[0], out[1], out[2:2 + ns], out[2 + ns:2 + ns + nl], out[-1]


def _split_wait(name, send, recv, srcs, lands, plan, after):
    ns, nl = len(srcs), len(lands)

    def body(*refs):
        sends, recvs = plan(refs[:ns], refs[ns:ns + nl], refs[ns + nl], refs[ns + nl + 1])
        for cp in sends:
            cp.wait_send()
        for cp in recvs:
            cp.wait_recv()

    arrs = list(srcs) + list(lands)
    out = pl.pallas_call(
        body, name=name, in_specs=[HBM] * (ns + nl) + [SEM, SEM, ANY], out_specs=[HBM] * (ns + nl),
        out_shape=[pltpu.HBM(a.shape, a.dtype) for a in arrs],
        input_output_aliases={i: i for i in range(ns + nl)},
        compiler_params=pltpu.CompilerParams(has_side_effects=EFFECT),
    )(*arrs, send, recv, after)
    return out[ns:]


def _gather_plan(perms):
    def plan(srcs, lands, send, recv):
        x, y, c = _mesh_pos()
        sends, recvs = [], []
        for a, perm in enumerate(perms):
            for j, (px, py) in enumerate(_other_chips(x, y)):
                for cc in (0, 1):
                    k = 6 * a + 2 * j + cc
                    sends.append(_rcopy(srcs[a].at[c], lands[a].at[_slot(x, y, perm), c], send, recv, k, (px, py, cc),
                                        kr=6 * a + 2 * j + c))
                    got = lands[a].at[_slot(px, py, perm), cc]
                    recvs.append(_rcopy(got, got, send, recv, k, (x, y, c)))
        return sends, recvs
    return plan


def _reduce_plan(perm):
    def plan(srcs, lands, send, recv):
        x, y, c = _mesh_pos()
        src, land = srcs[0], lands[0]
        sends = []
        for j, (px, py) in enumerate(_other_chips(x, y)):
            for hf in (0, 1):
                sends.append(_rcopy(src.at[_slot(px, py, perm), hf], land.at[2 * j + c], send, recv, 2 * j + hf,
                                    (px, py, hf), kr=2 * j + c))
        sends.append(_rcopy(src.at[_slot(x, y, perm), 1 - c], land.at[6], send, recv, 6, (x, y, 1 - c)))
        recvs = [_rcopy(land.at[i], land.at[i], send, recv, i, (x, y, c)) for i in range(7)]
        return sends, recvs
    return plan


def _sibling_share(rs):
    na = len(rs)

    def body(*refs):
        ins, outs, (send, recv) = refs[:na], refs[na:2 * na], refs[2 * na:]
        x, y, c = _mesh_pos()
        cps = [_rcopy(ins[a], outs[a], send, recv, a, (x, y, 1 - c)) for a in range(na)]
        for cp in cps:
            cp.start()
        for cp in cps:
            cp.wait()

    return pl.pallas_call(
        body, name="rs_share", in_specs=[ANY] * na, out_specs=[ANY] * na,
        out_shape=[jax.ShapeDtypeStruct(r.shape, F32) for r in rs],
        scratch_shapes=[pltpu.SemaphoreType.DMA((na,)), pltpu.SemaphoreType.DMA((na,))],
    )(*rs)


def _add_pieces(name, g, got, where):
    _, _, r2, cc = g.shape
    tr = 128

    def body(w_ref, g_ref, r_ref, o_ref):
        del w_ref
        acc = g_ref[...]
        for i in range(7):
            acc = acc + r_ref[i].astype(F32)
        o_ref[...] = acc

    return pl.pallas_call(
        body, name=name,
        grid_spec=pltpu.PrefetchScalarGridSpec(
            num_scalar_prefetch=1, grid=(r2 // tr,),
            in_specs=[_bs((None, None, tr, cc), lambda i, w_ref: (w_ref[0], w_ref[1], i, 0)),
                      _bs((7, tr, cc), lambda i, w_ref: (0, i, 0))],
            out_specs=_bs((tr, cc), lambda i, w_ref: (i, 0))),
        out_shape=jax.ShapeDtypeStruct((r2, cc), F32), compiler_params=_cp("parallel"),
    )(where, g, got)


def _adam_math(w, g, m, v):
    m = ADAM_B1 * m + (1.0 - ADAM_B1) * g
    v = ADAM_B2 * v + (1.0 - ADAM_B2) * (g * g)
    m_hat = m / (1.0 - ADAM_B1 ** ADAM_STEP)
    v_hat = v / (1.0 - ADAM_B2 ** ADAM_STEP)
    return -ADAM_LR * (m_hat / (jnp.sqrt(v_hat) + ADAM_EPS) + ADAM_WD * w), m, v


def _adam_big(name, w, g_mine, g_sib, m, v, c_arr):
    r, cols = w.shape
    tr = 128
    per = r // 2 // tr

    def body(c_ref, w_ref, a_ref, b_ref, m_ref, v_ref, g_ref, d_ref, m2_ref, v2_ref):
        g = jnp.where(pl.program_id(0) == c_ref[0], a_ref[...], b_ref[...])
        g_ref[...] = g
        d_ref[...], m2_ref[...], v2_ref[...] = _adam_math(w_ref[...], g, m_ref[...], v_ref[...])

    spec = _bs((tr, cols), lambda h, i, c_ref: (h * per + i, 0))
    half = _bs((tr, cols), lambda h, i, c_ref: (i, 0))
    out = jax.ShapeDtypeStruct((r, cols), F32)
    return pl.pallas_call(
        body, name=name,
        grid_spec=pltpu.PrefetchScalarGridSpec(
            num_scalar_prefetch=1, grid=(2, per), in_specs=[spec, half, half, spec, spec], out_specs=[spec] * 4),
        out_shape=[out] * 4, compiler_params=_cp("parallel", "parallel"),
    )(c_arr, w, g_mine, g_sib, m, v)


_CLASS_SHAPE = {"a": (8, D), "b": (8, RW), "c": (8, 2 * DFF), "d": (1048, HD)}
_SMALL = (
    ("g_mix", "a", 0, 1, D), ("g_ffn", "a", 1, 1, D),
    ("rec_conv_w", "b", 0, 4, RW), ("rec_conv_b", "b", 4, 1, RW), ("lru_lambda", "b", 5, 1, RW),
    ("g_attn_out", "b", 6, 1, RW), ("g_rec_out", "b", 7, 1, RW),
    ("ffn_conv_w", "c", 0, 3, 2 * DFF), ("ffn_conv_b", "c", 3, 1, 2 * DFF),
    ("w_rg", "d", 0, RW, HD), ("w_ig", "d", RW, RW, HD), ("b_rg", "d", 2 * RW, 8, HD), ("b_ig", "d", 2 * RW + 8, 8, HD),
    ("q_norm_g", "d", 2 * RW + 16, 1, HD), ("k_norm_g", "d", 2 * RW + 17, 1, HD),
)
_LOSS_ROW = 2
_CLASSES = ("a", "b", "c", "d")


def _small_allreduce(g, loss_blk):
    names = [s[0] for s in _SMALL]
    nin = len(names) + 1

    def body(*refs):
        ins = dict(zip(names, refs[:len(names)]))
        loss_ref = refs[len(names)]
        outs = dict(zip(_CLASSES, refs[nin:nin + 4]))
        pair = dict(zip(_CLASSES, refs[nin + 4:nin + 8]))
        quad = dict(zip(_CLASSES, refs[nin + 8:nin + 12]))
        send, recv = refs[nin + 12:]
        x, y, c = _mesh_pos()
        chip = 2 * x + y
        pair["a"][c] = jnp.zeros(_CLASS_SHAPE["a"], F32)
        pair["b"][c] = ins["rec_conv_w"][...]
        pair["c"][c] = ins["ffn_conv_w"][...]
        pair["d"][c, 2 * RW + 16:, :] = jnp.zeros((8, HD), F32)
        for name, k, r0, nr, _ in _SMALL:
            if name in ("rec_conv_w", "ffn_conv_w"):
                continue
            pair[k][c, r0:r0 + nr, :] = ins[name][...]
        pair["a"][c, _LOSS_ROW:_LOSS_ROW + 1, :] = jnp.broadcast_to(loss_ref[0:1, 0:1], (1, D))
        cps = [_rcopy(pair[k].at[c], pair[k].at[c], send, recv, ki, (x, y, 1 - c)) for ki, k in enumerate(_CLASSES)]
        for cp in cps:
            cp.start()
        for ki, k in enumerate(_CLASSES):
            _rcopy(pair[k].at[1 - c], pair[k].at[1 - c], send, recv, ki, (x, y, c)).wait_recv()
            quad[k][chip] = pair[k][0] + pair[k][1]
        cps2 = []
        for ki, k in enumerate(_CLASSES):
            for j, (px, py) in enumerate(_other_chips(x, y)):
                cps2.append(_rcopy(quad[k].at[chip], quad[k].at[chip], send, recv, 4 + 3 * ki + j, (px, py, c)))
        for cp in cps2:
            cp.start()
        for ki, k in enumerate(_CLASSES):
            for j, (px, py) in enumerate(_other_chips(x, y)):
                got = quad[k].at[2 * px + py]
                _rcopy(got, got, send, recv, 4 + 3 * ki + j, (x, y, c)).wait_recv()
            outs[k][...] = ((quad[k][0] + quad[k][1]) + quad[k][2]) + quad[k][3]
        for cp in cps + cps2:
            cp.wait_send()

    vm = pl.BlockSpec(memory_space=pltpu.VMEM)
    return pl.pallas_call(
        body, name="small_allreduce", in_specs=[vm] * nin, out_specs=[vm] * 4,
        out_shape=[jax.ShapeDtypeStruct(_CLASS_SHAPE[k], F32) for k in _CLASSES],
        scratch_shapes=[pltpu.VMEM((2,) + _CLASS_SHAPE[k], F32) for k in _CLASSES]
        + [pltpu.VMEM((NCHIP,) + _CLASS_SHAPE[k], F32) for k in _CLASSES]
        + [pltpu.SemaphoreType.DMA((16,)), pltpu.SemaphoreType.DMA((16,))],
        compiler_params=pltpu.CompilerParams(vmem_limit_bytes=VMEM_LIMIT),
    )(*[g[n] for n in names], loss_blk)


def _adam_small(red, w, m, v):
    names = [s[0] for s in _SMALL]
    n = len(names)

    def body(*refs):
        red_refs = dict(zip(_CLASSES, refs[:4]))
        w_refs, m_refs, v_refs = refs[4:4 + n], refs[4 + n:4 + 2 * n], refs[4 + 2 * n:4 + 3 * n]
        loss_ref = refs[4 + 3 * n]
        out_refs = refs[5 + 3 * n:]
        x, y, _ = _mesh_pos()
        chip = 2 * x + y
        loss_ref[...] = jnp.broadcast_to(red_refs["a"][_LOSS_ROW:_LOSS_ROW + 1, 0:1], loss_ref.shape)
        for pi, (name, k, r0, nr, width) in enumerate(_SMALL):
            gfull = red_refs[k][r0:r0 + nr, :]
            if name == "rec_conv_w":
                parts = [gfull[:, 128 * s:128 * (s + 1)] for s in range(NCHIP)]
                g = jnp.where(chip == 0, parts[0], jnp.where(chip == 1, parts[1], jnp.where(chip == 2, parts[2], parts[3])))
            elif name == "ffn_conv_w":
                parts = [gfull[:, FC * s:FC * (s + 1)] for s in range(NCHIP)]
                g = jnp.where(chip == 0, parts[0], jnp.where(chip == 1, parts[2], jnp.where(chip == 2, parts[1], parts[3])))
            elif name == "ffn_conv_b":
                g = jnp.concatenate([gfull[:, FC * s:FC * (s + 1)] for s in (0, 2, 1, 3)], axis=1)
            else:
                g = gfull
            d, m2, v2 = _adam_math(w_refs[pi][...], g, m_refs[pi][...], v_refs[pi][...])
            o = out_refs[4 * pi:4 * pi + 4]
            o[0][...], o[1][...], o[2][...], o[3][...] = g, d, m2, v2

    vm = pl.BlockSpec(memory_space=pltpu.VMEM)
    outs = [jax.ShapeDtypeStruct((1, 128), F32)]
    for name in names:
        outs += [jax.ShapeDtypeStruct(w[name].shape, F32)] * 4
    res = pl.pallas_call(
        body, name="adam_small", in_specs=[vm] * (4 + 3 * n), out_specs=[vm] * len(outs), out_shape=outs,
        compiler_params=pltpu.CompilerParams(vmem_limit_bytes=VMEM_LIMIT),
    )(*red, *[w[k] for k in names], *[m[k] for k in names], *[v[k] for k in names])
    return res[0], {name: res[1 + 4 * i:5 + 4 * i] for i, name in enumerate(names)}


_WEIGHTS = ("g_mix", "w_in", "q_norm_g", "k_norm_g", "rec_conv_w", "rec_conv_b", "w_rg", "b_rg", "w_ig", "b_ig",
            "lru_lambda", "g_attn_out", "g_rec_out", "w_out", "g_ffn", "w_up", "ffn_conv_w", "ffn_conv_b", "w_down")
_BIG = ("w_in", "w_out", "w_up", "w_down")
_BIG_PERM = {"w_in": False, "w_out": False, "w_up": True, "w_down": False}
_SMALL_2D = {"w_rg": (RW, HD), "w_ig": (RW, HD), "b_rg": (8, HD), "b_ig": (8, HD), "rec_conv_w": (4, 128),
             "ffn_conv_w": (3, FC)}


def _halves(a):
    r, c = a.shape
    return a.reshape(2, r // 2, c)


def kernel(x, positions, g_mix, w_in, q_norm_g, k_norm_g, rec_conv_w, rec_conv_b, w_rg, b_rg, w_ig, b_ig, lru_lambda, g_attn_out, g_rec_out, w_out, g_ffn, w_up, ffn_conv_w, ffn_conv_b, w_down, loss_target, m_g_mix, m_w_in, m_q_norm_g, m_k_norm_g, m_rec_conv_w, m_rec_conv_b, m_w_rg, m_b_rg, m_w_ig, m_b_ig, m_lru_lambda, m_g_attn_out, m_g_rec_out, m_w_out, m_g_ffn, m_w_up, m_ffn_conv_w, m_ffn_conv_b, m_w_down, v_g_mix, v_w_in, v_q_norm_g, v_k_norm_g, v_rec_conv_w, v_rec_conv_b, v_w_rg, v_b_rg, v_w_ig, v_b_ig, v_lru_lambda, v_g_attn_out, v_g_rec_out, v_w_out, v_g_ffn, v_w_up, v_ffn_conv_w, v_ffn_conv_b, v_w_down):
    given = dict(g_mix=g_mix, w_in=w_in, q_norm_g=q_norm_g, k_norm_g=k_norm_g, rec_conv_w=rec_conv_w, rec_conv_b=rec_conv_b, w_rg=w_rg, b_rg=b_rg, w_ig=w_ig, b_ig=b_ig, lru_lambda=lru_lambda, g_attn_out=g_attn_out, g_rec_out=g_rec_out, w_out=w_out, g_ffn=g_ffn, w_up=w_up, ffn_conv_w=ffn_conv_w, ffn_conv_b=ffn_conv_b, w_down=w_down)
    given_m = dict(g_mix=m_g_mix, w_in=m_w_in, q_norm_g=m_q_norm_g, k_norm_g=m_k_norm_g, rec_conv_w=m_rec_conv_w, rec_conv_b=m_rec_conv_b, w_rg=m_w_rg, b_rg=m_b_rg, w_ig=m_w_ig, b_ig=m_b_ig, lru_lambda=m_lru_lambda, g_attn_out=m_g_attn_out, g_rec_out=m_g_rec_out, w_out=m_w_out, g_ffn=m_g_ffn, w_up=m_w_up, ffn_conv_w=m_ffn_conv_w, ffn_conv_b=m_ffn_conv_b, w_down=m_w_down)
    given_v = dict(g_mix=v_g_mix, w_in=v_w_in, q_norm_g=v_q_norm_g, k_norm_g=v_k_norm_g, rec_conv_w=v_rec_conv_w, rec_conv_b=v_rec_conv_b, w_rg=v_w_rg, b_rg=v_b_rg, w_ig=v_w_ig, b_ig=v_b_ig, lru_lambda=v_lru_lambda, g_attn_out=v_g_attn_out, g_rec_out=v_g_rec_out, w_out=v_w_out, g_ffn=v_g_ffn, w_up=v_w_up, ffn_conv_w=v_ffn_conv_w, ffn_conv_b=v_ffn_conv_b, w_down=v_w_down)
    shapes = {n: a.shape for n, a in given.items()}

    def two_d(n, a):
        a = a[0]
        return a.reshape(_SMALL_2D[n]) if n in _SMALL_2D else (a if a.ndim == 2 else a[None])

    w = {n: two_d(n, a) for n, a in given.items()}
    m = {n: two_d(n, a) for n, a in given_m.items()}
    v = {n: two_d(n, a) for n, a in given_v.items()}
    cc = lax.axis_index("c").astype(jnp.int32)
    cx, cy = lax.axis_index("x").astype(jnp.int32), lax.axis_index("y").astype(jnp.int32)
    slot = {False: 2 * cx + cy, True: 2 * cy + cx}

    shards = {n: _halves(_cast_bf16(f"cast_{n}", w[n])) for n in _BIG}
    small = [(jnp.pad(w["ffn_conv_w"], ((0, 5), (0, 0))), True), (jnp.pad(w["rec_conv_w"], ((0, 4), (0, 0))), False)]
    f_in, f_fcw, f_rcw = _gather_weights([(shards["w_in"], False)], small, slot)
    p = {n: w[n] for n in ("g_mix", "g_ffn", "q_norm_g", "k_norm_g", "rec_conv_b", "lru_lambda", "g_attn_out", "g_rec_out")}
    p.update(w_rg=w["w_rg"].reshape(8, HD, HD), w_ig=w["w_ig"].reshape(8, HD, HD), b_rg=w["b_rg"], b_ig=w["b_ig"],
             w_in=f_in.reshape(NCHIP, D, INW // NCHIP), ffn_conv_w=f_fcw,
             ffn_conv_b=jnp.concatenate([w["ffn_conv_b"][:, FC * s:FC * (s + 1)] for s in (0, 2, 1, 3)], axis=1),
             rec_conv_w=f_rcw.transpose(1, 0, 2).reshape(8, RW))

    class Exchange:
        rest = ("w_out", "w_up", "w_down")
        order = []
        flight = {}

        def start_rest(self):
            srcs = [shards[n] for n in self.rest]
            lands = [lax.dynamic_update_slice(lax.empty((NCHIP,) + s.shape, BF16), s[None], (slot[_BIG_PERM[n]], 0, 0, 0))
                     for n, s in zip(self.rest, srcs)]
            plan = _gather_plan([_BIG_PERM[n] for n in self.rest])
            send, recv, srcs, lands, token = _split_start("gather_rest_start", srcs, lands, plan, 6 * len(srcs))
            self.flight["rest"] = (send, recv, srcs, lands, plan)
            return (token,)

        def wait_rest(self, after):
            send, recv, srcs, lands, plan = self.flight.pop("rest")
            f_out, f_up, f_down = _split_wait("gather_rest_wait", send, recv, srcs, lands, plan, after)
            return dict(w_out=f_out.reshape(D, D), w_up=f_up.reshape(NCHIP, D, FC), w_down=f_down.reshape(DFF, D))

        def reduce_start(self, name, g32, g16):
            r2, cols = shards[name].shape[1:]
            plan = _reduce_plan(_BIG_PERM[name])
            send, recv, srcs, lands, token = _split_start(
                f"reduce_{name}_start", [g16.reshape(NCHIP, 2, r2, cols)], [lax.empty((7, r2, cols), BF16)], plan, 7)
            self.flight[name] = (send, recv, srcs, lands, plan, g32.reshape(NCHIP, 2, r2, cols))
            self.order.append(name)
            return (token,)

        def finish(self, after):
            mine = {}
            for name in self.order:
                send, recv, srcs, lands, plan, g32 = self.flight.pop(name)
                (got,) = _split_wait(f"reduce_{name}_wait", send, recv, srcs, lands, plan, after)
                where = jnp.stack([slot[_BIG_PERM[name]], cc])
                mine[name] = after = _add_pieces(f"reduce_{name}_add", g32, got, where)
            theirs = dict(zip(_BIG, _sibling_share([mine[n] for n in _BIG])))
            return mine, theirs

    exch = Exchange()

    loss_blk, grad_x, g = _local_step(x[0], positions.reshape(T, 1), loss_target[0], p, exch)

    out_g, out_d, out_m, out_v = {}, {}, {}, {}
    red = _small_allreduce(g, loss_blk)
    loss_row, small_out = _adam_small(red, w, m, v)
    for n, (gn, dn, mn, vn) in small_out.items():
        out_g[n], out_d[n], out_m[n], out_v[n] = gn, dn, mn, vn

    mine, theirs = exch.finish(red[0])
    for n in _BIG:
        out_g[n], out_d[n], out_m[n], out_v[n] = _adam_big(f"adam_{n}", w[n], mine[n], theirs[n], m[n], v[n], cc.reshape(1))

    outs = [loss_row[0, 0], grad_x[None]]
    for group in (out_g, out_d, out_m, out_v):
        outs += [group[n].reshape(shapes[n]) for n in _WEIGHTS]
    return tuple(outs)
```

```python
import math

import jax
import jax.numpy as jnp
import numpy as np
from jax import lax
from jax.experimental import pallas as pl
from jax.experimental.pallas import tpu as pltpu

F32 = jnp.float32
BF16 = jnp.bfloat16

T = 4096
D = 1024
HD = 64
AW = 512
RW = 512
INW = 2560
DFF = 3072
NCHIP = 4
EPS = 1e-6
NEG = -1e30
LRU_C = 8.0
ROPE_THETA = 10000.0
BLK = 128
DILATIONS = (1, 4, 16)
ADAM_LR, ADAM_B1, ADAM_B2, ADAM_EPS, ADAM_WD, ADAM_STEP = 0.001, 0.9, 0.999, 1e-08, 0.01, 10
VMEM_LIMIT = 56 * 1024 * 1024
MESH = pl.DeviceIdType.MESH

NN = (((1,), (0,)), ((), ()))
NT = (((1,), (1,)), ((), ()))
TN = (((0,), (0,)), ((), ()))


def _cp(*sem):
    return pltpu.CompilerParams(dimension_semantics=sem, vmem_limit_bytes=VMEM_LIMIT)


def _bs(shape, fn):
    return pl.BlockSpec(shape, fn)


def _dot(a, b, dims=NN):
    return lax.dot_general(a, b, dims, preferred_element_type=F32)


_GC = math.sqrt(2.0 / math.pi)


def _gelu(x):
    return x * (0.5 + 0.5 * jnp.tanh(x * (_GC + (_GC * 0.044715) * (x * x))))


def _gelu_and_grad(x):
    x2 = x * x
    th = jnp.tanh(x * (_GC + (_GC * 0.044715) * x2))
    cdf = 0.5 + 0.5 * th
    dg = cdf + (x * (1.0 - th * th)) * ((0.5 * _GC) + (1.5 * 0.044715 * _GC) * x2)
    return x * cdf, dg


def _softplus(x):
    e = jnp.exp(-jnp.abs(x))
    u = 1.0 + e
    l1p = jnp.where(u == 1.0, e, jnp.log(u) * (e / (u - 1.0)))
    return jnp.maximum(x, 0.0) + l1p


def _segsum(z, e_bf16):
    hi = z.astype(BF16)
    lo = (z - hi.astype(F32)).astype(BF16)
    parts = []
    for c0 in range(0, z.shape[1], 128):
        parts.append(_dot(hi[:, c0:c0 + 128], e_bf16) + _dot(lo[:, c0:c0 + 128], e_bf16))
    return jnp.concatenate(parts, axis=1)


def _mm(name, a, b, mode, tm, tn, out_dtype=F32, res=None, stack=0, twin_bf16=False, after=(), a_full=False,
        b_full=False):
    if mode == "nn":
        (m, k), n = a.shape, (b.shape[1] if not stack else stack * b.shape[2])
        a_spec = _bs((tm, k), lambda j, i: (i, 0))
        if stack:
            per = b.shape[2] // tn
            b_spec = _bs((None, k, tn), lambda j, i: (j // per, 0, j % per))
        else:
            b_spec = _bs((k, tn), lambda j, i: (0, j))
    elif mode == "nt":
        (m, k), n = a.shape, (b.shape[0] if not stack else b.shape[1])
        a_spec = _bs((tm, k), lambda j, i: (i, 0))
        b_spec = _bs((stack, tn, k // stack), lambda j, i: (0, j, 0)) if stack else _bs((tn, k), lambda j, i: (j, 0))
    else:
        (k, m), n = a.shape, b.shape[1]
        a_spec, b_spec = _bs((k, tm), lambda j, i: (0, i)), _bs((k, tn), lambda j, i: (0, j))
    assert m % tm == 0 and n % tn == 0
    o_spec = _bs((tm, tn), lambda j, i: (i, j))
    o_shape = (m, n)
    if mode == "tn" and stack:
        per = n // stack // tn
        o_spec = _bs((None, tm, tn), lambda j, i: (j // per, i, j % per))
        o_shape = (stack, m, n // stack)
    dims = {"nn": NN, "nt": NT, "tn": TN}[mode]
    once = pl.Buffered(1)
    if a_full:
        a_spec = pl.BlockSpec(a.shape, lambda j, i: (0, 0), pipeline_mode=once)
    if b_full:
        assert n == tn
        b_spec = pl.BlockSpec(b_spec.block_shape, b_spec.index_map, pipeline_mode=once)

    def product(a_ref, b_ref):
        if a_full:
            mine = pl.ds(pl.multiple_of(pl.program_id(1) * tm, tm), tm)
            take = (lambda cols: a_ref[:, mine]) if mode == "tn" else (lambda cols: a_ref[mine, cols])
        else:
            take = lambda cols: a_ref[:, cols]
        if mode == "nt" and stack:
            cs = k // stack
            acc = _dot(take(pl.ds(0, cs)), b_ref[0], NT)
            for s in range(1, stack):
                acc = acc + _dot(take(pl.ds(s * cs, cs)), b_ref[s], NT)
            return acc
        return _dot(take(slice(None)), b_ref[...], dims)

    nres = 0 if res is None else 1

    def body(a_ref, b_ref, *rest):
        acc = product(a_ref, b_ref)
        if nres:
            acc = rest[0][...] + acc
        outs = rest[nres + len(after):]
        outs[0][...] = acc.astype(out_dtype)
        if twin_bf16:
            outs[1][...] = acc.astype(BF16)

    ins = (a, b) + ((res,) if nres else ()) + tuple(after)
    specs = [a_spec, b_spec] + ([o_spec] if nres else []) + [pl.BlockSpec(memory_space=pl.ANY)] * len(after)
    shapes = [jax.ShapeDtypeStruct(o_shape, out_dtype)] + ([jax.ShapeDtypeStruct(o_shape, BF16)] if twin_bf16 else [])
    out = pl.pallas_call(
        body, name=name, grid=(n // tn, m // tm), in_specs=specs, out_specs=[o_spec] * len(shapes),
        out_shape=shapes, compiler_params=_cp("parallel", "parallel"),
    )(*ins)
    return tuple(out) if twin_bf16 else out[0]


def _rms_fwd(name, x, g):
    tr = 512

    def body(x_ref, g_ref, o_ref):
        xv = x_ref[...]
        r = lax.rsqrt(jnp.mean(xv * xv, axis=-1, keepdims=True) + EPS)
        o_ref[...] = ((xv * r) * g_ref[...]).astype(BF16)

    return pl.pallas_call(
        body, name=name, grid=(T // tr,), in_specs=[_bs((tr, D), lambda i: (i, 0)), _bs((1, D), lambda i: (0, 0))],
        out_specs=_bs((tr, D), lambda i: (i, 0)), out_shape=jax.ShapeDtypeStruct((T, D), BF16),
        compiler_params=_cp("parallel"),
    )(x, g)


def _rms_bwd(name, x, g, dy, dres, want_bf16):
    tr = 256

    def body(x_ref, g_ref, dy_ref, dr_ref, dx_ref, *rest):
        dg_ref = rest[-1]
        xv, dyv = x_ref[...], dy_ref[...]
        r = lax.rsqrt(jnp.mean(xv * xv, axis=-1, keepdims=True) + EPS)
        gdy = g_ref[...] * dyv
        dx = r * gdy - xv * ((r * r * r) * jnp.mean(xv * gdy, axis=-1, keepdims=True)) + dr_ref[...]
        dx_ref[...] = dx
        if want_bf16:
            rest[0][...] = dx.astype(BF16)

        @pl.when(pl.program_id(0) == 0)
        def _():
            dg_ref[...] = jnp.zeros_like(dg_ref)

        dg_ref[...] += jnp.sum(dyv * (xv * r), axis=0, keepdims=True)

    row = _bs((tr, D), lambda i: (i, 0))
    vec = _bs((1, D), lambda i: (0, 0))
    outs = [jax.ShapeDtypeStruct((T, D), F32)] + ([jax.ShapeDtypeStruct((T, D), BF16)] if want_bf16 else [])
    return pl.pallas_call(
        body, name=name, grid=(T // tr,), in_specs=[row, vec, row, row],
        out_specs=[row] * len(outs) + [vec], out_shape=outs + [jax.ShapeDtypeStruct((1, D), F32)],
        compiler_params=_cp("arbitrary"),
    )(x, g, dy, dres)


def _head_ones():
    idx = np.arange(128) // HD
    return jnp.asarray((idx[:, None] == idx[None, :]).astype(np.float32), dtype=BF16)


def _freq_row():
    half = HD // 2
    inv = ROPE_THETA ** (-(np.arange(half, dtype=np.float64)) / half)
    return jnp.asarray(np.tile(inv, 4)[None, :], dtype=F32)


def _rot_tables(cos128, sin128):
    c = jnp.tile(cos128, (1, 4))
    s = jnp.tile(sin128, (1, 4))
    lane = lax.broadcasted_iota(jnp.int32, (1, AW), 1)
    first = (lane & 32) == 0
    return c, jnp.where(first, -s, s), first


def _swap_halves(y, first):
    return jnp.where(first, pltpu.roll(y, AW - 32, 1), pltpu.roll(y, 32, 1))


def _qk_prep(proj, pos_col, qg, kg):
    tr = 512

    def body(q_ref, k_ref, pos_ref, f_ref, qg_ref, kg_ref, e_ref, qo_ref, ko_ref, cos_ref, sin_ref):
        ang = pos_ref[...].astype(F32) * f_ref[...]
        cos_ref[...] = jnp.cos(ang)
        sin_ref[...] = jnp.sin(ang)
        c, s_signed, first = _rot_tables(cos_ref[...], sin_ref[...])
        e = e_ref[...]

        def norm_rot(xv, g, scale):
            r = lax.rsqrt(_segsum(xv * xv, e) * (1.0 / HD) + EPS)
            y = (xv * r) * g
            return (y * c + _swap_halves(y, first) * s_signed) * scale

        qo_ref[...] = norm_rot(q_ref[...], qg_ref[...], HD ** -0.5)
        ko_ref[...] = norm_rot(k_ref[...], kg_ref[...], 1.0)

    col = lambda j: _bs((tr, AW), lambda i, j=j: (i, j))
    vec = _bs((1, AW), lambda i: (0, 0))
    out = jax.ShapeDtypeStruct((T, AW), F32)
    tab = jax.ShapeDtypeStruct((T, 128), F32)
    tspec = _bs((tr, 128), lambda i: (i, 0))
    return pl.pallas_call(
        body, name="qk_prep", grid=(T // tr,),
        in_specs=[col(0), col(1), _bs((tr, 1), lambda i: (i, 0)), _bs((1, 128), lambda i: (0, 0)), vec, vec,
                  _bs((128, 128), lambda i: (0, 0))],
        out_specs=[col(0)] * 2 + [tspec] * 2, out_shape=[out, out, tab, tab], compiler_params=_cp("parallel"),
    )(proj, proj, pos_col, _freq_row(), qg, kg, _head_ones())


def _qk_bwd(proj, cos_t, sin_t, qg, kg, dq, dk, dv):
    tr = 256

    def body(q_ref, k_ref, cos_ref, sin_ref, qg_ref, kg_ref, e_ref, dq_ref, dk_ref, dv_ref, o_ref, dqg_ref, dkg_ref):
        i, j = pl.program_id(0), pl.program_id(1)

        @pl.when((i == 0) & (j == 0))
        def _():
            dqg_ref[...] = jnp.zeros_like(dqg_ref)
            dkg_ref[...] = jnp.zeros_like(dkg_ref)

        def norm_rot_bwd(x_ref, g_ref, dg_ref, d_ref, scale):
            c, s_signed, first = _rot_tables(cos_ref[...], sin_ref[...])
            e = e_ref[...]
            dout = d_ref[...] * scale
            dy = dout * c + _swap_halves(dout * s_signed, first)
            xv, g = x_ref[...], g_ref[...]
            r = lax.rsqrt(_segsum(xv * xv, e) * (1.0 / HD) + EPS)
            gdy = g * dy
            dx = r * gdy - xv * ((r * r * r) * (_segsum(xv * gdy, e) * (1.0 / HD)))
            o_ref[...] = dx.astype(BF16)
            dg_ref[...] += jnp.sum(dy * (xv * r), axis=0, keepdims=True)

        @pl.when(j == 0)
        def _():
            norm_rot_bwd(q_ref, qg_ref, dqg_ref, dq_ref, HD ** -0.5)

        @pl.when(j == 1)
        def _():
            norm_rot_bwd(k_ref, kg_ref, dkg_ref, dk_ref, 1.0)

        @pl.when(j == 2)
        def _():
            o_ref[...] = dv_ref[...].astype(BF16)

    col = lambda jj: _bs((tr, AW), lambda i, j, jj=jj: (i, jj))
    vec = _bs((1, AW), lambda i, j: (0, 0))
    piece = _bs((tr, AW), lambda i, j: (i, 0))
    return pl.pallas_call(
        body, name="qk_bwd", grid=(T // tr, 3),
        in_specs=[col(0), col(1), _bs((tr, 128), lambda i, j: (i, 0)), _bs((tr, 128), lambda i, j: (i, 0)), vec, vec,
                  _bs((128, 128), lambda i, j: (0, 0))] + [piece] * 3,
        out_specs=[_bs((tr, AW), lambda i, j: (i, j)), vec, vec],
        out_shape=[jax.ShapeDtypeStruct((T, INW), BF16), jax.ShapeDtypeStruct((1, AW), F32),
                   jax.ShapeDtypeStruct((1, AW), F32)],
        compiler_params=_cp("arbitrary", "arbitrary"),
    )(proj, proj, cos_t, sin_t, qg, kg, _head_ones(), dq, dk, dv)


RG = 256
QC = 64


def _stacked_band_mask(rows=2 * BLK, q0=0):
    qi = (lax.broadcasted_iota(jnp.int32, (rows, 2 * BLK), 0) + q0) & (BLK - 1)
    kj = lax.broadcasted_iota(jnp.int32, (rows, 2 * BLK), 1)
    rel = qi - kj + BLK
    return (rel >= 0) & (rel <= BLK), lax.broadcasted_iota(jnp.int32, (1, 2 * BLK), 1) >= BLK


def _natural_rows(r0, n_rows, d):
    if d == 1:
        return pl.ds(r0, n_rows)
    ln = T // d
    return pl.ds(r0 // ln + d * (r0 % ln), n_rows, stride=d)


def _regroup_into(dst, src_ref, d, pad, cast=True):
    def step(j, carry):
        r0 = pl.multiple_of(j * RG, RG)
        val = src_ref[_natural_rows(r0, RG, d), :]
        dst[pl.ds(pad + r0, RG), :] = val.astype(dst.dtype) if cast else val
        return carry
    lax.fori_loop(0, T // RG, step, 0)


def _stack_heads(x, h0):
    zero = jnp.zeros_like(x)
    return jnp.concatenate([jnp.where(h0, x, zero), jnp.where(h0, zero, x)], axis=0)


def _attn_fwd(q, k, proj):
    nblk = T // BLK

    def body(q_ref, k_ref, v_ref, a_ref, lse_ref, qs, ks, vs, o0, o1, o2, l0, l1, l2, sb0, sb1):
        band, cur_half = _stacked_band_mask()
        h0 = lax.broadcasted_iota(jnp.int32, (1, 128), 1) < HD
        ks[0:BLK, :] = jnp.zeros((BLK, 128), BF16)
        vs[0:BLK, :] = jnp.zeros((BLK, 128), BF16)
        for d, o_s, l_s in zip(DILATIONS, (o0, o1, o2), (l0, l1, l2)):
            nb = T // d // BLK
            _regroup_into(qs, q_ref, d, 0)
            _regroup_into(ks, k_ref, d, BLK)
            _regroup_into(vs, v_ref, d, BLK)

            def scores(b):
                r0 = pl.multiple_of(b * BLK, BLK)
                return _dot(_stack_heads(qs[pl.ds(r0, BLK), :], h0), ks[pl.ds(r0, 2 * BLK), :], NT)

            def finish(b, s_raw, d=d, nb=nb, o_s=o_s, l_s=l_s):
                r0 = pl.multiple_of(b * BLK, BLK)
                mask = band & (cur_half | ((b & (nb - 1)) > 0))
                s = jnp.where(mask, s_raw, NEG)
                m = jnp.max(s, axis=1, keepdims=True)
                p = jnp.exp(s - m)
                l = jnp.sum(p, axis=1, keepdims=True)
                o = _dot(p.astype(BF16), vs[pl.ds(r0, 2 * BLK), :]) / l
                lse = m + jnp.log(l)
                rows = _natural_rows(r0, BLK, d)
                o_s[rows, :] = jnp.where(h0, o[0:BLK, :], o[BLK:, :])
                l_s[rows, :] = jnp.where(h0, lse[0:BLK, :], lse[BLK:, :])

            sb0[...] = scores(0)

            def step(i, carry):
                b = 2 * i
                sb1[...] = scores(b + 1)
                finish(b, sb0[...])
                sb0[...] = scores(jnp.minimum(b + 2, nblk - 1))
                finish(b + 1, sb1[...])
                return carry

            lax.fori_loop(0, nblk // 2, step, 0)

        def merge(i, carry):
            r = pl.ds(pl.multiple_of(i * RG, RG), RG)
            la, lb, lc = l0[r, :], l1[r, :], l2[r, :]
            m = jnp.maximum(jnp.maximum(la, lb), lc)
            ea, eb, ec = jnp.exp(la - m), jnp.exp(lb - m), jnp.exp(lc - m)
            z = (ea + eb) + ec
            a_ref[r, :] = ((ea * o0[r, :] + eb * o1[r, :]) + ec * o2[r, :]) / z
            lse_ref[r, :] = m + jnp.log(z)
            return carry

        lax.fori_loop(0, T // RG, merge, 0)

    spec = lambda cb: _bs((T, 128), lambda p, cb=cb: (0, cb + p))
    out = jax.ShapeDtypeStruct((T, AW), F32)
    return pl.pallas_call(
        body, name="attn_fwd", grid=(AW // 128,), in_specs=[spec(0), spec(0), spec(8)], out_specs=[spec(0)] * 2,
        out_shape=[out] * 2,
        scratch_shapes=[pltpu.VMEM((T, 128), BF16), pltpu.VMEM((T + BLK, 128), BF16), pltpu.VMEM((T + BLK, 128), BF16)]
        + [pltpu.VMEM((T, 128), F32)] * 6 + [pltpu.VMEM((2 * BLK, 2 * BLK), F32)] * 2,
        compiler_params=_cp("parallel"),
    )(q, k, proj)


def _attn_bwd(q, k, proj, do, lse, delta):
    nblk = T // BLK

    def body(q_ref, k_ref, v_ref, do_ref, l_ref, dl_ref, dq_ref, dk_ref, dv_ref, qs, dos, ks, vs, ls, dls, dks, dvs,
             sa0, sa1, da0, da1):
        band, cur_half = _stacked_band_mask()
        h0 = lax.broadcasted_iota(jnp.int32, (1, 128), 1) < HD
        ks[0:BLK, :] = jnp.zeros((BLK, 128), BF16)
        vs[0:BLK, :] = jnp.zeros((BLK, 128), BF16)
        for d in DILATIONS:
            nb = T // d // BLK
            _regroup_into(qs, q_ref, d, 0)
            _regroup_into(dos, do_ref, d, 0)
            _regroup_into(ks, k_ref, d, BLK)
            _regroup_into(vs, v_ref, d, BLK)
            _regroup_into(ls, l_ref, d, 0, cast=False)
            _regroup_into(dls, dl_ref, d, 0, cast=False)
            dks[...] = jnp.zeros_like(dks)
            dvs[...] = jnp.zeros_like(dvs)

            def scores(b, s_buf, dp_buf):
                r0 = pl.multiple_of(b * BLK, BLK)
                win = pl.ds(r0, 2 * BLK)
                s_buf[...] = _dot(_stack_heads(qs[pl.ds(r0, BLK), :], h0), ks[win, :], NT)
                dp_buf[...] = _dot(_stack_heads(dos[pl.ds(r0, BLK), :], h0), vs[win, :], NT)

            def finish(b, s_buf, dp_buf, d=d, nb=nb):
                r0 = pl.multiple_of(b * BLK, BLK)
                mask = band & (cur_half | ((b & (nb - 1)) > 0))
                win = pl.ds(r0, 2 * BLK)
                lv, dlv = ls[pl.ds(r0, BLK), :], dls[pl.ds(r0, BLK), :]
                lse2 = jnp.concatenate([lv[:, 0:1], lv[:, HD:HD + 1]], axis=0)
                dl2 = jnp.concatenate([dlv[:, 0:1], dlv[:, HD:HD + 1]], axis=0)
                p = jnp.exp(jnp.where(mask, s_buf[...], NEG) - lse2)
                ds = p * (dp_buf[...] - dl2)
                pb, dsb = p.astype(BF16), ds.astype(BF16)
                dq2 = _dot(dsb, ks[win, :])
                dks[win, :] += _dot(dsb, _stack_heads(qs[pl.ds(r0, BLK), :], h0), TN)
                dvs[win, :] += _dot(pb, _stack_heads(dos[pl.ds(r0, BLK), :], h0), TN)
                rows = _natural_rows(r0, BLK, d)
                dq = jnp.where(h0, dq2[0:BLK, :], dq2[BLK:, :])
                dq_ref[rows, :] = dq if d == 1 else dq_ref[rows, :] + dq

            scores(0, sa0, da0)

            def step(i, carry):
                b = 2 * i
                scores(b + 1, sa1, da1)
                finish(b, sa0, da0)
                scores(jnp.minimum(b + 2, nblk - 1), sa0, da0)
                finish(b + 1, sa1, da1)
                return carry

            lax.fori_loop(0, nblk // 2, step, 0)

            def back(j, carry, d=d):
                r0 = pl.multiple_of(j * RG, RG)
                rows = _natural_rows(r0, RG, d)
                src = pl.ds(BLK + r0, RG)
                dk_ref[rows, :] = dks[src, :] if d == 1 else dk_ref[rows, :] + dks[src, :]
                dv_ref[rows, :] = dvs[src, :] if d == 1 else dv_ref[rows, :] + dvs[src, :]
                return carry

            lax.fori_loop(0, T // RG, back, 0)

    spec = lambda cb: _bs((T, 128), lambda p, cb=cb: (0, cb + p))
    ospec = _bs((T, 128), lambda p: (0, p))
    out = jax.ShapeDtypeStruct((T, AW), F32)
    return pl.pallas_call(
        body, name="attn_bwd", grid=(AW // 128,), in_specs=[spec(0), spec(0), spec(8), spec(0), spec(0), spec(0)],
        out_specs=[ospec] * 3, out_shape=[out] * 3,
        scratch_shapes=[pltpu.VMEM((T, 128), BF16), pltpu.VMEM((T, 128), BF16), pltpu.VMEM((T + BLK, 128), BF16),
                        pltpu.VMEM((T + BLK, 128), BF16), pltpu.VMEM((T, 128), F32), pltpu.VMEM((T, 128), F32),
                        pltpu.VMEM((T + BLK, 128), F32), pltpu.VMEM((T + BLK, 128), F32)]
        + [pltpu.VMEM((2 * BLK, 2 * BLK), F32)] * 4,
        compiler_params=_cp("parallel"),
    )(q, k, proj, do, lse, delta)


def _attn_norm(attn, g_attn):
    tr = 512

    def body(a_ref, g_ref, mix_ref):
        attn = a_ref[...]
        r = lax.rsqrt(jnp.mean(attn * attn, axis=-1, keepdims=True) + EPS)
        mix_ref[...] = ((attn * r) * g_ref[...]).astype(BF16)

    row = _bs((tr, AW), lambda i: (i, 0))
    return pl.pallas_call(
        body, name="attn_norm", grid=(T // tr,), in_specs=[row, _bs((1, AW), lambda i: (0, 0))],
        out_specs=row, out_shape=jax.ShapeDtypeStruct((T, D), BF16), compiler_params=_cp("parallel"),
    )(attn, g_attn)


def _attn_out_bwd(attn, dmix, g_attn):
    tr = 256

    def body(a_ref, d_ref, g_ref, e_ref, do_ref, dl_ref, dg_ref):
        av, dyv = a_ref[...], d_ref[...]
        r = lax.rsqrt(jnp.mean(av * av, axis=-1, keepdims=True) + EPS)
        gdy = g_ref[...] * dyv
        da = r * gdy - av * ((r * r * r) * jnp.mean(av * gdy, axis=-1, keepdims=True))
        do_ref[...] = da
        dl_ref[...] = _segsum(da * av, e_ref[...])

        @pl.when(pl.program_id(0) == 0)
        def _():
            dg_ref[...] = jnp.zeros_like(dg_ref)

        dg_ref[...] += jnp.sum(dyv * (av * r), axis=0, keepdims=True)

    row = _bs((tr, AW), lambda i: (i, 0))
    vec = _bs((1, AW), lambda i: (0, 0))
    return pl.pallas_call(
        body, name="attn_out_bwd", grid=(T // tr,), in_specs=[row, row, vec, _bs((128, 128), lambda i: (0, 0))],
        out_specs=[row, row, vec],
        out_shape=[jax.ShapeDtypeStruct((T, AW), F32), jax.ShapeDtypeStruct((T, AW), F32),
                   jax.ShapeDtypeStruct((1, AW), F32)],
        compiler_params=_cp("arbitrary"),
    )(attn, dmix, g_attn, _head_ones())


TRR = 256


def _scan_fwd(a, u):
    n = a.shape[0]
    row = lax.broadcasted_iota(jnp.int32, (n, 1), 0)
    s = 1
    while s < n:
        keep = row >= s
        u = jnp.where(keep, a * pltpu.roll(u, s, 0) + u, u)
        a = jnp.where(keep, a * pltpu.roll(a, s, 0), a)
        s *= 2
    return a, u


def _scan_bwd(c, w):
    n = c.shape[0]
    row = lax.broadcasted_iota(jnp.int32, (n, 1), 0)
    s = 1
    while s < n:
        keep = row < n - s
        w = jnp.where(keep, c * pltpu.roll(w, n - s, 0) + w, w)
        c = jnp.where(keep, c * pltpu.roll(c, n - s, 0), c)
        s *= 2
    return w


def _gates(xc, wrg, wig, brg, big, sp):
    xcb = xc.astype(BF16)
    r = jax.nn.sigmoid(_dot(xcb, wrg) + brg)
    ig = jax.nn.sigmoid(_dot(xcb, wig) + big)
    la = (-LRU_C * r) * sp
    a = jnp.exp(la)
    mult = jnp.sqrt(-jnp.tanh(la) * (a * a + 1.0))
    return r, ig, a, mult


def _conv4(ext_ref, xr, cw_ref, cb_ref, n):
    y = cb_ref[...] + ext_ref[pl.ds(5, n), :] * cw_ref[0:1, :]
    y = y + ext_ref[pl.ds(6, n), :] * cw_ref[1:2, :]
    y = y + ext_ref[pl.ds(7, n), :] * cw_ref[2:3, :]
    return y + xr * cw_ref[3:4, :]


def _rec_fwd(proj, mix, cw, cb, wrg, wig, brg, big, lam, g_rec):
    n = TRR

    def body(xr_ref, gr_ref, cw_ref, cb_ref, wrg_ref, wig_ref, brg_ref, big_ref, lam_ref, g_ref, mix_in,
             mix_ref, h_ref, ext, hcar):
        del mix_in

        @pl.when(pl.program_id(0) == 0)
        def _():
            ext[0:8, :] = jnp.zeros((8, RW), F32)
            hcar[...] = jnp.zeros_like(hcar)

        xr = xr_ref[...]
        ext[8:, :] = xr
        xc = _conv4(ext, xr, cw_ref, cb_ref, n)
        ext[0:8, :] = xr[n - 8:, :]
        sp = _softplus(-lam_ref[...])
        _, ig, a, mult = _gates(xc, wrg_ref[...], wig_ref[...], brg_ref[...], big_ref[...], sp)
        a_s, u_s = _scan_fwd(a, mult * (ig * xc))
        h = u_s + a_s * hcar[7:8, :]
        h_ref[...] = h
        hcar[...] = h[n - 8:, :]
        pre = h * _gelu(gr_ref[...])
        r = lax.rsqrt(jnp.mean(pre * pre, axis=-1, keepdims=True) + EPS)
        mix_ref[...] = ((pre * r) * g_ref[...]).astype(BF16)

    vec = _bs((1, RW), lambda i: (0, 0))
    mat = _bs((RW, RW), lambda i: (0, 0))
    return pl.pallas_call(
        body, name="rec_fwd", grid=(T // n,),
        in_specs=[_bs((n, RW), lambda i: (i, 3)), _bs((n, RW), lambda i: (i, 4)), _bs((8, RW), lambda i: (0, 0)), vec,
                  mat, mat, vec, vec, vec, vec, pl.BlockSpec(memory_space=pl.ANY)],
        out_specs=[_bs((n, RW), lambda i: (i, 1)), _bs((n, RW), lambda i: (i, 0))],
        out_shape=[jax.ShapeDtypeStruct((T, D), BF16), jax.ShapeDtypeStruct((T, RW), F32)],
        scratch_shapes=[pltpu.VMEM((n + 8, RW), F32), pltpu.VMEM((8, RW), F32)],
        input_output_aliases={10: 0}, compiler_params=_cp("arbitrary"),
    )(proj, proj, cw, cb, wrg, wig, brg, big, lam, g_rec, mix)


def _rec_bwd(proj, h, dmix, dproj, cw, cb, wrg, wig, brg, big, lam, g_rec):
    n = TRR
    nt = T // n
    hb = n // 8

    def body(xr_ref, xh_ref, gr_ref, h_ref, hh_ref, dm_ref, cw_ref, cb_ref, wrg_ref, wig_ref, brg_ref, big_ref,
             lam_ref, g_ref, dp_in, dp_ref, xc_ref, dr_ref, di_ref, dcw_ref, dcb_ref, dbr_ref, dbi_ref, dsp_ref,
             dg_ref, ext, exth, extd, adh, dgr_s):
        del dp_in
        i, j = pl.program_id(0), pl.program_id(1)
        first_tile = i == nt - 1
        last_tile = i == 0

        @pl.when(j == 0)
        def _():
            @pl.when(last_tile)
            def _():
                for ref in (dcw_ref, dcb_ref, dbr_ref, dbi_ref, dsp_ref, dg_ref):
                    ref[...] = jnp.zeros_like(ref)
                extd[n:, :] = jnp.zeros((8, RW), F32)
                adh[...] = jnp.zeros_like(adh)

            row = lax.broadcasted_iota(jnp.int32, (n, 1), 0)
            xr = xr_ref[...]
            ext[0:8, :] = jnp.where(first_tile, 0.0, xh_ref[...])
            ext[8:, :] = xr
            xc = _conv4(ext, xr, cw_ref, cb_ref, n)
            sp = _softplus(-lam_ref[...])
            wrg, wig = wrg_ref[...], wig_ref[...]
            r, ig, a, mult = _gates(xc, wrg, wig, brg_ref[...], big_ref[...], sp)

            hv = h_ref[...]
            gl, dgl = _gelu_and_grad(gr_ref[...])
            pre = hv * gl
            dyv = dm_ref[...]
            rr = lax.rsqrt(jnp.mean(pre * pre, axis=-1, keepdims=True) + EPS)
            gdy = g_ref[...] * dyv
            dpre = rr * gdy - pre * ((rr * rr * rr) * jnp.mean(pre * gdy, axis=-1, keepdims=True))
            dg_ref[...] += jnp.sum(dyv * (pre * rr), axis=0, keepdims=True)
            dgr_s[...] = dpre * hv * dgl

            is_last_row = row == n - 1
            w = dpre * gl + jnp.where(is_last_row, adh[0:1, :], 0.0)
            c = jnp.where(is_last_row, 0.0, pltpu.roll(a, n - 1, 0))
            dh = _scan_bwd(c, w)
            adh[...] = (a * dh)[0:8, :]

            exth[0:8, :] = jnp.where(first_tile, 0.0, hh_ref[...])
            exth[8:, :] = hv
            da = dh * exth[pl.ds(7, n), :]
            ixc = ig * xc
            dmult = dh * ixc
            dla = da * a - dmult * ((a * a) / mult)
            dsp_ref[...] += jnp.sum(dla * (-LRU_C * r), axis=0, keepdims=True)
            dpr = (dla * (-LRU_C * sp)) * (r * (1.0 - r))
            dpi = (dh * (mult * xc)) * (ig * (1.0 - ig))
            dprb, dpib = dpr.astype(BF16), dpi.astype(BF16)
            dxc = dh * (mult * ig) + _dot(dprb, wrg, NT) + _dot(dpib, wig, NT)
            dbr_ref[...] += jnp.sum(dpr, axis=0, keepdims=True)
            dbi_ref[...] += jnp.sum(dpi, axis=0, keepdims=True)
            xc_ref[...] = xc.astype(BF16)
            dr_ref[...] = dprb
            di_ref[...] = dpib

            extd[0:n, :] = dxc
            dxr = dxc * cw_ref[3:4, :] + extd[pl.ds(1, n), :] * cw_ref[2:3, :]
            dxr = dxr + extd[pl.ds(2, n), :] * cw_ref[1:2, :] + extd[pl.ds(3, n), :] * cw_ref[0:1, :]
            extd[n:, :] = dxc[0:8, :]
            dcb_ref[...] += jnp.sum(dxc, axis=0, keepdims=True)
            for kk in range(4):
                dcw_ref[kk:kk + 1, :] += jnp.sum(dxc * ext[pl.ds(5 + kk, n), :], axis=0, keepdims=True)

            @pl.when(first_tile)
            def _():
                dsp_ref[...] = dsp_ref[...] * (-jax.nn.sigmoid(-lam_ref[...]))

            dp_ref[...] = dxr.astype(BF16)

        @pl.when(j == 1)
        def _():
            dp_ref[...] = dgr_s[...].astype(BF16)

    vec = _bs((1, RW), lambda i, j: (0, 0))
    mat = _bs((RW, RW), lambda i, j: (0, 0))
    tile = lambda cblk: _bs((n, RW), lambda i, j, cblk=cblk: (nt - 1 - i, cblk))
    halo = lambda cblk: _bs((8, RW), lambda i, j, cblk=cblk: (jnp.maximum((nt - 1 - i) * hb - 1, 0), cblk))
    bt = jax.ShapeDtypeStruct((T, RW), BF16)
    v = jax.ShapeDtypeStruct((1, RW), F32)
    return pl.pallas_call(
        body, name="rec_bwd", grid=(nt, 2),
        in_specs=[tile(3), halo(3), tile(4), tile(0), halo(0), tile(1), _bs((8, RW), lambda i, j: (0, 0)), vec,
                  mat, mat, vec, vec, vec, vec, pl.BlockSpec(memory_space=pl.ANY)],
        out_specs=[_bs((n, RW), lambda i, j: (nt - 1 - i, 3 + j)), tile(0), tile(0), tile(0),
                   _bs((8, RW), lambda i, j: (0, 0)), vec, vec, vec, vec, vec],
        out_shape=[jax.ShapeDtypeStruct((T, INW), BF16), bt, bt, bt, jax.ShapeDtypeStruct((8, RW), F32), v, v, v, v, v],
        scratch_shapes=[pltpu.VMEM((n + 8, RW), F32), pltpu.VMEM((n + 8, RW), F32), pltpu.VMEM((n + 8, RW), F32),
                        pltpu.VMEM((8, RW), F32), pltpu.VMEM((n, RW), F32)],
        input_output_aliases={14: 0}, compiler_params=_cp("arbitrary", "arbitrary"),
    )(proj, proj, proj, h, h, dmix, cw, cb, wrg, wig, brg, big, lam, g_rec, dproj)


FC = 1536
TRF = 256


LC = 128


def _taps(x_ref, edge, cols, r):
    if r == 0:
        return edge[pl.ds(6, 8), cols], edge[pl.ds(7, 8), cols], edge[pl.ds(8, 8), cols]
    return x_ref[pl.ds(r - 2, 8), cols], x_ref[pl.ds(r - 1, 8), cols], x_ref[pl.ds(r, 8), cols]


def _ffn_act(up_pre, cw, cb):
    n = TRF
    hb = n // 8

    def body(g_ref, gh_ref, u_ref, uh_ref, wg_ref, wu_ref, bg_ref, bu_ref, o_ref, eg, eu):
        first = pl.program_id(1) == 0
        eg[0:8, :] = jnp.where(first, 0.0, gh_ref[...])
        eg[8:, :] = g_ref[0:8, :]
        eu[0:8, :] = jnp.where(first, 0.0, uh_ref[...])
        eu[8:, :] = u_ref[0:8, :]

        def column(ci, carry):
            cols = pl.ds(pl.multiple_of(ci * LC, LC), LC)
            rows8 = lambda v: jnp.broadcast_to(v, (8, LC))
            wg = [rows8(wg_ref[kk:kk + 1, cols]) for kk in range(3)]
            wu = [rows8(wu_ref[kk:kk + 1, cols]) for kk in range(3)]
            bg, bu = rows8(bg_ref[:, cols]), rows8(bu_ref[:, cols])
            for r in range(0, n, 16):
                res = []
                for rr in (r, r + 8):
                    g0, g1, g2 = _taps(g_ref, eg, cols, rr)
                    u0, u1, u2 = _taps(u_ref, eu, cols, rr)
                    ug = ((bg + g0 * wg[0]) + g1 * wg[1]) + g2 * wg[2]
                    uu = ((bu + u0 * wu[0]) + u1 * wu[1]) + u2 * wu[2]
                    res.append(_gelu(ug) * uu)
                o_ref[pl.ds(r, 16), cols] = jnp.concatenate(res, axis=0).astype(BF16)
            return carry

        lax.fori_loop(0, FC // LC, column, 0)

    main = lambda o: _bs((n, FC), lambda j, i, o=o: (i, 2 * j + o))
    halo = lambda o: _bs((8, FC), lambda j, i, o=o: (jnp.maximum(i * hb - 1, 0), 2 * j + o))
    wsp = lambda o: _bs((None, 8, FC), lambda j, i, o=o: (2 * j + o, 0, 0))
    bsp = lambda o: _bs((1, FC), lambda j, i, o=o: (0, 2 * j + o))
    return pl.pallas_call(
        body, name="ffn_act", grid=(2, T // n),
        in_specs=[main(0), halo(0), main(1), halo(1), wsp(0), wsp(1), bsp(0), bsp(1)],
        out_specs=_bs((n, FC), lambda j, i: (i, j)), out_shape=jax.ShapeDtypeStruct((T, DFF), BF16),
        scratch_shapes=[pltpu.VMEM((16, FC), F32)] * 2, compiler_params=_cp("parallel", "parallel"),
    )(up_pre, up_pre, up_pre, up_pre, cw, cw, cb, cb)


def _ffn_bwd(up_pre, dact, cw, cb):
    n = TRF
    hb = n // 8
    nt = T // n
    m = n + 8

    def body(g_ref, gp_ref, gn_ref, u_ref, up_ref, un_ref, d_ref, dn_ref, wg_ref, wu_ref, bg_ref, bu_ref,
             o_ref, dw_ref, db_ref, eg0, eg1, eu0, eu1, dug_s, duu_s):
        i = pl.program_id(1)
        first, last = i == 0, i == nt - 1

        @pl.when(first)
        def _():
            dw_ref[...] = jnp.zeros_like(dw_ref)
            db_ref[...] = jnp.zeros_like(db_ref)

        eg0[0:8, :] = jnp.where(first, 0.0, gp_ref[...])
        eg0[8:, :] = g_ref[0:8, :]
        eg1[0:8, :] = g_ref[n - 8:, :]
        eg1[8:, :] = gn_ref[...]
        eu0[0:8, :] = jnp.where(first, 0.0, up_ref[...])
        eu0[8:, :] = u_ref[0:8, :]
        eu1[0:8, :] = u_ref[n - 8:, :]
        eu1[8:, :] = un_ref[...]

        def column(ci, carry):
            cols = pl.ds(pl.multiple_of(ci * LC, LC), LC)
            ucols = pl.ds(pl.multiple_of(FC + ci * LC, LC), LC)
            rows8 = lambda v: jnp.broadcast_to(v, (8, LC))
            wg = [rows8(wg_ref[kk:kk + 1, cols]) for kk in range(3)]
            wu = [rows8(wu_ref[kk:kk + 1, cols]) for kk in range(3)]
            bg, bu = rows8(bg_ref[:, cols]), rows8(bu_ref[:, cols])
            zero = jnp.zeros((8, LC), F32)
            acc = [zero] * 8
            for r in range(0, n + 8, 8):
                if r == n:
                    gt = (eg1[pl.ds(6, 8), cols], eg1[pl.ds(7, 8), cols], eg1[pl.ds(8, 8), cols])
                    ut = (eu1[pl.ds(6, 8), cols], eu1[pl.ds(7, 8), cols], eu1[pl.ds(8, 8), cols])
                    dv = jnp.where(last, 0.0, dn_ref[:, cols])
                else:
                    gt, ut = _taps(g_ref, eg0, cols, r), _taps(u_ref, eu0, cols, r)
                    dv = d_ref[pl.ds(r, 8), cols]
                gl, dgl = _gelu_and_grad(((bg + gt[0] * wg[0]) + gt[1] * wg[1]) + gt[2] * wg[2])
                uu = ((bu + ut[0] * wu[0]) + ut[1] * wu[1]) + ut[2] * wu[2]
                dug, duu = dv * uu * dgl, dv * gl
                dug_s[pl.ds(r, 8), :] = dug
                duu_s[pl.ds(r, 8), :] = duu
                if r < n:
                    acc = [acc[0] + dug * gt[0], acc[1] + dug * gt[1], acc[2] + dug * gt[2],
                           acc[3] + duu * ut[0], acc[4] + duu * ut[1], acc[5] + duu * ut[2], acc[6] + dug, acc[7] + duu]
            for r in range(0, n, 16):
                og, ou = [], []
                for rr in (r, r + 8):
                    og.append((dug_s[pl.ds(rr, 8), :] * wg[2] + dug_s[pl.ds(rr + 1, 8), :] * wg[1])
                              + dug_s[pl.ds(rr + 2, 8), :] * wg[0])
                    ou.append((duu_s[pl.ds(rr, 8), :] * wu[2] + duu_s[pl.ds(rr + 1, 8), :] * wu[1])
                              + duu_s[pl.ds(rr + 2, 8), :] * wu[0])
                o_ref[pl.ds(r, 16), cols] = jnp.concatenate(og, axis=0).astype(BF16)
                o_ref[pl.ds(r, 16), ucols] = jnp.concatenate(ou, axis=0).astype(BF16)
            for kk in range(3):
                dw_ref[kk:kk + 1, cols] += jnp.sum(acc[kk], axis=0, keepdims=True)
                dw_ref[kk:kk + 1, ucols] += jnp.sum(acc[3 + kk], axis=0, keepdims=True)
            db_ref[:, cols] += jnp.sum(acc[6], axis=0, keepdims=True)
            db_ref[:, ucols] += jnp.sum(acc[7], axis=0, keepdims=True)
            return carry

        lax.fori_loop(0, FC // LC, column, 0)

    main = lambda o: _bs((n, FC), lambda j, i, o=o: (i, 2 * j + o))
    prev = lambda o: _bs((8, FC), lambda j, i, o=o: (jnp.maximum(i * hb - 1, 0), 2 * j + o))
    nxt = lambda o: _bs((8, FC), lambda j, i, o=o: (jnp.minimum((i + 1) * hb, T // 8 - 1), 2 * j + o))
    wsp = lambda o: _bs((None, 8, FC), lambda j, i, o=o: (2 * j + o, 0, 0))
    bsp = lambda o: _bs((1, FC), lambda j, i, o=o: (0, 2 * j + o))
    return pl.pallas_call(
        body, name="ffn_bwd", grid=(2, nt),
        in_specs=[main(0), prev(0), nxt(0), main(1), prev(1), nxt(1), _bs((n, FC), lambda j, i: (i, j)),
                  _bs((8, FC), lambda j, i: (jnp.minimum((i + 1) * hb, T // 8 - 1), j)), wsp(0), wsp(1), bsp(0), bsp(1)],
        out_specs=[_bs((n, 2 * FC), lambda j, i: (i, j)), _bs((8, 2 * FC), lambda j, i: (0, j)),
                   _bs((1, 2 * FC), lambda j, i: (0, j))],
        out_shape=[jax.ShapeDtypeStruct((T, 2 * DFF), BF16), jax.ShapeDtypeStruct((8, 2 * DFF), F32),
                   jax.ShapeDtypeStruct((1, 2 * DFF), F32)],
        scratch_shapes=[pltpu.VMEM((16, FC), F32)] * 4 + [pltpu.VMEM((m, LC), F32)] * 2,
        compiler_params=_cp("parallel", "arbitrary"),
    )(up_pre, up_pre, up_pre, up_pre, up_pre, up_pre, dact, dact, cw, cw, cb, cb)


def _down_loss(act, w_down, x1, target):
    tm, tn = 512, 512

    def body(a_ref, b_ref, r_ref, t_ref, dy_ref, dyb_ref, l_ref):
        @pl.when((pl.program_id(0) == 0) & (pl.program_id(1) == 0))
        def _():
            l_ref[...] = jnp.zeros_like(l_ref)

        err = (r_ref[...] + _dot(a_ref[...], b_ref[...])) - t_ref[...]
        dy = err * (1.0 / D)
        dy_ref[...] = dy
        dyb_ref[...] = dy.astype(BF16)
        l_ref[...] += jnp.sum(0.5 * (err * err) * (1.0 / D))

    o_spec = _bs((tm, tn), lambda j, i: (i, j))
    return pl.pallas_call(
        body, name="down_loss", grid=(D // tn, T // tm),
        in_specs=[_bs((tm, DFF), lambda j, i: (i, 0)), _bs((DFF, tn), lambda j, i: (0, j)), o_spec, o_spec],
        out_specs=[o_spec, o_spec, _bs((8, 128), lambda j, i: (0, 0))],
        out_shape=[jax.ShapeDtypeStruct((T, D), F32), jax.ShapeDtypeStruct((T, D), BF16),
                   jax.ShapeDtypeStruct((8, 128), F32)],
        compiler_params=_cp("arbitrary", "arbitrary"),
    )(act, w_down, x1, target)


def _block_diag(w):
    eye = jnp.eye(8, dtype=w.dtype)
    return (w[:, :, None, :] * eye[:, None, :, None]).reshape(RW, RW).astype(BF16)


def _diag_blocks(m):
    eye = jnp.eye(8, dtype=m.dtype)
    return (m.reshape(8, HD, 8, HD) * eye[:, None, :, None]).sum(axis=2)


def _local_step(x, pos_col, target, p, exch):
    qg, kg = jnp.tile(p["q_norm_g"], (1, 8)), jnp.tile(p["k_norm_g"], (1, 8))
    wrg, wig = _block_diag(p["w_rg"]), _block_diag(p["w_ig"])
    brg, big = p["b_rg"].reshape(1, RW), p["b_ig"].reshape(1, RW)

    h1 = _rms_fwd("rms1", x, p["g_mix"])
    proj = _mm("mm_in", h1, p["w_in"], "nn", 512, 640, stack=NCHIP, after=exch.start_rest(), a_full=True)
    q, k, cos_t, sin_t = _qk_prep(proj, pos_col, qg, kg)
    attn, lse = _attn_fwd(q, k, proj)
    mix = _attn_norm(attn, p["g_attn_out"])
    mix, hseq = _rec_fwd(proj, mix, p["rec_conv_w"], p["rec_conv_b"], wrg, wig, brg, big, p["lru_lambda"], p["g_rec_out"])
    rest = exch.wait_rest(mix)
    x1 = _mm("mm_out", mix, rest["w_out"], "nn", 512, 512, res=x, a_full=True)
    h2 = _rms_fwd("rms2", x1, p["g_ffn"])
    up_pre = _mm("mm_up", h2, rest["w_up"], "nn", 512, 768, stack=NCHIP, a_full=True)
    act = _ffn_act(up_pre, p["ffn_conv_w"], p["ffn_conv_b"])
    dy, dyb, loss_blk = _down_loss(act, rest["w_down"], x1, target)

    g = {}
    tok = exch.reduce_start("w_down", *_mm("wg_down", act, dyb, "tn", 512, 512, twin_bf16=True))
    dact = _mm("dg_down", dyb, rest["w_down"], "nt", 512, 512, after=tok, a_full=True)
    dup, g["ffn_conv_w"], g["ffn_conv_b"] = _ffn_bwd(up_pre, dact, p["ffn_conv_w"], p["ffn_conv_b"])
    tok = exch.reduce_start("w_up", *_mm("wg_up", h2, dup, "tn", 512, 768, stack=NCHIP, twin_bf16=True, a_full=True))
    dh2 = _mm("dg_up", dup, rest["w_up"], "nt", 512, D, stack=NCHIP, after=tok, b_full=True)
    dx1, dx1b, g["g_ffn"] = _rms_bwd("rms2_bwd", x1, p["g_ffn"], dh2, dy, True)
    tok = exch.reduce_start("w_out", *_mm("wg_out", mix, dx1b, "tn", 512, 512, twin_bf16=True, a_full=True))
    dmix = _mm("dg_out", dx1b, rest["w_out"], "nt", 512, 512, after=tok, a_full=True)
    do, delta, g["g_attn_out"] = _attn_out_bwd(attn, dmix, p["g_attn_out"])
    dq, dk, dv = _attn_bwd(q, k, proj, do, lse, delta)
    dproj, dqg, dkg = _qk_bwd(proj, cos_t, sin_t, qg, kg, dq, dk, dv)
    (dproj, xcb, dprb, dpib, g["rec_conv_w"], g["rec_conv_b"], dbr, dbi, dsp, g["g_rec_out"]) = _rec_bwd(
        proj, hseq, dmix, dproj, p["rec_conv_w"], p["rec_conv_b"], wrg, wig, brg, big, p["lru_lambda"], p["g_rec_out"])
    g["w_rg"] = _diag_blocks(_mm("wg_rg", xcb, dprb, "tn", 512, 512)).reshape(RW, HD)
    g["w_ig"] = _diag_blocks(_mm("wg_ig", xcb, dpib, "tn", 512, 512)).reshape(RW, HD)
    g["b_rg"], g["b_ig"] = dbr.reshape(8, HD), dbi.reshape(8, HD)
    g["lru_lambda"] = dsp
    g["q_norm_g"] = dqg.reshape(8, HD).sum(axis=0, keepdims=True)
    g["k_norm_g"] = dkg.reshape(8, HD).sum(axis=0, keepdims=True)
    tok = exch.reduce_start("w_in", *_mm("wg_in", h1, dproj, "tn", 512, 640, stack=NCHIP, twin_bf16=True, a_full=True))
    dh1 = _mm("dg_in", dproj, p["w_in"], "nt", 512, 512, stack=NCHIP, after=tok)
    grad_x, g["g_mix"] = _rms_bwd("rms1_bwd", x, p["g_mix"], dh1, dx1, False)
    return loss_blk, grad_x, g


ANY = pl.BlockSpec(memory_space=pl.ANY)


def _mesh_pos():
    return lax.axis_index("x"), lax.axis_index("y"), lax.axis_index("c")


def _slot(px, py, perm):
    return 2 * py + px if perm else 2 * px + py


def _other_chips(x, y):
    return [(1 - x, y), (x, 1 - y), (1 - x, 1 - y)]


def _rcopy(src, dst, send, recv, k, to, kr=None):
    return pltpu.make_async_remote_copy(src_ref=src, dst_ref=dst, send_sem=send.at[k],
                                        recv_sem=recv.at[k if kr is None else kr], device_id=to, device_id_type=MESH)


def _cast_bf16(name, w):
    r, c = w.shape
    tr = 128
    def body(w_ref, o_ref):
        o_ref[...] = w_ref[...].astype(BF16)
    return pl.pallas_call(
        body, name=name, grid=(r // tr,), in_specs=[_bs((tr, c), lambda i: (i, 0))],
        out_specs=_bs((tr, c), lambda i: (i, 0)), out_shape=jax.ShapeDtypeStruct((r, c), BF16),
        compiler_params=_cp("parallel"),
    )(w)


def _gather_weights(big, small, slot):
    nb, ns = len(big), len(small)
    perms = [p for _, p in big] + [p for _, p in small]

    def body(*refs):
        ins, outs = refs[:nb + ns], refs[2 * (nb + ns):3 * (nb + ns)]
        send, recv = refs[3 * (nb + ns):]
        x, y, c = _mesh_pos()
        me, sib = (x, y, c), (x, y, 1 - c)
        chips = _other_chips(x, y)
        first = []
        for a in range(nb):
            for j, (px, py) in enumerate(chips):
                first.append(_rcopy(ins[a].at[c], outs[a].at[_slot(x, y, perms[a]), c], send, recv, 3 * a + j, (px, py, c)))
        for t in range(ns):
            a = nb + t
            for j, (px, py) in enumerate(chips):
                first.append(_rcopy(ins[a], outs[a].at[_slot(x, y, perms[a])], send, recv, 6 * nb + 3 * t + j, (px, py, c)))
        for cp in first:
            cp.start()
        passed = []
        for a in range(nb):
            for j, (px, py) in enumerate(chips):
                got = outs[a].at[_slot(px, py, perms[a]), c]
                _rcopy(got, got, send, recv, 3 * a + j, me).wait_recv()
                fwd = _rcopy(got, got, send, recv, 3 * nb + 3 * a + j, sib)
                fwd.start()
                passed.append(fwd)
        for a in range(nb):
            for j, (px, py) in enumerate(chips):
                got = outs[a].at[_slot(px, py, perms[a]), 1 - c]
                _rcopy(got, got, send, recv, 3 * nb + 3 * a + j, me).wait_recv()
        for t in range(ns):
            a = nb + t
            for j, (px, py) in enumerate(chips):
                got = outs[a].at[_slot(px, py, perms[a])]
                _rcopy(got, got, send, recv, 6 * nb + 3 * t + j, me).wait_recv()
        for cp in first + passed:
            cp.wait_send()

    arrs = [a for a, _ in big] + [a for a, _ in small]
    lands = [lax.dynamic_update_slice(lax.empty((NCHIP,) + a.shape, a.dtype), a[None], (slot[p],) + (0,) * a.ndim)
             for a, p in zip(arrs, perms)]
    nsem = 6 * nb + 3 * ns
    return pl.pallas_call(
        body, name="gather_weights", in_specs=[ANY] * (2 * (nb + ns)), out_specs=[ANY] * (nb + ns),
        out_shape=[jax.ShapeDtypeStruct(a.shape, a.dtype) for a in lands],
        input_output_aliases={nb + ns + i: i for i in range(nb + ns)},
        scratch_shapes=[pltpu.SemaphoreType.DMA((nsem,)), pltpu.SemaphoreType.DMA((nsem,))],
    )(*arrs, *lands)


HBM = pl.BlockSpec(memory_space=pltpu.HBM)
SEM = pl.BlockSpec(memory_space=pltpu.SEMAPHORE)
EFFECT = pltpu.SideEffectType.DATAFLOW_SIDE_EFFECTING


def _split_start(name, srcs, lands, plan, nsem):
    ns, nl = len(srcs), len(lands)

    def body(*refs):
        send, recv = refs[ns + nl], refs[ns + nl + 1]
        sends, _ = plan(refs[:ns], refs[ns:ns + nl], send, recv)
        for cp in sends:
            cp.start()
        refs[-1][...] = jnp.zeros((8, 128), F32)

    arrs = list(srcs) + list(lands)
    out = pl.pallas_call(
        body, name=name, in_specs=[HBM] * (ns + nl),
        out_specs=[SEM, SEM] + [HBM] * (ns + nl) + [pl.BlockSpec(memory_space=pltpu.VMEM)],
        out_shape=[pltpu.SemaphoreType.DMA((nsem,)), pltpu.SemaphoreType.DMA((nsem,))]
        + [pltpu.HBM(a.shape, a.dtype) for a in arrs] + [jax.ShapeDtypeStruct((8, 128), F32)],
        input_output_aliases={i: 2 + i for i in range(ns + nl)},
        compiler_params=pltpu.CompilerParams(has_side_effects=EFFECT),
    )(*[pltpu.with_memory_space_constraint(a, pltpu.HBM) for a in arrs])
    return out[0], out[1], out[2:2 + ns], out[2 + ns:2 + ns + nl], out[-1]


def _split_wait(name, send, recv, srcs, lands, plan, after):
    ns, nl = len(srcs), len(lands)

    def body(*refs):
        sends, recvs = plan(refs[:ns], refs[ns:ns + nl], refs[ns + nl], refs[ns + nl + 1])
        for cp in sends:
            cp.wait_send()
        for cp in recvs:
            cp.wait_recv()

    arrs = list(srcs) + list(lands)
    out = pl.pallas_call(
        body, name=name, in_specs=[HBM] * (ns + nl) + [SEM, SEM, ANY], out_specs=[HBM] * (ns + nl),
        out_shape=[pltpu.HBM(a.shape, a.dtype) for a in arrs],
        input_output_aliases={i: i for i in range(ns + nl)},
        compiler_params=pltpu.CompilerParams(has_side_effects=EFFECT),
    )(*arrs, send, recv, after)
    return out[ns:]


def _gather_plan(perms):
    def plan(srcs, lands, send, recv):
        x, y, c = _mesh_pos()
        sends, recvs = [], []
        for a, perm in enumerate(perms):
            for j, (px, py) in enumerate(_other_chips(x, y)):
                for cc in (0, 1):
                    k = 6 * a + 2 * j + cc
                    sends.append(_rcopy(srcs[a].at[c], lands[a].at[_slot(x, y, perm), c], send, recv, k, (px, py, cc),
                                        kr=6 * a + 2 * j + c))
                    got = lands[a].at[_slot(px, py, perm), cc]
                    recvs.append(_rcopy(got, got, send, recv, k, (x, y, c)))
        return sends, recvs
    return plan


def _reduce_plan(perm):
    def plan(srcs, lands, send, recv):
        x, y, c = _mesh_pos()
        src, land = srcs[0], lands[0]
        sends = []
        for j, (px, py) in enumerate(_other_chips(x, y)):
            for hf in (0, 1):
                sends.append(_rcopy(src.at[_slot(px, py, perm), hf], land.at[2 * j + c], send, recv, 2 * j + hf,
                                    (px, py, hf), kr=2 * j + c))
        sends.append(_rcopy(src.at[_slot(x, y, perm), 1 - c], land.at[6], send, recv, 6, (x, y, 1 - c)))
        recvs = [_rcopy(land.at[i], land.at[i], send, recv, i, (x, y, c)) for i in range(7)]
        return sends, recvs
    return plan


def _sibling_share(rs):
    na = len(rs)

    def body(*refs):
        ins, outs, (send, recv) = refs[:na], refs[na:2 * na], refs[2 * na:]
        x, y, c = _mesh_pos()
        cps = [_rcopy(ins[a], outs[a], send, recv, a, (x, y, 1 - c)) for a in range(na)]
        for cp in cps:
            cp.start()
        for cp in cps:
            cp.wait()

    return pl.pallas_call(
        body, name="rs_share", in_specs=[ANY] * na, out_specs=[ANY] * na,
        out_shape=[jax.ShapeDtypeStruct(r.shape, F32) for r in rs],
        scratch_shapes=[pltpu.SemaphoreType.DMA((na,)), pltpu.SemaphoreType.DMA((na,))],
    )(*rs)


def _add_pieces(name, g, got, where):
    _, _, r2, cc = g.shape
    tr = 128

    def body(w_ref, g_ref, r_ref, o_ref):
        del w_ref
        acc = g_ref[...]
        for i in range(7):
            acc = acc + r_ref[i].astype(F32)
        o_ref[...] = acc

    return pl.pallas_call(
        body, name=name,
        grid_spec=pltpu.PrefetchScalarGridSpec(
            num_scalar_prefetch=1, grid=(r2 // tr,),
            in_specs=[_bs((None, None, tr, cc), lambda i, w_ref: (w_ref[0], w_ref[1], i, 0)),
                      _bs((7, tr, cc), lambda i, w_ref: (0, i, 0))],
            out_specs=_bs((tr, cc), lambda i, w_ref: (i, 0))),
        out_shape=jax.ShapeDtypeStruct((r2, cc), F32), compiler_params=_cp("parallel"),
    )(where, g, got)


def _adam_math(w, g, m, v):
    m = ADAM_B1 * m + (1.0 - ADAM_B1) * g
    v = ADAM_B2 * v + (1.0 - ADAM_B2) * (g * g)
    m_hat = m / (1.0 - ADAM_B1 ** ADAM_STEP)
    v_hat = v / (1.0 - ADAM_B2 ** ADAM_STEP)
    return -ADAM_LR * (m_hat / (jnp.sqrt(v_hat) + ADAM_EPS) + ADAM_WD * w), m, v


def _adam_big(name, w, g_mine, g_sib, m, v, c_arr):
    r, cols = w.shape
    tr = 128
    per = r // 2 // tr

    def body(c_ref, w_ref, a_ref, b_ref, m_ref, v_ref, g_ref, d_ref, m2_ref, v2_ref):
        g = jnp.where(pl.program_id(0) == c_ref[0], a_ref[...], b_ref[...])
        g_ref[...] = g
        d_ref[...], m2_ref[...], v2_ref[...] = _adam_math(w_ref[...], g, m_ref[...], v_ref[...])

    spec = _bs((tr, cols), lambda h, i, c_ref: (h * per + i, 0))
    half = _bs((tr, cols), lambda h, i, c_ref: (i, 0))
    out = jax.ShapeDtypeStruct((r, cols), F32)
    return pl.pallas_call(
        body, name=name,
        grid_spec=pltpu.PrefetchScalarGridSpec(
            num_scalar_prefetch=1, grid=(2, per), in_specs=[spec, half, half, spec, spec], out_specs=[spec] * 4),
        out_shape=[out] * 4, compiler_params=_cp("parallel", "parallel"),
    )(c_arr, w, g_mine, g_sib, m, v)


_CLASS_SHAPE = {"a": (8, D), "b": (8, RW), "c": (8, 2 * DFF), "d": (1048, HD)}
_SMALL = (
    ("g_mix", "a", 0, 1, D), ("g_ffn", "a", 1, 1, D),
    ("rec_conv_w", "b", 0, 4, RW), ("rec_conv_b", "b", 4, 1, RW), ("lru_lambda", "b", 5, 1, RW),
    ("g_attn_out", "b", 6, 1, RW), ("g_rec_out", "b", 7, 1, RW),
    ("ffn_conv_w", "c", 0, 3, 2 * DFF), ("ffn_conv_b", "c", 3, 1, 2 * DFF),
    ("w_rg", "d", 0, RW, HD), ("w_ig", "d", RW, RW, HD), ("b_rg", "d", 2 * RW, 8, HD), ("b_ig", "d", 2 * RW + 8, 8, HD),
    ("q_norm_g", "d", 2 * RW + 16, 1, HD), ("k_norm_g", "d", 2 * RW + 17, 1, HD),
)
_LOSS_ROW = 2
_CLASSES = ("a", "b", "c", "d")
_CLASS_OWNER = {"a": 0, "b": 0, "c": 0, "d": 1}


def _small_allreduce(g, loss_blk):
    names = [s[0] for s in _SMALL]
    nin = len(names) + 1

    def body(*refs):
        ins = dict(zip(names, refs[:len(names)]))
        loss_ref = refs[len(names)]
        outs = dict(zip(_CLASSES, refs[nin:nin + 4]))
        pair = dict(zip(_CLASSES, refs[nin + 4:nin + 8]))
        quad = dict(zip(_CLASSES, refs[nin + 8:nin + 12]))
        send, recv = refs[nin + 12:]
        x, y, c = _mesh_pos()
        chip = 2 * x + y
        pair["a"][c] = jnp.zeros(_CLASS_SHAPE["a"], F32)
        pair["b"][c] = ins["rec_conv_w"][...]
        pair["c"][c] = ins["ffn_conv_w"][...]
        pair["d"][c, 2 * RW + 16:, :] = jnp.zeros((8, HD), F32)
        for name, k, r0, nr, _ in _SMALL:
            if name in ("rec_conv_w", "ffn_conv_w"):
                continue
            pair[k][c, r0:r0 + nr, :] = ins[name][...]
        pair["a"][c, _LOSS_ROW:_LOSS_ROW + 1, :] = jnp.broadcast_to(loss_ref[0:1, 0:1], (1, D))
        cps = [_rcopy(pair[k].at[c], pair[k].at[c], send, recv, ki, (x, y, 1 - c)) for ki, k in enumerate(_CLASSES)]
        for cp in cps:
            cp.start()
        for ki, k in enumerate(_CLASSES):
            _rcopy(pair[k].at[1 - c], pair[k].at[1 - c], send, recv, ki, (x, y, c)).wait_recv()
            quad[k][chip] = pair[k][0] + pair[k][1]
        for cp in cps:
            cp.wait_send()
        for ki, k in enumerate(_CLASSES):
            owner = _CLASS_OWNER[k]

            @pl.when(c == owner)
            def _(ki=ki, k=k):
                cps2 = [_rcopy(quad[k].at[chip], quad[k].at[chip], send, recv, 4 + 3 * ki + j, (px, py, c))
                        for j, (px, py) in enumerate(_other_chips(x, y))]
                for cp in cps2:
                    cp.start()
                for j, (px, py) in enumerate(_other_chips(x, y)):
                    got = quad[k].at[2 * px + py]
                    _rcopy(got, got, send, recv, 4 + 3 * ki + j, (x, y, c)).wait_recv()
                outs[k][...] = ((quad[k][0] + quad[k][1]) + quad[k][2]) + quad[k][3]
                share = _rcopy(outs[k], outs[k], send, recv, 16 + ki, (x, y, 1 - c))
                share.start()
                for cp in cps2:
                    cp.wait_send()
                share.wait_send()

        for ki, k in enumerate(_CLASSES):
            @pl.when(c != _CLASS_OWNER[k])
            def _(ki=ki, k=k):
                _rcopy(outs[k], outs[k], send, recv, 16 + ki, (x, y, c)).wait_recv()

    vm = pl.BlockSpec(memory_space=pltpu.VMEM)
    return pl.pallas_call(
        body, name="small_allreduce", in_specs=[vm] * nin, out_specs=[vm] * 4,
        out_shape=[jax.ShapeDtypeStruct(_CLASS_SHAPE[k], F32) for k in _CLASSES],
        scratch_shapes=[pltpu.VMEM((2,) + _CLASS_SHAPE[k], F32) for k in _CLASSES]
        + [pltpu.VMEM((NCHIP,) + _CLASS_SHAPE[k], F32) for k in _CLASSES]
        + [pltpu.SemaphoreType.DMA((20,)), pltpu.SemaphoreType.DMA((20,))],
        compiler_params=pltpu.CompilerParams(vmem_limit_bytes=VMEM_LIMIT),
    )(*[g[n] for n in names], loss_blk)


def _adam_small(red, w, m, v):
    names = [s[0] for s in _SMALL]
    n = len(names)

    def body(*refs):
        red_refs = dict(zip(_CLASSES, refs[:4]))
        w_refs, m_refs, v_refs = refs[4:4 + n], refs[4 + n:4 + 2 * n], refs[4 + 2 * n:4 + 3 * n]
        loss_ref = refs[4 + 3 * n]
        out_refs = refs[5 + 3 * n:]
        x, y, _ = _mesh_pos()
        chip = 2 * x + y
        loss_ref[...] = jnp.broadcast_to(red_refs["a"][_LOSS_ROW:_LOSS_ROW + 1, 0:1], loss_ref.shape)
        for pi, (name, k, r0, nr, width) in enumerate(_SMALL):
            gfull = red_refs[k][r0:r0 + nr, :]
            if name == "rec_conv_w":
                parts = [gfull[:, 128 * s:128 * (s + 1)] for s in range(NCHIP)]
                g = jnp.where(chip == 0, parts[0], jnp.where(chip == 1, parts[1], jnp.where(chip == 2, parts[2], parts[3])))
            elif name == "ffn_conv_w":
                parts = [gfull[:, FC * s:FC * (s + 1)] for s in range(NCHIP)]
                g = jnp.where(chip == 0, parts[0], jnp.where(chip == 1, parts[2], jnp.where(chip == 2, parts[1], parts[3])))
            elif name == "ffn_conv_b":
                g = jnp.concatenate([gfull[:, FC * s:FC * (s + 1)] for s in (0, 2, 1, 3)], axis=1)
            else:
                g = gfull
            d, m2, v2 = _adam_math(w_refs[pi][...], g, m_refs[pi][...], v_refs[pi][...])
            o = out_refs[4 * pi:4 * pi + 4]
            o[0][...], o[1][...], o[2][...], o[3][...] = g, d, m2, v2

    vm = pl.BlockSpec(memory_space=pltpu.VMEM)
    outs = [jax.ShapeDtypeStruct((1, 128), F32)]
    for name in names:
        outs += [jax.ShapeDtypeStruct(w[name].shape, F32)] * 4
    res = pl.pallas_call(
        body, name="adam_small", in_specs=[vm] * (4 + 3 * n), out_specs=[vm] * len(outs), out_shape=outs,
        compiler_params=pltpu.CompilerParams(vmem_limit_bytes=VMEM_LIMIT),
    )(*red, *[w[k] for k in names], *[m[k] for k in names], *[v[k] for k in names])
    return res[0], {name: res[1 + 4 * i:5 + 4 * i] for i, name in enumerate(names)}


_WEIGHTS = ("g_mix", "w_in", "q_norm_g", "k_norm_g", "rec_conv_w", "rec_conv_b", "w_rg", "b_rg", "w_ig", "b_ig",
            "lru_lambda", "g_attn_out", "g_rec_out", "w_out", "g_ffn", "w_up", "ffn_conv_w", "ffn_conv_b", "w_down")
_BIG = ("w_in", "w_out", "w_up", "w_down")
_BIG_PERM = {"w_in": False, "w_out": False, "w_up": True, "w_down": False}
_SMALL_2D = {"w_rg": (RW, HD), "w_ig": (RW, HD), "b_rg": (8, HD), "b_ig": (8, HD), "rec_conv_w": (4, 128),
             "ffn_conv_w": (3, FC)}


def _halves(a):
    r, c = a.shape
    return a.reshape(2, r // 2, c)


def kernel(x, positions, g_mix, w_in, q_norm_g, k_norm_g, rec_conv_w, rec_conv_b, w_rg, b_rg, w_ig, b_ig, lru_lambda, g_attn_out, g_rec_out, w_out, g_ffn, w_up, ffn_conv_w, ffn_conv_b, w_down, loss_target, m_g_mix, m_w_in, m_q_norm_g, m_k_norm_g, m_rec_conv_w, m_rec_conv_b, m_w_rg, m_b_rg, m_w_ig, m_b_ig, m_lru_lambda, m_g_attn_out, m_g_rec_out, m_w_out, m_g_ffn, m_w_up, m_ffn_conv_w, m_ffn_conv_b, m_w_down, v_g_mix, v_w_in, v_q_norm_g, v_k_norm_g, v_rec_conv_w, v_rec_conv_b, v_w_rg, v_b_rg, v_w_ig, v_b_ig, v_lru_lambda, v_g_attn_out, v_g_rec_out, v_w_out, v_g_ffn, v_w_up, v_ffn_conv_w, v_ffn_conv_b, v_w_down):
    given = dict(g_mix=g_mix, w_in=w_in, q_norm_g=q_norm_g, k_norm_g=k_norm_g, rec_conv_w=rec_conv_w, rec_conv_b=rec_conv_b, w_rg=w_rg, b_rg=b_rg, w_ig=w_ig, b_ig=b_ig, lru_lambda=lru_lambda, g_attn_out=g_attn_out, g_rec_out=g_rec_out, w_out=w_out, g_ffn=g_ffn, w_up=w_up, ffn_conv_w=ffn_conv_w, ffn_conv_b=ffn_conv_b, w_down=w_down)
    given_m = dict(g_mix=m_g_mix, w_in=m_w_in, q_norm_g=m_q_norm_g, k_norm_g=m_k_norm_g, rec_conv_w=m_rec_conv_w, rec_conv_b=m_rec_conv_b, w_rg=m_w_rg, b_rg=m_b_rg, w_ig=m_w_ig, b_ig=m_b_ig, lru_lambda=m_lru_lambda, g_attn_out=m_g_attn_out, g_rec_out=m_g_rec_out, w_out=m_w_out, g_ffn=m_g_ffn, w_up=m_w_up, ffn_conv_w=m_ffn_conv_w, ffn_conv_b=m_ffn_conv_b, w_down=m_w_down)
    given_v = dict(g_mix=v_g_mix, w_in=v_w_in, q_norm_g=v_q_norm_g, k_norm_g=v_k_norm_g, rec_conv_w=v_rec_conv_w, rec_conv_b=v_rec_conv_b, w_rg=v_w_rg, b_rg=v_b_rg, w_ig=v_w_ig, b_ig=v_b_ig, lru_lambda=v_lru_lambda, g_attn_out=v_g_attn_out, g_rec_out=v_g_rec_out, w_out=v_w_out, g_ffn=v_g_ffn, w_up=v_w_up, ffn_conv_w=v_ffn_conv_w, ffn_conv_b=v_ffn_conv_b, w_down=v_w_down)
    shapes = {n: a.shape for n, a in given.items()}

    def two_d(n, a):
        a = a[0]
        return a.reshape(_SMALL_2D[n]) if n in _SMALL_2D else (a if a.ndim == 2 else a[None])

    w = {n: two_d(n, a) for n, a in given.items()}
    m = {n: two_d(n, a) for n, a in given_m.items()}
    v = {n: two_d(n, a) for n, a in given_v.items()}
    cc = lax.axis_index("c").astype(jnp.int32)
    cx, cy = lax.axis_index("x").astype(jnp.int32), lax.axis_index("y").astype(jnp.int32)
    slot = {False: 2 * cx + cy, True: 2 * cy + cx}

    shards = {n: _halves(_cast_bf16(f"cast_{n}", w[n])) for n in _BIG}
    small = [(jnp.pad(w["ffn_conv_w"], ((0, 5), (0, 0))), True), (jnp.pad(w["rec_conv_w"], ((0, 4), (0, 0))), False)]
    f_in, f_fcw, f_rcw = _gather_weights([(shards["w_in"], False)], small, slot)
    p = {n: w[n] for n in ("g_mix", "g_ffn", "q_norm_g", "k_norm_g", "rec_conv_b", "lru_lambda", "g_attn_out", "g_rec_out")}
    p.update(w_rg=w["w_rg"].reshape(8, HD, HD), w_ig=w["w_ig"].reshape(8, HD, HD), b_rg=w["b_rg"], b_ig=w["b_ig"],
             w_in=f_in.reshape(NCHIP, D, INW // NCHIP), ffn_conv_w=f_fcw,
             ffn_conv_b=jnp.concatenate([w["ffn_conv_b"][:, FC * s:FC * (s + 1)] for s in (0, 2, 1, 3)], axis=1),
             rec_conv_w=f_rcw.transpose(1, 0, 2).reshape(8, RW))

    class Exchange:
        rest = ("w_out", "w_up", "w_down")
        order = []
        flight = {}

        def start_rest(self):
            srcs = [shards[n] for n in self.rest]
            lands = [lax.dynamic_update_slice(lax.empty((NCHIP,) + s.shape, BF16), s[None], (slot[_BIG_PERM[n]], 0, 0, 0))
                     for n, s in zip(self.rest, srcs)]
            plan = _gather_plan([_BIG_PERM[n] for n in self.rest])
            send, recv, srcs, lands, token = _split_start("gather_rest_start", srcs, lands, plan, 6 * len(srcs))
            self.flight["rest"] = (send, recv, srcs, lands, plan)
            return (token,)

        def wait_rest(self, after):
            send, recv, srcs, lands, plan = self.flight.pop("rest")
            f_out, f_up, f_down = _split_wait("gather_rest_wait", send, recv, srcs, lands, plan, after)
            return dict(w_out=f_out.reshape(D, D), w_up=f_up.reshape(NCHIP, D, FC), w_down=f_down.reshape(DFF, D))

        def reduce_start(self, name, g32, g16):
            r2, cols = shards[name].shape[1:]
            plan = _reduce_plan(_BIG_PERM[name])
            send, recv, srcs, lands, token = _split_start(
                f"reduce_{name}_start", [g16.reshape(NCHIP, 2, r2, cols)], [lax.empty((7, r2, cols), BF16)], plan, 7)
            self.flight[name] = (send, recv, srcs, lands, plan, g32.reshape(NCHIP, 2, r2, cols))
            self.order.append(name)
            return (token,)

        def finish(self, after):
            mine = {}
            for name in self.order:
                send, recv, srcs, lands, plan, g32 = self.flight.pop(name)
                (got,) = _split_wait(f"reduce_{name}_wait", send, recv, srcs, lands, plan, after)
                where = jnp.stack([slot[_BIG_PERM[name]], cc])
                mine[name] = after = _add_pieces(f"reduce_{name}_add", g32, got, where)
            theirs = dict(zip(_BIG, _sibling_share([mine[n] for n in _BIG])))
            return mine, theirs

    exch = Exchange()

    loss_blk, grad_x, g = _local_step(x[0], positions.reshape(T, 1), loss_target[0], p, exch)

    out_g, out_d, out_m, out_v = {}, {}, {}, {}
    red = _small_allreduce(g, loss_blk)
    loss_row, small_out = _adam_small(red, w, m, v)
    for n, (gn, dn, mn, vn) in small_out.items():
        out_g[n], out_d[n], out_m[n], out_v[n] = gn, dn, mn, vn

    mine, theirs = exch.finish(red[0])
    for n in _BIG:
        out_g[n], out_d[n], out_m[n], out_v[n] = _adam_big(f"adam_{n}", w[n], mine[n], theirs[n], m[n], v[n], cc.reshape(1))

    outs = [loss_row[0, 0], grad_x[None]]
    for group in (out_g, out_d, out_m, out_v):
        outs += [group[n].reshape(shapes[n]) for n in _WEIGHTS]
    return tuple(outs)
```

```python
import math

import jax
import jax.numpy as jnp
import numpy as np
from jax import lax
from jax.experimental import pallas as pl
from jax.experimental.pallas import tpu as pltpu

F32 = jnp.float32
BF16 = jnp.bfloat16

T = 4096
D = 1024
HD = 64
AW = 512
RW = 512
INW = 2560
DFF = 3072
NCHIP = 4
EPS = 1e-6
NEG = -1e30
LRU_C = 8.0
ROPE_THETA = 10000.0
BLK = 128
DILATIONS = (1, 4, 16)
ADAM_LR, ADAM_B1, ADAM_B2, ADAM_EPS, ADAM_WD, ADAM_STEP = 0.001, 0.9, 0.999, 1e-08, 0.01, 10
VMEM_LIMIT = 56 * 1024 * 1024
MESH = pl.DeviceIdType.MESH

NN = (((1,), (0,)), ((), ()))
NT = (((1,), (1,)), ((), ()))
TN = (((0,), (0,)), ((), ()))


def _cp(*sem):
    return pltpu.CompilerParams(dimension_semantics=sem, vmem_limit_bytes=VMEM_LIMIT)


def _bs(shape, fn):
    return pl.BlockSpec(shape, fn)


def _dot(a, b, dims=NN):
    return lax.dot_general(a, b, dims, preferred_element_type=F32)


_GC = math.sqrt(2.0 / math.pi)


def _gelu(x):
    return x * (0.5 + 0.5 * jnp.tanh(x * (_GC + (_GC * 0.044715) * (x * x))))


def _gelu_and_grad(x):
    x2 = x * x
    th = jnp.tanh(x * (_GC + (_GC * 0.044715) * x2))
    cdf = 0.5 + 0.5 * th
    dg = cdf + (x * (1.0 - th * th)) * ((0.5 * _GC) + (1.5 * 0.044715 * _GC) * x2)
    return x * cdf, dg


def _softplus(x):
    e = jnp.exp(-jnp.abs(x))
    u = 1.0 + e
    l1p = jnp.where(u == 1.0, e, jnp.log(u) * (e / (u - 1.0)))
    return jnp.maximum(x, 0.0) + l1p


def _segsum(z, e_bf16):
    hi = z.astype(BF16)
    lo = (z - hi.astype(F32)).astype(BF16)
    parts = []
    for c0 in range(0, z.shape[1], 128):
        parts.append(_dot(hi[:, c0:c0 + 128], e_bf16) + _dot(lo[:, c0:c0 + 128], e_bf16))
    return jnp.concatenate(parts, axis=1)


def _mm(name, a, b, mode, tm, tn, out_dtype=F32, res=None, stack=0, twin_bf16=False, after=(), a_full=False,
        b_full=False):
    if mode == "nn":
        (m, k), n = a.shape, (b.shape[1] if not stack else stack * b.shape[2])
        a_spec = _bs((tm, k), lambda j, i: (i, 0))
        if stack:
            per = b.shape[2] // tn
            b_spec = _bs((None, k, tn), lambda j, i: (j // per, 0, j % per))
        else:
            b_spec = _bs((k, tn), lambda j, i: (0, j))
    elif mode == "nt":
        (m, k), n = a.shape, (b.shape[0] if not stack else b.shape[1])
        a_spec = _bs((tm, k), lambda j, i: (i, 0))
        b_spec = _bs((stack, tn, k // stack), lambda j, i: (0, j, 0)) if stack else _bs((tn, k), lambda j, i: (j, 0))
    else:
        (k, m), n = a.shape, b.shape[1]
        a_spec, b_spec = _bs((k, tm), lambda j, i: (0, i)), _bs((k, tn), lambda j, i: (0, j))
    assert m % tm == 0 and n % tn == 0
    o_spec = _bs((tm, tn), lambda j, i: (i, j))
    o_shape = (m, n)
    if mode == "tn" and stack:
        per = n // stack // tn
        o_spec = _bs((None, tm, tn), lambda j, i: (j // per, i, j % per))
        o_shape = (stack, m, n // stack)
    dims = {"nn": NN, "nt": NT, "tn": TN}[mode]
    once = pl.Buffered(1)
    if a_full:
        a_spec = pl.BlockSpec(a.shape, lambda j, i: (0, 0), pipeline_mode=once)
    if b_full:
        assert n == tn
        b_spec = pl.BlockSpec(b_spec.block_shape, b_spec.index_map, pipeline_mode=once)

    def product(a_ref, b_ref):
        if a_full:
            mine = pl.ds(pl.multiple_of(pl.program_id(1) * tm, tm), tm)
            take = (lambda cols: a_ref[:, mine]) if mode == "tn" else (lambda cols: a_ref[mine, cols])
        else:
            take = lambda cols: a_ref[:, cols]
        if mode == "nt" and stack:
            cs = k // stack
            acc = _dot(take(pl.ds(0, cs)), b_ref[0], NT)
            for s in range(1, stack):
                acc = acc + _dot(take(pl.ds(s * cs, cs)), b_ref[s], NT)
            return acc
        return _dot(take(slice(None)), b_ref[...], dims)

    nres = 0 if res is None else 1

    def body(a_ref, b_ref, *rest):
        acc = product(a_ref, b_ref)
        if nres:
            acc = rest[0][...] + acc
        outs = rest[nres + len(after):]
        outs[0][...] = acc.astype(out_dtype)
        if twin_bf16:
            outs[1][...] = acc.astype(BF16)

    ins = (a, b) + ((res,) if nres else ()) + tuple(after)
    specs = [a_spec, b_spec] + ([o_spec] if nres else []) + [pl.BlockSpec(memory_space=pl.ANY)] * len(after)
    shapes = [jax.ShapeDtypeStruct(o_shape, out_dtype)] + ([jax.ShapeDtypeStruct(o_shape, BF16)] if twin_bf16 else [])
    out = pl.pallas_call(
        body, name=name, grid=(n // tn, m // tm), in_specs=specs, out_specs=[o_spec] * len(shapes),
        out_shape=shapes, compiler_params=_cp("parallel", "parallel"),
    )(*ins)
    return tuple(out) if twin_bf16 else out[0]


def _rms_fwd(name, x, g):
    tr = 512

    def body(x_ref, g_ref, o_ref):
        xv = x_ref[...]
        r = lax.rsqrt(jnp.mean(xv * xv, axis=-1, keepdims=True) + EPS)
        o_ref[...] = ((xv * r) * g_ref[...]).astype(BF16)

    return pl.pallas_call(
        body, name=name, grid=(T // tr,), in_specs=[_bs((tr, D), lambda i: (i, 0)), _bs((1, D), lambda i: (0, 0))],
        out_specs=_bs((tr, D), lambda i: (i, 0)), out_shape=jax.ShapeDtypeStruct((T, D), BF16),
        compiler_params=_cp("parallel"),
    )(x, g)


def _rms_bwd(name, x, g, dy, dres, want_bf16, after=()):
    tr = 256
    halves = dy.ndim == 3

    def body(x_ref, g_ref, dy_ref, dr_ref, *rest):
        rest = rest[len(after):]
        dx_ref, rest = rest[0], rest[1:]
        dg_ref = rest[-1]
        xv = x_ref[...]
        dyv = dy_ref[0] + dy_ref[1] if halves else dy_ref[...]
        r = lax.rsqrt(jnp.mean(xv * xv, axis=-1, keepdims=True) + EPS)
        gdy = g_ref[...] * dyv
        dx = r * gdy - xv * ((r * r * r) * jnp.mean(xv * gdy, axis=-1, keepdims=True)) + dr_ref[...]
        dx_ref[...] = dx
        if want_bf16:
            rest[0][...] = dx.astype(BF16)

        @pl.when(pl.program_id(0) == 0)
        def _():
            dg_ref[...] = jnp.zeros_like(dg_ref)

        dg_ref[...] += jnp.sum(dyv * (xv * r), axis=0, keepdims=True)

    row = _bs((tr, D), lambda i: (i, 0))
    vec = _bs((1, D), lambda i: (0, 0))
    outs = [jax.ShapeDtypeStruct((T, D), F32)] + ([jax.ShapeDtypeStruct((T, D), BF16)] if want_bf16 else [])
    dy_spec = _bs((2, tr, D), lambda i: (0, i, 0)) if halves else row
    return pl.pallas_call(
        body, name=name, grid=(T // tr,),
        in_specs=[row, vec, dy_spec, row] + [pl.BlockSpec(memory_space=pl.ANY)] * len(after),
        out_specs=[row] * len(outs) + [vec], out_shape=outs + [jax.ShapeDtypeStruct((1, D), F32)],
        compiler_params=_cp("arbitrary"),
    )(x, g, dy, dres, *after)


def _head_ones():
    idx = np.arange(128) // HD
    return jnp.asarray((idx[:, None] == idx[None, :]).astype(np.float32), dtype=BF16)


def _freq_row():
    half = HD // 2
    inv = ROPE_THETA ** (-(np.arange(half, dtype=np.float64)) / half)
    return jnp.asarray(np.tile(inv, 4)[None, :], dtype=F32)


def _rot_tables(cos128, sin128):
    c = jnp.tile(cos128, (1, 4))
    s = jnp.tile(sin128, (1, 4))
    lane = lax.broadcasted_iota(jnp.int32, (1, AW), 1)
    first = (lane & 32) == 0
    return c, jnp.where(first, -s, s), first


def _swap_halves(y, first):
    return jnp.where(first, pltpu.roll(y, AW - 32, 1), pltpu.roll(y, 32, 1))


def _qk_prep(proj, pos_col, qg, kg):
    tr = 512

    def body(q_ref, k_ref, pos_ref, f_ref, qg_ref, kg_ref, e_ref, qo_ref, ko_ref, cos_ref, sin_ref):
        ang = pos_ref[...].astype(F32) * f_ref[...]
        cos_ref[...] = jnp.cos(ang)
        sin_ref[...] = jnp.sin(ang)
        c, s_signed, first = _rot_tables(cos_ref[...], sin_ref[...])
        e = e_ref[...]

        def norm_rot(xv, g, scale):
            r = lax.rsqrt(_segsum(xv * xv, e) * (1.0 / HD) + EPS)
            y = (xv * r) * g
            return (y * c + _swap_halves(y, first) * s_signed) * scale

        qo_ref[...] = norm_rot(q_ref[...], qg_ref[...], HD ** -0.5)
        ko_ref[...] = norm_rot(k_ref[...], kg_ref[...], 1.0)

    col = lambda j: _bs((tr, AW), lambda i, j=j: (i, j))
    vec = _bs((1, AW), lambda i: (0, 0))
    out = jax.ShapeDtypeStruct((T, AW), F32)
    tab = jax.ShapeDtypeStruct((T, 128), F32)
    tspec = _bs((tr, 128), lambda i: (i, 0))
    return pl.pallas_call(
        body, name="qk_prep", grid=(T // tr,),
        in_specs=[col(0), col(1), _bs((tr, 1), lambda i: (i, 0)), _bs((1, 128), lambda i: (0, 0)), vec, vec,
                  _bs((128, 128), lambda i: (0, 0))],
        out_specs=[col(0)] * 2 + [tspec] * 2, out_shape=[out, out, tab, tab], compiler_params=_cp("parallel"),
    )(proj, proj, pos_col, _freq_row(), qg, kg, _head_ones())


def _qk_bwd(proj, cos_t, sin_t, qg, kg, dq, dk, dv):
    tr = 256

    def body(q_ref, k_ref, cos_ref, sin_ref, qg_ref, kg_ref, e_ref, dq_ref, dk_ref, dv_ref, o_ref, dqg_ref, dkg_ref):
        i, j = pl.program_id(0), pl.program_id(1)

        @pl.when((i == 0) & (j == 0))
        def _():
            dqg_ref[...] = jnp.zeros_like(dqg_ref)
            dkg_ref[...] = jnp.zeros_like(dkg_ref)

        def norm_rot_bwd(x_ref, g_ref, dg_ref, d_ref, scale):
            c, s_signed, first = _rot_tables(cos_ref[...], sin_ref[...])
            e = e_ref[...]
            dout = d_ref[...] * scale
            dy = dout * c + _swap_halves(dout * s_signed, first)
            xv, g = x_ref[...], g_ref[...]
            r = lax.rsqrt(_segsum(xv * xv, e) * (1.0 / HD) + EPS)
            gdy = g * dy
            dx = r * gdy - xv * ((r * r * r) * (_segsum(xv * gdy, e) * (1.0 / HD)))
            o_ref[...] = dx.astype(BF16)
            dg_ref[...] += jnp.sum(dy * (xv * r), axis=0, keepdims=True)

        @pl.when(j == 0)
        def _():
            norm_rot_bwd(q_ref, qg_ref, dqg_ref, dq_ref, HD ** -0.5)

        @pl.when(j == 1)
        def _():
            norm_rot_bwd(k_ref, kg_ref, dkg_ref, dk_ref, 1.0)

        @pl.when(j == 2)
        def _():
            o_ref[...] = dv_ref[...].astype(BF16)

    col = lambda jj: _bs((tr, AW), lambda i, j, jj=jj: (i, jj))
    vec = _bs((1, AW), lambda i, j: (0, 0))
    piece = _bs((tr, AW), lambda i, j: (i, 0))
    return pl.pallas_call(
        body, name="qk_bwd", grid=(T // tr, 3),
        in_specs=[col(0), col(1), _bs((tr, 128), lambda i, j: (i, 0)), _bs((tr, 128), lambda i, j: (i, 0)), vec, vec,
                  _bs((128, 128), lambda i, j: (0, 0))] + [piece] * 3,
        out_specs=[_bs((tr, AW), lambda i, j: (i, j)), vec, vec],
        out_shape=[jax.ShapeDtypeStruct((T, INW), BF16), jax.ShapeDtypeStruct((1, AW), F32),
                   jax.ShapeDtypeStruct((1, AW), F32)],
        compiler_params=_cp("arbitrary", "arbitrary"),
    )(proj, proj, cos_t, sin_t, qg, kg, _head_ones(), dq, dk, dv)


RG = 256
QC = 64


def _stacked_band_mask(rows=2 * BLK, q0=0):
    qi = (lax.broadcasted_iota(jnp.int32, (rows, 2 * BLK), 0) + q0) & (BLK - 1)
    kj = lax.broadcasted_iota(jnp.int32, (rows, 2 * BLK), 1)
    rel = qi - kj + BLK
    return (rel >= 0) & (rel <= BLK), lax.broadcasted_iota(jnp.int32, (1, 2 * BLK), 1) >= BLK


def _natural_rows(r0, n_rows, d):
    if d == 1:
        return pl.ds(r0, n_rows)
    ln = T // d
    return pl.ds(r0 // ln + d * (r0 % ln), n_rows, stride=d)


def _regroup_into(dst, src_ref, d, pad, cast=True):
    def step(j, carry):
        r0 = pl.multiple_of(j * RG, RG)
        val = src_ref[_natural_rows(r0, RG, d), :]
        dst[pl.ds(pad + r0, RG), :] = val.astype(dst.dtype) if cast else val
        return carry
    lax.fori_loop(0, T // RG, step, 0)


def _stack_heads(x, h0):
    zero = jnp.zeros_like(x)
    return jnp.concatenate([jnp.where(h0, x, zero), jnp.where(h0, zero, x)], axis=0)


def _attn_fwd(q, k, proj):
    nblk = T // BLK

    def body(q_ref, k_ref, v_ref, a_ref, lse_ref, qs, ks, vs, o0, o1, o2, l0, l1, l2, sb0, sb1):
        band, cur_half = _stacked_band_mask()
        h0 = lax.broadcasted_iota(jnp.int32, (1, 128), 1) < HD
        ks[0:BLK, :] = jnp.zeros((BLK, 128), BF16)
        vs[0:BLK, :] = jnp.zeros((BLK, 128), BF16)
        for d, o_s, l_s in zip(DILATIONS, (o0, o1, o2), (l0, l1, l2)):
            nb = T // d // BLK
            _regroup_into(qs, q_ref, d, 0)
            _regroup_into(ks, k_ref, d, BLK)
            _regroup_into(vs, v_ref, d, BLK)

            def scores(b):
                r0 = pl.multiple_of(b * BLK, BLK)
                return _dot(_stack_heads(qs[pl.ds(r0, BLK), :], h0), ks[pl.ds(r0, 2 * BLK), :], NT)

            def finish(b, s_raw, d=d, nb=nb, o_s=o_s, l_s=l_s):
                r0 = pl.multiple_of(b * BLK, BLK)
                mask = band & (cur_half | ((b & (nb - 1)) > 0))
                s = jnp.where(mask, s_raw, NEG)
                m = jnp.max(s, axis=1, keepdims=True)
                p = jnp.exp(s - m)
                l = jnp.sum(p, axis=1, keepdims=True)
                o = _dot(p.astype(BF16), vs[pl.ds(r0, 2 * BLK), :]) / l
                lse = m + jnp.log(l)
                rows = _natural_rows(r0, BLK, d)
                o_s[rows, :] = jnp.where(h0, o[0:BLK, :], o[BLK:, :])
                l_s[rows, :] = jnp.where(h0, lse[0:BLK, :], lse[BLK:, :])

            sb0[...] = scores(0)

            def step(i, carry):
                b = 2 * i
                sb1[...] = scores(b + 1)
                finish(b, sb0[...])
                sb0[...] = scores(jnp.minimum(b + 2, nblk - 1))
                finish(b + 1, sb1[...])
                return carry

            lax.fori_loop(0, nblk // 2, step, 0)

        def merge(i, carry):
            r = pl.ds(pl.multiple_of(i * RG, RG), RG)
            la, lb, lc = l0[r, :], l1[r, :], l2[r, :]
            m = jnp.maximum(jnp.maximum(la, lb), lc)
            ea, eb, ec = jnp.exp(la - m), jnp.exp(lb - m), jnp.exp(lc - m)
            z = (ea + eb) + ec
            a_ref[r, :] = ((ea * o0[r, :] + eb * o1[r, :]) + ec * o2[r, :]) / z
            lse_ref[r, :] = m + jnp.log(z)
            return carry

        lax.fori_loop(0, T // RG, merge, 0)

    spec = lambda cb: _bs((T, 128), lambda p, cb=cb: (0, cb + p))
    out = jax.ShapeDtypeStruct((T, AW), F32)
    return pl.pallas_call(
        body, name="attn_fwd", grid=(AW // 128,), in_specs=[spec(0), spec(0), spec(8)], out_specs=[spec(0)] * 2,
        out_shape=[out] * 2,
        scratch_shapes=[pltpu.VMEM((T, 128), BF16), pltpu.VMEM((T + BLK, 128), BF16), pltpu.VMEM((T + BLK, 128), BF16)]
        + [pltpu.VMEM((T, 128), F32)] * 6 + [pltpu.VMEM((2 * BLK, 2 * BLK), F32)] * 2,
        compiler_params=_cp("parallel"),
    )(q, k, proj)


def _attn_bwd(q, k, proj, do, lse, delta):
    nblk = T // BLK

    def body(q_ref, k_ref, v_ref, do_ref, l_ref, dl_ref, dq_ref, dk_ref, dv_ref, qs, dos, ks, vs, ls, dls, dks, dvs,
             sa0, sa1, da0, da1):
        band, cur_half = _stacked_band_mask()
        h0 = lax.broadcasted_iota(jnp.int32, (1, 128), 1) < HD
        ks[0:BLK, :] = jnp.zeros((BLK, 128), BF16)
        vs[0:BLK, :] = jnp.zeros((BLK, 128), BF16)
        for d in DILATIONS:
            nb = T // d // BLK
            _regroup_into(qs, q_ref, d, 0)
            _regroup_into(dos, do_ref, d, 0)
            _regroup_into(ks, k_ref, d, BLK)
            _regroup_into(vs, v_ref, d, BLK)
            _regroup_into(ls, l_ref, d, 0, cast=False)
            _regroup_into(dls, dl_ref, d, 0, cast=False)
            dks[...] = jnp.zeros_like(dks)
            dvs[...] = jnp.zeros_like(dvs)

            def scores(b, s_buf, dp_buf):
                r0 = pl.multiple_of(b * BLK, BLK)
                win = pl.ds(r0, 2 * BLK)
                s_buf[...] = _dot(_stack_heads(qs[pl.ds(r0, BLK), :], h0), ks[win, :], NT)
                dp_buf[...] = _dot(_stack_heads(dos[pl.ds(r0, BLK), :], h0), vs[win, :], NT)

            def finish(b, s_buf, dp_buf, d=d, nb=nb):
                r0 = pl.multiple_of(b * BLK, BLK)
                mask = band & (cur_half | ((b & (nb - 1)) > 0))
                win = pl.ds(r0, 2 * BLK)
                lv, dlv = ls[pl.ds(r0, BLK), :], dls[pl.ds(r0, BLK), :]
                lse2 = jnp.concatenate([lv[:, 0:1], lv[:, HD:HD + 1]], axis=0)
                dl2 = jnp.concatenate([dlv[:, 0:1], dlv[:, HD:HD + 1]], axis=0)
                p = jnp.exp(jnp.where(mask, s_buf[...], NEG) - lse2)
                ds = p * (dp_buf[...] - dl2)
                pb, dsb = p.astype(BF16), ds.astype(BF16)
                dq2 = _dot(dsb, ks[win, :])
                dks[win, :] += _dot(dsb, _stack_heads(qs[pl.ds(r0, BLK), :], h0), TN)
                dvs[win, :] += _dot(pb, _stack_heads(dos[pl.ds(r0, BLK), :], h0), TN)
                rows = _natural_rows(r0, BLK, d)
                dq = jnp.where(h0, dq2[0:BLK, :], dq2[BLK:, :])
                dq_ref[rows, :] = dq if d == 1 else dq_ref[rows, :] + dq

            scores(0, sa0, da0)

            def step(i, carry):
                b = 2 * i
                scores(b + 1, sa1, da1)
                finish(b, sa0, da0)
                scores(jnp.minimum(b + 2, nblk - 1), sa0, da0)
                finish(b + 1, sa1, da1)
                return carry

            lax.fori_loop(0, nblk // 2, step, 0)

            def back(j, carry, d=d):
                r0 = pl.multiple_of(j * RG, RG)
                rows = _natural_rows(r0, RG, d)
                src = pl.ds(BLK + r0, RG)
                dk_ref[rows, :] = dks[src, :] if d == 1 else dk_ref[rows, :] + dks[src, :]
                dv_ref[rows, :] = dvs[src, :] if d == 1 else dv_ref[rows, :] + dvs[src, :]
                return carry

            lax.fori_loop(0, T // RG, back, 0)

    spec = lambda cb: _bs((T, 128), lambda p, cb=cb: (0, cb + p))
    ospec = _bs((T, 128), lambda p: (0, p))
    out = jax.ShapeDtypeStruct((T, AW), F32)
    return pl.pallas_call(
        body, name="attn_bwd", grid=(AW // 128,), in_specs=[spec(0), spec(0), spec(8), spec(0), spec(0), spec(0)],
        out_specs=[ospec] * 3, out_shape=[out] * 3,
        scratch_shapes=[pltpu.VMEM((T, 128), BF16), pltpu.VMEM((T, 128), BF16), pltpu.VMEM((T + BLK, 128), BF16),
                        pltpu.VMEM((T + BLK, 128), BF16), pltpu.VMEM((T, 128), F32), pltpu.VMEM((T, 128), F32),
                        pltpu.VMEM((T + BLK, 128), F32), pltpu.VMEM((T + BLK, 128), F32)]
        + [pltpu.VMEM((2 * BLK, 2 * BLK), F32)] * 4,
        compiler_params=_cp("parallel"),
    )(q, k, proj, do, lse, delta)


def _attn_norm(attn, g_attn):
    tr = 512

    def body(a_ref, g_ref, mix_ref):
        attn = a_ref[...]
        r = lax.rsqrt(jnp.mean(attn * attn, axis=-1, keepdims=True) + EPS)
        mix_ref[...] = ((attn * r) * g_ref[...]).astype(BF16)

    row = _bs((tr, AW), lambda i: (i, 0))
    return pl.pallas_call(
        body, name="attn_norm", grid=(T // tr,), in_specs=[row, _bs((1, AW), lambda i: (0, 0))],
        out_specs=row, out_shape=jax.ShapeDtypeStruct((T, D), BF16), compiler_params=_cp("parallel"),
    )(attn, g_attn)


def _attn_out_bwd(attn, dmix, g_attn):
    tr = 256

    def body(a_ref, d_ref, g_ref, e_ref, do_ref, dl_ref, dg_ref):
        av, dyv = a_ref[...], d_ref[...]
        r = lax.rsqrt(jnp.mean(av * av, axis=-1, keepdims=True) + EPS)
        gdy = g_ref[...] * dyv
        da = r * gdy - av * ((r * r * r) * jnp.mean(av * gdy, axis=-1, keepdims=True))
        do_ref[...] = da
        dl_ref[...] = _segsum(da * av, e_ref[...])

        @pl.when(pl.program_id(0) == 0)
        def _():
            dg_ref[...] = jnp.zeros_like(dg_ref)

        dg_ref[...] += jnp.sum(dyv * (av * r), axis=0, keepdims=True)

    row = _bs((tr, AW), lambda i: (i, 0))
    vec = _bs((1, AW), lambda i: (0, 0))
    return pl.pallas_call(
        body, name="attn_out_bwd", grid=(T // tr,), in_specs=[row, row, vec, _bs((128, 128), lambda i: (0, 0))],
        out_specs=[row, row, vec],
        out_shape=[jax.ShapeDtypeStruct((T, AW), F32), jax.ShapeDtypeStruct((T, AW), F32),
                   jax.ShapeDtypeStruct((1, AW), F32)],
        compiler_params=_cp("arbitrary"),
    )(attn, dmix, g_attn, _head_ones())


TRR = 256


def _scan_fwd(a, u):
    n = a.shape[0]
    row = lax.broadcasted_iota(jnp.int32, (n, 1), 0)
    s = 1
    while s < n:
        keep = row >= s
        u = jnp.where(keep, a * pltpu.roll(u, s, 0) + u, u)
        a = jnp.where(keep, a * pltpu.roll(a, s, 0), a)
        s *= 2
    return a, u


def _scan_bwd(c, w):
    n = c.shape[0]
    row = lax.broadcasted_iota(jnp.int32, (n, 1), 0)
    s = 1
    while s < n:
        keep = row < n - s
        w = jnp.where(keep, c * pltpu.roll(w, n - s, 0) + w, w)
        c = jnp.where(keep, c * pltpu.roll(c, n - s, 0), c)
        s *= 2
    return w


def _gates(xc, wrg, wig, brg, big, sp):
    xcb = xc.astype(BF16)
    r = jax.nn.sigmoid(_dot(xcb, wrg) + brg)
    ig = jax.nn.sigmoid(_dot(xcb, wig) + big)
    la = (-LRU_C * r) * sp
    a = jnp.exp(la)
    mult = jnp.sqrt(-jnp.tanh(la) * (a * a + 1.0))
    return r, ig, a, mult


def _conv4(ext_ref, xr, cw_ref, cb_ref, n):
    y = cb_ref[...] + ext_ref[pl.ds(5, n), :] * cw_ref[0:1, :]
    y = y + ext_ref[pl.ds(6, n), :] * cw_ref[1:2, :]
    y = y + ext_ref[pl.ds(7, n), :] * cw_ref[2:3, :]
    return y + xr * cw_ref[3:4, :]


def _rec_fwd(proj, mix, cw, cb, wrg, wig, brg, big, lam, g_rec):
    n = TRR

    def body(xr_ref, gr_ref, cw_ref, cb_ref, wrg_ref, wig_ref, brg_ref, big_ref, lam_ref, g_ref, mix_in,
             mix_ref, h_ref, ext, hcar):
        del mix_in

        @pl.when(pl.program_id(0) == 0)
        def _():
            ext[0:8, :] = jnp.zeros((8, RW), F32)
            hcar[...] = jnp.zeros_like(hcar)

        xr = xr_ref[...]
        ext[8:, :] = xr
        xc = _conv4(ext, xr, cw_ref, cb_ref, n)
        ext[0:8, :] = xr[n - 8:, :]
        sp = _softplus(-lam_ref[...])
        _, ig, a, mult = _gates(xc, wrg_ref[...], wig_ref[...], brg_ref[...], big_ref[...], sp)
        a_s, u_s = _scan_fwd(a, mult * (ig * xc))
        h = u_s + a_s * hcar[7:8, :]
        h_ref[...] = h
        hcar[...] = h[n - 8:, :]
        pre = h * _gelu(gr_ref[...])
        r = lax.rsqrt(jnp.mean(pre * pre, axis=-1, keepdims=True) + EPS)
        mix_ref[...] = ((pre * r) * g_ref[...]).astype(BF16)

    vec = _bs((1, RW), lambda i: (0, 0))
    mat = _bs((RW, RW), lambda i: (0, 0))
    return pl.pallas_call(
        body, name="rec_fwd", grid=(T // n,),
        in_specs=[_bs((n, RW), lambda i: (i, 3)), _bs((n, RW), lambda i: (i, 4)), _bs((8, RW), lambda i: (0, 0)), vec,
                  mat, mat, vec, vec, vec, vec, pl.BlockSpec(memory_space=pl.ANY)],
        out_specs=[_bs((n, RW), lambda i: (i, 1)), _bs((n, RW), lambda i: (i, 0))],
        out_shape=[jax.ShapeDtypeStruct((T, D), BF16), jax.ShapeDtypeStruct((T, RW), F32)],
        scratch_shapes=[pltpu.VMEM((n + 8, RW), F32), pltpu.VMEM((8, RW), F32)],
        input_output_aliases={10: 0}, compiler_params=_cp("arbitrary"),
    )(proj, proj, cw, cb, wrg, wig, brg, big, lam, g_rec, mix)


def _rec_bwd(proj, h, dmix, dproj, cw, cb, wrg, wig, brg, big, lam, g_rec):
    n = TRR
    nt = T // n
    hb = n // 8

    def body(xr_ref, xh_ref, gr_ref, h_ref, hh_ref, dm_ref, cw_ref, cb_ref, wrg_ref, wig_ref, brg_ref, big_ref,
             lam_ref, g_ref, dp_in, dp_ref, xc_ref, dr_ref, di_ref, dcw_ref, dcb_ref, dbr_ref, dbi_ref, dsp_ref,
             dg_ref, ext, exth, extd, adh, dgr_s):
        del dp_in
        i, j = pl.program_id(0), pl.program_id(1)
        first_tile = i == nt - 1
        last_tile = i == 0

        @pl.when(j == 0)
        def _():
            @pl.when(last_tile)
            def _():
                for ref in (dcw_ref, dcb_ref, dbr_ref, dbi_ref, dsp_ref, dg_ref):
                    ref[...] = jnp.zeros_like(ref)
                extd[n:, :] = jnp.zeros((8, RW), F32)
                adh[...] = jnp.zeros_like(adh)

            row = lax.broadcasted_iota(jnp.int32, (n, 1), 0)
            xr = xr_ref[...]
            ext[0:8, :] = jnp.where(first_tile, 0.0, xh_ref[...])
            ext[8:, :] = xr
            xc = _conv4(ext, xr, cw_ref, cb_ref, n)
            sp = _softplus(-lam_ref[...])
            wrg, wig = wrg_ref[...], wig_ref[...]
            r, ig, a, mult = _gates(xc, wrg, wig, brg_ref[...], big_ref[...], sp)

            hv = h_ref[...]
            gl, dgl = _gelu_and_grad(gr_ref[...])
            pre = hv * gl
            dyv = dm_ref[...]
            rr = lax.rsqrt(jnp.mean(pre * pre, axis=-1, keepdims=True) + EPS)
            gdy = g_ref[...] * dyv
            dpre = rr * gdy - pre * ((rr * rr * rr) * jnp.mean(pre * gdy, axis=-1, keepdims=True))
            dg_ref[...] += jnp.sum(dyv * (pre * rr), axis=0, keepdims=True)
            dgr_s[...] = dpre * hv * dgl

            is_last_row = row == n - 1
            w = dpre * gl + jnp.where(is_last_row, adh[0:1, :], 0.0)
            c = jnp.where(is_last_row, 0.0, pltpu.roll(a, n - 1, 0))
            dh = _scan_bwd(c, w)
            adh[...] = (a * dh)[0:8, :]

            exth[0:8, :] = jnp.where(first_tile, 0.0, hh_ref[...])
            exth[8:, :] = hv
            da = dh * exth[pl.ds(7, n), :]
            ixc = ig * xc
            dmult = dh * ixc
            dla = da * a - dmult * ((a * a) / mult)
            dsp_ref[...] += jnp.sum(dla * (-LRU_C * r), axis=0, keepdims=True)
            dpr = (dla * (-LRU_C * sp)) * (r * (1.0 - r))
            dpi = (dh * (mult * xc)) * (ig * (1.0 - ig))
            dprb, dpib = dpr.astype(BF16), dpi.astype(BF16)
            dxc = dh * (mult * ig) + _dot(dprb, wrg, NT) + _dot(dpib, wig, NT)
            dbr_ref[...] += jnp.sum(dpr, axis=0, keepdims=True)
            dbi_ref[...] += jnp.sum(dpi, axis=0, keepdims=True)
            xc_ref[...] = xc.astype(BF16)
            dr_ref[...] = dprb
            di_ref[...] = dpib

            extd[0:n, :] = dxc
            dxr = dxc * cw_ref[3:4, :] + extd[pl.ds(1, n), :] * cw_ref[2:3, :]
            dxr = dxr + extd[pl.ds(2, n), :] * cw_ref[1:2, :] + extd[pl.ds(3, n), :] * cw_ref[0:1, :]
            extd[n:, :] = dxc[0:8, :]
            dcb_ref[...] += jnp.sum(dxc, axis=0, keepdims=True)
            for kk in range(4):
                dcw_ref[kk:kk + 1, :] += jnp.sum(dxc * ext[pl.ds(5 + kk, n), :], axis=0, keepdims=True)

            @pl.when(first_tile)
            def _():
                dsp_ref[...] = dsp_ref[...] * (-jax.nn.sigmoid(-lam_ref[...]))

            dp_ref[...] = dxr.astype(BF16)

        @pl.when(j == 1)
        def _():
            dp_ref[...] = dgr_s[...].astype(BF16)

    vec = _bs((1, RW), lambda i, j: (0, 0))
    mat = _bs((RW, RW), lambda i, j: (0, 0))
    tile = lambda cblk: _bs((n, RW), lambda i, j, cblk=cblk: (nt - 1 - i, cblk))
    halo = lambda cblk: _bs((8, RW), lambda i, j, cblk=cblk: (jnp.maximum((nt - 1 - i) * hb - 1, 0), cblk))
    bt = jax.ShapeDtypeStruct((T, RW), BF16)
    v = jax.ShapeDtypeStruct((1, RW), F32)
    return pl.pallas_call(
        body, name="rec_bwd", grid=(nt, 2),
        in_specs=[tile(3), halo(3), tile(4), tile(0), halo(0), tile(1), _bs((8, RW), lambda i, j: (0, 0)), vec,
                  mat, mat, vec, vec, vec, vec, pl.BlockSpec(memory_space=pl.ANY)],
        out_specs=[_bs((n, RW), lambda i, j: (nt - 1 - i, 3 + j)), tile(0), tile(0), tile(0),
                   _bs((8, RW), lambda i, j: (0, 0)), vec, vec, vec, vec, vec],
        out_shape=[jax.ShapeDtypeStruct((T, INW), BF16), bt, bt, bt, jax.ShapeDtypeStruct((8, RW), F32), v, v, v, v, v],
        scratch_shapes=[pltpu.VMEM((n + 8, RW), F32), pltpu.VMEM((n + 8, RW), F32), pltpu.VMEM((n + 8, RW), F32),
                        pltpu.VMEM((8, RW), F32), pltpu.VMEM((n, RW), F32)],
        input_output_aliases={14: 0}, compiler_params=_cp("arbitrary", "arbitrary"),
    )(proj, proj, proj, h, h, dmix, cw, cb, wrg, wig, brg, big, lam, g_rec, dproj)


FC = 1536
TRF = 256


LC = 128


def _taps(x_ref, edge, cols, r):
    if r == 0:
        return edge[pl.ds(6, 8), cols], edge[pl.ds(7, 8), cols], edge[pl.ds(8, 8), cols]
    return x_ref[pl.ds(r - 2, 8), cols], x_ref[pl.ds(r - 1, 8), cols], x_ref[pl.ds(r, 8), cols]


def _ffn_act(up_pre, cw, cb):
    n = TRF
    hb = n // 8

    def body(g_ref, gh_ref, u_ref, uh_ref, wg_ref, wu_ref, bg_ref, bu_ref, o_ref, eg, eu):
        first = pl.program_id(1) == 0
        eg[0:8, :] = jnp.where(first, 0.0, gh_ref[...])
        eg[8:, :] = g_ref[0:8, :]
        eu[0:8, :] = jnp.where(first, 0.0, uh_ref[...])
        eu[8:, :] = u_ref[0:8, :]

        def column(ci, carry):
            cols = pl.ds(pl.multiple_of(ci * LC, LC), LC)
            rows8 = lambda v: jnp.broadcast_to(v, (8, LC))
            wg = [rows8(wg_ref[kk:kk + 1, cols]) for kk in range(3)]
            wu = [rows8(wu_ref[kk:kk + 1, cols]) for kk in range(3)]
            bg, bu = rows8(bg_ref[:, cols]), rows8(bu_ref[:, cols])
            for r in range(0, n, 16):
                res = []
                for rr in (r, r + 8):
                    g0, g1, g2 = _taps(g_ref, eg, cols, rr)
                    u0, u1, u2 = _taps(u_ref, eu, cols, rr)
                    ug = ((bg + g0 * wg[0]) + g1 * wg[1]) + g2 * wg[2]
                    uu = ((bu + u0 * wu[0]) + u1 * wu[1]) + u2 * wu[2]
                    res.append(_gelu(ug) * uu)
                o_ref[pl.ds(r, 16), cols] = jnp.concatenate(res, axis=0).astype(BF16)
            return carry

        lax.fori_loop(0, FC // LC, column, 0)

    main = lambda o: _bs((n, FC), lambda j, i, o=o: (i, 2 * j + o))
    halo = lambda o: _bs((8, FC), lambda j, i, o=o: (jnp.maximum(i * hb - 1, 0), 2 * j + o))
    wsp = lambda o: _bs((None, 8, FC), lambda j, i, o=o: (2 * j + o, 0, 0))
    bsp = lambda o: _bs((1, FC), lambda j, i, o=o: (0, 2 * j + o))
    return pl.pallas_call(
        body, name="ffn_act", grid=(2, T // n),
        in_specs=[main(0), halo(0), main(1), halo(1), wsp(0), wsp(1), bsp(0), bsp(1)],
        out_specs=_bs((n, FC), lambda j, i: (i, j)), out_shape=jax.ShapeDtypeStruct((T, DFF), BF16),
        scratch_shapes=[pltpu.VMEM((16, FC), F32)] * 2, compiler_params=_cp("parallel", "parallel"),
    )(up_pre, up_pre, up_pre, up_pre, cw, cw, cb, cb)


def _up_act(h2, w_up, cw, cb):
    n = TRF
    nt = T // n
    pw = 256
    npc = FC // pw

    def body(h_ref, wg_ref, wu_ref, cwg_ref, cwu_ref, bg_ref, bu_ref, up_ref, a_ref, hx, gb0, gb1, ub0, ub1):
        i = pl.program_id(1)
        halo = h_ref[pl.ds(pl.multiple_of(jnp.maximum(i * n - 16, 0), 16), 16), :]
        hx[0:16, :] = jnp.where(i == 0, jnp.zeros_like(halo), halo)
        hx[16:, :] = h_ref[pl.ds(pl.multiple_of(i * n, n), n), :]
        gbufs, ubufs = (gb0, gb1), (ub0, ub1)

        def dots(c):
            hv = hx[...]
            gbufs[c % 2][...] = _dot(hv, wg_ref[:, c * pw:(c + 1) * pw])
            ubufs[c % 2][...] = _dot(hv, wu_ref[:, c * pw:(c + 1) * pw])

        def chain(c):
            gb, ub = gbufs[c % 2], ubufs[c % 2]
            up_ref[:, c * pw:(c + 1) * pw] = gb[16:, :]
            up_ref[:, FC + c * pw:FC + (c + 1) * pw] = ub[16:, :]
            rows8 = lambda v: jnp.broadcast_to(v, (8, LC))
            for sub in range(pw // LC):
                lc = slice(sub * LC, (sub + 1) * LC)
                cols = slice(c * pw + sub * LC, c * pw + (sub + 1) * LC)
                wg = [rows8(cwg_ref[kk:kk + 1, cols]) for kk in range(3)]
                wu = [rows8(cwu_ref[kk:kk + 1, cols]) for kk in range(3)]
                bg, bu = rows8(bg_ref[:, cols]), rows8(bu_ref[:, cols])
                for r in range(0, n, 16):
                    res = []
                    for rr in (16 + r, 24 + r):
                        ug = ((bg + gb[pl.ds(rr - 2, 8), lc] * wg[0]) + gb[pl.ds(rr - 1, 8), lc] * wg[1]) \
                            + gb[pl.ds(rr, 8), lc] * wg[2]
                        uu = ((bu + ub[pl.ds(rr - 2, 8), lc] * wu[0]) + ub[pl.ds(rr - 1, 8), lc] * wu[1]) \
                            + ub[pl.ds(rr, 8), lc] * wu[2]
                        res.append(_gelu(ug) * uu)
                    a_ref[pl.ds(r, 16), cols] = jnp.concatenate(res, axis=0).astype(BF16)

        dots(0)
        for c in range(npc):
            if c + 1 < npc:
                dots(c + 1)
            chain(c)

    wsl = lambda o: _bs((None, D, FC), lambda j, i, o=o: (2 * j + o, 0, 0))
    wsp = lambda o: _bs((None, 8, FC), lambda j, i, o=o: (2 * j + o, 0, 0))
    bsp = lambda o: _bs((1, FC), lambda j, i, o=o: (0, 2 * j + o))
    return pl.pallas_call(
        body, name="up_act", grid=(2, nt),
        in_specs=[pl.BlockSpec((T, D), lambda j, i: (0, 0), pipeline_mode=pl.Buffered(1)), wsl(0), wsl(1),
                  wsp(0), wsp(1), bsp(0), bsp(1)],
        out_specs=[_bs((n, 2 * FC), lambda j, i: (i, j)), _bs((n, FC), lambda j, i: (i, j))],
        out_shape=[jax.ShapeDtypeStruct((T, 2 * DFF), F32), jax.ShapeDtypeStruct((T, DFF), BF16)],
        scratch_shapes=[pltpu.VMEM((n + 16, D), BF16)] + [pltpu.VMEM((n + 16, pw), F32)] * 4,
        compiler_params=_cp("parallel", "arbitrary"),
    )(h2, w_up, w_up, cw, cw, cb, cb)


def _ffn_bwd(up_pre, dyb, w_down, cw, cb, after=()):
    n = TRF
    hb = n // 8
    nt = T // n
    m = n + 8
    pw = 256
    npc = FC // pw

    def body(g_ref, gp_ref, gn_ref, u_ref, up_ref, un_ref, dy_ref, wd_ref, wg_ref, wu_ref, bg_ref, bu_ref, *rest):
        o_ref, dw_ref, db_ref, eg0, eg1, eu0, eu1, dug_s, duu_s, dyx, db0, db1 = rest[len(after):]
        i = pl.program_id(1)
        first, last = i == 0, i == nt - 1

        @pl.when(first)
        def _():
            dw_ref[...] = jnp.zeros_like(dw_ref)
            db_ref[...] = jnp.zeros_like(db_ref)

        tail = dy_ref[pl.ds(pl.multiple_of(jnp.minimum((i + 1) * n, T - 16), 16), 16), :]
        dyx[0:n, :] = dy_ref[pl.ds(pl.multiple_of(i * n, n), n), :]
        dyx[n:, :] = jnp.where(last, jnp.zeros_like(tail), tail)
        dbufs = (db0, db1)

        def dots(c):
            dbufs[c % 2][...] = _dot(dyx[...], wd_ref[c * pw:(c + 1) * pw, :], NT)

        eg0[0:8, :] = jnp.where(first, 0.0, gp_ref[...])
        eg0[8:, :] = g_ref[0:8, :]
        eg1[0:8, :] = g_ref[n - 8:, :]
        eg1[8:, :] = gn_ref[...]
        eu0[0:8, :] = jnp.where(first, 0.0, up_ref[...])
        eu0[8:, :] = u_ref[0:8, :]
        eu1[0:8, :] = u_ref[n - 8:, :]
        eu1[8:, :] = un_ref[...]

        def column(ci, dbuf, lc):
            cols = slice(ci * LC, (ci + 1) * LC)
            ucols = slice(FC + ci * LC, FC + (ci + 1) * LC)
            rows8 = lambda v: jnp.broadcast_to(v, (8, LC))
            wg = [rows8(wg_ref[kk:kk + 1, cols]) for kk in range(3)]
            wu = [rows8(wu_ref[kk:kk + 1, cols]) for kk in range(3)]
            bg, bu = rows8(bg_ref[:, cols]), rows8(bu_ref[:, cols])
            zero = jnp.zeros((8, LC), F32)
            acc = [zero] * 8
            for r in range(0, n + 8, 8):
                if r == n:
                    gt = (eg1[pl.ds(6, 8), cols], eg1[pl.ds(7, 8), cols], eg1[pl.ds(8, 8), cols])
                    ut = (eu1[pl.ds(6, 8), cols], eu1[pl.ds(7, 8), cols], eu1[pl.ds(8, 8), cols])
                else:
                    gt, ut = _taps(g_ref, eg0, cols, r), _taps(u_ref, eu0, cols, r)
                dv = dbuf[pl.ds(r, 8), lc]
                gl, dgl = _gelu_and_grad(((bg + gt[0] * wg[0]) + gt[1] * wg[1]) + gt[2] * wg[2])
                uu = ((bu + ut[0] * wu[0]) + ut[1] * wu[1]) + ut[2] * wu[2]
                dug, duu = dv * uu * dgl, dv * gl
                dug_s[pl.ds(r, 8), :] = dug
                duu_s[pl.ds(r, 8), :] = duu
                if r < n:
                    acc = [acc[0] + dug * gt[0], acc[1] + dug * gt[1], acc[2] + dug * gt[2],
                           acc[3] + duu * ut[0], acc[4] + duu * ut[1], acc[5] + duu * ut[2], acc[6] + dug, acc[7] + duu]
            for r in range(0, n, 16):
                og, ou = [], []
                for rr in (r, r + 8):
                    og.append((dug_s[pl.ds(rr, 8), :] * wg[2] + dug_s[pl.ds(rr + 1, 8), :] * wg[1])
                              + dug_s[pl.ds(rr + 2, 8), :] * wg[0])
                    ou.append((duu_s[pl.ds(rr, 8), :] * wu[2] + duu_s[pl.ds(rr + 1, 8), :] * wu[1])
                              + duu_s[pl.ds(rr + 2, 8), :] * wu[0])
                o_ref[pl.ds(r, 16), cols] = jnp.concatenate(og, axis=0).astype(BF16)
                o_ref[pl.ds(r, 16), ucols] = jnp.concatenate(ou, axis=0).astype(BF16)
            for kk in range(3):
                dw_ref[kk:kk + 1, cols] += jnp.sum(acc[kk], axis=0, keepdims=True)
                dw_ref[kk:kk + 1, ucols] += jnp.sum(acc[3 + kk], axis=0, keepdims=True)
            db_ref[:, cols] += jnp.sum(acc[6], axis=0, keepdims=True)
            db_ref[:, ucols] += jnp.sum(acc[7], axis=0, keepdims=True)

        dots(0)
        for c in range(npc):
            if c + 1 < npc:
                dots(c + 1)
            for sub in range(pw // LC):
                column(c * (pw // LC) + sub, dbufs[c % 2], slice(sub * LC, (sub + 1) * LC))

    main = lambda o: _bs((n, FC), lambda j, i, o=o: (i, 2 * j + o))
    prev = lambda o: _bs((8, FC), lambda j, i, o=o: (jnp.maximum(i * hb - 1, 0), 2 * j + o))
    nxt = lambda o: _bs((8, FC), lambda j, i, o=o: (jnp.minimum((i + 1) * hb, T // 8 - 1), 2 * j + o))
    wsp = lambda o: _bs((None, 8, FC), lambda j, i, o=o: (2 * j + o, 0, 0))
    bsp = lambda o: _bs((1, FC), lambda j, i, o=o: (0, 2 * j + o))
    return pl.pallas_call(
        body, name="ffn_bwd", grid=(2, nt),
        in_specs=[main(0), prev(0), nxt(0), main(1), prev(1), nxt(1),
                  pl.BlockSpec((T, D), lambda j, i: (0, 0), pipeline_mode=pl.Buffered(1)),
                  _bs((FC, D), lambda j, i: (j, 0)), wsp(0), wsp(1), bsp(0), bsp(1)]
        + [pl.BlockSpec(memory_space=pl.ANY)] * len(after),
        out_specs=[_bs((n, 2 * FC), lambda j, i: (i, j)), _bs((8, 2 * FC), lambda j, i: (0, j)),
                   _bs((1, 2 * FC), lambda j, i: (0, j))],
        out_shape=[jax.ShapeDtypeStruct((T, 2 * DFF), BF16), jax.ShapeDtypeStruct((8, 2 * DFF), F32),
                   jax.ShapeDtypeStruct((1, 2 * DFF), F32)],
        scratch_shapes=[pltpu.VMEM((16, FC), F32)] * 4 + [pltpu.VMEM((m, LC), F32)] * 2
        + [pltpu.VMEM((n + 16, D), BF16)] + [pltpu.VMEM((n + 16, pw), F32)] * 2,
        compiler_params=_cp("parallel", "arbitrary"),
    )(up_pre, up_pre, up_pre, up_pre, up_pre, up_pre, dyb, w_down, cw, cw, cb, cb, *after)


def _down_loss(act, w_down, x1, target):
    tm, tn = 512, D

    def body(a_ref, b_ref, r_ref, t_ref, dy_ref, dyb_ref, l_ref):
        @pl.when((pl.program_id(0) == 0) & (pl.program_id(1) == 0))
        def _():
            l_ref[...] = jnp.zeros_like(l_ref)

        err = (r_ref[...] + _dot(a_ref[...], b_ref[...])) - t_ref[...]
        dy = err * (1.0 / D)
        dy_ref[...] = dy
        dyb_ref[...] = dy.astype(BF16)
        l_ref[...] += jnp.sum(0.5 * (err * err) * (1.0 / D))

    o_spec = _bs((tm, tn), lambda j, i: (i, j))
    return pl.pallas_call(
        body, name="down_loss", grid=(D // tn, T // tm),
        in_specs=[_bs((tm, DFF), lambda j, i: (i, 0)),
                  pl.BlockSpec((DFF, tn), lambda j, i: (0, j), pipeline_mode=pl.Buffered(1)), o_spec, o_spec],
        out_specs=[o_spec, o_spec, _bs((8, 128), lambda j, i: (0, 0))],
        out_shape=[jax.ShapeDtypeStruct((T, D), F32), jax.ShapeDtypeStruct((T, D), BF16),
                   jax.ShapeDtypeStruct((8, 128), F32)],
        compiler_params=_cp("arbitrary", "arbitrary"),
    )(act, w_down, x1, target)


def _block_diag(w):
    eye = jnp.eye(8, dtype=w.dtype)
    return (w[:, :, None, :] * eye[:, None, :, None]).reshape(RW, RW).astype(BF16)


def _diag_blocks(m):
    eye = jnp.eye(8, dtype=m.dtype)
    return (m.reshape(8, HD, 8, HD) * eye[:, None, :, None]).sum(axis=2)


def _local_step(x, pos_col, target, p, exch):
    qg, kg = jnp.tile(p["q_norm_g"], (1, 8)), jnp.tile(p["k_norm_g"], (1, 8))
    wrg, wig = _block_diag(p["w_rg"]), _block_diag(p["w_ig"])
    brg, big = p["b_rg"].reshape(1, RW), p["b_ig"].reshape(1, RW)

    h1 = _rms_fwd("rms1", x, p["g_mix"])
    proj = _mm("mm_in", h1, p["w_in"], "nn", 512, 640, stack=NCHIP, after=exch.start_rest(), a_full=True)
    q, k, cos_t, sin_t = _qk_prep(proj, pos_col, qg, kg)
    attn, lse = _attn_fwd(q, k, proj)
    mix = _attn_norm(attn, p["g_attn_out"])
    mix, hseq = _rec_fwd(proj, mix, p["rec_conv_w"], p["rec_conv_b"], wrg, wig, brg, big, p["lru_lambda"], p["g_rec_out"])
    rest = exch.wait_rest(mix)
    x1 = _mm("mm_out", mix, rest["w_out"], "nn", 512, 512, res=x, a_full=True)
    h2 = _rms_fwd("rms2", x1, p["g_ffn"])
    up_pre, act = _up_act(h2, rest["w_up"], p["ffn_conv_w"], p["ffn_conv_b"])
    dy, dyb, loss_blk = _down_loss(act, rest["w_down"], x1, target)

    g = {}
    tok = exch.reduce_start("w_down", *_mm("wg_down", act, dyb, "tn", 512, 512, twin_bf16=True))
    dup, g["ffn_conv_w"], g["ffn_conv_b"] = _ffn_bwd(up_pre, dyb, rest["w_down"], p["ffn_conv_w"], p["ffn_conv_b"], tok)
    tok = exch.reduce_start("w_up", *_mm("wg_up", h2, dup, "tn", 512, 768, stack=NCHIP, twin_bf16=True, a_full=True))
    dh2 = _mm("dg_up", dup, rest["w_up"], "nt", 512, D, stack=NCHIP, after=tok, b_full=True)
    dx1, dx1b, g["g_ffn"] = _rms_bwd("rms2_bwd", x1, p["g_ffn"], dh2, dy, True)
    tok = exch.reduce_start("w_out", *_mm("wg_out", mix, dx1b, "tn", 512, 512, twin_bf16=True, a_full=True))
    dmix = _mm("dg_out", dx1b, rest["w_out"], "nt", 512, 512, after=tok, a_full=True)
    do, delta, g["g_attn_out"] = _attn_out_bwd(attn, dmix, p["g_attn_out"])
    dq, dk, dv = _attn_bwd(q, k, proj, do, lse, delta)
    dproj, dqg, dkg = _qk_bwd(proj, cos_t, sin_t, qg, kg, dq, dk, dv)
    (dproj, xcb, dprb, dpib, g["rec_conv_w"], g["rec_conv_b"], dbr, dbi, dsp, g["g_rec_out"]) = _rec_bwd(
        proj, hseq, dmix, dproj, p["rec_conv_w"], p["rec_conv_b"], wrg, wig, brg, big, p["lru_lambda"], p["g_rec_out"])
    g["w_rg"] = _diag_blocks(_mm("wg_rg", xcb, dprb, "tn", 512, 512)).reshape(RW, HD)
    g["w_ig"] = _diag_blocks(_mm("wg_ig", xcb, dpib, "tn", 512, 512)).reshape(RW, HD)
    g["b_rg"], g["b_ig"] = dbr.reshape(8, HD), dbi.reshape(8, HD)
    g["lru_lambda"] = dsp
    g["q_norm_g"] = dqg.reshape(8, HD).sum(axis=0, keepdims=True)
    g["k_norm_g"] = dkg.reshape(8, HD).sum(axis=0, keepdims=True)
    tok = exch.reduce_start("w_in", *_mm("wg_in", h1, dproj, "tn", 512, 640, stack=NCHIP, twin_bf16=True, a_full=True))
    dh1 = _mm("dg_in", dproj, p["w_in"], "nt", 512, 512, stack=NCHIP, after=tok)
    grad_x, g["g_mix"] = _rms_bwd("rms1_bwd", x, p["g_mix"], dh1, dx1, False)
    return loss_blk, grad_x, g


ANY = pl.BlockSpec(memory_space=pl.ANY)


def _mesh_pos():
    return lax.axis_index("x"), lax.axis_index("y"), lax.axis_index("c")


def _slot(px, py, perm):
    return 2 * py + px if perm else 2 * px + py


def _other_chips(x, y):
    return [(1 - x, y), (x, 1 - y), (1 - x, 1 - y)]


def _rcopy(src, dst, send, recv, k, to, kr=None):
    return pltpu.make_async_remote_copy(src_ref=src, dst_ref=dst, send_sem=send.at[k],
                                        recv_sem=recv.at[k if kr is None else kr], device_id=to, device_id_type=MESH)


def _cast_bf16(name, w):
    r, c = w.shape
    tr = 128
    def body(w_ref, o_ref):
        o_ref[...] = w_ref[...].astype(BF16)
    return pl.pallas_call(
        body, name=name, grid=(r // tr,), in_specs=[_bs((tr, c), lambda i: (i, 0))],
        out_specs=_bs((tr, c), lambda i: (i, 0)), out_shape=jax.ShapeDtypeStruct((r, c), BF16),
        compiler_params=_cp("parallel"),
    )(w)


def _gather_weights(big, small, slot):
    nb, ns = len(big), len(small)
    perms = [p for _, p in big] + [p for _, p in small]

    def body(*refs):
        ins, outs = refs[:nb + ns], refs[2 * (nb + ns):3 * (nb + ns)]
        send, recv = refs[3 * (nb + ns):]
        x, y, c = _mesh_pos()
        me, sib = (x, y, c), (x, y, 1 - c)
        chips = _other_chips(x, y)
        first = []
        for a in range(nb):
            for j, (px, py) in enumerate(chips):
                first.append(_rcopy(ins[a].at[c], outs[a].at[_slot(x, y, perms[a]), c], send, recv, 3 * a + j, (px, py, c)))
        for t in range(ns):
            a = nb + t
            for j, (px, py) in enumerate(chips):
                first.append(_rcopy(ins[a], outs[a].at[_slot(x, y, perms[a])], send, recv, 6 * nb + 3 * t + j, (px, py, c)))
        for cp in first:
            cp.start()
        passed = []
        for a in range(nb):
            for j, (px, py) in enumerate(chips):
                got = outs[a].at[_slot(px, py, perms[a]), c]
                _rcopy(got, got, send, recv, 3 * a + j, me).wait_recv()
                fwd = _rcopy(got, got, send, recv, 3 * nb + 3 * a + j, sib)
                fwd.start()
                passed.append(fwd)
        for a in range(nb):
            for j, (px, py) in enumerate(chips):
                got = outs[a].at[_slot(px, py, perms[a]), 1 - c]
                _rcopy(got, got, send, recv, 3 * nb + 3 * a + j, me).wait_recv()
        for t in range(ns):
            a = nb + t
            for j, (px, py) in enumerate(chips):
                got = outs[a].at[_slot(px, py, perms[a])]
                _rcopy(got, got, send, recv, 6 * nb + 3 * t + j, me).wait_recv()
        for cp in first + passed:
            cp.wait_send()

    arrs = [a for a, _ in big] + [a for a, _ in small]
    lands = [lax.dynamic_update_slice(lax.empty((NCHIP,) + a.shape, a.dtype), a[None], (slot[p],) + (0,) * a.ndim)
             for a, p in zip(arrs, perms)]
    nsem = 6 * nb + 3 * ns
    return pl.pallas_call(
        body, name="gather_weights", in_specs=[ANY] * (2 * (nb + ns)), out_specs=[ANY] * (nb + ns),
        out_shape=[jax.ShapeDtypeStruct(a.shape, a.dtype) for a in lands],
        input_output_aliases={nb + ns + i: i for i in range(nb + ns)},
        scratch_shapes=[pltpu.SemaphoreType.DMA((nsem,)), pltpu.SemaphoreType.DMA((nsem,))],
    )(*arrs, *lands)


HBM = pl.BlockSpec(memory_space=pltpu.HBM)
SEM = pl.BlockSpec(memory_space=pltpu.SEMAPHORE)
EFFECT = pltpu.SideEffectType.DATAFLOW_SIDE_EFFECTING


def _split_start(name, srcs, lands, plan, nsem):
    ns, nl = len(srcs), len(lands)

    def body(*refs):
        send, recv = refs[ns + nl], refs[ns + nl + 1]
        sends, _ = plan(refs[:ns], refs[ns:ns + nl], send, recv)
        for cp in sends:
            cp.start()
        refs[-1][...] = jnp.zeros((8, 128), F32)

    arrs = list(srcs) + list(lands)
    out = pl.pallas_call(
        body, name=name, in_specs=[HBM] * (ns + nl),
        out_specs=[SEM, SEM] + [HBM] * (ns + nl) + [pl.BlockSpec(memory_space=pltpu.VMEM)],
        out_shape=[pltpu.SemaphoreType.DMA((nsem,)), pltpu.SemaphoreType.DMA((nsem,))]
        + [pltpu.HBM(a.shape, a.dtype) for a in arrs] + [jax.ShapeDtypeStruct((8, 128), F32)],
        input_output_aliases={i: 2 + i for i in range(ns + nl)},
        compiler_params=pltpu.CompilerParams(has_side_effects=EFFECT),
    )(*[pltpu.with_memory_space_constraint(a, pltpu.HBM) for a in arrs])
    return out[0], out[1], out[2:2 + ns], out[2 + ns:2 + ns + nl], out[-1]


def _split_wait(name, send, recv, srcs, lands, plan, after):
    ns, nl = len(srcs), len(lands)

    def body(*refs):
        sends, recvs = plan(refs[:ns], refs[ns:ns + nl], refs[ns + nl], refs[ns + nl + 1])
        for cp in sends:
            cp.wait_send()
        for cp in recvs:
            cp.wait_recv()

    arrs = list(srcs) + list(lands)
    out = pl.pallas_call(
        body, name=name, in_specs=[HBM] * (ns + nl) + [SEM, SEM, ANY], out_specs=[HBM] * (ns + nl),
        out_shape=[pltpu.HBM(a.shape, a.dtype) for a in arrs],
        input_output_aliases={i: i for i in range(ns + nl)},
        compiler_params=pltpu.CompilerParams(has_side_effects=EFFECT),
    )(*arrs, send, recv, after)
    return out[ns:]


def _gather_plan(perms):
    def plan(srcs, lands, send, recv):
        x, y, c = _mesh_pos()
        sends, recvs = [], []
        for a, perm in enumerate(perms):
            for j, (px, py) in enumerate(_other_chips(x, y)):
                for cc in (0, 1):
                    k = 6 * a + 2 * j + cc
                    sends.append(_rcopy(srcs[a].at[c], lands[a].at[_slot(x, y, perm), c], send, recv, k, (px, py, cc),
                                        kr=6 * a + 2 * j + c))
                    got = lands[a].at[_slot(px, py, perm), cc]
                    recvs.append(_rcopy(got, got, send, recv, k, (x, y, c)))
        return sends, recvs
    return plan


def _reduce_plan(perm):
    def plan(srcs, lands, send, recv):
        x, y, c = _mesh_pos()
        src, land = srcs[0], lands[0]
        sends = []
        for j, (px, py) in enumerate(_other_chips(x, y)):
            for hf in (0, 1):
                sends.append(_rcopy(src.at[_slot(px, py, perm), hf], land.at[2 * j + c], send, recv, 2 * j + hf,
                                    (px, py, hf), kr=2 * j + c))
        sends.append(_rcopy(src.at[_slot(x, y, perm), 1 - c], land.at[6], send, recv, 6, (x, y, 1 - c)))
        recvs = [_rcopy(land.at[i], land.at[i], send, recv, i, (x, y, c)) for i in range(7)]
        return sends, recvs
    return plan


def _sibling_share(rs):
    na = len(rs)

    def body(*refs):
        ins, outs, (send, recv) = refs[:na], refs[na:2 * na], refs[2 * na:]
        x, y, c = _mesh_pos()
        cps = [_rcopy(ins[a], outs[a], send, recv, a, (x, y, 1 - c)) for a in range(na)]
        for cp in cps:
            cp.start()
        for cp in cps:
            cp.wait()

    return pl.pallas_call(
        body, name="rs_share", in_specs=[ANY] * na, out_specs=[ANY] * na,
        out_shape=[jax.ShapeDtypeStruct(r.shape, F32) for r in rs],
        scratch_shapes=[pltpu.SemaphoreType.DMA((na,)), pltpu.SemaphoreType.DMA((na,))],
    )(*rs)


def _add_pieces(name, g, got, where):
    _, _, r2, cc = g.shape
    tr = 128

    def body(w_ref, g_ref, r_ref, o_ref):
        del w_ref
        acc = g_ref[...]
        for i in range(7):
            acc = acc + r_ref[i].astype(F32)
        o_ref[...] = acc

    return pl.pallas_call(
        body, name=name,
        grid_spec=pltpu.PrefetchScalarGridSpec(
            num_scalar_prefetch=1, grid=(r2 // tr,),
            in_specs=[_bs((None, None, tr, cc), lambda i, w_ref: (w_ref[0], w_ref[1], i, 0)),
                      _bs((7, tr, cc), lambda i, w_ref: (0, i, 0))],
            out_specs=_bs((tr, cc), lambda i, w_ref: (i, 0))),
        out_shape=jax.ShapeDtypeStruct((r2, cc), F32), compiler_params=_cp("parallel"),
    )(where, g, got)


def _adam_math(w, g, m, v):
    m = ADAM_B1 * m + (1.0 - ADAM_B1) * g
    v = ADAM_B2 * v + (1.0 - ADAM_B2) * (g * g)
    m_hat = m / (1.0 - ADAM_B1 ** ADAM_STEP)
    v_hat = v / (1.0 - ADAM_B2 ** ADAM_STEP)
    return -ADAM_LR * (m_hat / (jnp.sqrt(v_hat) + ADAM_EPS) + ADAM_WD * w), m, v


def _adam_big(name, w, g_mine, g_sib, m, v, c_arr):
    r, cols = w.shape
    tr = 128
    per = r // 2 // tr

    def body(c_ref, w_ref, a_ref, b_ref, m_ref, v_ref, g_ref, d_ref, m2_ref, v2_ref):
        g = jnp.where(pl.program_id(0) == c_ref[0], a_ref[...], b_ref[...])
        g_ref[...] = g
        d_ref[...], m2_ref[...], v2_ref[...] = _adam_math(w_ref[...], g, m_ref[...], v_ref[...])

    spec = _bs((tr, cols), lambda h, i, c_ref: (h * per + i, 0))
    half = _bs((tr, cols), lambda h, i, c_ref: (i, 0))
    out = jax.ShapeDtypeStruct((r, cols), F32)
    return pl.pallas_call(
        body, name=name,
        grid_spec=pltpu.PrefetchScalarGridSpec(
            num_scalar_prefetch=1, grid=(2, per), in_specs=[spec, half, half, spec, spec], out_specs=[spec] * 4),
        out_shape=[out] * 4, compiler_params=_cp("parallel", "parallel"),
    )(c_arr, w, g_mine, g_sib, m, v)


_CLASS_SHAPE = {"a": (8, D), "b": (8, RW), "c": (8, 2 * DFF), "d": (1048, HD)}
_SMALL = (
    ("g_mix", "a", 0, 1, D), ("g_ffn", "a", 1, 1, D),
    ("rec_conv_w", "b", 0, 4, RW), ("rec_conv_b", "b", 4, 1, RW), ("lru_lambda", "b", 5, 1, RW),
    ("g_attn_out", "b", 6, 1, RW), ("g_rec_out", "b", 7, 1, RW),
    ("ffn_conv_w", "c", 0, 3, 2 * DFF), ("ffn_conv_b", "c", 3, 1, 2 * DFF),
    ("w_rg", "d", 0, RW, HD), ("w_ig", "d", RW, RW, HD), ("b_rg", "d", 2 * RW, 8, HD), ("b_ig", "d", 2 * RW + 8, 8, HD),
    ("q_norm_g", "d", 2 * RW + 16, 1, HD), ("k_norm_g", "d", 2 * RW + 17, 1, HD),
)
_LOSS_ROW = 2
_CLASSES = ("a", "b", "c", "d")
_CLASS_OWNER = {"a": 0, "b": 0, "c": 0, "d": 1}


def _small_allreduce(g, loss_blk):
    names = [s[0] for s in _SMALL]
    nin = len(names) + 1

    def body(*refs):
        ins = dict(zip(names, refs[:len(names)]))
        loss_ref = refs[len(names)]
        outs = dict(zip(_CLASSES, refs[nin:nin + 4]))
        pair = dict(zip(_CLASSES, refs[nin + 4:nin + 8]))
        quad = dict(zip(_CLASSES, refs[nin + 8:nin + 12]))
        send, recv = refs[nin + 12:]
        x, y, c = _mesh_pos()
        chip = 2 * x + y
        pair["a"][c] = jnp.zeros(_CLASS_SHAPE["a"], F32)
        pair["b"][c] = ins["rec_conv_w"][...]
        pair["c"][c] = ins["ffn_conv_w"][...]
        pair["d"][c, 2 * RW + 16:, :] = jnp.zeros((8, HD), F32)
        for name, k, r0, nr, _ in _SMALL:
            if name in ("rec_conv_w", "ffn_conv_w"):
                continue
            pair[k][c, r0:r0 + nr, :] = ins[name][...]
        pair["a"][c, _LOSS_ROW:_LOSS_ROW + 1, :] = jnp.broadcast_to(loss_ref[0:1, 0:1], (1, D))
        cps = [_rcopy(pair[k].at[c], pair[k].at[c], send, recv, ki, (x, y, 1 - c)) for ki, k in enumerate(_CLASSES)]
        for cp in cps:
            cp.start()
        for ki, k in enumerate(_CLASSES):
            _rcopy(pair[k].at[1 - c], pair[k].at[1 - c], send, recv, ki, (x, y, c)).wait_recv()
            quad[k][chip] = pair[k][0] + pair[k][1]
        for cp in cps:
            cp.wait_send()
        for ki, k in enumerate(_CLASSES):
            owner = _CLASS_OWNER[k]

            @pl.when(c == owner)
            def _(ki=ki, k=k):
                cps2 = [_rcopy(quad[k].at[chip], quad[k].at[chip], send, recv, 4 + 3 * ki + j, (px, py, c))
                        for j, (px, py) in enumerate(_other_chips(x, y))]
                for cp in cps2:
                    cp.start()
                for j, (px, py) in enumerate(_other_chips(x, y)):
                    got = quad[k].at[2 * px + py]
                    _rcopy(got, got, send, recv, 4 + 3 * ki + j, (x, y, c)).wait_recv()
                outs[k][...] = ((quad[k][0] + quad[k][1]) + quad[k][2]) + quad[k][3]
                share = _rcopy(outs[k], outs[k], send, recv, 16 + ki, (x, y, 1 - c))
                share.start()
                for cp in cps2:
                    cp.wait_send()
                share.wait_send()

        for ki, k in enumerate(_CLASSES):
            @pl.when(c != _CLASS_OWNER[k])
            def _(ki=ki, k=k):
                _rcopy(outs[k], outs[k], send, recv, 16 + ki, (x, y, c)).wait_recv()

    vm = pl.BlockSpec(memory_space=pltpu.VMEM)
    return pl.pallas_call(
        body, name="small_allreduce", in_specs=[vm] * nin, out_specs=[vm] * 4,
        out_shape=[jax.ShapeDtypeStruct(_CLASS_SHAPE[k], F32) for k in _CLASSES],
        scratch_shapes=[pltpu.VMEM((2,) + _CLASS_SHAPE[k], F32) for k in _CLASSES]
        + [pltpu.VMEM((NCHIP,) + _CLASS_SHAPE[k], F32) for k in _CLASSES]
        + [pltpu.SemaphoreType.DMA((20,)), pltpu.SemaphoreType.DMA((20,))],
        compiler_params=pltpu.CompilerParams(vmem_limit_bytes=VMEM_LIMIT),
    )(*[g[n] for n in names], loss_blk)


def _adam_small(red, w, m, v):
    names = [s[0] for s in _SMALL]
    n = len(names)

    def body(*refs):
        red_refs = dict(zip(_CLASSES, refs[:4]))
        w_refs, m_refs, v_refs = refs[4:4 + n], refs[4 + n:4 + 2 * n], refs[4 + 2 * n:4 + 3 * n]
        loss_ref = refs[4 + 3 * n]
        out_refs = refs[5 + 3 * n:]
        x, y, _ = _mesh_pos()
        chip = 2 * x + y
        loss_ref[...] = jnp.broadcast_to(red_refs["a"][_LOSS_ROW:_LOSS_ROW + 1, 0:1], loss_ref.shape)
        for pi, (name, k, r0, nr, width) in enumerate(_SMALL):
            gfull = red_refs[k][r0:r0 + nr, :]
            if name == "rec_conv_w":
                parts = [gfull[:, 128 * s:128 * (s + 1)] for s in range(NCHIP)]
                g = jnp.where(chip == 0, parts[0], jnp.where(chip == 1, parts[1], jnp.where(chip == 2, parts[2], parts[3])))
            elif name == "ffn_conv_w":
                parts = [gfull[:, FC * s:FC * (s + 1)] for s in range(NCHIP)]
                g = jnp.where(chip == 0, parts[0], jnp.where(chip == 1, parts[2], jnp.where(chip == 2, parts[1], parts[3])))
            elif name == "ffn_conv_b":
                g = jnp.concatenate([gfull[:, FC * s:FC * (s + 1)] for s in (0, 2, 1, 3)], axis=1)
            else:
                g = gfull
            d, m2, v2 = _adam_math(w_refs[pi][...], g, m_refs[pi][...], v_refs[pi][...])
            o = out_refs[4 * pi:4 * pi + 4]
            o[0][...], o[1][...], o[2][...], o[3][...] = g, d, m2, v2

    vm = pl.BlockSpec(memory_space=pltpu.VMEM)
    outs = [jax.ShapeDtypeStruct((1, 128), F32)]
    for name in names:
        outs += [jax.ShapeDtypeStruct(w[name].shape, F32)] * 4
    res = pl.pallas_call(
        body, name="adam_small", in_specs=[vm] * (4 + 3 * n), out_specs=[vm] * len(outs), out_shape=outs,
        compiler_params=pltpu.CompilerParams(vmem_limit_bytes=VMEM_LIMIT),
    )(*red, *[w[k] for k in names], *[m[k] for k in names], *[v[k] for k in names])
    return res[0], {name: res[1 + 4 * i:5 + 4 * i] for i, name in enumerate(names)}


_WEIGHTS = ("g_mix", "w_in", "q_norm_g", "k_norm_g", "rec_conv_w", "rec_conv_b", "w_rg", "b_rg", "w_ig", "b_ig",
            "lru_lambda", "g_attn_out", "g_rec_out", "w_out", "g_ffn", "w_up", "ffn_conv_w", "ffn_conv_b", "w_down")
_BIG = ("w_in", "w_out", "w_up", "w_down")
_BIG_PERM = {"w_in": False, "w_out": False, "w_up": True, "w_down": False}
_SMALL_2D = {"w_rg": (RW, HD), "w_ig": (RW, HD), "b_rg": (8, HD), "b_ig": (8, HD), "rec_conv_w": (4, 128),
             "ffn_conv_w": (3, FC)}


def _halves(a):
    r, c = a.shape
    return a.reshape(2, r // 2, c)


def kernel(x, positions, g_mix, w_in, q_norm_g, k_norm_g, rec_conv_w, rec_conv_b, w_rg, b_rg, w_ig, b_ig, lru_lambda, g_attn_out, g_rec_out, w_out, g_ffn, w_up, ffn_conv_w, ffn_conv_b, w_down, loss_target, m_g_mix, m_w_in, m_q_norm_g, m_k_norm_g, m_rec_conv_w, m_rec_conv_b, m_w_rg, m_b_rg, m_w_ig, m_b_ig, m_lru_lambda, m_g_attn_out, m_g_rec_out, m_w_out, m_g_ffn, m_w_up, m_ffn_conv_w, m_ffn_conv_b, m_w_down, v_g_mix, v_w_in, v_q_norm_g, v_k_norm_g, v_rec_conv_w, v_rec_conv_b, v_w_rg, v_b_rg, v_w_ig, v_b_ig, v_lru_lambda, v_g_attn_out, v_g_rec_out, v_w_out, v_g_ffn, v_w_up, v_ffn_conv_w, v_ffn_conv_b, v_w_down):
    given = dict(g_mix=g_mix, w_in=w_in, q_norm_g=q_norm_g, k_norm_g=k_norm_g, rec_conv_w=rec_conv_w, rec_conv_b=rec_conv_b, w_rg=w_rg, b_rg=b_rg, w_ig=w_ig, b_ig=b_ig, lru_lambda=lru_lambda, g_attn_out=g_attn_out, g_rec_out=g_rec_out, w_out=w_out, g_ffn=g_ffn, w_up=w_up, ffn_conv_w=ffn_conv_w, ffn_conv_b=ffn_conv_b, w_down=w_down)
    given_m = dict(g_mix=m_g_mix, w_in=m_w_in, q_norm_g=m_q_norm_g, k_norm_g=m_k_norm_g, rec_conv_w=m_rec_conv_w, rec_conv_b=m_rec_conv_b, w_rg=m_w_rg, b_rg=m_b_rg, w_ig=m_w_ig, b_ig=m_b_ig, lru_lambda=m_lru_lambda, g_attn_out=m_g_attn_out, g_rec_out=m_g_rec_out, w_out=m_w_out, g_ffn=m_g_ffn, w_up=m_w_up, ffn_conv_w=m_ffn_conv_w, ffn_conv_b=m_ffn_conv_b, w_down=m_w_down)
    given_v = dict(g_mix=v_g_mix, w_in=v_w_in, q_norm_g=v_q_norm_g, k_norm_g=v_k_norm_g, rec_conv_w=v_rec_conv_w, rec_conv_b=v_rec_conv_b, w_rg=v_w_rg, b_rg=v_b_rg, w_ig=v_w_ig, b_ig=v_b_ig, lru_lambda=v_lru_lambda, g_attn_out=v_g_attn_out, g_rec_out=v_g_rec_out, w_out=v_w_out, g_ffn=v_g_ffn, w_up=v_w_up, ffn_conv_w=v_ffn_conv_w, ffn_conv_b=v_ffn_conv_b, w_down=v_w_down)
    shapes = {n: a.shape for n, a in given.items()}

    def two_d(n, a):
        a = a[0]
        return a.reshape(_SMALL_2D[n]) if n in _SMALL_2D else (a if a.ndim == 2 else a[None])

    w = {n: two_d(n, a) for n, a in given.items()}
    m = {n: two_d(n, a) for n, a in given_m.items()}
    v = {n: two_d(n, a) for n, a in given_v.items()}
    cc = lax.axis_index("c").astype(jnp.int32)
    cx, cy = lax.axis_index("x").astype(jnp.int32), lax.axis_index("y").astype(jnp.int32)
    slot = {False: 2 * cx + cy, True: 2 * cy + cx}

    shards = {n: _halves(_cast_bf16(f"cast_{n}", w[n])) for n in _BIG}
    small = [(jnp.pad(w["ffn_conv_w"], ((0, 5), (0, 0))), True), (jnp.pad(w["rec_conv_w"], ((0, 4), (0, 0))), False)]
    f_in, f_fcw, f_rcw = _gather_weights([(shards["w_in"], False)], small, slot)
    p = {n: w[n] for n in ("g_mix", "g_ffn", "q_norm_g", "k_norm_g", "rec_conv_b", "lru_lambda", "g_attn_out", "g_rec_out")}
    p.update(w_rg=w["w_rg"].reshape(8, HD, HD), w_ig=w["w_ig"].reshape(8, HD, HD), b_rg=w["b_rg"], b_ig=w["b_ig"],
             w_in=f_in.reshape(NCHIP, D, INW // NCHIP), ffn_conv_w=f_fcw,
             ffn_conv_b=jnp.concatenate([w["ffn_conv_b"][:, FC * s:FC * (s + 1)] for s in (0, 2, 1, 3)], axis=1),
             rec_conv_w=f_rcw.transpose(1, 0, 2).reshape(8, RW))

    class Exchange:
        rest = ("w_out", "w_up", "w_down")
        order = []
        flight = {}

        def start_rest(self):
            srcs = [shards[n] for n in self.rest]
            lands = [lax.dynamic_update_slice(lax.empty((NCHIP,) + s.shape, BF16), s[None], (slot[_BIG_PERM[n]], 0, 0, 0))
                     for n, s in zip(self.rest, srcs)]
            plan = _gather_plan([_BIG_PERM[n] for n in self.rest])
            send, recv, srcs, lands, token = _split_start("gather_rest_start", srcs, lands, plan, 6 * len(srcs))
            self.flight["rest"] = (send, recv, srcs, lands, plan)
            return (token,)

        def wait_rest(self, after):
            send, recv, srcs, lands, plan = self.flight.pop("rest")
            f_out, f_up, f_down = _split_wait("gather_rest_wait", send, recv, srcs, lands, plan, after)
            return dict(w_out=f_out.reshape(D, D), w_up=f_up.reshape(NCHIP, D, FC), w_down=f_down.reshape(DFF, D))

        def reduce_start(self, name, g32, g16):
            r2, cols = shards[name].shape[1:]
            plan = _reduce_plan(_BIG_PERM[name])
            send, recv, srcs, lands, token = _split_start(
                f"reduce_{name}_start", [g16.reshape(NCHIP, 2, r2, cols)], [lax.empty((7, r2, cols), BF16)], plan, 7)
            self.flight[name] = (send, recv, srcs, lands, plan, g32.reshape(NCHIP, 2, r2, cols))
            self.order.append(name)
            return (token,)

        def finish(self, after):
            mine = {}
            for name in self.order:
                send, recv, srcs, lands, plan, g32 = self.flight.pop(name)
                (got,) = _split_wait(f"reduce_{name}_wait", send, recv, srcs, lands, plan, after)
                where = jnp.stack([slot[_BIG_PERM[name]], cc])
                mine[name] = after = _add_pieces(f"reduce_{name}_add", g32, got, where)
            theirs = dict(zip(_BIG, _sibling_share([mine[n] for n in _BIG])))
            return mine, theirs

    exch = Exchange()

    loss_blk, grad_x, g = _local_step(x[0], positions.reshape(T, 1), loss_target[0], p, exch)

    out_g, out_d, out_m, out_v = {}, {}, {}, {}
    red = _small_allreduce(g, loss_blk)
    loss_row, small_out = _adam_small(red, w, m, v)
    for n, (gn, dn, mn, vn) in small_out.items():
        out_g[n], out_d[n], out_m[n], out_v[n] = gn, dn, mn, vn

    mine, theirs = exch.finish(red[0])
    for n in _BIG:
        out_g[n], out_d[n], out_m[n], out_v[n] = _adam_big(f"adam_{n}", w[n], mine[n], theirs[n], m[n], v[n], cc.reshape(1))

    outs = [loss_row[0, 0], grad_x[None]]
    for group in (out_g, out_d, out_m, out_v):
        outs += [group[n].reshape(shapes[n]) for n in _WEIGHTS]
    return tuple(outs)
```

```python
import math

import jax
import jax.numpy as jnp
import numpy as np
from jax import lax
from jax.experimental import pallas as pl
from jax.experimental.pallas import tpu as pltpu

F32 = jnp.float32
BF16 = jnp.bfloat16

T = 4096
D = 1024
HD = 64
AW = 512
RW = 512
INW = 2560
DFF = 3072
NCHIP = 4
EPS = 1e-6
NEG = -1e30
LRU_C = 8.0
ROPE_THETA = 10000.0
BLK = 128
DILATIONS = (1, 4, 16)
ADAM_LR, ADAM_B1, ADAM_B2, ADAM_EPS, ADAM_WD, ADAM_STEP = 0.001, 0.9, 0.999, 1e-08, 0.01, 10
VMEM_LIMIT = 56 * 1024 * 1024
MESH = pl.DeviceIdType.MESH

NN = (((1,), (0,)), ((), ()))
NT = (((1,), (1,)), ((), ()))
TN = (((0,), (0,)), ((), ()))


def _cp(*sem):
    return pltpu.CompilerParams(dimension_semantics=sem, vmem_limit_bytes=VMEM_LIMIT)


def _bs(shape, fn):
    return pl.BlockSpec(shape, fn)


def _dot(a, b, dims=NN):
    return lax.dot_general(a, b, dims, preferred_element_type=F32)


_GC = math.sqrt(2.0 / math.pi)


def _gelu(x):
    return x * (0.5 + 0.5 * jnp.tanh(x * (_GC + (_GC * 0.044715) * (x * x))))


def _gelu_and_grad(x):
    x2 = x * x
    th = jnp.tanh(x * (_GC + (_GC * 0.044715) * x2))
    cdf = 0.5 + 0.5 * th
    dg = cdf + (x * (1.0 - th * th)) * ((0.5 * _GC) + (1.5 * 0.044715 * _GC) * x2)
    return x * cdf, dg


def _softplus(x):
    e = jnp.exp(-jnp.abs(x))
    u = 1.0 + e
    l1p = jnp.where(u == 1.0, e, jnp.log(u) * (e / (u - 1.0)))
    return jnp.maximum(x, 0.0) + l1p


def _segsum(z, e_bf16):
    hi = z.astype(BF16)
    lo = (z - hi.astype(F32)).astype(BF16)
    parts = []
    for c0 in range(0, z.shape[1], 128):
        parts.append(_dot(hi[:, c0:c0 + 128], e_bf16) + _dot(lo[:, c0:c0 + 128], e_bf16))
    return jnp.concatenate(parts, axis=1)


def _mm(name, a, b, mode, tm, tn, out_dtype=F32, res=None, stack=0, twin_bf16=False, after=(), a_full=False,
        b_full=False):
    if mode == "nn":
        (m, k), n = a.shape, (b.shape[1] if not stack else stack * b.shape[2])
        a_spec = _bs((tm, k), lambda j, i: (i, 0))
        if stack:
            per = b.shape[2] // tn
            b_spec = _bs((None, k, tn), lambda j, i: (j // per, 0, j % per))
        else:
            b_spec = _bs((k, tn), lambda j, i: (0, j))
    elif mode == "nt":
        (m, k), n = a.shape, (b.shape[0] if not stack else b.shape[1])
        a_spec = _bs((tm, k), lambda j, i: (i, 0))
        b_spec = _bs((stack, tn, k // stack), lambda j, i: (0, j, 0)) if stack else _bs((tn, k), lambda j, i: (j, 0))
    else:
        (k, m), n = a.shape, b.shape[1]
        a_spec, b_spec = _bs((k, tm), lambda j, i: (0, i)), _bs((k, tn), lambda j, i: (0, j))
    assert m % tm == 0 and n % tn == 0
    o_spec = _bs((tm, tn), lambda j, i: (i, j))
    o_shape = (m, n)
    if mode == "tn" and stack:
        per = n // stack // tn
        o_spec = _bs((None, tm, tn), lambda j, i: (j // per, i, j % per))
        o_shape = (stack, m, n // stack)
    dims = {"nn": NN, "nt": NT, "tn": TN}[mode]
    once = pl.Buffered(1)
    if a_full:
        a_spec = pl.BlockSpec(a.shape, lambda j, i: (0, 0), pipeline_mode=once)
    if b_full:
        assert n == tn
        b_spec = pl.BlockSpec(b_spec.block_shape, b_spec.index_map, pipeline_mode=once)

    def product(a_ref, b_ref):
        if a_full:
            mine = pl.ds(pl.multiple_of(pl.program_id(1) * tm, tm), tm)
            take = (lambda cols: a_ref[:, mine]) if mode == "tn" else (lambda cols: a_ref[mine, cols])
        else:
            take = lambda cols: a_ref[:, cols]
        if mode == "nt" and stack:
            cs = k // stack
            acc = _dot(take(pl.ds(0, cs)), b_ref[0], NT)
            for s in range(1, stack):
                acc = acc + _dot(take(pl.ds(s * cs, cs)), b_ref[s], NT)
            return acc
        return _dot(take(slice(None)), b_ref[...], dims)

    nres = 0 if res is None else 1

    def body(a_ref, b_ref, *rest):
        acc = product(a_ref, b_ref)
        if nres:
            acc = rest[0][...] + acc
        outs = rest[nres + len(after):]
        outs[0][...] = acc.astype(out_dtype)
        if twin_bf16:
            outs[1][...] = acc.astype(BF16)

    ins = (a, b) + ((res,) if nres else ()) + tuple(after)
    specs = [a_spec, b_spec] + ([o_spec] if nres else []) + [pl.BlockSpec(memory_space=pl.ANY)] * len(after)
    shapes = [jax.ShapeDtypeStruct(o_shape, out_dtype)] + ([jax.ShapeDtypeStruct(o_shape, BF16)] if twin_bf16 else [])
    out = pl.pallas_call(
        body, name=name, grid=(n // tn, m // tm), in_specs=specs, out_specs=[o_spec] * len(shapes),
        out_shape=shapes, compiler_params=_cp("parallel", "parallel"),
    )(*ins)
    return tuple(out) if twin_bf16 else out[0]


def _rms_fwd(name, x, g):
    tr = 512

    def body(x_ref, g_ref, o_ref):
        xv = x_ref[...]
        r = lax.rsqrt(jnp.mean(xv * xv, axis=-1, keepdims=True) + EPS)
        o_ref[...] = ((xv * r) * g_ref[...]).astype(BF16)

    return pl.pallas_call(
        body, name=name, grid=(T // tr,), in_specs=[_bs((tr, D), lambda i: (i, 0)), _bs((1, D), lambda i: (0, 0))],
        out_specs=_bs((tr, D), lambda i: (i, 0)), out_shape=jax.ShapeDtypeStruct((T, D), BF16),
        compiler_params=_cp("parallel"),
    )(x, g)


def _rms_bwd(name, x, g, dy, dres, want_bf16, after=()):
    tr = 256
    halves = dy.ndim == 3

    def body(x_ref, g_ref, dy_ref, dr_ref, *rest):
        rest = rest[len(after):]
        dx_ref, rest = rest[0], rest[1:]
        dg_ref = rest[-1]
        xv = x_ref[...]
        dyv = dy_ref[0] + dy_ref[1] if halves else dy_ref[...]
        r = lax.rsqrt(jnp.mean(xv * xv, axis=-1, keepdims=True) + EPS)
        gdy = g_ref[...] * dyv
        dx = r * gdy - xv * ((r * r * r) * jnp.mean(xv * gdy, axis=-1, keepdims=True)) + dr_ref[...]
        dx_ref[...] = dx
        if want_bf16:
            rest[0][...] = dx.astype(BF16)

        @pl.when(pl.program_id(0) == 0)
        def _():
            dg_ref[...] = jnp.zeros_like(dg_ref)

        dg_ref[...] += jnp.sum(dyv * (xv * r), axis=0, keepdims=True)

    row = _bs((tr, D), lambda i: (i, 0))
    vec = _bs((1, D), lambda i: (0, 0))
    outs = [jax.ShapeDtypeStruct((T, D), F32)] + ([jax.ShapeDtypeStruct((T, D), BF16)] if want_bf16 else [])
    dy_spec = _bs((2, tr, D), lambda i: (0, i, 0)) if halves else row
    return pl.pallas_call(
        body, name=name, grid=(T // tr,),
        in_specs=[row, vec, dy_spec, row] + [pl.BlockSpec(memory_space=pl.ANY)] * len(after),
        out_specs=[row] * len(outs) + [vec], out_shape=outs + [jax.ShapeDtypeStruct((1, D), F32)],
        compiler_params=_cp("arbitrary"),
    )(x, g, dy, dres, *after)


def _head_ones():
    idx = np.arange(128) // HD
    return jnp.asarray((idx[:, None] == idx[None, :]).astype(np.float32), dtype=BF16)


def _freq_row():
    half = HD // 2
    inv = ROPE_THETA ** (-(np.arange(half, dtype=np.float64)) / half)
    return jnp.asarray(np.tile(inv, 4)[None, :], dtype=F32)


def _rot_tables(cos128, sin128):
    c = jnp.tile(cos128, (1, 4))
    s = jnp.tile(sin128, (1, 4))
    lane = lax.broadcasted_iota(jnp.int32, (1, AW), 1)
    first = (lane & 32) == 0
    return c, jnp.where(first, -s, s), first


def _swap_halves(y, first):
    return jnp.where(first, pltpu.roll(y, AW - 32, 1), pltpu.roll(y, 32, 1))


def _qk_prep(proj, pos_col, qg, kg):
    tr = 512

    def body(q_ref, k_ref, pos_ref, f_ref, qg_ref, kg_ref, e_ref, qo_ref, ko_ref, cos_ref, sin_ref):
        ang = pos_ref[...].astype(F32) * f_ref[...]
        cos_ref[...] = jnp.cos(ang)
        sin_ref[...] = jnp.sin(ang)
        c, s_signed, first = _rot_tables(cos_ref[...], sin_ref[...])
        e = e_ref[...]

        def norm_rot(xv, g, scale):
            r = lax.rsqrt(_segsum(xv * xv, e) * (1.0 / HD) + EPS)
            y = (xv * r) * g
            return (y * c + _swap_halves(y, first) * s_signed) * scale

        qo_ref[...] = norm_rot(q_ref[...], qg_ref[...], HD ** -0.5)
        ko_ref[...] = norm_rot(k_ref[...], kg_ref[...], 1.0)

    col = lambda j: _bs((tr, AW), lambda i, j=j: (i, j))
    vec = _bs((1, AW), lambda i: (0, 0))
    out = jax.ShapeDtypeStruct((T, AW), F32)
    tab = jax.ShapeDtypeStruct((T, 128), F32)
    tspec = _bs((tr, 128), lambda i: (i, 0))
    return pl.pallas_call(
        body, name="qk_prep", grid=(T // tr,),
        in_specs=[col(0), col(1), _bs((tr, 1), lambda i: (i, 0)), _bs((1, 128), lambda i: (0, 0)), vec, vec,
                  _bs((128, 128), lambda i: (0, 0))],
        out_specs=[col(0)] * 2 + [tspec] * 2, out_shape=[out, out, tab, tab], compiler_params=_cp("parallel"),
    )(proj, proj, pos_col, _freq_row(), qg, kg, _head_ones())


def _qk_bwd(proj, cos_t, sin_t, qg, kg, dq, dk, dv):
    tr = 256

    def body(q_ref, k_ref, cos_ref, sin_ref, qg_ref, kg_ref, e_ref, dq_ref, dk_ref, dv_ref, o_ref, dqg_ref, dkg_ref):
        i, j = pl.program_id(0), pl.program_id(1)

        @pl.when((i == 0) & (j == 0))
        def _():
            dqg_ref[...] = jnp.zeros_like(dqg_ref)
            dkg_ref[...] = jnp.zeros_like(dkg_ref)

        def norm_rot_bwd(x_ref, g_ref, dg_ref, d_ref, scale):
            c, s_signed, first = _rot_tables(cos_ref[...], sin_ref[...])
            e = e_ref[...]
            dout = d_ref[...] * scale
            dy = dout * c + _swap_halves(dout * s_signed, first)
            xv, g = x_ref[...], g_ref[...]
            r = lax.rsqrt(_segsum(xv * xv, e) * (1.0 / HD) + EPS)
            gdy = g * dy
            dx = r * gdy - xv * ((r * r * r) * (_segsum(xv * gdy, e) * (1.0 / HD)))
            o_ref[...] = dx.astype(BF16)
            dg_ref[...] += jnp.sum(dy * (xv * r), axis=0, keepdims=True)

        @pl.when(j == 0)
        def _():
            norm_rot_bwd(q_ref, qg_ref, dqg_ref, dq_ref, HD ** -0.5)

        @pl.when(j == 1)
        def _():
            norm_rot_bwd(k_ref, kg_ref, dkg_ref, dk_ref, 1.0)

        @pl.when(j == 2)
        def _():
            o_ref[...] = dv_ref[...].astype(BF16)

    col = lambda jj: _bs((tr, AW), lambda i, j, jj=jj: (i, jj))
    vec = _bs((1, AW), lambda i, j: (0, 0))
    piece = _bs((tr, AW), lambda i, j: (i, 0))
    return pl.pallas_call(
        body, name="qk_bwd", grid=(T // tr, 3),
        in_specs=[col(0), col(1), _bs((tr, 128), lambda i, j: (i, 0)), _bs((tr, 128), lambda i, j: (i, 0)), vec, vec,
                  _bs((128, 128), lambda i, j: (0, 0))] + [piece] * 3,
        out_specs=[_bs((tr, AW), lambda i, j: (i, j)), vec, vec],
        out_shape=[jax.ShapeDtypeStruct((T, INW), BF16), jax.ShapeDtypeStruct((1, AW), F32),
                   jax.ShapeDtypeStruct((1, AW), F32)],
        compiler_params=_cp("arbitrary", "arbitrary"),
    )(proj, proj, cos_t, sin_t, qg, kg, _head_ones(), dq, dk, dv)


RG = 256
QC = 64


def _stacked_band_mask(rows=2 * BLK, q0=0):
    qi = (lax.broadcasted_iota(jnp.int32, (rows, 2 * BLK), 0) + q0) & (BLK - 1)
    kj = lax.broadcasted_iota(jnp.int32, (rows, 2 * BLK), 1)
    rel = qi - kj + BLK
    return (rel >= 0) & (rel <= BLK), lax.broadcasted_iota(jnp.int32, (1, 2 * BLK), 1) >= BLK


def _natural_rows(r0, n_rows, d):
    if d == 1:
        return pl.ds(r0, n_rows)
    ln = T // d
    return pl.ds(r0 // ln + d * (r0 % ln), n_rows, stride=d)


def _regroup_into(dst, src_ref, d, pad, cast=True):
    def step(j, carry):
        r0 = pl.multiple_of(j * RG, RG)
        val = src_ref[_natural_rows(r0, RG, d), :]
        dst[pl.ds(pad + r0, RG), :] = val.astype(dst.dtype) if cast else val
        return carry
    lax.fori_loop(0, T // RG, step, 0)


def _stack_heads(x, h0):
    zero = jnp.zeros_like(x)
    return jnp.concatenate([jnp.where(h0, x, zero), jnp.where(h0, zero, x)], axis=0)


def _attn_fwd(q, k, proj):
    nblk = T // BLK

    def body(q_ref, k_ref, v_ref, a_ref, lse_ref, qs, ks, vs, o0, o1, o2, l0, l1, l2, sb0, sb1):
        band, cur_half = _stacked_band_mask()
        h0 = lax.broadcasted_iota(jnp.int32, (1, 128), 1) < HD
        ks[0:BLK, :] = jnp.zeros((BLK, 128), BF16)
        vs[0:BLK, :] = jnp.zeros((BLK, 128), BF16)
        for d, o_s, l_s in zip(DILATIONS, (o0, o1, o2), (l0, l1, l2)):
            nb = T // d // BLK
            _regroup_into(qs, q_ref, d, 0)
            _regroup_into(ks, k_ref, d, BLK)
            _regroup_into(vs, v_ref, d, BLK)

            def scores(b):
                r0 = pl.multiple_of(b * BLK, BLK)
                return _dot(_stack_heads(qs[pl.ds(r0, BLK), :], h0), ks[pl.ds(r0, 2 * BLK), :], NT)

            def finish(b, s_raw, d=d, nb=nb, o_s=o_s, l_s=l_s):
                r0 = pl.multiple_of(b * BLK, BLK)
                mask = band & (cur_half | ((b & (nb - 1)) > 0))
                s = jnp.where(mask, s_raw, NEG)
                m = jnp.max(s, axis=1, keepdims=True)
                p = jnp.exp(s - m)
                l = jnp.sum(p, axis=1, keepdims=True)
                o = _dot(p.astype(BF16), vs[pl.ds(r0, 2 * BLK), :]) / l
                lse = m + jnp.log(l)
                rows = _natural_rows(r0, BLK, d)
                o_s[rows, :] = jnp.where(h0, o[0:BLK, :], o[BLK:, :])
                l_s[rows, :] = jnp.where(h0, lse[0:BLK, :], lse[BLK:, :])

            sb0[...] = scores(0)

            def step(i, carry):
                b = 2 * i
                sb1[...] = scores(b + 1)
                finish(b, sb0[...])
                sb0[...] = scores(jnp.minimum(b + 2, nblk - 1))
                finish(b + 1, sb1[...])
                return carry

            lax.fori_loop(0, nblk // 2, step, 0)

        def merge(i, carry):
            r = pl.ds(pl.multiple_of(i * RG, RG), RG)
            la, lb, lc = l0[r, :], l1[r, :], l2[r, :]
            m = jnp.maximum(jnp.maximum(la, lb), lc)
            ea, eb, ec = jnp.exp(la - m), jnp.exp(lb - m), jnp.exp(lc - m)
            z = (ea + eb) + ec
            a_ref[r, :] = ((ea * o0[r, :] + eb * o1[r, :]) + ec * o2[r, :]) / z
            lse_ref[r, :] = m + jnp.log(z)
            return carry

        lax.fori_loop(0, T // RG, merge, 0)

    spec = lambda cb: _bs((T, 128), lambda p, cb=cb: (0, cb + p))
    out = jax.ShapeDtypeStruct((T, AW), F32)
    return pl.pallas_call(
        body, name="attn_fwd", grid=(AW // 128,), in_specs=[spec(0), spec(0), spec(8)], out_specs=[spec(0)] * 2,
        out_shape=[out] * 2,
        scratch_shapes=[pltpu.VMEM((T, 128), BF16), pltpu.VMEM((T + BLK, 128), BF16), pltpu.VMEM((T + BLK, 128), BF16)]
        + [pltpu.VMEM((T, 128), F32)] * 6 + [pltpu.VMEM((2 * BLK, 2 * BLK), F32)] * 2,
        compiler_params=_cp("parallel"),
    )(q, k, proj)


def _attn_bwd(q, k, proj, do, lse, delta):
    nblk = T // BLK

    def body(q_ref, k_ref, v_ref, do_ref, l_ref, dl_ref, dq_ref, dk_ref, dv_ref, qs, dos, ks, vs, ls, dls, dks, dvs,
             sa0, sa1, da0, da1):
        band, cur_half = _stacked_band_mask()
        h0 = lax.broadcasted_iota(jnp.int32, (1, 128), 1) < HD
        ks[0:BLK, :] = jnp.zeros((BLK, 128), BF16)
        vs[0:BLK, :] = jnp.zeros((BLK, 128), BF16)
        for d in DILATIONS:
            nb = T // d // BLK
            _regroup_into(qs, q_ref, d, 0)
            _regroup_into(dos, do_ref, d, 0)
            _regroup_into(ks, k_ref, d, BLK)
            _regroup_into(vs, v_ref, d, BLK)
            _regroup_into(ls, l_ref, d, 0, cast=False)
            _regroup_into(dls, dl_ref, d, 0, cast=False)
            dks[...] = jnp.zeros_like(dks)
            dvs[...] = jnp.zeros_like(dvs)

            def scores(b, s_buf, dp_buf):
                r0 = pl.multiple_of(b * BLK, BLK)
                win = pl.ds(r0, 2 * BLK)
                s_buf[...] = _dot(_stack_heads(qs[pl.ds(r0, BLK), :], h0), ks[win, :], NT)
                dp_buf[...] = _dot(_stack_heads(dos[pl.ds(r0, BLK), :], h0), vs[win, :], NT)

            def finish(b, s_buf, dp_buf, d=d, nb=nb):
                r0 = pl.multiple_of(b * BLK, BLK)
                mask = band & (cur_half | ((b & (nb - 1)) > 0))
                win = pl.ds(r0, 2 * BLK)
                lv, dlv = ls[pl.ds(r0, BLK), :], dls[pl.ds(r0, BLK), :]
                lse2 = jnp.concatenate([lv[:, 0:1], lv[:, HD:HD + 1]], axis=0)
                dl2 = jnp.concatenate([dlv[:, 0:1], dlv[:, HD:HD + 1]], axis=0)
                p = jnp.exp(jnp.where(mask, s_buf[...], NEG) - lse2)
                ds = p * (dp_buf[...] - dl2)
                pb, dsb = p.astype(BF16), ds.astype(BF16)
                dq2 = _dot(dsb, ks[win, :])
                dks[win, :] += _dot(dsb, _stack_heads(qs[pl.ds(r0, BLK), :], h0), TN)
                dvs[win, :] += _dot(pb, _stack_heads(dos[pl.ds(r0, BLK), :], h0), TN)
                rows = _natural_rows(r0, BLK, d)
                dq = jnp.where(h0, dq2[0:BLK, :], dq2[BLK:, :])
                dq_ref[rows, :] = dq if d == 1 else dq_ref[rows, :] + dq

            scores(0, sa0, da0)

            def step(i, carry):
                b = 2 * i
                scores(b + 1, sa1, da1)
                finish(b, sa0, da0)
                scores(jnp.minimum(b + 2, nblk - 1), sa0, da0)
                finish(b + 1, sa1, da1)
                return carry

            lax.fori_loop(0, nblk // 2, step, 0)

            def back(j, carry, d=d):
                r0 = pl.multiple_of(j * RG, RG)
                rows = _natural_rows(r0, RG, d)
                src = pl.ds(BLK + r0, RG)
                dk_ref[rows, :] = dks[src, :] if d == 1 else dk_ref[rows, :] + dks[src, :]
                dv_ref[rows, :] = dvs[src, :] if d == 1 else dv_ref[rows, :] + dvs[src, :]
                return carry

            lax.fori_loop(0, T // RG, back, 0)

    spec = lambda cb: _bs((T, 128), lambda p, cb=cb: (0, cb + p))
    ospec = _bs((T, 128), lambda p: (0, p))
    out = jax.ShapeDtypeStruct((T, AW), F32)
    return pl.pallas_call(
        body, name="attn_bwd", grid=(AW // 128,), in_specs=[spec(0), spec(0), spec(8), spec(0), spec(0), spec(0)],
        out_specs=[ospec] * 3, out_shape=[out] * 3,
        scratch_shapes=[pltpu.VMEM((T, 128), BF16), pltpu.VMEM((T, 128), BF16), pltpu.VMEM((T + BLK, 128), BF16),
                        pltpu.VMEM((T + BLK, 128), BF16), pltpu.VMEM((T, 128), F32), pltpu.VMEM((T, 128), F32),
                        pltpu.VMEM((T + BLK, 128), F32), pltpu.VMEM((T + BLK, 128), F32)]
        + [pltpu.VMEM((2 * BLK, 2 * BLK), F32)] * 4,
        compiler_params=_cp("parallel"),
    )(q, k, proj, do, lse, delta)


def _attn_norm(attn, g_attn):
    tr = 512

    def body(a_ref, g_ref, mix_ref):
        attn = a_ref[...]
        r = lax.rsqrt(jnp.mean(attn * attn, axis=-1, keepdims=True) + EPS)
        mix_ref[...] = ((attn * r) * g_ref[...]).astype(BF16)

    row = _bs((tr, AW), lambda i: (i, 0))
    return pl.pallas_call(
        body, name="attn_norm", grid=(T // tr,), in_specs=[row, _bs((1, AW), lambda i: (0, 0))],
        out_specs=row, out_shape=jax.ShapeDtypeStruct((T, D), BF16), compiler_params=_cp("parallel"),
    )(attn, g_attn)


def _attn_out_bwd(attn, dmix, g_attn):
    tr = 256

    def body(a_ref, d_ref, g_ref, e_ref, do_ref, dl_ref, dg_ref):
        av, dyv = a_ref[...], d_ref[...]
        r = lax.rsqrt(jnp.mean(av * av, axis=-1, keepdims=True) + EPS)
        gdy = g_ref[...] * dyv
        da = r * gdy - av * ((r * r * r) * jnp.mean(av * gdy, axis=-1, keepdims=True))
        do_ref[...] = da
        dl_ref[...] = _segsum(da * av, e_ref[...])

        @pl.when(pl.program_id(0) == 0)
        def _():
            dg_ref[...] = jnp.zeros_like(dg_ref)

        dg_ref[...] += jnp.sum(dyv * (av * r), axis=0, keepdims=True)

    row = _bs((tr, AW), lambda i: (i, 0))
    vec = _bs((1, AW), lambda i: (0, 0))
    return pl.pallas_call(
        body, name="attn_out_bwd", grid=(T // tr,), in_specs=[row, row, vec, _bs((128, 128), lambda i: (0, 0))],
        out_specs=[row, row, vec],
        out_shape=[jax.ShapeDtypeStruct((T, AW), F32), jax.ShapeDtypeStruct((T, AW), F32),
                   jax.ShapeDtypeStruct((1, AW), F32)],
        compiler_params=_cp("arbitrary"),
    )(attn, dmix, g_attn, _head_ones())


TRR = 256


def _scan_fwd(a, u):
    n = a.shape[0]
    row = lax.broadcasted_iota(jnp.int32, (n, 1), 0)
    s = 1
    while s < n:
        keep = row >= s
        u = jnp.where(keep, a * pltpu.roll(u, s, 0) + u, u)
        a = jnp.where(keep, a * pltpu.roll(a, s, 0), a)
        s *= 2
    return a, u


def _scan_bwd(c, w):
    n = c.shape[0]
    row = lax.broadcasted_iota(jnp.int32, (n, 1), 0)
    s = 1
    while s < n:
        keep = row < n - s
        w = jnp.where(keep, c * pltpu.roll(w, n - s, 0) + w, w)
        c = jnp.where(keep, c * pltpu.roll(c, n - s, 0), c)
        s *= 2
    return w


def _gates(xc, wrg, wig, brg, big, sp):
    xcb = xc.astype(BF16)
    r = jax.nn.sigmoid(_dot(xcb, wrg) + brg)
    ig = jax.nn.sigmoid(_dot(xcb, wig) + big)
    la = (-LRU_C * r) * sp
    a = jnp.exp(la)
    mult = jnp.sqrt(-jnp.tanh(la) * (a * a + 1.0))
    return r, ig, a, mult


def _conv4(ext_ref, xr, cw_ref, cb_ref, n):
    y = cb_ref[...] + ext_ref[pl.ds(5, n), :] * cw_ref[0:1, :]
    y = y + ext_ref[pl.ds(6, n), :] * cw_ref[1:2, :]
    y = y + ext_ref[pl.ds(7, n), :] * cw_ref[2:3, :]
    return y + xr * cw_ref[3:4, :]


def _rec_fwd(proj, mix, cw, cb, wrg, wig, brg, big, lam, g_rec):
    n = TRR

    def body(xr_ref, gr_ref, cw_ref, cb_ref, wrg_ref, wig_ref, brg_ref, big_ref, lam_ref, g_ref, mix_in,
             mix_ref, h_ref, ext, hcar):
        del mix_in

        @pl.when(pl.program_id(0) == 0)
        def _():
            ext[0:8, :] = jnp.zeros((8, RW), F32)
            hcar[...] = jnp.zeros_like(hcar)

        xr = xr_ref[...]
        ext[8:, :] = xr
        xc = _conv4(ext, xr, cw_ref, cb_ref, n)
        ext[0:8, :] = xr[n - 8:, :]
        sp = _softplus(-lam_ref[...])
        _, ig, a, mult = _gates(xc, wrg_ref[...], wig_ref[...], brg_ref[...], big_ref[...], sp)
        a_s, u_s = _scan_fwd(a, mult * (ig * xc))
        h = u_s + a_s * hcar[7:8, :]
        h_ref[...] = h
        hcar[...] = h[n - 8:, :]
        pre = h * _gelu(gr_ref[...])
        r = lax.rsqrt(jnp.mean(pre * pre, axis=-1, keepdims=True) + EPS)
        mix_ref[...] = ((pre * r) * g_ref[...]).astype(BF16)

    vec = _bs((1, RW), lambda i: (0, 0))
    mat = _bs((RW, RW), lambda i: (0, 0))
    return pl.pallas_call(
        body, name="rec_fwd", grid=(T // n,),
        in_specs=[_bs((n, RW), lambda i: (i, 3)), _bs((n, RW), lambda i: (i, 4)), _bs((8, RW), lambda i: (0, 0)), vec,
                  mat, mat, vec, vec, vec, vec, pl.BlockSpec(memory_space=pl.ANY)],
        out_specs=[_bs((n, RW), lambda i: (i, 1)), _bs((n, RW), lambda i: (i, 0))],
        out_shape=[jax.ShapeDtypeStruct((T, D), BF16), jax.ShapeDtypeStruct((T, RW), F32)],
        scratch_shapes=[pltpu.VMEM((n + 8, RW), F32), pltpu.VMEM((8, RW), F32)],
        input_output_aliases={10: 0}, compiler_params=_cp("arbitrary"),
    )(proj, proj, cw, cb, wrg, wig, brg, big, lam, g_rec, mix)


def _rec_bwd(proj, h, dmix, dproj, cw, cb, wrg, wig, brg, big, lam, g_rec):
    n = TRR
    nt = T // n
    hb = n // 8

    def body(xr_ref, xh_ref, gr_ref, h_ref, hh_ref, dm_ref, cw_ref, cb_ref, wrg_ref, wig_ref, brg_ref, big_ref,
             lam_ref, g_ref, dp_in, dp_ref, xc_ref, dr_ref, di_ref, dcw_ref, dcb_ref, dbr_ref, dbi_ref, dsp_ref,
             dg_ref, ext, exth, extd, adh, dgr_s):
        del dp_in
        i, j = pl.program_id(0), pl.program_id(1)
        first_tile = i == nt - 1
        last_tile = i == 0

        @pl.when(j == 0)
        def _():
            @pl.when(last_tile)
            def _():
                for ref in (dcw_ref, dcb_ref, dbr_ref, dbi_ref, dsp_ref, dg_ref):
                    ref[...] = jnp.zeros_like(ref)
                extd[n:, :] = jnp.zeros((8, RW), F32)
                adh[...] = jnp.zeros_like(adh)

            row = lax.broadcasted_iota(jnp.int32, (n, 1), 0)
            xr = xr_ref[...]
            ext[0:8, :] = jnp.where(first_tile, 0.0, xh_ref[...])
            ext[8:, :] = xr
            xc = _conv4(ext, xr, cw_ref, cb_ref, n)
            sp = _softplus(-lam_ref[...])
            wrg, wig = wrg_ref[...], wig_ref[...]
            r, ig, a, mult = _gates(xc, wrg, wig, brg_ref[...], big_ref[...], sp)

            hv = h_ref[...]
            gl, dgl = _gelu_and_grad(gr_ref[...])
            pre = hv * gl
            dyv = dm_ref[...]
            rr = lax.rsqrt(jnp.mean(pre * pre, axis=-1, keepdims=True) + EPS)
            gdy = g_ref[...] * dyv
            dpre = rr * gdy - pre * ((rr * rr * rr) * jnp.mean(pre * gdy, axis=-1, keepdims=True))
            dg_ref[...] += jnp.sum(dyv * (pre * rr), axis=0, keepdims=True)
            dgr_s[...] = dpre * hv * dgl

            is_last_row = row == n - 1
            w = dpre * gl + jnp.where(is_last_row, adh[0:1, :], 0.0)
            c = jnp.where(is_last_row, 0.0, pltpu.roll(a, n - 1, 0))
            dh = _scan_bwd(c, w)
            adh[...] = (a * dh)[0:8, :]

            exth[0:8, :] = jnp.where(first_tile, 0.0, hh_ref[...])
            exth[8:, :] = hv
            da = dh * exth[pl.ds(7, n), :]
            ixc = ig * xc
            dmult = dh * ixc
            dla = da * a - dmult * ((a * a) / mult)
            dsp_ref[...] += jnp.sum(dla * (-LRU_C * r), axis=0, keepdims=True)
            dpr = (dla * (-LRU_C * sp)) * (r * (1.0 - r))
            dpi = (dh * (mult * xc)) * (ig * (1.0 - ig))
            dprb, dpib = dpr.astype(BF16), dpi.astype(BF16)
            dxc = dh * (mult * ig) + _dot(dprb, wrg, NT) + _dot(dpib, wig, NT)
            dbr_ref[...] += jnp.sum(dpr, axis=0, keepdims=True)
            dbi_ref[...] += jnp.sum(dpi, axis=0, keepdims=True)
            xc_ref[...] = xc.astype(BF16)
            dr_ref[...] = dprb
            di_ref[...] = dpib

            extd[0:n, :] = dxc
            dxr = dxc * cw_ref[3:4, :] + extd[pl.ds(1, n), :] * cw_ref[2:3, :]
            dxr = dxr + extd[pl.ds(2, n), :] * cw_ref[1:2, :] + extd[pl.ds(3, n), :] * cw_ref[0:1, :]
            extd[n:, :] = dxc[0:8, :]
            dcb_ref[...] += jnp.sum(dxc, axis=0, keepdims=True)
            for kk in range(4):
                dcw_ref[kk:kk + 1, :] += jnp.sum(dxc * ext[pl.ds(5 + kk, n), :], axis=0, keepdims=True)

            @pl.when(first_tile)
            def _():
                dsp_ref[...] = dsp_ref[...] * (-jax.nn.sigmoid(-lam_ref[...]))

            dp_ref[...] = dxr.astype(BF16)

        @pl.when(j == 1)
        def _():
            dp_ref[...] = dgr_s[...].astype(BF16)

    vec = _bs((1, RW), lambda i, j: (0, 0))
    mat = _bs((RW, RW), lambda i, j: (0, 0))
    tile = lambda cblk: _bs((n, RW), lambda i, j, cblk=cblk: (nt - 1 - i, cblk))
    halo = lambda cblk: _bs((8, RW), lambda i, j, cblk=cblk: (jnp.maximum((nt - 1 - i) * hb - 1, 0), cblk))
    bt = jax.ShapeDtypeStruct((T, RW), BF16)
    v = jax.ShapeDtypeStruct((1, RW), F32)
    return pl.pallas_call(
        body, name="rec_bwd", grid=(nt, 2),
        in_specs=[tile(3), halo(3), tile(4), tile(0), halo(0), tile(1), _bs((8, RW), lambda i, j: (0, 0)), vec,
                  mat, mat, vec, vec, vec, vec, pl.BlockSpec(memory_space=pl.ANY)],
        out_specs=[_bs((n, RW), lambda i, j: (nt - 1 - i, 3 + j)), tile(0), tile(0), tile(0),
                   _bs((8, RW), lambda i, j: (0, 0)), vec, vec, vec, vec, vec],
        out_shape=[jax.ShapeDtypeStruct((T, INW), BF16), bt, bt, bt, jax.ShapeDtypeStruct((8, RW), F32), v, v, v, v, v],
        scratch_shapes=[pltpu.VMEM((n + 8, RW), F32), pltpu.VMEM((n + 8, RW), F32), pltpu.VMEM((n + 8, RW), F32),
                        pltpu.VMEM((8, RW), F32), pltpu.VMEM((n, RW), F32)],
        input_output_aliases={14: 0}, compiler_params=_cp("arbitrary", "arbitrary"),
    )(proj, proj, proj, h, h, dmix, cw, cb, wrg, wig, brg, big, lam, g_rec, dproj)


FC = 1536
TRF = 256


LC = 128


def _taps(x_ref, edge, cols, r):
    if r == 0:
        return edge[pl.ds(6, 8), cols], edge[pl.ds(7, 8), cols], edge[pl.ds(8, 8), cols]
    return x_ref[pl.ds(r - 2, 8), cols], x_ref[pl.ds(r - 1, 8), cols], x_ref[pl.ds(r, 8), cols]


def _ffn_act(up_pre, cw, cb):
    n = TRF
    hb = n // 8

    def body(g_ref, gh_ref, u_ref, uh_ref, wg_ref, wu_ref, bg_ref, bu_ref, o_ref, eg, eu):
        first = pl.program_id(1) == 0
        eg[0:8, :] = jnp.where(first, 0.0, gh_ref[...])
        eg[8:, :] = g_ref[0:8, :]
        eu[0:8, :] = jnp.where(first, 0.0, uh_ref[...])
        eu[8:, :] = u_ref[0:8, :]

        def column(ci, carry):
            cols = pl.ds(pl.multiple_of(ci * LC, LC), LC)
            rows8 = lambda v: jnp.broadcast_to(v, (8, LC))
            wg = [rows8(wg_ref[kk:kk + 1, cols]) for kk in range(3)]
            wu = [rows8(wu_ref[kk:kk + 1, cols]) for kk in range(3)]
            bg, bu = rows8(bg_ref[:, cols]), rows8(bu_ref[:, cols])
            for r in range(0, n, 16):
                res = []
                for rr in (r, r + 8):
                    g0, g1, g2 = _taps(g_ref, eg, cols, rr)
                    u0, u1, u2 = _taps(u_ref, eu, cols, rr)
                    ug = ((bg + g0 * wg[0]) + g1 * wg[1]) + g2 * wg[2]
                    uu = ((bu + u0 * wu[0]) + u1 * wu[1]) + u2 * wu[2]
                    res.append(_gelu(ug) * uu)
                o_ref[pl.ds(r, 16), cols] = jnp.concatenate(res, axis=0).astype(BF16)
            return carry

        lax.fori_loop(0, FC // LC, column, 0)

    main = lambda o: _bs((n, FC), lambda j, i, o=o: (i, 2 * j + o))
    halo = lambda o: _bs((8, FC), lambda j, i, o=o: (jnp.maximum(i * hb - 1, 0), 2 * j + o))
    wsp = lambda o: _bs((None, 8, FC), lambda j, i, o=o: (2 * j + o, 0, 0))
    bsp = lambda o: _bs((1, FC), lambda j, i, o=o: (0, 2 * j + o))
    return pl.pallas_call(
        body, name="ffn_act", grid=(2, T // n),
        in_specs=[main(0), halo(0), main(1), halo(1), wsp(0), wsp(1), bsp(0), bsp(1)],
        out_specs=_bs((n, FC), lambda j, i: (i, j)), out_shape=jax.ShapeDtypeStruct((T, DFF), BF16),
        scratch_shapes=[pltpu.VMEM((16, FC), F32)] * 2, compiler_params=_cp("parallel", "parallel"),
    )(up_pre, up_pre, up_pre, up_pre, cw, cw, cb, cb)


def _up_act(h2, w_up, cw, cb):
    n = TRF
    nt = T // n
    pw = 256
    npc = FC // pw

    def body(h_ref, wg_ref, wu_ref, cwg_ref, cwu_ref, bg_ref, bu_ref, up_ref, a_ref, hx, gb0, gb1, ub0, ub1):
        i = pl.program_id(1)
        halo = h_ref[pl.ds(pl.multiple_of(jnp.maximum(i * n - 16, 0), 16), 16), :]
        hx[0:16, :] = jnp.where(i == 0, jnp.zeros_like(halo), halo)
        hx[16:, :] = h_ref[pl.ds(pl.multiple_of(i * n, n), n), :]
        gbufs, ubufs = (gb0, gb1), (ub0, ub1)

        def dots(c):
            hv = hx[...]
            gbufs[c % 2][...] = _dot(hv, wg_ref[:, c * pw:(c + 1) * pw])
            ubufs[c % 2][...] = _dot(hv, wu_ref[:, c * pw:(c + 1) * pw])

        def chain(c):
            gb, ub = gbufs[c % 2], ubufs[c % 2]
            up_ref[:, c * pw:(c + 1) * pw] = gb[16:, :]
            up_ref[:, FC + c * pw:FC + (c + 1) * pw] = ub[16:, :]
            rows8 = lambda v: jnp.broadcast_to(v, (8, LC))
            for sub in range(pw // LC):
                lc = slice(sub * LC, (sub + 1) * LC)
                cols = slice(c * pw + sub * LC, c * pw + (sub + 1) * LC)
                wg = [rows8(cwg_ref[kk:kk + 1, cols]) for kk in range(3)]
                wu = [rows8(cwu_ref[kk:kk + 1, cols]) for kk in range(3)]
                bg, bu = rows8(bg_ref[:, cols]), rows8(bu_ref[:, cols])
                for r in range(0, n, 16):
                    res = []
                    for rr in (16 + r, 24 + r):
                        ug = ((bg + gb[pl.ds(rr - 2, 8), lc] * wg[0]) + gb[pl.ds(rr - 1, 8), lc] * wg[1]) \
                            + gb[pl.ds(rr, 8), lc] * wg[2]
                        uu = ((bu + ub[pl.ds(rr - 2, 8), lc] * wu[0]) + ub[pl.ds(rr - 1, 8), lc] * wu[1]) \
                            + ub[pl.ds(rr, 8), lc] * wu[2]
                        res.append(_gelu(ug) * uu)
                    a_ref[pl.ds(r, 16), cols] = jnp.concatenate(res, axis=0).astype(BF16)

        dots(0)
        for c in range(npc):
            if c + 1 < npc:
                dots(c + 1)
            chain(c)

    wsl = lambda o: _bs((None, D, FC), lambda j, i, o=o: (2 * j + o, 0, 0))
    wsp = lambda o: _bs((None, 8, FC), lambda j, i, o=o: (2 * j + o, 0, 0))
    bsp = lambda o: _bs((1, FC), lambda j, i, o=o: (0, 2 * j + o))
    return pl.pallas_call(
        body, name="up_act", grid=(2, nt),
        in_specs=[pl.BlockSpec((T, D), lambda j, i: (0, 0), pipeline_mode=pl.Buffered(1)), wsl(0), wsl(1),
                  wsp(0), wsp(1), bsp(0), bsp(1)],
        out_specs=[_bs((n, 2 * FC), lambda j, i: (i, j)), _bs((n, FC), lambda j, i: (i, j))],
        out_shape=[jax.ShapeDtypeStruct((T, 2 * DFF), F32), jax.ShapeDtypeStruct((T, DFF), BF16)],
        scratch_shapes=[pltpu.VMEM((n + 16, D), BF16)] + [pltpu.VMEM((n + 16, pw), F32)] * 4,
        compiler_params=_cp("parallel", "arbitrary"),
    )(h2, w_up, w_up, cw, cw, cb, cb)


def _ffn_bwd(up_pre, dyb, w_down_t, cw, cb, after=()):
    n = TRF
    hb = n // 8
    nt = T // n
    m = n + 8
    pw = 256
    npc = FC // pw

    def body(g_ref, gp_ref, gn_ref, u_ref, up_ref, un_ref, dy_ref, wd_ref, wg_ref, wu_ref, bg_ref, bu_ref, *rest):
        o_ref, dw_ref, db_ref, eg0, eg1, eu0, eu1, dug_s, duu_s, dyx, db0, db1 = rest[len(after):]
        i = pl.program_id(1)
        first, last = i == 0, i == nt - 1

        @pl.when(first)
        def _():
            dw_ref[...] = jnp.zeros_like(dw_ref)
            db_ref[...] = jnp.zeros_like(db_ref)

        tail = dy_ref[pl.ds(pl.multiple_of(jnp.minimum((i + 1) * n, T - 16), 16), 16), :]
        dyx[0:n, :] = dy_ref[pl.ds(pl.multiple_of(i * n, n), n), :]
        dyx[n:, :] = jnp.where(last, jnp.zeros_like(tail), tail)
        dbufs = (db0, db1)

        def dots(c):
            dbufs[c % 2][...] = _dot(dyx[...], wd_ref[:, c * pw:(c + 1) * pw])

        eg0[0:8, :] = jnp.where(first, 0.0, gp_ref[...])
        eg0[8:, :] = g_ref[0:8, :]
        eg1[0:8, :] = g_ref[n - 8:, :]
        eg1[8:, :] = gn_ref[...]
        eu0[0:8, :] = jnp.where(first, 0.0, up_ref[...])
        eu0[8:, :] = u_ref[0:8, :]
        eu1[0:8, :] = u_ref[n - 8:, :]
        eu1[8:, :] = un_ref[...]

        def column(ci, dbuf, lc):
            cols = slice(ci * LC, (ci + 1) * LC)
            ucols = slice(FC + ci * LC, FC + (ci + 1) * LC)
            rows8 = lambda v: jnp.broadcast_to(v, (8, LC))
            wg = [rows8(wg_ref[kk:kk + 1, cols]) for kk in range(3)]
            wu = [rows8(wu_ref[kk:kk + 1, cols]) for kk in range(3)]
            bg, bu = rows8(bg_ref[:, cols]), rows8(bu_ref[:, cols])
            zero = jnp.zeros((8, LC), F32)
            acc = [zero] * 8
            for r in range(0, n + 8, 8):
                if r == n:
                    gt = (eg1[pl.ds(6, 8), cols], eg1[pl.ds(7, 8), cols], eg1[pl.ds(8, 8), cols])
                    ut = (eu1[pl.ds(6, 8), cols], eu1[pl.ds(7, 8), cols], eu1[pl.ds(8, 8), cols])
                else:
                    gt, ut = _taps(g_ref, eg0, cols, r), _taps(u_ref, eu0, cols, r)
                dv = dbuf[pl.ds(r, 8), lc]
                gl, dgl = _gelu_and_grad(((bg + gt[0] * wg[0]) + gt[1] * wg[1]) + gt[2] * wg[2])
                uu = ((bu + ut[0] * wu[0]) + ut[1] * wu[1]) + ut[2] * wu[2]
                dug, duu = dv * uu * dgl, dv * gl
                dug_s[pl.ds(r, 8), :] = dug
                duu_s[pl.ds(r, 8), :] = duu
                if r < n:
                    acc = [acc[0] + dug * gt[0], acc[1] + dug * gt[1], acc[2] + dug * gt[2],
                           acc[3] + duu * ut[0], acc[4] + duu * ut[1], acc[5] + duu * ut[2], acc[6] + dug, acc[7] + duu]
            for r in range(0, n, 16):
                og, ou = [], []
                for rr in (r, r + 8):
                    og.append((dug_s[pl.ds(rr, 8), :] * wg[2] + dug_s[pl.ds(rr + 1, 8), :] * wg[1])
                              + dug_s[pl.ds(rr + 2, 8), :] * wg[0])
                    ou.append((duu_s[pl.ds(rr, 8), :] * wu[2] + duu_s[pl.ds(rr + 1, 8), :] * wu[1])
                              + duu_s[pl.ds(rr + 2, 8), :] * wu[0])
                o_ref[pl.ds(r, 16), cols] = jnp.concatenate(og, axis=0).astype(BF16)
                o_ref[pl.ds(r, 16), ucols] = jnp.concatenate(ou, axis=0).astype(BF16)
            for kk in range(3):
                dw_ref[kk:kk + 1, cols] += jnp.sum(acc[kk], axis=0, keepdims=True)
                dw_ref[kk:kk + 1, ucols] += jnp.sum(acc[3 + kk], axis=0, keepdims=True)
            db_ref[:, cols] += jnp.sum(acc[6], axis=0, keepdims=True)
            db_ref[:, ucols] += jnp.sum(acc[7], axis=0, keepdims=True)

        dots(0)
        for c in range(npc):
            if c + 1 < npc:
                dots(c + 1)
            for sub in range(pw // LC):
                column(c * (pw // LC) + sub, dbufs[c % 2], slice(sub * LC, (sub + 1) * LC))

    main = lambda o: _bs((n, FC), lambda j, i, o=o: (i, 2 * j + o))
    prev = lambda o: _bs((8, FC), lambda j, i, o=o: (jnp.maximum(i * hb - 1, 0), 2 * j + o))
    nxt = lambda o: _bs((8, FC), lambda j, i, o=o: (jnp.minimum((i + 1) * hb, T // 8 - 1), 2 * j + o))
    wsp = lambda o: _bs((None, 8, FC), lambda j, i, o=o: (2 * j + o, 0, 0))
    bsp = lambda o: _bs((1, FC), lambda j, i, o=o: (0, 2 * j + o))
    return pl.pallas_call(
        body, name="ffn_bwd", grid=(2, nt),
        in_specs=[main(0), prev(0), nxt(0), main(1), prev(1), nxt(1),
                  pl.BlockSpec((T, D), lambda j, i: (0, 0), pipeline_mode=pl.Buffered(1)),
                  _bs((D, FC), lambda j, i: (0, j)), wsp(0), wsp(1), bsp(0), bsp(1)]
        + [pl.BlockSpec(memory_space=pl.ANY)] * len(after),
        out_specs=[_bs((n, 2 * FC), lambda j, i: (i, j)), _bs((8, 2 * FC), lambda j, i: (0, j)),
                   _bs((1, 2 * FC), lambda j, i: (0, j))],
        out_shape=[jax.ShapeDtypeStruct((T, 2 * DFF), BF16), jax.ShapeDtypeStruct((8, 2 * DFF), F32),
                   jax.ShapeDtypeStruct((1, 2 * DFF), F32)],
        scratch_shapes=[pltpu.VMEM((16, FC), F32)] * 4 + [pltpu.VMEM((m, LC), F32)] * 2
        + [pltpu.VMEM((n + 16, D), BF16)] + [pltpu.VMEM((n + 16, pw), F32)] * 2,
        compiler_params=_cp("parallel", "arbitrary"),
    )(up_pre, up_pre, up_pre, up_pre, up_pre, up_pre, dyb, w_down_t, cw, cw, cb, cb, *after)


def _down_loss(act, w_down, x1, target):
    tm, tn = 512, D

    def body(a_ref, b_ref, r_ref, t_ref, dy_ref, dyb_ref, l_ref):
        @pl.when((pl.program_id(0) == 0) & (pl.program_id(1) == 0))
        def _():
            l_ref[...] = jnp.zeros_like(l_ref)

        err = (r_ref[...] + _dot(a_ref[...], b_ref[...])) - t_ref[...]
        dy = err * (1.0 / D)
        dy_ref[...] = dy
        dyb_ref[...] = dy.astype(BF16)
        l_ref[...] += jnp.sum(0.5 * (err * err) * (1.0 / D))

    o_spec = _bs((tm, tn), lambda j, i: (i, j))
    return pl.pallas_call(
        body, name="down_loss", grid=(D // tn, T // tm),
        in_specs=[_bs((tm, DFF), lambda j, i: (i, 0)),
                  pl.BlockSpec((DFF, tn), lambda j, i: (0, j), pipeline_mode=pl.Buffered(1)), o_spec, o_spec],
        out_specs=[o_spec, o_spec, _bs((8, 128), lambda j, i: (0, 0))],
        out_shape=[jax.ShapeDtypeStruct((T, D), F32), jax.ShapeDtypeStruct((T, D), BF16),
                   jax.ShapeDtypeStruct((8, 128), F32)],
        compiler_params=_cp("arbitrary", "arbitrary"),
    )(act, w_down, x1, target)


def _block_diag(w):
    eye = jnp.eye(8, dtype=w.dtype)
    return (w[:, :, None, :] * eye[:, None, :, None]).reshape(RW, RW).astype(BF16)


def _diag_blocks(m):
    eye = jnp.eye(8, dtype=m.dtype)
    return (m.reshape(8, HD, 8, HD) * eye[:, None, :, None]).sum(axis=2)


def _local_step(x, pos_col, target, p, exch):
    qg, kg = jnp.tile(p["q_norm_g"], (1, 8)), jnp.tile(p["k_norm_g"], (1, 8))
    wrg, wig = _block_diag(p["w_rg"]), _block_diag(p["w_ig"])
    brg, big = p["b_rg"].reshape(1, RW), p["b_ig"].reshape(1, RW)

    h1 = _rms_fwd("rms1", x, p["g_mix"])
    p = {**p, **exch.wait_first(h1)}
    proj = _mm("mm_in", h1, p["w_in"], "nn", 512, 640, stack=NCHIP, after=exch.start_rest(), a_full=True)
    q, k, cos_t, sin_t = _qk_prep(proj, pos_col, qg, kg)
    attn, lse = _attn_fwd(q, k, proj)
    mix = _attn_norm(attn, p["g_attn_out"])
    mix, hseq = _rec_fwd(proj, mix, p["rec_conv_w"], p["rec_conv_b"], wrg, wig, brg, big, p["lru_lambda"], p["g_rec_out"])
    rest = exch.wait_rest(mix)
    x1 = _mm("mm_out", mix, rest["w_out"], "nn", 512, 512, res=x, a_full=True)
    h2 = _rms_fwd("rms2", x1, p["g_ffn"])
    up_pre, act = _up_act(h2, rest["w_up"], p["ffn_conv_w"], p["ffn_conv_b"])
    dy, dyb, loss_blk = _down_loss(act, rest["w_down"], x1, target)

    g = {}
    tok = exch.reduce_start("w_down", *_mm("wg_down", act, dyb, "tn", 512, 512, twin_bf16=True))
    dup, g["ffn_conv_w"], g["ffn_conv_b"] = _ffn_bwd(up_pre, dyb, rest["w_down"].T, p["ffn_conv_w"], p["ffn_conv_b"], tok)
    tok = exch.reduce_start("w_up", *_mm("wg_up", h2, dup, "tn", 512, 768, stack=NCHIP, twin_bf16=True, a_full=True))
    dh2 = _mm("dg_up", dup, rest["w_up"], "nt", 512, D, stack=NCHIP, after=tok, b_full=True)
    dx1, dx1b, g["g_ffn"] = _rms_bwd("rms2_bwd", x1, p["g_ffn"], dh2, dy, True)
    tok = exch.reduce_start("w_out", *_mm("wg_out", mix, dx1b, "tn", 512, 512, twin_bf16=True, a_full=True))
    dmix = _mm("dg_out", dx1b, rest["w_out"], "nt", 512, 512, after=tok, a_full=True)
    do, delta, g["g_attn_out"] = _attn_out_bwd(attn, dmix, p["g_attn_out"])
    dq, dk, dv = _attn_bwd(q, k, proj, do, lse, delta)
    dproj, dqg, dkg = _qk_bwd(proj, cos_t, sin_t, qg, kg, dq, dk, dv)
    (dproj, xcb, dprb, dpib, g["rec_conv_w"], g["rec_conv_b"], dbr, dbi, dsp, g["g_rec_out"]) = _rec_bwd(
        proj, hseq, dmix, dproj, p["rec_conv_w"], p["rec_conv_b"], wrg, wig, brg, big, p["lru_lambda"], p["g_rec_out"])
    g["w_rg"] = _diag_blocks(_mm("wg_rg", xcb, dprb, "tn", 512, 512)).reshape(RW, HD)
    g["w_ig"] = _diag_blocks(_mm("wg_ig", xcb, dpib, "tn", 512, 512)).reshape(RW, HD)
    g["b_rg"], g["b_ig"] = dbr.reshape(8, HD), dbi.reshape(8, HD)
    g["lru_lambda"] = dsp
    g["q_norm_g"] = dqg.reshape(8, HD).sum(axis=0, keepdims=True)
    g["k_norm_g"] = dkg.reshape(8, HD).sum(axis=0, keepdims=True)
    tok = exch.reduce_start("w_in", *_mm("wg_in", h1, dproj, "tn", 512, 640, stack=NCHIP, twin_bf16=True, a_full=True))
    dh1 = _mm("dg_in", dproj, p["w_in"], "nt", 512, 512, stack=NCHIP, after=tok)
    grad_x, g["g_mix"] = _rms_bwd("rms1_bwd", x, p["g_mix"], dh1, dx1, False)
    return loss_blk, grad_x, g


ANY = pl.BlockSpec(memory_space=pl.ANY)


def _mesh_pos():
    return lax.axis_index("x"), lax.axis_index("y"), lax.axis_index("c")


def _slot(px, py, perm):
    return 2 * py + px if perm else 2 * px + py


def _other_chips(x, y):
    return [(1 - x, y), (x, 1 - y), (1 - x, 1 - y)]


def _rcopy(src, dst, send, recv, k, to, kr=None):
    return pltpu.make_async_remote_copy(src_ref=src, dst_ref=dst, send_sem=send.at[k],
                                        recv_sem=recv.at[k if kr is None else kr], device_id=to, device_id_type=MESH)


def _cast_bf16(name, w, after=()):
    r, c = w.shape
    tr = 128

    def body(w_ref, *rest):
        rest[-1][...] = w_ref[...].astype(BF16)

    return pl.pallas_call(
        body, name=name, grid=(r // tr,), in_specs=[_bs((tr, c), lambda i: (i, 0))] + [ANY] * len(after),
        out_specs=_bs((tr, c), lambda i: (i, 0)), out_shape=jax.ShapeDtypeStruct((r, c), BF16),
        compiler_params=_cp("parallel"),
    )(w, *after)


def _sibling_fill(lands, perms):
    na = len(lands)

    def body(*refs):
        outs, (send, recv) = refs[na:2 * na], refs[2 * na:]
        x, y, c = _mesh_pos()
        cps = []
        for a in range(na):
            for j, (px, py) in enumerate(_other_chips(x, y)):
                mine = outs[a].at[_slot(px, py, perms[a]), c]
                cps.append(_rcopy(mine, mine, send, recv, 3 * a + j, (x, y, 1 - c)))
        for cp in cps:
            cp.start()
        for a in range(na):
            for j, (px, py) in enumerate(_other_chips(x, y)):
                got = outs[a].at[_slot(px, py, perms[a]), 1 - c]
                _rcopy(got, got, send, recv, 3 * a + j, (x, y, c)).wait_recv()
        for cp in cps:
            cp.wait_send()

    return pl.pallas_call(
        body, name="gather_fill", in_specs=[ANY] * na, out_specs=[ANY] * na,
        out_shape=[jax.ShapeDtypeStruct(a.shape, a.dtype) for a in lands],
        input_output_aliases={i: i for i in range(na)},
        scratch_shapes=[pltpu.SemaphoreType.DMA((3 * na,)), pltpu.SemaphoreType.DMA((3 * na,))],
    )(*lands)


HBM = pl.BlockSpec(memory_space=pltpu.HBM)
SEM = pl.BlockSpec(memory_space=pltpu.SEMAPHORE)
EFFECT = pltpu.SideEffectType.DATAFLOW_SIDE_EFFECTING


def _split_start(name, srcs, lands, plan, nsem):
    ns, nl = len(srcs), len(lands)

    def body(*refs):
        send, recv = refs[ns + nl], refs[ns + nl + 1]
        sends, _ = plan(refs[:ns], refs[ns:ns + nl], send, recv)
        for cp in sends:
            cp.start()
        refs[-1][...] = jnp.zeros((8, 128), F32)

    arrs = list(srcs) + list(lands)
    out = pl.pallas_call(
        body, name=name, in_specs=[HBM] * (ns + nl),
        out_specs=[SEM, SEM] + [HBM] * (ns + nl) + [pl.BlockSpec(memory_space=pltpu.VMEM)],
        out_shape=[pltpu.SemaphoreType.DMA((nsem,)), pltpu.SemaphoreType.DMA((nsem,))]
        + [pltpu.HBM(a.shape, a.dtype) for a in arrs] + [jax.ShapeDtypeStruct((8, 128), F32)],
        input_output_aliases={i: 2 + i for i in range(ns + nl)},
        compiler_params=pltpu.CompilerParams(has_side_effects=EFFECT),
    )(*[pltpu.with_memory_space_constraint(a, pltpu.HBM) for a in arrs])
    return out[0], out[1], out[2:2 + ns], out[2 + ns:2 + ns + nl], out[-1]


def _split_wait(name, send, recv, srcs, lands, plan, after):
    ns, nl = len(srcs), len(lands)

    def body(*refs):
        sends, recvs = plan(refs[:ns], refs[ns:ns + nl], refs[ns + nl], refs[ns + nl + 1])
        for cp in sends:
            cp.wait_send()
        for cp in recvs:
            cp.wait_recv()

    arrs = list(srcs) + list(lands)
    after = tuple(after) if isinstance(after, (tuple, list)) else (after,)
    out = pl.pallas_call(
        body, name=name, in_specs=[HBM] * (ns + nl) + [SEM, SEM] + [ANY] * len(after), out_specs=[HBM] * (ns + nl),
        out_shape=[pltpu.HBM(a.shape, a.dtype) for a in arrs],
        input_output_aliases={i: i for i in range(ns + nl)},
        compiler_params=pltpu.CompilerParams(has_side_effects=EFFECT),
    )(*arrs, send, recv, *after)
    return out[ns:]


def _gather_plan(perms):
    def plan(srcs, lands, send, recv):
        x, y, c = _mesh_pos()
        sends, recvs = [], []
        for a, perm in enumerate(perms):
            for j, (px, py) in enumerate(_other_chips(x, y)):
                for cc in (0, 1):
                    k = 6 * a + 2 * j + cc
                    sends.append(_rcopy(srcs[a].at[c], lands[a].at[_slot(x, y, perm), c], send, recv, k, (px, py, cc),
                                        kr=6 * a + 2 * j + c))
                    got = lands[a].at[_slot(px, py, perm), cc]
                    recvs.append(_rcopy(got, got, send, recv, k, (x, y, c)))
        return sends, recvs
    return plan


def _gather_half_plan(perms, halved):
    def plan(srcs, lands, send, recv):
        x, y, c = _mesh_pos()
        sends, recvs = [], []
        for a, perm in enumerate(perms):
            for j, (px, py) in enumerate(_other_chips(x, y)):
                k = 3 * a + j
                mine, theirs = _slot(x, y, perm), _slot(px, py, perm)
                if halved[a]:
                    sends.append(_rcopy(srcs[a].at[c], lands[a].at[mine, c], send, recv, k, (px, py, c)))
                    got = lands[a].at[theirs, c]
                else:
                    sends.append(_rcopy(srcs[a], lands[a].at[mine], send, recv, k, (px, py, c)))
                    got = lands[a].at[theirs]
                recvs.append(_rcopy(got, got, send, recv, k, (x, y, c)))
        return sends, recvs
    return plan


def _reduce_plan(perm):
    def plan(srcs, lands, send, recv):
        x, y, c = _mesh_pos()
        src, land = srcs[0], lands[0]
        sends = []
        for j, (px, py) in enumerate(_other_chips(x, y)):
            for hf in (0, 1):
                sends.append(_rcopy(src.at[_slot(px, py, perm), hf], land.at[2 * j + c], send, recv, 2 * j + hf,
                                    (px, py, hf), kr=2 * j + c))
        sends.append(_rcopy(src.at[_slot(x, y, perm), 1 - c], land.at[6], send, recv, 6, (x, y, 1 - c)))
        recvs = [_rcopy(land.at[i], land.at[i], send, recv, i, (x, y, c)) for i in range(7)]
        return sends, recvs
    return plan


def _sibling_share(rs):
    na = len(rs)

    def body(*refs):
        ins, outs, (send, recv) = refs[:na], refs[na:2 * na], refs[2 * na:]
        x, y, c = _mesh_pos()
        cps = [_rcopy(ins[a], outs[a], send, recv, a, (x, y, 1 - c)) for a in range(na)]
        for cp in cps:
            cp.start()
        for cp in cps:
            cp.wait()

    return pl.pallas_call(
        body, name="rs_share", in_specs=[ANY] * na, out_specs=[ANY] * na,
        out_shape=[jax.ShapeDtypeStruct(r.shape, F32) for r in rs],
        scratch_shapes=[pltpu.SemaphoreType.DMA((na,)), pltpu.SemaphoreType.DMA((na,))],
    )(*rs)


def _add_pieces(name, g, got, where):
    _, _, r2, cc = g.shape
    tr = 128

    def body(w_ref, g_ref, r_ref, o_ref):
        del w_ref
        acc = g_ref[...]
        for i in range(7):
            acc = acc + r_ref[i].astype(F32)
        o_ref[...] = acc

    return pl.pallas_call(
        body, name=name,
        grid_spec=pltpu.PrefetchScalarGridSpec(
            num_scalar_prefetch=1, grid=(r2 // tr,),
            in_specs=[_bs((None, None, tr, cc), lambda i, w_ref: (w_ref[0], w_ref[1], i, 0)),
                      _bs((7, tr, cc), lambda i, w_ref: (0, i, 0))],
            out_specs=_bs((tr, cc), lambda i, w_ref: (i, 0))),
        out_shape=jax.ShapeDtypeStruct((r2, cc), F32), compiler_params=_cp("parallel"),
    )(where, g, got)


def _adam_math(w, g, m, v):
    m = ADAM_B1 * m + (1.0 - ADAM_B1) * g
    v = ADAM_B2 * v + (1.0 - ADAM_B2) * (g * g)
    m_hat = m / (1.0 - ADAM_B1 ** ADAM_STEP)
    v_hat = v / (1.0 - ADAM_B2 ** ADAM_STEP)
    return -ADAM_LR * (m_hat / (jnp.sqrt(v_hat) + ADAM_EPS) + ADAM_WD * w), m, v


def _adam_big(name, w, g_mine, g_sib, m, v, c_arr):
    r, cols = w.shape
    tr = 128
    per = r // 2 // tr

    def body(c_ref, w_ref, a_ref, b_ref, m_ref, v_ref, g_ref, d_ref, m2_ref, v2_ref):
        g = jnp.where(pl.program_id(0) == c_ref[0], a_ref[...], b_ref[...])
        g_ref[...] = g
        d_ref[...], m2_ref[...], v2_ref[...] = _adam_math(w_ref[...], g, m_ref[...], v_ref[...])

    spec = _bs((tr, cols), lambda h, i, c_ref: (h * per + i, 0))
    half = _bs((tr, cols), lambda h, i, c_ref: (i, 0))
    out = jax.ShapeDtypeStruct((r, cols), F32)
    return pl.pallas_call(
        body, name=name,
        grid_spec=pltpu.PrefetchScalarGridSpec(
            num_scalar_prefetch=1, grid=(2, per), in_specs=[spec, half, half, spec, spec], out_specs=[spec] * 4),
        out_shape=[out] * 4, compiler_params=_cp("parallel", "parallel"),
    )(c_arr, w, g_mine, g_sib, m, v)


_CLASS_SHAPE = {"a": (8, D), "b": (8, RW), "c": (8, 2 * DFF), "d": (1048, HD)}
_SMALL = (
    ("g_mix", "a", 0, 1, D), ("g_ffn", "a", 1, 1, D),
    ("rec_conv_w", "b", 0, 4, RW), ("rec_conv_b", "b", 4, 1, RW), ("lru_lambda", "b", 5, 1, RW),
    ("g_attn_out", "b", 6, 1, RW), ("g_rec_out", "b", 7, 1, RW),
    ("ffn_conv_w", "c", 0, 3, 2 * DFF), ("ffn_conv_b", "c", 3, 1, 2 * DFF),
    ("w_rg", "d", 0, RW, HD), ("w_ig", "d", RW, RW, HD), ("b_rg", "d", 2 * RW, 8, HD), ("b_ig", "d", 2 * RW + 8, 8, HD),
    ("q_norm_g", "d", 2 * RW + 16, 1, HD), ("k_norm_g", "d", 2 * RW + 17, 1, HD),
)
_LOSS_ROW = 2
_CLASSES = ("a", "b", "c", "d")
_CLASS_OWNER = {"a": 0, "b": 0, "c": 0, "d": 1}


def _small_allreduce(g, loss_blk):
    names = [s[0] for s in _SMALL]
    nin = len(names) + 1

    def body(*refs):
        ins = dict(zip(names, refs[:len(names)]))
        loss_ref = refs[len(names)]
        outs = dict(zip(_CLASSES, refs[nin:nin + 4]))
        pair = dict(zip(_CLASSES, refs[nin + 4:nin + 8]))
        quad = dict(zip(_CLASSES, refs[nin + 8:nin + 12]))
        send, recv = refs[nin + 12:]
        x, y, c = _mesh_pos()
        chip = 2 * x + y
        pair["a"][c] = jnp.zeros(_CLASS_SHAPE["a"], F32)
        pair["b"][c] = ins["rec_conv_w"][...]
        pair["c"][c] = ins["ffn_conv_w"][...]
        pair["d"][c, 2 * RW + 16:, :] = jnp.zeros((8, HD), F32)
        for name, k, r0, nr, _ in _SMALL:
            if name in ("rec_conv_w", "ffn_conv_w"):
                continue
            pair[k][c, r0:r0 + nr, :] = ins[name][...]
        pair["a"][c, _LOSS_ROW:_LOSS_ROW + 1, :] = jnp.broadcast_to(loss_ref[0:1, 0:1], (1, D))
        cps = [_rcopy(pair[k].at[c], pair[k].at[c], send, recv, ki, (x, y, 1 - c)) for ki, k in enumerate(_CLASSES)]
        for cp in cps:
            cp.start()
        for ki, k in enumerate(_CLASSES):
            _rcopy(pair[k].at[1 - c], pair[k].at[1 - c], send, recv, ki, (x, y, c)).wait_recv()
            quad[k][chip] = pair[k][0] + pair[k][1]
        for cp in cps:
            cp.wait_send()
        for ki, k in enumerate(_CLASSES):
            owner = _CLASS_OWNER[k]

            @pl.when(c == owner)
            def _(ki=ki, k=k):
                cps2 = [_rcopy(quad[k].at[chip], quad[k].at[chip], send, recv, 4 + 3 * ki + j, (px, py, c))
                        for j, (px, py) in enumerate(_other_chips(x, y))]
                for cp in cps2:
                    cp.start()
                for j, (px, py) in enumerate(_other_chips(x, y)):
                    got = quad[k].at[2 * px + py]
                    _rcopy(got, got, send, recv, 4 + 3 * ki + j, (x, y, c)).wait_recv()
                outs[k][...] = ((quad[k][0] + quad[k][1]) + quad[k][2]) + quad[k][3]
                share = _rcopy(outs[k], outs[k], send, recv, 16 + ki, (x, y, 1 - c))
                share.start()
                for cp in cps2:
                    cp.wait_send()
                share.wait_send()

        for ki, k in enumerate(_CLASSES):
            @pl.when(c != _CLASS_OWNER[k])
            def _(ki=ki, k=k):
                _rcopy(outs[k], outs[k], send, recv, 16 + ki, (x, y, c)).wait_recv()

    vm = pl.BlockSpec(memory_space=pltpu.VMEM)
    return pl.pallas_call(
        body, name="small_allreduce", in_specs=[vm] * nin, out_specs=[vm] * 4,
        out_shape=[jax.ShapeDtypeStruct(_CLASS_SHAPE[k], F32) for k in _CLASSES],
        scratch_shapes=[pltpu.VMEM((2,) + _CLASS_SHAPE[k], F32) for k in _CLASSES]
        + [pltpu.VMEM((NCHIP,) + _CLASS_SHAPE[k], F32) for k in _CLASSES]
        + [pltpu.SemaphoreType.DMA((20,)), pltpu.SemaphoreType.DMA((20,))],
        compiler_params=pltpu.CompilerParams(vmem_limit_bytes=VMEM_LIMIT),
    )(*[g[n] for n in names], loss_blk)


def _adam_small(red, w, m, v):
    names = [s[0] for s in _SMALL]
    n = len(names)

    def body(*refs):
        red_refs = dict(zip(_CLASSES, refs[:4]))
        w_refs, m_refs, v_refs = refs[4:4 + n], refs[4 + n:4 + 2 * n], refs[4 + 2 * n:4 + 3 * n]
        loss_ref = refs[4 + 3 * n]
        out_refs = refs[5 + 3 * n:]
        x, y, _ = _mesh_pos()
        chip = 2 * x + y
        loss_ref[...] = jnp.broadcast_to(red_refs["a"][_LOSS_ROW:_LOSS_ROW + 1, 0:1], loss_ref.shape)
        for pi, (name, k, r0, nr, width) in enumerate(_SMALL):
            gfull = red_refs[k][r0:r0 + nr, :]
            if name == "rec_conv_w":
                parts = [gfull[:, 128 * s:128 * (s + 1)] for s in range(NCHIP)]
                g = jnp.where(chip == 0, parts[0], jnp.where(chip == 1, parts[1], jnp.where(chip == 2, parts[2], parts[3])))
            elif name == "ffn_conv_w":
                parts = [gfull[:, FC * s:FC * (s + 1)] for s in range(NCHIP)]
                g = jnp.where(chip == 0, parts[0], jnp.where(chip == 1, parts[2], jnp.where(chip == 2, parts[1], parts[3])))
            elif name == "ffn_conv_b":
                g = jnp.concatenate([gfull[:, FC * s:FC * (s + 1)] for s in (0, 2, 1, 3)], axis=1)
            else:
                g = gfull
            d, m2, v2 = _adam_math(w_refs[pi][...], g, m_refs[pi][...], v_refs[pi][...])
            o = out_refs[4 * pi:4 * pi + 4]
            o[0][...], o[1][...], o[2][...], o[3][...] = g, d, m2, v2

    vm = pl.BlockSpec(memory_space=pltpu.VMEM)
    outs = [jax.ShapeDtypeStruct((1, 128), F32)]
    for name in names:
        outs += [jax.ShapeDtypeStruct(w[name].shape, F32)] * 4
    res = pl.pallas_call(
        body, name="adam_small", in_specs=[vm] * (4 + 3 * n), out_specs=[vm] * len(outs), out_shape=outs,
        compiler_params=pltpu.CompilerParams(vmem_limit_bytes=VMEM_LIMIT),
    )(*red, *[w[k] for k in names], *[m[k] for k in names], *[v[k] for k in names])
    return res[0], {name: res[1 + 4 * i:5 + 4 * i] for i, name in enumerate(names)}


_WEIGHTS = ("g_mix", "w_in", "q_norm_g", "k_norm_g", "rec_conv_w", "rec_conv_b", "w_rg", "b_rg", "w_ig", "b_ig",
            "lru_lambda", "g_attn_out", "g_rec_out", "w_out", "g_ffn", "w_up", "ffn_conv_w", "ffn_conv_b", "w_down")
_BIG = ("w_in", "w_out", "w_up", "w_down")
_BIG_PERM = {"w_in": False, "w_out": False, "w_up": True, "w_down": False}
_SMALL_2D = {"w_rg": (RW, HD), "w_ig": (RW, HD), "b_rg": (8, HD), "b_ig": (8, HD), "rec_conv_w": (4, 128),
             "ffn_conv_w": (3, FC)}


def _halves(a):
    r, c = a.shape
    return a.reshape(2, r // 2, c)


def kernel(x, positions, g_mix, w_in, q_norm_g, k_norm_g, rec_conv_w, rec_conv_b, w_rg, b_rg, w_ig, b_ig, lru_lambda, g_attn_out, g_rec_out, w_out, g_ffn, w_up, ffn_conv_w, ffn_conv_b, w_down, loss_target, m_g_mix, m_w_in, m_q_norm_g, m_k_norm_g, m_rec_conv_w, m_rec_conv_b, m_w_rg, m_b_rg, m_w_ig, m_b_ig, m_lru_lambda, m_g_attn_out, m_g_rec_out, m_w_out, m_g_ffn, m_w_up, m_ffn_conv_w, m_ffn_conv_b, m_w_down, v_g_mix, v_w_in, v_q_norm_g, v_k_norm_g, v_rec_conv_w, v_rec_conv_b, v_w_rg, v_b_rg, v_w_ig, v_b_ig, v_lru_lambda, v_g_attn_out, v_g_rec_out, v_w_out, v_g_ffn, v_w_up, v_ffn_conv_w, v_ffn_conv_b, v_w_down):
    given = dict(g_mix=g_mix, w_in=w_in, q_norm_g=q_norm_g, k_norm_g=k_norm_g, rec_conv_w=rec_conv_w, rec_conv_b=rec_conv_b, w_rg=w_rg, b_rg=b_rg, w_ig=w_ig, b_ig=b_ig, lru_lambda=lru_lambda, g_attn_out=g_attn_out, g_rec_out=g_rec_out, w_out=w_out, g_ffn=g_ffn, w_up=w_up, ffn_conv_w=ffn_conv_w, ffn_conv_b=ffn_conv_b, w_down=w_down)
    given_m = dict(g_mix=m_g_mix, w_in=m_w_in, q_norm_g=m_q_norm_g, k_norm_g=m_k_norm_g, rec_conv_w=m_rec_conv_w, rec_conv_b=m_rec_conv_b, w_rg=m_w_rg, b_rg=m_b_rg, w_ig=m_w_ig, b_ig=m_b_ig, lru_lambda=m_lru_lambda, g_attn_out=m_g_attn_out, g_rec_out=m_g_rec_out, w_out=m_w_out, g_ffn=m_g_ffn, w_up=m_w_up, ffn_conv_w=m_ffn_conv_w, ffn_conv_b=m_ffn_conv_b, w_down=m_w_down)
    given_v = dict(g_mix=v_g_mix, w_in=v_w_in, q_norm_g=v_q_norm_g, k_norm_g=v_k_norm_g, rec_conv_w=v_rec_conv_w, rec_conv_b=v_rec_conv_b, w_rg=v_w_rg, b_rg=v_b_rg, w_ig=v_w_ig, b_ig=v_b_ig, lru_lambda=v_lru_lambda, g_attn_out=v_g_attn_out, g_rec_out=v_g_rec_out, w_out=v_w_out, g_ffn=v_g_ffn, w_up=v_w_up, ffn_conv_w=v_ffn_conv_w, ffn_conv_b=v_ffn_conv_b, w_down=v_w_down)
    shapes = {n: a.shape for n, a in given.items()}

    def two_d(n, a):
        a = a[0]
        return a.reshape(_SMALL_2D[n]) if n in _SMALL_2D else (a if a.ndim == 2 else a[None])

    w = {n: two_d(n, a) for n, a in given.items()}
    m = {n: two_d(n, a) for n, a in given_m.items()}
    v = {n: two_d(n, a) for n, a in given_v.items()}
    cc = lax.axis_index("c").astype(jnp.int32)
    cx, cy = lax.axis_index("x").astype(jnp.int32), lax.axis_index("y").astype(jnp.int32)
    slot = {False: 2 * cx + cy, True: 2 * cy + cx}

    shards = {"w_in": _halves(_cast_bf16("cast_w_in", w["w_in"]))}
    first = [shards["w_in"], jnp.pad(w["ffn_conv_w"], ((0, 5), (0, 0))), jnp.pad(w["rec_conv_w"], ((0, 4), (0, 0)))]
    first_perm = [False, True, False]
    first_plan = _gather_half_plan(first_perm, [True, False, False])
    in_flight = _split_start(
        "gather_in_start", first,
        [lax.dynamic_update_slice(lax.empty((NCHIP,) + a.shape, a.dtype), a[None], (slot[pm],) + (0,) * a.ndim)
         for a, pm in zip(first, first_perm)], first_plan, 3 * len(first))
    for n in ("w_out", "w_up", "w_down"):
        shards[n] = _halves(_cast_bf16(f"cast_{n}", w[n], after=(in_flight[4],)))
    p = {n: w[n] for n in ("g_mix", "g_ffn", "q_norm_g", "k_norm_g", "rec_conv_b", "lru_lambda", "g_attn_out", "g_rec_out")}
    p.update(w_rg=w["w_rg"].reshape(8, HD, HD), w_ig=w["w_ig"].reshape(8, HD, HD), b_rg=w["b_rg"], b_ig=w["b_ig"],
             ffn_conv_b=jnp.concatenate([w["ffn_conv_b"][:, FC * s:FC * (s + 1)] for s in (0, 2, 1, 3)], axis=1))

    class Exchange:
        rest = ("w_out", "w_up", "w_down")
        order = []
        flight = {}

        def wait_first(self, after):
            send, recv, srcs, lands, _ = in_flight
            f_in, f_fcw, f_rcw = _split_wait("gather_in_wait", send, recv, srcs, lands, first_plan,
                                             (after,) + tuple(shards[n] for n in self.rest))
            (f_in,) = _sibling_fill([f_in], [False])
            return dict(w_in=f_in.reshape(NCHIP, D, INW // NCHIP), ffn_conv_w=f_fcw,
                        rec_conv_w=f_rcw.transpose(1, 0, 2).reshape(8, RW))

        def start_rest(self):
            srcs = [shards[n] for n in self.rest]
            lands = [lax.dynamic_update_slice(lax.empty((NCHIP,) + s.shape, BF16), s[None], (slot[_BIG_PERM[n]], 0, 0, 0))
                     for n, s in zip(self.rest, srcs)]
            plan = _gather_plan([_BIG_PERM[n] for n in self.rest])
            send, recv, srcs, lands, token = _split_start("gather_rest_start", srcs, lands, plan, 6 * len(srcs))
            self.flight["rest"] = (send, recv, srcs, lands, plan)
            return (token,)

        def wait_rest(self, after):
            send, recv, srcs, lands, plan = self.flight.pop("rest")
            f_out, f_up, f_down = _split_wait("gather_rest_wait", send, recv, srcs, lands, plan, after)
            return dict(w_out=f_out.reshape(D, D), w_up=f_up.reshape(NCHIP, D, FC), w_down=f_down.reshape(DFF, D))

        def reduce_start(self, name, g32, g16):
            r2, cols = shards[name].shape[1:]
            plan = _reduce_plan(_BIG_PERM[name])
            send, recv, srcs, lands, token = _split_start(
                f"reduce_{name}_start", [g16.reshape(NCHIP, 2, r2, cols)], [lax.empty((7, r2, cols), BF16)], plan, 7)
            self.flight[name] = (send, recv, srcs, lands, plan, g32.reshape(NCHIP, 2, r2, cols))
            self.order.append(name)
            return (token,)

        def finish(self, after):
            mine = {}
            for name in self.order:
                send, recv, srcs, lands, plan, g32 = self.flight.pop(name)
                (got,) = _split_wait(f"reduce_{name}_wait", send, recv, srcs, lands, plan, after)
                where = jnp.stack([slot[_BIG_PERM[name]], cc])
                mine[name] = after = _add_pieces(f"reduce_{name}_add", g32, got, where)
            theirs = dict(zip(_BIG, _sibling_share([mine[n] for n in _BIG])))
            return mine, theirs

    exch = Exchange()

    loss_blk, grad_x, g = _local_step(x[0], positions.reshape(T, 1), loss_target[0], p, exch)

    out_g, out_d, out_m, out_v = {}, {}, {}, {}
    red = _small_allreduce(g, loss_blk)
    loss_row, small_out = _adam_small(red, w, m, v)
    for n, (gn, dn, mn, vn) in small_out.items():
        out_g[n], out_d[n], out_m[n], out_v[n] = gn, dn, mn, vn

    mine, theirs = exch.finish(red[0])
    for n in _BIG:
        out_g[n], out_d[n], out_m[n], out_v[n] = _adam_big(f"adam_{n}", w[n], mine[n], theirs[n], m[n], v[n], cc.reshape(1))

    outs = [loss_row[0, 0], grad_x[None]]
    for group in (out_g, out_d, out_m, out_v):
        outs += [group[n].reshape(shapes[n]) for n in _WEIGHTS]
    return tuple(outs)
```

```python
import math

import jax
import jax.numpy as jnp
import numpy as np
from jax import lax
from jax.experimental import pallas as pl
from jax.experimental.pallas import tpu as pltpu

F32 = jnp.float32
BF16 = jnp.bfloat16

T = 4096
D = 1024
HD = 64
AW = 512
RW = 512
INW = 2560
DFF = 3072
NCHIP = 4
EPS = 1e-6
NEG = -1e30
LRU_C = 8.0
ROPE_THETA = 10000.0
BLK = 128
DILATIONS = (1, 4, 16)
ADAM_LR, ADAM_B1, ADAM_B2, ADAM_EPS, ADAM_WD, ADAM_STEP = 0.001, 0.9, 0.999, 1e-08, 0.01, 10
VMEM_LIMIT = 56 * 1024 * 1024
MESH = pl.DeviceIdType.MESH

NN = (((1,), (0,)), ((), ()))
NT = (((1,), (1,)), ((), ()))
TN = (((0,), (0,)), ((), ()))


def _cp(*sem):
    return pltpu.CompilerParams(dimension_semantics=sem, vmem_limit_bytes=VMEM_LIMIT)


def _bs(shape, fn):
    return pl.BlockSpec(shape, fn)


def _dot(a, b, dims=NN):
    return lax.dot_general(a, b, dims, preferred_element_type=F32)


_GC = math.sqrt(2.0 / math.pi)


def _gelu(x):
    return x * (0.5 + 0.5 * jnp.tanh(x * (_GC + (_GC * 0.044715) * (x * x))))


def _gelu_and_grad(x):
    x2 = x * x
    th = jnp.tanh(x * (_GC + (_GC * 0.044715) * x2))
    cdf = 0.5 + 0.5 * th
    dg = cdf + (x * (1.0 - th * th)) * ((0.5 * _GC) + (1.5 * 0.044715 * _GC) * x2)
    return x * cdf, dg


def _softplus(x):
    e = jnp.exp(-jnp.abs(x))
    u = 1.0 + e
    l1p = jnp.where(u == 1.0, e, jnp.log(u) * (e / (u - 1.0)))
    return jnp.maximum(x, 0.0) + l1p


def _segsum(z, e_bf16):
    hi = z.astype(BF16)
    lo = (z - hi.astype(F32)).astype(BF16)
    parts = []
    for c0 in range(0, z.shape[1], 128):
        parts.append(_dot(hi[:, c0:c0 + 128], e_bf16) + _dot(lo[:, c0:c0 + 128], e_bf16))
    return jnp.concatenate(parts, axis=1)


def _mm(name, a, b, mode, tm, tn, out_dtype=F32, res=None, stack=0, twin_bf16=False, after=(), a_full=False,
        b_full=False):
    if mode == "nn":
        (m, k), n = a.shape, (b.shape[1] if not stack else stack * b.shape[2])
        a_spec = _bs((tm, k), lambda j, i: (i, 0))
        if stack:
            per = b.shape[2] // tn
            b_spec = _bs((None, k, tn), lambda j, i: (j // per, 0, j % per))
        else:
            b_spec = _bs((k, tn), lambda j, i: (0, j))
    elif mode == "nt":
        (m, k), n = a.shape, (b.shape[0] if not stack else b.shape[1])
        a_spec = _bs((tm, k), lambda j, i: (i, 0))
        b_spec = _bs((stack, tn, k // stack), lambda j, i: (0, j, 0)) if stack else _bs((tn, k), lambda j, i: (j, 0))
    else:
        (k, m), n = a.shape, b.shape[1]
        a_spec, b_spec = _bs((k, tm), lambda j, i: (0, i)), _bs((k, tn), lambda j, i: (0, j))
    assert m % tm == 0 and n % tn == 0
    o_spec = _bs((tm, tn), lambda j, i: (i, j))
    o_shape = (m, n)
    if mode == "tn" and stack:
        per = n // stack // tn
        o_spec = _bs((None, tm, tn), lambda j, i: (j // per, i, j % per))
        o_shape = (stack, m, n // stack)
    dims = {"nn": NN, "nt": NT, "tn": TN}[mode]
    once = pl.Buffered(1)
    if a_full:
        a_spec = pl.BlockSpec(a.shape, lambda j, i: (0, 0), pipeline_mode=once)
    if b_full:
        assert n == tn
        b_spec = pl.BlockSpec(b_spec.block_shape, b_spec.index_map, pipeline_mode=once)

    def product(a_ref, b_ref):
        if a_full:
            mine = pl.ds(pl.multiple_of(pl.program_id(1) * tm, tm), tm)
            take = (lambda cols: a_ref[:, mine]) if mode == "tn" else (lambda cols: a_ref[mine, cols])
        else:
            take = lambda cols: a_ref[:, cols]
        if mode == "nt" and stack:
            cs = k // stack
            acc = _dot(take(pl.ds(0, cs)), b_ref[0], NT)
            for s in range(1, stack):
                acc = acc + _dot(take(pl.ds(s * cs, cs)), b_ref[s], NT)
            return acc
        return _dot(take(slice(None)), b_ref[...], dims)

    nres = 0 if res is None else 1

    def body(a_ref, b_ref, *rest):
        acc = product(a_ref, b_ref)
        if nres:
            acc = rest[0][...] + acc
        outs = rest[nres + len(after):]
        outs[0][...] = acc.astype(out_dtype)
        if twin_bf16:
            outs[1][...] = acc.astype(BF16)

    ins = (a, b) + ((res,) if nres else ()) + tuple(after)
    specs = [a_spec, b_spec] + ([o_spec] if nres else []) + [pl.BlockSpec(memory_space=pl.ANY)] * len(after)
    shapes = [jax.ShapeDtypeStruct(o_shape, out_dtype)] + ([jax.ShapeDtypeStruct(o_shape, BF16)] if twin_bf16 else [])
    out = pl.pallas_call(
        body, name=name, grid=(n // tn, m // tm), in_specs=specs, out_specs=[o_spec] * len(shapes),
        out_shape=shapes, compiler_params=_cp("parallel", "parallel"),
    )(*ins)
    return tuple(out) if twin_bf16 else out[0]


def _rms_fwd(name, x, g):
    tr = 512

    def body(x_ref, g_ref, o_ref):
        xv = x_ref[...]
        r = lax.rsqrt(jnp.mean(xv * xv, axis=-1, keepdims=True) + EPS)
        o_ref[...] = ((xv * r) * g_ref[...]).astype(BF16)

    return pl.pallas_call(
        body, name=name, grid=(T // tr,), in_specs=[_bs((tr, D), lambda i: (i, 0)), _bs((1, D), lambda i: (0, 0))],
        out_specs=_bs((tr, D), lambda i: (i, 0)), out_shape=jax.ShapeDtypeStruct((T, D), BF16),
        compiler_params=_cp("parallel"),
    )(x, g)


def _rms_bwd(name, x, g, dy, dres, want_bf16, after=()):
    tr = 256
    halves = dy.ndim == 3

    def body(x_ref, g_ref, dy_ref, dr_ref, *rest):
        rest = rest[len(after):]
        dx_ref, rest = rest[0], rest[1:]
        dg_ref = rest[-1]
        xv = x_ref[...]
        dyv = dy_ref[0] + dy_ref[1] if halves else dy_ref[...]
        r = lax.rsqrt(jnp.mean(xv * xv, axis=-1, keepdims=True) + EPS)
        gdy = g_ref[...] * dyv
        dx = r * gdy - xv * ((r * r * r) * jnp.mean(xv * gdy, axis=-1, keepdims=True)) + dr_ref[...]
        dx_ref[...] = dx
        if want_bf16:
            rest[0][...] = dx.astype(BF16)

        @pl.when(pl.program_id(0) == 0)
        def _():
            dg_ref[...] = jnp.zeros_like(dg_ref)

        dg_ref[...] += jnp.sum(dyv * (xv * r), axis=0, keepdims=True)

    row = _bs((tr, D), lambda i: (i, 0))
    vec = _bs((1, D), lambda i: (0, 0))
    outs = [jax.ShapeDtypeStruct((T, D), F32)] + ([jax.ShapeDtypeStruct((T, D), BF16)] if want_bf16 else [])
    dy_spec = _bs((2, tr, D), lambda i: (0, i, 0)) if halves else row
    return pl.pallas_call(
        body, name=name, grid=(T // tr,),
        in_specs=[row, vec, dy_spec, row] + [pl.BlockSpec(memory_space=pl.ANY)] * len(after),
        out_specs=[row] * len(outs) + [vec], out_shape=outs + [jax.ShapeDtypeStruct((1, D), F32)],
        compiler_params=_cp("arbitrary"),
    )(x, g, dy, dres, *after)


def _head_ones():
    idx = np.arange(128) // HD
    return jnp.asarray((idx[:, None] == idx[None, :]).astype(np.float32), dtype=BF16)


def _freq_row():
    half = HD // 2
    inv = ROPE_THETA ** (-(np.arange(half, dtype=np.float64)) / half)
    return jnp.asarray(np.tile(inv, 4)[None, :], dtype=F32)


def _rot_tables(cos128, sin128):
    c = jnp.tile(cos128, (1, 4))
    s = jnp.tile(sin128, (1, 4))
    lane = lax.broadcasted_iota(jnp.int32, (1, AW), 1)
    first = (lane & 32) == 0
    return c, jnp.where(first, -s, s), first


def _swap_halves(y, first):
    return jnp.where(first, pltpu.roll(y, AW - 32, 1), pltpu.roll(y, 32, 1))


def _qk_prep(proj, pos_col, qg, kg):
    tr = 512

    def body(q_ref, k_ref, pos_ref, f_ref, qg_ref, kg_ref, e_ref, qo_ref, ko_ref, cos_ref, sin_ref):
        ang = pos_ref[...].astype(F32) * f_ref[...]
        cos_ref[...] = jnp.cos(ang)
        sin_ref[...] = jnp.sin(ang)
        c, s_signed, first = _rot_tables(cos_ref[...], sin_ref[...])
        e = e_ref[...]

        def norm_rot(xv, g, scale):
            r = lax.rsqrt(_segsum(xv * xv, e) * (1.0 / HD) + EPS)
            y = (xv * r) * g
            return (y * c + _swap_halves(y, first) * s_signed) * scale

        qo_ref[...] = norm_rot(q_ref[...], qg_ref[...], HD ** -0.5)
        ko_ref[...] = norm_rot(k_ref[...], kg_ref[...], 1.0)

    col = lambda j: _bs((tr, AW), lambda i, j=j: (i, j))
    vec = _bs((1, AW), lambda i: (0, 0))
    out = jax.ShapeDtypeStruct((T, AW), F32)
    tab = jax.ShapeDtypeStruct((T, 128), F32)
    tspec = _bs((tr, 128), lambda i: (i, 0))
    return pl.pallas_call(
        body, name="qk_prep", grid=(T // tr,),
        in_specs=[col(0), col(1), _bs((tr, 1), lambda i: (i, 0)), _bs((1, 128), lambda i: (0, 0)), vec, vec,
                  _bs((128, 128), lambda i: (0, 0))],
        out_specs=[col(0)] * 2 + [tspec] * 2, out_shape=[out, out, tab, tab], compiler_params=_cp("parallel"),
    )(proj, proj, pos_col, _freq_row(), qg, kg, _head_ones())


def _qk_bwd(proj, cos_t, sin_t, qg, kg, dq, dk, dv):
    tr = 256

    def body(q_ref, k_ref, cos_ref, sin_ref, qg_ref, kg_ref, e_ref, dq_ref, dk_ref, dv_ref, o_ref, dqg_ref, dkg_ref):
        i, j = pl.program_id(0), pl.program_id(1)

        @pl.when((i == 0) & (j == 0))
        def _():
            dqg_ref[...] = jnp.zeros_like(dqg_ref)
            dkg_ref[...] = jnp.zeros_like(dkg_ref)

        def norm_rot_bwd(x_ref, g_ref, dg_ref, d_ref, scale):
            c, s_signed, first = _rot_tables(cos_ref[...], sin_ref[...])
            e = e_ref[...]
            dout = d_ref[...] * scale
            dy = dout * c + _swap_halves(dout * s_signed, first)
            xv, g = x_ref[...], g_ref[...]
            r = lax.rsqrt(_segsum(xv * xv, e) * (1.0 / HD) + EPS)
            gdy = g * dy
            dx = r * gdy - xv * ((r * r * r) * (_segsum(xv * gdy, e) * (1.0 / HD)))
            o_ref[...] = dx.astype(BF16)
            dg_ref[...] += jnp.sum(dy * (xv * r), axis=0, keepdims=True)

        @pl.when(j == 0)
        def _():
            norm_rot_bwd(q_ref, qg_ref, dqg_ref, dq_ref, HD ** -0.5)

        @pl.when(j == 1)
        def _():
            norm_rot_bwd(k_ref, kg_ref, dkg_ref, dk_ref, 1.0)

        @pl.when(j == 2)
        def _():
            o_ref[...] = dv_ref[...].astype(BF16)

    col = lambda jj: _bs((tr, AW), lambda i, j, jj=jj: (i, jj))
    vec = _bs((1, AW), lambda i, j: (0, 0))
    piece = _bs((tr, AW), lambda i, j: (i, 0))
    return pl.pallas_call(
        body, name="qk_bwd", grid=(T // tr, 3),
        in_specs=[col(0), col(1), _bs((tr, 128), lambda i, j: (i, 0)), _bs((tr, 128), lambda i, j: (i, 0)), vec, vec,
                  _bs((128, 128), lambda i, j: (0, 0))] + [piece] * 3,
        out_specs=[_bs((tr, AW), lambda i, j: (i, j)), vec, vec],
        out_shape=[jax.ShapeDtypeStruct((T, INW), BF16), jax.ShapeDtypeStruct((1, AW), F32),
                   jax.ShapeDtypeStruct((1, AW), F32)],
        compiler_params=_cp("arbitrary", "arbitrary"),
    )(proj, proj, cos_t, sin_t, qg, kg, _head_ones(), dq, dk, dv)


RG = 256
QC = 64


def _stacked_band_mask(rows=2 * BLK, q0=0):
    qi = (lax.broadcasted_iota(jnp.int32, (rows, 2 * BLK), 0) + q0) & (BLK - 1)
    kj = lax.broadcasted_iota(jnp.int32, (rows, 2 * BLK), 1)
    rel = qi - kj + BLK
    return (rel >= 0) & (rel <= BLK), lax.broadcasted_iota(jnp.int32, (1, 2 * BLK), 1) >= BLK


def _natural_rows(r0, n_rows, d):
    if d == 1:
        return pl.ds(r0, n_rows)
    ln = T // d
    return pl.ds(r0 // ln + d * (r0 % ln), n_rows, stride=d)


def _regroup_into(dst, src_ref, d, pad, cast=True):
    def step(j, carry):
        r0 = pl.multiple_of(j * RG, RG)
        val = src_ref[_natural_rows(r0, RG, d), :]
        dst[pl.ds(pad + r0, RG), :] = val.astype(dst.dtype) if cast else val
        return carry
    lax.fori_loop(0, T // RG, step, 0)


def _stack_heads(x, h0):
    zero = jnp.zeros_like(x)
    return jnp.concatenate([jnp.where(h0, x, zero), jnp.where(h0, zero, x)], axis=0)


def _attn_fwd(q, k, proj):
    nblk = T // BLK

    def body(q_ref, k_ref, v_ref, a_ref, lse_ref, qs, ks, vs, o0, o1, o2, l0, l1, l2, sb0, sb1):
        band, cur_half = _stacked_band_mask()
        h0 = lax.broadcasted_iota(jnp.int32, (1, 128), 1) < HD
        ks[0:BLK, :] = jnp.zeros((BLK, 128), BF16)
        vs[0:BLK, :] = jnp.zeros((BLK, 128), BF16)
        for d, o_s, l_s in zip(DILATIONS, (o0, o1, o2), (l0, l1, l2)):
            nb = T // d // BLK
            _regroup_into(qs, q_ref, d, 0)
            _regroup_into(ks, k_ref, d, BLK)
            _regroup_into(vs, v_ref, d, BLK)

            def scores(b):
                r0 = pl.multiple_of(b * BLK, BLK)
                return _dot(_stack_heads(qs[pl.ds(r0, BLK), :], h0), ks[pl.ds(r0, 2 * BLK), :], NT)

            def finish(b, s_raw, d=d, nb=nb, o_s=o_s, l_s=l_s):
                r0 = pl.multiple_of(b * BLK, BLK)
                mask = band & (cur_half | ((b & (nb - 1)) > 0))
                s = jnp.where(mask, s_raw, NEG)
                m = jnp.max(s, axis=1, keepdims=True)
                p = jnp.exp(s - m)
                l = jnp.sum(p, axis=1, keepdims=True)
                o = _dot(p.astype(BF16), vs[pl.ds(r0, 2 * BLK), :]) / l
                lse = m + jnp.log(l)
                rows = _natural_rows(r0, BLK, d)
                o_s[rows, :] = jnp.where(h0, o[0:BLK, :], o[BLK:, :])
                l_s[rows, :] = jnp.where(h0, lse[0:BLK, :], lse[BLK:, :])

            sb0[...] = scores(0)

            def step(i, carry):
                b = 2 * i
                sb1[...] = scores(b + 1)
                finish(b, sb0[...])
                sb0[...] = scores(jnp.minimum(b + 2, nblk - 1))
                finish(b + 1, sb1[...])
                return carry

            lax.fori_loop(0, nblk // 2, step, 0)

        def merge(i, carry):
            r = pl.ds(pl.multiple_of(i * RG, RG), RG)
            la, lb, lc = l0[r, :], l1[r, :], l2[r, :]
            m = jnp.maximum(jnp.maximum(la, lb), lc)
            ea, eb, ec = jnp.exp(la - m), jnp.exp(lb - m), jnp.exp(lc - m)
            z = (ea + eb) + ec
            a_ref[r, :] = ((ea * o0[r, :] + eb * o1[r, :]) + ec * o2[r, :]) / z
            lse_ref[r, :] = m + jnp.log(z)
            return carry

        lax.fori_loop(0, T // RG, merge, 0)

    spec = lambda cb: _bs((T, 128), lambda p, cb=cb: (0, cb + p))
    out = jax.ShapeDtypeStruct((T, AW), F32)
    return pl.pallas_call(
        body, name="attn_fwd", grid=(AW // 128,), in_specs=[spec(0), spec(0), spec(8)], out_specs=[spec(0)] * 2,
        out_shape=[out] * 2,
        scratch_shapes=[pltpu.VMEM((T, 128), BF16), pltpu.VMEM((T + BLK, 128), BF16), pltpu.VMEM((T + BLK, 128), BF16)]
        + [pltpu.VMEM((T, 128), F32)] * 6 + [pltpu.VMEM((2 * BLK, 2 * BLK), F32)] * 2,
        compiler_params=_cp("parallel"),
    )(q, k, proj)


def _attn_bwd(q, k, proj, do, lse, delta):
    nblk = T // BLK

    def body(q_ref, k_ref, v_ref, do_ref, l_ref, dl_ref, dq_ref, dk_ref, dv_ref, qs, dos, ks, vs, ls, dls, dks, dvs,
             sa0, sa1, da0, da1):
        band, cur_half = _stacked_band_mask()
        h0 = lax.broadcasted_iota(jnp.int32, (1, 128), 1) < HD
        ks[0:BLK, :] = jnp.zeros((BLK, 128), BF16)
        vs[0:BLK, :] = jnp.zeros((BLK, 128), BF16)
        for d in DILATIONS:
            nb = T // d // BLK
            _regroup_into(qs, q_ref, d, 0)
            _regroup_into(dos, do_ref, d, 0)
            _regroup_into(ks, k_ref, d, BLK)
            _regroup_into(vs, v_ref, d, BLK)
            _regroup_into(ls, l_ref, d, 0, cast=False)
            _regroup_into(dls, dl_ref, d, 0, cast=False)
            dks[...] = jnp.zeros_like(dks)
            dvs[...] = jnp.zeros_like(dvs)

            def scores(b, s_buf, dp_buf):
                r0 = pl.multiple_of(b * BLK, BLK)
                win = pl.ds(r0, 2 * BLK)
                s_buf[...] = _dot(_stack_heads(qs[pl.ds(r0, BLK), :], h0), ks[win, :], NT)
                dp_buf[...] = _dot(_stack_heads(dos[pl.ds(r0, BLK), :], h0), vs[win, :], NT)

            def finish(b, s_buf, dp_buf, d=d, nb=nb):
                r0 = pl.multiple_of(b * BLK, BLK)
                mask = band & (cur_half | ((b & (nb - 1)) > 0))
                win = pl.ds(r0, 2 * BLK)
                lv, dlv = ls[pl.ds(r0, BLK), :], dls[pl.ds(r0, BLK), :]
                lse2 = jnp.concatenate([lv[:, 0:1], lv[:, HD:HD + 1]], axis=0)
                dl2 = jnp.concatenate([dlv[:, 0:1], dlv[:, HD:HD + 1]], axis=0)
                p = jnp.exp(jnp.where(mask, s_buf[...], NEG) - lse2)
                ds = p * (dp_buf[...] - dl2)
                pb, dsb = p.astype(BF16), ds.astype(BF16)
                dq2 = _dot(dsb, ks[win, :])
                dks[win, :] += _dot(dsb, _stack_heads(qs[pl.ds(r0, BLK), :], h0), TN)
                dvs[win, :] += _dot(pb, _stack_heads(dos[pl.ds(r0, BLK), :], h0), TN)
                rows = _natural_rows(r0, BLK, d)
                dq = jnp.where(h0, dq2[0:BLK, :], dq2[BLK:, :])
                dq_ref[rows, :] = dq if d == 1 else dq_ref[rows, :] + dq

            scores(0, sa0, da0)

            def step(i, carry):
                b = 2 * i
                scores(b + 1, sa1, da1)
                finish(b, sa0, da0)
                scores(jnp.minimum(b + 2, nblk - 1), sa0, da0)
                finish(b + 1, sa1, da1)
                return carry

            lax.fori_loop(0, nblk // 2, step, 0)

            def back(j, carry, d=d):
                r0 = pl.multiple_of(j * RG, RG)
                rows = _natural_rows(r0, RG, d)
                src = pl.ds(BLK + r0, RG)
                dk_ref[rows, :] = dks[src, :] if d == 1 else dk_ref[rows, :] + dks[src, :]
                dv_ref[rows, :] = dvs[src, :] if d == 1 else dv_ref[rows, :] + dvs[src, :]
                return carry

            lax.fori_loop(0, T // RG, back, 0)

    spec = lambda cb: _bs((T, 128), lambda p, cb=cb: (0, cb + p))
    ospec = _bs((T, 128), lambda p: (0, p))
    out = jax.ShapeDtypeStruct((T, AW), F32)
    return pl.pallas_call(
        body, name="attn_bwd", grid=(AW // 128,), in_specs=[spec(0), spec(0), spec(8), spec(0), spec(0), spec(0)],
        out_specs=[ospec] * 3, out_shape=[out] * 3,
        scratch_shapes=[pltpu.VMEM((T, 128), BF16), pltpu.VMEM((T, 128), BF16), pltpu.VMEM((T + BLK, 128), BF16),
                        pltpu.VMEM((T + BLK, 128), BF16), pltpu.VMEM((T, 128), F32), pltpu.VMEM((T, 128), F32),
                        pltpu.VMEM((T + BLK, 128), F32), pltpu.VMEM((T + BLK, 128), F32)]
        + [pltpu.VMEM((2 * BLK, 2 * BLK), F32)] * 4,
        compiler_params=_cp("parallel"),
    )(q, k, proj, do, lse, delta)


def _attn_norm(attn, g_attn):
    tr = 512

    def body(a_ref, g_ref, mix_ref):
        attn = a_ref[...]
        r = lax.rsqrt(jnp.mean(attn * attn, axis=-1, keepdims=True) + EPS)
        mix_ref[...] = ((attn * r) * g_ref[...]).astype(BF16)

    row = _bs((tr, AW), lambda i: (i, 0))
    return pl.pallas_call(
        body, name="attn_norm", grid=(T // tr,), in_specs=[row, _bs((1, AW), lambda i: (0, 0))],
        out_specs=row, out_shape=jax.ShapeDtypeStruct((T, D), BF16), compiler_params=_cp("parallel"),
    )(attn, g_attn)


def _attn_out_bwd(attn, dmix, g_attn):
    tr = 256

    def body(a_ref, d_ref, g_ref, e_ref, do_ref, dl_ref, dg_ref):
        av, dyv = a_ref[...], d_ref[...]
        r = lax.rsqrt(jnp.mean(av * av, axis=-1, keepdims=True) + EPS)
        gdy = g_ref[...] * dyv
        da = r * gdy - av * ((r * r * r) * jnp.mean(av * gdy, axis=-1, keepdims=True))
        do_ref[...] = da
        dl_ref[...] = _segsum(da * av, e_ref[...])

        @pl.when(pl.program_id(0) == 0)
        def _():
            dg_ref[...] = jnp.zeros_like(dg_ref)

        dg_ref[...] += jnp.sum(dyv * (av * r), axis=0, keepdims=True)

    row = _bs((tr, AW), lambda i: (i, 0))
    vec = _bs((1, AW), lambda i: (0, 0))
    return pl.pallas_call(
        body, name="attn_out_bwd", grid=(T // tr,), in_specs=[row, row, vec, _bs((128, 128), lambda i: (0, 0))],
        out_specs=[row, row, vec],
        out_shape=[jax.ShapeDtypeStruct((T, AW), F32), jax.ShapeDtypeStruct((T, AW), F32),
                   jax.ShapeDtypeStruct((1, AW), F32)],
        compiler_params=_cp("arbitrary"),
    )(attn, dmix, g_attn, _head_ones())


TRR = 256


def _scan_fwd(a, u):
    n = a.shape[0]
    row = lax.broadcasted_iota(jnp.int32, (n, 1), 0)
    s = 1
    while s < n:
        keep = row >= s
        u = jnp.where(keep, a * pltpu.roll(u, s, 0) + u, u)
        a = jnp.where(keep, a * pltpu.roll(a, s, 0), a)
        s *= 2
    return a, u


def _scan_bwd(c, w):
    n = c.shape[0]
    row = lax.broadcasted_iota(jnp.int32, (n, 1), 0)
    s = 1
    while s < n:
        keep = row < n - s
        w = jnp.where(keep, c * pltpu.roll(w, n - s, 0) + w, w)
        c = jnp.where(keep, c * pltpu.roll(c, n - s, 0), c)
        s *= 2
    return w


def _gates(xc, wrg, wig, brg, big, sp):
    xcb = xc.astype(BF16)
    r = jax.nn.sigmoid(_dot(xcb, wrg) + brg)
    ig = jax.nn.sigmoid(_dot(xcb, wig) + big)
    la = (-LRU_C * r) * sp
    a = jnp.exp(la)
    mult = jnp.sqrt(-jnp.tanh(la) * (a * a + 1.0))
    return r, ig, a, mult


def _conv4(ext_ref, xr, cw_ref, cb_ref, n):
    y = cb_ref[...] + ext_ref[pl.ds(5, n), :] * cw_ref[0:1, :]
    y = y + ext_ref[pl.ds(6, n), :] * cw_ref[1:2, :]
    y = y + ext_ref[pl.ds(7, n), :] * cw_ref[2:3, :]
    return y + xr * cw_ref[3:4, :]


def _rec_fwd(proj, mix, cw, cb, wrg, wig, brg, big, lam, g_rec):
    n = TRR

    def body(xr_ref, gr_ref, cw_ref, cb_ref, wrg_ref, wig_ref, brg_ref, big_ref, lam_ref, g_ref, mix_in,
             mix_ref, h_ref, ext, hcar):
        del mix_in

        @pl.when(pl.program_id(0) == 0)
        def _():
            ext[0:8, :] = jnp.zeros((8, RW), F32)
            hcar[...] = jnp.zeros_like(hcar)

        xr = xr_ref[...]
        ext[8:, :] = xr
        xc = _conv4(ext, xr, cw_ref, cb_ref, n)
        ext[0:8, :] = xr[n - 8:, :]
        sp = _softplus(-lam_ref[...])
        _, ig, a, mult = _gates(xc, wrg_ref[...], wig_ref[...], brg_ref[...], big_ref[...], sp)
        a_s, u_s = _scan_fwd(a, mult * (ig * xc))
        h = u_s + a_s * hcar[7:8, :]
        h_ref[...] = h
        hcar[...] = h[n - 8:, :]
        pre = h * _gelu(gr_ref[...])
        r = lax.rsqrt(jnp.mean(pre * pre, axis=-1, keepdims=True) + EPS)
        mix_ref[...] = ((pre * r) * g_ref[...]).astype(BF16)

    vec = _bs((1, RW), lambda i: (0, 0))
    mat = _bs((RW, RW), lambda i: (0, 0))
    return pl.pallas_call(
        body, name="rec_fwd", grid=(T // n,),
        in_specs=[_bs((n, RW), lambda i: (i, 3)), _bs((n, RW), lambda i: (i, 4)), _bs((8, RW), lambda i: (0, 0)), vec,
                  mat, mat, vec, vec, vec, vec, pl.BlockSpec(memory_space=pl.ANY)],
        out_specs=[_bs((n, RW), lambda i: (i, 1)), _bs((n, RW), lambda i: (i, 0))],
        out_shape=[jax.ShapeDtypeStruct((T, D), BF16), jax.ShapeDtypeStruct((T, RW), F32)],
        scratch_shapes=[pltpu.VMEM((n + 8, RW), F32), pltpu.VMEM((8, RW), F32)],
        input_output_aliases={10: 0}, compiler_params=_cp("arbitrary"),
    )(proj, proj, cw, cb, wrg, wig, brg, big, lam, g_rec, mix)


def _rec_bwd(proj, h, dmix, dproj, cw, cb, wrg, wig, brg, big, lam, g_rec):
    n = TRR
    nt = T // n
    hb = n // 8

    def body(xr_ref, xh_ref, gr_ref, h_ref, hh_ref, dm_ref, cw_ref, cb_ref, wrg_ref, wig_ref, brg_ref, big_ref,
             lam_ref, g_ref, dp_in, dp_ref, xc_ref, dr_ref, di_ref, dcw_ref, dcb_ref, dbr_ref, dbi_ref, dsp_ref,
             dg_ref, ext, exth, extd, adh, dgr_s):
        del dp_in
        i, j = pl.program_id(0), pl.program_id(1)
        first_tile = i == nt - 1
        last_tile = i == 0

        @pl.when(j == 0)
        def _():
            @pl.when(last_tile)
            def _():
                for ref in (dcw_ref, dcb_ref, dbr_ref, dbi_ref, dsp_ref, dg_ref):
                    ref[...] = jnp.zeros_like(ref)
                extd[n:, :] = jnp.zeros((8, RW), F32)
                adh[...] = jnp.zeros_like(adh)

            row = lax.broadcasted_iota(jnp.int32, (n, 1), 0)
            xr = xr_ref[...]
            ext[0:8, :] = jnp.where(first_tile, 0.0, xh_ref[...])
            ext[8:, :] = xr
            xc = _conv4(ext, xr, cw_ref, cb_ref, n)
            sp = _softplus(-lam_ref[...])
            wrg, wig = wrg_ref[...], wig_ref[...]
            r, ig, a, mult = _gates(xc, wrg, wig, brg_ref[...], big_ref[...], sp)

            hv = h_ref[...]
            gl, dgl = _gelu_and_grad(gr_ref[...])
            pre = hv * gl
            dyv = dm_ref[...]
            rr = lax.rsqrt(jnp.mean(pre * pre, axis=-1, keepdims=True) + EPS)
            gdy = g_ref[...] * dyv
            dpre = rr * gdy - pre * ((rr * rr * rr) * jnp.mean(pre * gdy, axis=-1, keepdims=True))
            dg_ref[...] += jnp.sum(dyv * (pre * rr), axis=0, keepdims=True)
            dgr_s[...] = dpre * hv * dgl

            is_last_row = row == n - 1
            w = dpre * gl + jnp.where(is_last_row, adh[0:1, :], 0.0)
            c = jnp.where(is_last_row, 0.0, pltpu.roll(a, n - 1, 0))
            dh = _scan_bwd(c, w)
            adh[...] = (a * dh)[0:8, :]

            exth[0:8, :] = jnp.where(first_tile, 0.0, hh_ref[...])
            exth[8:, :] = hv
            da = dh * exth[pl.ds(7, n), :]
            ixc = ig * xc
            dmult = dh * ixc
            dla = da * a - dmult * ((a * a) / mult)
            dsp_ref[...] += jnp.sum(dla * (-LRU_C * r), axis=0, keepdims=True)
            dpr = (dla * (-LRU_C * sp)) * (r * (1.0 - r))
            dpi = (dh * (mult * xc)) * (ig * (1.0 - ig))
            dprb, dpib = dpr.astype(BF16), dpi.astype(BF16)
            dxc = dh * (mult * ig) + _dot(dprb, wrg, NT) + _dot(dpib, wig, NT)
            dbr_ref[...] += jnp.sum(dpr, axis=0, keepdims=True)
            dbi_ref[...] += jnp.sum(dpi, axis=0, keepdims=True)
            xc_ref[...] = xc.astype(BF16)
            dr_ref[...] = dprb
            di_ref[...] = dpib

            extd[0:n, :] = dxc
            dxr = dxc * cw_ref[3:4, :] + extd[pl.ds(1, n), :] * cw_ref[2:3, :]
            dxr = dxr + extd[pl.ds(2, n), :] * cw_ref[1:2, :] + extd[pl.ds(3, n), :] * cw_ref[0:1, :]
            extd[n:, :] = dxc[0:8, :]
            dcb_ref[...] += jnp.sum(dxc, axis=0, keepdims=True)
            for kk in range(4):
                dcw_ref[kk:kk + 1, :] += jnp.sum(dxc * ext[pl.ds(5 + kk, n), :], axis=0, keepdims=True)

            @pl.when(first_tile)
            def _():
                dsp_ref[...] = dsp_ref[...] * (-jax.nn.sigmoid(-lam_ref[...]))

            dp_ref[...] = dxr.astype(BF16)

        @pl.when(j == 1)
        def _():
            dp_ref[...] = dgr_s[...].astype(BF16)

    vec = _bs((1, RW), lambda i, j: (0, 0))
    mat = _bs((RW, RW), lambda i, j: (0, 0))
    tile = lambda cblk: _bs((n, RW), lambda i, j, cblk=cblk: (nt - 1 - i, cblk))
    halo = lambda cblk: _bs((8, RW), lambda i, j, cblk=cblk: (jnp.maximum((nt - 1 - i) * hb - 1, 0), cblk))
    bt = jax.ShapeDtypeStruct((T, RW), BF16)
    v = jax.ShapeDtypeStruct((1, RW), F32)
    return pl.pallas_call(
        body, name="rec_bwd", grid=(nt, 2),
        in_specs=[tile(3), halo(3), tile(4), tile(0), halo(0), tile(1), _bs((8, RW), lambda i, j: (0, 0)), vec,
                  mat, mat, vec, vec, vec, vec, pl.BlockSpec(memory_space=pl.ANY)],
        out_specs=[_bs((n, RW), lambda i, j: (nt - 1 - i, 3 + j)), tile(0), tile(0), tile(0),
                   _bs((8, RW), lambda i, j: (0, 0)), vec, vec, vec, vec, vec],
        out_shape=[jax.ShapeDtypeStruct((T, INW), BF16), bt, bt, bt, jax.ShapeDtypeStruct((8, RW), F32), v, v, v, v, v],
        scratch_shapes=[pltpu.VMEM((n + 8, RW), F32), pltpu.VMEM((n + 8, RW), F32), pltpu.VMEM((n + 8, RW), F32),
                        pltpu.VMEM((8, RW), F32), pltpu.VMEM((n, RW), F32)],
        input_output_aliases={14: 0}, compiler_params=_cp("arbitrary", "arbitrary"),
    )(proj, proj, proj, h, h, dmix, cw, cb, wrg, wig, brg, big, lam, g_rec, dproj)


FC = 1536
TRF = 256


LC = 128


def _taps(x_ref, edge, cols, r):
    if r == 0:
        return edge[pl.ds(6, 8), cols], edge[pl.ds(7, 8), cols], edge[pl.ds(8, 8), cols]
    return x_ref[pl.ds(r - 2, 8), cols], x_ref[pl.ds(r - 1, 8), cols], x_ref[pl.ds(r, 8), cols]


def _ffn_act(up_pre, cw, cb):
    n = TRF
    hb = n // 8

    def body(g_ref, gh_ref, u_ref, uh_ref, wg_ref, wu_ref, bg_ref, bu_ref, o_ref, eg, eu):
        first = pl.program_id(1) == 0
        eg[0:8, :] = jnp.where(first, 0.0, gh_ref[...])
        eg[8:, :] = g_ref[0:8, :]
        eu[0:8, :] = jnp.where(first, 0.0, uh_ref[...])
        eu[8:, :] = u_ref[0:8, :]

        def column(ci, carry):
            cols = pl.ds(pl.multiple_of(ci * LC, LC), LC)
            rows8 = lambda v: jnp.broadcast_to(v, (8, LC))
            wg = [rows8(wg_ref[kk:kk + 1, cols]) for kk in range(3)]
            wu = [rows8(wu_ref[kk:kk + 1, cols]) for kk in range(3)]
            bg, bu = rows8(bg_ref[:, cols]), rows8(bu_ref[:, cols])
            for r in range(0, n, 16):
                res = []
                for rr in (r, r + 8):
                    g0, g1, g2 = _taps(g_ref, eg, cols, rr)
                    u0, u1, u2 = _taps(u_ref, eu, cols, rr)
                    ug = ((bg + g0 * wg[0]) + g1 * wg[1]) + g2 * wg[2]
                    uu = ((bu + u0 * wu[0]) + u1 * wu[1]) + u2 * wu[2]
                    res.append(_gelu(ug) * uu)
                o_ref[pl.ds(r, 16), cols] = jnp.concatenate(res, axis=0).astype(BF16)
            return carry

        lax.fori_loop(0, FC // LC, column, 0)

    main = lambda o: _bs((n, FC), lambda j, i, o=o: (i, 2 * j + o))
    halo = lambda o: _bs((8, FC), lambda j, i, o=o: (jnp.maximum(i * hb - 1, 0), 2 * j + o))
    wsp = lambda o: _bs((None, 8, FC), lambda j, i, o=o: (2 * j + o, 0, 0))
    bsp = lambda o: _bs((1, FC), lambda j, i, o=o: (0, 2 * j + o))
    return pl.pallas_call(
        body, name="ffn_act", grid=(2, T // n),
        in_specs=[main(0), halo(0), main(1), halo(1), wsp(0), wsp(1), bsp(0), bsp(1)],
        out_specs=_bs((n, FC), lambda j, i: (i, j)), out_shape=jax.ShapeDtypeStruct((T, DFF), BF16),
        scratch_shapes=[pltpu.VMEM((16, FC), F32)] * 2, compiler_params=_cp("parallel", "parallel"),
    )(up_pre, up_pre, up_pre, up_pre, cw, cw, cb, cb)


def _up_act(h2, w_up, cw, cb):
    n = TRF
    nt = T // n
    pw = 256
    npc = FC // pw

    def body(h_ref, wg_ref, wu_ref, cwg_ref, cwu_ref, bg_ref, bu_ref, up_ref, a_ref, fa_ref, fb_ref, hx, gb0, gb1,
             ub0, ub1):
        i = pl.program_id(1)
        halo = h_ref[pl.ds(pl.multiple_of(jnp.maximum(i * n - 16, 0), 16), 16), :]
        hx[0:16, :] = jnp.where(i == 0, jnp.zeros_like(halo), halo)
        hx[16:, :] = h_ref[pl.ds(pl.multiple_of(i * n, n), n), :]
        gbufs, ubufs = (gb0, gb1), (ub0, ub1)

        def dots(c):
            hv = hx[...]
            gbufs[c % 2][...] = _dot(hv, wg_ref[:, c * pw:(c + 1) * pw])
            ubufs[c % 2][...] = _dot(hv, wu_ref[:, c * pw:(c + 1) * pw])

        def chain(c):
            gb, ub = gbufs[c % 2], ubufs[c % 2]
            up_ref[:, c * pw:(c + 1) * pw] = gb[16:, :]
            up_ref[:, FC + c * pw:FC + (c + 1) * pw] = ub[16:, :]
            rows8 = lambda v: jnp.broadcast_to(v, (8, LC))
            for sub in range(pw // LC):
                lc = slice(sub * LC, (sub + 1) * LC)
                cols = slice(c * pw + sub * LC, c * pw + (sub + 1) * LC)
                wg = [rows8(cwg_ref[kk:kk + 1, cols]) for kk in range(3)]
                wu = [rows8(cwu_ref[kk:kk + 1, cols]) for kk in range(3)]
                bg, bu = rows8(bg_ref[:, cols]), rows8(bu_ref[:, cols])
                for r in range(0, n, 16):
                    res, fa, fb = [], [], []
                    for rr in (16 + r, 24 + r):
                        ug = ((bg + gb[pl.ds(rr - 2, 8), lc] * wg[0]) + gb[pl.ds(rr - 1, 8), lc] * wg[1]) \
                            + gb[pl.ds(rr, 8), lc] * wg[2]
                        uu = ((bu + ub[pl.ds(rr - 2, 8), lc] * wu[0]) + ub[pl.ds(rr - 1, 8), lc] * wu[1]) \
                            + ub[pl.ds(rr, 8), lc] * wu[2]
                        gl, dgl = _gelu_and_grad(ug)
                        res.append(gl * uu)
                        fa.append(uu * dgl)
                        fb.append(gl)
                    a_ref[pl.ds(r, 16), cols] = jnp.concatenate(res, axis=0).astype(BF16)
                    fa_ref[pl.ds(r, 16), cols] = jnp.concatenate(fa, axis=0).astype(BF16)
                    fb_ref[pl.ds(r, 16), cols] = jnp.concatenate(fb, axis=0).astype(BF16)

        dots(0)
        for c in range(npc):
            if c + 1 < npc:
                dots(c + 1)
            chain(c)

    wsl = lambda o: _bs((None, D, FC), lambda j, i, o=o: (2 * j + o, 0, 0))
    wsp = lambda o: _bs((None, 8, FC), lambda j, i, o=o: (2 * j + o, 0, 0))
    bsp = lambda o: _bs((1, FC), lambda j, i, o=o: (0, 2 * j + o))
    return pl.pallas_call(
        body, name="up_act", grid=(2, nt),
        in_specs=[pl.BlockSpec((T, D), lambda j, i: (0, 0), pipeline_mode=pl.Buffered(1)), wsl(0), wsl(1),
                  wsp(0), wsp(1), bsp(0), bsp(1)],
        out_specs=[_bs((n, 2 * FC), lambda j, i: (i, j))] + [_bs((n, FC), lambda j, i: (i, j))] * 3,
        out_shape=[jax.ShapeDtypeStruct((T, 2 * DFF), F32)] + [jax.ShapeDtypeStruct((T, DFF), BF16)] * 3,
        scratch_shapes=[pltpu.VMEM((n + 16, D), BF16)] + [pltpu.VMEM((n + 16, pw), F32)] * 4,
        compiler_params=_cp("parallel", "arbitrary"),
    )(h2, w_up, w_up, cw, cw, cb, cb)


def _ffn_bwd(up_pre, fa, fb, dyb, w_down_t, cw, after=()):
    n = TRF
    hb = n // 8
    nt = T // n
    m = n + 8
    pw = 256
    npc = FC // pw

    def body(g_ref, gp_ref, u_ref, up_ref, fa_ref, fan_ref, fb_ref, fbn_ref, dy_ref, wd_ref, wg_ref, wu_ref, *rest):
        o_ref, dw_ref, db_ref, eg0, eu0, dug_s, duu_s, dyx, db0, db1 = rest[len(after):]
        i = pl.program_id(1)
        first, last = i == 0, i == nt - 1

        @pl.when(first)
        def _():
            dw_ref[...] = jnp.zeros_like(dw_ref)
            db_ref[...] = jnp.zeros_like(db_ref)

        tail = dy_ref[pl.ds(pl.multiple_of(jnp.minimum((i + 1) * n, T - 16), 16), 16), :]
        dyx[0:n, :] = dy_ref[pl.ds(pl.multiple_of(i * n, n), n), :]
        dyx[n:, :] = jnp.where(last, jnp.zeros_like(tail), tail)
        dbufs = (db0, db1)

        def dots(c):
            dbufs[c % 2][...] = _dot(dyx[...], wd_ref[:, c * pw:(c + 1) * pw])

        eg0[0:8, :] = jnp.where(first, 0.0, gp_ref[...])
        eg0[8:, :] = g_ref[0:8, :]
        eu0[0:8, :] = jnp.where(first, 0.0, up_ref[...])
        eu0[8:, :] = u_ref[0:8, :]

        def column(ci, dbuf, lc):
            cols = slice(ci * LC, (ci + 1) * LC)
            ucols = slice(FC + ci * LC, FC + (ci + 1) * LC)
            rows8 = lambda v: jnp.broadcast_to(v, (8, LC))
            wg = [rows8(wg_ref[kk:kk + 1, cols]) for kk in range(3)]
            wu = [rows8(wu_ref[kk:kk + 1, cols]) for kk in range(3)]
            zero = jnp.zeros((8, LC), F32)
            acc = [zero] * 8
            for r in range(0, n + 16, 16):
                src_a, src_b, r16 = (fan_ref, fbn_ref, 0) if r == n else (fa_ref, fb_ref, r)
                fa16 = src_a[pl.ds(r16, 16), cols].astype(F32)
                fb16 = src_b[pl.ds(r16, 16), cols].astype(F32)
                for half in range(1 if r == n else 2):
                    rr = r + 8 * half
                    dv = dbuf[pl.ds(rr, 8), lc]
                    dug = dv * fa16[8 * half:8 * half + 8, :]
                    duu = dv * fb16[8 * half:8 * half + 8, :]
                    dug_s[pl.ds(rr, 8), :] = dug
                    duu_s[pl.ds(rr, 8), :] = duu
                    if rr < n:
                        gt, ut = _taps(g_ref, eg0, cols, rr), _taps(u_ref, eu0, cols, rr)
                        acc = [acc[0] + dug * gt[0], acc[1] + dug * gt[1], acc[2] + dug * gt[2],
                               acc[3] + duu * ut[0], acc[4] + duu * ut[1], acc[5] + duu * ut[2],
                               acc[6] + dug, acc[7] + duu]
            for r in range(0, n, 16):
                og, ou = [], []
                for rr in (r, r + 8):
                    og.append((dug_s[pl.ds(rr, 8), :] * wg[2] + dug_s[pl.ds(rr + 1, 8), :] * wg[1])
                              + dug_s[pl.ds(rr + 2, 8), :] * wg[0])
                    ou.append((duu_s[pl.ds(rr, 8), :] * wu[2] + duu_s[pl.ds(rr + 1, 8), :] * wu[1])
                              + duu_s[pl.ds(rr + 2, 8), :] * wu[0])
                o_ref[pl.ds(r, 16), cols] = jnp.concatenate(og, axis=0).astype(BF16)
                o_ref[pl.ds(r, 16), ucols] = jnp.concatenate(ou, axis=0).astype(BF16)
            for kk in range(3):
                dw_ref[kk:kk + 1, cols] += jnp.sum(acc[kk], axis=0, keepdims=True)
                dw_ref[kk:kk + 1, ucols] += jnp.sum(acc[3 + kk], axis=0, keepdims=True)
            db_ref[:, cols] += jnp.sum(acc[6], axis=0, keepdims=True)
            db_ref[:, ucols] += jnp.sum(acc[7], axis=0, keepdims=True)

        dots(0)
        for c in range(npc):
            if c + 1 < npc:
                dots(c + 1)
            for sub in range(pw // LC):
                column(c * (pw // LC) + sub, dbufs[c % 2], slice(sub * LC, (sub + 1) * LC))

    main = lambda o: _bs((n, FC), lambda j, i, o=o: (i, 2 * j + o))
    prev = lambda o: _bs((8, FC), lambda j, i, o=o: (jnp.maximum(i * hb - 1, 0), 2 * j + o))
    saved = _bs((n, FC), lambda j, i: (i, j))
    saved_next = _bs((16, FC), lambda j, i: (jnp.minimum((i + 1) * (n // 16), T // 16 - 1), j))
    wsp = lambda o: _bs((None, 8, FC), lambda j, i, o=o: (2 * j + o, 0, 0))
    return pl.pallas_call(
        body, name="ffn_bwd", grid=(2, nt),
        in_specs=[main(0), prev(0), main(1), prev(1), saved, saved_next, saved, saved_next,
                  pl.BlockSpec((T, D), lambda j, i: (0, 0), pipeline_mode=pl.Buffered(1)),
                  _bs((D, FC), lambda j, i: (0, j)), wsp(0), wsp(1)]
        + [pl.BlockSpec(memory_space=pl.ANY)] * len(after),
        out_specs=[_bs((n, 2 * FC), lambda j, i: (i, j)), _bs((8, 2 * FC), lambda j, i: (0, j)),
                   _bs((1, 2 * FC), lambda j, i: (0, j))],
        out_shape=[jax.ShapeDtypeStruct((T, 2 * DFF), BF16), jax.ShapeDtypeStruct((8, 2 * DFF), F32),
                   jax.ShapeDtypeStruct((1, 2 * DFF), F32)],
        scratch_shapes=[pltpu.VMEM((16, FC), F32)] * 2 + [pltpu.VMEM((m, LC), F32)] * 2
        + [pltpu.VMEM((n + 16, D), BF16)] + [pltpu.VMEM((n + 16, pw), F32)] * 2,
        compiler_params=_cp("parallel", "arbitrary"),
    )(up_pre, up_pre, up_pre, up_pre, fa, fa, fb, fb, dyb, w_down_t, cw, cw, *after)


def _down_loss(act, w_down, x1, target):
    tm, tn = 512, D

    def body(a_ref, b_ref, r_ref, t_ref, dy_ref, dyb_ref, l_ref):
        @pl.when((pl.program_id(0) == 0) & (pl.program_id(1) == 0))
        def _():
            l_ref[...] = jnp.zeros_like(l_ref)

        err = (r_ref[...] + _dot(a_ref[...], b_ref[...])) - t_ref[...]
        dy = err * (1.0 / D)
        dy_ref[...] = dy
        dyb_ref[...] = dy.astype(BF16)
        l_ref[...] += jnp.sum(0.5 * (err * err) * (1.0 / D))

    o_spec = _bs((tm, tn), lambda j, i: (i, j))
    return pl.pallas_call(
        body, name="down_loss", grid=(D // tn, T // tm),
        in_specs=[_bs((tm, DFF), lambda j, i: (i, 0)),
                  pl.BlockSpec((DFF, tn), lambda j, i: (0, j), pipeline_mode=pl.Buffered(1)), o_spec, o_spec],
        out_specs=[o_spec, o_spec, _bs((8, 128), lambda j, i: (0, 0))],
        out_shape=[jax.ShapeDtypeStruct((T, D), F32), jax.ShapeDtypeStruct((T, D), BF16),
                   jax.ShapeDtypeStruct((8, 128), F32)],
        compiler_params=_cp("arbitrary", "arbitrary"),
    )(act, w_down, x1, target)


def _block_diag(w):
    eye = jnp.eye(8, dtype=w.dtype)
    return (w[:, :, None, :] * eye[:, None, :, None]).reshape(RW, RW).astype(BF16)


def _diag_blocks(m):
    eye = jnp.eye(8, dtype=m.dtype)
    return (m.reshape(8, HD, 8, HD) * eye[:, None, :, None]).sum(axis=2)


def _local_step(x, pos_col, target, p, exch):
    qg, kg = jnp.tile(p["q_norm_g"], (1, 8)), jnp.tile(p["k_norm_g"], (1, 8))
    wrg, wig = _block_diag(p["w_rg"]), _block_diag(p["w_ig"])
    brg, big = p["b_rg"].reshape(1, RW), p["b_ig"].reshape(1, RW)

    h1 = _rms_fwd("rms1", x, p["g_mix"])
    p = {**p, **exch.wait_first(h1)}
    proj = _mm("mm_in", h1, p["w_in"], "nn", 512, 640, stack=NCHIP, after=exch.start_rest(), a_full=True)
    q, k, cos_t, sin_t = _qk_prep(proj, pos_col, qg, kg)
    attn, lse = _attn_fwd(q, k, proj)
    mix = _attn_norm(attn, p["g_attn_out"])
    mix, hseq = _rec_fwd(proj, mix, p["rec_conv_w"], p["rec_conv_b"], wrg, wig, brg, big, p["lru_lambda"], p["g_rec_out"])
    rest = exch.wait_rest(mix)
    x1 = _mm("mm_out", mix, rest["w_out"], "nn", 512, 512, res=x, a_full=True)
    h2 = _rms_fwd("rms2", x1, p["g_ffn"])
    up_pre, act, fa, fb = _up_act(h2, rest["w_up"], p["ffn_conv_w"], p["ffn_conv_b"])
    dy, dyb, loss_blk = _down_loss(act, rest["w_down"], x1, target)

    g = {}
    tok = exch.reduce_start("w_down", *_mm("wg_down", act, dyb, "tn", 512, 512, twin_bf16=True))
    dup, g["ffn_conv_w"], g["ffn_conv_b"] = _ffn_bwd(up_pre, fa, fb, dyb, rest["w_down"].T, p["ffn_conv_w"], tok)
    tok = exch.reduce_start("w_up", *_mm("wg_up", h2, dup, "tn", 512, 768, stack=NCHIP, twin_bf16=True, a_full=True))
    dh2 = _mm("dg_up", dup, rest["w_up"], "nt", 512, D, stack=NCHIP, after=tok, b_full=True)
    dx1, dx1b, g["g_ffn"] = _rms_bwd("rms2_bwd", x1, p["g_ffn"], dh2, dy, True)
    tok = exch.reduce_start("w_out", *_mm("wg_out", mix, dx1b, "tn", 512, 512, twin_bf16=True, a_full=True))
    dmix = _mm("dg_out", dx1b, rest["w_out"], "nt", 512, 512, after=tok, a_full=True)
    do, delta, g["g_attn_out"] = _attn_out_bwd(attn, dmix, p["g_attn_out"])
    dq, dk, dv = _attn_bwd(q, k, proj, do, lse, delta)
    dproj, dqg, dkg = _qk_bwd(proj, cos_t, sin_t, qg, kg, dq, dk, dv)
    (dproj, xcb, dprb, dpib, g["rec_conv_w"], g["rec_conv_b"], dbr, dbi, dsp, g["g_rec_out"]) = _rec_bwd(
        proj, hseq, dmix, dproj, p["rec_conv_w"], p["rec_conv_b"], wrg, wig, brg, big, p["lru_lambda"], p["g_rec_out"])
    g["w_rg"] = _diag_blocks(_mm("wg_rg", xcb, dprb, "tn", 512, 512)).reshape(RW, HD)
    g["w_ig"] = _diag_blocks(_mm("wg_ig", xcb, dpib, "tn", 512, 512)).reshape(RW, HD)
    g["b_rg"], g["b_ig"] = dbr.reshape(8, HD), dbi.reshape(8, HD)
    g["lru_lambda"] = dsp
    g["q_norm_g"] = dqg.reshape(8, HD).sum(axis=0, keepdims=True)
    g["k_norm_g"] = dkg.reshape(8, HD).sum(axis=0, keepdims=True)
    tok = exch.reduce_start("w_in", *_mm("wg_in", h1, dproj, "tn", 512, 640, stack=NCHIP, twin_bf16=True, a_full=True))
    dh1 = _mm("dg_in", dproj, p["w_in"], "nt", 512, 512, stack=NCHIP, after=tok)
    grad_x, g["g_mix"] = _rms_bwd("rms1_bwd", x, p["g_mix"], dh1, dx1, False)
    return loss_blk, grad_x, g


ANY = pl.BlockSpec(memory_space=pl.ANY)


def _mesh_pos():
    return lax.axis_index("x"), lax.axis_index("y"), lax.axis_index("c")


def _slot(px, py, perm):
    return 2 * py + px if perm else 2 * px + py


def _other_chips(x, y):
    return [(1 - x, y), (x, 1 - y), (1 - x, 1 - y)]


def _rcopy(src, dst, send, recv, k, to, kr=None):
    return pltpu.make_async_remote_copy(src_ref=src, dst_ref=dst, send_sem=send.at[k],
                                        recv_sem=recv.at[k if kr is None else kr], device_id=to, device_id_type=MESH)


def _cast_bf16(name, w, after=()):
    r, c = w.shape
    tr = 128

    def body(w_ref, *rest):
        rest[-1][...] = w_ref[...].astype(BF16)

    return pl.pallas_call(
        body, name=name, grid=(r // tr,), in_specs=[_bs((tr, c), lambda i: (i, 0))] + [ANY] * len(after),
        out_specs=_bs((tr, c), lambda i: (i, 0)), out_shape=jax.ShapeDtypeStruct((r, c), BF16),
        compiler_params=_cp("parallel"),
    )(w, *after)


def _sibling_fill(lands, perms):
    na = len(lands)

    def body(*refs):
        outs, (send, recv) = refs[na:2 * na], refs[2 * na:]
        x, y, c = _mesh_pos()
        cps = []
        for a in range(na):
            for j, (px, py) in enumerate(_other_chips(x, y)):
                mine = outs[a].at[_slot(px, py, perms[a]), c]
                cps.append(_rcopy(mine, mine, send, recv, 3 * a + j, (x, y, 1 - c)))
        for cp in cps:
            cp.start()
        for a in range(na):
            for j, (px, py) in enumerate(_other_chips(x, y)):
                got = outs[a].at[_slot(px, py, perms[a]), 1 - c]
                _rcopy(got, got, send, recv, 3 * a + j, (x, y, c)).wait_recv()
        for cp in cps:
            cp.wait_send()

    return pl.pallas_call(
        body, name="gather_fill", in_specs=[ANY] * na, out_specs=[ANY] * na,
        out_shape=[jax.ShapeDtypeStruct(a.shape, a.dtype) for a in lands],
        input_output_aliases={i: i for i in range(na)},
        scratch_shapes=[pltpu.SemaphoreType.DMA((3 * na,)), pltpu.SemaphoreType.DMA((3 * na,))],
    )(*lands)


HBM = pl.BlockSpec(memory_space=pltpu.HBM)
SEM = pl.BlockSpec(memory_space=pltpu.SEMAPHORE)
EFFECT = pltpu.SideEffectType.DATAFLOW_SIDE_EFFECTING


def _split_start(name, srcs, lands, plan, nsem):
    ns, nl = len(srcs), len(lands)

    def body(*refs):
        send, recv = refs[ns + nl], refs[ns + nl + 1]
        sends, _ = plan(refs[:ns], refs[ns:ns + nl], send, recv)
        for cp in sends:
            cp.start()
        refs[-1][...] = jnp.zeros((8, 128), F32)

    arrs = list(srcs) + list(lands)
    out = pl.pallas_call(
        body, name=name, in_specs=[HBM] * (ns + nl),
        out_specs=[SEM, SEM] + [HBM] * (ns + nl) + [pl.BlockSpec(memory_space=pltpu.VMEM)],
        out_shape=[pltpu.SemaphoreType.DMA((nsem,)), pltpu.SemaphoreType.DMA((nsem,))]
        + [pltpu.HBM(a.shape, a.dtype) for a in arrs] + [jax.ShapeDtypeStruct((8, 128), F32)],
        input_output_aliases={i: 2 + i for i in range(ns + nl)},
        compiler_params=pltpu.CompilerParams(has_side_effects=EFFECT),
    )(*[pltpu.with_memory_space_constraint(a, pltpu.HBM) for a in arrs])
    return out[0], out[1], out[2:2 + ns], out[2 + ns:2 + ns + nl], out[-1]


def _split_wait(name, send, recv, srcs, lands, plan, after):
    ns, nl = len(srcs), len(lands)

    def body(*refs):
        sends, recvs = plan(refs[:ns], refs[ns:ns + nl], refs[ns + nl], refs[ns + nl + 1])
        for cp in sends:
            cp.wait_send()
        for cp in recvs:
            cp.wait_recv()

    arrs = list(srcs) + list(lands)
    after = tuple(after) if isinstance(after, (tuple, list)) else (after,)
    out = pl.pallas_call(
        body, name=name, in_specs=[HBM] * (ns + nl) + [SEM, SEM] + [ANY] * len(after), out_specs=[HBM] * (ns + nl),
        out_shape=[pltpu.HBM(a.shape, a.dtype) for a in arrs],
        input_output_aliases={i: i for i in range(ns + nl)},
        compiler_params=pltpu.CompilerParams(has_side_effects=EFFECT),
    )(*arrs, send, recv, *after)
    return out[ns:]


def _gather_plan(perms):
    def plan(srcs, lands, send, recv):
        x, y, c = _mesh_pos()
        sends, recvs = [], []
        for a, perm in enumerate(perms):
            for j, (px, py) in enumerate(_other_chips(x, y)):
                for cc in (0, 1):
                    k = 6 * a + 2 * j + cc
                    sends.append(_rcopy(srcs[a].at[c], lands[a].at[_slot(x, y, perm), c], send, recv, k, (px, py, cc),
                                        kr=6 * a + 2 * j + c))
                    got = lands[a].at[_slot(px, py, perm), cc]
                    recvs.append(_rcopy(got, got, send, recv, k, (x, y, c)))
        return sends, recvs
    return plan


def _gather_half_plan(perms, halved):
    def plan(srcs, lands, send, recv):
        x, y, c = _mesh_pos()
        sends, recvs = [], []
        for a, perm in enumerate(perms):
            for j, (px, py) in enumerate(_other_chips(x, y)):
                k = 3 * a + j
                mine, theirs = _slot(x, y, perm), _slot(px, py, perm)
                if halved[a]:
                    sends.append(_rcopy(srcs[a].at[c], lands[a].at[mine, c], send, recv, k, (px, py, c)))
                    got = lands[a].at[theirs, c]
                else:
                    sends.append(_rcopy(srcs[a], lands[a].at[mine], send, recv, k, (px, py, c)))
                    got = lands[a].at[theirs]
                recvs.append(_rcopy(got, got, send, recv, k, (x, y, c)))
        return sends, recvs
    return plan


def _reduce_plan(perm):
    def plan(srcs, lands, send, recv):
        x, y, c = _mesh_pos()
        src, land = srcs[0], lands[0]
        sends = []
        for j, (px, py) in enumerate(_other_chips(x, y)):
            for hf in (0, 1):
                sends.append(_rcopy(src.at[_slot(px, py, perm), hf], land.at[2 * j + c], send, recv, 2 * j + hf,
                                    (px, py, hf), kr=2 * j + c))
        sends.append(_rcopy(src.at[_slot(x, y, perm), 1 - c], land.at[6], send, recv, 6, (x, y, 1 - c)))
        recvs = [_rcopy(land.at[i], land.at[i], send, recv, i, (x, y, c)) for i in range(7)]
        return sends, recvs
    return plan


def _sibling_share(rs):
    na = len(rs)

    def body(*refs):
        ins, outs, (send, recv) = refs[:na], refs[na:2 * na], refs[2 * na:]
        x, y, c = _mesh_pos()
        cps = [_rcopy(ins[a], outs[a], send, recv, a, (x, y, 1 - c)) for a in range(na)]
        for cp in cps:
            cp.start()
        for cp in cps:
            cp.wait()

    return pl.pallas_call(
        body, name="rs_share", in_specs=[ANY] * na, out_specs=[ANY] * na,
        out_shape=[jax.ShapeDtypeStruct(r.shape, F32) for r in rs],
        scratch_shapes=[pltpu.SemaphoreType.DMA((na,)), pltpu.SemaphoreType.DMA((na,))],
    )(*rs)


def _add_pieces(name, g, got, where):
    _, _, r2, cc = g.shape
    tr = 128

    def body(w_ref, g_ref, r_ref, o_ref):
        del w_ref
        acc = g_ref[...]
        for i in range(7):
            acc = acc + r_ref[i].astype(F32)
        o_ref[...] = acc

    return pl.pallas_call(
        body, name=name,
        grid_spec=pltpu.PrefetchScalarGridSpec(
            num_scalar_prefetch=1, grid=(r2 // tr,),
            in_specs=[_bs((None, None, tr, cc), lambda i, w_ref: (w_ref[0], w_ref[1], i, 0)),
                      _bs((7, tr, cc), lambda i, w_ref: (0, i, 0))],
            out_specs=_bs((tr, cc), lambda i, w_ref: (i, 0))),
        out_shape=jax.ShapeDtypeStruct((r2, cc), F32), compiler_params=_cp("parallel"),
    )(where, g, got)


def _adam_math(w, g, m, v):
    m = ADAM_B1 * m + (1.0 - ADAM_B1) * g
    v = ADAM_B2 * v + (1.0 - ADAM_B2) * (g * g)
    m_hat = m / (1.0 - ADAM_B1 ** ADAM_STEP)
    v_hat = v / (1.0 - ADAM_B2 ** ADAM_STEP)
    return -ADAM_LR * (m_hat / (jnp.sqrt(v_hat) + ADAM_EPS) + ADAM_WD * w), m, v


def _adam_big(name, w, g_mine, g_sib, m, v, c_arr):
    r, cols = w.shape
    tr = 128
    per = r // 2 // tr

    def body(c_ref, w_ref, a_ref, b_ref, m_ref, v_ref, g_ref, d_ref, m2_ref, v2_ref):
        g = jnp.where(pl.program_id(0) == c_ref[0], a_ref[...], b_ref[...])
        g_ref[...] = g
        d_ref[...], m2_ref[...], v2_ref[...] = _adam_math(w_ref[...], g, m_ref[...], v_ref[...])

    spec = _bs((tr, cols), lambda h, i, c_ref: (h * per + i, 0))
    half = _bs((tr, cols), lambda h, i, c_ref: (i, 0))
    out = jax.ShapeDtypeStruct((r, cols), F32)
    return pl.pallas_call(
        body, name=name,
        grid_spec=pltpu.PrefetchScalarGridSpec(
            num_scalar_prefetch=1, grid=(2, per), in_specs=[spec, half, half, spec, spec], out_specs=[spec] * 4),
        out_shape=[out] * 4, compiler_params=_cp("parallel", "parallel"),
    )(c_arr, w, g_mine, g_sib, m, v)


_CLASS_SHAPE = {"a": (8, D), "b": (8, RW), "c": (8, 2 * DFF), "d": (1048, HD)}
_SMALL = (
    ("g_mix", "a", 0, 1, D), ("g_ffn", "a", 1, 1, D),
    ("rec_conv_w", "b", 0, 4, RW), ("rec_conv_b", "b", 4, 1, RW), ("lru_lambda", "b", 5, 1, RW),
    ("g_attn_out", "b", 6, 1, RW), ("g_rec_out", "b", 7, 1, RW),
    ("ffn_conv_w", "c", 0, 3, 2 * DFF), ("ffn_conv_b", "c", 3, 1, 2 * DFF),
    ("w_rg", "d", 0, RW, HD), ("w_ig", "d", RW, RW, HD), ("b_rg", "d", 2 * RW, 8, HD), ("b_ig", "d", 2 * RW + 8, 8, HD),
    ("q_norm_g", "d", 2 * RW + 16, 1, HD), ("k_norm_g", "d", 2 * RW + 17, 1, HD),
)
_LOSS_ROW = 2
_CLASSES = ("a", "b", "c", "d")
_CLASS_OWNER = {"a": 0, "b": 0, "c": 0, "d": 1}


def _small_allreduce(g, loss_blk):
    names = [s[0] for s in _SMALL]
    nin = len(names) + 1

    def body(*refs):
        ins = dict(zip(names, refs[:len(names)]))
        loss_ref = refs[len(names)]
        outs = dict(zip(_CLASSES, refs[nin:nin + 4]))
        pair = dict(zip(_CLASSES, refs[nin + 4:nin + 8]))
        quad = dict(zip(_CLASSES, refs[nin + 8:nin + 12]))
        send, recv = refs[nin + 12:]
        x, y, c = _mesh_pos()
        chip = 2 * x + y
        pair["a"][c] = jnp.zeros(_CLASS_SHAPE["a"], F32)
        pair["b"][c] = ins["rec_conv_w"][...]
        pair["c"][c] = ins["ffn_conv_w"][...]
        pair["d"][c, 2 * RW + 16:, :] = jnp.zeros((8, HD), F32)
        for name, k, r0, nr, _ in _SMALL:
            if name in ("rec_conv_w", "ffn_conv_w"):
                continue
            pair[k][c, r0:r0 + nr, :] = ins[name][...]
        pair["a"][c, _LOSS_ROW:_LOSS_ROW + 1, :] = jnp.broadcast_to(loss_ref[0:1, 0:1], (1, D))
        cps = [_rcopy(pair[k].at[c], pair[k].at[c], send, recv, ki, (x, y, 1 - c)) for ki, k in enumerate(_CLASSES)]
        for cp in cps:
            cp.start()
        for ki, k in enumerate(_CLASSES):
            _rcopy(pair[k].at[1 - c], pair[k].at[1 - c], send, recv, ki, (x, y, c)).wait_recv()
            quad[k][chip] = pair[k][0] + pair[k][1]
        for cp in cps:
            cp.wait_send()
        for ki, k in enumerate(_CLASSES):
            owner = _CLASS_OWNER[k]

            @pl.when(c == owner)
            def _(ki=ki, k=k):
                cps2 = [_rcopy(quad[k].at[chip], quad[k].at[chip], send, recv, 4 + 3 * ki + j, (px, py, c))
                        for j, (px, py) in enumerate(_other_chips(x, y))]
                for cp in cps2:
                    cp.start()
                for j, (px, py) in enumerate(_other_chips(x, y)):
                    got = quad[k].at[2 * px + py]
                    _rcopy(got, got, send, recv, 4 + 3 * ki + j, (x, y, c)).wait_recv()
                outs[k][...] = ((quad[k][0] + quad[k][1]) + quad[k][2]) + quad[k][3]
                share = _rcopy(outs[k], outs[k], send, recv, 16 + ki, (x, y, 1 - c))
                share.start()
                for cp in cps2:
                    cp.wait_send()
                share.wait_send()

        for ki, k in enumerate(_CLASSES):
            @pl.when(c != _CLASS_OWNER[k])
            def _(ki=ki, k=k):
                _rcopy(outs[k], outs[k], send, recv, 16 + ki, (x, y, c)).wait_recv()

    vm = pl.BlockSpec(memory_space=pltpu.VMEM)
    return pl.pallas_call(
        body, name="small_allreduce", in_specs=[vm] * nin, out_specs=[vm] * 4,
        out_shape=[jax.ShapeDtypeStruct(_CLASS_SHAPE[k], F32) for k in _CLASSES],
        scratch_shapes=[pltpu.VMEM((2,) + _CLASS_SHAPE[k], F32) for k in _CLASSES]
        + [pltpu.VMEM((NCHIP,) + _CLASS_SHAPE[k], F32) for k in _CLASSES]
        + [pltpu.SemaphoreType.DMA((20,)), pltpu.SemaphoreType.DMA((20,))],
        compiler_params=pltpu.CompilerParams(vmem_limit_bytes=VMEM_LIMIT),
    )(*[g[n] for n in names], loss_blk)


def _adam_small(red, w, m, v):
    names = [s[0] for s in _SMALL]
    n = len(names)

    def body(*refs):
        red_refs = dict(zip(_CLASSES, refs[:4]))
        w_refs, m_refs, v_refs = refs[4:4 + n], refs[4 + n:4 + 2 * n], refs[4 + 2 * n:4 + 3 * n]
        loss_ref = refs[4 + 3 * n]
        out_refs = refs[5 + 3 * n:]
        x, y, _ = _mesh_pos()
        chip = 2 * x + y
        loss_ref[...] = jnp.broadcast_to(red_refs["a"][_LOSS_ROW:_LOSS_ROW + 1, 0:1], loss_ref.shape)
        for pi, (name, k, r0, nr, width) in enumerate(_SMALL):
            gfull = red_refs[k][r0:r0 + nr, :]
            if name == "rec_conv_w":
                parts = [gfull[:, 128 * s:128 * (s + 1)] for s in range(NCHIP)]
                g = jnp.where(chip == 0, parts[0], jnp.where(chip == 1, parts[1], jnp.where(chip == 2, parts[2], parts[3])))
            elif name == "ffn_conv_w":
                parts = [gfull[:, FC * s:FC * (s + 1)] for s in range(NCHIP)]
                g = jnp.where(chip == 0, parts[0], jnp.where(chip == 1, parts[2], jnp.where(chip == 2, parts[1], parts[3])))
            elif name == "ffn_conv_b":
                g = jnp.concatenate([gfull[:, FC * s:FC * (s + 1)] for s in (0, 2, 1, 3)], axis=1)
            else:
                g = gfull
            d, m2, v2 = _adam_math(w_refs[pi][...], g, m_refs[pi][...], v_refs[pi][...])
            o = out_refs[4 * pi:4 * pi + 4]
            o[0][...], o[1][...], o[2][...], o[3][...] = g, d, m2, v2

    vm = pl.BlockSpec(memory_space=pltpu.VMEM)
    outs = [jax.ShapeDtypeStruct((1, 128), F32)]
    for name in names:
        outs += [jax.ShapeDtypeStruct(w[name].shape, F32)] * 4
    res = pl.pallas_call(
        body, name="adam_small", in_specs=[vm] * (4 + 3 * n), out_specs=[vm] * len(outs), out_shape=outs,
        compiler_params=pltpu.CompilerParams(vmem_limit_bytes=VMEM_LIMIT),
    )(*red, *[w[k] for k in names], *[m[k] for k in names], *[v[k] for k in names])
    return res[0], {name: res[1 + 4 * i:5 + 4 * i] for i, name in enumerate(names)}


_WEIGHTS = ("g_mix", "w_in", "q_norm_g", "k_norm_g", "rec_conv_w", "rec_conv_b", "w_rg", "b_rg", "w_ig", "b_ig",
            "lru_lambda", "g_attn_out", "g_rec_out", "w_out", "g_ffn", "w_up", "ffn_conv_w", "ffn_conv_b", "w_down")
_BIG = ("w_in", "w_out", "w_up", "w_down")
_BIG_PERM = {"w_in": False, "w_out": False, "w_up": True, "w_down": False}
_SMALL_2D = {"w_rg": (RW, HD), "w_ig": (RW, HD), "b_rg": (8, HD), "b_ig": (8, HD), "rec_conv_w": (4, 128),
             "ffn_conv_w": (3, FC)}


def _halves(a):
    r, c = a.shape
    return a.reshape(2, r // 2, c)


def kernel(x, positions, g_mix, w_in, q_norm_g, k_norm_g, rec_conv_w, rec_conv_b, w_rg, b_rg, w_ig, b_ig, lru_lambda, g_attn_out, g_rec_out, w_out, g_ffn, w_up, ffn_conv_w, ffn_conv_b, w_down, loss_target, m_g_mix, m_w_in, m_q_norm_g, m_k_norm_g, m_rec_conv_w, m_rec_conv_b, m_w_rg, m_b_rg, m_w_ig, m_b_ig, m_lru_lambda, m_g_attn_out, m_g_rec_out, m_w_out, m_g_ffn, m_w_up, m_ffn_conv_w, m_ffn_conv_b, m_w_down, v_g_mix, v_w_in, v_q_norm_g, v_k_norm_g, v_rec_conv_w, v_rec_conv_b, v_w_rg, v_b_rg, v_w_ig, v_b_ig, v_lru_lambda, v_g_attn_out, v_g_rec_out, v_w_out, v_g_ffn, v_w_up, v_ffn_conv_w, v_ffn_conv_b, v_w_down):
    given = dict(g_mix=g_mix, w_in=w_in, q_norm_g=q_norm_g, k_norm_g=k_norm_g, rec_conv_w=rec_conv_w, rec_conv_b=rec_conv_b, w_rg=w_rg, b_rg=b_rg, w_ig=w_ig, b_ig=b_ig, lru_lambda=lru_lambda, g_attn_out=g_attn_out, g_rec_out=g_rec_out, w_out=w_out, g_ffn=g_ffn, w_up=w_up, ffn_conv_w=ffn_conv_w, ffn_conv_b=ffn_conv_b, w_down=w_down)
    given_m = dict(g_mix=m_g_mix, w_in=m_w_in, q_norm_g=m_q_norm_g, k_norm_g=m_k_norm_g, rec_conv_w=m_rec_conv_w, rec_conv_b=m_rec_conv_b, w_rg=m_w_rg, b_rg=m_b_rg, w_ig=m_w_ig, b_ig=m_b_ig, lru_lambda=m_lru_lambda, g_attn_out=m_g_attn_out, g_rec_out=m_g_rec_out, w_out=m_w_out, g_ffn=m_g_ffn, w_up=m_w_up, ffn_conv_w=m_ffn_conv_w, ffn_conv_b=m_ffn_conv_b, w_down=m_w_down)
    given_v = dict(g_mix=v_g_mix, w_in=v_w_in, q_norm_g=v_q_norm_g, k_norm_g=v_k_norm_g, rec_conv_w=v_rec_conv_w, rec_conv_b=v_rec_conv_b, w_rg=v_w_rg, b_rg=v_b_rg, w_ig=v_w_ig, b_ig=v_b_ig, lru_lambda=v_lru_lambda, g_attn_out=v_g_attn_out, g_rec_out=v_g_rec_out, w_out=v_w_out, g_ffn=v_g_ffn, w_up=v_w_up, ffn_conv_w=v_ffn_conv_w, ffn_conv_b=v_ffn_conv_b, w_down=v_w_down)
    shapes = {n: a.shape for n, a in given.items()}

    def two_d(n, a):
        a = a[0]
        return a.reshape(_SMALL_2D[n]) if n in _SMALL_2D else (a if a.ndim == 2 else a[None])

    w = {n: two_d(n, a) for n, a in given.items()}
    m = {n: two_d(n, a) for n, a in given_m.items()}
    v = {n: two_d(n, a) for n, a in given_v.items()}
    cc = lax.axis_index("c").astype(jnp.int32)
    cx, cy = lax.axis_index("x").astype(jnp.int32), lax.axis_index("y").astype(jnp.int32)
    slot = {False: 2 * cx + cy, True: 2 * cy + cx}

    shards = {"w_in": _halves(_cast_bf16("cast_w_in", w["w_in"]))}
    first = [shards["w_in"], jnp.pad(w["ffn_conv_w"], ((0, 5), (0, 0))), jnp.pad(w["rec_conv_w"], ((0, 4), (0, 0)))]
    first_perm = [False, True, False]
    first_plan = _gather_half_plan(first_perm, [True, False, False])
    in_flight = _split_start(
        "gather_in_start", first,
        [lax.dynamic_update_slice(lax.empty((NCHIP,) + a.shape, a.dtype), a[None], (slot[pm],) + (0,) * a.ndim)
         for a, pm in zip(first, first_perm)], first_plan, 3 * len(first))
    for n in ("w_out", "w_up", "w_down"):
        shards[n] = _halves(_cast_bf16(f"cast_{n}", w[n], after=(in_flight[4],)))
    p = {n: w[n] for n in ("g_mix", "g_ffn", "q_norm_g", "k_norm_g", "rec_conv_b", "lru_lambda", "g_attn_out", "g_rec_out")}
    p.update(w_rg=w["w_rg"].reshape(8, HD, HD), w_ig=w["w_ig"].reshape(8, HD, HD), b_rg=w["b_rg"], b_ig=w["b_ig"],
             ffn_conv_b=jnp.concatenate([w["ffn_conv_b"][:, FC * s:FC * (s + 1)] for s in (0, 2, 1, 3)], axis=1))

    class Exchange:
        rest = ("w_out", "w_up", "w_down")
        order = []
        flight = {}

        def wait_first(self, after):
            send, recv, srcs, lands, _ = in_flight
            f_in, f_fcw, f_rcw = _split_wait("gather_in_wait", send, recv, srcs, lands, first_plan,
                                             (after,) + tuple(shards[n] for n in self.rest))
            (f_in,) = _sibling_fill([f_in], [False])
            return dict(w_in=f_in.reshape(NCHIP, D, INW // NCHIP), ffn_conv_w=f_fcw,
                        rec_conv_w=f_rcw.transpose(1, 0, 2).reshape(8, RW))

        def start_rest(self):
            srcs = [shards[n] for n in self.rest]
            lands = [lax.dynamic_update_slice(lax.empty((NCHIP,) + s.shape, BF16), s[None], (slot[_BIG_PERM[n]], 0, 0, 0))
                     for n, s in zip(self.rest, srcs)]
            plan = _gather_plan([_BIG_PERM[n] for n in self.rest])
            send, recv, srcs, lands, token = _split_start("gather_rest_start", srcs, lands, plan, 6 * len(srcs))
            self.flight["rest"] = (send, recv, srcs, lands, plan)
            return (token,)

        def wait_rest(self, after):
            send, recv, srcs, lands, plan = self.flight.pop("rest")
            f_out, f_up, f_down = _split_wait("gather_rest_wait", send, recv, srcs, lands, plan, after)
            return dict(w_out=f_out.reshape(D, D), w_up=f_up.reshape(NCHIP, D, FC), w_down=f_down.reshape(DFF, D))

        def reduce_start(self, name, g32, g16):
            r2, cols = shards[name].shape[1:]
            plan = _reduce_plan(_BIG_PERM[name])
            send, recv, srcs, lands, token = _split_start(
                f"reduce_{name}_start", [g16.reshape(NCHIP, 2, r2, cols)], [lax.empty((7, r2, cols), BF16)], plan, 7)
            self.flight[name] = (send, recv, srcs, lands, plan, g32.reshape(NCHIP, 2, r2, cols))
            self.order.append(name)
            return (token,)

        def finish(self, after):
            mine = {}
            for name in self.order:
                send, recv, srcs, lands, plan, g32 = self.flight.pop(name)
                (got,) = _split_wait(f"reduce_{name}_wait", send, recv, srcs, lands, plan, after)
                where = jnp.stack([slot[_BIG_PERM[name]], cc])
                mine[name] = after = _add_pieces(f"reduce_{name}_add", g32, got, where)
            theirs = dict(zip(_BIG, _sibling_share([mine[n] for n in _BIG])))
            return mine, theirs

    exch = Exchange()

    loss_blk, grad_x, g = _local_step(x[0], positions.reshape(T, 1), loss_target[0], p, exch)

    out_g, out_d, out_m, out_v = {}, {}, {}, {}
    red = _small_allreduce(g, loss_blk)
    loss_row, small_out = _adam_small(red, w, m, v)
    for n, (gn, dn, mn, vn) in small_out.items():
        out_g[n], out_d[n], out_m[n], out_v[n] = gn, dn, mn, vn

    mine, theirs = exch.finish(red[0])
    for n in _BIG:
        out_g[n], out_d[n], out_m[n], out_v[n] = _adam_big(f"adam_{n}", w[n], mine[n], theirs[n], m[n], v[n], cc.reshape(1))

    outs = [loss_row[0, 0], grad_x[None]]
    for group in (out_g, out_d, out_m, out_v):
        outs += [group[n].reshape(shapes[n]) for n in _WEIGHTS]
    return tuple(outs)
```

```python
import math

import jax
import jax.numpy as jnp
import numpy as np
from jax import lax
from jax.experimental import pallas as pl
from jax.experimental.pallas import tpu as pltpu

F32 = jnp.float32
BF16 = jnp.bfloat16

T = 4096
D = 1024
HD = 64
AW = 512
RW = 512
INW = 2560
DFF = 3072
NCHIP = 4
EPS = 1e-6
NEG = -1e30
LRU_C = 8.0
ROPE_THETA = 10000.0
BLK = 128
DILATIONS = (1, 4, 16)
ADAM_LR, ADAM_B1, ADAM_B2, ADAM_EPS, ADAM_WD, ADAM_STEP = 0.001, 0.9, 0.999, 1e-08, 0.01, 10
VMEM_LIMIT = 56 * 1024 * 1024
MESH = pl.DeviceIdType.MESH

NN = (((1,), (0,)), ((), ()))
NT = (((1,), (1,)), ((), ()))
TN = (((0,), (0,)), ((), ()))


def _cp(*sem):
    return pltpu.CompilerParams(dimension_semantics=sem, vmem_limit_bytes=VMEM_LIMIT)


def _bs(shape, fn):
    return pl.BlockSpec(shape, fn)


def _dot(a, b, dims=NN):
    return lax.dot_general(a, b, dims, preferred_element_type=F32)


_GC = math.sqrt(2.0 / math.pi)


def _gelu(x):
    return x * (0.5 + 0.5 * jnp.tanh(x * (_GC + (_GC * 0.044715) * (x * x))))


def _gelu_and_grad(x):
    x2 = x * x
    th = jnp.tanh(x * (_GC + (_GC * 0.044715) * x2))
    cdf = 0.5 + 0.5 * th
    dg = cdf + (x * (1.0 - th * th)) * ((0.5 * _GC) + (1.5 * 0.044715 * _GC) * x2)
    return x * cdf, dg


def _softplus(x):
    e = jnp.exp(-jnp.abs(x))
    u = 1.0 + e
    l1p = jnp.where(u == 1.0, e, jnp.log(u) * (e / (u - 1.0)))
    return jnp.maximum(x, 0.0) + l1p


def _segsum(z, e_bf16):
    hi = z.astype(BF16)
    lo = (z - hi.astype(F32)).astype(BF16)
    parts = []
    for c0 in range(0, z.shape[1], 128):
        parts.append(_dot(hi[:, c0:c0 + 128], e_bf16) + _dot(lo[:, c0:c0 + 128], e_bf16))
    return jnp.concatenate(parts, axis=1)


def _mm(name, a, b, mode, tm, tn, out_dtype=F32, res=None, stack=0, twin_bf16=False, after=(), a_full=False,
        b_full=False):
    if mode == "nn":
        (m, k), n = a.shape, (b.shape[1] if not stack else stack * b.shape[2])
        a_spec = _bs((tm, k), lambda j, i: (i, 0))
        if stack:
            per = b.shape[2] // tn
            b_spec = _bs((None, k, tn), lambda j, i: (j // per, 0, j % per))
        else:
            b_spec = _bs((k, tn), lambda j, i: (0, j))
    elif mode == "nt":
        (m, k), n = a.shape, (b.shape[0] if not stack else b.shape[1])
        a_spec = _bs((tm, k), lambda j, i: (i, 0))
        b_spec = _bs((stack, tn, k // stack), lambda j, i: (0, j, 0)) if stack else _bs((tn, k), lambda j, i: (j, 0))
    else:
        (k, m), n = a.shape, b.shape[1]
        a_spec, b_spec = _bs((k, tm), lambda j, i: (0, i)), _bs((k, tn), lambda j, i: (0, j))
    assert m % tm == 0 and n % tn == 0
    o_spec = _bs((tm, tn), lambda j, i: (i, j))
    o_shape = (m, n)
    if mode == "tn" and stack:
        per = n // stack // tn
        o_spec = _bs((None, tm, tn), lambda j, i: (j // per, i, j % per))
        o_shape = (stack, m, n // stack)
    dims = {"nn": NN, "nt": NT, "tn": TN}[mode]
    once = pl.Buffered(1)
    if a_full:
        a_spec = pl.BlockSpec(a.shape, lambda j, i: (0, 0), pipeline_mode=once)
    if b_full:
        assert n == tn
        b_spec = pl.BlockSpec(b_spec.block_shape, b_spec.index_map, pipeline_mode=once)

    def product(a_ref, b_ref):
        if a_full:
            mine = pl.ds(pl.multiple_of(pl.program_id(1) * tm, tm), tm)
            take = (lambda cols: a_ref[:, mine]) if mode == "tn" else (lambda cols: a_ref[mine, cols])
        else:
            take = lambda cols: a_ref[:, cols]
        if mode == "nt" and stack:
            cs = k // stack
            acc = _dot(take(pl.ds(0, cs)), b_ref[0], NT)
            for s in range(1, stack):
                acc = acc + _dot(take(pl.ds(s * cs, cs)), b_ref[s], NT)
            return acc
        return _dot(take(slice(None)), b_ref[...], dims)

    nres = 0 if res is None else 1

    def body(a_ref, b_ref, *rest):
        acc = product(a_ref, b_ref)
        if nres:
            acc = rest[0][...] + acc
        outs = rest[nres + len(after):]
        outs[0][...] = acc.astype(out_dtype)
        if twin_bf16:
            outs[1][...] = acc.astype(BF16)

    ins = (a, b) + ((res,) if nres else ()) + tuple(after)
    specs = [a_spec, b_spec] + ([o_spec] if nres else []) + [pl.BlockSpec(memory_space=pl.ANY)] * len(after)
    shapes = [jax.ShapeDtypeStruct(o_shape, out_dtype)] + ([jax.ShapeDtypeStruct(o_shape, BF16)] if twin_bf16 else [])
    out = pl.pallas_call(
        body, name=name, grid=(n // tn, m // tm), in_specs=specs, out_specs=[o_spec] * len(shapes),
        out_shape=shapes, compiler_params=_cp("parallel", "parallel"),
    )(*ins)
    return tuple(out) if twin_bf16 else out[0]


def _rms_fwd(name, x, g):
    tr = 512

    def body(x_ref, g_ref, o_ref):
        xv = x_ref[...]
        r = lax.rsqrt(jnp.mean(xv * xv, axis=-1, keepdims=True) + EPS)
        o_ref[...] = ((xv * r) * g_ref[...]).astype(BF16)

    return pl.pallas_call(
        body, name=name, grid=(T // tr,), in_specs=[_bs((tr, D), lambda i: (i, 0)), _bs((1, D), lambda i: (0, 0))],
        out_specs=_bs((tr, D), lambda i: (i, 0)), out_shape=jax.ShapeDtypeStruct((T, D), BF16),
        compiler_params=_cp("parallel"),
    )(x, g)


def _rms_bwd(name, x, g, dy, dres, want_bf16, after=()):
    tr = 256
    halves = dy.ndim == 3

    def body(x_ref, g_ref, dy_ref, dr_ref, *rest):
        rest = rest[len(after):]
        dx_ref, rest = rest[0], rest[1:]
        dg_ref = rest[-1]
        xv = x_ref[...]
        dyv = dy_ref[0] + dy_ref[1] if halves else dy_ref[...]
        r = lax.rsqrt(jnp.mean(xv * xv, axis=-1, keepdims=True) + EPS)
        gdy = g_ref[...] * dyv
        dx = r * gdy - xv * ((r * r * r) * jnp.mean(xv * gdy, axis=-1, keepdims=True)) + dr_ref[...]
        dx_ref[...] = dx
        if want_bf16:
            rest[0][...] = dx.astype(BF16)

        @pl.when(pl.program_id(0) == 0)
        def _():
            dg_ref[...] = jnp.zeros_like(dg_ref)

        dg_ref[...] += jnp.sum(dyv * (xv * r), axis=0, keepdims=True)

    row = _bs((tr, D), lambda i: (i, 0))
    vec = _bs((1, D), lambda i: (0, 0))
    outs = [jax.ShapeDtypeStruct((T, D), F32)] + ([jax.ShapeDtypeStruct((T, D), BF16)] if want_bf16 else [])
    dy_spec = _bs((2, tr, D), lambda i: (0, i, 0)) if halves else row
    return pl.pallas_call(
        body, name=name, grid=(T // tr,),
        in_specs=[row, vec, dy_spec, row] + [pl.BlockSpec(memory_space=pl.ANY)] * len(after),
        out_specs=[row] * len(outs) + [vec], out_shape=outs + [jax.ShapeDtypeStruct((1, D), F32)],
        compiler_params=_cp("arbitrary"),
    )(x, g, dy, dres, *after)


def _head_ones():
    idx = np.arange(128) // HD
    return jnp.asarray((idx[:, None] == idx[None, :]).astype(np.float32), dtype=BF16)


def _freq_row():
    half = HD // 2
    inv = ROPE_THETA ** (-(np.arange(half, dtype=np.float64)) / half)
    return jnp.asarray(np.tile(inv, 4)[None, :], dtype=F32)


def _rot_tables(cos128, sin128):
    c = jnp.tile(cos128, (1, 4))
    s = jnp.tile(sin128, (1, 4))
    lane = lax.broadcasted_iota(jnp.int32, (1, AW), 1)
    first = (lane & 32) == 0
    return c, jnp.where(first, -s, s), first


def _swap_halves(y, first):
    return jnp.where(first, pltpu.roll(y, AW - 32, 1), pltpu.roll(y, 32, 1))


def _qk_prep(proj, pos_col, qg, kg):
    tr = 512

    def body(q_ref, k_ref, pos_ref, f_ref, qg_ref, kg_ref, e_ref, qo_ref, ko_ref, cos_ref, sin_ref):
        ang = pos_ref[...].astype(F32) * f_ref[...]
        cos_ref[...] = jnp.cos(ang)
        sin_ref[...] = jnp.sin(ang)
        c, s_signed, first = _rot_tables(cos_ref[...], sin_ref[...])
        e = e_ref[...]

        def norm_rot(xv, g, scale):
            r = lax.rsqrt(_segsum(xv * xv, e) * (1.0 / HD) + EPS)
            y = (xv * r) * g
            return (y * c + _swap_halves(y, first) * s_signed) * scale

        qo_ref[...] = norm_rot(q_ref[...], qg_ref[...], HD ** -0.5)
        ko_ref[...] = norm_rot(k_ref[...], kg_ref[...], 1.0)

    col = lambda j: _bs((tr, AW), lambda i, j=j: (i, j))
    vec = _bs((1, AW), lambda i: (0, 0))
    out = jax.ShapeDtypeStruct((T, AW), F32)
    tab = jax.ShapeDtypeStruct((T, 128), F32)
    tspec = _bs((tr, 128), lambda i: (i, 0))
    return pl.pallas_call(
        body, name="qk_prep", grid=(T // tr,),
        in_specs=[col(0), col(1), _bs((tr, 1), lambda i: (i, 0)), _bs((1, 128), lambda i: (0, 0)), vec, vec,
                  _bs((128, 128), lambda i: (0, 0))],
        out_specs=[col(0)] * 2 + [tspec] * 2, out_shape=[out, out, tab, tab], compiler_params=_cp("parallel"),
    )(proj, proj, pos_col, _freq_row(), qg, kg, _head_ones())


def _qk_bwd(proj, cos_t, sin_t, qg, kg, dq, dk, dv):
    tr = 256

    def body(q_ref, k_ref, cos_ref, sin_ref, qg_ref, kg_ref, e_ref, dq_ref, dk_ref, dv_ref, o_ref, dqg_ref, dkg_ref):
        i, j = pl.program_id(0), pl.program_id(1)

        @pl.when((i == 0) & (j == 0))
        def _():
            dqg_ref[...] = jnp.zeros_like(dqg_ref)
            dkg_ref[...] = jnp.zeros_like(dkg_ref)

        def norm_rot_bwd(x_ref, g_ref, dg_ref, d_ref, scale):
            c, s_signed, first = _rot_tables(cos_ref[...], sin_ref[...])
            e = e_ref[...]
            dout = d_ref[...] * scale
            dy = dout * c + _swap_halves(dout * s_signed, first)
            xv, g = x_ref[...], g_ref[...]
            r = lax.rsqrt(_segsum(xv * xv, e) * (1.0 / HD) + EPS)
            gdy = g * dy
            dx = r * gdy - xv * ((r * r * r) * (_segsum(xv * gdy, e) * (1.0 / HD)))
            o_ref[...] = dx.astype(BF16)
            dg_ref[...] += jnp.sum(dy * (xv * r), axis=0, keepdims=True)

        @pl.when(j == 0)
        def _():
            o_ref[...] = dv_ref[...].astype(BF16)

        @pl.when(j == 1)
        def _():
            norm_rot_bwd(q_ref, qg_ref, dqg_ref, dq_ref, HD ** -0.5)

        @pl.when(j == 2)
        def _():
            norm_rot_bwd(k_ref, kg_ref, dkg_ref, dk_ref, 1.0)

    col = lambda jj: _bs((tr, AW), lambda i, j, jj=jj: (i, jj))
    vec = _bs((1, AW), lambda i, j: (0, 0))
    piece = _bs((tr, AW), lambda i, j: (i, 0))
    return pl.pallas_call(
        body, name="qk_bwd", grid=(T // tr, 3),
        in_specs=[col(0), col(1), _bs((tr, 128), lambda i, j: (i, 0)), _bs((tr, 128), lambda i, j: (i, 0)), vec, vec,
                  _bs((128, 128), lambda i, j: (0, 0))] + [piece] * 3,
        out_specs=[_bs((tr, AW), lambda i, j: (i, (j + 2) % 3)), vec, vec],
        out_shape=[jax.ShapeDtypeStruct((T, 3 * AW), BF16), jax.ShapeDtypeStruct((1, AW), F32),
                   jax.ShapeDtypeStruct((1, AW), F32)],
        compiler_params=_cp("arbitrary", "arbitrary"),
    )(proj, proj, cos_t, sin_t, qg, kg, _head_ones(), dq, dk, dv)


RG = 256
QC = 64


def _stacked_band_mask(rows=2 * BLK, q0=0):
    qi = (lax.broadcasted_iota(jnp.int32, (rows, 2 * BLK), 0) + q0) & (BLK - 1)
    kj = lax.broadcasted_iota(jnp.int32, (rows, 2 * BLK), 1)
    rel = qi - kj + BLK
    return (rel >= 0) & (rel <= BLK), lax.broadcasted_iota(jnp.int32, (1, 2 * BLK), 1) >= BLK


def _natural_rows(r0, n_rows, d):
    if d == 1:
        return pl.ds(r0, n_rows)
    ln = T // d
    return pl.ds(r0 // ln + d * (r0 % ln), n_rows, stride=d)


def _regroup_into(dst, src_ref, d, pad, cast=True):
    def step(j, carry):
        r0 = pl.multiple_of(j * RG, RG)
        val = src_ref[_natural_rows(r0, RG, d), :]
        dst[pl.ds(pad + r0, RG), :] = val.astype(dst.dtype) if cast else val
        return carry
    lax.fori_loop(0, T // RG, step, 0)


def _stack_heads(x, h0):
    zero = jnp.zeros_like(x)
    return jnp.concatenate([jnp.where(h0, x, zero), jnp.where(h0, zero, x)], axis=0)


def _attn_fwd(q, k, proj):
    nblk = T // BLK

    def body(q_ref, k_ref, v_ref, a_ref, lse_ref, qs, ks, vs, o0, o1, o2, l0, l1, l2, sb0, sb1):
        band, cur_half = _stacked_band_mask()
        h0 = lax.broadcasted_iota(jnp.int32, (1, 128), 1) < HD
        ks[0:BLK, :] = jnp.zeros((BLK, 128), BF16)
        vs[0:BLK, :] = jnp.zeros((BLK, 128), BF16)
        for d, o_s, l_s in zip(DILATIONS, (o0, o1, o2), (l0, l1, l2)):
            nb = T // d // BLK
            _regroup_into(qs, q_ref, d, 0)
            _regroup_into(ks, k_ref, d, BLK)
            _regroup_into(vs, v_ref, d, BLK)

            def scores(b):
                r0 = pl.multiple_of(b * BLK, BLK)
                return _dot(_stack_heads(qs[pl.ds(r0, BLK), :], h0), ks[pl.ds(r0, 2 * BLK), :], NT)

            def finish(b, s_raw, d=d, nb=nb, o_s=o_s, l_s=l_s):
                r0 = pl.multiple_of(b * BLK, BLK)
                mask = band & (cur_half | ((b & (nb - 1)) > 0))
                s = jnp.where(mask, s_raw, NEG)
                m = jnp.max(s, axis=1, keepdims=True)
                p = jnp.exp(s - m)
                l = jnp.sum(p, axis=1, keepdims=True)
                o = _dot(p.astype(BF16), vs[pl.ds(r0, 2 * BLK), :]) / l
                lse = m + jnp.log(l)
                rows = _natural_rows(r0, BLK, d)
                o_s[rows, :] = jnp.where(h0, o[0:BLK, :], o[BLK:, :])
                l_s[rows, :] = jnp.where(h0, lse[0:BLK, :], lse[BLK:, :])

            sb0[...] = scores(0)

            def step(i, carry):
                b = 2 * i
                sb1[...] = scores(b + 1)
                finish(b, sb0[...])
                sb0[...] = scores(jnp.minimum(b + 2, nblk - 1))
                finish(b + 1, sb1[...])
                return carry

            lax.fori_loop(0, nblk // 2, step, 0)

        def merge(i, carry):
            r = pl.ds(pl.multiple_of(i * RG, RG), RG)
            la, lb, lc = l0[r, :], l1[r, :], l2[r, :]
            m = jnp.maximum(jnp.maximum(la, lb), lc)
            ea, eb, ec = jnp.exp(la - m), jnp.exp(lb - m), jnp.exp(lc - m)
            z = (ea + eb) + ec
            a_ref[r, :] = ((ea * o0[r, :] + eb * o1[r, :]) + ec * o2[r, :]) / z
            lse_ref[r, :] = m + jnp.log(z)
            return carry

        lax.fori_loop(0, T // RG, merge, 0)

    spec = lambda cb: _bs((T, 128), lambda p, cb=cb: (0, cb + p))
    out = jax.ShapeDtypeStruct((T, AW), F32)
    return pl.pallas_call(
        body, name="attn_fwd", grid=(AW // 128,), in_specs=[spec(0), spec(0), spec(8)], out_specs=[spec(0)] * 2,
        out_shape=[out] * 2,
        scratch_shapes=[pltpu.VMEM((T, 128), BF16), pltpu.VMEM((T + BLK, 128), BF16), pltpu.VMEM((T + BLK, 128), BF16)]
        + [pltpu.VMEM((T, 128), F32)] * 6 + [pltpu.VMEM((2 * BLK, 2 * BLK), F32)] * 2,
        compiler_params=_cp("parallel"),
    )(q, k, proj)


def _attn_bwd(q, k, proj, do, lse, delta):
    nblk = T // BLK

    def body(q_ref, k_ref, v_ref, do_ref, l_ref, dl_ref, dq_ref, dk_ref, dv_ref, qs, dos, ks, vs, ls, dls, dks, dvs,
             sa0, sa1, da0, da1):
        band, cur_half = _stacked_band_mask()
        h0 = lax.broadcasted_iota(jnp.int32, (1, 128), 1) < HD
        ks[0:BLK, :] = jnp.zeros((BLK, 128), BF16)
        vs[0:BLK, :] = jnp.zeros((BLK, 128), BF16)
        for d in DILATIONS:
            nb = T // d // BLK
            _regroup_into(qs, q_ref, d, 0)
            _regroup_into(dos, do_ref, d, 0)
            _regroup_into(ks, k_ref, d, BLK)
            _regroup_into(vs, v_ref, d, BLK)
            _regroup_into(ls, l_ref, d, 0, cast=False)
            _regroup_into(dls, dl_ref, d, 0, cast=False)
            dks[...] = jnp.zeros_like(dks)
            dvs[...] = jnp.zeros_like(dvs)

            def scores(b, s_buf, dp_buf):
                r0 = pl.multiple_of(b * BLK, BLK)
                win = pl.ds(r0, 2 * BLK)
                s_buf[...] = _dot(_stack_heads(qs[pl.ds(r0, BLK), :], h0), ks[win, :], NT)
                dp_buf[...] = _dot(_stack_heads(dos[pl.ds(r0, BLK), :], h0), vs[win, :], NT)

            def finish(b, s_buf, dp_buf, d=d, nb=nb):
                r0 = pl.multiple_of(b * BLK, BLK)
                mask = band & (cur_half | ((b & (nb - 1)) > 0))
                win = pl.ds(r0, 2 * BLK)
                lv, dlv = ls[pl.ds(r0, BLK), :], dls[pl.ds(r0, BLK), :]
                lse2 = jnp.concatenate([lv[:, 0:1], lv[:, HD:HD + 1]], axis=0)
                dl2 = jnp.concatenate([dlv[:, 0:1], dlv[:, HD:HD + 1]], axis=0)
                p = jnp.exp(jnp.where(mask, s_buf[...], NEG) - lse2)
                ds = p * (dp_buf[...] - dl2)
                pb, dsb = p.astype(BF16), ds.astype(BF16)
                dq2 = _dot(dsb, ks[win, :])
                dks[win, :] += _dot(dsb, _stack_heads(qs[pl.ds(r0, BLK), :], h0), TN)
                dvs[win, :] += _dot(pb, _stack_heads(dos[pl.ds(r0, BLK), :], h0), TN)
                rows = _natural_rows(r0, BLK, d)
                dq = jnp.where(h0, dq2[0:BLK, :], dq2[BLK:, :])
                dq_ref[rows, :] = dq if d == 1 else dq_ref[rows, :] + dq

            scores(0, sa0, da0)

            def step(i, carry):
                b = 2 * i
                scores(b + 1, sa1, da1)
                finish(b, sa0, da0)
                scores(jnp.minimum(b + 2, nblk - 1), sa0, da0)
                finish(b + 1, sa1, da1)
                return carry

            lax.fori_loop(0, nblk // 2, step, 0)

            def back(j, carry, d=d):
                r0 = pl.multiple_of(j * RG, RG)
                rows = _natural_rows(r0, RG, d)
                src = pl.ds(BLK + r0, RG)
                dk_ref[rows, :] = dks[src, :] if d == 1 else dk_ref[rows, :] + dks[src, :]
                dv_ref[rows, :] = dvs[src, :] if d == 1 else dv_ref[rows, :] + dvs[src, :]
                return carry

            lax.fori_loop(0, T // RG, back, 0)

    spec = lambda cb: _bs((T, 128), lambda p, cb=cb: (0, cb + p))
    ospec = _bs((T, 128), lambda p: (0, p))
    out = jax.ShapeDtypeStruct((T, AW), F32)
    return pl.pallas_call(
        body, name="attn_bwd", grid=(AW // 128,), in_specs=[spec(0), spec(0), spec(8), spec(0), spec(0), spec(0)],
        out_specs=[ospec] * 3, out_shape=[out] * 3,
        scratch_shapes=[pltpu.VMEM((T, 128), BF16), pltpu.VMEM((T, 128), BF16), pltpu.VMEM((T + BLK, 128), BF16),
                        pltpu.VMEM((T + BLK, 128), BF16), pltpu.VMEM((T, 128), F32), pltpu.VMEM((T, 128), F32),
                        pltpu.VMEM((T + BLK, 128), F32), pltpu.VMEM((T + BLK, 128), F32)]
        + [pltpu.VMEM((2 * BLK, 2 * BLK), F32)] * 4,
        compiler_params=_cp("parallel"),
    )(q, k, proj, do, lse, delta)


def _attn_norm(attn, g_attn):
    tr = 512

    def body(a_ref, g_ref, mix_ref):
        attn = a_ref[...]
        r = lax.rsqrt(jnp.mean(attn * attn, axis=-1, keepdims=True) + EPS)
        mix_ref[...] = ((attn * r) * g_ref[...]).astype(BF16)

    row = _bs((tr, AW), lambda i: (i, 0))
    return pl.pallas_call(
        body, name="attn_norm", grid=(T // tr,), in_specs=[row, _bs((1, AW), lambda i: (0, 0))],
        out_specs=row, out_shape=jax.ShapeDtypeStruct((T, D), BF16), compiler_params=_cp("parallel"),
    )(attn, g_attn)


def _attn_out_bwd(attn, dmix, g_attn):
    tr = 256

    def body(a_ref, d_ref, g_ref, e_ref, do_ref, dl_ref, dg_ref):
        av, dyv = a_ref[...], d_ref[...]
        r = lax.rsqrt(jnp.mean(av * av, axis=-1, keepdims=True) + EPS)
        gdy = g_ref[...] * dyv
        da = r * gdy - av * ((r * r * r) * jnp.mean(av * gdy, axis=-1, keepdims=True))
        do_ref[...] = da
        dl_ref[...] = _segsum(da * av, e_ref[...])

        @pl.when(pl.program_id(0) == 0)
        def _():
            dg_ref[...] = jnp.zeros_like(dg_ref)

        dg_ref[...] += jnp.sum(dyv * (av * r), axis=0, keepdims=True)

    row = _bs((tr, AW), lambda i: (i, 0))
    vec = _bs((1, AW), lambda i: (0, 0))
    return pl.pallas_call(
        body, name="attn_out_bwd", grid=(T // tr,), in_specs=[row, row, vec, _bs((128, 128), lambda i: (0, 0))],
        out_specs=[row, row, vec],
        out_shape=[jax.ShapeDtypeStruct((T, AW), F32), jax.ShapeDtypeStruct((T, AW), F32),
                   jax.ShapeDtypeStruct((1, AW), F32)],
        compiler_params=_cp("arbitrary"),
    )(attn, dmix, g_attn, _head_ones())


TRR = 256


def _scan_fwd(a, u):
    n = a.shape[0]
    row = lax.broadcasted_iota(jnp.int32, (n, 1), 0)
    s = 1
    while s < n:
        keep = row >= s
        u = jnp.where(keep, a * pltpu.roll(u, s, 0) + u, u)
        a = jnp.where(keep, a * pltpu.roll(a, s, 0), a)
        s *= 2
    return a, u


def _scan_bwd(c, w):
    n = c.shape[0]
    row = lax.broadcasted_iota(jnp.int32, (n, 1), 0)
    s = 1
    while s < n:
        keep = row < n - s
        w = jnp.where(keep, c * pltpu.roll(w, n - s, 0) + w, w)
        c = jnp.where(keep, c * pltpu.roll(c, n - s, 0), c)
        s *= 2
    return w


def _gates(xc, wrg, wig, brg, big, sp):
    xcb = xc.astype(BF16)
    r = jax.nn.sigmoid(_dot(xcb, wrg) + brg)
    ig = jax.nn.sigmoid(_dot(xcb, wig) + big)
    la = (-LRU_C * r) * sp
    a = jnp.exp(la)
    mult = jnp.sqrt(-jnp.tanh(la) * (a * a + 1.0))
    return r, ig, a, mult


def _conv4(ext_ref, xr, cw_ref, cb_ref, n):
    y = cb_ref[...] + ext_ref[pl.ds(5, n), :] * cw_ref[0:1, :]
    y = y + ext_ref[pl.ds(6, n), :] * cw_ref[1:2, :]
    y = y + ext_ref[pl.ds(7, n), :] * cw_ref[2:3, :]
    return y + xr * cw_ref[3:4, :]


def _rec_fwd(proj, mix, cw, cb, wrg, wig, brg, big, lam, g_rec):
    n = TRR

    def body(xr_ref, gr_ref, cw_ref, cb_ref, wrg_ref, wig_ref, brg_ref, big_ref, lam_ref, g_ref, mix_in,
             mix_ref, h_ref, ext, hcar):
        del mix_in

        @pl.when(pl.program_id(0) == 0)
        def _():
            ext[0:8, :] = jnp.zeros((8, RW), F32)
            hcar[...] = jnp.zeros_like(hcar)

        xr = xr_ref[...]
        ext[8:, :] = xr
        xc = _conv4(ext, xr, cw_ref, cb_ref, n)
        ext[0:8, :] = xr[n - 8:, :]
        sp = _softplus(-lam_ref[...])
        _, ig, a, mult = _gates(xc, wrg_ref[...], wig_ref[...], brg_ref[...], big_ref[...], sp)
        a_s, u_s = _scan_fwd(a, mult * (ig * xc))
        h = u_s + a_s * hcar[7:8, :]
        h_ref[...] = h
        hcar[...] = h[n - 8:, :]
        pre = h * _gelu(gr_ref[...])
        r = lax.rsqrt(jnp.mean(pre * pre, axis=-1, keepdims=True) + EPS)
        mix_ref[...] = ((pre * r) * g_ref[...]).astype(BF16)

    vec = _bs((1, RW), lambda i: (0, 0))
    mat = _bs((RW, RW), lambda i: (0, 0))
    return pl.pallas_call(
        body, name="rec_fwd", grid=(T // n,),
        in_specs=[_bs((n, RW), lambda i: (i, 3)), _bs((n, RW), lambda i: (i, 4)), _bs((8, RW), lambda i: (0, 0)), vec,
                  mat, mat, vec, vec, vec, vec, pl.BlockSpec(memory_space=pl.ANY)],
        out_specs=[_bs((n, RW), lambda i: (i, 1)), _bs((n, RW), lambda i: (i, 0))],
        out_shape=[jax.ShapeDtypeStruct((T, D), BF16), jax.ShapeDtypeStruct((T, RW), F32)],
        scratch_shapes=[pltpu.VMEM((n + 8, RW), F32), pltpu.VMEM((8, RW), F32)],
        input_output_aliases={10: 0}, compiler_params=_cp("arbitrary"),
    )(proj, proj, cw, cb, wrg, wig, brg, big, lam, g_rec, mix)


def _rec_bwd(proj, h, dmix, cw, cb, wrg, wig, brg, big, lam, g_rec):
    n = TRR
    nt = T // n
    hb = n // 8

    def body(xr_ref, xh_ref, gr_ref, h_ref, hh_ref, dm_ref, cw_ref, cb_ref, wrg_ref, wig_ref, brg_ref, big_ref,
             lam_ref, g_ref, dp_ref, xc_ref, dr_ref, di_ref, dcw_ref, dcb_ref, dbr_ref, dbi_ref, dsp_ref,
             dg_ref, ext, exth, extd, adh):
        i, j = pl.program_id(0), pl.program_id(1)
        first_tile = i == nt - 1
        last_tile = i == 0

        @pl.when(j == 0)
        def _():
            @pl.when(last_tile)
            def _():
                for ref in (dcw_ref, dcb_ref, dbr_ref, dbi_ref, dsp_ref, dg_ref):
                    ref[...] = jnp.zeros_like(ref)
                extd[n:, :] = jnp.zeros((8, RW), F32)
                adh[...] = jnp.zeros_like(adh)

            row = lax.broadcasted_iota(jnp.int32, (n, 1), 0)
            xr = xr_ref[...]
            ext[0:8, :] = jnp.where(first_tile, 0.0, xh_ref[...])
            ext[8:, :] = xr
            xc = _conv4(ext, xr, cw_ref, cb_ref, n)
            sp = _softplus(-lam_ref[...])
            wrg, wig = wrg_ref[...], wig_ref[...]
            r, ig, a, mult = _gates(xc, wrg, wig, brg_ref[...], big_ref[...], sp)

            hv = h_ref[...]
            gl, dgl = _gelu_and_grad(gr_ref[...])
            pre = hv * gl
            dyv = dm_ref[...]
            rr = lax.rsqrt(jnp.mean(pre * pre, axis=-1, keepdims=True) + EPS)
            gdy = g_ref[...] * dyv
            dpre = rr * gdy - pre * ((rr * rr * rr) * jnp.mean(pre * gdy, axis=-1, keepdims=True))
            dg_ref[...] += jnp.sum(dyv * (pre * rr), axis=0, keepdims=True)
            dp_ref[:, RW:] = (dpre * hv * dgl).astype(BF16)

            is_last_row = row == n - 1
            w = dpre * gl + jnp.where(is_last_row, adh[0:1, :], 0.0)
            c = jnp.where(is_last_row, 0.0, pltpu.roll(a, n - 1, 0))
            dh = _scan_bwd(c, w)
            adh[...] = (a * dh)[0:8, :]

            exth[0:8, :] = jnp.where(first_tile, 0.0, hh_ref[...])
            exth[8:, :] = hv
            da = dh * exth[pl.ds(7, n), :]
            ixc = ig * xc
            dmult = dh * ixc
            dla = da * a - dmult * ((a * a) / mult)
            dsp_ref[...] += jnp.sum(dla * (-LRU_C * r), axis=0, keepdims=True)
            dpr = (dla * (-LRU_C * sp)) * (r * (1.0 - r))
            dpi = (dh * (mult * xc)) * (ig * (1.0 - ig))
            dprb, dpib = dpr.astype(BF16), dpi.astype(BF16)
            dxc = dh * (mult * ig) + _dot(dprb, wrg, NT) + _dot(dpib, wig, NT)
            dbr_ref[...] += jnp.sum(dpr, axis=0, keepdims=True)
            dbi_ref[...] += jnp.sum(dpi, axis=0, keepdims=True)
            xc_ref[...] = xc.astype(BF16)
            dr_ref[...] = dprb
            di_ref[...] = dpib

            extd[0:n, :] = dxc
            dxr = dxc * cw_ref[3:4, :] + extd[pl.ds(1, n), :] * cw_ref[2:3, :]
            dxr = dxr + extd[pl.ds(2, n), :] * cw_ref[1:2, :] + extd[pl.ds(3, n), :] * cw_ref[0:1, :]
            extd[n:, :] = dxc[0:8, :]
            dcb_ref[...] += jnp.sum(dxc, axis=0, keepdims=True)
            for kk in range(4):
                dcw_ref[kk:kk + 1, :] += jnp.sum(dxc * ext[pl.ds(5 + kk, n), :], axis=0, keepdims=True)

            @pl.when(first_tile)
            def _():
                dsp_ref[...] = dsp_ref[...] * (-jax.nn.sigmoid(-lam_ref[...]))

            dp_ref[:, 0:RW] = dxr.astype(BF16)

    vec = _bs((1, RW), lambda i, j: (0, 0))
    mat = _bs((RW, RW), lambda i, j: (0, 0))
    tile = lambda cblk: _bs((n, RW), lambda i, j, cblk=cblk: (nt - 1 - i, cblk))
    halo = lambda cblk: _bs((8, RW), lambda i, j, cblk=cblk: (jnp.maximum((nt - 1 - i) * hb - 1, 0), cblk))
    bt = jax.ShapeDtypeStruct((T, RW), BF16)
    v = jax.ShapeDtypeStruct((1, RW), F32)
    return pl.pallas_call(
        body, name="rec_bwd", grid=(nt, 1),
        in_specs=[tile(3), halo(3), tile(4), tile(0), halo(0), tile(1), _bs((8, RW), lambda i, j: (0, 0)), vec,
                  mat, mat, vec, vec, vec, vec],
        out_specs=[_bs((n, 2 * RW), lambda i, j: (nt - 1 - i, 0)), tile(0), tile(0), tile(0),
                   _bs((8, RW), lambda i, j: (0, 0)), vec, vec, vec, vec, vec],
        out_shape=[jax.ShapeDtypeStruct((T, 2 * RW), BF16), bt, bt, bt, jax.ShapeDtypeStruct((8, RW), F32),
                   v, v, v, v, v],
        scratch_shapes=[pltpu.VMEM((n + 8, RW), F32), pltpu.VMEM((n + 8, RW), F32), pltpu.VMEM((n + 8, RW), F32),
                        pltpu.VMEM((8, RW), F32)],
        compiler_params=_cp("arbitrary", "arbitrary"),
    )(proj, proj, proj, h, h, dmix, cw, cb, wrg, wig, brg, big, lam, g_rec)


FC = 1536
TRF = 256


LC = 128


def _taps(x_ref, edge, cols, r):
    if r == 0:
        return edge[pl.ds(6, 8), cols], edge[pl.ds(7, 8), cols], edge[pl.ds(8, 8), cols]
    return x_ref[pl.ds(r - 2, 8), cols], x_ref[pl.ds(r - 1, 8), cols], x_ref[pl.ds(r, 8), cols]


def _ffn_act(up_pre, cw, cb):
    n = TRF
    hb = n // 8

    def body(g_ref, gh_ref, u_ref, uh_ref, wg_ref, wu_ref, bg_ref, bu_ref, o_ref, eg, eu):
        first = pl.program_id(1) == 0
        eg[0:8, :] = jnp.where(first, 0.0, gh_ref[...])
        eg[8:, :] = g_ref[0:8, :]
        eu[0:8, :] = jnp.where(first, 0.0, uh_ref[...])
        eu[8:, :] = u_ref[0:8, :]

        def column(ci, carry):
            cols = pl.ds(pl.multiple_of(ci * LC, LC), LC)
            rows8 = lambda v: jnp.broadcast_to(v, (8, LC))
            wg = [rows8(wg_ref[kk:kk + 1, cols]) for kk in range(3)]
            wu = [rows8(wu_ref[kk:kk + 1, cols]) for kk in range(3)]
            bg, bu = rows8(bg_ref[:, cols]), rows8(bu_ref[:, cols])
            for r in range(0, n, 16):
                res = []
                for rr in (r, r + 8):
                    g0, g1, g2 = _taps(g_ref, eg, cols, rr)
                    u0, u1, u2 = _taps(u_ref, eu, cols, rr)
                    ug = ((bg + g0 * wg[0]) + g1 * wg[1]) + g2 * wg[2]
                    uu = ((bu + u0 * wu[0]) + u1 * wu[1]) + u2 * wu[2]
                    res.append(_gelu(ug) * uu)
                o_ref[pl.ds(r, 16), cols] = jnp.concatenate(res, axis=0).astype(BF16)
            return carry

        lax.fori_loop(0, FC // LC, column, 0)

    main = lambda o: _bs((n, FC), lambda j, i, o=o: (i, 2 * j + o))
    halo = lambda o: _bs((8, FC), lambda j, i, o=o: (jnp.maximum(i * hb - 1, 0), 2 * j + o))
    wsp = lambda o: _bs((None, 8, FC), lambda j, i, o=o: (2 * j + o, 0, 0))
    bsp = lambda o: _bs((1, FC), lambda j, i, o=o: (0, 2 * j + o))
    return pl.pallas_call(
        body, name="ffn_act", grid=(2, T // n),
        in_specs=[main(0), halo(0), main(1), halo(1), wsp(0), wsp(1), bsp(0), bsp(1)],
        out_specs=_bs((n, FC), lambda j, i: (i, j)), out_shape=jax.ShapeDtypeStruct((T, DFF), BF16),
        scratch_shapes=[pltpu.VMEM((16, FC), F32)] * 2, compiler_params=_cp("parallel", "parallel"),
    )(up_pre, up_pre, up_pre, up_pre, cw, cw, cb, cb)


def _up_act(h2, w_up, cw, cb):
    n = TRF
    nt = T // n
    pw = 256
    npc = FC // pw

    def body(h_ref, wg_ref, wu_ref, cwg_ref, cwu_ref, bg_ref, bu_ref, up_ref, a_ref, fa_ref, fb_ref, hx, gb0, gb1,
             ub0, ub1):
        i = pl.program_id(1)
        halo = h_ref[pl.ds(pl.multiple_of(jnp.maximum(i * n - 16, 0), 16), 16), :]
        hx[0:16, :] = jnp.where(i == 0, jnp.zeros_like(halo), halo)
        hx[16:, :] = h_ref[pl.ds(pl.multiple_of(i * n, n), n), :]
        gbufs, ubufs = (gb0, gb1), (ub0, ub1)

        def dots(c):
            hv = hx[...]
            gbufs[c % 2][...] = _dot(hv, wg_ref[:, c * pw:(c + 1) * pw])
            ubufs[c % 2][...] = _dot(hv, wu_ref[:, c * pw:(c + 1) * pw])

        def chain(c):
            gb, ub = gbufs[c % 2], ubufs[c % 2]
            up_ref[:, c * pw:(c + 1) * pw] = gb[16:, :]
            up_ref[:, FC + c * pw:FC + (c + 1) * pw] = ub[16:, :]
            rows8 = lambda v: jnp.broadcast_to(v, (8, LC))
            for sub in range(pw // LC):
                lc = slice(sub * LC, (sub + 1) * LC)
                cols = slice(c * pw + sub * LC, c * pw + (sub + 1) * LC)
                wg = [rows8(cwg_ref[kk:kk + 1, cols]) for kk in range(3)]
                wu = [rows8(cwu_ref[kk:kk + 1, cols]) for kk in range(3)]
                bg, bu = rows8(bg_ref[:, cols]), rows8(bu_ref[:, cols])
                for r in range(0, n, 16):
                    res, fa, fb = [], [], []
                    for rr in (16 + r, 24 + r):
                        ug = ((bg + gb[pl.ds(rr - 2, 8), lc] * wg[0]) + gb[pl.ds(rr - 1, 8), lc] * wg[1]) \
                            + gb[pl.ds(rr, 8), lc] * wg[2]
                        uu = ((bu + ub[pl.ds(rr - 2, 8), lc] * wu[0]) + ub[pl.ds(rr - 1, 8), lc] * wu[1]) \
                            + ub[pl.ds(rr, 8), lc] * wu[2]
                        gl, dgl = _gelu_and_grad(ug)
                        res.append(gl * uu)
                        fa.append(uu * dgl)
                        fb.append(gl)
                    a_ref[pl.ds(r, 16), cols] = jnp.concatenate(res, axis=0).astype(BF16)
                    fa_ref[pl.ds(r, 16), cols] = jnp.concatenate(fa, axis=0).astype(BF16)
                    fb_ref[pl.ds(r, 16), cols] = jnp.concatenate(fb, axis=0).astype(BF16)

        dots(0)
        for c in range(npc):
            if c + 1 < npc:
                dots(c + 1)
            chain(c)

    wsl = lambda o: _bs((None, D, FC), lambda j, i, o=o: (2 * j + o, 0, 0))
    wsp = lambda o: _bs((None, 8, FC), lambda j, i, o=o: (2 * j + o, 0, 0))
    bsp = lambda o: _bs((1, FC), lambda j, i, o=o: (0, 2 * j + o))
    return pl.pallas_call(
        body, name="up_act", grid=(2, nt),
        in_specs=[pl.BlockSpec((T, D), lambda j, i: (0, 0), pipeline_mode=pl.Buffered(1)), wsl(0), wsl(1),
                  wsp(0), wsp(1), bsp(0), bsp(1)],
        out_specs=[_bs((n, 2 * FC), lambda j, i: (i, j))] + [_bs((n, FC), lambda j, i: (i, j))] * 3,
        out_shape=[jax.ShapeDtypeStruct((T, 2 * DFF), F32)] + [jax.ShapeDtypeStruct((T, DFF), BF16)] * 3,
        scratch_shapes=[pltpu.VMEM((n + 16, D), BF16)] + [pltpu.VMEM((n + 16, pw), F32)] * 4,
        compiler_params=_cp("parallel", "arbitrary"),
    )(h2, w_up, w_up, cw, cw, cb, cb)


def _ffn_bwd(up_pre, fa, fb, dyb, w_down_t, cw, after=()):
    n = TRF
    hb = n // 8
    nt = T // n
    m = n + 8
    pw = 256
    npc = FC // pw

    def body(g_ref, gp_ref, u_ref, up_ref, fa_ref, fan_ref, fb_ref, fbn_ref, dy_ref, wd_ref, wg_ref, wu_ref, *rest):
        o_ref, dw_ref, db_ref, eg0, eu0, dug_s, duu_s, dyx, db0, db1 = rest[len(after):]
        i = pl.program_id(1)
        first, last = i == 0, i == nt - 1

        @pl.when(first)
        def _():
            dw_ref[...] = jnp.zeros_like(dw_ref)
            db_ref[...] = jnp.zeros_like(db_ref)

        tail = dy_ref[pl.ds(pl.multiple_of(jnp.minimum((i + 1) * n, T - 16), 16), 16), :]
        dyx[0:n, :] = dy_ref[pl.ds(pl.multiple_of(i * n, n), n), :]
        dyx[n:, :] = jnp.where(last, jnp.zeros_like(tail), tail)
        dbufs = (db0, db1)

        def dots(c):
            dbufs[c % 2][...] = _dot(dyx[...], wd_ref[:, c * pw:(c + 1) * pw])

        eg0[0:8, :] = jnp.where(first, 0.0, gp_ref[...])
        eg0[8:, :] = g_ref[0:8, :]
        eu0[0:8, :] = jnp.where(first, 0.0, up_ref[...])
        eu0[8:, :] = u_ref[0:8, :]

        def column(ci, dbuf, lc):
            cols = slice(ci * LC, (ci + 1) * LC)
            ucols = slice(FC + ci * LC, FC + (ci + 1) * LC)
            rows8 = lambda v: jnp.broadcast_to(v, (8, LC))
            wg = [rows8(wg_ref[kk:kk + 1, cols]) for kk in range(3)]
            wu = [rows8(wu_ref[kk:kk + 1, cols]) for kk in range(3)]
            zero = jnp.zeros((8, LC), F32)
            acc = [zero] * 8
            for r in range(0, n + 16, 16):
                src_a, src_b, r16 = (fan_ref, fbn_ref, 0) if r == n else (fa_ref, fb_ref, r)
                fa16 = src_a[pl.ds(r16, 16), cols].astype(F32)
                fb16 = src_b[pl.ds(r16, 16), cols].astype(F32)
                for half in range(1 if r == n else 2):
                    rr = r + 8 * half
                    dv = dbuf[pl.ds(rr, 8), lc]
                    dug = dv * fa16[8 * half:8 * half + 8, :]
                    duu = dv * fb16[8 * half:8 * half + 8, :]
                    dug_s[pl.ds(rr, 8), :] = dug
                    duu_s[pl.ds(rr, 8), :] = duu
                    if rr < n:
                        gt, ut = _taps(g_ref, eg0, cols, rr), _taps(u_ref, eu0, cols, rr)
                        acc = [acc[0] + dug * gt[0], acc[1] + dug * gt[1], acc[2] + dug * gt[2],
                               acc[3] + duu * ut[0], acc[4] + duu * ut[1], acc[5] + duu * ut[2],
                               acc[6] + dug, acc[7] + duu]
            for r in range(0, n, 16):
                og, ou = [], []
                for rr in (r, r + 8):
                    og.append((dug_s[pl.ds(rr, 8), :] * wg[2] + dug_s[pl.ds(rr + 1, 8), :] * wg[1])
                              + dug_s[pl.ds(rr + 2, 8), :] * wg[0])
                    ou.append((duu_s[pl.ds(rr, 8), :] * wu[2] + duu_s[pl.ds(rr + 1, 8), :] * wu[1])
                              + duu_s[pl.ds(rr + 2, 8), :] * wu[0])
                o_ref[pl.ds(r, 16), cols] = jnp.concatenate(og, axis=0).astype(BF16)
                o_ref[pl.ds(r, 16), ucols] = jnp.concatenate(ou, axis=0).astype(BF16)
            for kk in range(3):
                dw_ref[kk:kk + 1, cols] += jnp.sum(acc[kk], axis=0, keepdims=True)
                dw_ref[kk:kk + 1, ucols] += jnp.sum(acc[3 + kk], axis=0, keepdims=True)
            db_ref[:, cols] += jnp.sum(acc[6], axis=0, keepdims=True)
            db_ref[:, ucols] += jnp.sum(acc[7], axis=0, keepdims=True)

        dots(0)
        for c in range(npc):
            if c + 1 < npc:
                dots(c + 1)
            for sub in range(pw // LC):
                column(c * (pw // LC) + sub, dbufs[c % 2], slice(sub * LC, (sub + 1) * LC))

    main = lambda o: _bs((n, FC), lambda j, i, o=o: (i, 2 * j + o))
    prev = lambda o: _bs((8, FC), lambda j, i, o=o: (jnp.maximum(i * hb - 1, 0), 2 * j + o))
    saved = _bs((n, FC), lambda j, i: (i, j))
    saved_next = _bs((16, FC), lambda j, i: (jnp.minimum((i + 1) * (n // 16), T // 16 - 1), j))
    wsp = lambda o: _bs((None, 8, FC), lambda j, i, o=o: (2 * j + o, 0, 0))
    return pl.pallas_call(
        body, name="ffn_bwd", grid=(2, nt),
        in_specs=[main(0), prev(0), main(1), prev(1), saved, saved_next, saved, saved_next,
                  pl.BlockSpec((T, D), lambda j, i: (0, 0), pipeline_mode=pl.Buffered(1)),
                  _bs((D, FC), lambda j, i: (0, j)), wsp(0), wsp(1)]
        + [pl.BlockSpec(memory_space=pl.ANY)] * len(after),
        out_specs=[_bs((n, 2 * FC), lambda j, i: (i, j)), _bs((8, 2 * FC), lambda j, i: (0, j)),
                   _bs((1, 2 * FC), lambda j, i: (0, j))],
        out_shape=[jax.ShapeDtypeStruct((T, 2 * DFF), BF16), jax.ShapeDtypeStruct((8, 2 * DFF), F32),
                   jax.ShapeDtypeStruct((1, 2 * DFF), F32)],
        scratch_shapes=[pltpu.VMEM((16, FC), F32)] * 2 + [pltpu.VMEM((m, LC), F32)] * 2
        + [pltpu.VMEM((n + 16, D), BF16)] + [pltpu.VMEM((n + 16, pw), F32)] * 2,
        compiler_params=_cp("parallel", "arbitrary"),
    )(up_pre, up_pre, up_pre, up_pre, fa, fa, fb, fb, dyb, w_down_t, cw, cw, *after)


def _down_loss(act, w_down, x1, target):
    tm, tn = 512, D

    def body(a_ref, b_ref, r_ref, t_ref, dy_ref, dyb_ref, l_ref):
        @pl.when((pl.program_id(0) == 0) & (pl.program_id(1) == 0))
        def _():
            l_ref[...] = jnp.zeros_like(l_ref)

        err = (r_ref[...] + _dot(a_ref[...], b_ref[...])) - t_ref[...]
        dy = err * (1.0 / D)
        dy_ref[...] = dy
        dyb_ref[...] = dy.astype(BF16)
        l_ref[...] += jnp.sum(0.5 * (err * err) * (1.0 / D))

    o_spec = _bs((tm, tn), lambda j, i: (i, j))
    return pl.pallas_call(
        body, name="down_loss", grid=(D // tn, T // tm),
        in_specs=[_bs((tm, DFF), lambda j, i: (i, 0)),
                  pl.BlockSpec((DFF, tn), lambda j, i: (0, j), pipeline_mode=pl.Buffered(1)), o_spec, o_spec],
        out_specs=[o_spec, o_spec, _bs((8, 128), lambda j, i: (0, 0))],
        out_shape=[jax.ShapeDtypeStruct((T, D), F32), jax.ShapeDtypeStruct((T, D), BF16),
                   jax.ShapeDtypeStruct((8, 128), F32)],
        compiler_params=_cp("arbitrary", "arbitrary"),
    )(act, w_down, x1, target)


def _block_diag(w):
    eye = jnp.eye(8, dtype=w.dtype)
    return (w[:, :, None, :] * eye[:, None, :, None]).reshape(RW, RW).astype(BF16)


def _diag_blocks(m):
    eye = jnp.eye(8, dtype=m.dtype)
    return (m.reshape(8, HD, 8, HD) * eye[:, None, :, None]).sum(axis=2)


def _local_step(x, pos_col, target, p, exch):
    qg, kg = jnp.tile(p["q_norm_g"], (1, 8)), jnp.tile(p["k_norm_g"], (1, 8))
    wrg, wig = _block_diag(p["w_rg"]), _block_diag(p["w_ig"])
    brg, big = p["b_rg"].reshape(1, RW), p["b_ig"].reshape(1, RW)

    h1 = _rms_fwd("rms1", x, p["g_mix"])
    p = {**p, **exch.wait_first(h1)}
    proj = _mm("mm_in", h1, p["w_in"], "nn", 512, 640, stack=NCHIP, after=exch.start_rest(), a_full=True)
    q, k, cos_t, sin_t = _qk_prep(proj, pos_col, qg, kg)
    attn, lse = _attn_fwd(q, k, proj)
    mix = _attn_norm(attn, p["g_attn_out"])
    mix, hseq = _rec_fwd(proj, mix, p["rec_conv_w"], p["rec_conv_b"], wrg, wig, brg, big, p["lru_lambda"], p["g_rec_out"])
    rest = exch.wait_rest(mix)
    x1 = _mm("mm_out", mix, rest["w_out"], "nn", 512, 512, res=x, a_full=True)
    h2 = _rms_fwd("rms2", x1, p["g_ffn"])
    up_pre, act, fa, fb = _up_act(h2, rest["w_up"], p["ffn_conv_w"], p["ffn_conv_b"])
    dy, dyb, loss_blk = _down_loss(act, rest["w_down"], x1, target)

    g = {}
    tok = exch.reduce_start("w_down", *_mm("wg_down", act, dyb, "tn", 512, 512, twin_bf16=True))
    dup, g["ffn_conv_w"], g["ffn_conv_b"] = _ffn_bwd(up_pre, fa, fb, dyb, rest["w_down"].T, p["ffn_conv_w"], tok)
    tok = exch.reduce_start("w_up", *_mm("wg_up", h2, dup, "tn", 512, 768, stack=NCHIP, twin_bf16=True, a_full=True))
    dh2 = _mm("dg_up", dup, rest["w_up"], "nt", 512, D, stack=NCHIP, after=tok, b_full=True)
    dx1, dx1b, g["g_ffn"] = _rms_bwd("rms2_bwd", x1, p["g_ffn"], dh2, dy, True)
    tok = exch.reduce_start("w_out", *_mm("wg_out", mix, dx1b, "tn", 512, 512, twin_bf16=True, a_full=True))
    dmix = _mm("dg_out", dx1b, rest["w_out"], "nt", 512, 512, after=tok, a_full=True)
    do, delta, g["g_attn_out"] = _attn_out_bwd(attn, dmix, p["g_attn_out"])
    dq, dk, dv = _attn_bwd(q, k, proj, do, lse, delta)
    dqkv, dqg, dkg = _qk_bwd(proj, cos_t, sin_t, qg, kg, dq, dk, dv)
    (drec, xcb, dprb, dpib, g["rec_conv_w"], g["rec_conv_b"], dbr, dbi, dsp, g["g_rec_out"]) = _rec_bwd(
        proj, hseq, dmix, p["rec_conv_w"], p["rec_conv_b"], wrg, wig, brg, big, p["lru_lambda"], p["g_rec_out"])
    dproj = jnp.concatenate([dqkv, drec], axis=1)
    g["w_rg"] = _diag_blocks(_mm("wg_rg", xcb, dprb, "tn", 512, 512)).reshape(RW, HD)
    g["w_ig"] = _diag_blocks(_mm("wg_ig", xcb, dpib, "tn", 512, 512)).reshape(RW, HD)
    g["b_rg"], g["b_ig"] = dbr.reshape(8, HD), dbi.reshape(8, HD)
    g["lru_lambda"] = dsp
    g["q_norm_g"] = dqg.reshape(8, HD).sum(axis=0, keepdims=True)
    g["k_norm_g"] = dkg.reshape(8, HD).sum(axis=0, keepdims=True)
    tok = exch.reduce_start("w_in", *_mm("wg_in", h1, dproj, "tn", 512, 640, stack=NCHIP, twin_bf16=True, a_full=True))
    dh1 = _mm("dg_in", dproj, p["w_in"], "nt", 512, 512, stack=NCHIP, after=tok)
    grad_x, g["g_mix"] = _rms_bwd("rms1_bwd", x, p["g_mix"], dh1, dx1, False)
    return loss_blk, grad_x, g


ANY = pl.BlockSpec(memory_space=pl.ANY)


def _mesh_pos():
    return lax.axis_index("x"), lax.axis_index("y"), lax.axis_index("c")


def _slot(px, py, perm):
    return 2 * py + px if perm else 2 * px + py


def _other_chips(x, y):
    return [(1 - x, y), (x, 1 - y), (1 - x, 1 - y)]


def _rcopy(src, dst, send, recv, k, to, kr=None):
    return pltpu.make_async_remote_copy(src_ref=src, dst_ref=dst, send_sem=send.at[k],
                                        recv_sem=recv.at[k if kr is None else kr], device_id=to, device_id_type=MESH)


def _cast_bf16(name, w, after=()):
    r, c = w.shape
    tr = 128

    def body(w_ref, *rest):
        rest[-1][...] = w_ref[...].astype(BF16)

    return pl.pallas_call(
        body, name=name, grid=(r // tr,), in_specs=[_bs((tr, c), lambda i: (i, 0))] + [ANY] * len(after),
        out_specs=_bs((tr, c), lambda i: (i, 0)), out_shape=jax.ShapeDtypeStruct((r, c), BF16),
        compiler_params=_cp("parallel"),
    )(w, *after)


def _sibling_fill(lands, perms):
    na = len(lands)

    def body(*refs):
        outs, (send, recv) = refs[na:2 * na], refs[2 * na:]
        x, y, c = _mesh_pos()
        cps = []
        for a in range(na):
            for j, (px, py) in enumerate(_other_chips(x, y)):
                mine = outs[a].at[_slot(px, py, perms[a]), c]
                cps.append(_rcopy(mine, mine, send, recv, 3 * a + j, (x, y, 1 - c)))
        for cp in cps:
            cp.start()
        for a in range(na):
            for j, (px, py) in enumerate(_other_chips(x, y)):
                got = outs[a].at[_slot(px, py, perms[a]), 1 - c]
                _rcopy(got, got, send, recv, 3 * a + j, (x, y, c)).wait_recv()
        for cp in cps:
            cp.wait_send()

    return pl.pallas_call(
        body, name="gather_fill", in_specs=[ANY] * na, out_specs=[ANY] * na,
        out_shape=[jax.ShapeDtypeStruct(a.shape, a.dtype) for a in lands],
        input_output_aliases={i: i for i in range(na)},
        scratch_shapes=[pltpu.SemaphoreType.DMA((3 * na,)), pltpu.SemaphoreType.DMA((3 * na,))],
    )(*lands)


HBM = pl.BlockSpec(memory_space=pltpu.HBM)
SEM = pl.BlockSpec(memory_space=pltpu.SEMAPHORE)
EFFECT = pltpu.SideEffectType.DATAFLOW_SIDE_EFFECTING


def _split_start(name, srcs, lands, plan, nsem):
    ns, nl = len(srcs), len(lands)

    def body(*refs):
        send, recv = refs[ns + nl], refs[ns + nl + 1]
        sends, _ = plan(refs[:ns], refs[ns:ns + nl], send, recv)
        for cp in sends:
            cp.start()
        refs[-1][...] = jnp.zeros((8, 128), F32)

    arrs = list(srcs) + list(lands)
    out = pl.pallas_call(
        body, name=name, in_specs=[HBM] * (ns + nl),
        out_specs=[SEM, SEM] + [HBM] * (ns + nl) + [pl.BlockSpec(memory_space=pltpu.VMEM)],
        out_shape=[pltpu.SemaphoreType.DMA((nsem,)), pltpu.SemaphoreType.DMA((nsem,))]
        + [pltpu.HBM(a.shape, a.dtype) for a in arrs] + [jax.ShapeDtypeStruct((8, 128), F32)],
        input_output_aliases={i: 2 + i for i in range(ns + nl)},
        compiler_params=pltpu.CompilerParams(has_side_effects=EFFECT),
    )(*[pltpu.with_memory_space_constraint(a, pltpu.HBM) for a in arrs])
    return out[0], out[1], out[2:2 + ns], out[2 + ns:2 + ns + nl], out[-1]


def _split_wait(name, send, recv, srcs, lands, plan, after):
    ns, nl = len(srcs), len(lands)

    def body(*refs):
        sends, recvs = plan(refs[:ns], refs[ns:ns + nl], refs[ns + nl], refs[ns + nl + 1])
        for cp in sends:
            cp.wait_send()
        for cp in recvs:
            cp.wait_recv()

    arrs = list(srcs) + list(lands)
    after = tuple(after) if isinstance(after, (tuple, list)) else (after,)
    out = pl.pallas_call(
        body, name=name, in_specs=[HBM] * (ns + nl) + [SEM, SEM] + [ANY] * len(after), out_specs=[HBM] * (ns + nl),
        out_shape=[pltpu.HBM(a.shape, a.dtype) for a in arrs],
        input_output_aliases={i: i for i in range(ns + nl)},
        compiler_params=pltpu.CompilerParams(has_side_effects=EFFECT),
    )(*arrs, send, recv, *after)
    return out[ns:]


def _gather_plan(perms):
    def plan(srcs, lands, send, recv):
        x, y, c = _mesh_pos()
        sends, recvs = [], []
        for a, perm in enumerate(perms):
            for j, (px, py) in enumerate(_other_chips(x, y)):
                for cc in (0, 1):
                    k = 6 * a + 2 * j + cc
                    sends.append(_rcopy(srcs[a].at[c], lands[a].at[_slot(x, y, perm), c], send, recv, k, (px, py, cc),
                                        kr=6 * a + 2 * j + c))
                    got = lands[a].at[_slot(px, py, perm), cc]
                    recvs.append(_rcopy(got, got, send, recv, k, (x, y, c)))
        return sends, recvs
    return plan


def _gather_half_plan(perms, halved):
    def plan(srcs, lands, send, recv):
        x, y, c = _mesh_pos()
        sends, recvs = [], []
        for a, perm in enumerate(perms):
            for j, (px, py) in enumerate(_other_chips(x, y)):
                k = 3 * a + j
                mine, theirs = _slot(x, y, perm), _slot(px, py, perm)
                if halved[a]:
                    sends.append(_rcopy(srcs[a].at[c], lands[a].at[mine, c], send, recv, k, (px, py, c)))
                    got = lands[a].at[theirs, c]
                else:
                    sends.append(_rcopy(srcs[a], lands[a].at[mine], send, recv, k, (px, py, c)))
                    got = lands[a].at[theirs]
                recvs.append(_rcopy(got, got, send, recv, k, (x, y, c)))
        return sends, recvs
    return plan


def _reduce_plan(perm):
    def plan(srcs, lands, send, recv):
        x, y, c = _mesh_pos()
        src, land = srcs[0], lands[0]
        sends = []
        for j, (px, py) in enumerate(_other_chips(x, y)):
            for hf in (0, 1):
                sends.append(_rcopy(src.at[_slot(px, py, perm), hf], land.at[2 * j + c], send, recv, 2 * j + hf,
                                    (px, py, hf), kr=2 * j + c))
        sends.append(_rcopy(src.at[_slot(x, y, perm), 1 - c], land.at[6], send, recv, 6, (x, y, 1 - c)))
        recvs = [_rcopy(land.at[i], land.at[i], send, recv, i, (x, y, c)) for i in range(7)]
        return sends, recvs
    return plan


def _sibling_share(rs):
    na = len(rs)

    def body(*refs):
        ins, outs, (send, recv) = refs[:na], refs[na:2 * na], refs[2 * na:]
        x, y, c = _mesh_pos()
        cps = [_rcopy(ins[a], outs[a], send, recv, a, (x, y, 1 - c)) for a in range(na)]
        for cp in cps:
            cp.start()
        for cp in cps:
            cp.wait()

    return pl.pallas_call(
        body, name="rs_share", in_specs=[ANY] * na, out_specs=[ANY] * na,
        out_shape=[jax.ShapeDtypeStruct(r.shape, F32) for r in rs],
        scratch_shapes=[pltpu.SemaphoreType.DMA((na,)), pltpu.SemaphoreType.DMA((na,))],
    )(*rs)


def _add_pieces(name, g, got, where):
    _, _, r2, cc = g.shape
    tr = 128

    def body(w_ref, g_ref, r_ref, o_ref):
        del w_ref
        acc = g_ref[...]
        for i in range(7):
            acc = acc + r_ref[i].astype(F32)
        o_ref[...] = acc

    return pl.pallas_call(
        body, name=name,
        grid_spec=pltpu.PrefetchScalarGridSpec(
            num_scalar_prefetch=1, grid=(r2 // tr,),
            in_specs=[_bs((None, None, tr, cc), lambda i, w_ref: (w_ref[0], w_ref[1], i, 0)),
                      _bs((7, tr, cc), lambda i, w_ref: (0, i, 0))],
            out_specs=_bs((tr, cc), lambda i, w_ref: (i, 0))),
        out_shape=jax.ShapeDtypeStruct((r2, cc), F32), compiler_params=_cp("parallel"),
    )(where, g, got)


def _adam_math(w, g, m, v):
    m = ADAM_B1 * m + (1.0 - ADAM_B1) * g
    v = ADAM_B2 * v + (1.0 - ADAM_B2) * (g * g)
    m_hat = m / (1.0 - ADAM_B1 ** ADAM_STEP)
    v_hat = v / (1.0 - ADAM_B2 ** ADAM_STEP)
    return -ADAM_LR * (m_hat / (jnp.sqrt(v_hat) + ADAM_EPS) + ADAM_WD * w), m, v


def _adam_big(name, w, g_mine, g_sib, m, v, c_arr):
    r, cols = w.shape
    tr = 128
    per = r // 2 // tr

    def body(c_ref, w_ref, a_ref, b_ref, m_ref, v_ref, g_ref, d_ref, m2_ref, v2_ref):
        g = jnp.where(pl.program_id(0) == c_ref[0], a_ref[...], b_ref[...])
        g_ref[...] = g
        d_ref[...], m2_ref[...], v2_ref[...] = _adam_math(w_ref[...], g, m_ref[...], v_ref[...])

    spec = _bs((tr, cols), lambda h, i, c_ref: (h * per + i, 0))
    half = _bs((tr, cols), lambda h, i, c_ref: (i, 0))
    out = jax.ShapeDtypeStruct((r, cols), F32)
    return pl.pallas_call(
        body, name=name,
        grid_spec=pltpu.PrefetchScalarGridSpec(
            num_scalar_prefetch=1, grid=(2, per), in_specs=[spec, half, half, spec, spec], out_specs=[spec] * 4),
        out_shape=[out] * 4, compiler_params=_cp("parallel", "parallel"),
    )(c_arr, w, g_mine, g_sib, m, v)


_CLASS_SHAPE = {"a": (8, D), "b": (8, RW), "c": (8, 2 * DFF), "d": (1048, HD)}
_SMALL = (
    ("g_mix", "a", 0, 1, D), ("g_ffn", "a", 1, 1, D),
    ("rec_conv_w", "b", 0, 4, RW), ("rec_conv_b", "b", 4, 1, RW), ("lru_lambda", "b", 5, 1, RW),
    ("g_attn_out", "b", 6, 1, RW), ("g_rec_out", "b", 7, 1, RW),
    ("ffn_conv_w", "c", 0, 3, 2 * DFF), ("ffn_conv_b", "c", 3, 1, 2 * DFF),
    ("w_rg", "d", 0, RW, HD), ("w_ig", "d", RW, RW, HD), ("b_rg", "d", 2 * RW, 8, HD), ("b_ig", "d", 2 * RW + 8, 8, HD),
    ("q_norm_g", "d", 2 * RW + 16, 1, HD), ("k_norm_g", "d", 2 * RW + 17, 1, HD),
)
_LOSS_ROW = 2
_CLASSES = ("a", "b", "c", "d")
_CLASS_OWNER = {"a": 0, "b": 0, "c": 0, "d": 1}


def _small_allreduce(g, loss_blk):
    names = [s[0] for s in _SMALL]
    nin = len(names) + 1

    def body(*refs):
        ins = dict(zip(names, refs[:len(names)]))
        loss_ref = refs[len(names)]
        outs = dict(zip(_CLASSES, refs[nin:nin + 4]))
        pair = dict(zip(_CLASSES, refs[nin + 4:nin + 8]))
        quad = dict(zip(_CLASSES, refs[nin + 8:nin + 12]))
        send, recv = refs[nin + 12:]
        x, y, c = _mesh_pos()
        chip = 2 * x + y
        pair["a"][c] = jnp.zeros(_CLASS_SHAPE["a"], F32)
        pair["b"][c] = ins["rec_conv_w"][...]
        pair["c"][c] = ins["ffn_conv_w"][...]
        pair["d"][c, 2 * RW + 16:, :] = jnp.zeros((8, HD), F32)
        for name, k, r0, nr, _ in _SMALL:
            if name in ("rec_conv_w", "ffn_conv_w"):
                continue
            pair[k][c, r0:r0 + nr, :] = ins[name][...]
        pair["a"][c, _LOSS_ROW:_LOSS_ROW + 1, :] = jnp.broadcast_to(loss_ref[0:1, 0:1], (1, D))
        cps = [_rcopy(pair[k].at[c], pair[k].at[c], send, recv, ki, (x, y, 1 - c)) for ki, k in enumerate(_CLASSES)]
        for cp in cps:
            cp.start()
        for ki, k in enumerate(_CLASSES):
            _rcopy(pair[k].at[1 - c], pair[k].at[1 - c], send, recv, ki, (x, y, c)).wait_recv()
            quad[k][chip] = pair[k][0] + pair[k][1]
        for cp in cps:
            cp.wait_send()
        for ki, k in enumerate(_CLASSES):
            owner = _CLASS_OWNER[k]

            @pl.when(c == owner)
            def _(ki=ki, k=k):
                cps2 = [_rcopy(quad[k].at[chip], quad[k].at[chip], send, recv, 4 + 3 * ki + j, (px, py, c))
                        for j, (px, py) in enumerate(_other_chips(x, y))]
                for cp in cps2:
                    cp.start()
                for j, (px, py) in enumerate(_other_chips(x, y)):
                    got = quad[k].at[2 * px + py]
                    _rcopy(got, got, send, recv, 4 + 3 * ki + j, (x, y, c)).wait_recv()
                outs[k][...] = ((quad[k][0] + quad[k][1]) + quad[k][2]) + quad[k][3]
                share = _rcopy(outs[k], outs[k], send, recv, 16 + ki, (x, y, 1 - c))
                share.start()
                for cp in cps2:
                    cp.wait_send()
                share.wait_send()

        for ki, k in enumerate(_CLASSES):
            @pl.when(c != _CLASS_OWNER[k])
            def _(ki=ki, k=k):
                _rcopy(outs[k], outs[k], send, recv, 16 + ki, (x, y, c)).wait_recv()

    vm = pl.BlockSpec(memory_space=pltpu.VMEM)
    return pl.pallas_call(
        body, name="small_allreduce", in_specs=[vm] * nin, out_specs=[vm] * 4,
        out_shape=[jax.ShapeDtypeStruct(_CLASS_SHAPE[k], F32) for k in _CLASSES],
        scratch_shapes=[pltpu.VMEM((2,) + _CLASS_SHAPE[k], F32) for k in _CLASSES]
        + [pltpu.VMEM((NCHIP,) + _CLASS_SHAPE[k], F32) for k in _CLASSES]
        + [pltpu.SemaphoreType.DMA((20,)), pltpu.SemaphoreType.DMA((20,))],
        compiler_params=pltpu.CompilerParams(vmem_limit_bytes=VMEM_LIMIT),
    )(*[g[n] for n in names], loss_blk)


def _adam_small(red, w, m, v):
    names = [s[0] for s in _SMALL]
    n = len(names)

    def body(*refs):
        red_refs = dict(zip(_CLASSES, refs[:4]))
        w_refs, m_refs, v_refs = refs[4:4 + n], refs[4 + n:4 + 2 * n], refs[4 + 2 * n:4 + 3 * n]
        loss_ref = refs[4 + 3 * n]
        out_refs = refs[5 + 3 * n:]
        x, y, _ = _mesh_pos()
        chip = 2 * x + y
        loss_ref[...] = jnp.broadcast_to(red_refs["a"][_LOSS_ROW:_LOSS_ROW + 1, 0:1], loss_ref.shape)
        for pi, (name, k, r0, nr, width) in enumerate(_SMALL):
            gfull = red_refs[k][r0:r0 + nr, :]
            if name == "rec_conv_w":
                parts = [gfull[:, 128 * s:128 * (s + 1)] for s in range(NCHIP)]
                g = jnp.where(chip == 0, parts[0], jnp.where(chip == 1, parts[1], jnp.where(chip == 2, parts[2], parts[3])))
            elif name == "ffn_conv_w":
                parts = [gfull[:, FC * s:FC * (s + 1)] for s in range(NCHIP)]
                g = jnp.where(chip == 0, parts[0], jnp.where(chip == 1, parts[2], jnp.where(chip == 2, parts[1], parts[3])))
            elif name == "ffn_conv_b":
                g = jnp.concatenate([gfull[:, FC * s:FC * (s + 1)] for s in (0, 2, 1, 3)], axis=1)
            else:
                g = gfull
            d, m2, v2 = _adam_math(w_refs[pi][...], g, m_refs[pi][...], v_refs[pi][...])
            o = out_refs[4 * pi:4 * pi + 4]
            o[0][...], o[1][...], o[2][...], o[3][...] = g, d, m2, v2

    vm = pl.BlockSpec(memory_space=pltpu.VMEM)
    outs = [jax.ShapeDtypeStruct((1, 128), F32)]
    for name in names:
        outs += [jax.ShapeDtypeStruct(w[name].shape, F32)] * 4
    res = pl.pallas_call(
        body, name="adam_small", in_specs=[vm] * (4 + 3 * n), out_specs=[vm] * len(outs), out_shape=outs,
        compiler_params=pltpu.CompilerParams(vmem_limit_bytes=VMEM_LIMIT),
    )(*red, *[w[k] for k in names], *[m[k] for k in names], *[v[k] for k in names])
    return res[0], {name: res[1 + 4 * i:5 + 4 * i] for i, name in enumerate(names)}


_WEIGHTS = ("g_mix", "w_in", "q_norm_g", "k_norm_g", "rec_conv_w", "rec_conv_b", "w_rg", "b_rg", "w_ig", "b_ig",
            "lru_lambda", "g_attn_out", "g_rec_out", "w_out", "g_ffn", "w_up", "ffn_conv_w", "ffn_conv_b", "w_down")
_BIG = ("w_in", "w_out", "w_up", "w_down")
_BIG_PERM = {"w_in": False, "w_out": False, "w_up": True, "w_down": False}
_SMALL_2D = {"w_rg": (RW, HD), "w_ig": (RW, HD), "b_rg": (8, HD), "b_ig": (8, HD), "rec_conv_w": (4, 128),
             "ffn_conv_w": (3, FC)}


def _halves(a):
    r, c = a.shape
    return a.reshape(2, r // 2, c)


def kernel(x, positions, g_mix, w_in, q_norm_g, k_norm_g, rec_conv_w, rec_conv_b, w_rg, b_rg, w_ig, b_ig, lru_lambda, g_attn_out, g_rec_out, w_out, g_ffn, w_up, ffn_conv_w, ffn_conv_b, w_down, loss_target, m_g_mix, m_w_in, m_q_norm_g, m_k_norm_g, m_rec_conv_w, m_rec_conv_b, m_w_rg, m_b_rg, m_w_ig, m_b_ig, m_lru_lambda, m_g_attn_out, m_g_rec_out, m_w_out, m_g_ffn, m_w_up, m_ffn_conv_w, m_ffn_conv_b, m_w_down, v_g_mix, v_w_in, v_q_norm_g, v_k_norm_g, v_rec_conv_w, v_rec_conv_b, v_w_rg, v_b_rg, v_w_ig, v_b_ig, v_lru_lambda, v_g_attn_out, v_g_rec_out, v_w_out, v_g_ffn, v_w_up, v_ffn_conv_w, v_ffn_conv_b, v_w_down):
    given = dict(g_mix=g_mix, w_in=w_in, q_norm_g=q_norm_g, k_norm_g=k_norm_g, rec_conv_w=rec_conv_w, rec_conv_b=rec_conv_b, w_rg=w_rg, b_rg=b_rg, w_ig=w_ig, b_ig=b_ig, lru_lambda=lru_lambda, g_attn_out=g_attn_out, g_rec_out=g_rec_out, w_out=w_out, g_ffn=g_ffn, w_up=w_up, ffn_conv_w=ffn_conv_w, ffn_conv_b=ffn_conv_b, w_down=w_down)
    given_m = dict(g_mix=m_g_mix, w_in=m_w_in, q_norm_g=m_q_norm_g, k_norm_g=m_k_norm_g, rec_conv_w=m_rec_conv_w, rec_conv_b=m_rec_conv_b, w_rg=m_w_rg, b_rg=m_b_rg, w_ig=m_w_ig, b_ig=m_b_ig, lru_lambda=m_lru_lambda, g_attn_out=m_g_attn_out, g_rec_out=m_g_rec_out, w_out=m_w_out, g_ffn=m_g_ffn, w_up=m_w_up, ffn_conv_w=m_ffn_conv_w, ffn_conv_b=m_ffn_conv_b, w_down=m_w_down)
    given_v = dict(g_mix=v_g_mix, w_in=v_w_in, q_norm_g=v_q_norm_g, k_norm_g=v_k_norm_g, rec_conv_w=v_rec_conv_w, rec_conv_b=v_rec_conv_b, w_rg=v_w_rg, b_rg=v_b_rg, w_ig=v_w_ig, b_ig=v_b_ig, lru_lambda=v_lru_lambda, g_attn_out=v_g_attn_out, g_rec_out=v_g_rec_out, w_out=v_w_out, g_ffn=v_g_ffn, w_up=v_w_up, ffn_conv_w=v_ffn_conv_w, ffn_conv_b=v_ffn_conv_b, w_down=v_w_down)
    shapes = {n: a.shape for n, a in given.items()}

    def two_d(n, a):
        a = a[0]
        return a.reshape(_SMALL_2D[n]) if n in _SMALL_2D else (a if a.ndim == 2 else a[None])

    w = {n: two_d(n, a) for n, a in given.items()}
    m = {n: two_d(n, a) for n, a in given_m.items()}
    v = {n: two_d(n, a) for n, a in given_v.items()}
    cc = lax.axis_index("c").astype(jnp.int32)
    cx, cy = lax.axis_index("x").astype(jnp.int32), lax.axis_index("y").astype(jnp.int32)
    slot = {False: 2 * cx + cy, True: 2 * cy + cx}

    shards = {"w_in": _halves(_cast_bf16("cast_w_in", w["w_in"]))}
    first = [shards["w_in"], jnp.pad(w["ffn_conv_w"], ((0, 5), (0, 0))), jnp.pad(w["rec_conv_w"], ((0, 4), (0, 0)))]
    first_perm = [False, True, False]
    first_plan = _gather_half_plan(first_perm, [True, False, False])
    in_flight = _split_start(
        "gather_in_start", first,
        [lax.dynamic_update_slice(lax.empty((NCHIP,) + a.shape, a.dtype), a[None], (slot[pm],) + (0,) * a.ndim)
         for a, pm in zip(first, first_perm)], first_plan, 3 * len(first))
    for n in ("w_out", "w_up", "w_down"):
        shards[n] = _halves(_cast_bf16(f"cast_{n}", w[n], after=(in_flight[4],)))
    p = {n: w[n] for n in ("g_mix", "g_ffn", "q_norm_g", "k_norm_g", "rec_conv_b", "lru_lambda", "g_attn_out", "g_rec_out")}
    p.update(w_rg=w["w_rg"].reshape(8, HD, HD), w_ig=w["w_ig"].reshape(8, HD, HD), b_rg=w["b_rg"], b_ig=w["b_ig"],
             ffn_conv_b=jnp.concatenate([w["ffn_conv_b"][:, FC * s:FC * (s + 1)] for s in (0, 2, 1, 3)], axis=1))

    class Exchange:
        rest = ("w_out", "w_up", "w_down")
        order = []
        flight = {}

        def wait_first(self, after):
            send, recv, srcs, lands, _ = in_flight
            f_in, f_fcw, f_rcw = _split_wait("gather_in_wait", send, recv, srcs, lands, first_plan,
                                             (after,) + tuple(shards[n] for n in self.rest))
            (f_in,) = _sibling_fill([f_in], [False])
            return dict(w_in=f_in.reshape(NCHIP, D, INW // NCHIP), ffn_conv_w=f_fcw,
                        rec_conv_w=f_rcw.transpose(1, 0, 2).reshape(8, RW))

        def start_rest(self):
            srcs = [shards[n] for n in self.rest]
            lands = [lax.dynamic_update_slice(lax.empty((NCHIP,) + s.shape, BF16), s[None], (slot[_BIG_PERM[n]], 0, 0, 0))
                     for n, s in zip(self.rest, srcs)]
            plan = _gather_plan([_BIG_PERM[n] for n in self.rest])
            send, recv, srcs, lands, token = _split_start("gather_rest_start", srcs, lands, plan, 6 * len(srcs))
            self.flight["rest"] = (send, recv, srcs, lands, plan)
            return (token,)

        def wait_rest(self, after):
            send, recv, srcs, lands, plan = self.flight.pop("rest")
            f_out, f_up, f_down = _split_wait("gather_rest_wait", send, recv, srcs, lands, plan, after)
            return dict(w_out=f_out.reshape(D, D), w_up=f_up.reshape(NCHIP, D, FC), w_down=f_down.reshape(DFF, D))

        def reduce_start(self, name, g32, g16):
            r2, cols = shards[name].shape[1:]
            plan = _reduce_plan(_BIG_PERM[name])
            send, recv, srcs, lands, token = _split_start(
                f"reduce_{name}_start", [g16.reshape(NCHIP, 2, r2, cols)], [lax.empty((7, r2, cols), BF16)], plan, 7)
            self.flight[name] = (send, recv, srcs, lands, plan, g32.reshape(NCHIP, 2, r2, cols))
            self.order.append(name)
            return (token,)

        def finish(self, after):
            mine = {}
            for name in self.order:
                send, recv, srcs, lands, plan, g32 = self.flight.pop(name)
                (got,) = _split_wait(f"reduce_{name}_wait", send, recv, srcs, lands, plan, after)
                where = jnp.stack([slot[_BIG_PERM[name]], cc])
                mine[name] = after = _add_pieces(f"reduce_{name}_add", g32, got, where)
            theirs = dict(zip(_BIG, _sibling_share([mine[n] for n in _BIG])))
            return mine, theirs

    exch = Exchange()

    loss_blk, grad_x, g = _local_step(x[0], positions.reshape(T, 1), loss_target[0], p, exch)

    out_g, out_d, out_m, out_v = {}, {}, {}, {}
    red = _small_allreduce(g, loss_blk)
    loss_row, small_out = _adam_small(red, w, m, v)
    for n, (gn, dn, mn, vn) in small_out.items():
        out_g[n], out_d[n], out_m[n], out_v[n] = gn, dn, mn, vn

    mine, theirs = exch.finish(red[0])
    for n in _BIG:
        out_g[n], out_d[n], out_m[n], out_v[n] = _adam_big(f"adam_{n}", w[n], mine[n], theirs[n], m[n], v[n], cc.reshape(1))

    outs = [loss_row[0, 0], grad_x[None]]
    for group in (out_g, out_d, out_m, out_v):
        outs += [group[n].reshape(shapes[n]) for n in _WEIGHTS]
    return tuple(outs)
```

```python
import math

import jax
import jax.numpy as jnp
import numpy as np
from jax import lax
from jax.experimental import pallas as pl
from jax.experimental.pallas import tpu as pltpu

F32 = jnp.float32
BF16 = jnp.bfloat16

T = 4096
D = 1024
HD = 64
AW = 512
RW = 512
INW = 2560
DFF = 3072
NCHIP = 4
EPS = 1e-6
NEG = -1e30
LRU_C = 8.0
ROPE_THETA = 10000.0
BLK = 128
DILATIONS = (1, 4, 16)
ADAM_LR, ADAM_B1, ADAM_B2, ADAM_EPS, ADAM_WD, ADAM_STEP = 0.001, 0.9, 0.999, 1e-08, 0.01, 10
VMEM_LIMIT = 56 * 1024 * 1024
MESH = pl.DeviceIdType.MESH

NN = (((1,), (0,)), ((), ()))
NT = (((1,), (1,)), ((), ()))
TN = (((0,), (0,)), ((), ()))


def _cp(*sem):
    return pltpu.CompilerParams(dimension_semantics=sem, vmem_limit_bytes=VMEM_LIMIT)


def _bs(shape, fn):
    return pl.BlockSpec(shape, fn)


def _dot(a, b, dims=NN):
    return lax.dot_general(a, b, dims, preferred_element_type=F32)


_GC = math.sqrt(2.0 / math.pi)


def _gelu(x):
    return x * (0.5 + 0.5 * jnp.tanh(x * (_GC + (_GC * 0.044715) * (x * x))))


def _gelu_and_grad(x):
    x2 = x * x
    th = jnp.tanh(x * (_GC + (_GC * 0.044715) * x2))
    cdf = 0.5 + 0.5 * th
    dg = cdf + (x * (1.0 - th * th)) * ((0.5 * _GC) + (1.5 * 0.044715 * _GC) * x2)
    return x * cdf, dg


def _softplus(x):
    e = jnp.exp(-jnp.abs(x))
    u = 1.0 + e
    l1p = jnp.where(u == 1.0, e, jnp.log(u) * (e / (u - 1.0)))
    return jnp.maximum(x, 0.0) + l1p


def _segsum(z, e_bf16):
    hi = z.astype(BF16)
    lo = (z - hi.astype(F32)).astype(BF16)
    parts = []
    for c0 in range(0, z.shape[1], 128):
        parts.append(_dot(hi[:, c0:c0 + 128], e_bf16) + _dot(lo[:, c0:c0 + 128], e_bf16))
    return jnp.concatenate(parts, axis=1)


def _mm(name, a, b, mode, tm, tn, out_dtype=F32, res=None, stack=0, twin_bf16=False, after=(), a_full=False,
        b_full=False):
    if mode == "nn":
        (m, k), n = a.shape, (b.shape[1] if not stack else stack * b.shape[2])
        a_spec = _bs((tm, k), lambda j, i: (i, 0))
        if stack:
            per = b.shape[2] // tn
            b_spec = _bs((None, k, tn), lambda j, i: (j // per, 0, j % per))
        else:
            b_spec = _bs((k, tn), lambda j, i: (0, j))
    elif mode == "nt":
        (m, k), n = a.shape, (b.shape[0] if not stack else b.shape[1])
        a_spec = _bs((tm, k), lambda j, i: (i, 0))
        b_spec = _bs((stack, tn, k // stack), lambda j, i: (0, j, 0)) if stack else _bs((tn, k), lambda j, i: (j, 0))
    else:
        (k, m), n = a.shape, b.shape[1]
        a_spec, b_spec = _bs((k, tm), lambda j, i: (0, i)), _bs((k, tn), lambda j, i: (0, j))
    assert m % tm == 0 and n % tn == 0
    o_spec = _bs((tm, tn), lambda j, i: (i, j))
    o_shape = (m, n)
    if mode == "tn" and stack:
        per = n // stack // tn
        o_spec = _bs((None, tm, tn), lambda j, i: (j // per, i, j % per))
        o_shape = (stack, m, n // stack)
    dims = {"nn": NN, "nt": NT, "tn": TN}[mode]
    once = pl.Buffered(1)
    if a_full:
        a_spec = pl.BlockSpec(a.shape, lambda j, i: (0, 0), pipeline_mode=once)
    if b_full:
        assert n == tn
        b_spec = pl.BlockSpec(b_spec.block_shape, b_spec.index_map, pipeline_mode=once)

    def product(a_ref, b_ref):
        if a_full:
            mine = pl.ds(pl.multiple_of(pl.program_id(1) * tm, tm), tm)
            take = (lambda cols: a_ref[:, mine]) if mode == "tn" else (lambda cols: a_ref[mine, cols])
        else:
            take = lambda cols: a_ref[:, cols]
        if mode == "nt" and stack:
            cs = k // stack
            acc = _dot(take(pl.ds(0, cs)), b_ref[0], NT)
            for s in range(1, stack):
                acc = acc + _dot(take(pl.ds(s * cs, cs)), b_ref[s], NT)
            return acc
        return _dot(take(slice(None)), b_ref[...], dims)

    nres = 0 if res is None else 1

    def body(a_ref, b_ref, *rest):
        acc = product(a_ref, b_ref)
        if nres:
            acc = rest[0][...] + acc
        outs = rest[nres + len(after):]
        outs[0][...] = acc.astype(out_dtype)
        if twin_bf16:
            outs[1][...] = acc.astype(BF16)

    ins = (a, b) + ((res,) if nres else ()) + tuple(after)
    specs = [a_spec, b_spec] + ([o_spec] if nres else []) + [pl.BlockSpec(memory_space=pl.ANY)] * len(after)
    shapes = [jax.ShapeDtypeStruct(o_shape, out_dtype)] + ([jax.ShapeDtypeStruct(o_shape, BF16)] if twin_bf16 else [])
    out = pl.pallas_call(
        body, name=name, grid=(n // tn, m // tm), in_specs=specs, out_specs=[o_spec] * len(shapes),
        out_shape=shapes, compiler_params=_cp("parallel", "parallel"),
    )(*ins)
    return tuple(out) if twin_bf16 else out[0]


def _rms_fwd(name, x, g):
    tr = 512

    def body(x_ref, g_ref, o_ref):
        xv = x_ref[...]
        r = lax.rsqrt(jnp.mean(xv * xv, axis=-1, keepdims=True) + EPS)
        o_ref[...] = ((xv * r) * g_ref[...]).astype(BF16)

    return pl.pallas_call(
        body, name=name, grid=(T // tr,), in_specs=[_bs((tr, D), lambda i: (i, 0)), _bs((1, D), lambda i: (0, 0))],
        out_specs=_bs((tr, D), lambda i: (i, 0)), out_shape=jax.ShapeDtypeStruct((T, D), BF16),
        compiler_params=_cp("parallel"),
    )(x, g)


def _rms_bwd(name, x, g, dy, dres, want_bf16, after=()):
    tr = 256
    halves = dy.ndim == 3

    def body(x_ref, g_ref, dy_ref, dr_ref, *rest):
        rest = rest[len(after):]
        dx_ref, rest = rest[0], rest[1:]
        dg_ref = rest[-1]
        xv = x_ref[...]
        dyv = dy_ref[0] + dy_ref[1] if halves else dy_ref[...]
        r = lax.rsqrt(jnp.mean(xv * xv, axis=-1, keepdims=True) + EPS)
        gdy = g_ref[...] * dyv
        dx = r * gdy - xv * ((r * r * r) * jnp.mean(xv * gdy, axis=-1, keepdims=True)) + dr_ref[...]
        dx_ref[...] = dx
        if want_bf16:
            rest[0][...] = dx.astype(BF16)

        @pl.when(pl.program_id(0) == 0)
        def _():
            dg_ref[...] = jnp.zeros_like(dg_ref)

        dg_ref[...] += jnp.sum(dyv * (xv * r), axis=0, keepdims=True)

    row = _bs((tr, D), lambda i: (i, 0))
    vec = _bs((1, D), lambda i: (0, 0))
    outs = [jax.ShapeDtypeStruct((T, D), F32)] + ([jax.ShapeDtypeStruct((T, D), BF16)] if want_bf16 else [])
    dy_spec = _bs((2, tr, D), lambda i: (0, i, 0)) if halves else row
    return pl.pallas_call(
        body, name=name, grid=(T // tr,),
        in_specs=[row, vec, dy_spec, row] + [pl.BlockSpec(memory_space=pl.ANY)] * len(after),
        out_specs=[row] * len(outs) + [vec], out_shape=outs + [jax.ShapeDtypeStruct((1, D), F32)],
        compiler_params=_cp("arbitrary"),
    )(x, g, dy, dres, *after)


def _head_ones():
    idx = np.arange(128) // HD
    return jnp.asarray((idx[:, None] == idx[None, :]).astype(np.float32), dtype=BF16)


def _freq_row():
    half = HD // 2
    inv = ROPE_THETA ** (-(np.arange(half, dtype=np.float64)) / half)
    return jnp.asarray(np.tile(inv, 4)[None, :], dtype=F32)


def _rot_tables(cos128, sin128):
    c = jnp.tile(cos128, (1, 4))
    s = jnp.tile(sin128, (1, 4))
    lane = lax.broadcasted_iota(jnp.int32, (1, AW), 1)
    first = (lane & 32) == 0
    return c, jnp.where(first, -s, s), first


def _swap_halves(y, first):
    return jnp.where(first, pltpu.roll(y, AW - 32, 1), pltpu.roll(y, 32, 1))


def _qk_prep(proj, pos_col, qg, kg):
    tr = 512

    def body(q_ref, k_ref, pos_ref, f_ref, qg_ref, kg_ref, e_ref, qo_ref, ko_ref, cos_ref, sin_ref):
        ang = pos_ref[...].astype(F32) * f_ref[...]
        cos_ref[...] = jnp.cos(ang)
        sin_ref[...] = jnp.sin(ang)
        c, s_signed, first = _rot_tables(cos_ref[...], sin_ref[...])
        e = e_ref[...]

        def norm_rot(xv, g, scale):
            r = lax.rsqrt(_segsum(xv * xv, e) * (1.0 / HD) + EPS)
            y = (xv * r) * g
            return (y * c + _swap_halves(y, first) * s_signed) * scale

        qo_ref[...] = norm_rot(q_ref[...], qg_ref[...], HD ** -0.5)
        ko_ref[...] = norm_rot(k_ref[...], kg_ref[...], 1.0)

    col = lambda j: _bs((tr, AW), lambda i, j=j: (i, j))
    vec = _bs((1, AW), lambda i: (0, 0))
    out = jax.ShapeDtypeStruct((T, AW), F32)
    tab = jax.ShapeDtypeStruct((T, 128), F32)
    tspec = _bs((tr, 128), lambda i: (i, 0))
    return pl.pallas_call(
        body, name="qk_prep", grid=(T // tr,),
        in_specs=[col(0), col(1), _bs((tr, 1), lambda i: (i, 0)), _bs((1, 128), lambda i: (0, 0)), vec, vec,
                  _bs((128, 128), lambda i: (0, 0))],
        out_specs=[col(0)] * 2 + [tspec] * 2, out_shape=[out, out, tab, tab], compiler_params=_cp("parallel"),
    )(proj, proj, pos_col, _freq_row(), qg, kg, _head_ones())


def _qk_bwd(proj, cos_t, sin_t, qg, kg, dq, dk, dv):
    tr = 256

    def body(q_ref, k_ref, cos_ref, sin_ref, qg_ref, kg_ref, e_ref, dq_ref, dk_ref, dv_ref, o_ref, dqg_ref, dkg_ref):
        i, j = pl.program_id(0), pl.program_id(1)

        @pl.when((i == 0) & (j == 0))
        def _():
            dqg_ref[...] = jnp.zeros_like(dqg_ref)
            dkg_ref[...] = jnp.zeros_like(dkg_ref)

        def norm_rot_bwd(x_ref, g_ref, dg_ref, d_ref, scale):
            c, s_signed, first = _rot_tables(cos_ref[...], sin_ref[...])
            e = e_ref[...]
            dout = d_ref[...] * scale
            dy = dout * c + _swap_halves(dout * s_signed, first)
            xv, g = x_ref[...], g_ref[...]
            r = lax.rsqrt(_segsum(xv * xv, e) * (1.0 / HD) + EPS)
            gdy = g * dy
            dx = r * gdy - xv * ((r * r * r) * (_segsum(xv * gdy, e) * (1.0 / HD)))
            o_ref[...] = dx.astype(BF16)
            dg_ref[...] += jnp.sum(dy * (xv * r), axis=0, keepdims=True)

        @pl.when(j == 0)
        def _():
            o_ref[...] = dv_ref[...].astype(BF16)

        @pl.when(j == 1)
        def _():
            norm_rot_bwd(q_ref, qg_ref, dqg_ref, dq_ref, HD ** -0.5)

        @pl.when(j == 2)
        def _():
            norm_rot_bwd(k_ref, kg_ref, dkg_ref, dk_ref, 1.0)

    col = lambda jj: _bs((tr, AW), lambda i, j, jj=jj: (i, jj))
    vec = _bs((1, AW), lambda i, j: (0, 0))
    piece = _bs((tr, AW), lambda i, j: (i, 0))
    return pl.pallas_call(
        body, name="qk_bwd", grid=(T // tr, 3),
        in_specs=[col(0), col(1), _bs((tr, 128), lambda i, j: (i, 0)), _bs((tr, 128), lambda i, j: (i, 0)), vec, vec,
                  _bs((128, 128), lambda i, j: (0, 0))] + [piece] * 3,
        out_specs=[_bs((tr, AW), lambda i, j: (i, (j + 2) % 3)), vec, vec],
        out_shape=[jax.ShapeDtypeStruct((T, 3 * AW), BF16), jax.ShapeDtypeStruct((1, AW), F32),
                   jax.ShapeDtypeStruct((1, AW), F32)],
        compiler_params=_cp("arbitrary", "arbitrary"),
    )(proj, proj, cos_t, sin_t, qg, kg, _head_ones(), dq, dk, dv)


RG = 256
QC = 64


def _stacked_band_mask(rows=2 * BLK, q0=0):
    qi = (lax.broadcasted_iota(jnp.int32, (rows, 2 * BLK), 0) + q0) & (BLK - 1)
    kj = lax.broadcasted_iota(jnp.int32, (rows, 2 * BLK), 1)
    rel = qi - kj + BLK
    return (rel >= 0) & (rel <= BLK), lax.broadcasted_iota(jnp.int32, (1, 2 * BLK), 1) >= BLK


def _natural_rows(r0, n_rows, d):
    if d == 1:
        return pl.ds(r0, n_rows)
    ln = T // d
    return pl.ds(r0 // ln + d * (r0 % ln), n_rows, stride=d)


def _regroup_into(dst, src_ref, d, pad, cast=True):
    def step(j, carry):
        r0 = pl.multiple_of(j * RG, RG)
        val = src_ref[_natural_rows(r0, RG, d), :]
        dst[pl.ds(pad + r0, RG), :] = val.astype(dst.dtype) if cast else val
        return carry
    lax.fori_loop(0, T // RG, step, 0)


def _stack_heads(x, h0):
    zero = jnp.zeros_like(x)
    return jnp.concatenate([jnp.where(h0, x, zero), jnp.where(h0, zero, x)], axis=0)


def _attn_fwd(q, k, proj):
    nblk = T // BLK

    def body(q_ref, k_ref, v_ref, a_ref, lse_ref, qs, ks, vs, o0, o1, o2, l0, l1, l2, sb0, sb1):
        band, cur_half = _stacked_band_mask()
        h0 = lax.broadcasted_iota(jnp.int32, (1, 128), 1) < HD
        ks[0:BLK, :] = jnp.zeros((BLK, 128), BF16)
        vs[0:BLK, :] = jnp.zeros((BLK, 128), BF16)
        for d, o_s, l_s in zip(DILATIONS, (o0, o1, o2), (l0, l1, l2)):
            nb = T // d // BLK
            _regroup_into(qs, q_ref, d, 0)
            _regroup_into(ks, k_ref, d, BLK)
            _regroup_into(vs, v_ref, d, BLK)

            def scores(b):
                r0 = pl.multiple_of(b * BLK, BLK)
                return _dot(_stack_heads(qs[pl.ds(r0, BLK), :], h0), ks[pl.ds(r0, 2 * BLK), :], NT)

            def finish(b, s_raw, d=d, nb=nb, o_s=o_s, l_s=l_s):
                r0 = pl.multiple_of(b * BLK, BLK)
                mask = band & (cur_half | ((b & (nb - 1)) > 0))
                s = jnp.where(mask, s_raw, NEG)
                m = jnp.max(s, axis=1, keepdims=True)
                p = jnp.exp(s - m)
                l = jnp.sum(p, axis=1, keepdims=True)
                o = _dot(p.astype(BF16), vs[pl.ds(r0, 2 * BLK), :]) / l
                lse = m + jnp.log(l)
                rows = _natural_rows(r0, BLK, d)
                o_s[rows, :] = jnp.where(h0, o[0:BLK, :], o[BLK:, :])
                l_s[rows, :] = jnp.where(h0, lse[0:BLK, :], lse[BLK:, :])

            sb0[...] = scores(0)

            def step(i, carry):
                b = 2 * i
                sb1[...] = scores(b + 1)
                finish(b, sb0[...])
                sb0[...] = scores(jnp.minimum(b + 2, nblk - 1))
                finish(b + 1, sb1[...])
                return carry

            lax.fori_loop(0, nblk // 2, step, 0)

        def merge(i, carry):
            r = pl.ds(pl.multiple_of(i * RG, RG), RG)
            la, lb, lc = l0[r, :], l1[r, :], l2[r, :]
            m = jnp.maximum(jnp.maximum(la, lb), lc)
            ea, eb, ec = jnp.exp(la - m), jnp.exp(lb - m), jnp.exp(lc - m)
            z = (ea + eb) + ec
            a_ref[r, :] = ((ea * o0[r, :] + eb * o1[r, :]) + ec * o2[r, :]) / z
            lse_ref[r, :] = m + jnp.log(z)
            return carry

        lax.fori_loop(0, T // RG, merge, 0)

    spec = lambda cb: _bs((T, 128), lambda p, cb=cb: (0, cb + p))
    out = jax.ShapeDtypeStruct((T, AW), F32)
    return pl.pallas_call(
        body, name="attn_fwd", grid=(AW // 128,), in_specs=[spec(0), spec(0), spec(8)], out_specs=[spec(0)] * 2,
        out_shape=[out] * 2,
        scratch_shapes=[pltpu.VMEM((T, 128), BF16), pltpu.VMEM((T + BLK, 128), BF16), pltpu.VMEM((T + BLK, 128), BF16)]
        + [pltpu.VMEM((T, 128), F32)] * 6 + [pltpu.VMEM((2 * BLK, 2 * BLK), F32)] * 2,
        compiler_params=_cp("parallel"),
    )(q, k, proj)


def _attn_bwd(q, k, proj, do, lse, delta):
    nblk = T // BLK

    def body(q_ref, k_ref, v_ref, do_ref, l_ref, dl_ref, dq_ref, dk_ref, dv_ref, qs, dos, ks, vs, ls, dls, dks, dvs,
             sa0, sa1, da0, da1):
        band, cur_half = _stacked_band_mask()
        h0 = lax.broadcasted_iota(jnp.int32, (1, 128), 1) < HD
        ks[0:BLK, :] = jnp.zeros((BLK, 128), BF16)
        vs[0:BLK, :] = jnp.zeros((BLK, 128), BF16)
        for d in DILATIONS:
            nb = T // d // BLK
            _regroup_into(qs, q_ref, d, 0)
            _regroup_into(dos, do_ref, d, 0)
            _regroup_into(ks, k_ref, d, BLK)
            _regroup_into(vs, v_ref, d, BLK)
            _regroup_into(ls, l_ref, d, 0, cast=False)
            _regroup_into(dls, dl_ref, d, 0, cast=False)
            dks[...] = jnp.zeros_like(dks)
            dvs[...] = jnp.zeros_like(dvs)

            def scores(b, s_buf, dp_buf):
                r0 = pl.multiple_of(b * BLK, BLK)
                win = pl.ds(r0, 2 * BLK)
                s_buf[...] = _dot(_stack_heads(qs[pl.ds(r0, BLK), :], h0), ks[win, :], NT)
                dp_buf[...] = _dot(_stack_heads(dos[pl.ds(r0, BLK), :], h0), vs[win, :], NT)

            def finish(b, s_buf, dp_buf, d=d, nb=nb):
                r0 = pl.multiple_of(b * BLK, BLK)
                mask = band & (cur_half | ((b & (nb - 1)) > 0))
                win = pl.ds(r0, 2 * BLK)
                lv, dlv = ls[pl.ds(r0, BLK), :], dls[pl.ds(r0, BLK), :]
                lse2 = jnp.concatenate([lv[:, 0:1], lv[:, HD:HD + 1]], axis=0)
                dl2 = jnp.concatenate([dlv[:, 0:1], dlv[:, HD:HD + 1]], axis=0)
                p = jnp.exp(jnp.where(mask, s_buf[...], NEG) - lse2)
                ds = p * (dp_buf[...] - dl2)
                pb, dsb = p.astype(BF16), ds.astype(BF16)
                dq2 = _dot(dsb, ks[win, :])
                dks[win, :] += _dot(dsb, _stack_heads(qs[pl.ds(r0, BLK), :], h0), TN)
                dvs[win, :] += _dot(pb, _stack_heads(dos[pl.ds(r0, BLK), :], h0), TN)
                rows = _natural_rows(r0, BLK, d)
                dq = jnp.where(h0, dq2[0:BLK, :], dq2[BLK:, :])
                dq_ref[rows, :] = dq if d == 1 else dq_ref[rows, :] + dq

            scores(0, sa0, da0)

            def step(i, carry):
                b = 2 * i
                scores(b + 1, sa1, da1)
                finish(b, sa0, da0)
                scores(jnp.minimum(b + 2, nblk - 1), sa0, da0)
                finish(b + 1, sa1, da1)
                return carry

            lax.fori_loop(0, nblk // 2, step, 0)

            def back(j, carry, d=d):
                r0 = pl.multiple_of(j * RG, RG)
                rows = _natural_rows(r0, RG, d)
                src = pl.ds(BLK + r0, RG)
                dk_ref[rows, :] = dks[src, :] if d == 1 else dk_ref[rows, :] + dks[src, :]
                dv_ref[rows, :] = dvs[src, :] if d == 1 else dv_ref[rows, :] + dvs[src, :]
                return carry

            lax.fori_loop(0, T // RG, back, 0)

    spec = lambda cb: _bs((T, 128), lambda p, cb=cb: (0, cb + p))
    ospec = _bs((T, 128), lambda p: (0, p))
    out = jax.ShapeDtypeStruct((T, AW), F32)
    return pl.pallas_call(
        body, name="attn_bwd", grid=(AW // 128,), in_specs=[spec(0), spec(0), spec(8), spec(0), spec(0), spec(0)],
        out_specs=[ospec] * 3, out_shape=[out] * 3,
        scratch_shapes=[pltpu.VMEM((T, 128), BF16), pltpu.VMEM((T, 128), BF16), pltpu.VMEM((T + BLK, 128), BF16),
                        pltpu.VMEM((T + BLK, 128), BF16), pltpu.VMEM((T, 128), F32), pltpu.VMEM((T, 128), F32),
                        pltpu.VMEM((T + BLK, 128), F32), pltpu.VMEM((T + BLK, 128), F32)]
        + [pltpu.VMEM((2 * BLK, 2 * BLK), F32)] * 4,
        compiler_params=_cp("parallel"),
    )(q, k, proj, do, lse, delta)


def _attn_norm(attn, g_attn):
    tr = 512

    def body(a_ref, g_ref, mix_ref):
        attn = a_ref[...]
        r = lax.rsqrt(jnp.mean(attn * attn, axis=-1, keepdims=True) + EPS)
        mix_ref[...] = ((attn * r) * g_ref[...]).astype(BF16)

    row = _bs((tr, AW), lambda i: (i, 0))
    return pl.pallas_call(
        body, name="attn_norm", grid=(T // tr,), in_specs=[row, _bs((1, AW), lambda i: (0, 0))],
        out_specs=row, out_shape=jax.ShapeDtypeStruct((T, D), BF16), compiler_params=_cp("parallel"),
    )(attn, g_attn)


def _attn_out_bwd(attn, dmix, g_attn):
    tr = 256

    def body(a_ref, d_ref, g_ref, e_ref, do_ref, dl_ref, dg_ref):
        av, dyv = a_ref[...], d_ref[...]
        r = lax.rsqrt(jnp.mean(av * av, axis=-1, keepdims=True) + EPS)
        gdy = g_ref[...] * dyv
        da = r * gdy - av * ((r * r * r) * jnp.mean(av * gdy, axis=-1, keepdims=True))
        do_ref[...] = da
        dl_ref[...] = _segsum(da * av, e_ref[...])

        @pl.when(pl.program_id(0) == 0)
        def _():
            dg_ref[...] = jnp.zeros_like(dg_ref)

        dg_ref[...] += jnp.sum(dyv * (av * r), axis=0, keepdims=True)

    row = _bs((tr, AW), lambda i: (i, 0))
    vec = _bs((1, AW), lambda i: (0, 0))
    return pl.pallas_call(
        body, name="attn_out_bwd", grid=(T // tr,), in_specs=[row, row, vec, _bs((128, 128), lambda i: (0, 0))],
        out_specs=[row, row, vec],
        out_shape=[jax.ShapeDtypeStruct((T, AW), F32), jax.ShapeDtypeStruct((T, AW), F32),
                   jax.ShapeDtypeStruct((1, AW), F32)],
        compiler_params=_cp("arbitrary"),
    )(attn, dmix, g_attn, _head_ones())


TRR = 256


def _scan_fwd(a, u):
    n = a.shape[0]
    row = lax.broadcasted_iota(jnp.int32, (n, 1), 0)
    s = 1
    while s < n:
        keep = row >= s
        u = jnp.where(keep, a * pltpu.roll(u, s, 0) + u, u)
        a = jnp.where(keep, a * pltpu.roll(a, s, 0), a)
        s *= 2
    return a, u


def _scan_bwd(c, w):
    n = c.shape[0]
    row = lax.broadcasted_iota(jnp.int32, (n, 1), 0)
    s = 1
    while s < n:
        keep = row < n - s
        w = jnp.where(keep, c * pltpu.roll(w, n - s, 0) + w, w)
        c = jnp.where(keep, c * pltpu.roll(c, n - s, 0), c)
        s *= 2
    return w


def _gates(xc, wrg, wig, brg, big, sp):
    xcb = xc.astype(BF16)
    r = jax.nn.sigmoid(_dot(xcb, wrg) + brg)
    ig = jax.nn.sigmoid(_dot(xcb, wig) + big)
    la = (-LRU_C * r) * sp
    a = jnp.exp(la)
    mult = jnp.sqrt(-jnp.tanh(la) * (a * a + 1.0))
    return r, ig, a, mult


def _conv4(ext_ref, xr, cw_ref, cb_ref, n):
    y = cb_ref[...] + ext_ref[pl.ds(5, n), :] * cw_ref[0:1, :]
    y = y + ext_ref[pl.ds(6, n), :] * cw_ref[1:2, :]
    y = y + ext_ref[pl.ds(7, n), :] * cw_ref[2:3, :]
    return y + xr * cw_ref[3:4, :]


def _rec_fwd(proj, mix, cw, cb, wrg, wig, brg, big, lam, g_rec):
    n = TRR

    def body(xr_ref, gr_ref, cw_ref, cb_ref, wrg_ref, wig_ref, brg_ref, big_ref, lam_ref, g_ref, mix_in,
             mix_ref, h_ref, ext, hcar):
        del mix_in

        @pl.when(pl.program_id(0) == 0)
        def _():
            ext[0:8, :] = jnp.zeros((8, RW), F32)
            hcar[...] = jnp.zeros_like(hcar)

        xr = xr_ref[...]
        ext[8:, :] = xr
        xc = _conv4(ext, xr, cw_ref, cb_ref, n)
        ext[0:8, :] = xr[n - 8:, :]
        sp = _softplus(-lam_ref[...])
        _, ig, a, mult = _gates(xc, wrg_ref[...], wig_ref[...], brg_ref[...], big_ref[...], sp)
        a_s, u_s = _scan_fwd(a, mult * (ig * xc))
        h = u_s + a_s * hcar[7:8, :]
        h_ref[...] = h
        hcar[...] = h[n - 8:, :]
        pre = h * _gelu(gr_ref[...])
        r = lax.rsqrt(jnp.mean(pre * pre, axis=-1, keepdims=True) + EPS)
        mix_ref[...] = ((pre * r) * g_ref[...]).astype(BF16)

    vec = _bs((1, RW), lambda i: (0, 0))
    mat = _bs((RW, RW), lambda i: (0, 0))
    return pl.pallas_call(
        body, name="rec_fwd", grid=(T // n,),
        in_specs=[_bs((n, RW), lambda i: (i, 3)), _bs((n, RW), lambda i: (i, 4)), _bs((8, RW), lambda i: (0, 0)), vec,
                  mat, mat, vec, vec, vec, vec, pl.BlockSpec(memory_space=pl.ANY)],
        out_specs=[_bs((n, RW), lambda i: (i, 1)), _bs((n, RW), lambda i: (i, 0))],
        out_shape=[jax.ShapeDtypeStruct((T, D), BF16), jax.ShapeDtypeStruct((T, RW), F32)],
        scratch_shapes=[pltpu.VMEM((n + 8, RW), F32), pltpu.VMEM((8, RW), F32)],
        input_output_aliases={10: 0}, compiler_params=_cp("arbitrary"),
    )(proj, proj, cw, cb, wrg, wig, brg, big, lam, g_rec, mix)


def _rec_bwd(proj, h, dmix, cw, cb, wrg, wig, brg, big, lam, g_rec):
    n = TRR
    nt = T // n
    hb = n // 8

    def body(xr_ref, xh_ref, gr_ref, h_ref, hh_ref, dm_ref, cw_ref, cb_ref, wrg_ref, wig_ref, brg_ref, big_ref,
             lam_ref, g_ref, dp_ref, xc_ref, dr_ref, di_ref, dcw_ref, dcb_ref, dbr_ref, dbi_ref, dsp_ref,
             dg_ref, ext, exth, extd, adh):
        i, j = pl.program_id(0), pl.program_id(1)
        first_tile = i == nt - 1
        last_tile = i == 0

        @pl.when(j == 0)
        def _():
            @pl.when(last_tile)
            def _():
                for ref in (dcw_ref, dcb_ref, dbr_ref, dbi_ref, dsp_ref, dg_ref):
                    ref[...] = jnp.zeros_like(ref)
                extd[n:, :] = jnp.zeros((8, RW), F32)
                adh[...] = jnp.zeros_like(adh)

            row = lax.broadcasted_iota(jnp.int32, (n, 1), 0)
            xr = xr_ref[...]
            ext[0:8, :] = jnp.where(first_tile, 0.0, xh_ref[...])
            ext[8:, :] = xr
            xc = _conv4(ext, xr, cw_ref, cb_ref, n)
            sp = _softplus(-lam_ref[...])
            wrg, wig = wrg_ref[...], wig_ref[...]
            r, ig, a, mult = _gates(xc, wrg, wig, brg_ref[...], big_ref[...], sp)

            hv = h_ref[...]
            gl, dgl = _gelu_and_grad(gr_ref[...])
            pre = hv * gl
            dyv = dm_ref[...]
            rr = lax.rsqrt(jnp.mean(pre * pre, axis=-1, keepdims=True) + EPS)
            gdy = g_ref[...] * dyv
            dpre = rr * gdy - pre * ((rr * rr * rr) * jnp.mean(pre * gdy, axis=-1, keepdims=True))
            dg_ref[...] += jnp.sum(dyv * (pre * rr), axis=0, keepdims=True)
            dp_ref[:, RW:] = (dpre * hv * dgl).astype(BF16)

            is_last_row = row == n - 1
            w = dpre * gl + jnp.where(is_last_row, adh[0:1, :], 0.0)
            c = jnp.where(is_last_row, 0.0, pltpu.roll(a, n - 1, 0))
            dh = _scan_bwd(c, w)
            adh[...] = (a * dh)[0:8, :]

            exth[0:8, :] = jnp.where(first_tile, 0.0, hh_ref[...])
            exth[8:, :] = hv
            da = dh * exth[pl.ds(7, n), :]
            ixc = ig * xc
            dmult = dh * ixc
            dla = da * a - dmult * ((a * a) / mult)
            dsp_ref[...] += jnp.sum(dla * (-LRU_C * r), axis=0, keepdims=True)
            dpr = (dla * (-LRU_C * sp)) * (r * (1.0 - r))
            dpi = (dh * (mult * xc)) * (ig * (1.0 - ig))
            dprb, dpib = dpr.astype(BF16), dpi.astype(BF16)
            dxc = dh * (mult * ig) + _dot(dprb, wrg, NT) + _dot(dpib, wig, NT)
            dbr_ref[...] += jnp.sum(dpr, axis=0, keepdims=True)
            dbi_ref[...] += jnp.sum(dpi, axis=0, keepdims=True)
            xc_ref[...] = xc.astype(BF16)
            dr_ref[...] = dprb
            di_ref[...] = dpib

            extd[0:n, :] = dxc
            dxr = dxc * cw_ref[3:4, :] + extd[pl.ds(1, n), :] * cw_ref[2:3, :]
            dxr = dxr + extd[pl.ds(2, n), :] * cw_ref[1:2, :] + extd[pl.ds(3, n), :] * cw_ref[0:1, :]
            extd[n:, :] = dxc[0:8, :]
            dcb_ref[...] += jnp.sum(dxc, axis=0, keepdims=True)
            for kk in range(4):
                dcw_ref[kk:kk + 1, :] += jnp.sum(dxc * ext[pl.ds(5 + kk, n), :], axis=0, keepdims=True)

            @pl.when(first_tile)
            def _():
                dsp_ref[...] = dsp_ref[...] * (-jax.nn.sigmoid(-lam_ref[...]))

            dp_ref[:, 0:RW] = dxr.astype(BF16)

    vec = _bs((1, RW), lambda i, j: (0, 0))
    mat = _bs((RW, RW), lambda i, j: (0, 0))
    tile = lambda cblk: _bs((n, RW), lambda i, j, cblk=cblk: (nt - 1 - i, cblk))
    halo = lambda cblk: _bs((8, RW), lambda i, j, cblk=cblk: (jnp.maximum((nt - 1 - i) * hb - 1, 0), cblk))
    bt = jax.ShapeDtypeStruct((T, RW), BF16)
    v = jax.ShapeDtypeStruct((1, RW), F32)
    return pl.pallas_call(
        body, name="rec_bwd", grid=(nt, 1),
        in_specs=[tile(3), halo(3), tile(4), tile(0), halo(0), tile(1), _bs((8, RW), lambda i, j: (0, 0)), vec,
                  mat, mat, vec, vec, vec, vec],
        out_specs=[_bs((n, 2 * RW), lambda i, j: (nt - 1 - i, 0)), tile(0), tile(0), tile(0),
                   _bs((8, RW), lambda i, j: (0, 0)), vec, vec, vec, vec, vec],
        out_shape=[jax.ShapeDtypeStruct((T, 2 * RW), BF16), bt, bt, bt, jax.ShapeDtypeStruct((8, RW), F32),
                   v, v, v, v, v],
        scratch_shapes=[pltpu.VMEM((n + 8, RW), F32), pltpu.VMEM((n + 8, RW), F32), pltpu.VMEM((n + 8, RW), F32),
                        pltpu.VMEM((8, RW), F32)],
        compiler_params=_cp("arbitrary", "arbitrary"),
    )(proj, proj, proj, h, h, dmix, cw, cb, wrg, wig, brg, big, lam, g_rec)


FC = 1536
TRF = 256


LC = 128


class _RowsBack:
    def __init__(self, before):
        row = lax.broadcasted_iota(jnp.int32, before.shape, 0)
        self.top1, self.top2 = row < 1, row < 2
        self.r1, self.r2 = pltpu.roll(before, 1, 0), pltpu.roll(before, 2, 0)

    def step(self, cur):
        r1, r2 = pltpu.roll(cur, 1, 0), pltpu.roll(cur, 2, 0)
        out = jnp.where(self.top1, self.r1, r1), jnp.where(self.top2, self.r2, r2)
        self.r1, self.r2 = r1, r2
        return out


def _up_act(h2, w_up, cw, cb):
    n = TRF
    nt = T // n
    pw = 256
    npc = FC // pw

    def body(h_ref, wg_ref, wu_ref, cwg_ref, cwu_ref, bg_ref, bu_ref, up_ref, a_ref, fa_ref, fb_ref, hx, gb0, gb1,
             ub0, ub1):
        i = pl.program_id(1)
        halo = h_ref[pl.ds(pl.multiple_of(jnp.maximum(i * n - 16, 0), 16), 16), :]
        hx[0:16, :] = jnp.where(i == 0, jnp.zeros_like(halo), halo)
        hx[16:, :] = h_ref[pl.ds(pl.multiple_of(i * n, n), n), :]
        gbufs, ubufs = (gb0, gb1), (ub0, ub1)

        def dots(c):
            hv = hx[...]
            gbufs[c % 2][...] = _dot(hv, wg_ref[:, c * pw:(c + 1) * pw])
            ubufs[c % 2][...] = _dot(hv, wu_ref[:, c * pw:(c + 1) * pw])

        def chain(c):
            gb, ub = gbufs[c % 2], ubufs[c % 2]
            up_ref[:, c * pw:(c + 1) * pw] = gb[16:, :]
            up_ref[:, FC + c * pw:FC + (c + 1) * pw] = ub[16:, :]
            rows8 = lambda v: jnp.broadcast_to(v, (8, LC))
            for sub in range(pw // LC):
                lc = slice(sub * LC, (sub + 1) * LC)
                cols = slice(c * pw + sub * LC, c * pw + (sub + 1) * LC)
                wg = [rows8(cwg_ref[kk:kk + 1, cols]) for kk in range(3)]
                wu = [rows8(cwu_ref[kk:kk + 1, cols]) for kk in range(3)]
                bg, bu = rows8(bg_ref[:, cols]), rows8(bu_ref[:, cols])
                g_back = _RowsBack(gb[pl.ds(8, 8), lc])
                u_back = _RowsBack(ub[pl.ds(8, 8), lc])
                for r in range(0, n, 16):
                    res, fa, fb = [], [], []
                    for rr in (16 + r, 24 + r):
                        g0, u0 = gb[pl.ds(rr, 8), lc], ub[pl.ds(rr, 8), lc]
                        g1, g2 = g_back.step(g0)
                        u1, u2 = u_back.step(u0)
                        ug = ((bg + g2 * wg[0]) + g1 * wg[1]) + g0 * wg[2]
                        uu = ((bu + u2 * wu[0]) + u1 * wu[1]) + u0 * wu[2]
                        gl, dgl = _gelu_and_grad(ug)
                        res.append(gl * uu)
                        fa.append(uu * dgl)
                        fb.append(gl)
                    a_ref[pl.ds(r, 16), cols] = jnp.concatenate(res, axis=0).astype(BF16)
                    fa_ref[pl.ds(r, 16), cols] = jnp.concatenate(fa, axis=0).astype(BF16)
                    fb_ref[pl.ds(r, 16), cols] = jnp.concatenate(fb, axis=0).astype(BF16)

        dots(0)
        for c in range(npc):
            if c + 1 < npc:
                dots(c + 1)
            chain(c)

    wsl = lambda o: _bs((None, D, FC), lambda j, i, o=o: (2 * j + o, 0, 0))
    wsp = lambda o: _bs((None, 8, FC), lambda j, i, o=o: (2 * j + o, 0, 0))
    bsp = lambda o: _bs((1, FC), lambda j, i, o=o: (0, 2 * j + o))
    return pl.pallas_call(
        body, name="up_act", grid=(2, nt),
        in_specs=[pl.BlockSpec((T, D), lambda j, i: (0, 0), pipeline_mode=pl.Buffered(1)), wsl(0), wsl(1),
                  wsp(0), wsp(1), bsp(0), bsp(1)],
        out_specs=[_bs((n, 2 * FC), lambda j, i: (i, j))] + [_bs((n, FC), lambda j, i: (i, j))] * 3,
        out_shape=[jax.ShapeDtypeStruct((T, 2 * DFF), F32)] + [jax.ShapeDtypeStruct((T, DFF), BF16)] * 3,
        scratch_shapes=[pltpu.VMEM((n + 16, D), BF16)] + [pltpu.VMEM((n + 16, pw), F32)] * 4,
        compiler_params=_cp("parallel", "arbitrary"),
    )(h2, w_up, w_up, cw, cw, cb, cb)


def _ffn_bwd(up_pre, fa, fb, dyb, w_down_t, cw, after=()):
    n = TRF
    hb = n // 8
    nt = T // n
    m = n + 8
    pw = 256
    npc = FC // pw

    def body(g_ref, gp_ref, u_ref, up_ref, fa_ref, fan_ref, fb_ref, fbn_ref, dy_ref, wd_ref, wg_ref, wu_ref, *rest):
        o_ref, dw_ref, db_ref, eg0, eu0, dug_s, duu_s, dyx, db0, db1 = rest[len(after):]
        i = pl.program_id(1)
        first, last = i == 0, i == nt - 1

        @pl.when(first)
        def _():
            dw_ref[...] = jnp.zeros_like(dw_ref)
            db_ref[...] = jnp.zeros_like(db_ref)

        tail = dy_ref[pl.ds(pl.multiple_of(jnp.minimum((i + 1) * n, T - 16), 16), 16), :]
        dyx[0:n, :] = dy_ref[pl.ds(pl.multiple_of(i * n, n), n), :]
        dyx[n:, :] = jnp.where(last, jnp.zeros_like(tail), tail)
        dbufs = (db0, db1)

        def dots(c):
            dbufs[c % 2][...] = _dot(dyx[...], wd_ref[:, c * pw:(c + 1) * pw])

        eg0[0:8, :] = jnp.where(first, 0.0, gp_ref[...])
        eg0[8:, :] = g_ref[0:8, :]
        eu0[0:8, :] = jnp.where(first, 0.0, up_ref[...])
        eu0[8:, :] = u_ref[0:8, :]

        def column(ci, dbuf, lc):
            cols = slice(ci * LC, (ci + 1) * LC)
            ucols = slice(FC + ci * LC, FC + (ci + 1) * LC)
            rows8 = lambda v: jnp.broadcast_to(v, (8, LC))
            wg = [rows8(wg_ref[kk:kk + 1, cols]) for kk in range(3)]
            wu = [rows8(wu_ref[kk:kk + 1, cols]) for kk in range(3)]
            zero = jnp.zeros((8, LC), F32)
            acc = [zero] * 8
            g_back, u_back = _RowsBack(eg0[pl.ds(0, 8), cols]), _RowsBack(eu0[pl.ds(0, 8), cols])
            for r in range(0, n + 16, 16):
                src_a, src_b, r16 = (fan_ref, fbn_ref, 0) if r == n else (fa_ref, fb_ref, r)
                fa16 = src_a[pl.ds(r16, 16), cols].astype(F32)
                fb16 = src_b[pl.ds(r16, 16), cols].astype(F32)
                for half in range(1 if r == n else 2):
                    rr = r + 8 * half
                    dv = dbuf[pl.ds(rr, 8), lc]
                    dug = dv * fa16[8 * half:8 * half + 8, :]
                    duu = dv * fb16[8 * half:8 * half + 8, :]
                    dug_s[pl.ds(rr, 8), :] = dug
                    duu_s[pl.ds(rr, 8), :] = duu
                    if rr < n:
                        g0, u0 = g_ref[pl.ds(rr, 8), cols], u_ref[pl.ds(rr, 8), cols]
                        g1, g2 = g_back.step(g0)
                        u1, u2 = u_back.step(u0)
                        gt, ut = (g2, g1, g0), (u2, u1, u0)
                        acc = [acc[0] + dug * gt[0], acc[1] + dug * gt[1], acc[2] + dug * gt[2],
                               acc[3] + duu * ut[0], acc[4] + duu * ut[1], acc[5] + duu * ut[2],
                               acc[6] + dug, acc[7] + duu]
            for r in range(0, n, 16):
                og, ou = [], []
                for rr in (r, r + 8):
                    og.append((dug_s[pl.ds(rr, 8), :] * wg[2] + dug_s[pl.ds(rr + 1, 8), :] * wg[1])
                              + dug_s[pl.ds(rr + 2, 8), :] * wg[0])
                    ou.append((duu_s[pl.ds(rr, 8), :] * wu[2] + duu_s[pl.ds(rr + 1, 8), :] * wu[1])
                              + duu_s[pl.ds(rr + 2, 8), :] * wu[0])
                o_ref[pl.ds(r, 16), cols] = jnp.concatenate(og, axis=0).astype(BF16)
                o_ref[pl.ds(r, 16), ucols] = jnp.concatenate(ou, axis=0).astype(BF16)
            for kk in range(3):
                dw_ref[kk:kk + 1, cols] += jnp.sum(acc[kk], axis=0, keepdims=True)
                dw_ref[kk:kk + 1, ucols] += jnp.sum(acc[3 + kk], axis=0, keepdims=True)
            db_ref[:, cols] += jnp.sum(acc[6], axis=0, keepdims=True)
            db_ref[:, ucols] += jnp.sum(acc[7], axis=0, keepdims=True)

        dots(0)
        for c in range(npc):
            if c + 1 < npc:
                dots(c + 1)
            for sub in range(pw // LC):
                column(c * (pw // LC) + sub, dbufs[c % 2], slice(sub * LC, (sub + 1) * LC))

    main = lambda o: _bs((n, FC), lambda j, i, o=o: (i, 2 * j + o))
    prev = lambda o: _bs((8, FC), lambda j, i, o=o: (jnp.maximum(i * hb - 1, 0), 2 * j + o))
    saved = _bs((n, FC), lambda j, i: (i, j))
    saved_next = _bs((16, FC), lambda j, i: (jnp.minimum((i + 1) * (n // 16), T // 16 - 1), j))
    wsp = lambda o: _bs((None, 8, FC), lambda j, i, o=o: (2 * j + o, 0, 0))
    return pl.pallas_call(
        body, name="ffn_bwd", grid=(2, nt),
        in_specs=[main(0), prev(0), main(1), prev(1), saved, saved_next, saved, saved_next,
                  pl.BlockSpec((T, D), lambda j, i: (0, 0), pipeline_mode=pl.Buffered(1)),
                  _bs((D, FC), lambda j, i: (0, j)), wsp(0), wsp(1)]
        + [pl.BlockSpec(memory_space=pl.ANY)] * len(after),
        out_specs=[_bs((n, 2 * FC), lambda j, i: (i, j)), _bs((8, 2 * FC), lambda j, i: (0, j)),
                   _bs((1, 2 * FC), lambda j, i: (0, j))],
        out_shape=[jax.ShapeDtypeStruct((T, 2 * DFF), BF16), jax.ShapeDtypeStruct((8, 2 * DFF), F32),
                   jax.ShapeDtypeStruct((1, 2 * DFF), F32)],
        scratch_shapes=[pltpu.VMEM((16, FC), F32)] * 2 + [pltpu.VMEM((m, LC), F32)] * 2
        + [pltpu.VMEM((n + 16, D), BF16)] + [pltpu.VMEM((n + 16, pw), F32)] * 2,
        compiler_params=_cp("parallel", "arbitrary"),
    )(up_pre, up_pre, up_pre, up_pre, fa, fa, fb, fb, dyb, w_down_t, cw, cw, *after)


def _down_loss(act, w_down, x1, target):
    tm, tn = 512, D

    def body(a_ref, b_ref, r_ref, t_ref, dy_ref, dyb_ref, l_ref):
        @pl.when((pl.program_id(0) == 0) & (pl.program_id(1) == 0))
        def _():
            l_ref[...] = jnp.zeros_like(l_ref)

        err = (r_ref[...] + _dot(a_ref[...], b_ref[...])) - t_ref[...]
        dy = err * (1.0 / D)
        dy_ref[...] = dy
        dyb_ref[...] = dy.astype(BF16)
        l_ref[...] += jnp.sum(0.5 * (err * err) * (1.0 / D))

    o_spec = _bs((tm, tn), lambda j, i: (i, j))
    return pl.pallas_call(
        body, name="down_loss", grid=(D // tn, T // tm),
        in_specs=[_bs((tm, DFF), lambda j, i: (i, 0)),
                  pl.BlockSpec((DFF, tn), lambda j, i: (0, j), pipeline_mode=pl.Buffered(1)), o_spec, o_spec],
        out_specs=[o_spec, o_spec, _bs((8, 128), lambda j, i: (0, 0))],
        out_shape=[jax.ShapeDtypeStruct((T, D), F32), jax.ShapeDtypeStruct((T, D), BF16),
                   jax.ShapeDtypeStruct((8, 128), F32)],
        compiler_params=_cp("arbitrary", "arbitrary"),
    )(act, w_down, x1, target)


def _block_diag(w):
    eye = jnp.eye(8, dtype=w.dtype)
    return (w[:, :, None, :] * eye[:, None, :, None]).reshape(RW, RW).astype(BF16)


def _diag_blocks(m):
    eye = jnp.eye(8, dtype=m.dtype)
    return (m.reshape(8, HD, 8, HD) * eye[:, None, :, None]).sum(axis=2)


def _local_step(x, pos_col, target, p, exch):
    qg, kg = jnp.tile(p["q_norm_g"], (1, 8)), jnp.tile(p["k_norm_g"], (1, 8))
    wrg, wig = _block_diag(p["w_rg"]), _block_diag(p["w_ig"])
    brg, big = p["b_rg"].reshape(1, RW), p["b_ig"].reshape(1, RW)

    h1 = _rms_fwd("rms1", x, p["g_mix"])
    p = {**p, **exch.wait_first(h1)}
    proj = _mm("mm_in", h1, p["w_in"], "nn", 512, 640, stack=NCHIP, after=exch.start_rest(), a_full=True)
    q, k, cos_t, sin_t = _qk_prep(proj, pos_col, qg, kg)
    attn, lse = _attn_fwd(q, k, proj)
    mix = _attn_norm(attn, p["g_attn_out"])
    mix, hseq = _rec_fwd(proj, mix, p["rec_conv_w"], p["rec_conv_b"], wrg, wig, brg, big, p["lru_lambda"], p["g_rec_out"])
    rest = exch.wait_rest(mix)
    x1 = _mm("mm_out", mix, rest["w_out"], "nn", 512, 512, res=x, a_full=True)
    h2 = _rms_fwd("rms2", x1, p["g_ffn"])
    up_pre, act, fa, fb = _up_act(h2, rest["w_up"], p["ffn_conv_w"], p["ffn_conv_b"])
    dy, dyb, loss_blk = _down_loss(act, rest["w_down"], x1, target)

    g = {}
    tok = exch.reduce_start("w_down", *_mm("wg_down", act, dyb, "tn", 512, 512, twin_bf16=True))
    dup, g["ffn_conv_w"], g["ffn_conv_b"] = _ffn_bwd(up_pre, fa, fb, dyb, rest["w_down"].T, p["ffn_conv_w"], tok)
    tok = exch.reduce_start("w_up", *_mm("wg_up", h2, dup, "tn", 512, 768, stack=NCHIP, twin_bf16=True, a_full=True))
    dh2 = _mm("dg_up", dup, rest["w_up"], "nt", 512, D, stack=NCHIP, after=tok, b_full=True)
    dx1, dx1b, g["g_ffn"] = _rms_bwd("rms2_bwd", x1, p["g_ffn"], dh2, dy, True)
    tok = exch.reduce_start("w_out", *_mm("wg_out", mix, dx1b, "tn", 512, 512, twin_bf16=True, a_full=True))
    dmix = _mm("dg_out", dx1b, rest["w_out"], "nt", 512, 512, after=tok, a_full=True)
    do, delta, g["g_attn_out"] = _attn_out_bwd(attn, dmix, p["g_attn_out"])
    dq, dk, dv = _attn_bwd(q, k, proj, do, lse, delta)
    dqkv, dqg, dkg = _qk_bwd(proj, cos_t, sin_t, qg, kg, dq, dk, dv)
    (drec, xcb, dprb, dpib, g["rec_conv_w"], g["rec_conv_b"], dbr, dbi, dsp, g["g_rec_out"]) = _rec_bwd(
        proj, hseq, dmix, p["rec_conv_w"], p["rec_conv_b"], wrg, wig, brg, big, p["lru_lambda"], p["g_rec_out"])
    dproj = jnp.concatenate([dqkv, drec], axis=1)
    g["w_rg"] = _diag_blocks(_mm("wg_rg", xcb, dprb, "tn", 512, 512)).reshape(RW, HD)
    g["w_ig"] = _diag_blocks(_mm("wg_ig", xcb, dpib, "tn", 512, 512)).reshape(RW, HD)
    g["b_rg"], g["b_ig"] = dbr.reshape(8, HD), dbi.reshape(8, HD)
    g["lru_lambda"] = dsp
    g["q_norm_g"] = dqg.reshape(8, HD).sum(axis=0, keepdims=True)
    g["k_norm_g"] = dkg.reshape(8, HD).sum(axis=0, keepdims=True)
    tok = exch.reduce_start("w_in", *_mm("wg_in", h1, dproj, "tn", 512, 640, stack=NCHIP, twin_bf16=True, a_full=True))
    dh1 = _mm("dg_in", dproj, p["w_in"], "nt", 512, 512, stack=NCHIP, after=tok)
    grad_x, g["g_mix"] = _rms_bwd("rms1_bwd", x, p["g_mix"], dh1, dx1, False)
    return loss_blk, grad_x, g


ANY = pl.BlockSpec(memory_space=pl.ANY)


def _mesh_pos():
    return lax.axis_index("x"), lax.axis_index("y"), lax.axis_index("c")


def _slot(px, py, perm):
    return 2 * py + px if perm else 2 * px + py


def _other_chips(x, y):
    return [(1 - x, y), (x, 1 - y), (1 - x, 1 - y)]


def _rcopy(src, dst, send, recv, k, to, kr=None):
    return pltpu.make_async_remote_copy(src_ref=src, dst_ref=dst, send_sem=send.at[k],
                                        recv_sem=recv.at[k if kr is None else kr], device_id=to, device_id_type=MESH)


def _cast_bf16(name, w, after=()):
    r, c = w.shape
    tr = 128

    def body(w_ref, *rest):
        rest[-1][...] = w_ref[...].astype(BF16)

    return pl.pallas_call(
        body, name=name, grid=(r // tr,), in_specs=[_bs((tr, c), lambda i: (i, 0))] + [ANY] * len(after),
        out_specs=_bs((tr, c), lambda i: (i, 0)), out_shape=jax.ShapeDtypeStruct((r, c), BF16),
        compiler_params=_cp("parallel"),
    )(w, *after)


def _sibling_fill(lands, perms):
    na = len(lands)

    def body(*refs):
        outs, (send, recv) = refs[na:2 * na], refs[2 * na:]
        x, y, c = _mesh_pos()
        cps = []
        for a in range(na):
            for j, (px, py) in enumerate(_other_chips(x, y)):
                mine = outs[a].at[_slot(px, py, perms[a]), c]
                cps.append(_rcopy(mine, mine, send, recv, 3 * a + j, (x, y, 1 - c)))
        for cp in cps:
            cp.start()
        for a in range(na):
            for j, (px, py) in enumerate(_other_chips(x, y)):
                got = outs[a].at[_slot(px, py, perms[a]), 1 - c]
                _rcopy(got, got, send, recv, 3 * a + j, (x, y, c)).wait_recv()
        for cp in cps:
            cp.wait_send()

    return pl.pallas_call(
        body, name="gather_fill", in_specs=[ANY] * na, out_specs=[ANY] * na,
        out_shape=[jax.ShapeDtypeStruct(a.shape, a.dtype) for a in lands],
        input_output_aliases={i: i for i in range(na)},
        scratch_shapes=[pltpu.SemaphoreType.DMA((3 * na,)), pltpu.SemaphoreType.DMA((3 * na,))],
    )(*lands)


HBM = pl.BlockSpec(memory_space=pltpu.HBM)
SEM = pl.BlockSpec(memory_space=pltpu.SEMAPHORE)
EFFECT = pltpu.SideEffectType.DATAFLOW_SIDE_EFFECTING


def _split_start(name, srcs, lands, plan, nsem):
    ns, nl = len(srcs), len(lands)

    def body(*refs):
        send, recv = refs[ns + nl], refs[ns + nl + 1]
        sends, _ = plan(refs[:ns], refs[ns:ns + nl], send, recv)
        for cp in sends:
            cp.start()
        refs[-1][...] = jnp.zeros((8, 128), F32)

    arrs = list(srcs) + list(lands)
    out = pl.pallas_call(
        body, name=name, in_specs=[HBM] * (ns + nl),
        out_specs=[SEM, SEM] + [HBM] * (ns + nl) + [pl.BlockSpec(memory_space=pltpu.VMEM)],
        out_shape=[pltpu.SemaphoreType.DMA((nsem,)), pltpu.SemaphoreType.DMA((nsem,))]
        + [pltpu.HBM(a.shape, a.dtype) for a in arrs] + [jax.ShapeDtypeStruct((8, 128), F32)],
        input_output_aliases={i: 2 + i for i in range(ns + nl)},
        compiler_params=pltpu.CompilerParams(has_side_effects=EFFECT),
    )(*[pltpu.with_memory_space_constraint(a, pltpu.HBM) for a in arrs])
    return out[0], out[1], out[2:2 + ns], out[2 + ns:2 + ns + nl], out[-1]


def _split_wait(name, send, recv, srcs, lands, plan, after):
    ns, nl = len(srcs), len(lands)

    def body(*refs):
        sends, recvs = plan(refs[:ns], refs[ns:ns + nl], refs[ns + nl], refs[ns + nl + 1])
        for cp in sends:
            cp.wait_send()
        for cp in recvs:
            cp.wait_recv()

    arrs = list(srcs) + list(lands)
    after = tuple(after) if isinstance(after, (tuple, list)) else (after,)
    out = pl.pallas_call(
        body, name=name, in_specs=[HBM] * (ns + nl) + [SEM, SEM] + [ANY] * len(after), out_specs=[HBM] * (ns + nl),
        out_shape=[pltpu.HBM(a.shape, a.dtype) for a in arrs],
        input_output_aliases={i: i for i in range(ns + nl)},
        compiler_params=pltpu.CompilerParams(has_side_effects=EFFECT),
    )(*arrs, send, recv, *after)
    return out[ns:]


def _gather_plan(perms):
    def plan(srcs, lands, send, recv):
        x, y, c = _mesh_pos()
        sends, recvs = [], []
        for a, perm in enumerate(perms):
            for j, (px, py) in enumerate(_other_chips(x, y)):
                for cc in (0, 1):
                    k = 6 * a + 2 * j + cc
                    sends.append(_rcopy(srcs[a].at[c], lands[a].at[_slot(x, y, perm), c], send, recv, k, (px, py, cc),
                                        kr=6 * a + 2 * j + c))
                    got = lands[a].at[_slot(px, py, perm), cc]
                    recvs.append(_rcopy(got, got, send, recv, k, (x, y, c)))
        return sends, recvs
    return plan


def _gather_half_plan(perms, halved):
    def plan(srcs, lands, send, recv):
        x, y, c = _mesh_pos()
        sends, recvs = [], []
        for a, perm in enumerate(perms):
            for j, (px, py) in enumerate(_other_chips(x, y)):
                k = 3 * a + j
                mine, theirs = _slot(x, y, perm), _slot(px, py, perm)
                if halved[a]:
                    sends.append(_rcopy(srcs[a].at[c], lands[a].at[mine, c], send, recv, k, (px, py, c)))
                    got = lands[a].at[theirs, c]
                else:
                    sends.append(_rcopy(srcs[a], lands[a].at[mine], send, recv, k, (px, py, c)))
                    got = lands[a].at[theirs]
                recvs.append(_rcopy(got, got, send, recv, k, (x, y, c)))
        return sends, recvs
    return plan


def _reduce_plan(perm):
    def plan(srcs, lands, send, recv):
        x, y, c = _mesh_pos()
        src, land = srcs[0], lands[0]
        sends = []
        for j, (px, py) in enumerate(_other_chips(x, y)):
            for hf in (0, 1):
                sends.append(_rcopy(src.at[_slot(px, py, perm), hf], land.at[2 * j + c], send, recv, 2 * j + hf,
                                    (px, py, hf), kr=2 * j + c))
        sends.append(_rcopy(src.at[_slot(x, y, perm), 1 - c], land.at[6], send, recv, 6, (x, y, 1 - c)))
        recvs = [_rcopy(land.at[i], land.at[i], send, recv, i, (x, y, c)) for i in range(7)]
        return sends, recvs
    return plan


def _sibling_share(rs):
    na = len(rs)

    def body(*refs):
        ins, outs, (send, recv) = refs[:na], refs[na:2 * na], refs[2 * na:]
        x, y, c = _mesh_pos()
        cps = [_rcopy(ins[a], outs[a], send, recv, a, (x, y, 1 - c)) for a in range(na)]
        for cp in cps:
            cp.start()
        for cp in cps:
            cp.wait()

    return pl.pallas_call(
        body, name="rs_share", in_specs=[ANY] * na, out_specs=[ANY] * na,
        out_shape=[jax.ShapeDtypeStruct(r.shape, F32) for r in rs],
        scratch_shapes=[pltpu.SemaphoreType.DMA((na,)), pltpu.SemaphoreType.DMA((na,))],
    )(*rs)


def _add_pieces(name, g, got, where):
    _, _, r2, cc = g.shape
    tr = 128

    def body(w_ref, g_ref, r_ref, o_ref):
        del w_ref
        acc = g_ref[...]
        for i in range(7):
            acc = acc + r_ref[i].astype(F32)
        o_ref[...] = acc

    return pl.pallas_call(
        body, name=name,
        grid_spec=pltpu.PrefetchScalarGridSpec(
            num_scalar_prefetch=1, grid=(r2 // tr,),
            in_specs=[_bs((None, None, tr, cc), lambda i, w_ref: (w_ref[0], w_ref[1], i, 0)),
                      _bs((7, tr, cc), lambda i, w_ref: (0, i, 0))],
            out_specs=_bs((tr, cc), lambda i, w_ref: (i, 0))),
        out_shape=jax.ShapeDtypeStruct((r2, cc), F32), compiler_params=_cp("parallel"),
    )(where, g, got)


def _adam_math(w, g, m, v):
    m = ADAM_B1 * m + (1.0 - ADAM_B1) * g
    v = ADAM_B2 * v + (1.0 - ADAM_B2) * (g * g)
    m_hat = m / (1.0 - ADAM_B1 ** ADAM_STEP)
    v_hat = v / (1.0 - ADAM_B2 ** ADAM_STEP)
    return -ADAM_LR * (m_hat / (jnp.sqrt(v_hat) + ADAM_EPS) + ADAM_WD * w), m, v


def _adam_big(name, w, g_mine, g_sib, m, v, c_arr):
    r, cols = w.shape
    tr = 128
    per = r // 2 // tr

    def body(c_ref, w_ref, a_ref, b_ref, m_ref, v_ref, g_ref, d_ref, m2_ref, v2_ref):
        g = jnp.where(pl.program_id(0) == c_ref[0], a_ref[...], b_ref[...])
        g_ref[...] = g
        d_ref[...], m2_ref[...], v2_ref[...] = _adam_math(w_ref[...], g, m_ref[...], v_ref[...])

    spec = _bs((tr, cols), lambda h, i, c_ref: (h * per + i, 0))
    half = _bs((tr, cols), lambda h, i, c_ref: (i, 0))
    out = jax.ShapeDtypeStruct((r, cols), F32)
    return pl.pallas_call(
        body, name=name,
        grid_spec=pltpu.PrefetchScalarGridSpec(
            num_scalar_prefetch=1, grid=(2, per), in_specs=[spec, half, half, spec, spec], out_specs=[spec] * 4),
        out_shape=[out] * 4, compiler_params=_cp("parallel", "parallel"),
    )(c_arr, w, g_mine, g_sib, m, v)


_CLASS_SHAPE = {"a": (8, D), "b": (8, RW), "c": (8, 2 * DFF), "d": (1048, HD)}
_SMALL = (
    ("g_mix", "a", 0, 1, D), ("g_ffn", "a", 1, 1, D),
    ("rec_conv_w", "b", 0, 4, RW), ("rec_conv_b", "b", 4, 1, RW), ("lru_lambda", "b", 5, 1, RW),
    ("g_attn_out", "b", 6, 1, RW), ("g_rec_out", "b", 7, 1, RW),
    ("ffn_conv_w", "c", 0, 3, 2 * DFF), ("ffn_conv_b", "c", 3, 1, 2 * DFF),
    ("w_rg", "d", 0, RW, HD), ("w_ig", "d", RW, RW, HD), ("b_rg", "d", 2 * RW, 8, HD), ("b_ig", "d", 2 * RW + 8, 8, HD),
    ("q_norm_g", "d", 2 * RW + 16, 1, HD), ("k_norm_g", "d", 2 * RW + 17, 1, HD),
)
_LOSS_ROW = 2
_CLASSES = ("a", "b", "c", "d")
_CLASS_OWNER = {"a": 0, "b": 0, "c": 0, "d": 1}


def _small_allreduce(g, loss_blk):
    names = [s[0] for s in _SMALL]
    nin = len(names) + 1

    def body(*refs):
        ins = dict(zip(names, refs[:len(names)]))
        loss_ref = refs[len(names)]
        outs = dict(zip(_CLASSES, refs[nin:nin + 4]))
        pair = dict(zip(_CLASSES, refs[nin + 4:nin + 8]))
        quad = dict(zip(_CLASSES, refs[nin + 8:nin + 12]))
        send, recv = refs[nin + 12:]
        x, y, c = _mesh_pos()
        chip = 2 * x + y
        pair["a"][c] = jnp.zeros(_CLASS_SHAPE["a"], F32)
        pair["b"][c] = ins["rec_conv_w"][...]
        pair["c"][c] = ins["ffn_conv_w"][...]
        pair["d"][c, 2 * RW + 16:, :] = jnp.zeros((8, HD), F32)
        for name, k, r0, nr, _ in _SMALL:
            if name in ("rec_conv_w", "ffn_conv_w"):
                continue
            pair[k][c, r0:r0 + nr, :] = ins[name][...]
        pair["a"][c, _LOSS_ROW:_LOSS_ROW + 1, :] = jnp.broadcast_to(loss_ref[0:1, 0:1], (1, D))
        cps = [_rcopy(pair[k].at[c], pair[k].at[c], send, recv, ki, (x, y, 1 - c)) for ki, k in enumerate(_CLASSES)]
        for cp in cps:
            cp.start()
        for ki, k in enumerate(_CLASSES):
            _rcopy(pair[k].at[1 - c], pair[k].at[1 - c], send, recv, ki, (x, y, c)).wait_recv()
            quad[k][chip] = pair[k][0] + pair[k][1]
        for cp in cps:
            cp.wait_send()
        for ki, k in enumerate(_CLASSES):
            owner = _CLASS_OWNER[k]

            @pl.when(c == owner)
            def _(ki=ki, k=k):
                cps2 = [_rcopy(quad[k].at[chip], quad[k].at[chip], send, recv, 4 + 3 * ki + j, (px, py, c))
                        for j, (px, py) in enumerate(_other_chips(x, y))]
                for cp in cps2:
                    cp.start()
                for j, (px, py) in enumerate(_other_chips(x, y)):
                    got = quad[k].at[2 * px + py]
                    _rcopy(got, got, send, recv, 4 + 3 * ki + j, (x, y, c)).wait_recv()
                outs[k][...] = ((quad[k][0] + quad[k][1]) + quad[k][2]) + quad[k][3]
                share = _rcopy(outs[k], outs[k], send, recv, 16 + ki, (x, y, 1 - c))
                share.start()
                for cp in cps2:
                    cp.wait_send()
                share.wait_send()

        for ki, k in enumerate(_CLASSES):
            @pl.when(c != _CLASS_OWNER[k])
            def _(ki=ki, k=k):
                _rcopy(outs[k], outs[k], send, recv, 16 + ki, (x, y, c)).wait_recv()

    vm = pl.BlockSpec(memory_space=pltpu.VMEM)
    return pl.pallas_call(
        body, name="small_allreduce", in_specs=[vm] * nin, out_specs=[vm] * 4,
        out_shape=[jax.ShapeDtypeStruct(_CLASS_SHAPE[k], F32) for k in _CLASSES],
        scratch_shapes=[pltpu.VMEM((2,) + _CLASS_SHAPE[k], F32) for k in _CLASSES]
        + [pltpu.VMEM((NCHIP,) + _CLASS_SHAPE[k], F32) for k in _CLASSES]
        + [pltpu.SemaphoreType.DMA((20,)), pltpu.SemaphoreType.DMA((20,))],
        compiler_params=pltpu.CompilerParams(vmem_limit_bytes=VMEM_LIMIT),
    )(*[g[n] for n in names], loss_blk)


def _adam_small(red, w, m, v):
    names = [s[0] for s in _SMALL]
    n = len(names)

    def body(*refs):
        red_refs = dict(zip(_CLASSES, refs[:4]))
        w_refs, m_refs, v_refs = refs[4:4 + n], refs[4 + n:4 + 2 * n], refs[4 + 2 * n:4 + 3 * n]
        loss_ref = refs[4 + 3 * n]
        out_refs = refs[5 + 3 * n:]
        x, y, _ = _mesh_pos()
        chip = 2 * x + y
        loss_ref[...] = jnp.broadcast_to(red_refs["a"][_LOSS_ROW:_LOSS_ROW + 1, 0:1], loss_ref.shape)
        for pi, (name, k, r0, nr, width) in enumerate(_SMALL):
            gfull = red_refs[k][r0:r0 + nr, :]
            if name == "rec_conv_w":
                parts = [gfull[:, 128 * s:128 * (s + 1)] for s in range(NCHIP)]
                g = jnp.where(chip == 0, parts[0], jnp.where(chip == 1, parts[1], jnp.where(chip == 2, parts[2], parts[3])))
            elif name == "ffn_conv_w":
                parts = [gfull[:, FC * s:FC * (s + 1)] for s in range(NCHIP)]
                g = jnp.where(chip == 0, parts[0], jnp.where(chip == 1, parts[2], jnp.where(chip == 2, parts[1], parts[3])))
            elif name == "ffn_conv_b":
                g = jnp.concatenate([gfull[:, FC * s:FC * (s + 1)] for s in (0, 2, 1, 3)], axis=1)
            else:
                g = gfull
            d, m2, v2 = _adam_math(w_refs[pi][...], g, m_refs[pi][...], v_refs[pi][...])
            o = out_refs[4 * pi:4 * pi + 4]
            o[0][...], o[1][...], o[2][...], o[3][...] = g, d, m2, v2

    vm = pl.BlockSpec(memory_space=pltpu.VMEM)
    outs = [jax.ShapeDtypeStruct((1, 128), F32)]
    for name in names:
        outs += [jax.ShapeDtypeStruct(w[name].shape, F32)] * 4
    res = pl.pallas_call(
        body, name="adam_small", in_specs=[vm] * (4 + 3 * n), out_specs=[vm] * len(outs), out_shape=outs,
        compiler_params=pltpu.CompilerParams(vmem_limit_bytes=VMEM_LIMIT),
    )(*red, *[w[k] for k in names], *[m[k] for k in names], *[v[k] for k in names])
    return res[0], {name: res[1 + 4 * i:5 + 4 * i] for i, name in enumerate(names)}


_WEIGHTS = ("g_mix", "w_in", "q_norm_g", "k_norm_g", "rec_conv_w", "rec_conv_b", "w_rg", "b_rg", "w_ig", "b_ig",
            "lru_lambda", "g_attn_out", "g_rec_out", "w_out", "g_ffn", "w_up", "ffn_conv_w", "ffn_conv_b", "w_down")
_BIG = ("w_in", "w_out", "w_up", "w_down")
_BIG_PERM = {"w_in": False, "w_out": False, "w_up": True, "w_down": False}
_SMALL_2D = {"w_rg": (RW, HD), "w_ig": (RW, HD), "b_rg": (8, HD), "b_ig": (8, HD), "rec_conv_w": (4, 128),
             "ffn_conv_w": (3, FC)}


def _halves(a):
    r, c = a.shape
    return a.reshape(2, r // 2, c)


def kernel(x, positions, g_mix, w_in, q_norm_g, k_norm_g, rec_conv_w, rec_conv_b, w_rg, b_rg, w_ig, b_ig, lru_lambda, g_attn_out, g_rec_out, w_out, g_ffn, w_up, ffn_conv_w, ffn_conv_b, w_down, loss_target, m_g_mix, m_w_in, m_q_norm_g, m_k_norm_g, m_rec_conv_w, m_rec_conv_b, m_w_rg, m_b_rg, m_w_ig, m_b_ig, m_lru_lambda, m_g_attn_out, m_g_rec_out, m_w_out, m_g_ffn, m_w_up, m_ffn_conv_w, m_ffn_conv_b, m_w_down, v_g_mix, v_w_in, v_q_norm_g, v_k_norm_g, v_rec_conv_w, v_rec_conv_b, v_w_rg, v_b_rg, v_w_ig, v_b_ig, v_lru_lambda, v_g_attn_out, v_g_rec_out, v_w_out, v_g_ffn, v_w_up, v_ffn_conv_w, v_ffn_conv_b, v_w_down):
    given = dict(g_mix=g_mix, w_in=w_in, q_norm_g=q_norm_g, k_norm_g=k_norm_g, rec_conv_w=rec_conv_w, rec_conv_b=rec_conv_b, w_rg=w_rg, b_rg=b_rg, w_ig=w_ig, b_ig=b_ig, lru_lambda=lru_lambda, g_attn_out=g_attn_out, g_rec_out=g_rec_out, w_out=w_out, g_ffn=g_ffn, w_up=w_up, ffn_conv_w=ffn_conv_w, ffn_conv_b=ffn_conv_b, w_down=w_down)
    given_m = dict(g_mix=m_g_mix, w_in=m_w_in, q_norm_g=m_q_norm_g, k_norm_g=m_k_norm_g, rec_conv_w=m_rec_conv_w, rec_conv_b=m_rec_conv_b, w_rg=m_w_rg, b_rg=m_b_rg, w_ig=m_w_ig, b_ig=m_b_ig, lru_lambda=m_lru_lambda, g_attn_out=m_g_attn_out, g_rec_out=m_g_rec_out, w_out=m_w_out, g_ffn=m_g_ffn, w_up=m_w_up, ffn_conv_w=m_ffn_conv_w, ffn_conv_b=m_ffn_conv_b, w_down=m_w_down)
    given_v = dict(g_mix=v_g_mix, w_in=v_w_in, q_norm_g=v_q_norm_g, k_norm_g=v_k_norm_g, rec_conv_w=v_rec_conv_w, rec_conv_b=v_rec_conv_b, w_rg=v_w_rg, b_rg=v_b_rg, w_ig=v_w_ig, b_ig=v_b_ig, lru_lambda=v_lru_lambda, g_attn_out=v_g_attn_out, g_rec_out=v_g_rec_out, w_out=v_w_out, g_ffn=v_g_ffn, w_up=v_w_up, ffn_conv_w=v_ffn_conv_w, ffn_conv_b=v_ffn_conv_b, w_down=v_w_down)
    shapes = {n: a.shape for n, a in given.items()}

    def two_d(n, a):
        a = a[0]
        return a.reshape(_SMALL_2D[n]) if n in _SMALL_2D else (a if a.ndim == 2 else a[None])

    w = {n: two_d(n, a) for n, a in given.items()}
    m = {n: two_d(n, a) for n, a in given_m.items()}
    v = {n: two_d(n, a) for n, a in given_v.items()}
    cc = lax.axis_index("c").astype(jnp.int32)
    cx, cy = lax.axis_index("x").astype(jnp.int32), lax.axis_index("y").astype(jnp.int32)
    slot = {False: 2 * cx + cy, True: 2 * cy + cx}

    shards = {"w_in": _halves(_cast_bf16("cast_w_in", w["w_in"]))}
    first = [shards["w_in"], jnp.pad(w["ffn_conv_w"], ((0, 5), (0, 0))), jnp.pad(w["rec_conv_w"], ((0, 4), (0, 0)))]
    first_perm = [False, True, False]
    first_plan = _gather_half_plan(first_perm, [True, False, False])
    in_flight = _split_start(
        "gather_in_start", first,
        [lax.dynamic_update_slice(lax.empty((NCHIP,) + a.shape, a.dtype), a[None], (slot[pm],) + (0,) * a.ndim)
         for a, pm in zip(first, first_perm)], first_plan, 3 * len(first))
    for n in ("w_out", "w_up", "w_down"):
        shards[n] = _halves(_cast_bf16(f"cast_{n}", w[n], after=(in_flight[4],)))
    p = {n: w[n] for n in ("g_mix", "g_ffn", "q_norm_g", "k_norm_g", "rec_conv_b", "lru_lambda", "g_attn_out", "g_rec_out")}
    p.update(w_rg=w["w_rg"].reshape(8, HD, HD), w_ig=w["w_ig"].reshape(8, HD, HD), b_rg=w["b_rg"], b_ig=w["b_ig"],
             ffn_conv_b=jnp.concatenate([w["ffn_conv_b"][:, FC * s:FC * (s + 1)] for s in (0, 2, 1, 3)], axis=1))

    class Exchange:
        rest = ("w_out", "w_up", "w_down")
        order = []
        flight = {}

        def wait_first(self, after):
            send, recv, srcs, lands, _ = in_flight
            f_in, f_fcw, f_rcw = _split_wait("gather_in_wait", send, recv, srcs, lands, first_plan,
                                             (after,) + tuple(shards[n] for n in self.rest))
            (f_in,) = _sibling_fill([f_in], [False])
            return dict(w_in=f_in.reshape(NCHIP, D, INW // NCHIP), ffn_conv_w=f_fcw,
                        rec_conv_w=f_rcw.transpose(1, 0, 2).reshape(8, RW))

        def start_rest(self):
            srcs = [shards[n] for n in self.rest]
            lands = [lax.dynamic_update_slice(lax.empty((NCHIP,) + s.shape, BF16), s[None], (slot[_BIG_PERM[n]], 0, 0, 0))
                     for n, s in zip(self.rest, srcs)]
            plan = _gather_plan([_BIG_PERM[n] for n in self.rest])
            send, recv, srcs, lands, token = _split_start("gather_rest_start", srcs, lands, plan, 6 * len(srcs))
            self.flight["rest"] = (send, recv, srcs, lands, plan)
            return (token,)

        def wait_rest(self, after):
            send, recv, srcs, lands, plan = self.flight.pop("rest")
            f_out, f_up, f_down = _split_wait("gather_rest_wait", send, recv, srcs, lands, plan, after)
            return dict(w_out=f_out.reshape(D, D), w_up=f_up.reshape(NCHIP, D, FC), w_down=f_down.reshape(DFF, D))

        def reduce_start(self, name, g32, g16):
            r2, cols = shards[name].shape[1:]
            plan = _reduce_plan(_BIG_PERM[name])
            send, recv, srcs, lands, token = _split_start(
                f"reduce_{name}_start", [g16.reshape(NCHIP, 2, r2, cols)], [lax.empty((7, r2, cols), BF16)], plan, 7)
            self.flight[name] = (send, recv, srcs, lands, plan, g32.reshape(NCHIP, 2, r2, cols))
            self.order.append(name)
            return (token,)

        def finish(self, after):
            mine = {}
            for name in self.order:
                send, recv, srcs, lands, plan, g32 = self.flight.pop(name)
                (got,) = _split_wait(f"reduce_{name}_wait", send, recv, srcs, lands, plan, after)
                where = jnp.stack([slot[_BIG_PERM[name]], cc])
                mine[name] = after = _add_pieces(f"reduce_{name}_add", g32, got, where)
            theirs = dict(zip(_BIG, _sibling_share([mine[n] for n in _BIG])))
            return mine, theirs

    exch = Exchange()

    loss_blk, grad_x, g = _local_step(x[0], positions.reshape(T, 1), loss_target[0], p, exch)

    out_g, out_d, out_m, out_v = {}, {}, {}, {}
    red = _small_allreduce(g, loss_blk)
    loss_row, small_out = _adam_small(red, w, m, v)
    for n, (gn, dn, mn, vn) in small_out.items():
        out_g[n], out_d[n], out_m[n], out_v[n] = gn, dn, mn, vn

    mine, theirs = exch.finish(red[0])
    for n in _BIG:
        out_g[n], out_d[n], out_m[n], out_v[n] = _adam_big(f"adam_{n}", w[n], mine[n], theirs[n], m[n], v[n], cc.reshape(1))

    outs = [loss_row[0, 0], grad_x[None]]
    for group in (out_g, out_d, out_m, out_v):
        outs += [group[n].reshape(shapes[n]) for n in _WEIGHTS]
    return tuple(outs)
```

```python
import math

import jax
import jax.numpy as jnp
import numpy as np
from jax import lax
from jax.experimental import pallas as pl
from jax.experimental.pallas import tpu as pltpu

F32 = jnp.float32
BF16 = jnp.bfloat16

T = 4096
D = 1024
HD = 64
AW = 512
RW = 512
INW = 2560
DFF = 3072
NCHIP = 4
EPS = 1e-6
NEG = -1e30
LRU_C = 8.0
ROPE_THETA = 10000.0
BLK = 128
DILATIONS = (1, 4, 16)
ADAM_LR, ADAM_B1, ADAM_B2, ADAM_EPS, ADAM_WD, ADAM_STEP = 0.001, 0.9, 0.999, 1e-08, 0.01, 10
VMEM_LIMIT = 56 * 1024 * 1024
MESH = pl.DeviceIdType.MESH

NN = (((1,), (0,)), ((), ()))
NT = (((1,), (1,)), ((), ()))
TN = (((0,), (0,)), ((), ()))


def _cp(*sem):
    return pltpu.CompilerParams(dimension_semantics=sem, vmem_limit_bytes=VMEM_LIMIT)


def _bs(shape, fn):
    return pl.BlockSpec(shape, fn)


def _dot(a, b, dims=NN):
    return lax.dot_general(a, b, dims, preferred_element_type=F32)


_GC = math.sqrt(2.0 / math.pi)


def _gelu(x):
    return x * (0.5 + 0.5 * jnp.tanh(x * (_GC + (_GC * 0.044715) * (x * x))))


def _gelu_and_grad(x):
    x2 = x * x
    th = jnp.tanh(x * (_GC + (_GC * 0.044715) * x2))
    cdf = 0.5 + 0.5 * th
    dg = cdf + (x * (1.0 - th * th)) * ((0.5 * _GC) + (1.5 * 0.044715 * _GC) * x2)
    return x * cdf, dg


def _softplus(x):
    e = jnp.exp(-jnp.abs(x))
    u = 1.0 + e
    l1p = jnp.where(u == 1.0, e, jnp.log(u) * (e / (u - 1.0)))
    return jnp.maximum(x, 0.0) + l1p


def _segsum(z, e_bf16):
    hi = z.astype(BF16)
    lo = (z - hi.astype(F32)).astype(BF16)
    parts = []
    for c0 in range(0, z.shape[1], 128):
        parts.append(_dot(hi[:, c0:c0 + 128], e_bf16) + _dot(lo[:, c0:c0 + 128], e_bf16))
    return jnp.concatenate(parts, axis=1)


def _mm(name, a, b, mode, tm, tn, out_dtype=F32, res=None, stack=0, twin_bf16=False, after=(), a_full=False,
        b_full=False):
    if mode == "nn":
        (m, k), n = a.shape, (b.shape[1] if not stack else stack * b.shape[2])
        a_spec = _bs((tm, k), lambda j, i: (i, 0))
        if stack:
            per = b.shape[2] // tn
            b_spec = _bs((None, k, tn), lambda j, i: (j // per, 0, j % per))
        else:
            b_spec = _bs((k, tn), lambda j, i: (0, j))
    elif mode == "nt":
        (m, k), n = a.shape, (b.shape[0] if not stack else b.shape[1])
        a_spec = _bs((tm, k), lambda j, i: (i, 0))
        b_spec = _bs((stack, tn, k // stack), lambda j, i: (0, j, 0)) if stack else _bs((tn, k), lambda j, i: (j, 0))
    else:
        (k, m), n = a.shape, b.shape[1]
        a_spec, b_spec = _bs((k, tm), lambda j, i: (0, i)), _bs((k, tn), lambda j, i: (0, j))
    assert m % tm == 0 and n % tn == 0
    o_spec = _bs((tm, tn), lambda j, i: (i, j))
    o_shape = (m, n)
    if mode == "tn" and stack:
        per = n // stack // tn
        o_spec = _bs((None, tm, tn), lambda j, i: (j // per, i, j % per))
        o_shape = (stack, m, n // stack)
    dims = {"nn": NN, "nt": NT, "tn": TN}[mode]
    once = pl.Buffered(1)
    if a_full:
        a_spec = pl.BlockSpec(a.shape, lambda j, i: (0, 0), pipeline_mode=once)
    if b_full:
        assert n == tn
        b_spec = pl.BlockSpec(b_spec.block_shape, b_spec.index_map, pipeline_mode=once)

    def product(a_ref, b_ref):
        if a_full:
            mine = pl.ds(pl.multiple_of(pl.program_id(1) * tm, tm), tm)
            take = (lambda cols: a_ref[:, mine]) if mode == "tn" else (lambda cols: a_ref[mine, cols])
        else:
            take = lambda cols: a_ref[:, cols]
        if mode == "nt" and stack:
            cs = k // stack
            acc = _dot(take(pl.ds(0, cs)), b_ref[0], NT)
            for s in range(1, stack):
                acc = acc + _dot(take(pl.ds(s * cs, cs)), b_ref[s], NT)
            return acc
        return _dot(take(slice(None)), b_ref[...], dims)

    nres = 0 if res is None else 1

    def body(a_ref, b_ref, *rest):
        acc = product(a_ref, b_ref)
        if nres:
            acc = rest[0][...] + acc
        outs = rest[nres + len(after):]
        outs[0][...] = acc.astype(out_dtype)
        if twin_bf16:
            outs[1][...] = acc.astype(BF16)

    ins = (a, b) + ((res,) if nres else ()) + tuple(after)
    specs = [a_spec, b_spec] + ([o_spec] if nres else []) + [pl.BlockSpec(memory_space=pl.ANY)] * len(after)
    shapes = [jax.ShapeDtypeStruct(o_shape, out_dtype)] + ([jax.ShapeDtypeStruct(o_shape, BF16)] if twin_bf16 else [])
    out = pl.pallas_call(
        body, name=name, grid=(n // tn, m // tm), in_specs=specs, out_specs=[o_spec] * len(shapes),
        out_shape=shapes, compiler_params=_cp("parallel", "parallel"),
    )(*ins)
    return tuple(out) if twin_bf16 else out[0]


def _rms_fwd(name, x, g):
    tr = 512

    def body(x_ref, g_ref, o_ref):
        xv = x_ref[...]
        r = lax.rsqrt(jnp.mean(xv * xv, axis=-1, keepdims=True) + EPS)
        o_ref[...] = ((xv * r) * g_ref[...]).astype(BF16)

    return pl.pallas_call(
        body, name=name, grid=(T // tr,), in_specs=[_bs((tr, D), lambda i: (i, 0)), _bs((1, D), lambda i: (0, 0))],
        out_specs=_bs((tr, D), lambda i: (i, 0)), out_shape=jax.ShapeDtypeStruct((T, D), BF16),
        compiler_params=_cp("parallel"),
    )(x, g)


def _rms_bwd(name, x, g, dy, dres, want_bf16, after=()):
    tr = 256
    halves = dy.ndim == 3

    def body(x_ref, g_ref, dy_ref, dr_ref, *rest):
        rest = rest[len(after):]
        dx_ref, rest = rest[0], rest[1:]
        dg_ref = rest[-1]
        xv = x_ref[...]
        dyv = dy_ref[0] + dy_ref[1] if halves else dy_ref[...]
        r = lax.rsqrt(jnp.mean(xv * xv, axis=-1, keepdims=True) + EPS)
        gdy = g_ref[...] * dyv
        dx = r * gdy - xv * ((r * r * r) * jnp.mean(xv * gdy, axis=-1, keepdims=True)) + dr_ref[...]
        dx_ref[...] = dx
        if want_bf16:
            rest[0][...] = dx.astype(BF16)

        @pl.when(pl.program_id(0) == 0)
        def _():
            dg_ref[...] = jnp.zeros_like(dg_ref)

        dg_ref[...] += jnp.sum(dyv * (xv * r), axis=0, keepdims=True)

    row = _bs((tr, D), lambda i: (i, 0))
    vec = _bs((1, D), lambda i: (0, 0))
    outs = [jax.ShapeDtypeStruct((T, D), F32)] + ([jax.ShapeDtypeStruct((T, D), BF16)] if want_bf16 else [])
    dy_spec = _bs((2, tr, D), lambda i: (0, i, 0)) if halves else row
    return pl.pallas_call(
        body, name=name, grid=(T // tr,),
        in_specs=[row, vec, dy_spec, row] + [pl.BlockSpec(memory_space=pl.ANY)] * len(after),
        out_specs=[row] * len(outs) + [vec], out_shape=outs + [jax.ShapeDtypeStruct((1, D), F32)],
        compiler_params=_cp("arbitrary"),
    )(x, g, dy, dres, *after)


def _head_ones():
    idx = np.arange(128) // HD
    return jnp.asarray((idx[:, None] == idx[None, :]).astype(np.float32), dtype=BF16)


def _freq_row():
    half = HD // 2
    inv = ROPE_THETA ** (-(np.arange(half, dtype=np.float64)) / half)
    return jnp.asarray(np.tile(inv, 4)[None, :], dtype=F32)


def _rot_tables(cos128, sin128):
    c = jnp.tile(cos128, (1, 4))
    s = jnp.tile(sin128, (1, 4))
    lane = lax.broadcasted_iota(jnp.int32, (1, AW), 1)
    first = (lane & 32) == 0
    return c, jnp.where(first, -s, s), first


def _swap_halves(y, first):
    return jnp.where(first, pltpu.roll(y, AW - 32, 1), pltpu.roll(y, 32, 1))


def _qk_prep(proj, pos_col, qg, kg):
    tr = 512

    def body(q_ref, k_ref, pos_ref, f_ref, qg_ref, kg_ref, e_ref, qo_ref, ko_ref, cos_ref, sin_ref):
        ang = pos_ref[...].astype(F32) * f_ref[...]
        cos_ref[...] = jnp.cos(ang)
        sin_ref[...] = jnp.sin(ang)
        c, s_signed, first = _rot_tables(cos_ref[...], sin_ref[...])
        e = e_ref[...]

        def norm_rot(xv, g, scale):
            r = lax.rsqrt(_segsum(xv * xv, e) * (1.0 / HD) + EPS)
            y = (xv * r) * g
            return (y * c + _swap_halves(y, first) * s_signed) * scale

        qo_ref[...] = norm_rot(q_ref[...], qg_ref[...], HD ** -0.5)
        ko_ref[...] = norm_rot(k_ref[...], kg_ref[...], 1.0)

    col = lambda j: _bs((tr, AW), lambda i, j=j: (i, j))
    vec = _bs((1, AW), lambda i: (0, 0))
    out = jax.ShapeDtypeStruct((T, AW), F32)
    tab = jax.ShapeDtypeStruct((T, 128), F32)
    tspec = _bs((tr, 128), lambda i: (i, 0))
    return pl.pallas_call(
        body, name="qk_prep", grid=(T // tr,),
        in_specs=[col(0), col(1), _bs((tr, 1), lambda i: (i, 0)), _bs((1, 128), lambda i: (0, 0)), vec, vec,
                  _bs((128, 128), lambda i: (0, 0))],
        out_specs=[col(0)] * 2 + [tspec] * 2, out_shape=[out, out, tab, tab], compiler_params=_cp("parallel"),
    )(proj, proj, pos_col, _freq_row(), qg, kg, _head_ones())


def _qk_bwd(proj, cos_t, sin_t, qg, kg, dq, dk, dv):
    tr = 512

    def body(q_ref, k_ref, cos_ref, sin_ref, qg_ref, kg_ref, e_ref, dq_ref, dk_ref, dv_ref, o_ref, dqg_ref, dkg_ref):
        i, j = pl.program_id(0), pl.program_id(1)

        @pl.when((i == 0) & (j == 0))
        def _():
            dqg_ref[...] = jnp.zeros_like(dqg_ref)
            dkg_ref[...] = jnp.zeros_like(dkg_ref)

        def norm_rot_bwd(x_ref, g_ref, dg_ref, d_ref, scale):
            c, s_signed, first = _rot_tables(cos_ref[...], sin_ref[...])
            e = e_ref[...]
            dout = d_ref[...] * scale
            dy = dout * c + _swap_halves(dout * s_signed, first)
            xv, g = x_ref[...], g_ref[...]
            r = lax.rsqrt(_segsum(xv * xv, e) * (1.0 / HD) + EPS)
            gdy = g * dy
            dx = r * gdy - xv * ((r * r * r) * (_segsum(xv * gdy, e) * (1.0 / HD)))
            o_ref[...] = dx.astype(BF16)
            dg_ref[...] += jnp.sum(dy * (xv * r), axis=0, keepdims=True)

        @pl.when(j == 0)
        def _():
            o_ref[...] = dv_ref[...].astype(BF16)

        @pl.when(j == 1)
        def _():
            norm_rot_bwd(q_ref, qg_ref, dqg_ref, dq_ref, HD ** -0.5)

        @pl.when(j == 2)
        def _():
            norm_rot_bwd(k_ref, kg_ref, dkg_ref, dk_ref, 1.0)

    col = lambda jj: _bs((tr, AW), lambda i, j, jj=jj: (i, jj))
    vec = _bs((1, AW), lambda i, j: (0, 0))
    piece = _bs((tr, AW), lambda i, j: (i, 0))
    return pl.pallas_call(
        body, name="qk_bwd", grid=(T // tr, 3),
        in_specs=[col(0), col(1), _bs((tr, 128), lambda i, j: (i, 0)), _bs((tr, 128), lambda i, j: (i, 0)), vec, vec,
                  _bs((128, 128), lambda i, j: (0, 0))] + [piece] * 3,
        out_specs=[_bs((tr, AW), lambda i, j: (i, (j + 2) % 3)), vec, vec],
        out_shape=[jax.ShapeDtypeStruct((T, 3 * AW), BF16), jax.ShapeDtypeStruct((1, AW), F32),
                   jax.ShapeDtypeStruct((1, AW), F32)],
        compiler_params=_cp("arbitrary", "arbitrary"),
    )(proj, proj, cos_t, sin_t, qg, kg, _head_ones(), dq, dk, dv)


RG = 256
QC = 64


def _stacked_band_mask(rows=2 * BLK, q0=0):
    qi = (lax.broadcasted_iota(jnp.int32, (rows, 2 * BLK), 0) + q0) & (BLK - 1)
    kj = lax.broadcasted_iota(jnp.int32, (rows, 2 * BLK), 1)
    rel = qi - kj + BLK
    return (rel >= 0) & (rel <= BLK), lax.broadcasted_iota(jnp.int32, (1, 2 * BLK), 1) >= BLK


def _natural_rows(r0, n_rows, d):
    if d == 1:
        return pl.ds(r0, n_rows)
    ln = T // d
    return pl.ds(r0 // ln + d * (r0 % ln), n_rows, stride=d)


def _regroup_into(dst, src_ref, d, pad, cast=True):
    def step(j, carry):
        r0 = pl.multiple_of(j * RG, RG)
        val = src_ref[_natural_rows(r0, RG, d), :]
        dst[pl.ds(pad + r0, RG), :] = val.astype(dst.dtype) if cast else val
        return carry
    lax.fori_loop(0, T // RG, step, 0)


def _stack_heads(x, h0):
    zero = jnp.zeros_like(x)
    return jnp.concatenate([jnp.where(h0, x, zero), jnp.where(h0, zero, x)], axis=0)


def _attn_fwd(q, k, proj):
    nblk = T // BLK

    def body(q_ref, k_ref, v_ref, a_ref, lse_ref, qs, ks, vs, o0, o1, o2, l0, l1, l2, sb0, sb1):
        band, cur_half = _stacked_band_mask()
        h0 = lax.broadcasted_iota(jnp.int32, (1, 128), 1) < HD
        ks[0:BLK, :] = jnp.zeros((BLK, 128), BF16)
        vs[0:BLK, :] = jnp.zeros((BLK, 128), BF16)
        for d, o_s, l_s in zip(DILATIONS, (o0, o1, o2), (l0, l1, l2)):
            nb = T // d // BLK
            _regroup_into(qs, q_ref, d, 0)
            _regroup_into(ks, k_ref, d, BLK)
            _regroup_into(vs, v_ref, d, BLK)

            def scores(b):
                r0 = pl.multiple_of(b * BLK, BLK)
                return _dot(_stack_heads(qs[pl.ds(r0, BLK), :], h0), ks[pl.ds(r0, 2 * BLK), :], NT)

            def finish(b, s_raw, d=d, nb=nb, o_s=o_s, l_s=l_s):
                r0 = pl.multiple_of(b * BLK, BLK)
                mask = band & (cur_half | ((b & (nb - 1)) > 0))
                s = jnp.where(mask, s_raw, NEG)
                m = jnp.max(s, axis=1, keepdims=True)
                p = jnp.exp(s - m)
                l = jnp.sum(p, axis=1, keepdims=True)
                o = _dot(p.astype(BF16), vs[pl.ds(r0, 2 * BLK), :]) / l
                lse = m + jnp.log(l)
                rows = _natural_rows(r0, BLK, d)
                o_s[rows, :] = jnp.where(h0, o[0:BLK, :], o[BLK:, :])
                l_s[rows, :] = jnp.where(h0, lse[0:BLK, :], lse[BLK:, :])

            sb0[...] = scores(0)

            def step(i, carry):
                b = 2 * i
                sb1[...] = scores(b + 1)
                finish(b, sb0[...])
                sb0[...] = scores(jnp.minimum(b + 2, nblk - 1))
                finish(b + 1, sb1[...])
                return carry

            lax.fori_loop(0, nblk // 2, step, 0)

        def merge(i, carry):
            r = pl.ds(pl.multiple_of(i * RG, RG), RG)
            la, lb, lc = l0[r, :], l1[r, :], l2[r, :]
            m = jnp.maximum(jnp.maximum(la, lb), lc)
            ea, eb, ec = jnp.exp(la - m), jnp.exp(lb - m), jnp.exp(lc - m)
            z = (ea + eb) + ec
            a_ref[r, :] = ((ea * o0[r, :] + eb * o1[r, :]) + ec * o2[r, :]) / z
            lse_ref[r, :] = m + jnp.log(z)
            return carry

        lax.fori_loop(0, T // RG, merge, 0)

    spec = lambda cb: _bs((T, 128), lambda p, cb=cb: (0, cb + p))
    out = jax.ShapeDtypeStruct((T, AW), F32)
    return pl.pallas_call(
        body, name="attn_fwd", grid=(AW // 128,), in_specs=[spec(0), spec(0), spec(8)], out_specs=[spec(0)] * 2,
        out_shape=[out] * 2,
        scratch_shapes=[pltpu.VMEM((T, 128), BF16), pltpu.VMEM((T + BLK, 128), BF16), pltpu.VMEM((T + BLK, 128), BF16)]
        + [pltpu.VMEM((T, 128), F32)] * 6 + [pltpu.VMEM((2 * BLK, 2 * BLK), F32)] * 2,
        compiler_params=_cp("parallel"),
    )(q, k, proj)


def _attn_bwd(q, k, proj, do, lse, delta):
    nblk = T // BLK

    def body(q_ref, k_ref, v_ref, do_ref, l_ref, dl_ref, dq_ref, dk_ref, dv_ref, qs, dos, ks, vs, ls, dls, dks, dvs,
             sa0, sa1, da0, da1):
        band, cur_half = _stacked_band_mask()
        h0 = lax.broadcasted_iota(jnp.int32, (1, 128), 1) < HD
        ks[0:BLK, :] = jnp.zeros((BLK, 128), BF16)
        vs[0:BLK, :] = jnp.zeros((BLK, 128), BF16)
        for d in DILATIONS:
            nb = T // d // BLK
            _regroup_into(qs, q_ref, d, 0)
            _regroup_into(dos, do_ref, d, 0)
            _regroup_into(ks, k_ref, d, BLK)
            _regroup_into(vs, v_ref, d, BLK)
            _regroup_into(ls, l_ref, d, 0, cast=False)
            _regroup_into(dls, dl_ref, d, 0, cast=False)
            dks[...] = jnp.zeros_like(dks)
            dvs[...] = jnp.zeros_like(dvs)

            def scores(b, s_buf, dp_buf):
                r0 = pl.multiple_of(b * BLK, BLK)
                win = pl.ds(r0, 2 * BLK)
                s_buf[...] = _dot(_stack_heads(qs[pl.ds(r0, BLK), :], h0), ks[win, :], NT)
                dp_buf[...] = _dot(_stack_heads(dos[pl.ds(r0, BLK), :], h0), vs[win, :], NT)

            def finish(b, s_buf, dp_buf, d=d, nb=nb):
                r0 = pl.multiple_of(b * BLK, BLK)
                mask = band & (cur_half | ((b & (nb - 1)) > 0))
                win = pl.ds(r0, 2 * BLK)
                lv, dlv = ls[pl.ds(r0, BLK), :], dls[pl.ds(r0, BLK), :]
                lse2 = jnp.concatenate([lv[:, 0:1], lv[:, HD:HD + 1]], axis=0)
                dl2 = jnp.concatenate([dlv[:, 0:1], dlv[:, HD:HD + 1]], axis=0)
                p = jnp.exp(jnp.where(mask, s_buf[...], NEG) - lse2)
                ds = p * (dp_buf[...] - dl2)
                pb, dsb = p.astype(BF16), ds.astype(BF16)
                dq2 = _dot(dsb, ks[win, :])
                dks[win, :] += _dot(dsb, _stack_heads(qs[pl.ds(r0, BLK), :], h0), TN)
                dvs[win, :] += _dot(pb, _stack_heads(dos[pl.ds(r0, BLK), :], h0), TN)
                rows = _natural_rows(r0, BLK, d)
                dq = jnp.where(h0, dq2[0:BLK, :], dq2[BLK:, :])
                dq_ref[rows, :] = dq if d == 1 else dq_ref[rows, :] + dq

            scores(0, sa0, da0)

            def step(i, carry):
                b = 2 * i
                scores(b + 1, sa1, da1)
                finish(b, sa0, da0)
                scores(jnp.minimum(b + 2, nblk - 1), sa0, da0)
                finish(b + 1, sa1, da1)
                return carry

            lax.fori_loop(0, nblk // 2, step, 0)

            def back(j, carry, d=d):
                r0 = pl.multiple_of(j * RG, RG)
                rows = _natural_rows(r0, RG, d)
                src = pl.ds(BLK + r0, RG)
                dk_ref[rows, :] = dks[src, :] if d == 1 else dk_ref[rows, :] + dks[src, :]
                dv_ref[rows, :] = dvs[src, :] if d == 1 else dv_ref[rows, :] + dvs[src, :]
                return carry

            lax.fori_loop(0, T // RG, back, 0)

    spec = lambda cb: _bs((T, 128), lambda p, cb=cb: (0, cb + p))
    ospec = _bs((T, 128), lambda p: (0, p))
    out = jax.ShapeDtypeStruct((T, AW), F32)
    return pl.pallas_call(
        body, name="attn_bwd", grid=(AW // 128,), in_specs=[spec(0), spec(0), spec(8), spec(0), spec(0), spec(0)],
        out_specs=[ospec] * 3, out_shape=[out] * 3,
        scratch_shapes=[pltpu.VMEM((T, 128), BF16), pltpu.VMEM((T, 128), BF16), pltpu.VMEM((T + BLK, 128), BF16),
                        pltpu.VMEM((T + BLK, 128), BF16), pltpu.VMEM((T, 128), F32), pltpu.VMEM((T, 128), F32),
                        pltpu.VMEM((T + BLK, 128), F32), pltpu.VMEM((T + BLK, 128), F32)]
        + [pltpu.VMEM((2 * BLK, 2 * BLK), F32)] * 4,
        compiler_params=_cp("parallel"),
    )(q, k, proj, do, lse, delta)


def _attn_norm(attn, g_attn):
    tr = 512

    def body(a_ref, g_ref, mix_ref):
        attn = a_ref[...]
        r = lax.rsqrt(jnp.mean(attn * attn, axis=-1, keepdims=True) + EPS)
        mix_ref[...] = ((attn * r) * g_ref[...]).astype(BF16)

    row = _bs((tr, AW), lambda i: (i, 0))
    return pl.pallas_call(
        body, name="attn_norm", grid=(T // tr,), in_specs=[row, _bs((1, AW), lambda i: (0, 0))],
        out_specs=row, out_shape=jax.ShapeDtypeStruct((T, D), BF16), compiler_params=_cp("parallel"),
    )(attn, g_attn)


def _attn_out_bwd(attn, dmix, g_attn):
    tr = 512

    def body(a_ref, d_ref, g_ref, e_ref, do_ref, dl_ref, dg_ref):
        av, dyv = a_ref[...], d_ref[...]
        r = lax.rsqrt(jnp.mean(av * av, axis=-1, keepdims=True) + EPS)
        gdy = g_ref[...] * dyv
        da = r * gdy - av * ((r * r * r) * jnp.mean(av * gdy, axis=-1, keepdims=True))
        do_ref[...] = da
        dl_ref[...] = _segsum(da * av, e_ref[...])

        @pl.when(pl.program_id(0) == 0)
        def _():
            dg_ref[...] = jnp.zeros_like(dg_ref)

        dg_ref[...] += jnp.sum(dyv * (av * r), axis=0, keepdims=True)

    row = _bs((tr, AW), lambda i: (i, 0))
    vec = _bs((1, AW), lambda i: (0, 0))
    return pl.pallas_call(
        body, name="attn_out_bwd", grid=(T // tr,), in_specs=[row, row, vec, _bs((128, 128), lambda i: (0, 0))],
        out_specs=[row, row, vec],
        out_shape=[jax.ShapeDtypeStruct((T, AW), F32), jax.ShapeDtypeStruct((T, AW), F32),
                   jax.ShapeDtypeStruct((1, AW), F32)],
        compiler_params=_cp("arbitrary"),
    )(attn, dmix, g_attn, _head_ones())


TRR = 256


def _scan_fwd(a, u):
    n = a.shape[0]
    row = lax.broadcasted_iota(jnp.int32, (n, 1), 0)
    s = 1
    while s < n:
        keep = row >= s
        u = jnp.where(keep, a * pltpu.roll(u, s, 0) + u, u)
        a = jnp.where(keep, a * pltpu.roll(a, s, 0), a)
        s *= 2
    return a, u


def _scan_bwd(c, w):
    n = c.shape[0]
    row = lax.broadcasted_iota(jnp.int32, (n, 1), 0)
    s = 1
    while s < n:
        keep = row < n - s
        w = jnp.where(keep, c * pltpu.roll(w, n - s, 0) + w, w)
        c = jnp.where(keep, c * pltpu.roll(c, n - s, 0), c)
        s *= 2
    return w


def _gates(xc, wrg, wig, brg, big, sp):
    xcb = xc.astype(BF16)
    r = jax.nn.sigmoid(_dot(xcb, wrg) + brg)
    ig = jax.nn.sigmoid(_dot(xcb, wig) + big)
    la = (-LRU_C * r) * sp
    a = jnp.exp(la)
    mult = jnp.sqrt(-jnp.tanh(la) * (a * a + 1.0))
    return r, ig, a, mult


def _conv4(ext_ref, xr, cw_ref, cb_ref, n):
    y = cb_ref[...] + ext_ref[pl.ds(5, n), :] * cw_ref[0:1, :]
    y = y + ext_ref[pl.ds(6, n), :] * cw_ref[1:2, :]
    y = y + ext_ref[pl.ds(7, n), :] * cw_ref[2:3, :]
    return y + xr * cw_ref[3:4, :]


def _rec_fwd(proj, mix, cw, cb, wrg, wig, brg, big, lam, g_rec):
    n = TRR

    def body(xr_ref, gr_ref, cw_ref, cb_ref, wrg_ref, wig_ref, brg_ref, big_ref, lam_ref, g_ref, mix_in,
             mix_ref, h_ref, ext, hcar):
        del mix_in

        @pl.when(pl.program_id(0) == 0)
        def _():
            ext[0:8, :] = jnp.zeros((8, RW), F32)
            hcar[...] = jnp.zeros_like(hcar)

        xr = xr_ref[...]
        ext[8:, :] = xr
        xc = _conv4(ext, xr, cw_ref, cb_ref, n)
        ext[0:8, :] = xr[n - 8:, :]
        sp = _softplus(-lam_ref[...])
        _, ig, a, mult = _gates(xc, wrg_ref[...], wig_ref[...], brg_ref[...], big_ref[...], sp)
        a_s, u_s = _scan_fwd(a, mult * (ig * xc))
        h = u_s + a_s * hcar[7:8, :]
        h_ref[...] = h
        hcar[...] = h[n - 8:, :]
        pre = h * _gelu(gr_ref[...])
        r = lax.rsqrt(jnp.mean(pre * pre, axis=-1, keepdims=True) + EPS)
        mix_ref[...] = ((pre * r) * g_ref[...]).astype(BF16)

    vec = _bs((1, RW), lambda i: (0, 0))
    mat = _bs((RW, RW), lambda i: (0, 0))
    return pl.pallas_call(
        body, name="rec_fwd", grid=(T // n,),
        in_specs=[_bs((n, RW), lambda i: (i, 3)), _bs((n, RW), lambda i: (i, 4)), _bs((8, RW), lambda i: (0, 0)), vec,
                  mat, mat, vec, vec, vec, vec, pl.BlockSpec(memory_space=pl.ANY)],
        out_specs=[_bs((n, RW), lambda i: (i, 1)), _bs((n, RW), lambda i: (i, 0))],
        out_shape=[jax.ShapeDtypeStruct((T, D), BF16), jax.ShapeDtypeStruct((T, RW), F32)],
        scratch_shapes=[pltpu.VMEM((n + 8, RW), F32), pltpu.VMEM((8, RW), F32)],
        input_output_aliases={10: 0}, compiler_params=_cp("arbitrary"),
    )(proj, proj, cw, cb, wrg, wig, brg, big, lam, g_rec, mix)


def _rec_bwd(proj, h, dmix, cw, cb, wrg, wig, brg, big, lam, g_rec):
    n = TRR
    nt = T // n
    hb = n // 8

    def body(xr_ref, xh_ref, gr_ref, h_ref, hh_ref, dm_ref, cw_ref, cb_ref, wrg_ref, wig_ref, brg_ref, big_ref,
             lam_ref, g_ref, dp_ref, xc_ref, dr_ref, di_ref, dcw_ref, dcb_ref, dbr_ref, dbi_ref, dsp_ref,
             dg_ref, ext, exth, extd, adh):
        i, j = pl.program_id(0), pl.program_id(1)
        first_tile = i == nt - 1
        last_tile = i == 0

        @pl.when(j == 0)
        def _():
            @pl.when(last_tile)
            def _():
                for ref in (dcw_ref, dcb_ref, dbr_ref, dbi_ref, dsp_ref, dg_ref):
                    ref[...] = jnp.zeros_like(ref)
                extd[n:, :] = jnp.zeros((8, RW), F32)
                adh[...] = jnp.zeros_like(adh)

            row = lax.broadcasted_iota(jnp.int32, (n, 1), 0)
            xr = xr_ref[...]
            ext[0:8, :] = jnp.where(first_tile, 0.0, xh_ref[...])
            ext[8:, :] = xr
            xc = _conv4(ext, xr, cw_ref, cb_ref, n)
            sp = _softplus(-lam_ref[...])
            wrg, wig = wrg_ref[...], wig_ref[...]
            r, ig, a, mult = _gates(xc, wrg, wig, brg_ref[...], big_ref[...], sp)

            hv = h_ref[...]
            gl, dgl = _gelu_and_grad(gr_ref[...])
            pre = hv * gl
            dyv = dm_ref[...]
            rr = lax.rsqrt(jnp.mean(pre * pre, axis=-1, keepdims=True) + EPS)
            gdy = g_ref[...] * dyv
            dpre = rr * gdy - pre * ((rr * rr * rr) * jnp.mean(pre * gdy, axis=-1, keepdims=True))
            dg_ref[...] += jnp.sum(dyv * (pre * rr), axis=0, keepdims=True)
            dp_ref[:, RW:] = (dpre * hv * dgl).astype(BF16)

            is_last_row = row == n - 1
            w = dpre * gl + jnp.where(is_last_row, adh[0:1, :], 0.0)
            c = jnp.where(is_last_row, 0.0, pltpu.roll(a, n - 1, 0))
            dh = _scan_bwd(c, w)
            adh[...] = (a * dh)[0:8, :]

            exth[0:8, :] = jnp.where(first_tile, 0.0, hh_ref[...])
            exth[8:, :] = hv
            da = dh * exth[pl.ds(7, n), :]
            ixc = ig * xc
            dmult = dh * ixc
            dla = da * a - dmult * ((a * a) / mult)
            dsp_ref[...] += jnp.sum(dla * (-LRU_C * r), axis=0, keepdims=True)
            dpr = (dla * (-LRU_C * sp)) * (r * (1.0 - r))
            dpi = (dh * (mult * xc)) * (ig * (1.0 - ig))
            dprb, dpib = dpr.astype(BF16), dpi.astype(BF16)
            dxc = dh * (mult * ig) + _dot(dprb, wrg, NT) + _dot(dpib, wig, NT)
            dbr_ref[...] += jnp.sum(dpr, axis=0, keepdims=True)
            dbi_ref[...] += jnp.sum(dpi, axis=0, keepdims=True)
            xc_ref[...] = xc.astype(BF16)
            dr_ref[...] = dprb
            di_ref[...] = dpib

            extd[0:n, :] = dxc
            dxr = dxc * cw_ref[3:4, :] + extd[pl.ds(1, n), :] * cw_ref[2:3, :]
            dxr = dxr + extd[pl.ds(2, n), :] * cw_ref[1:2, :] + extd[pl.ds(3, n), :] * cw_ref[0:1, :]
            extd[n:, :] = dxc[0:8, :]
            dcb_ref[...] += jnp.sum(dxc, axis=0, keepdims=True)
            for kk in range(4):
                dcw_ref[kk:kk + 1, :] += jnp.sum(dxc * ext[pl.ds(5 + kk, n), :], axis=0, keepdims=True)

            @pl.when(first_tile)
            def _():
                dsp_ref[...] = dsp_ref[...] * (-jax.nn.sigmoid(-lam_ref[...]))

            dp_ref[:, 0:RW] = dxr.astype(BF16)

    vec = _bs((1, RW), lambda i, j: (0, 0))
    mat = _bs((RW, RW), lambda i, j: (0, 0))
    tile = lambda cblk: _bs((n, RW), lambda i, j, cblk=cblk: (nt - 1 - i, cblk))
    halo = lambda cblk: _bs((8, RW), lambda i, j, cblk=cblk: (jnp.maximum((nt - 1 - i) * hb - 1, 0), cblk))
    bt = jax.ShapeDtypeStruct((T, RW), BF16)
    v = jax.ShapeDtypeStruct((1, RW), F32)
    return pl.pallas_call(
        body, name="rec_bwd", grid=(nt, 1),
        in_specs=[tile(3), halo(3), tile(4), tile(0), halo(0), tile(1), _bs((8, RW), lambda i, j: (0, 0)), vec,
                  mat, mat, vec, vec, vec, vec],
        out_specs=[_bs((n, 2 * RW), lambda i, j: (nt - 1 - i, 0)), tile(0), tile(0), tile(0),
                   _bs((8, RW), lambda i, j: (0, 0)), vec, vec, vec, vec, vec],
        out_shape=[jax.ShapeDtypeStruct((T, 2 * RW), BF16), bt, bt, bt, jax.ShapeDtypeStruct((8, RW), F32),
                   v, v, v, v, v],
        scratch_shapes=[pltpu.VMEM((n + 8, RW), F32), pltpu.VMEM((n + 8, RW), F32), pltpu.VMEM((n + 8, RW), F32),
                        pltpu.VMEM((8, RW), F32)],
        compiler_params=_cp("arbitrary", "arbitrary"),
    )(proj, proj, proj, h, h, dmix, cw, cb, wrg, wig, brg, big, lam, g_rec)


FC = 1536
TRF = 512


LC = 128


class _RowsBack:
    def __init__(self, before):
        row = lax.broadcasted_iota(jnp.int32, before.shape, 0)
        self.top1, self.top2 = row < 1, row < 2
        self.r1, self.r2 = pltpu.roll(before, 1, 0), pltpu.roll(before, 2, 0)

    def step(self, cur):
        r1, r2 = pltpu.roll(cur, 1, 0), pltpu.roll(cur, 2, 0)
        out = jnp.where(self.top1, self.r1, r1), jnp.where(self.top2, self.r2, r2)
        self.r1, self.r2 = r1, r2
        return out


def _up_act(h2, w_up, cw, cb):
    n = TRF
    nt = T // n
    pw = 256
    npc = FC // pw

    def body(h_ref, wg_ref, wu_ref, cwg_ref, cwu_ref, bg_ref, bu_ref, up_ref, a_ref, fa_ref, fb_ref, hx, gb0, gb1,
             ub0, ub1):
        i = pl.program_id(1)
        halo = h_ref[pl.ds(pl.multiple_of(jnp.maximum(i * n - 16, 0), 16), 16), :]
        hx[0:16, :] = jnp.where(i == 0, jnp.zeros_like(halo), halo)
        hx[16:, :] = h_ref[pl.ds(pl.multiple_of(i * n, n), n), :]
        gbufs, ubufs = (gb0, gb1), (ub0, ub1)

        def dots(c):
            hv = hx[...]
            gbufs[c % 2][...] = _dot(hv, wg_ref[:, c * pw:(c + 1) * pw])
            ubufs[c % 2][...] = _dot(hv, wu_ref[:, c * pw:(c + 1) * pw])

        def chain(c):
            gb, ub = gbufs[c % 2], ubufs[c % 2]
            up_ref[:, c * pw:(c + 1) * pw] = gb[16:, :]
            up_ref[:, FC + c * pw:FC + (c + 1) * pw] = ub[16:, :]
            rows8 = lambda v: jnp.broadcast_to(v, (8, LC))
            for sub in range(pw // LC):
                lc = slice(sub * LC, (sub + 1) * LC)
                cols = slice(c * pw + sub * LC, c * pw + (sub + 1) * LC)
                wg = [rows8(cwg_ref[kk:kk + 1, cols]) for kk in range(3)]
                wu = [rows8(cwu_ref[kk:kk + 1, cols]) for kk in range(3)]
                bg, bu = rows8(bg_ref[:, cols]), rows8(bu_ref[:, cols])
                g_back = _RowsBack(gb[pl.ds(8, 8), lc])
                u_back = _RowsBack(ub[pl.ds(8, 8), lc])
                for r in range(0, n, 16):
                    res, fa, fb = [], [], []
                    for rr in (16 + r, 24 + r):
                        g0, u0 = gb[pl.ds(rr, 8), lc], ub[pl.ds(rr, 8), lc]
                        g1, g2 = g_back.step(g0)
                        u1, u2 = u_back.step(u0)
                        ug = ((bg + g2 * wg[0]) + g1 * wg[1]) + g0 * wg[2]
                        uu = ((bu + u2 * wu[0]) + u1 * wu[1]) + u0 * wu[2]
                        gl, dgl = _gelu_and_grad(ug)
                        res.append(gl * uu)
                        fa.append(uu * dgl)
                        fb.append(gl)
                    a_ref[pl.ds(r, 16), cols] = jnp.concatenate(res, axis=0).astype(BF16)
                    fa_ref[pl.ds(r, 16), cols] = jnp.concatenate(fa, axis=0).astype(BF16)
                    fb_ref[pl.ds(r, 16), cols] = jnp.concatenate(fb, axis=0).astype(BF16)

        dots(0)
        for c in range(npc):
            if c + 1 < npc:
                dots(c + 1)
            chain(c)

    wsl = lambda o: _bs((None, D, FC), lambda j, i, o=o: (2 * j + o, 0, 0))
    wsp = lambda o: _bs((None, 8, FC), lambda j, i, o=o: (2 * j + o, 0, 0))
    bsp = lambda o: _bs((1, FC), lambda j, i, o=o: (0, 2 * j + o))
    return pl.pallas_call(
        body, name="up_act", grid=(2, nt),
        in_specs=[pl.BlockSpec((T, D), lambda j, i: (0, 0), pipeline_mode=pl.Buffered(1)), wsl(0), wsl(1),
                  wsp(0), wsp(1), bsp(0), bsp(1)],
        out_specs=[_bs((n, 2 * FC), lambda j, i: (i, j))] + [_bs((n, FC), lambda j, i: (i, j))] * 3,
        out_shape=[jax.ShapeDtypeStruct((T, 2 * DFF), F32)] + [jax.ShapeDtypeStruct((T, DFF), BF16)] * 3,
        scratch_shapes=[pltpu.VMEM((n + 16, D), BF16)] + [pltpu.VMEM((n + 16, pw), F32)] * 4,
        compiler_params=_cp("parallel", "arbitrary"),
    )(h2, w_up, w_up, cw, cw, cb, cb)


def _ffn_bwd(up_pre, fa, fb, dyb, w_down_t, cw, after=()):
    n = TRF
    hb = n // 8
    nt = T // n
    m = n + 8
    pw = 256
    npc = FC // pw

    def body(g_ref, gp_ref, u_ref, up_ref, fa_ref, fan_ref, fb_ref, fbn_ref, dy_ref, wd_ref, wg_ref, wu_ref, *rest):
        o_ref, dw_ref, db_ref, eg0, eu0, dug_s, duu_s, dyx, db0, db1 = rest[len(after):]
        i = pl.program_id(1)
        first, last = i == 0, i == nt - 1

        @pl.when(first)
        def _():
            dw_ref[...] = jnp.zeros_like(dw_ref)
            db_ref[...] = jnp.zeros_like(db_ref)

        tail = dy_ref[pl.ds(pl.multiple_of(jnp.minimum((i + 1) * n, T - 16), 16), 16), :]
        dyx[0:n, :] = dy_ref[pl.ds(pl.multiple_of(i * n, n), n), :]
        dyx[n:, :] = jnp.where(last, jnp.zeros_like(tail), tail)
        dbufs = (db0, db1)

        def dots(c):
            dbufs[c % 2][...] = _dot(dyx[...], wd_ref[:, c * pw:(c + 1) * pw])

        eg0[0:8, :] = jnp.where(first, 0.0, gp_ref[...])
        eg0[8:, :] = g_ref[0:8, :]
        eu0[0:8, :] = jnp.where(first, 0.0, up_ref[...])
        eu0[8:, :] = u_ref[0:8, :]

        def column(ci, dbuf, lc):
            cols = slice(ci * LC, (ci + 1) * LC)
            ucols = slice(FC + ci * LC, FC + (ci + 1) * LC)
            rows8 = lambda v: jnp.broadcast_to(v, (8, LC))
            wg = [rows8(wg_ref[kk:kk + 1, cols]) for kk in range(3)]
            wu = [rows8(wu_ref[kk:kk + 1, cols]) for kk in range(3)]
            zero = jnp.zeros((8, LC), F32)
            acc = [zero] * 8
            g_back, u_back = _RowsBack(eg0[pl.ds(0, 8), cols]), _RowsBack(eu0[pl.ds(0, 8), cols])
            for r in range(0, n + 16, 16):
                src_a, src_b, r16 = (fan_ref, fbn_ref, 0) if r == n else (fa_ref, fb_ref, r)
                fa16 = src_a[pl.ds(r16, 16), cols].astype(F32)
                fb16 = src_b[pl.ds(r16, 16), cols].astype(F32)
                for half in range(1 if r == n else 2):
                    rr = r + 8 * half
                    dv = dbuf[pl.ds(rr, 8), lc]
                    dug = dv * fa16[8 * half:8 * half + 8, :]
                    duu = dv * fb16[8 * half:8 * half + 8, :]
                    dug_s[pl.ds(rr, 8), :] = dug
                    duu_s[pl.ds(rr, 8), :] = duu
                    if rr < n:
                        g0, u0 = g_ref[pl.ds(rr, 8), cols], u_ref[pl.ds(rr, 8), cols]
                        g1, g2 = g_back.step(g0)
                        u1, u2 = u_back.step(u0)
                        gt, ut = (g2, g1, g0), (u2, u1, u0)
                        acc = [acc[0] + dug * gt[0], acc[1] + dug * gt[1], acc[2] + dug * gt[2],
                               acc[3] + duu * ut[0], acc[4] + duu * ut[1], acc[5] + duu * ut[2],
                               acc[6] + dug, acc[7] + duu]
            for r in range(0, n, 16):
                og, ou = [], []
                for rr in (r, r + 8):
                    og.append((dug_s[pl.ds(rr, 8), :] * wg[2] + dug_s[pl.ds(rr + 1, 8), :] * wg[1])
                              + dug_s[pl.ds(rr + 2, 8), :] * wg[0])
                    ou.append((duu_s[pl.ds(rr, 8), :] * wu[2] + duu_s[pl.ds(rr + 1, 8), :] * wu[1])
                              + duu_s[pl.ds(rr + 2, 8), :] * wu[0])
                o_ref[pl.ds(r, 16), cols] = jnp.concatenate(og, axis=0).astype(BF16)
                o_ref[pl.ds(r, 16), ucols] = jnp.concatenate(ou, axis=0).astype(BF16)
            for kk in range(3):
                dw_ref[kk:kk + 1, cols] += jnp.sum(acc[kk], axis=0, keepdims=True)
                dw_ref[kk:kk + 1, ucols] += jnp.sum(acc[3 + kk], axis=0, keepdims=True)
            db_ref[:, cols] += jnp.sum(acc[6], axis=0, keepdims=True)
            db_ref[:, ucols] += jnp.sum(acc[7], axis=0, keepdims=True)

        dots(0)
        for c in range(npc):
            if c + 1 < npc:
                dots(c + 1)
            for sub in range(pw // LC):
                column(c * (pw // LC) + sub, dbufs[c % 2], slice(sub * LC, (sub + 1) * LC))

    main = lambda o: _bs((n, FC), lambda j, i, o=o: (i, 2 * j + o))
    prev = lambda o: _bs((8, FC), lambda j, i, o=o: (jnp.maximum(i * hb - 1, 0), 2 * j + o))
    saved = _bs((n, FC), lambda j, i: (i, j))
    saved_next = _bs((16, FC), lambda j, i: (jnp.minimum((i + 1) * (n // 16), T // 16 - 1), j))
    wsp = lambda o: _bs((None, 8, FC), lambda j, i, o=o: (2 * j + o, 0, 0))
    return pl.pallas_call(
        body, name="ffn_bwd", grid=(2, nt),
        in_specs=[main(0), prev(0), main(1), prev(1), saved, saved_next, saved, saved_next,
                  pl.BlockSpec((T, D), lambda j, i: (0, 0), pipeline_mode=pl.Buffered(1)),
                  _bs((D, FC), lambda j, i: (0, j)), wsp(0), wsp(1)]
        + [pl.BlockSpec(memory_space=pl.ANY)] * len(after),
        out_specs=[_bs((n, 2 * FC), lambda j, i: (i, j)), _bs((8, 2 * FC), lambda j, i: (0, j)),
                   _bs((1, 2 * FC), lambda j, i: (0, j))],
        out_shape=[jax.ShapeDtypeStruct((T, 2 * DFF), BF16), jax.ShapeDtypeStruct((8, 2 * DFF), F32),
                   jax.ShapeDtypeStruct((1, 2 * DFF), F32)],
        scratch_shapes=[pltpu.VMEM((16, FC), F32)] * 2 + [pltpu.VMEM((m, LC), F32)] * 2
        + [pltpu.VMEM((n + 16, D), BF16)] + [pltpu.VMEM((n + 16, pw), F32)] * 2,
        compiler_params=_cp("parallel", "arbitrary"),
    )(up_pre, up_pre, up_pre, up_pre, fa, fa, fb, fb, dyb, w_down_t, cw, cw, *after)


def _down_loss(act, w_down, x1, target):
    tm, tn = 512, D

    def body(a_ref, b_ref, r_ref, t_ref, dy_ref, dyb_ref, l_ref):
        @pl.when((pl.program_id(0) == 0) & (pl.program_id(1) == 0))
        def _():
            l_ref[...] = jnp.zeros_like(l_ref)

        err = (r_ref[...] + _dot(a_ref[...], b_ref[...])) - t_ref[...]
        dy = err * (1.0 / D)
        dy_ref[...] = dy
        dyb_ref[...] = dy.astype(BF16)
        l_ref[...] += jnp.sum(0.5 * (err * err) * (1.0 / D))

    o_spec = _bs((tm, tn), lambda j, i: (i, j))
    return pl.pallas_call(
        body, name="down_loss", grid=(D // tn, T // tm),
        in_specs=[_bs((tm, DFF), lambda j, i: (i, 0)),
                  pl.BlockSpec((DFF, tn), lambda j, i: (0, j), pipeline_mode=pl.Buffered(1)), o_spec, o_spec],
        out_specs=[o_spec, o_spec, _bs((8, 128), lambda j, i: (0, 0))],
        out_shape=[jax.ShapeDtypeStruct((T, D), F32), jax.ShapeDtypeStruct((T, D), BF16),
                   jax.ShapeDtypeStruct((8, 128), F32)],
        compiler_params=_cp("arbitrary", "arbitrary"),
    )(act, w_down, x1, target)


def _block_diag(w):
    eye = jnp.eye(8, dtype=w.dtype)
    return (w[:, :, None, :] * eye[:, None, :, None]).reshape(RW, RW).astype(BF16)


def _diag_blocks(m):
    eye = jnp.eye(8, dtype=m.dtype)
    return (m.reshape(8, HD, 8, HD) * eye[:, None, :, None]).sum(axis=2)


def _local_step(x, pos_col, target, p, exch):
    qg, kg = jnp.tile(p["q_norm_g"], (1, 8)), jnp.tile(p["k_norm_g"], (1, 8))
    wrg, wig = _block_diag(p["w_rg"]), _block_diag(p["w_ig"])
    brg, big = p["b_rg"].reshape(1, RW), p["b_ig"].reshape(1, RW)

    h1 = _rms_fwd("rms1", x, p["g_mix"])
    p = {**p, **exch.wait_first(h1)}
    proj = _mm("mm_in", h1, p["w_in"], "nn", 512, 640, stack=NCHIP, after=exch.start_rest(), a_full=True)
    q, k, cos_t, sin_t = _qk_prep(proj, pos_col, qg, kg)
    attn, lse = _attn_fwd(q, k, proj)
    mix = _attn_norm(attn, p["g_attn_out"])
    mix, hseq = _rec_fwd(proj, mix, p["rec_conv_w"], p["rec_conv_b"], wrg, wig, brg, big, p["lru_lambda"], p["g_rec_out"])
    rest = exch.wait_rest(mix)
    x1 = _mm("mm_out", mix, rest["w_out"], "nn", 512, 512, res=x, a_full=True)
    h2 = _rms_fwd("rms2", x1, p["g_ffn"])
    up_pre, act, fa, fb = _up_act(h2, rest["w_up"], p["ffn_conv_w"], p["ffn_conv_b"])
    dy, dyb, loss_blk = _down_loss(act, rest["w_down"], x1, target)

    g = {}
    tok = exch.reduce_start("w_down", *_mm("wg_down", act, dyb, "tn", 512, 512, twin_bf16=True))
    dup, g["ffn_conv_w"], g["ffn_conv_b"] = _ffn_bwd(up_pre, fa, fb, dyb, rest["w_down"].T, p["ffn_conv_w"], tok)
    tok = exch.reduce_start("w_up", *_mm("wg_up", h2, dup, "tn", 512, 768, stack=NCHIP, twin_bf16=True, a_full=True))
    dh2 = _mm("dg_up", dup, rest["w_up"], "nt", 512, D, stack=NCHIP, after=tok, b_full=True)
    dx1, dx1b, g["g_ffn"] = _rms_bwd("rms2_bwd", x1, p["g_ffn"], dh2, dy, True)
    tok = exch.reduce_start("w_out", *_mm("wg_out", mix, dx1b, "tn", 512, 512, twin_bf16=True, a_full=True))
    dmix = _mm("dg_out", dx1b, rest["w_out"], "nt", 512, 512, after=tok, a_full=True)
    do, delta, g["g_attn_out"] = _attn_out_bwd(attn, dmix, p["g_attn_out"])
    dq, dk, dv = _attn_bwd(q, k, proj, do, lse, delta)
    dqkv, dqg, dkg = _qk_bwd(proj, cos_t, sin_t, qg, kg, dq, dk, dv)
    (drec, xcb, dprb, dpib, g["rec_conv_w"], g["rec_conv_b"], dbr, dbi, dsp, g["g_rec_out"]) = _rec_bwd(
        proj, hseq, dmix, p["rec_conv_w"], p["rec_conv_b"], wrg, wig, brg, big, p["lru_lambda"], p["g_rec_out"])
    dproj = jnp.concatenate([dqkv, drec], axis=1)
    g["w_rg"] = _diag_blocks(_mm("wg_rg", xcb, dprb, "tn", 512, 512)).reshape(RW, HD)
    g["w_ig"] = _diag_blocks(_mm("wg_ig", xcb, dpib, "tn", 512, 512)).reshape(RW, HD)
    g["b_rg"], g["b_ig"] = dbr.reshape(8, HD), dbi.reshape(8, HD)
    g["lru_lambda"] = dsp
    g["q_norm_g"] = dqg.reshape(8, HD).sum(axis=0, keepdims=True)
    g["k_norm_g"] = dkg.reshape(8, HD).sum(axis=0, keepdims=True)
    tok = exch.reduce_start("w_in", *_mm("wg_in", h1, dproj, "tn", 512, 640, stack=NCHIP, twin_bf16=True, a_full=True))
    dh1 = _mm("dg_in", dproj, p["w_in"], "nt", 512, 512, stack=NCHIP, after=tok)
    grad_x, g["g_mix"] = _rms_bwd("rms1_bwd", x, p["g_mix"], dh1, dx1, False)
    return loss_blk, grad_x, g


ANY = pl.BlockSpec(memory_space=pl.ANY)


def _mesh_pos():
    return lax.axis_index("x"), lax.axis_index("y"), lax.axis_index("c")


def _slot(px, py, perm):
    return 2 * py + px if perm else 2 * px + py


def _other_chips(x, y):
    return [(1 - x, y), (x, 1 - y), (1 - x, 1 - y)]


def _rcopy(src, dst, send, recv, k, to, kr=None):
    return pltpu.make_async_remote_copy(src_ref=src, dst_ref=dst, send_sem=send.at[k],
                                        recv_sem=recv.at[k if kr is None else kr], device_id=to, device_id_type=MESH)


def _cast_bf16(name, w, after=()):
    r, c = w.shape
    tr = 128

    def body(w_ref, *rest):
        rest[-1][...] = w_ref[...].astype(BF16)

    return pl.pallas_call(
        body, name=name, grid=(r // tr,), in_specs=[_bs((tr, c), lambda i: (i, 0))] + [ANY] * len(after),
        out_specs=_bs((tr, c), lambda i: (i, 0)), out_shape=jax.ShapeDtypeStruct((r, c), BF16),
        compiler_params=_cp("parallel"),
    )(w, *after)


def _sibling_fill(lands, perms):
    na = len(lands)

    def body(*refs):
        outs, (send, recv) = refs[na:2 * na], refs[2 * na:]
        x, y, c = _mesh_pos()
        cps = []
        for a in range(na):
            for j, (px, py) in enumerate(_other_chips(x, y)):
                mine = outs[a].at[_slot(px, py, perms[a]), c]
                cps.append(_rcopy(mine, mine, send, recv, 3 * a + j, (x, y, 1 - c)))
        for cp in cps:
            cp.start()
        for a in range(na):
            for j, (px, py) in enumerate(_other_chips(x, y)):
                got = outs[a].at[_slot(px, py, perms[a]), 1 - c]
                _rcopy(got, got, send, recv, 3 * a + j, (x, y, c)).wait_recv()
        for cp in cps:
            cp.wait_send()

    return pl.pallas_call(
        body, name="gather_fill", in_specs=[ANY] * na, out_specs=[ANY] * na,
        out_shape=[jax.ShapeDtypeStruct(a.shape, a.dtype) for a in lands],
        input_output_aliases={i: i for i in range(na)},
        scratch_shapes=[pltpu.SemaphoreType.DMA((3 * na,)), pltpu.SemaphoreType.DMA((3 * na,))],
    )(*lands)


HBM = pl.BlockSpec(memory_space=pltpu.HBM)
SEM = pl.BlockSpec(memory_space=pltpu.SEMAPHORE)
EFFECT = pltpu.SideEffectType.DATAFLOW_SIDE_EFFECTING


def _split_start(name, srcs, lands, plan, nsem):
    ns, nl = len(srcs), len(lands)

    def body(*refs):
        send, recv = refs[ns + nl], refs[ns + nl + 1]
        sends, _ = plan(refs[:ns], refs[ns:ns + nl], send, recv)
        for cp in sends:
            cp.start()
        refs[-1][...] = jnp.zeros((8, 128), F32)

    arrs = list(srcs) + list(lands)
    out = pl.pallas_call(
        body, name=name, in_specs=[HBM] * (ns + nl),
        out_specs=[SEM, SEM] + [HBM] * (ns + nl) + [pl.BlockSpec(memory_space=pltpu.VMEM)],
        out_shape=[pltpu.SemaphoreType.DMA((nsem,)), pltpu.SemaphoreType.DMA((nsem,))]
        + [pltpu.HBM(a.shape, a.dtype) for a in arrs] + [jax.ShapeDtypeStruct((8, 128), F32)],
        input_output_aliases={i: 2 + i for i in range(ns + nl)},
        compiler_params=pltpu.CompilerParams(has_side_effects=EFFECT),
    )(*[pltpu.with_memory_space_constraint(a, pltpu.HBM) for a in arrs])
    return out[0], out[1], out[2:2 + ns], out[2 + ns:2 + ns + nl], out[-1]


def _split_wait(name, send, recv, srcs, lands, plan, after):
    ns, nl = len(srcs), len(lands)

    def body(*refs):
        sends, recvs = plan(refs[:ns], refs[ns:ns + nl], refs[ns + nl], refs[ns + nl + 1])
        for cp in sends:
            cp.wait_send()
        for cp in recvs:
            cp.wait_recv()

    arrs = list(srcs) + list(lands)
    after = tuple(after) if isinstance(after, (tuple, list)) else (after,)
    out = pl.pallas_call(
        body, name=name, in_specs=[HBM] * (ns + nl) + [SEM, SEM] + [ANY] * len(after), out_specs=[HBM] * (ns + nl),
        out_shape=[pltpu.HBM(a.shape, a.dtype) for a in arrs],
        input_output_aliases={i: i for i in range(ns + nl)},
        compiler_params=pltpu.CompilerParams(has_side_effects=EFFECT),
    )(*arrs, send, recv, *after)
    return out[ns:]


def _gather_plan(perms):
    def plan(srcs, lands, send, recv):
        x, y, c = _mesh_pos()
        sends, recvs = [], []
        for a, perm in enumerate(perms):
            for j, (px, py) in enumerate(_other_chips(x, y)):
                for cc in (0, 1):
                    k = 6 * a + 2 * j + cc
                    sends.append(_rcopy(srcs[a].at[c], lands[a].at[_slot(x, y, perm), c], send, recv, k, (px, py, cc),
                                        kr=6 * a + 2 * j + c))
                    got = lands[a].at[_slot(px, py, perm), cc]
                    recvs.append(_rcopy(got, got, send, recv, k, (x, y, c)))
        return sends, recvs
    return plan


def _gather_half_plan(perms, halved):
    def plan(srcs, lands, send, recv):
        x, y, c = _mesh_pos()
        sends, recvs = [], []
        for a, perm in enumerate(perms):
            for j, (px, py) in enumerate(_other_chips(x, y)):
                k = 3 * a + j
                mine, theirs = _slot(x, y, perm), _slot(px, py, perm)
                if halved[a]:
                    sends.append(_rcopy(srcs[a].at[c], lands[a].at[mine, c], send, recv, k, (px, py, c)))
                    got = lands[a].at[theirs, c]
                else:
                    sends.append(_rcopy(srcs[a], lands[a].at[mine], send, recv, k, (px, py, c)))
                    got = lands[a].at[theirs]
                recvs.append(_rcopy(got, got, send, recv, k, (x, y, c)))
        return sends, recvs
    return plan


def _reduce_plan(perm):
    def plan(srcs, lands, send, recv):
        x, y, c = _mesh_pos()
        src, land = srcs[0], lands[0]
        sends = []
        for j, (px, py) in enumerate(_other_chips(x, y)):
            for hf in (0, 1):
                sends.append(_rcopy(src.at[_slot(px, py, perm), hf], land.at[2 * j + c], send, recv, 2 * j + hf,
                                    (px, py, hf), kr=2 * j + c))
        sends.append(_rcopy(src.at[_slot(x, y, perm), 1 - c], land.at[6], send, recv, 6, (x, y, 1 - c)))
        recvs = [_rcopy(land.at[i], land.at[i], send, recv, i, (x, y, c)) for i in range(7)]
        return sends, recvs
    return plan


def _sibling_share(rs):
    na = len(rs)

    def body(*refs):
        ins, outs, (send, recv) = refs[:na], refs[na:2 * na], refs[2 * na:]
        x, y, c = _mesh_pos()
        cps = [_rcopy(ins[a], outs[a], send, recv, a, (x, y, 1 - c)) for a in range(na)]
        for cp in cps:
            cp.start()
        for cp in cps:
            cp.wait()

    return pl.pallas_call(
        body, name="rs_share", in_specs=[ANY] * na, out_specs=[ANY] * na,
        out_shape=[jax.ShapeDtypeStruct(r.shape, F32) for r in rs],
        scratch_shapes=[pltpu.SemaphoreType.DMA((na,)), pltpu.SemaphoreType.DMA((na,))],
    )(*rs)


def _add_pieces(name, g, got, where):
    _, _, r2, cc = g.shape
    tr = 128

    def body(w_ref, g_ref, r_ref, o_ref):
        del w_ref
        acc = g_ref[...]
        for i in range(7):
            acc = acc + r_ref[i].astype(F32)
        o_ref[...] = acc

    return pl.pallas_call(
        body, name=name,
        grid_spec=pltpu.PrefetchScalarGridSpec(
            num_scalar_prefetch=1, grid=(r2 // tr,),
            in_specs=[_bs((None, None, tr, cc), lambda i, w_ref: (w_ref[0], w_ref[1], i, 0)),
                      _bs((7, tr, cc), lambda i, w_ref: (0, i, 0))],
            out_specs=_bs((tr, cc), lambda i, w_ref: (i, 0))),
        out_shape=jax.ShapeDtypeStruct((r2, cc), F32), compiler_params=_cp("parallel"),
    )(where, g, got)


def _adam_math(w, g, m, v):
    m = ADAM_B1 * m + (1.0 - ADAM_B1) * g
    v = ADAM_B2 * v + (1.0 - ADAM_B2) * (g * g)
    m_hat = m / (1.0 - ADAM_B1 ** ADAM_STEP)
    v_hat = v / (1.0 - ADAM_B2 ** ADAM_STEP)
    return -ADAM_LR * (m_hat / (jnp.sqrt(v_hat) + ADAM_EPS) + ADAM_WD * w), m, v


def _adam_big(name, w, g_mine, g_sib, m, v, c_arr):
    r, cols = w.shape
    tr = 128
    per = r // 2 // tr

    def body(c_ref, w_ref, a_ref, b_ref, m_ref, v_ref, g_ref, d_ref, m2_ref, v2_ref):
        g = jnp.where(pl.program_id(0) == c_ref[0], a_ref[...], b_ref[...])
        g_ref[...] = g
        d_ref[...], m2_ref[...], v2_ref[...] = _adam_math(w_ref[...], g, m_ref[...], v_ref[...])

    spec = _bs((tr, cols), lambda h, i, c_ref: (h * per + i, 0))
    half = _bs((tr, cols), lambda h, i, c_ref: (i, 0))
    out = jax.ShapeDtypeStruct((r, cols), F32)
    return pl.pallas_call(
        body, name=name,
        grid_spec=pltpu.PrefetchScalarGridSpec(
            num_scalar_prefetch=1, grid=(2, per), in_specs=[spec, half, half, spec, spec], out_specs=[spec] * 4),
        out_shape=[out] * 4, compiler_params=_cp("parallel", "parallel"),
    )(c_arr, w, g_mine, g_sib, m, v)


_CLASS_SHAPE = {"a": (8, D), "b": (8, RW), "c": (8, 2 * DFF), "d": (1048, HD)}
_SMALL = (
    ("g_mix", "a", 0, 1, D), ("g_ffn", "a", 1, 1, D),
    ("rec_conv_w", "b", 0, 4, RW), ("rec_conv_b", "b", 4, 1, RW), ("lru_lambda", "b", 5, 1, RW),
    ("g_attn_out", "b", 6, 1, RW), ("g_rec_out", "b", 7, 1, RW),
    ("ffn_conv_w", "c", 0, 3, 2 * DFF), ("ffn_conv_b", "c", 3, 1, 2 * DFF),
    ("w_rg", "d", 0, RW, HD), ("w_ig", "d", RW, RW, HD), ("b_rg", "d", 2 * RW, 8, HD), ("b_ig", "d", 2 * RW + 8, 8, HD),
    ("q_norm_g", "d", 2 * RW + 16, 1, HD), ("k_norm_g", "d", 2 * RW + 17, 1, HD),
)
_LOSS_ROW = 2
_CLASSES = ("a", "b", "c", "d")
_CLASS_OWNER = {"a": 0, "b": 0, "c": 0, "d": 1}


def _small_allreduce(g, loss_blk):
    names = [s[0] for s in _SMALL]
    nin = len(names) + 1

    def body(*refs):
        ins = dict(zip(names, refs[:len(names)]))
        loss_ref = refs[len(names)]
        outs = dict(zip(_CLASSES, refs[nin:nin + 4]))
        pair = dict(zip(_CLASSES, refs[nin + 4:nin + 8]))
        quad = dict(zip(_CLASSES, refs[nin + 8:nin + 12]))
        send, recv = refs[nin + 12:]
        x, y, c = _mesh_pos()
        chip = 2 * x + y
        pair["a"][c] = jnp.zeros(_CLASS_SHAPE["a"], F32)
        pair["b"][c] = ins["rec_conv_w"][...]
        pair["c"][c] = ins["ffn_conv_w"][...]
        pair["d"][c, 2 * RW + 16:, :] = jnp.zeros((8, HD), F32)
        for name, k, r0, nr, _ in _SMALL:
            if name in ("rec_conv_w", "ffn_conv_w"):
                continue
            pair[k][c, r0:r0 + nr, :] = ins[name][...]
        pair["a"][c, _LOSS_ROW:_LOSS_ROW + 1, :] = jnp.broadcast_to(loss_ref[0:1, 0:1], (1, D))
        cps = [_rcopy(pair[k].at[c], pair[k].at[c], send, recv, ki, (x, y, 1 - c)) for ki, k in enumerate(_CLASSES)]
        for cp in cps:
            cp.start()
        for ki, k in enumerate(_CLASSES):
            _rcopy(pair[k].at[1 - c], pair[k].at[1 - c], send, recv, ki, (x, y, c)).wait_recv()
            quad[k][chip] = pair[k][0] + pair[k][1]
        for cp in cps:
            cp.wait_send()
        for ki, k in enumerate(_CLASSES):
            owner = _CLASS_OWNER[k]

            @pl.when(c == owner)
            def _(ki=ki, k=k):
                cps2 = [_rcopy(quad[k].at[chip], quad[k].at[chip], send, recv, 4 + 3 * ki + j, (px, py, c))
                        for j, (px, py) in enumerate(_other_chips(x, y))]
                for cp in cps2:
                    cp.start()
                for j, (px, py) in enumerate(_other_chips(x, y)):
                    got = quad[k].at[2 * px + py]
                    _rcopy(got, got, send, recv, 4 + 3 * ki + j, (x, y, c)).wait_recv()
                outs[k][...] = ((quad[k][0] + quad[k][1]) + quad[k][2]) + quad[k][3]
                share = _rcopy(outs[k], outs[k], send, recv, 16 + ki, (x, y, 1 - c))
                share.start()
                for cp in cps2:
                    cp.wait_send()
                share.wait_send()

        for ki, k in enumerate(_CLASSES):
            @pl.when(c != _CLASS_OWNER[k])
            def _(ki=ki, k=k):
                _rcopy(outs[k], outs[k], send, recv, 16 + ki, (x, y, c)).wait_recv()

    vm = pl.BlockSpec(memory_space=pltpu.VMEM)
    return pl.pallas_call(
        body, name="small_allreduce", in_specs=[vm] * nin, out_specs=[vm] * 4,
        out_shape=[jax.ShapeDtypeStruct(_CLASS_SHAPE[k], F32) for k in _CLASSES],
        scratch_shapes=[pltpu.VMEM((2,) + _CLASS_SHAPE[k], F32) for k in _CLASSES]
        + [pltpu.VMEM((NCHIP,) + _CLASS_SHAPE[k], F32) for k in _CLASSES]
        + [pltpu.SemaphoreType.DMA((20,)), pltpu.SemaphoreType.DMA((20,))],
        compiler_params=pltpu.CompilerParams(vmem_limit_bytes=VMEM_LIMIT),
    )(*[g[n] for n in names], loss_blk)


def _adam_small(red, w, m, v):
    names = [s[0] for s in _SMALL]
    n = len(names)

    def body(*refs):
        red_refs = dict(zip(_CLASSES, refs[:4]))
        w_refs, m_refs, v_refs = refs[4:4 + n], refs[4 + n:4 + 2 * n], refs[4 + 2 * n:4 + 3 * n]
        loss_ref = refs[4 + 3 * n]
        out_refs = refs[5 + 3 * n:]
        x, y, _ = _mesh_pos()
        chip = 2 * x + y
        loss_ref[...] = jnp.broadcast_to(red_refs["a"][_LOSS_ROW:_LOSS_ROW + 1, 0:1], loss_ref.shape)
        for pi, (name, k, r0, nr, width) in enumerate(_SMALL):
            gfull = red_refs[k][r0:r0 + nr, :]
            if name == "rec_conv_w":
                parts = [gfull[:, 128 * s:128 * (s + 1)] for s in range(NCHIP)]
                g = jnp.where(chip == 0, parts[0], jnp.where(chip == 1, parts[1], jnp.where(chip == 2, parts[2], parts[3])))
            elif name == "ffn_conv_w":
                parts = [gfull[:, FC * s:FC * (s + 1)] for s in range(NCHIP)]
                g = jnp.where(chip == 0, parts[0], jnp.where(chip == 1, parts[2], jnp.where(chip == 2, parts[1], parts[3])))
            elif name == "ffn_conv_b":
                g = jnp.concatenate([gfull[:, FC * s:FC * (s + 1)] for s in (0, 2, 1, 3)], axis=1)
            else:
                g = gfull
            d, m2, v2 = _adam_math(w_refs[pi][...], g, m_refs[pi][...], v_refs[pi][...])
            o = out_refs[4 * pi:4 * pi + 4]
            o[0][...], o[1][...], o[2][...], o[3][...] = g, d, m2, v2

    vm = pl.BlockSpec(memory_space=pltpu.VMEM)
    outs = [jax.ShapeDtypeStruct((1, 128), F32)]
    for name in names:
        outs += [jax.ShapeDtypeStruct(w[name].shape, F32)] * 4
    res = pl.pallas_call(
        body, name="adam_small", in_specs=[vm] * (4 + 3 * n), out_specs=[vm] * len(outs), out_shape=outs,
        compiler_params=pltpu.CompilerParams(vmem_limit_bytes=VMEM_LIMIT),
    )(*red, *[w[k] for k in names], *[m[k] for k in names], *[v[k] for k in names])
    return res[0], {name: res[1 + 4 * i:5 + 4 * i] for i, name in enumerate(names)}


_WEIGHTS = ("g_mix", "w_in", "q_norm_g", "k_norm_g", "rec_conv_w", "rec_conv_b", "w_rg", "b_rg", "w_ig", "b_ig",
            "lru_lambda", "g_attn_out", "g_rec_out", "w_out", "g_ffn", "w_up", "ffn_conv_w", "ffn_conv_b", "w_down")
_BIG = ("w_in", "w_out", "w_up", "w_down")
_BIG_PERM = {"w_in": False, "w_out": False, "w_up": True, "w_down": False}
_SMALL_2D = {"w_rg": (RW, HD), "w_ig": (RW, HD), "b_rg": (8, HD), "b_ig": (8, HD), "rec_conv_w": (4, 128),
             "ffn_conv_w": (3, FC)}


def _halves(a):
    r, c = a.shape
    return a.reshape(2, r // 2, c)


def kernel(x, positions, g_mix, w_in, q_norm_g, k_norm_g, rec_conv_w, rec_conv_b, w_rg, b_rg, w_ig, b_ig, lru_lambda, g_attn_out, g_rec_out, w_out, g_ffn, w_up, ffn_conv_w, ffn_conv_b, w_down, loss_target, m_g_mix, m_w_in, m_q_norm_g, m_k_norm_g, m_rec_conv_w, m_rec_conv_b, m_w_rg, m_b_rg, m_w_ig, m_b_ig, m_lru_lambda, m_g_attn_out, m_g_rec_out, m_w_out, m_g_ffn, m_w_up, m_ffn_conv_w, m_ffn_conv_b, m_w_down, v_g_mix, v_w_in, v_q_norm_g, v_k_norm_g, v_rec_conv_w, v_rec_conv_b, v_w_rg, v_b_rg, v_w_ig, v_b_ig, v_lru_lambda, v_g_attn_out, v_g_rec_out, v_w_out, v_g_ffn, v_w_up, v_ffn_conv_w, v_ffn_conv_b, v_w_down):
    given = dict(g_mix=g_mix, w_in=w_in, q_norm_g=q_norm_g, k_norm_g=k_norm_g, rec_conv_w=rec_conv_w, rec_conv_b=rec_conv_b, w_rg=w_rg, b_rg=b_rg, w_ig=w_ig, b_ig=b_ig, lru_lambda=lru_lambda, g_attn_out=g_attn_out, g_rec_out=g_rec_out, w_out=w_out, g_ffn=g_ffn, w_up=w_up, ffn_conv_w=ffn_conv_w, ffn_conv_b=ffn_conv_b, w_down=w_down)
    given_m = dict(g_mix=m_g_mix, w_in=m_w_in, q_norm_g=m_q_norm_g, k_norm_g=m_k_norm_g, rec_conv_w=m_rec_conv_w, rec_conv_b=m_rec_conv_b, w_rg=m_w_rg, b_rg=m_b_rg, w_ig=m_w_ig, b_ig=m_b_ig, lru_lambda=m_lru_lambda, g_attn_out=m_g_attn_out, g_rec_out=m_g_rec_out, w_out=m_w_out, g_ffn=m_g_ffn, w_up=m_w_up, ffn_conv_w=m_ffn_conv_w, ffn_conv_b=m_ffn_conv_b, w_down=m_w_down)
    given_v = dict(g_mix=v_g_mix, w_in=v_w_in, q_norm_g=v_q_norm_g, k_norm_g=v_k_norm_g, rec_conv_w=v_rec_conv_w, rec_conv_b=v_rec_conv_b, w_rg=v_w_rg, b_rg=v_b_rg, w_ig=v_w_ig, b_ig=v_b_ig, lru_lambda=v_lru_lambda, g_attn_out=v_g_attn_out, g_rec_out=v_g_rec_out, w_out=v_w_out, g_ffn=v_g_ffn, w_up=v_w_up, ffn_conv_w=v_ffn_conv_w, ffn_conv_b=v_ffn_conv_b, w_down=v_w_down)
    shapes = {n: a.shape for n, a in given.items()}

    def two_d(n, a):
        a = a[0]
        return a.reshape(_SMALL_2D[n]) if n in _SMALL_2D else (a if a.ndim == 2 else a[None])

    w = {n: two_d(n, a) for n, a in given.items()}
    m = {n: two_d(n, a) for n, a in given_m.items()}
    v = {n: two_d(n, a) for n, a in given_v.items()}
    cc = lax.axis_index("c").astype(jnp.int32)
    cx, cy = lax.axis_index("x").astype(jnp.int32), lax.axis_index("y").astype(jnp.int32)
    slot = {False: 2 * cx + cy, True: 2 * cy + cx}

    shards = {"w_in": _halves(_cast_bf16("cast_w_in", w["w_in"]))}
    first = [shards["w_in"], jnp.pad(w["ffn_conv_w"], ((0, 5), (0, 0))), jnp.pad(w["rec_conv_w"], ((0, 4), (0, 0)))]
    first_perm = [False, True, False]
    first_plan = _gather_half_plan(first_perm, [True, False, False])
    in_flight = _split_start(
        "gather_in_start", first,
        [lax.dynamic_update_slice(lax.empty((NCHIP,) + a.shape, a.dtype), a[None], (slot[pm],) + (0,) * a.ndim)
         for a, pm in zip(first, first_perm)], first_plan, 3 * len(first))
    for n in ("w_out", "w_up", "w_down"):
        shards[n] = _halves(_cast_bf16(f"cast_{n}", w[n], after=(in_flight[4],)))
    p = {n: w[n] for n in ("g_mix", "g_ffn", "q_norm_g", "k_norm_g", "rec_conv_b", "lru_lambda", "g_attn_out", "g_rec_out")}
    p.update(w_rg=w["w_rg"].reshape(8, HD, HD), w_ig=w["w_ig"].reshape(8, HD, HD), b_rg=w["b_rg"], b_ig=w["b_ig"],
             ffn_conv_b=jnp.concatenate([w["ffn_conv_b"][:, FC * s:FC * (s + 1)] for s in (0, 2, 1, 3)], axis=1))

    class Exchange:
        rest = ("w_out", "w_up", "w_down")
        order = []
        flight = {}

        def wait_first(self, after):
            send, recv, srcs, lands, _ = in_flight
            f_in, f_fcw, f_rcw = _split_wait("gather_in_wait", send, recv, srcs, lands, first_plan,
                                             (after,) + tuple(shards[n] for n in self.rest))
            (f_in,) = _sibling_fill([f_in], [False])
            return dict(w_in=f_in.reshape(NCHIP, D, INW // NCHIP), ffn_conv_w=f_fcw,
                        rec_conv_w=f_rcw.transpose(1, 0, 2).reshape(8, RW))

        def start_rest(self):
            srcs = [shards[n] for n in self.rest]
            lands = [lax.dynamic_update_slice(lax.empty((NCHIP,) + s.shape, BF16), s[None], (slot[_BIG_PERM[n]], 0, 0, 0))
                     for n, s in zip(self.rest, srcs)]
            plan = _gather_plan([_BIG_PERM[n] for n in self.rest])
            send, recv, srcs, lands, token = _split_start("gather_rest_start", srcs, lands, plan, 6 * len(srcs))
            self.flight["rest"] = (send, recv, srcs, lands, plan)
            return (token,)

        def wait_rest(self, after):
            send, recv, srcs, lands, plan = self.flight.pop("rest")
            f_out, f_up, f_down = _split_wait("gather_rest_wait", send, recv, srcs, lands, plan, after)
            return dict(w_out=f_out.reshape(D, D), w_up=f_up.reshape(NCHIP, D, FC), w_down=f_down.reshape(DFF, D))

        def reduce_start(self, name, g32, g16):
            r2, cols = shards[name].shape[1:]
            plan = _reduce_plan(_BIG_PERM[name])
            send, recv, srcs, lands, token = _split_start(
                f"reduce_{name}_start", [g16.reshape(NCHIP, 2, r2, cols)], [lax.empty((7, r2, cols), BF16)], plan, 7)
            self.flight[name] = (send, recv, srcs, lands, plan, g32.reshape(NCHIP, 2, r2, cols))
            self.order.append(name)
            return (token,)

        def finish(self, after):
            mine = {}
            for name in self.order:
                send, recv, srcs, lands, plan, g32 = self.flight.pop(name)
                (got,) = _split_wait(f"reduce_{name}_wait", send, recv, srcs, lands, plan, after)
                where = jnp.stack([slot[_BIG_PERM[name]], cc])
                mine[name] = after = _add_pieces(f"reduce_{name}_add", g32, got, where)
            theirs = dict(zip(_BIG, _sibling_share([mine[n] for n in _BIG])))
            return mine, theirs

    exch = Exchange()

    loss_blk, grad_x, g = _local_step(x[0], positions.reshape(T, 1), loss_target[0], p, exch)

    out_g, out_d, out_m, out_v = {}, {}, {}, {}
    red = _small_allreduce(g, loss_blk)
    loss_row, small_out = _adam_small(red, w, m, v)
    for n, (gn, dn, mn, vn) in small_out.items():
        out_g[n], out_d[n], out_m[n], out_v[n] = gn, dn, mn, vn

    mine, theirs = exch.finish(red[0])
    for n in _BIG:
        out_g[n], out_d[n], out_m[n], out_v[n] = _adam_big(f"adam_{n}", w[n], mine[n], theirs[n], m[n], v[n], cc.reshape(1))

    outs = [loss_row[0, 0], grad_x[None]]
    for group in (out_g, out_d, out_m, out_v):
        outs += [group[n].reshape(shapes[n]) for n in _WEIGHTS]
    return tuple(outs)
```

```python
import math

import jax
import jax.numpy as jnp
import numpy as np
from jax import lax
from jax.experimental import pallas as pl
from jax.experimental.pallas import tpu as pltpu

F32 = jnp.float32
BF16 = jnp.bfloat16

T = 4096
D = 1024
HD = 64
AW = 512
RW = 512
INW = 2560
DFF = 3072
NCHIP = 4
EPS = 1e-6
NEG = -1e30
LRU_C = 8.0
ROPE_THETA = 10000.0
BLK = 128
DILATIONS = (1, 4, 16)
ADAM_LR, ADAM_B1, ADAM_B2, ADAM_EPS, ADAM_WD, ADAM_STEP = 0.001, 0.9, 0.999, 1e-08, 0.01, 10
VMEM_LIMIT = 56 * 1024 * 1024
MESH = pl.DeviceIdType.MESH

NN = (((1,), (0,)), ((), ()))
NT = (((1,), (1,)), ((), ()))
TN = (((0,), (0,)), ((), ()))


def _cp(*sem):
    return pltpu.CompilerParams(dimension_semantics=sem, vmem_limit_bytes=VMEM_LIMIT)


def _bs(shape, fn):
    return pl.BlockSpec(shape, fn)


def _dot(a, b, dims=NN):
    return lax.dot_general(a, b, dims, preferred_element_type=F32)


_GC = math.sqrt(2.0 / math.pi)


def _gelu(x):
    return x * (0.5 + 0.5 * jnp.tanh(x * (_GC + (_GC * 0.044715) * (x * x))))


def _gelu_and_grad(x):
    x2 = x * x
    th = jnp.tanh(x * (_GC + (_GC * 0.044715) * x2))
    cdf = 0.5 + 0.5 * th
    dg = cdf + (x * (1.0 - th * th)) * ((0.5 * _GC) + (1.5 * 0.044715 * _GC) * x2)
    return x * cdf, dg


def _softplus(x):
    e = jnp.exp(-jnp.abs(x))
    u = 1.0 + e
    l1p = jnp.where(u == 1.0, e, jnp.log(u) * (e / (u - 1.0)))
    return jnp.maximum(x, 0.0) + l1p


def _segsum(z, e_bf16):
    hi = z.astype(BF16)
    lo = (z - hi.astype(F32)).astype(BF16)
    parts = []
    for c0 in range(0, z.shape[1], 128):
        parts.append(_dot(hi[:, c0:c0 + 128], e_bf16) + _dot(lo[:, c0:c0 + 128], e_bf16))
    return jnp.concatenate(parts, axis=1)


def _mm(name, a, b, mode, tm, tn, out_dtype=F32, res=None, stack=0, twin_bf16=False, after=(), a_full=False,
        b_full=False):
    if mode == "nn":
        (m, k), n = a.shape, (b.shape[1] if not stack else stack * b.shape[2])
        a_spec = _bs((tm, k), lambda j, i: (i, 0))
        if stack:
            per = b.shape[2] // tn
            b_spec = _bs((None, k, tn), lambda j, i: (j // per, 0, j % per))
        else:
            b_spec = _bs((k, tn), lambda j, i: (0, j))
    elif mode == "nt":
        (m, k), n = a.shape, (b.shape[0] if not stack else b.shape[1])
        a_spec = _bs((tm, k), lambda j, i: (i, 0))
        b_spec = _bs((stack, tn, k // stack), lambda j, i: (0, j, 0)) if stack else _bs((tn, k), lambda j, i: (j, 0))
    else:
        (k, m), n = a.shape, b.shape[1]
        a_spec, b_spec = _bs((k, tm), lambda j, i: (0, i)), _bs((k, tn), lambda j, i: (0, j))
    assert m % tm == 0 and n % tn == 0
    o_spec = _bs((tm, tn), lambda j, i: (i, j))
    o_shape = (m, n)
    if mode == "tn" and stack:
        per = n // stack // tn
        o_spec = _bs((None, tm, tn), lambda j, i: (j // per, i, j % per))
        o_shape = (stack, m, n // stack)
    dims = {"nn": NN, "nt": NT, "tn": TN}[mode]
    once = pl.Buffered(1)
    if a_full:
        a_spec = pl.BlockSpec(a.shape, lambda j, i: (0, 0), pipeline_mode=once)
    if b_full:
        assert n == tn
        b_spec = pl.BlockSpec(b_spec.block_shape, b_spec.index_map, pipeline_mode=once)

    def product(a_ref, b_ref):
        if a_full:
            mine = pl.ds(pl.multiple_of(pl.program_id(1) * tm, tm), tm)
            take = (lambda cols: a_ref[:, mine]) if mode == "tn" else (lambda cols: a_ref[mine, cols])
        else:
            take = lambda cols: a_ref[:, cols]
        if mode == "nt" and stack:
            cs = k // stack
            acc = _dot(take(pl.ds(0, cs)), b_ref[0], NT)
            for s in range(1, stack):
                acc = acc + _dot(take(pl.ds(s * cs, cs)), b_ref[s], NT)
            return acc
        return _dot(take(slice(None)), b_ref[...], dims)

    nres = 0 if res is None else 1

    def body(a_ref, b_ref, *rest):
        acc = product(a_ref, b_ref)
        if nres:
            acc = rest[0][...] + acc
        outs = rest[nres + len(after):]
        outs[0][...] = acc.astype(out_dtype)
        if twin_bf16:
            outs[1][...] = acc.astype(BF16)

    ins = (a, b) + ((res,) if nres else ()) + tuple(after)
    specs = [a_spec, b_spec] + ([o_spec] if nres else []) + [pl.BlockSpec(memory_space=pl.ANY)] * len(after)
    shapes = [jax.ShapeDtypeStruct(o_shape, out_dtype)] + ([jax.ShapeDtypeStruct(o_shape, BF16)] if twin_bf16 else [])
    out = pl.pallas_call(
        body, name=name, grid=(n // tn, m // tm), in_specs=specs, out_specs=[o_spec] * len(shapes),
        out_shape=shapes, compiler_params=_cp("parallel", "parallel"),
    )(*ins)
    return tuple(out) if twin_bf16 else out[0]


def _rms_fwd(name, x, g):
    tr = 512

    def body(x_ref, g_ref, o_ref):
        xv = x_ref[...]
        r = lax.rsqrt(jnp.mean(xv * xv, axis=-1, keepdims=True) + EPS)
        o_ref[...] = ((xv * r) * g_ref[...]).astype(BF16)

    return pl.pallas_call(
        body, name=name, grid=(T // tr,), in_specs=[_bs((tr, D), lambda i: (i, 0)), _bs((1, D), lambda i: (0, 0))],
        out_specs=_bs((tr, D), lambda i: (i, 0)), out_shape=jax.ShapeDtypeStruct((T, D), BF16),
        compiler_params=_cp("parallel"),
    )(x, g)


def _rms_bwd(name, x, g, dy, dres, want_bf16):
    tr = 512

    def body(x_ref, g_ref, dy_ref, dr_ref, dx_ref, *rest):
        dg_ref = rest[-1]
        xv, dyv = x_ref[...], dy_ref[...]
        r = lax.rsqrt(jnp.mean(xv * xv, axis=-1, keepdims=True) + EPS)
        gdy = g_ref[...] * dyv
        dx = r * gdy - xv * ((r * r * r) * jnp.mean(xv * gdy, axis=-1, keepdims=True)) + dr_ref[...]
        dx_ref[...] = dx
        if want_bf16:
            rest[0][...] = dx.astype(BF16)

        @pl.when(pl.program_id(0) == 0)
        def _():
            dg_ref[...] = jnp.zeros_like(dg_ref)

        dg_ref[...] += jnp.sum(dyv * (xv * r), axis=0, keepdims=True)

    row = _bs((tr, D), lambda i: (i, 0))
    vec = _bs((1, D), lambda i: (0, 0))
    outs = [jax.ShapeDtypeStruct((T, D), F32)] + ([jax.ShapeDtypeStruct((T, D), BF16)] if want_bf16 else [])
    return pl.pallas_call(
        body, name=name, grid=(T // tr,), in_specs=[row, vec, row, row],
        out_specs=[row] * len(outs) + [vec], out_shape=outs + [jax.ShapeDtypeStruct((1, D), F32)],
        compiler_params=_cp("arbitrary"),
    )(x, g, dy, dres)


def _head_ones():
    idx = np.arange(128) // HD
    return jnp.asarray((idx[:, None] == idx[None, :]).astype(np.float32), dtype=BF16)


def _freq_row():
    half = HD // 2
    inv = ROPE_THETA ** (-(np.arange(half, dtype=np.float64)) / half)
    return jnp.asarray(np.tile(inv, 4)[None, :], dtype=F32)


def _rot_tables(cos128, sin128):
    c = jnp.tile(cos128, (1, 4))
    s = jnp.tile(sin128, (1, 4))
    lane = lax.broadcasted_iota(jnp.int32, (1, AW), 1)
    first = (lane & 32) == 0
    return c, jnp.where(first, -s, s), first


def _swap_halves(y, first):
    return jnp.where(first, pltpu.roll(y, AW - 32, 1), pltpu.roll(y, 32, 1))


def _qk_prep(proj, pos_col, qg, kg):
    tr = 512

    def body(q_ref, k_ref, pos_ref, f_ref, qg_ref, kg_ref, e_ref, qo_ref, ko_ref, cos_ref, sin_ref):
        ang = pos_ref[...].astype(F32) * f_ref[...]
        cos_ref[...] = jnp.cos(ang)
        sin_ref[...] = jnp.sin(ang)
        c, s_signed, first = _rot_tables(cos_ref[...], sin_ref[...])
        e = e_ref[...]

        def norm_rot(xv, g, scale):
            r = lax.rsqrt(_segsum(xv * xv, e) * (1.0 / HD) + EPS)
            y = (xv * r) * g
            return (y * c + _swap_halves(y, first) * s_signed) * scale

        qo_ref[...] = norm_rot(q_ref[...], qg_ref[...], HD ** -0.5)
        ko_ref[...] = norm_rot(k_ref[...], kg_ref[...], 1.0)

    col = lambda j: _bs((tr, AW), lambda i, j=j: (i, j))
    vec = _bs((1, AW), lambda i: (0, 0))
    out = jax.ShapeDtypeStruct((T, AW), F32)
    tab = jax.ShapeDtypeStruct((T, 128), F32)
    tspec = _bs((tr, 128), lambda i: (i, 0))
    return pl.pallas_call(
        body, name="qk_prep", grid=(T // tr,),
        in_specs=[col(0), col(1), _bs((tr, 1), lambda i: (i, 0)), _bs((1, 128), lambda i: (0, 0)), vec, vec,
                  _bs((128, 128), lambda i: (0, 0))],
        out_specs=[col(0)] * 2 + [tspec] * 2, out_shape=[out, out, tab, tab], compiler_params=_cp("parallel"),
    )(proj, proj, pos_col, _freq_row(), qg, kg, _head_ones())


def _qk_bwd(proj, cos_t, sin_t, qg, kg, dq, dk, dv):
    tr = 512

    def body(q_ref, k_ref, cos_ref, sin_ref, qg_ref, kg_ref, e_ref, dq_ref, dk_ref, dv_ref, o_ref, dqg_ref, dkg_ref):
        i, j = pl.program_id(0), pl.program_id(1)

        @pl.when((i == 0) & (j == 0))
        def _():
            dqg_ref[...] = jnp.zeros_like(dqg_ref)
            dkg_ref[...] = jnp.zeros_like(dkg_ref)

        def norm_rot_bwd(x_ref, g_ref, dg_ref, d_ref, scale):
            c, s_signed, first = _rot_tables(cos_ref[...], sin_ref[...])
            e = e_ref[...]
            dout = d_ref[...] * scale
            dy = dout * c + _swap_halves(dout * s_signed, first)
            xv, g = x_ref[...], g_ref[...]
            r = lax.rsqrt(_segsum(xv * xv, e) * (1.0 / HD) + EPS)
            gdy = g * dy
            dx = r * gdy - xv * ((r * r * r) * (_segsum(xv * gdy, e) * (1.0 / HD)))
            o_ref[...] = dx.astype(BF16)
            dg_ref[...] += jnp.sum(dy * (xv * r), axis=0, keepdims=True)

        @pl.when(j == 0)
        def _():
            o_ref[...] = dv_ref[...].astype(BF16)

        @pl.when(j == 1)
        def _():
            norm_rot_bwd(q_ref, qg_ref, dqg_ref, dq_ref, HD ** -0.5)

        @pl.when(j == 2)
        def _():
            norm_rot_bwd(k_ref, kg_ref, dkg_ref, dk_ref, 1.0)

    col = lambda jj: _bs((tr, AW), lambda i, j, jj=jj: (i, jj))
    vec = _bs((1, AW), lambda i, j: (0, 0))
    piece = _bs((tr, AW), lambda i, j: (i, 0))
    return pl.pallas_call(
        body, name="qk_bwd", grid=(T // tr, 3),
        in_specs=[col(0), col(1), _bs((tr, 128), lambda i, j: (i, 0)), _bs((tr, 128), lambda i, j: (i, 0)), vec, vec,
                  _bs((128, 128), lambda i, j: (0, 0))] + [piece] * 3,
        out_specs=[_bs((tr, AW), lambda i, j: (i, (j + 2) % 3)), vec, vec],
        out_shape=[jax.ShapeDtypeStruct((T, 3 * AW), BF16), jax.ShapeDtypeStruct((1, AW), F32),
                   jax.ShapeDtypeStruct((1, AW), F32)],
        compiler_params=_cp("arbitrary", "arbitrary"),
    )(proj, proj, cos_t, sin_t, qg, kg, _head_ones(), dq, dk, dv)


RG = 256


def _stacked_band_mask():
    qi = lax.broadcasted_iota(jnp.int32, (2 * BLK, 2 * BLK), 0) & (BLK - 1)
    kj = lax.broadcasted_iota(jnp.int32, (2 * BLK, 2 * BLK), 1)
    rel = qi - kj + BLK
    return (rel >= 0) & (rel <= BLK), lax.broadcasted_iota(jnp.int32, (1, 2 * BLK), 1) >= BLK


def _natural_rows(r0, n_rows, d):
    if d == 1:
        return pl.ds(r0, n_rows)
    ln = T // d
    return pl.ds(r0 // ln + d * (r0 % ln), n_rows, stride=d)


def _regroup_into(dst, src_ref, d, pad, cast=True):
    def step(j, carry):
        r0 = pl.multiple_of(j * RG, RG)
        val = src_ref[_natural_rows(r0, RG, d), :]
        dst[pl.ds(pad + r0, RG), :] = val.astype(dst.dtype) if cast else val
        return carry
    lax.fori_loop(0, T // RG, step, 0)


def _stack_heads(x, h0):
    zero = jnp.zeros_like(x)
    return jnp.concatenate([jnp.where(h0, x, zero), jnp.where(h0, zero, x)], axis=0)


def _attn_fwd(q, k, proj):
    nblk = T // BLK

    def body(q_ref, k_ref, v_ref, a_ref, lse_ref, qs, ks, vs, o0, o1, o2, l0, l1, l2, sb0, sb1):
        band, cur_half = _stacked_band_mask()
        h0 = lax.broadcasted_iota(jnp.int32, (1, 128), 1) < HD
        ks[0:BLK, :] = jnp.zeros((BLK, 128), BF16)
        vs[0:BLK, :] = jnp.zeros((BLK, 128), BF16)
        for d, o_s, l_s in zip(DILATIONS, (o0, o1, o2), (l0, l1, l2)):
            nb = T // d // BLK
            _regroup_into(qs, q_ref, d, 0)
            _regroup_into(ks, k_ref, d, BLK)
            _regroup_into(vs, v_ref, d, BLK)

            def scores(b):
                r0 = pl.multiple_of(b * BLK, BLK)
                return _dot(_stack_heads(qs[pl.ds(r0, BLK), :], h0), ks[pl.ds(r0, 2 * BLK), :], NT)

            def finish(b, s_raw, d=d, nb=nb, o_s=o_s, l_s=l_s):
                r0 = pl.multiple_of(b * BLK, BLK)
                mask = band & (cur_half | ((b & (nb - 1)) > 0))
                s = jnp.where(mask, s_raw, NEG)
                m = jnp.max(s, axis=1, keepdims=True)
                p = jnp.exp(s - m)
                l = jnp.sum(p, axis=1, keepdims=True)
                o = _dot(p.astype(BF16), vs[pl.ds(r0, 2 * BLK), :]) / l
                lse = m + jnp.log(l)
                rows = _natural_rows(r0, BLK, d)
                o_s[rows, :] = jnp.where(h0, o[0:BLK, :], o[BLK:, :])
                l_s[rows, :] = jnp.where(h0, lse[0:BLK, :], lse[BLK:, :])

            sb0[...] = scores(0)

            def step(i, carry):
                b = 2 * i
                sb1[...] = scores(b + 1)
                finish(b, sb0[...])
                sb0[...] = scores(jnp.minimum(b + 2, nblk - 1))
                finish(b + 1, sb1[...])
                return carry

            lax.fori_loop(0, nblk // 2, step, 0)

        def merge(i, carry):
            r = pl.ds(pl.multiple_of(i * RG, RG), RG)
            la, lb, lc = l0[r, :], l1[r, :], l2[r, :]
            m = jnp.maximum(jnp.maximum(la, lb), lc)
            ea, eb, ec = jnp.exp(la - m), jnp.exp(lb - m), jnp.exp(lc - m)
            z = (ea + eb) + ec
            a_ref[r, :] = ((ea * o0[r, :] + eb * o1[r, :]) + ec * o2[r, :]) / z
            lse_ref[r, :] = m + jnp.log(z)
            return carry

        lax.fori_loop(0, T // RG, merge, 0)

    spec = lambda cb: _bs((T, 128), lambda p, cb=cb: (0, cb + p))
    out = jax.ShapeDtypeStruct((T, AW), F32)
    return pl.pallas_call(
        body, name="attn_fwd", grid=(AW // 128,), in_specs=[spec(0), spec(0), spec(8)], out_specs=[spec(0)] * 2,
        out_shape=[out] * 2,
        scratch_shapes=[pltpu.VMEM((T, 128), BF16), pltpu.VMEM((T + BLK, 128), BF16), pltpu.VMEM((T + BLK, 128), BF16)]
        + [pltpu.VMEM((T, 128), F32)] * 6 + [pltpu.VMEM((2 * BLK, 2 * BLK), F32)] * 2,
        compiler_params=_cp("parallel"),
    )(q, k, proj)


def _attn_bwd(q, k, proj, do, lse, delta):
    nblk = T // BLK

    def body(q_ref, k_ref, v_ref, do_ref, l_ref, dl_ref, dq_ref, dk_ref, dv_ref, qs, dos, ks, vs, ls, dls, dks, dvs,
             sa0, sa1, da0, da1):
        band, cur_half = _stacked_band_mask()
        h0 = lax.broadcasted_iota(jnp.int32, (1, 128), 1) < HD
        ks[0:BLK, :] = jnp.zeros((BLK, 128), BF16)
        vs[0:BLK, :] = jnp.zeros((BLK, 128), BF16)
        for d in DILATIONS:
            nb = T // d // BLK
            _regroup_into(qs, q_ref, d, 0)
            _regroup_into(dos, do_ref, d, 0)
            _regroup_into(ks, k_ref, d, BLK)
            _regroup_into(vs, v_ref, d, BLK)
            _regroup_into(ls, l_ref, d, 0, cast=False)
            _regroup_into(dls, dl_ref, d, 0, cast=False)
            dks[...] = jnp.zeros_like(dks)
            dvs[...] = jnp.zeros_like(dvs)

            def scores(b, s_buf, dp_buf):
                r0 = pl.multiple_of(b * BLK, BLK)
                win = pl.ds(r0, 2 * BLK)
                s_buf[...] = _dot(_stack_heads(qs[pl.ds(r0, BLK), :], h0), ks[win, :], NT)
                dp_buf[...] = _dot(_stack_heads(dos[pl.ds(r0, BLK), :], h0), vs[win, :], NT)

            def finish(b, s_buf, dp_buf, d=d, nb=nb):
                r0 = pl.multiple_of(b * BLK, BLK)
                mask = band & (cur_half | ((b & (nb - 1)) > 0))
                win = pl.ds(r0, 2 * BLK)
                lv, dlv = ls[pl.ds(r0, BLK), :], dls[pl.ds(r0, BLK), :]
                lse2 = jnp.concatenate([lv[:, 0:1], lv[:, HD:HD + 1]], axis=0)
                dl2 = jnp.concatenate([dlv[:, 0:1], dlv[:, HD:HD + 1]], axis=0)
                p = jnp.exp(jnp.where(mask, s_buf[...], NEG) - lse2)
                ds = p * (dp_buf[...] - dl2)
                pb, dsb = p.astype(BF16), ds.astype(BF16)
                dq2 = _dot(dsb, ks[win, :])
                dks[win, :] += _dot(dsb, _stack_heads(qs[pl.ds(r0, BLK), :], h0), TN)
                dvs[win, :] += _dot(pb, _stack_heads(dos[pl.ds(r0, BLK), :], h0), TN)
                rows = _natural_rows(r0, BLK, d)
                dq = jnp.where(h0, dq2[0:BLK, :], dq2[BLK:, :])
                dq_ref[rows, :] = dq if d == 1 else dq_ref[rows, :] + dq

            scores(0, sa0, da0)

            def step(i, carry):
                b = 2 * i
                scores(b + 1, sa1, da1)
                finish(b, sa0, da0)
                scores(jnp.minimum(b + 2, nblk - 1), sa0, da0)
                finish(b + 1, sa1, da1)
                return carry

            lax.fori_loop(0, nblk // 2, step, 0)

            def back(j, carry, d=d):
                r0 = pl.multiple_of(j * RG, RG)
                rows = _natural_rows(r0, RG, d)
                src = pl.ds(BLK + r0, RG)
                dk_ref[rows, :] = dks[src, :] if d == 1 else dk_ref[rows, :] + dks[src, :]
                dv_ref[rows, :] = dvs[src, :] if d == 1 else dv_ref[rows, :] + dvs[src, :]
                return carry

            lax.fori_loop(0, T // RG, back, 0)

    spec = lambda cb: _bs((T, 128), lambda p, cb=cb: (0, cb + p))
    ospec = _bs((T, 128), lambda p: (0, p))
    out = jax.ShapeDtypeStruct((T, AW), F32)
    return pl.pallas_call(
        body, name="attn_bwd", grid=(AW // 128,), in_specs=[spec(0), spec(0), spec(8), spec(0), spec(0), spec(0)],
        out_specs=[ospec] * 3, out_shape=[out] * 3,
        scratch_shapes=[pltpu.VMEM((T, 128), BF16), pltpu.VMEM((T, 128), BF16), pltpu.VMEM((T + BLK, 128), BF16),
                        pltpu.VMEM((T + BLK, 128), BF16), pltpu.VMEM((T, 128), F32), pltpu.VMEM((T, 128), F32),
                        pltpu.VMEM((T + BLK, 128), F32), pltpu.VMEM((T + BLK, 128), F32)]
        + [pltpu.VMEM((2 * BLK, 2 * BLK), F32)] * 4,
        compiler_params=_cp("parallel"),
    )(q, k, proj, do, lse, delta)


def _attn_norm(attn, g_attn):
    tr = 512

    def body(a_ref, g_ref, mix_ref):
        attn = a_ref[...]
        r = lax.rsqrt(jnp.mean(attn * attn, axis=-1, keepdims=True) + EPS)
        mix_ref[...] = ((attn * r) * g_ref[...]).astype(BF16)

    row = _bs((tr, AW), lambda i: (i, 0))
    return pl.pallas_call(
        body, name="attn_norm", grid=(T // tr,), in_specs=[row, _bs((1, AW), lambda i: (0, 0))],
        out_specs=row, out_shape=jax.ShapeDtypeStruct((T, D), BF16), compiler_params=_cp("parallel"),
    )(attn, g_attn)


def _attn_out_bwd(attn, dmix, g_attn):
    tr = 512

    def body(a_ref, d_ref, g_ref, e_ref, do_ref, dl_ref, dg_ref):
        av, dyv = a_ref[...], d_ref[...]
        r = lax.rsqrt(jnp.mean(av * av, axis=-1, keepdims=True) + EPS)
        gdy = g_ref[...] * dyv
        da = r * gdy - av * ((r * r * r) * jnp.mean(av * gdy, axis=-1, keepdims=True))
        do_ref[...] = da
        dl_ref[...] = _segsum(da * av, e_ref[...])

        @pl.when(pl.program_id(0) == 0)
        def _():
            dg_ref[...] = jnp.zeros_like(dg_ref)

        dg_ref[...] += jnp.sum(dyv * (av * r), axis=0, keepdims=True)

    row = _bs((tr, AW), lambda i: (i, 0))
    vec = _bs((1, AW), lambda i: (0, 0))
    return pl.pallas_call(
        body, name="attn_out_bwd", grid=(T // tr,), in_specs=[row, row, vec, _bs((128, 128), lambda i: (0, 0))],
        out_specs=[row, row, vec],
        out_shape=[jax.ShapeDtypeStruct((T, AW), F32), jax.ShapeDtypeStruct((T, AW), F32),
                   jax.ShapeDtypeStruct((1, AW), F32)],
        compiler_params=_cp("arbitrary"),
    )(attn, dmix, g_attn, _head_ones())


TRR = 256


def _scan_fwd(a, u):
    n = a.shape[0]
    row = lax.broadcasted_iota(jnp.int32, (n, 1), 0)
    s = 1
    while s < n:
        keep = row >= s
        u = jnp.where(keep, a * pltpu.roll(u, s, 0) + u, u)
        a = jnp.where(keep, a * pltpu.roll(a, s, 0), a)
        s *= 2
    return a, u


def _scan_bwd(c, w):
    n = c.shape[0]
    row = lax.broadcasted_iota(jnp.int32, (n, 1), 0)
    s = 1
    while s < n:
        keep = row < n - s
        w = jnp.where(keep, c * pltpu.roll(w, n - s, 0) + w, w)
        c = jnp.where(keep, c * pltpu.roll(c, n - s, 0), c)
        s *= 2
    return w


def _gates(xc, wrg, wig, brg, big, sp):
    xcb = xc.astype(BF16)
    r = jax.nn.sigmoid(_dot(xcb, wrg) + brg)
    ig = jax.nn.sigmoid(_dot(xcb, wig) + big)
    la = (-LRU_C * r) * sp
    a = jnp.exp(la)
    mult = jnp.sqrt(-jnp.tanh(la) * (a * a + 1.0))
    return r, ig, a, mult


def _conv4(ext_ref, xr, cw_ref, cb_ref, n):
    y = cb_ref[...] + ext_ref[pl.ds(5, n), :] * cw_ref[0:1, :]
    y = y + ext_ref[pl.ds(6, n), :] * cw_ref[1:2, :]
    y = y + ext_ref[pl.ds(7, n), :] * cw_ref[2:3, :]
    return y + xr * cw_ref[3:4, :]


def _rec_fwd(proj, mix, cw, cb, wrg, wig, brg, big, lam, g_rec):
    n = TRR

    def body(xr_ref, gr_ref, cw_ref, cb_ref, wrg_ref, wig_ref, brg_ref, big_ref, lam_ref, g_ref, mix_in,
             mix_ref, h_ref, ext, hcar):
        del mix_in

        @pl.when(pl.program_id(0) == 0)
        def _():
            ext[0:8, :] = jnp.zeros((8, RW), F32)
            hcar[...] = jnp.zeros_like(hcar)

        xr = xr_ref[...]
        ext[8:, :] = xr
        xc = _conv4(ext, xr, cw_ref, cb_ref, n)
        ext[0:8, :] = xr[n - 8:, :]
        sp = _softplus(-lam_ref[...])
        _, ig, a, mult = _gates(xc, wrg_ref[...], wig_ref[...], brg_ref[...], big_ref[...], sp)
        a_s, u_s = _scan_fwd(a, mult * (ig * xc))
        h = u_s + a_s * hcar[7:8, :]
        h_ref[...] = h
        hcar[...] = h[n - 8:, :]
        pre = h * _gelu(gr_ref[...])
        r = lax.rsqrt(jnp.mean(pre * pre, axis=-1, keepdims=True) + EPS)
        mix_ref[...] = ((pre * r) * g_ref[...]).astype(BF16)

    vec = _bs((1, RW), lambda i: (0, 0))
    mat = _bs((RW, RW), lambda i: (0, 0))
    return pl.pallas_call(
        body, name="rec_fwd", grid=(T // n,),
        in_specs=[_bs((n, RW), lambda i: (i, 3)), _bs((n, RW), lambda i: (i, 4)), _bs((8, RW), lambda i: (0, 0)), vec,
                  mat, mat, vec, vec, vec, vec, pl.BlockSpec(memory_space=pl.ANY)],
        out_specs=[_bs((n, RW), lambda i: (i, 1)), _bs((n, RW), lambda i: (i, 0))],
        out_shape=[jax.ShapeDtypeStruct((T, D), BF16), jax.ShapeDtypeStruct((T, RW), F32)],
        scratch_shapes=[pltpu.VMEM((n + 8, RW), F32), pltpu.VMEM((8, RW), F32)],
        input_output_aliases={10: 0}, compiler_params=_cp("arbitrary"),
    )(proj, proj, cw, cb, wrg, wig, brg, big, lam, g_rec, mix)


def _rec_bwd(proj, h, dmix, cw, cb, wrg, wig, brg, big, lam, g_rec):
    n = TRR
    nt = T // n
    hb = n // 8

    def body(xr_ref, xh_ref, gr_ref, h_ref, hh_ref, dm_ref, cw_ref, cb_ref, wrg_ref, wig_ref, brg_ref, big_ref,
             lam_ref, g_ref, dp_ref, xc_ref, dr_ref, di_ref, dcw_ref, dcb_ref, dbr_ref, dbi_ref, dsp_ref,
             dg_ref, ext, exth, extd, adh):
        i, j = pl.program_id(0), pl.program_id(1)
        first_tile = i == nt - 1
        last_tile = i == 0

        @pl.when(j == 0)
        def _():
            @pl.when(last_tile)
            def _():
                for ref in (dcw_ref, dcb_ref, dbr_ref, dbi_ref, dsp_ref, dg_ref):
                    ref[...] = jnp.zeros_like(ref)
                extd[n:, :] = jnp.zeros((8, RW), F32)
                adh[...] = jnp.zeros_like(adh)

            row = lax.broadcasted_iota(jnp.int32, (n, 1), 0)
            xr = xr_ref[...]
            ext[0:8, :] = jnp.where(first_tile, 0.0, xh_ref[...])
            ext[8:, :] = xr
            xc = _conv4(ext, xr, cw_ref, cb_ref, n)
            sp = _softplus(-lam_ref[...])
            wrg, wig = wrg_ref[...], wig_ref[...]
            r, ig, a, mult = _gates(xc, wrg, wig, brg_ref[...], big_ref[...], sp)

            hv = h_ref[...]
            gl, dgl = _gelu_and_grad(gr_ref[...])
            pre = hv * gl
            dyv = dm_ref[...]
            rr = lax.rsqrt(jnp.mean(pre * pre, axis=-1, keepdims=True) + EPS)
            gdy = g_ref[...] * dyv
            dpre = rr * gdy - pre * ((rr * rr * rr) * jnp.mean(pre * gdy, axis=-1, keepdims=True))
            dg_ref[...] += jnp.sum(dyv * (pre * rr), axis=0, keepdims=True)
            dp_ref[:, RW:] = (dpre * hv * dgl).astype(BF16)

            is_last_row = row == n - 1
            w = dpre * gl + jnp.where(is_last_row, adh[0:1, :], 0.0)
            c = jnp.where(is_last_row, 0.0, pltpu.roll(a, n - 1, 0))
            dh = _scan_bwd(c, w)
            adh[...] = (a * dh)[0:8, :]

            exth[0:8, :] = jnp.where(first_tile, 0.0, hh_ref[...])
            exth[8:, :] = hv
            da = dh * exth[pl.ds(7, n), :]
            ixc = ig * xc
            dmult = dh * ixc
            dla = da * a - dmult * ((a * a) / mult)
            dsp_ref[...] += jnp.sum(dla * (-LRU_C * r), axis=0, keepdims=True)
            dpr = (dla * (-LRU_C * sp)) * (r * (1.0 - r))
            dpi = (dh * (mult * xc)) * (ig * (1.0 - ig))
            dprb, dpib = dpr.astype(BF16), dpi.astype(BF16)
            dxc = dh * (mult * ig) + _dot(dprb, wrg, NT) + _dot(dpib, wig, NT)
            dbr_ref[...] += jnp.sum(dpr, axis=0, keepdims=True)
            dbi_ref[...] += jnp.sum(dpi, axis=0, keepdims=True)
            xc_ref[...] = xc.astype(BF16)
            dr_ref[...] = dprb
            di_ref[...] = dpib

            extd[0:n, :] = dxc
            dxr = dxc * cw_ref[3:4, :] + extd[pl.ds(1, n), :] * cw_ref[2:3, :]
            dxr = dxr + extd[pl.ds(2, n), :] * cw_ref[1:2, :] + extd[pl.ds(3, n), :] * cw_ref[0:1, :]
            extd[n:, :] = dxc[0:8, :]
            dcb_ref[...] += jnp.sum(dxc, axis=0, keepdims=True)
            for kk in range(4):
                dcw_ref[kk:kk + 1, :] += jnp.sum(dxc * ext[pl.ds(5 + kk, n), :], axis=0, keepdims=True)

            @pl.when(first_tile)
            def _():
                dsp_ref[...] = dsp_ref[...] * (-jax.nn.sigmoid(-lam_ref[...]))

            dp_ref[:, 0:RW] = dxr.astype(BF16)

    vec = _bs((1, RW), lambda i, j: (0, 0))
    mat = _bs((RW, RW), lambda i, j: (0, 0))
    tile = lambda cblk: _bs((n, RW), lambda i, j, cblk=cblk: (nt - 1 - i, cblk))
    halo = lambda cblk: _bs((8, RW), lambda i, j, cblk=cblk: (jnp.maximum((nt - 1 - i) * hb - 1, 0), cblk))
    bt = jax.ShapeDtypeStruct((T, RW), BF16)
    v = jax.ShapeDtypeStruct((1, RW), F32)
    return pl.pallas_call(
        body, name="rec_bwd", grid=(nt, 1),
        in_specs=[tile(3), halo(3), tile(4), tile(0), halo(0), tile(1), _bs((8, RW), lambda i, j: (0, 0)), vec,
                  mat, mat, vec, vec, vec, vec],
        out_specs=[_bs((n, 2 * RW), lambda i, j: (nt - 1 - i, 0)), tile(0), tile(0), tile(0),
                   _bs((8, RW), lambda i, j: (0, 0)), vec, vec, vec, vec, vec],
        out_shape=[jax.ShapeDtypeStruct((T, 2 * RW), BF16), bt, bt, bt, jax.ShapeDtypeStruct((8, RW), F32),
                   v, v, v, v, v],
        scratch_shapes=[pltpu.VMEM((n + 8, RW), F32), pltpu.VMEM((n + 8, RW), F32), pltpu.VMEM((n + 8, RW), F32),
                        pltpu.VMEM((8, RW), F32)],
        compiler_params=_cp("arbitrary", "arbitrary"),
    )(proj, proj, proj, h, h, dmix, cw, cb, wrg, wig, brg, big, lam, g_rec)


FC = 1536
TRF = 512


LC = 128


class _RowsBack:
    def __init__(self, before):
        row = lax.broadcasted_iota(jnp.int32, before.shape, 0)
        self.top1, self.top2 = row < 1, row < 2
        self.r1, self.r2 = pltpu.roll(before, 1, 0), pltpu.roll(before, 2, 0)

    def step(self, cur):
        r1, r2 = pltpu.roll(cur, 1, 0), pltpu.roll(cur, 2, 0)
        out = jnp.where(self.top1, self.r1, r1), jnp.where(self.top2, self.r2, r2)
        self.r1, self.r2 = r1, r2
        return out


def _up_act(h2, w_up, cw, cb):
    n = TRF
    nt = T // n
    pw = 256
    npc = FC // pw

    def body(h_ref, wg_ref, wu_ref, cwg_ref, cwu_ref, bg_ref, bu_ref, up_ref, a_ref, fa_ref, fb_ref, hx, gb0, gb1,
             ub0, ub1):
        i = pl.program_id(1)
        halo = h_ref[pl.ds(pl.multiple_of(jnp.maximum(i * n - 16, 0), 16), 16), :]
        hx[0:16, :] = jnp.where(i == 0, jnp.zeros_like(halo), halo)
        hx[16:, :] = h_ref[pl.ds(pl.multiple_of(i * n, n), n), :]
        gbufs, ubufs = (gb0, gb1), (ub0, ub1)

        def dots(c):
            hv = hx[...]
            gbufs[c % 2][...] = _dot(hv, wg_ref[:, c * pw:(c + 1) * pw])
            ubufs[c % 2][...] = _dot(hv, wu_ref[:, c * pw:(c + 1) * pw])

        def chain(c):
            gb, ub = gbufs[c % 2], ubufs[c % 2]
            up_ref[:, c * pw:(c + 1) * pw] = gb[16:, :]
            up_ref[:, FC + c * pw:FC + (c + 1) * pw] = ub[16:, :]
            rows8 = lambda v: jnp.broadcast_to(v, (8, LC))
            for sub in range(pw // LC):
                lc = slice(sub * LC, (sub + 1) * LC)
                cols = slice(c * pw + sub * LC, c * pw + (sub + 1) * LC)
                wg = [rows8(cwg_ref[kk:kk + 1, cols]) for kk in range(3)]
                wu = [rows8(cwu_ref[kk:kk + 1, cols]) for kk in range(3)]
                bg, bu = rows8(bg_ref[:, cols]), rows8(bu_ref[:, cols])
                g_back = _RowsBack(gb[pl.ds(8, 8), lc])
                u_back = _RowsBack(ub[pl.ds(8, 8), lc])
                for r in range(0, n, 16):
                    res, fa, fb = [], [], []
                    for rr in (16 + r, 24 + r):
                        g0, u0 = gb[pl.ds(rr, 8), lc], ub[pl.ds(rr, 8), lc]
                        g1, g2 = g_back.step(g0)
                        u1, u2 = u_back.step(u0)
                        ug = ((bg + g2 * wg[0]) + g1 * wg[1]) + g0 * wg[2]
                        uu = ((bu + u2 * wu[0]) + u1 * wu[1]) + u0 * wu[2]
                        gl, dgl = _gelu_and_grad(ug)
                        res.append(gl * uu)
                        fa.append(uu * dgl)
                        fb.append(gl)
                    a_ref[pl.ds(r, 16), cols] = jnp.concatenate(res, axis=0).astype(BF16)
                    fa_ref[pl.ds(r, 16), cols] = jnp.concatenate(fa, axis=0).astype(BF16)
                    fb_ref[pl.ds(r, 16), cols] = jnp.concatenate(fb, axis=0).astype(BF16)

        dots(0)
        for c in range(npc):
            if c + 1 < npc:
                dots(c + 1)
            chain(c)

    wsl = lambda o: _bs((None, D, FC), lambda j, i, o=o: (2 * j + o, 0, 0))
    wsp = lambda o: _bs((None, 8, FC), lambda j, i, o=o: (2 * j + o, 0, 0))
    bsp = lambda o: _bs((1, FC), lambda j, i, o=o: (0, 2 * j + o))
    return pl.pallas_call(
        body, name="up_act", grid=(2, nt),
        in_specs=[pl.BlockSpec((T, D), lambda j, i: (0, 0), pipeline_mode=pl.Buffered(1)), wsl(0), wsl(1),
                  wsp(0), wsp(1), bsp(0), bsp(1)],
        out_specs=[_bs((n, 2 * FC), lambda j, i: (i, j))] + [_bs((n, FC), lambda j, i: (i, j))] * 3,
        out_shape=[jax.ShapeDtypeStruct((T, 2 * DFF), F32)] + [jax.ShapeDtypeStruct((T, DFF), BF16)] * 3,
        scratch_shapes=[pltpu.VMEM((n + 16, D), BF16)] + [pltpu.VMEM((n + 16, pw), F32)] * 4,
        compiler_params=_cp("parallel", "arbitrary"),
    )(h2, w_up, w_up, cw, cw, cb, cb)


def _ffn_bwd(up_pre, fa, fb, dyb, w_down_t, cw, after=()):
    n = TRF
    hb = n // 8
    nt = T // n
    m = n + 8
    pw = 256
    npc = FC // pw

    def body(g_ref, gp_ref, u_ref, up_ref, fa_ref, fan_ref, fb_ref, fbn_ref, dy_ref, wd_ref, wg_ref, wu_ref, *rest):
        o_ref, dw_ref, db_ref, eg0, eu0, dug_s, duu_s, dyx, db0, db1 = rest[len(after):]
        i = pl.program_id(1)
        first, last = i == 0, i == nt - 1

        @pl.when(first)
        def _():
            dw_ref[...] = jnp.zeros_like(dw_ref)
            db_ref[...] = jnp.zeros_like(db_ref)

        tail = dy_ref[pl.ds(pl.multiple_of(jnp.minimum((i + 1) * n, T - 16), 16), 16), :]
        dyx[0:n, :] = dy_ref[pl.ds(pl.multiple_of(i * n, n), n), :]
        dyx[n:, :] = jnp.where(last, jnp.zeros_like(tail), tail)
        dbufs = (db0, db1)

        def dots(c):
            dbufs[c % 2][...] = _dot(dyx[...], wd_ref[:, c * pw:(c + 1) * pw])

        eg0[0:8, :] = jnp.where(first, 0.0, gp_ref[...])
        eg0[8:, :] = g_ref[0:8, :]
        eu0[0:8, :] = jnp.where(first, 0.0, up_ref[...])
        eu0[8:, :] = u_ref[0:8, :]

        def column(ci, dbuf, lc):
            cols = slice(ci * LC, (ci + 1) * LC)
            ucols = slice(FC + ci * LC, FC + (ci + 1) * LC)
            rows8 = lambda v: jnp.broadcast_to(v, (8, LC))
            wg = [rows8(wg_ref[kk:kk + 1, cols]) for kk in range(3)]
            wu = [rows8(wu_ref[kk:kk + 1, cols]) for kk in range(3)]
            zero = jnp.zeros((8, LC), F32)
            acc = [zero] * 8
            g_back, u_back = _RowsBack(eg0[pl.ds(0, 8), cols]), _RowsBack(eu0[pl.ds(0, 8), cols])
            for r in range(0, n + 16, 16):
                src_a, src_b, r16 = (fan_ref, fbn_ref, 0) if r == n else (fa_ref, fb_ref, r)
                fa16 = src_a[pl.ds(r16, 16), cols].astype(F32)
                fb16 = src_b[pl.ds(r16, 16), cols].astype(F32)
                for half in range(1 if r == n else 2):
                    rr = r + 8 * half
                    dv = dbuf[pl.ds(rr, 8), lc]
                    dug = dv * fa16[8 * half:8 * half + 8, :]
                    duu = dv * fb16[8 * half:8 * half + 8, :]
                    dug_s[pl.ds(rr, 8), :] = dug
                    duu_s[pl.ds(rr, 8), :] = duu
                    if rr < n:
                        g0, u0 = g_ref[pl.ds(rr, 8), cols], u_ref[pl.ds(rr, 8), cols]
                        g1, g2 = g_back.step(g0)
                        u1, u2 = u_back.step(u0)
                        gt, ut = (g2, g1, g0), (u2, u1, u0)
                        acc = [acc[0] + dug * gt[0], acc[1] + dug * gt[1], acc[2] + dug * gt[2],
                               acc[3] + duu * ut[0], acc[4] + duu * ut[1], acc[5] + duu * ut[2],
                               acc[6] + dug, acc[7] + duu]
            for r in range(0, n, 16):
                og, ou = [], []
                for rr in (r, r + 8):
                    og.append((dug_s[pl.ds(rr, 8), :] * wg[2] + dug_s[pl.ds(rr + 1, 8), :] * wg[1])
                              + dug_s[pl.ds(rr + 2, 8), :] * wg[0])
                    ou.append((duu_s[pl.ds(rr, 8), :] * wu[2] + duu_s[pl.ds(rr + 1, 8), :] * wu[1])
                              + duu_s[pl.ds(rr + 2, 8), :] * wu[0])
                o_ref[pl.ds(r, 16), cols] = jnp.concatenate(og, axis=0).astype(BF16)
                o_ref[pl.ds(r, 16), ucols] = jnp.concatenate(ou, axis=0).astype(BF16)
            for kk in range(3):
                dw_ref[kk:kk + 1, cols] += jnp.sum(acc[kk], axis=0, keepdims=True)
                dw_ref[kk:kk + 1, ucols] += jnp.sum(acc[3 + kk], axis=0, keepdims=True)
            db_ref[:, cols] += jnp.sum(acc[6], axis=0, keepdims=True)
            db_ref[:, ucols] += jnp.sum(acc[7], axis=0, keepdims=True)

        dots(0)
        for c in range(npc):
            if c + 1 < npc:
                dots(c + 1)
            for sub in range(pw // LC):
                column(c * (pw // LC) + sub, dbufs[c % 2], slice(sub * LC, (sub + 1) * LC))

    main = lambda o: _bs((n, FC), lambda j, i, o=o: (i, 2 * j + o))
    prev = lambda o: _bs((8, FC), lambda j, i, o=o: (jnp.maximum(i * hb - 1, 0), 2 * j + o))
    saved = _bs((n, FC), lambda j, i: (i, j))
    saved_next = _bs((16, FC), lambda j, i: (jnp.minimum((i + 1) * (n // 16), T // 16 - 1), j))
    wsp = lambda o: _bs((None, 8, FC), lambda j, i, o=o: (2 * j + o, 0, 0))
    return pl.pallas_call(
        body, name="ffn_bwd", grid=(2, nt),
        in_specs=[main(0), prev(0), main(1), prev(1), saved, saved_next, saved, saved_next,
                  pl.BlockSpec((T, D), lambda j, i: (0, 0), pipeline_mode=pl.Buffered(1)),
                  _bs((D, FC), lambda j, i: (0, j)), wsp(0), wsp(1)]
        + [pl.BlockSpec(memory_space=pl.ANY)] * len(after),
        out_specs=[_bs((n, 2 * FC), lambda j, i: (i, j)), _bs((8, 2 * FC), lambda j, i: (0, j)),
                   _bs((1, 2 * FC), lambda j, i: (0, j))],
        out_shape=[jax.ShapeDtypeStruct((T, 2 * DFF), BF16), jax.ShapeDtypeStruct((8, 2 * DFF), F32),
                   jax.ShapeDtypeStruct((1, 2 * DFF), F32)],
        scratch_shapes=[pltpu.VMEM((16, FC), F32)] * 2 + [pltpu.VMEM((m, LC), F32)] * 2
        + [pltpu.VMEM((n + 16, D), BF16)] + [pltpu.VMEM((n + 16, pw), F32)] * 2,
        compiler_params=_cp("parallel", "arbitrary"),
    )(up_pre, up_pre, up_pre, up_pre, fa, fa, fb, fb, dyb, w_down_t, cw, cw, *after)


def _down_loss(act, w_down, x1, target):
    tm, tn = 512, D

    def body(a_ref, b_ref, r_ref, t_ref, dy_ref, dyb_ref, l_ref):
        @pl.when((pl.program_id(0) == 0) & (pl.program_id(1) == 0))
        def _():
            l_ref[...] = jnp.zeros_like(l_ref)

        err = (r_ref[...] + _dot(a_ref[...], b_ref[...])) - t_ref[...]
        dy = err * (1.0 / D)
        dy_ref[...] = dy
        dyb_ref[...] = dy.astype(BF16)
        l_ref[...] += jnp.sum(0.5 * (err * err) * (1.0 / D))

    o_spec = _bs((tm, tn), lambda j, i: (i, j))
    return pl.pallas_call(
        body, name="down_loss", grid=(D // tn, T // tm),
        in_specs=[_bs((tm, DFF), lambda j, i: (i, 0)),
                  pl.BlockSpec((DFF, tn), lambda j, i: (0, j), pipeline_mode=pl.Buffered(1)), o_spec, o_spec],
        out_specs=[o_spec, o_spec, _bs((8, 128), lambda j, i: (0, 0))],
        out_shape=[jax.ShapeDtypeStruct((T, D), F32), jax.ShapeDtypeStruct((T, D), BF16),
                   jax.ShapeDtypeStruct((8, 128), F32)],
        compiler_params=_cp("arbitrary", "arbitrary"),
    )(act, w_down, x1, target)


def _block_diag(w):
    eye = jnp.eye(8, dtype=w.dtype)
    return (w[:, :, None, :] * eye[:, None, :, None]).reshape(RW, RW).astype(BF16)


def _diag_blocks(m):
    eye = jnp.eye(8, dtype=m.dtype)
    return (m.reshape(8, HD, 8, HD) * eye[:, None, :, None]).sum(axis=2)


def _local_step(x, pos_col, target, p, exch):
    qg, kg = jnp.tile(p["q_norm_g"], (1, 8)), jnp.tile(p["k_norm_g"], (1, 8))
    wrg, wig = _block_diag(p["w_rg"]), _block_diag(p["w_ig"])
    brg, big = p["b_rg"].reshape(1, RW), p["b_ig"].reshape(1, RW)

    h1 = _rms_fwd("rms1", x, p["g_mix"])
    p = {**p, **exch.wait_first(h1)}
    proj = _mm("mm_in", h1, p["w_in"], "nn", 1024, 640, stack=NCHIP, after=exch.start_rest(), a_full=True)
    q, k, cos_t, sin_t = _qk_prep(proj, pos_col, qg, kg)
    attn, lse = _attn_fwd(q, k, proj)
    mix = _attn_norm(attn, p["g_attn_out"])
    mix, hseq = _rec_fwd(proj, mix, p["rec_conv_w"], p["rec_conv_b"], wrg, wig, brg, big, p["lru_lambda"], p["g_rec_out"])
    rest = exch.wait_rest(mix)
    x1 = _mm("mm_out", mix, rest["w_out"], "nn", 1024, 512, res=x, a_full=True)
    h2 = _rms_fwd("rms2", x1, p["g_ffn"])
    up_pre, act, fa, fb = _up_act(h2, rest["w_up"], p["ffn_conv_w"], p["ffn_conv_b"])
    dy, dyb, loss_blk = _down_loss(act, rest["w_down"], x1, target)

    g = {}
    tok = exch.reduce_start("w_down", *_mm("wg_down", act, dyb, "tn", 512, 512, twin_bf16=True))
    dup, g["ffn_conv_w"], g["ffn_conv_b"] = _ffn_bwd(up_pre, fa, fb, dyb, rest["w_down"].T, p["ffn_conv_w"], tok)
    tok = exch.reduce_start("w_up", *_mm("wg_up", h2, dup, "tn", 512, 768, stack=NCHIP, twin_bf16=True, a_full=True))
    dh2 = _mm("dg_up", dup, rest["w_up"], "nt", 512, D, stack=NCHIP, after=tok, b_full=True)
    dx1, dx1b, g["g_ffn"] = _rms_bwd("rms2_bwd", x1, p["g_ffn"], dh2, dy, True)
    tok = exch.reduce_start("w_out", *_mm("wg_out", mix, dx1b, "tn", 512, 512, twin_bf16=True, a_full=True))
    dmix = _mm("dg_out", dx1b, rest["w_out"], "nt", 1024, 512, after=tok, a_full=True)
    do, delta, g["g_attn_out"] = _attn_out_bwd(attn, dmix, p["g_attn_out"])
    dq, dk, dv = _attn_bwd(q, k, proj, do, lse, delta)
    dqkv, dqg, dkg = _qk_bwd(proj, cos_t, sin_t, qg, kg, dq, dk, dv)
    (drec, xcb, dprb, dpib, g["rec_conv_w"], g["rec_conv_b"], dbr, dbi, dsp, g["g_rec_out"]) = _rec_bwd(
        proj, hseq, dmix, p["rec_conv_w"], p["rec_conv_b"], wrg, wig, brg, big, p["lru_lambda"], p["g_rec_out"])
    dproj = jnp.concatenate([dqkv, drec], axis=1)
    g["w_rg"] = _diag_blocks(_mm("wg_rg", xcb, dprb, "tn", 512, 512)).reshape(RW, HD)
    g["w_ig"] = _diag_blocks(_mm("wg_ig", xcb, dpib, "tn", 512, 512)).reshape(RW, HD)
    g["b_rg"], g["b_ig"] = dbr.reshape(8, HD), dbi.reshape(8, HD)
    g["lru_lambda"] = dsp
    g["q_norm_g"] = dqg.reshape(8, HD).sum(axis=0, keepdims=True)
    g["k_norm_g"] = dkg.reshape(8, HD).sum(axis=0, keepdims=True)
    tok = exch.reduce_start("w_in", *_mm("wg_in", h1, dproj, "tn", 512, 640, stack=NCHIP, twin_bf16=True, a_full=True))
    dh1 = _mm("dg_in", dproj, p["w_in"], "nt", 1024, 512, stack=NCHIP, after=tok)
    grad_x, g["g_mix"] = _rms_bwd("rms1_bwd", x, p["g_mix"], dh1, dx1, False)
    return loss_blk, grad_x, g


ANY = pl.BlockSpec(memory_space=pl.ANY)


def _mesh_pos():
    return lax.axis_index("x"), lax.axis_index("y"), lax.axis_index("c")


def _slot(px, py, perm):
    return 2 * py + px if perm else 2 * px + py


def _other_chips(x, y):
    return [(1 - x, y), (x, 1 - y), (1 - x, 1 - y)]


def _rcopy(src, dst, send, recv, k, to, kr=None):
    return pltpu.make_async_remote_copy(src_ref=src, dst_ref=dst, send_sem=send.at[k],
                                        recv_sem=recv.at[k if kr is None else kr], device_id=to, device_id_type=MESH)


def _cast_bf16(name, w, after=()):
    r, c = w.shape
    tr = 256

    def body(w_ref, *rest):
        rest[-1][...] = w_ref[...].astype(BF16)

    return pl.pallas_call(
        body, name=name, grid=(r // tr,), in_specs=[_bs((tr, c), lambda i: (i, 0))] + [ANY] * len(after),
        out_specs=_bs((tr, c), lambda i: (i, 0)), out_shape=jax.ShapeDtypeStruct((r, c), BF16),
        compiler_params=_cp("parallel"),
    )(w, *after)


def _sibling_fill(lands, perms):
    na = len(lands)

    def body(*refs):
        outs, (send, recv) = refs[na:2 * na], refs[2 * na:]
        x, y, c = _mesh_pos()
        cps = []
        for a in range(na):
            for j, (px, py) in enumerate(_other_chips(x, y)):
                mine = outs[a].at[_slot(px, py, perms[a]), c]
                cps.append(_rcopy(mine, mine, send, recv, 3 * a + j, (x, y, 1 - c)))
        for cp in cps:
            cp.start()
        for a in range(na):
            for j, (px, py) in enumerate(_other_chips(x, y)):
                got = outs[a].at[_slot(px, py, perms[a]), 1 - c]
                _rcopy(got, got, send, recv, 3 * a + j, (x, y, c)).wait_recv()
        for cp in cps:
            cp.wait_send()

    return pl.pallas_call(
        body, name="gather_fill", in_specs=[ANY] * na, out_specs=[ANY] * na,
        out_shape=[jax.ShapeDtypeStruct(a.shape, a.dtype) for a in lands],
        input_output_aliases={i: i for i in range(na)},
        scratch_shapes=[pltpu.SemaphoreType.DMA((3 * na,)), pltpu.SemaphoreType.DMA((3 * na,))],
    )(*lands)


HBM = pl.BlockSpec(memory_space=pltpu.HBM)
SEM = pl.BlockSpec(memory_space=pltpu.SEMAPHORE)
EFFECT = pltpu.SideEffectType.DATAFLOW_SIDE_EFFECTING


def _split_start(name, srcs, lands, plan, nsem):
    ns, nl = len(srcs), len(lands)

    def body(*refs):
        send, recv = refs[ns + nl], refs[ns + nl + 1]
        sends, _ = plan(refs[:ns], refs[ns:ns + nl], send, recv)
        for cp in sends:
            cp.start()
        refs[-1][...] = jnp.zeros((8, 128), F32)

    arrs = list(srcs) + list(lands)
    out = pl.pallas_call(
        body, name=name, in_specs=[HBM] * (ns + nl),
        out_specs=[SEM, SEM] + [HBM] * (ns + nl) + [pl.BlockSpec(memory_space=pltpu.VMEM)],
        out_shape=[pltpu.SemaphoreType.DMA((nsem,)), pltpu.SemaphoreType.DMA((nsem,))]
        + [pltpu.HBM(a.shape, a.dtype) for a in arrs] + [jax.ShapeDtypeStruct((8, 128), F32)],
        input_output_aliases={i: 2 + i for i in range(ns + nl)},
        compiler_params=pltpu.CompilerParams(has_side_effects=EFFECT),
    )(*[pltpu.with_memory_space_constraint(a, pltpu.HBM) for a in arrs])
    return out[0], out[1], out[2:2 + ns], out[2 + ns:2 + ns + nl], out[-1]


def _split_wait(name, send, recv, srcs, lands, plan, after):
    ns, nl = len(srcs), len(lands)

    def body(*refs):
        sends, recvs = plan(refs[:ns], refs[ns:ns + nl], refs[ns + nl], refs[ns + nl + 1])
        for cp in sends:
            cp.wait_send()
        for cp in recvs:
            cp.wait_recv()

    arrs = list(srcs) + list(lands)
    after = tuple(after) if isinstance(after, (tuple, list)) else (after,)
    out = pl.pallas_call(
        body, name=name, in_specs=[HBM] * (ns + nl) + [SEM, SEM] + [ANY] * len(after), out_specs=[HBM] * (ns + nl),
        out_shape=[pltpu.HBM(a.shape, a.dtype) for a in arrs],
        input_output_aliases={i: i for i in range(ns + nl)},
        compiler_params=pltpu.CompilerParams(has_side_effects=EFFECT),
    )(*arrs, send, recv, *after)
    return out[ns:]


def _gather_plan(perms):
    def plan(srcs, lands, send, recv):
        x, y, c = _mesh_pos()
        sends, recvs = [], []
        for a, perm in enumerate(perms):
            for j, (px, py) in enumerate(_other_chips(x, y)):
                for cc in (0, 1):
                    k = 6 * a + 2 * j + cc
                    sends.append(_rcopy(srcs[a].at[c], lands[a].at[_slot(x, y, perm), c], send, recv, k, (px, py, cc),
                                        kr=6 * a + 2 * j + c))
                    got = lands[a].at[_slot(px, py, perm), cc]
                    recvs.append(_rcopy(got, got, send, recv, k, (x, y, c)))
        return sends, recvs
    return plan


def _gather_half_plan(perms, halved):
    def plan(srcs, lands, send, recv):
        x, y, c = _mesh_pos()
        sends, recvs = [], []
        for a, perm in enumerate(perms):
            for j, (px, py) in enumerate(_other_chips(x, y)):
                k = 3 * a + j
                mine, theirs = _slot(x, y, perm), _slot(px, py, perm)
                if halved[a]:
                    sends.append(_rcopy(srcs[a].at[c], lands[a].at[mine, c], send, recv, k, (px, py, c)))
                    got = lands[a].at[theirs, c]
                else:
                    sends.append(_rcopy(srcs[a], lands[a].at[mine], send, recv, k, (px, py, c)))
                    got = lands[a].at[theirs]
                recvs.append(_rcopy(got, got, send, recv, k, (x, y, c)))
        return sends, recvs
    return plan


def _reduce_plan(perm):
    def plan(srcs, lands, send, recv):
        x, y, c = _mesh_pos()
        src, land = srcs[0], lands[0]
        sends = []
        for j, (px, py) in enumerate(_other_chips(x, y)):
            for hf in (0, 1):
                sends.append(_rcopy(src.at[_slot(px, py, perm), hf], land.at[2 * j + c], send, recv, 2 * j + hf,
                                    (px, py, hf), kr=2 * j + c))
        sends.append(_rcopy(src.at[_slot(x, y, perm), 1 - c], land.at[6], send, recv, 6, (x, y, 1 - c)))
        recvs = [_rcopy(land.at[i], land.at[i], send, recv, i, (x, y, c)) for i in range(7)]
        return sends, recvs
    return plan


def _sibling_share(rs):
    na = len(rs)

    def body(*refs):
        ins, outs, (send, recv) = refs[:na], refs[na:2 * na], refs[2 * na:]
        x, y, c = _mesh_pos()
        cps = [_rcopy(ins[a], outs[a], send, recv, a, (x, y, 1 - c)) for a in range(na)]
        for cp in cps:
            cp.start()
        for cp in cps:
            cp.wait()

    return pl.pallas_call(
        body, name="rs_share", in_specs=[ANY] * na, out_specs=[ANY] * na,
        out_shape=[jax.ShapeDtypeStruct(r.shape, F32) for r in rs],
        scratch_shapes=[pltpu.SemaphoreType.DMA((na,)), pltpu.SemaphoreType.DMA((na,))],
    )(*rs)


def _add_pieces(name, g, got, where):
    _, _, r2, cc = g.shape
    tr = 256 if r2 % 256 == 0 else 128

    def body(w_ref, g_ref, r_ref, o_ref):
        del w_ref
        acc = g_ref[...]
        for i in range(7):
            acc = acc + r_ref[i].astype(F32)
        o_ref[...] = acc

    return pl.pallas_call(
        body, name=name,
        grid_spec=pltpu.PrefetchScalarGridSpec(
            num_scalar_prefetch=1, grid=(r2 // tr,),
            in_specs=[_bs((None, None, tr, cc), lambda i, w_ref: (w_ref[0], w_ref[1], i, 0)),
                      _bs((7, tr, cc), lambda i, w_ref: (0, i, 0))],
            out_specs=_bs((tr, cc), lambda i, w_ref: (i, 0))),
        out_shape=jax.ShapeDtypeStruct((r2, cc), F32), compiler_params=_cp("parallel"),
    )(where, g, got)


def _adam_math(w, g, m, v):
    m = ADAM_B1 * m + (1.0 - ADAM_B1) * g
    v = ADAM_B2 * v + (1.0 - ADAM_B2) * (g * g)
    m_hat = m / (1.0 - ADAM_B1 ** ADAM_STEP)
    v_hat = v / (1.0 - ADAM_B2 ** ADAM_STEP)
    return -ADAM_LR * (m_hat / (jnp.sqrt(v_hat) + ADAM_EPS) + ADAM_WD * w), m, v


def _adam_big(name, w, g_mine, g_sib, m, v, c_arr):
    r, cols = w.shape
    tr = 256 if (r // 2) % 256 == 0 else 128
    per = r // 2 // tr

    def body(c_ref, w_ref, a_ref, b_ref, m_ref, v_ref, g_ref, d_ref, m2_ref, v2_ref):
        g = jnp.where(pl.program_id(0) == c_ref[0], a_ref[...], b_ref[...])
        g_ref[...] = g
        d_ref[...], m2_ref[...], v2_ref[...] = _adam_math(w_ref[...], g, m_ref[...], v_ref[...])

    spec = _bs((tr, cols), lambda h, i, c_ref: (h * per + i, 0))
    half = _bs((tr, cols), lambda h, i, c_ref: (i, 0))
    out = jax.ShapeDtypeStruct((r, cols), F32)
    return pl.pallas_call(
        body, name=name,
        grid_spec=pltpu.PrefetchScalarGridSpec(
            num_scalar_prefetch=1, grid=(2, per), in_specs=[spec, half, half, spec, spec], out_specs=[spec] * 4),
        out_shape=[out] * 4, compiler_params=_cp("parallel", "parallel"),
    )(c_arr, w, g_mine, g_sib, m, v)


_CLASS_SHAPE = {"a": (8, D), "b": (8, RW), "c": (8, 2 * DFF), "d": (1048, HD)}
_SMALL = (
    ("g_mix", "a", 0, 1, D), ("g_ffn", "a", 1, 1, D),
    ("rec_conv_w", "b", 0, 4, RW), ("rec_conv_b", "b", 4, 1, RW), ("lru_lambda", "b", 5, 1, RW),
    ("g_attn_out", "b", 6, 1, RW), ("g_rec_out", "b", 7, 1, RW),
    ("ffn_conv_w", "c", 0, 3, 2 * DFF), ("ffn_conv_b", "c", 3, 1, 2 * DFF),
    ("w_rg", "d", 0, RW, HD), ("w_ig", "d", RW, RW, HD), ("b_rg", "d", 2 * RW, 8, HD), ("b_ig", "d", 2 * RW + 8, 8, HD),
    ("q_norm_g", "d", 2 * RW + 16, 1, HD), ("k_norm_g", "d", 2 * RW + 17, 1, HD),
)
_LOSS_ROW = 2
_CLASSES = ("a", "b", "c", "d")
_CLASS_OWNER = {"a": 0, "b": 0, "c": 0, "d": 1}


def _small_allreduce(g, loss_blk):
    names = [s[0] for s in _SMALL]
    nin = len(names) + 1

    def body(*refs):
        ins = dict(zip(names, refs[:len(names)]))
        loss_ref = refs[len(names)]
        outs = dict(zip(_CLASSES, refs[nin:nin + 4]))
        pair = dict(zip(_CLASSES, refs[nin + 4:nin + 8]))
        quad = dict(zip(_CLASSES, refs[nin + 8:nin + 12]))
        send, recv = refs[nin + 12:]
        x, y, c = _mesh_pos()
        chip = 2 * x + y
        pair["a"][c] = jnp.zeros(_CLASS_SHAPE["a"], F32)
        pair["b"][c] = ins["rec_conv_w"][...]
        pair["c"][c] = ins["ffn_conv_w"][...]
        pair["d"][c, 2 * RW + 16:, :] = jnp.zeros((8, HD), F32)
        for name, k, r0, nr, _ in _SMALL:
            if name in ("rec_conv_w", "ffn_conv_w"):
                continue
            pair[k][c, r0:r0 + nr, :] = ins[name][...]
        pair["a"][c, _LOSS_ROW:_LOSS_ROW + 1, :] = jnp.broadcast_to(loss_ref[0:1, 0:1], (1, D))
        cps = [_rcopy(pair[k].at[c], pair[k].at[c], send, recv, ki, (x, y, 1 - c)) for ki, k in enumerate(_CLASSES)]
        for cp in cps:
            cp.start()
        for ki, k in enumerate(_CLASSES):
            _rcopy(pair[k].at[1 - c], pair[k].at[1 - c], send, recv, ki, (x, y, c)).wait_recv()
            quad[k][chip] = pair[k][0] + pair[k][1]
        for cp in cps:
            cp.wait_send()
        for ki, k in enumerate(_CLASSES):
            owner = _CLASS_OWNER[k]

            @pl.when(c == owner)
            def _(ki=ki, k=k):
                cps2 = [_rcopy(quad[k].at[chip], quad[k].at[chip], send, recv, 4 + 3 * ki + j, (px, py, c))
                        for j, (px, py) in enumerate(_other_chips(x, y))]
                for cp in cps2:
                    cp.start()
                for j, (px, py) in enumerate(_other_chips(x, y)):
                    got = quad[k].at[2 * px + py]
                    _rcopy(got, got, send, recv, 4 + 3 * ki + j, (x, y, c)).wait_recv()
                outs[k][...] = ((quad[k][0] + quad[k][1]) + quad[k][2]) + quad[k][3]
                share = _rcopy(outs[k], outs[k], send, recv, 16 + ki, (x, y, 1 - c))
                share.start()
                for cp in cps2:
                    cp.wait_send()
                share.wait_send()

        for ki, k in enumerate(_CLASSES):
            @pl.when(c != _CLASS_OWNER[k])
            def _(ki=ki, k=k):
                _rcopy(outs[k], outs[k], send, recv, 16 + ki, (x, y, c)).wait_recv()

    vm = pl.BlockSpec(memory_space=pltpu.VMEM)
    return pl.pallas_call(
        body, name="small_allreduce", in_specs=[vm] * nin, out_specs=[vm] * 4,
        out_shape=[jax.ShapeDtypeStruct(_CLASS_SHAPE[k], F32) for k in _CLASSES],
        scratch_shapes=[pltpu.VMEM((2,) + _CLASS_SHAPE[k], F32) for k in _CLASSES]
        + [pltpu.VMEM((NCHIP,) + _CLASS_SHAPE[k], F32) for k in _CLASSES]
        + [pltpu.SemaphoreType.DMA((20,)), pltpu.SemaphoreType.DMA((20,))],
        compiler_params=pltpu.CompilerParams(vmem_limit_bytes=VMEM_LIMIT),
    )(*[g[n] for n in names], loss_blk)


def _adam_small(red, w, m, v):
    names = [s[0] for s in _SMALL]
    n = len(names)

    def body(*refs):
        red_refs = dict(zip(_CLASSES, refs[:4]))
        w_refs, m_refs, v_refs = refs[4:4 + n], refs[4 + n:4 + 2 * n], refs[4 + 2 * n:4 + 3 * n]
        loss_ref = refs[4 + 3 * n]
        out_refs = refs[5 + 3 * n:]
        x, y, _ = _mesh_pos()
        chip = 2 * x + y
        loss_ref[...] = jnp.broadcast_to(red_refs["a"][_LOSS_ROW:_LOSS_ROW + 1, 0:1], loss_ref.shape)
        for pi, (name, k, r0, nr, width) in enumerate(_SMALL):
            gfull = red_refs[k][r0:r0 + nr, :]
            if name == "rec_conv_w":
                parts = [gfull[:, 128 * s:128 * (s + 1)] for s in range(NCHIP)]
                g = jnp.where(chip == 0, parts[0], jnp.where(chip == 1, parts[1], jnp.where(chip == 2, parts[2], parts[3])))
            elif name == "ffn_conv_w":
                parts = [gfull[:, FC * s:FC * (s + 1)] for s in range(NCHIP)]
                g = jnp.where(chip == 0, parts[0], jnp.where(chip == 1, parts[2], jnp.where(chip == 2, parts[1], parts[3])))
            elif name == "ffn_conv_b":
                g = jnp.concatenate([gfull[:, FC * s:FC * (s + 1)] for s in (0, 2, 1, 3)], axis=1)
            else:
                g = gfull
            d, m2, v2 = _adam_math(w_refs[pi][...], g, m_refs[pi][...], v_refs[pi][...])
            o = out_refs[4 * pi:4 * pi + 4]
            o[0][...], o[1][...], o[2][...], o[3][...] = g, d, m2, v2

    vm = pl.BlockSpec(memory_space=pltpu.VMEM)
    outs = [jax.ShapeDtypeStruct((1, 128), F32)]
    for name in names:
        outs += [jax.ShapeDtypeStruct(w[name].shape, F32)] * 4
    res = pl.pallas_call(
        body, name="adam_small", in_specs=[vm] * (4 + 3 * n), out_specs=[vm] * len(outs), out_shape=outs,
        compiler_params=pltpu.CompilerParams(vmem_limit_bytes=VMEM_LIMIT),
    )(*red, *[w[k] for k in names], *[m[k] for k in names], *[v[k] for k in names])
    return res[0], {name: res[1 + 4 * i:5 + 4 * i] for i, name in enumerate(names)}


_WEIGHTS = ("g_mix", "w_in", "q_norm_g", "k_norm_g", "rec_conv_w", "rec_conv_b", "w_rg", "b_rg", "w_ig", "b_ig",
            "lru_lambda", "g_attn_out", "g_rec_out", "w_out", "g_ffn", "w_up", "ffn_conv_w", "ffn_conv_b", "w_down")
_BIG = ("w_in", "w_out", "w_up", "w_down")
_BIG_PERM = {"w_in": False, "w_out": False, "w_up": True, "w_down": False}
_SMALL_2D = {"w_rg": (RW, HD), "w_ig": (RW, HD), "b_rg": (8, HD), "b_ig": (8, HD), "rec_conv_w": (4, 128),
             "ffn_conv_w": (3, FC)}


def _halves(a):
    r, c = a.shape
    return a.reshape(2, r // 2, c)


def kernel(x, positions, g_mix, w_in, q_norm_g, k_norm_g, rec_conv_w, rec_conv_b, w_rg, b_rg, w_ig, b_ig, lru_lambda, g_attn_out, g_rec_out, w_out, g_ffn, w_up, ffn_conv_w, ffn_conv_b, w_down, loss_target, m_g_mix, m_w_in, m_q_norm_g, m_k_norm_g, m_rec_conv_w, m_rec_conv_b, m_w_rg, m_b_rg, m_w_ig, m_b_ig, m_lru_lambda, m_g_attn_out, m_g_rec_out, m_w_out, m_g_ffn, m_w_up, m_ffn_conv_w, m_ffn_conv_b, m_w_down, v_g_mix, v_w_in, v_q_norm_g, v_k_norm_g, v_rec_conv_w, v_rec_conv_b, v_w_rg, v_b_rg, v_w_ig, v_b_ig, v_lru_lambda, v_g_attn_out, v_g_rec_out, v_w_out, v_g_ffn, v_w_up, v_ffn_conv_w, v_ffn_conv_b, v_w_down):
    given = dict(g_mix=g_mix, w_in=w_in, q_norm_g=q_norm_g, k_norm_g=k_norm_g, rec_conv_w=rec_conv_w, rec_conv_b=rec_conv_b, w_rg=w_rg, b_rg=b_rg, w_ig=w_ig, b_ig=b_ig, lru_lambda=lru_lambda, g_attn_out=g_attn_out, g_rec_out=g_rec_out, w_out=w_out, g_ffn=g_ffn, w_up=w_up, ffn_conv_w=ffn_conv_w, ffn_conv_b=ffn_conv_b, w_down=w_down)
    given_m = dict(g_mix=m_g_mix, w_in=m_w_in, q_norm_g=m_q_norm_g, k_norm_g=m_k_norm_g, rec_conv_w=m_rec_conv_w, rec_conv_b=m_rec_conv_b, w_rg=m_w_rg, b_rg=m_b_rg, w_ig=m_w_ig, b_ig=m_b_ig, lru_lambda=m_lru_lambda, g_attn_out=m_g_attn_out, g_rec_out=m_g_rec_out, w_out=m_w_out, g_ffn=m_g_ffn, w_up=m_w_up, ffn_conv_w=m_ffn_conv_w, ffn_conv_b=m_ffn_conv_b, w_down=m_w_down)
    given_v = dict(g_mix=v_g_mix, w_in=v_w_in, q_norm_g=v_q_norm_g, k_norm_g=v_k_norm_g, rec_conv_w=v_rec_conv_w, rec_conv_b=v_rec_conv_b, w_rg=v_w_rg, b_rg=v_b_rg, w_ig=v_w_ig, b_ig=v_b_ig, lru_lambda=v_lru_lambda, g_attn_out=v_g_attn_out, g_rec_out=v_g_rec_out, w_out=v_w_out, g_ffn=v_g_ffn, w_up=v_w_up, ffn_conv_w=v_ffn_conv_w, ffn_conv_b=v_ffn_conv_b, w_down=v_w_down)
    shapes = {n: a.shape for n, a in given.items()}

    def two_d(n, a):
        a = a[0]
        return a.reshape(_SMALL_2D[n]) if n in _SMALL_2D else (a if a.ndim == 2 else a[None])

    w = {n: two_d(n, a) for n, a in given.items()}
    m = {n: two_d(n, a) for n, a in given_m.items()}
    v = {n: two_d(n, a) for n, a in given_v.items()}
    cc = lax.axis_index("c").astype(jnp.int32)
    cx, cy = lax.axis_index("x").astype(jnp.int32), lax.axis_index("y").astype(jnp.int32)
    slot = {False: 2 * cx + cy, True: 2 * cy + cx}

    shards = {"w_in": _halves(_cast_bf16("cast_w_in", w["w_in"]))}
    first = [shards["w_in"], jnp.pad(w["ffn_conv_w"], ((0, 5), (0, 0))), jnp.pad(w["rec_conv_w"], ((0, 4), (0, 0)))]
    first_perm = [False, True, False]
    first_plan = _gather_half_plan(first_perm, [True, False, False])
    in_flight = _split_start(
        "gather_in_start", first,
        [lax.dynamic_update_slice(lax.empty((NCHIP,) + a.shape, a.dtype), a[None], (slot[pm],) + (0,) * a.ndim)
         for a, pm in zip(first, first_perm)], first_plan, 3 * len(first))
    for n in ("w_out", "w_up", "w_down"):
        shards[n] = _halves(_cast_bf16(f"cast_{n}", w[n], after=(in_flight[4],)))
    p = {n: w[n] for n in ("g_mix", "g_ffn", "q_norm_g", "k_norm_g", "rec_conv_b", "lru_lambda", "g_attn_out", "g_rec_out")}
    p.update(w_rg=w["w_rg"].reshape(8, HD, HD), w_ig=w["w_ig"].reshape(8, HD, HD), b_rg=w["b_rg"], b_ig=w["b_ig"],
             ffn_conv_b=jnp.concatenate([w["ffn_conv_b"][:, FC * s:FC * (s + 1)] for s in (0, 2, 1, 3)], axis=1))

    class Exchange:
        rest = ("w_out", "w_up", "w_down")
        order = []
        flight = {}

        def wait_first(self, after):
            send, recv, srcs, lands, _ = in_flight
            f_in, f_fcw, f_rcw = _split_wait("gather_in_wait", send, recv, srcs, lands, first_plan,
                                             (after,) + tuple(shards[n] for n in self.rest))
            (f_in,) = _sibling_fill([f_in], [False])
            return dict(w_in=f_in.reshape(NCHIP, D, INW // NCHIP), ffn_conv_w=f_fcw,
                        rec_conv_w=f_rcw.transpose(1, 0, 2).reshape(8, RW))

        def start_rest(self):
            srcs = [shards[n] for n in self.rest]
            lands = [lax.dynamic_update_slice(lax.empty((NCHIP,) + s.shape, BF16), s[None], (slot[_BIG_PERM[n]], 0, 0, 0))
                     for n, s in zip(self.rest, srcs)]
            plan = _gather_plan([_BIG_PERM[n] for n in self.rest])
            send, recv, srcs, lands, token = _split_start("gather_rest_start", srcs, lands, plan, 6 * len(srcs))
            self.flight["rest"] = (send, recv, srcs, lands, plan)
            return (token,)

        def wait_rest(self, after):
            send, recv, srcs, lands, plan = self.flight.pop("rest")
            f_out, f_up, f_down = _split_wait("gather_rest_wait", send, recv, srcs, lands, plan, after)
            return dict(w_out=f_out.reshape(D, D), w_up=f_up.reshape(NCHIP, D, FC), w_down=f_down.reshape(DFF, D))

        def reduce_start(self, name, g32, g16):
            r2, cols = shards[name].shape[1:]
            plan = _reduce_plan(_BIG_PERM[name])
            send, recv, srcs, lands, token = _split_start(
                f"reduce_{name}_start", [g16.reshape(NCHIP, 2, r2, cols)], [lax.empty((7, r2, cols), BF16)], plan, 7)
            self.flight[name] = (send, recv, srcs, lands, plan, g32.reshape(NCHIP, 2, r2, cols))
            self.order.append(name)
            return (token,)

        def finish(self, after):
            mine = {}
            for name in self.order:
                send, recv, srcs, lands, plan, g32 = self.flight.pop(name)
                (got,) = _split_wait(f"reduce_{name}_wait", send, recv, srcs, lands, plan, after)
                where = jnp.stack([slot[_BIG_PERM[name]], cc])
                mine[name] = after = _add_pieces(f"reduce_{name}_add", g32, got, where)
            theirs = dict(zip(_BIG, _sibling_share([mine[n] for n in _BIG])))
            return mine, theirs

    exch = Exchange()

    loss_blk, grad_x, g = _local_step(x[0], positions.reshape(T, 1), loss_target[0], p, exch)

    out_g, out_d, out_m, out_v = {}, {}, {}, {}
    red = _small_allreduce(g, loss_blk)
    loss_row, small_out = _adam_small(red, w, m, v)
    for n, (gn, dn, mn, vn) in small_out.items():
        out_g[n], out_d[n], out_m[n], out_v[n] = gn, dn, mn, vn

    mine, theirs = exch.finish(red[0])
    for n in _BIG:
        out_g[n], out_d[n], out_m[n], out_v[n] = _adam_big(f"adam_{n}", w[n], mine[n], theirs[n], m[n], v[n], cc.reshape(1))

    outs = [loss_row[0, 0], grad_x[None]]
    for group in (out_g, out_d, out_m, out_v):
        outs += [group[n].reshape(shapes[n]) for n in _WEIGHTS]
    return tuple(outs)
```

```python
import math

import jax
import jax.numpy as jnp
import numpy as np
from jax import lax
from jax.experimental import pallas as pl
from jax.experimental.pallas import tpu as pltpu

F32 = jnp.float32
BF16 = jnp.bfloat16

T = 4096
D = 1024
HD = 64
AW = 512
RW = 512
INW = 2560
DFF = 3072
NCHIP = 4
EPS = 1e-6
NEG = -1e30
LRU_C = 8.0
ROPE_THETA = 10000.0
BLK = 128
DILATIONS = (1, 4, 16)
ADAM_LR, ADAM_B1, ADAM_B2, ADAM_EPS, ADAM_WD, ADAM_STEP = 0.001, 0.9, 0.999, 1e-08, 0.01, 10
VMEM_LIMIT = 56 * 1024 * 1024
MESH = pl.DeviceIdType.MESH

NN = (((1,), (0,)), ((), ()))
NT = (((1,), (1,)), ((), ()))
TN = (((0,), (0,)), ((), ()))


def _cp(*sem):
    return pltpu.CompilerParams(dimension_semantics=sem, vmem_limit_bytes=VMEM_LIMIT)


def _bs(shape, fn):
    return pl.BlockSpec(shape, fn)


def _dot(a, b, dims=NN):
    return lax.dot_general(a, b, dims, preferred_element_type=F32)


_GC = math.sqrt(2.0 / math.pi)


def _gelu(x):
    return x * (0.5 + 0.5 * jnp.tanh(x * (_GC + (_GC * 0.044715) * (x * x))))


def _gelu_and_grad(x):
    x2 = x * x
    th = jnp.tanh(x * (_GC + (_GC * 0.044715) * x2))
    cdf = 0.5 + 0.5 * th
    dg = cdf + (x * (1.0 - th * th)) * ((0.5 * _GC) + (1.5 * 0.044715 * _GC) * x2)
    return x * cdf, dg


def _softplus(x):
    e = jnp.exp(-jnp.abs(x))
    u = 1.0 + e
    l1p = jnp.where(u == 1.0, e, jnp.log(u) * (e / (u - 1.0)))
    return jnp.maximum(x, 0.0) + l1p


def _segsum(z, e_bf16):
    hi = z.astype(BF16)
    lo = (z - hi.astype(F32)).astype(BF16)
    parts = []
    for c0 in range(0, z.shape[1], 128):
        parts.append(_dot(hi[:, c0:c0 + 128], e_bf16) + _dot(lo[:, c0:c0 + 128], e_bf16))
    return jnp.concatenate(parts, axis=1)


def _mm(name, a, b, mode, tm, tn, out_dtype=F32, stack=0, twin_bf16=False, after=(), a_full=False):
    if mode == "nn":
        (m, k), n = a.shape, (b.shape[1] if not stack else stack * b.shape[2])
        a_spec = _bs((tm, k), lambda j, i: (i, 0))
        if stack:
            per = b.shape[2] // tn
            b_spec = _bs((None, k, tn), lambda j, i: (j // per, 0, j % per))
        else:
            b_spec = _bs((k, tn), lambda j, i: (0, j))
    elif mode == "nt":
        (m, k), n = a.shape, b.shape[0]
        a_spec, b_spec = _bs((tm, k), lambda j, i: (i, 0)), _bs((tn, k), lambda j, i: (j, 0))
    else:
        (k, m), n = a.shape, b.shape[1]
        a_spec, b_spec = _bs((k, tm), lambda j, i: (0, i)), _bs((k, tn), lambda j, i: (0, j))
    assert m % tm == 0 and n % tn == 0
    o_spec = _bs((tm, tn), lambda j, i: (i, j))
    o_shape = (m, n)
    if mode == "tn" and stack:
        per = n // stack // tn
        o_spec = _bs((None, tm, tn), lambda j, i: (j // per, i, j % per))
        o_shape = (stack, m, n // stack)
    dims = {"nn": NN, "nt": NT, "tn": TN}[mode]
    if a_full:
        a_spec = pl.BlockSpec(a.shape, lambda j, i: (0, 0), pipeline_mode=pl.Buffered(1))

    def body(a_ref, b_ref, *rest):
        if a_full:
            mine = pl.ds(pl.multiple_of(pl.program_id(1) * tm, tm), tm)
            av = a_ref[:, mine] if mode == "tn" else a_ref[mine, :]
        else:
            av = a_ref[...]
        acc = _dot(av, b_ref[...], dims)
        outs = rest[len(after):]
        outs[0][...] = acc.astype(out_dtype)
        if twin_bf16:
            outs[1][...] = acc.astype(BF16)

    ins = (a, b) + tuple(after)
    specs = [a_spec, b_spec] + [pl.BlockSpec(memory_space=pl.ANY)] * len(after)
    shapes = [jax.ShapeDtypeStruct(o_shape, out_dtype)] + ([jax.ShapeDtypeStruct(o_shape, BF16)] if twin_bf16 else [])
    out = pl.pallas_call(
        body, name=name, grid=(n // tn, m // tm), in_specs=specs, out_specs=[o_spec] * len(shapes),
        out_shape=shapes, compiler_params=_cp("parallel", "parallel"),
    )(*ins)
    return tuple(out) if twin_bf16 else out[0]


def _rms_fwd(name, x, g):
    tr = 512

    def body(x_ref, g_ref, o_ref):
        xv = x_ref[...]
        r = lax.rsqrt(jnp.mean(xv * xv, axis=-1, keepdims=True) + EPS)
        o_ref[...] = ((xv * r) * g_ref[...]).astype(BF16)

    return pl.pallas_call(
        body, name=name, grid=(T // tr,), in_specs=[_bs((tr, D), lambda i: (i, 0)), _bs((1, D), lambda i: (0, 0))],
        out_specs=_bs((tr, D), lambda i: (i, 0)), out_shape=jax.ShapeDtypeStruct((T, D), BF16),
        compiler_params=_cp("parallel"),
    )(x, g)


def _dgrad_rms_bwd(name, a, w, x, g, dres, want_bf16, after=()):
    tm = 512
    stack, _, cs = w.shape
    k = a.shape[1]

    def body(a_ref, w_ref, x_ref, g_ref, dr_ref, *rest):
        rest = rest[len(after):]
        dx_ref, dg_ref = rest[0], rest[-1]
        dyv = _dot(a_ref[:, 0:cs], w_ref[0], NT)
        for s in range(1, stack):
            dyv = dyv + _dot(a_ref[:, s * cs:(s + 1) * cs], w_ref[s], NT)
        xv = x_ref[...]
        r = lax.rsqrt(jnp.mean(xv * xv, axis=-1, keepdims=True) + EPS)
        gdy = g_ref[...] * dyv
        dx = r * gdy - xv * ((r * r * r) * jnp.mean(xv * gdy, axis=-1, keepdims=True)) + dr_ref[...]
        dx_ref[...] = dx
        if want_bf16:
            rest[1][...] = dx.astype(BF16)

        @pl.when(pl.program_id(0) == 0)
        def _():
            dg_ref[...] = jnp.zeros_like(dg_ref)

        dg_ref[...] += jnp.sum(dyv * (xv * r), axis=0, keepdims=True)

    row = _bs((tm, D), lambda i: (i, 0))
    vec = _bs((1, D), lambda i: (0, 0))
    outs = [jax.ShapeDtypeStruct((T, D), F32)] + ([jax.ShapeDtypeStruct((T, D), BF16)] if want_bf16 else [])
    return pl.pallas_call(
        body, name=name, grid=(T // tm,),
        in_specs=[_bs((tm, k), lambda i: (i, 0)),
                  pl.BlockSpec(w.shape, lambda i: (0, 0, 0), pipeline_mode=pl.Buffered(1)), row, vec, row]
        + [pl.BlockSpec(memory_space=pl.ANY)] * len(after),
        out_specs=[row] * len(outs) + [vec], out_shape=outs + [jax.ShapeDtypeStruct((1, D), F32)],
        compiler_params=_cp("arbitrary"),
    )(a, w, x, g, dres, *after)


def _out_rms(mix, w_out, x, g):
    tm = 512

    def body(a_ref, b_ref, r_ref, g_ref, x1_ref, h_ref):
        x1 = r_ref[...] + _dot(a_ref[...], b_ref[...])
        x1_ref[...] = x1
        r = lax.rsqrt(jnp.mean(x1 * x1, axis=-1, keepdims=True) + EPS)
        h_ref[...] = ((x1 * r) * g_ref[...]).astype(BF16)

    row = _bs((tm, D), lambda i: (i, 0))
    return pl.pallas_call(
        body, name="mm_out", grid=(T // tm,),
        in_specs=[row, pl.BlockSpec((D, D), lambda i: (0, 0), pipeline_mode=pl.Buffered(1)), row,
                  _bs((1, D), lambda i: (0, 0))],
        out_specs=[row, row], out_shape=[jax.ShapeDtypeStruct((T, D), F32), jax.ShapeDtypeStruct((T, D), BF16)],
        compiler_params=_cp("parallel"),
    )(mix, w_out, x, g)


def _head_ones():
    idx = np.arange(128) // HD
    return jnp.asarray((idx[:, None] == idx[None, :]).astype(np.float32), dtype=BF16)


def _freq_row():
    half = HD // 2
    inv = ROPE_THETA ** (-(np.arange(half, dtype=np.float64)) / half)
    return jnp.asarray(np.tile(inv, 4)[None, :], dtype=F32)


def _rot_tables(cos128, sin128):
    c = jnp.tile(cos128, (1, 4))
    s = jnp.tile(sin128, (1, 4))
    lane = lax.broadcasted_iota(jnp.int32, (1, AW), 1)
    first = (lane & 32) == 0
    return c, jnp.where(first, -s, s), first


def _swap_halves(y, first):
    return jnp.where(first, pltpu.roll(y, AW - 32, 1), pltpu.roll(y, 32, 1))


def _qk_prep(proj, pos_col, qg, kg):
    tr = 512

    def body(q_ref, k_ref, pos_ref, f_ref, qg_ref, kg_ref, e_ref, qo_ref, ko_ref, cos_ref, sin_ref):
        ang = pos_ref[...].astype(F32) * f_ref[...]
        cos_ref[...] = jnp.cos(ang)
        sin_ref[...] = jnp.sin(ang)
        c, s_signed, first = _rot_tables(cos_ref[...], sin_ref[...])
        e = e_ref[...]

        def norm_rot(xv, g, scale):
            r = lax.rsqrt(_segsum(xv * xv, e) * (1.0 / HD) + EPS)
            y = (xv * r) * g
            return (y * c + _swap_halves(y, first) * s_signed) * scale

        qo_ref[...] = norm_rot(q_ref[...], qg_ref[...], HD ** -0.5)
        ko_ref[...] = norm_rot(k_ref[...], kg_ref[...], 1.0)

    col = lambda j: _bs((tr, AW), lambda i, j=j: (i, j))
    vec = _bs((1, AW), lambda i: (0, 0))
    out = jax.ShapeDtypeStruct((T, AW), F32)
    tab = jax.ShapeDtypeStruct((T, 128), F32)
    tspec = _bs((tr, 128), lambda i: (i, 0))
    return pl.pallas_call(
        body, name="qk_prep", grid=(T // tr,),
        in_specs=[col(0), col(1), _bs((tr, 1), lambda i: (i, 0)), _bs((1, 128), lambda i: (0, 0)), vec, vec,
                  _bs((128, 128), lambda i: (0, 0))],
        out_specs=[col(0)] * 2 + [tspec] * 2, out_shape=[out, out, tab, tab], compiler_params=_cp("parallel"),
    )(proj, proj, pos_col, _freq_row(), qg, kg, _head_ones())


def _qk_bwd(proj, cos_t, sin_t, qg, kg, dq, dk, dv):
    tr = 512

    def body(q_ref, k_ref, cos_ref, sin_ref, qg_ref, kg_ref, e_ref, dq_ref, dk_ref, dv_ref, o_ref, dqg_ref, dkg_ref):
        i, j = pl.program_id(0), pl.program_id(1)

        @pl.when((i == 0) & (j == 0))
        def _():
            dqg_ref[...] = jnp.zeros_like(dqg_ref)
            dkg_ref[...] = jnp.zeros_like(dkg_ref)

        def norm_rot_bwd(x_ref, g_ref, dg_ref, d_ref, scale):
            c, s_signed, first = _rot_tables(cos_ref[...], sin_ref[...])
            e = e_ref[...]
            dout = d_ref[...] * scale
            dy = dout * c + _swap_halves(dout * s_signed, first)
            xv, g = x_ref[...], g_ref[...]
            r = lax.rsqrt(_segsum(xv * xv, e) * (1.0 / HD) + EPS)
            gdy = g * dy
            dx = r * gdy - xv * ((r * r * r) * (_segsum(xv * gdy, e) * (1.0 / HD)))
            o_ref[...] = dx.astype(BF16)
            dg_ref[...] += jnp.sum(dy * (xv * r), axis=0, keepdims=True)

        @pl.when(j == 0)
        def _():
            o_ref[...] = dv_ref[...].astype(BF16)

        @pl.when(j == 1)
        def _():
            norm_rot_bwd(q_ref, qg_ref, dqg_ref, dq_ref, HD ** -0.5)

        @pl.when(j == 2)
        def _():
            norm_rot_bwd(k_ref, kg_ref, dkg_ref, dk_ref, 1.0)

    col = lambda jj: _bs((tr, AW), lambda i, j, jj=jj: (i, jj))
    vec = _bs((1, AW), lambda i, j: (0, 0))
    piece = _bs((tr, AW), lambda i, j: (i, 0))
    return pl.pallas_call(
        body, name="qk_bwd", grid=(T // tr, 3),
        in_specs=[col(0), col(1), _bs((tr, 128), lambda i, j: (i, 0)), _bs((tr, 128), lambda i, j: (i, 0)), vec, vec,
                  _bs((128, 128), lambda i, j: (0, 0))] + [piece] * 3,
        out_specs=[_bs((tr, AW), lambda i, j: (i, (j + 2) % 3)), vec, vec],
        out_shape=[jax.ShapeDtypeStruct((T, 3 * AW), BF16), jax.ShapeDtypeStruct((1, AW), F32),
                   jax.ShapeDtypeStruct((1, AW), F32)],
        compiler_params=_cp("arbitrary", "arbitrary"),
    )(proj, proj, cos_t, sin_t, qg, kg, _head_ones(), dq, dk, dv)


RG = 256


def _stacked_band_mask():
    qi = lax.broadcasted_iota(jnp.int32, (2 * BLK, 2 * BLK), 0) & (BLK - 1)
    kj = lax.broadcasted_iota(jnp.int32, (2 * BLK, 2 * BLK), 1)
    rel = qi - kj + BLK
    return (rel >= 0) & (rel <= BLK), lax.broadcasted_iota(jnp.int32, (1, 2 * BLK), 1) >= BLK


def _natural_rows(r0, n_rows, d):
    if d == 1:
        return pl.ds(r0, n_rows)
    ln = T // d
    return pl.ds(r0 // ln + d * (r0 % ln), n_rows, stride=d)


def _regroup_into(dst, src_ref, d, pad, cast=True):
    def step(j, carry):
        r0 = pl.multiple_of(j * RG, RG)
        val = src_ref[_natural_rows(r0, RG, d), :]
        dst[pl.ds(pad + r0, RG), :] = val.astype(dst.dtype) if cast else val
        return carry
    lax.fori_loop(0, T // RG, step, 0)


def _stack_heads(x, h0):
    zero = jnp.zeros_like(x)
    return jnp.concatenate([jnp.where(h0, x, zero), jnp.where(h0, zero, x)], axis=0)


def _attn_fwd(q, k, proj):
    nblk = T // BLK

    def body(q_ref, k_ref, v_ref, a_ref, lse_ref, qs, ks, vs, o0, o1, o2, l0, l1, l2, sb0, sb1):
        band, cur_half = _stacked_band_mask()
        h0 = lax.broadcasted_iota(jnp.int32, (1, 128), 1) < HD
        ks[0:BLK, :] = jnp.zeros((BLK, 128), BF16)
        vs[0:BLK, :] = jnp.zeros((BLK, 128), BF16)
        for d, o_s, l_s in zip(DILATIONS, (o0, o1, o2), (l0, l1, l2)):
            nb = T // d // BLK
            _regroup_into(qs, q_ref, d, 0)
            _regroup_into(ks, k_ref, d, BLK)
            _regroup_into(vs, v_ref, d, BLK)

            def scores(b):
                r0 = pl.multiple_of(b * BLK, BLK)
                return _dot(_stack_heads(qs[pl.ds(r0, BLK), :], h0), ks[pl.ds(r0, 2 * BLK), :], NT)

            def finish(b, s_raw, d=d, nb=nb, o_s=o_s, l_s=l_s):
                r0 = pl.multiple_of(b * BLK, BLK)
                mask = band & (cur_half | ((b & (nb - 1)) > 0))
                s = jnp.where(mask, s_raw, NEG)
                m = jnp.max(s, axis=1, keepdims=True)
                p = jnp.exp(s - m)
                l = jnp.sum(p, axis=1, keepdims=True)
                o = _dot(p.astype(BF16), vs[pl.ds(r0, 2 * BLK), :]) / l
                lse = m + jnp.log(l)
                rows = _natural_rows(r0, BLK, d)
                o_s[rows, :] = jnp.where(h0, o[0:BLK, :], o[BLK:, :])
                l_s[rows, :] = jnp.where(h0, lse[0:BLK, :], lse[BLK:, :])

            sb0[...] = scores(0)

            def step(i, carry):
                b = 2 * i
                sb1[...] = scores(b + 1)
                finish(b, sb0[...])
                sb0[...] = scores(jnp.minimum(b + 2, nblk - 1))
                finish(b + 1, sb1[...])
                return carry

            lax.fori_loop(0, nblk // 2, step, 0)

        def merge(i, carry):
            r = pl.ds(pl.multiple_of(i * RG, RG), RG)
            la, lb, lc = l0[r, :], l1[r, :], l2[r, :]
            m = jnp.maximum(jnp.maximum(la, lb), lc)
            ea, eb, ec = jnp.exp(la - m), jnp.exp(lb - m), jnp.exp(lc - m)
            z = (ea + eb) + ec
            a_ref[r, :] = ((ea * o0[r, :] + eb * o1[r, :]) + ec * o2[r, :]) / z
            lse_ref[r, :] = m + jnp.log(z)
            return carry

        lax.fori_loop(0, T // RG, merge, 0)

    spec = lambda cb: _bs((T, 128), lambda p, cb=cb: (0, cb + p))
    out = jax.ShapeDtypeStruct((T, AW), F32)
    return pl.pallas_call(
        body, name="attn_fwd", grid=(AW // 128,), in_specs=[spec(0), spec(0), spec(8)], out_specs=[spec(0)] * 2,
        out_shape=[out] * 2,
        scratch_shapes=[pltpu.VMEM((T, 128), BF16), pltpu.VMEM((T + BLK, 128), BF16), pltpu.VMEM((T + BLK, 128), BF16)]
        + [pltpu.VMEM((T, 128), F32)] * 6 + [pltpu.VMEM((2 * BLK, 2 * BLK), F32)] * 2,
        compiler_params=_cp("parallel"),
    )(q, k, proj)


def _attn_bwd(q, k, proj, do, lse, delta):
    nblk = T // BLK

    def body(q_ref, k_ref, v_ref, do_ref, l_ref, dl_ref, dq_ref, dk_ref, dv_ref, qs, dos, ks, vs, ls, dls, dks, dvs,
             sa0, sa1, da0, da1):
        band, cur_half = _stacked_band_mask()
        h0 = lax.broadcasted_iota(jnp.int32, (1, 128), 1) < HD
        ks[0:BLK, :] = jnp.zeros((BLK, 128), BF16)
        vs[0:BLK, :] = jnp.zeros((BLK, 128), BF16)
        for d in DILATIONS:
            nb = T // d // BLK
            _regroup_into(qs, q_ref, d, 0)
            _regroup_into(dos, do_ref, d, 0)
            _regroup_into(ks, k_ref, d, BLK)
            _regroup_into(vs, v_ref, d, BLK)
            _regroup_into(ls, l_ref, d, 0, cast=False)
            _regroup_into(dls, dl_ref, d, 0, cast=False)
            dks[...] = jnp.zeros_like(dks)
            dvs[...] = jnp.zeros_like(dvs)

            def scores(b, s_buf, dp_buf):
                r0 = pl.multiple_of(b * BLK, BLK)
                win = pl.ds(r0, 2 * BLK)
                s_buf[...] = _dot(_stack_heads(qs[pl.ds(r0, BLK), :], h0), ks[win, :], NT)
                dp_buf[...] = _dot(_stack_heads(dos[pl.ds(r0, BLK), :], h0), vs[win, :], NT)

            def finish(b, s_buf, dp_buf, d=d, nb=nb):
                r0 = pl.multiple_of(b * BLK, BLK)
                mask = band & (cur_half | ((b & (nb - 1)) > 0))
                win = pl.ds(r0, 2 * BLK)
                lv, dlv = ls[pl.ds(r0, BLK), :], dls[pl.ds(r0, BLK), :]
                lse2 = jnp.concatenate([lv[:, 0:1], lv[:, HD:HD + 1]], axis=0)
                dl2 = jnp.concatenate([dlv[:, 0:1], dlv[:, HD:HD + 1]], axis=0)
                p = jnp.exp(jnp.where(mask, s_buf[...], NEG) - lse2)
                ds = p * (dp_buf[...] - dl2)
                pb, dsb = p.astype(BF16), ds.astype(BF16)
                dq2 = _dot(dsb, ks[win, :])
                dks[win, :] += _dot(dsb, _stack_heads(qs[pl.ds(r0, BLK), :], h0), TN)
                dvs[win, :] += _dot(pb, _stack_heads(dos[pl.ds(r0, BLK), :], h0), TN)
                rows = _natural_rows(r0, BLK, d)
                dq = jnp.where(h0, dq2[0:BLK, :], dq2[BLK:, :])
                dq_ref[rows, :] = dq if d == 1 else dq_ref[rows, :] + dq

            scores(0, sa0, da0)

            def step(i, carry):
                b = 2 * i
                scores(b + 1, sa1, da1)
                finish(b, sa0, da0)
                scores(jnp.minimum(b + 2, nblk - 1), sa0, da0)
                finish(b + 1, sa1, da1)
                return carry

            lax.fori_loop(0, nblk // 2, step, 0)

            def back(j, carry, d=d):
                r0 = pl.multiple_of(j * RG, RG)
                rows = _natural_rows(r0, RG, d)
                src = pl.ds(BLK + r0, RG)
                dk_ref[rows, :] = dks[src, :] if d == 1 else dk_ref[rows, :] + dks[src, :]
                dv_ref[rows, :] = dvs[src, :] if d == 1 else dv_ref[rows, :] + dvs[src, :]
                return carry

            lax.fori_loop(0, T // RG, back, 0)

    spec = lambda cb: _bs((T, 128), lambda p, cb=cb: (0, cb + p))
    ospec = _bs((T, 128), lambda p: (0, p))
    out = jax.ShapeDtypeStruct((T, AW), F32)
    return pl.pallas_call(
        body, name="attn_bwd", grid=(AW // 128,), in_specs=[spec(0), spec(0), spec(8), spec(0), spec(0), spec(0)],
        out_specs=[ospec] * 3, out_shape=[out] * 3,
        scratch_shapes=[pltpu.VMEM((T, 128), BF16), pltpu.VMEM((T, 128), BF16), pltpu.VMEM((T + BLK, 128), BF16),
                        pltpu.VMEM((T + BLK, 128), BF16), pltpu.VMEM((T, 128), F32), pltpu.VMEM((T, 128), F32),
                        pltpu.VMEM((T + BLK, 128), F32), pltpu.VMEM((T + BLK, 128), F32)]
        + [pltpu.VMEM((2 * BLK, 2 * BLK), F32)] * 4,
        compiler_params=_cp("parallel"),
    )(q, k, proj, do, lse, delta)


def _attn_norm(attn, g_attn):
    tr = 512

    def body(a_ref, g_ref, mix_ref):
        attn = a_ref[...]
        r = lax.rsqrt(jnp.mean(attn * attn, axis=-1, keepdims=True) + EPS)
        mix_ref[...] = ((attn * r) * g_ref[...]).astype(BF16)

    row = _bs((tr, AW), lambda i: (i, 0))
    return pl.pallas_call(
        body, name="attn_norm", grid=(T // tr,), in_specs=[row, _bs((1, AW), lambda i: (0, 0))],
        out_specs=row, out_shape=jax.ShapeDtypeStruct((T, D), BF16), compiler_params=_cp("parallel"),
    )(attn, g_attn)


def _attn_out_bwd(attn, dmix, g_attn):
    tr = 512

    def body(a_ref, d_ref, g_ref, e_ref, do_ref, dl_ref, dg_ref):
        av, dyv = a_ref[...], d_ref[...]
        r = lax.rsqrt(jnp.mean(av * av, axis=-1, keepdims=True) + EPS)
        gdy = g_ref[...] * dyv
        da = r * gdy - av * ((r * r * r) * jnp.mean(av * gdy, axis=-1, keepdims=True))
        do_ref[...] = da
        dl_ref[...] = _segsum(da * av, e_ref[...])

        @pl.when(pl.program_id(0) == 0)
        def _():
            dg_ref[...] = jnp.zeros_like(dg_ref)

        dg_ref[...] += jnp.sum(dyv * (av * r), axis=0, keepdims=True)

    row = _bs((tr, AW), lambda i: (i, 0))
    vec = _bs((1, AW), lambda i: (0, 0))
    return pl.pallas_call(
        body, name="attn_out_bwd", grid=(T // tr,), in_specs=[row, row, vec, _bs((128, 128), lambda i: (0, 0))],
        out_specs=[row, row, vec],
        out_shape=[jax.ShapeDtypeStruct((T, AW), F32), jax.ShapeDtypeStruct((T, AW), F32),
                   jax.ShapeDtypeStruct((1, AW), F32)],
        compiler_params=_cp("arbitrary"),
    )(attn, dmix, g_attn, _head_ones())


TRR = 256


def _scan_fwd(a, u):
    n = a.shape[0]
    row = lax.broadcasted_iota(jnp.int32, (n, 1), 0)
    s = 1
    while s < n:
        keep = row >= s
        u = jnp.where(keep, a * pltpu.roll(u, s, 0) + u, u)
        a = jnp.where(keep, a * pltpu.roll(a, s, 0), a)
        s *= 2
    return a, u


def _scan_bwd(c, w):
    n = c.shape[0]
    row = lax.broadcasted_iota(jnp.int32, (n, 1), 0)
    s = 1
    while s < n:
        keep = row < n - s
        w = jnp.where(keep, c * pltpu.roll(w, n - s, 0) + w, w)
        c = jnp.where(keep, c * pltpu.roll(c, n - s, 0), c)
        s *= 2
    return w


def _gates(xc, wrg, wig, brg, big, sp):
    xcb = xc.astype(BF16)
    r = jax.nn.sigmoid(_dot(xcb, wrg) + brg)
    ig = jax.nn.sigmoid(_dot(xcb, wig) + big)
    la = (-LRU_C * r) * sp
    a = jnp.exp(la)
    mult = jnp.sqrt(-jnp.tanh(la) * (a * a + 1.0))
    return r, ig, a, mult


def _conv4(ext_ref, xr, cw_ref, cb_ref, n):
    y = cb_ref[...] + ext_ref[pl.ds(5, n), :] * cw_ref[0:1, :]
    y = y + ext_ref[pl.ds(6, n), :] * cw_ref[1:2, :]
    y = y + ext_ref[pl.ds(7, n), :] * cw_ref[2:3, :]
    return y + xr * cw_ref[3:4, :]


def _rec_fwd(proj, mix, cw, cb, wrg, wig, brg, big, lam, g_rec):
    n = TRR

    def body(xr_ref, gr_ref, cw_ref, cb_ref, wrg_ref, wig_ref, brg_ref, big_ref, lam_ref, g_ref, mix_in,
             mix_ref, h_ref, ext, hcar):
        del mix_in

        @pl.when(pl.program_id(0) == 0)
        def _():
            ext[0:8, :] = jnp.zeros((8, RW), F32)
            hcar[...] = jnp.zeros_like(hcar)

        xr = xr_ref[...]
        ext[8:, :] = xr
        xc = _conv4(ext, xr, cw_ref, cb_ref, n)
        ext[0:8, :] = xr[n - 8:, :]
        sp = _softplus(-lam_ref[...])
        _, ig, a, mult = _gates(xc, wrg_ref[...], wig_ref[...], brg_ref[...], big_ref[...], sp)
        a_s, u_s = _scan_fwd(a, mult * (ig * xc))
        h = u_s + a_s * hcar[7:8, :]
        h_ref[...] = h
        hcar[...] = h[n - 8:, :]
        pre = h * _gelu(gr_ref[...])
        r = lax.rsqrt(jnp.mean(pre * pre, axis=-1, keepdims=True) + EPS)
        mix_ref[...] = ((pre * r) * g_ref[...]).astype(BF16)

    vec = _bs((1, RW), lambda i: (0, 0))
    mat = _bs((RW, RW), lambda i: (0, 0))
    return pl.pallas_call(
        body, name="rec_fwd", grid=(T // n,),
        in_specs=[_bs((n, RW), lambda i: (i, 3)), _bs((n, RW), lambda i: (i, 4)), _bs((8, RW), lambda i: (0, 0)), vec,
                  mat, mat, vec, vec, vec, vec, pl.BlockSpec(memory_space=pl.ANY)],
        out_specs=[_bs((n, RW), lambda i: (i, 1)), _bs((n, RW), lambda i: (i, 0))],
        out_shape=[jax.ShapeDtypeStruct((T, D), BF16), jax.ShapeDtypeStruct((T, RW), F32)],
        scratch_shapes=[pltpu.VMEM((n + 8, RW), F32), pltpu.VMEM((8, RW), F32)],
        input_output_aliases={10: 0}, compiler_params=_cp("arbitrary"),
    )(proj, proj, cw, cb, wrg, wig, brg, big, lam, g_rec, mix)


def _rec_bwd(proj, h, dmix, cw, cb, wrg, wig, brg, big, lam, g_rec):
    n = TRR
    nt = T // n
    hb = n // 8

    def body(xr_ref, xh_ref, gr_ref, h_ref, hh_ref, dm_ref, cw_ref, cb_ref, wrg_ref, wig_ref, brg_ref, big_ref,
             lam_ref, g_ref, dp_ref, xc_ref, dr_ref, di_ref, dcw_ref, dcb_ref, dbr_ref, dbi_ref, dsp_ref,
             dg_ref, ext, exth, extd, adh):
        i, j = pl.program_id(0), pl.program_id(1)
        first_tile = i == nt - 1
        last_tile = i == 0

        @pl.when(j == 0)
        def _():
            @pl.when(last_tile)
            def _():
                for ref in (dcw_ref, dcb_ref, dbr_ref, dbi_ref, dsp_ref, dg_ref):
                    ref[...] = jnp.zeros_like(ref)
                extd[n:, :] = jnp.zeros((8, RW), F32)
                adh[...] = jnp.zeros_like(adh)

            row = lax.broadcasted_iota(jnp.int32, (n, 1), 0)
            xr = xr_ref[...]
            ext[0:8, :] = jnp.where(first_tile, 0.0, xh_ref[...])
            ext[8:, :] = xr
            xc = _conv4(ext, xr, cw_ref, cb_ref, n)
            sp = _softplus(-lam_ref[...])
            wrg, wig = wrg_ref[...], wig_ref[...]
            r, ig, a, mult = _gates(xc, wrg, wig, brg_ref[...], big_ref[...], sp)

            hv = h_ref[...]
            gl, dgl = _gelu_and_grad(gr_ref[...])
            pre = hv * gl
            dyv = dm_ref[...]
            rr = lax.rsqrt(jnp.mean(pre * pre, axis=-1, keepdims=True) + EPS)
            gdy = g_ref[...] * dyv
            dpre = rr * gdy - pre * ((rr * rr * rr) * jnp.mean(pre * gdy, axis=-1, keepdims=True))
            dg_ref[...] += jnp.sum(dyv * (pre * rr), axis=0, keepdims=True)
            dp_ref[:, RW:] = (dpre * hv * dgl).astype(BF16)

            is_last_row = row == n - 1
            w = dpre * gl + jnp.where(is_last_row, adh[0:1, :], 0.0)
            c = jnp.where(is_last_row, 0.0, pltpu.roll(a, n - 1, 0))
            dh = _scan_bwd(c, w)
            adh[...] = (a * dh)[0:8, :]

            exth[0:8, :] = jnp.where(first_tile, 0.0, hh_ref[...])
            exth[8:, :] = hv
            da = dh * exth[pl.ds(7, n), :]
            ixc = ig * xc
            dmult = dh * ixc
            dla = da * a - dmult * ((a * a) / mult)
            dsp_ref[...] += jnp.sum(dla * (-LRU_C * r), axis=0, keepdims=True)
            dpr = (dla * (-LRU_C * sp)) * (r * (1.0 - r))
            dpi = (dh * (mult * xc)) * (ig * (1.0 - ig))
            dprb, dpib = dpr.astype(BF16), dpi.astype(BF16)
            dxc = dh * (mult * ig) + _dot(dprb, wrg, NT) + _dot(dpib, wig, NT)
            dbr_ref[...] += jnp.sum(dpr, axis=0, keepdims=True)
            dbi_ref[...] += jnp.sum(dpi, axis=0, keepdims=True)
            xc_ref[...] = xc.astype(BF16)
            dr_ref[...] = dprb
            di_ref[...] = dpib

            extd[0:n, :] = dxc
            dxr = dxc * cw_ref[3:4, :] + extd[pl.ds(1, n), :] * cw_ref[2:3, :]
            dxr = dxr + extd[pl.ds(2, n), :] * cw_ref[1:2, :] + extd[pl.ds(3, n), :] * cw_ref[0:1, :]
            extd[n:, :] = dxc[0:8, :]
            dcb_ref[...] += jnp.sum(dxc, axis=0, keepdims=True)
            for kk in range(4):
                dcw_ref[kk:kk + 1, :] += jnp.sum(dxc * ext[pl.ds(5 + kk, n), :], axis=0, keepdims=True)

            @pl.when(first_tile)
            def _():
                dsp_ref[...] = dsp_ref[...] * (-jax.nn.sigmoid(-lam_ref[...]))

            dp_ref[:, 0:RW] = dxr.astype(BF16)

    vec = _bs((1, RW), lambda i, j: (0, 0))
    mat = _bs((RW, RW), lambda i, j: (0, 0))
    tile = lambda cblk: _bs((n, RW), lambda i, j, cblk=cblk: (nt - 1 - i, cblk))
    halo = lambda cblk: _bs((8, RW), lambda i, j, cblk=cblk: (jnp.maximum((nt - 1 - i) * hb - 1, 0), cblk))
    bt = jax.ShapeDtypeStruct((T, RW), BF16)
    v = jax.ShapeDtypeStruct((1, RW), F32)
    return pl.pallas_call(
        body, name="rec_bwd", grid=(nt, 1),
        in_specs=[tile(3), halo(3), tile(4), tile(0), halo(0), tile(1), _bs((8, RW), lambda i, j: (0, 0)), vec,
                  mat, mat, vec, vec, vec, vec],
        out_specs=[_bs((n, 2 * RW), lambda i, j: (nt - 1 - i, 0)), tile(0), tile(0), tile(0),
                   _bs((8, RW), lambda i, j: (0, 0)), vec, vec, vec, vec, vec],
        out_shape=[jax.ShapeDtypeStruct((T, 2 * RW), BF16), bt, bt, bt, jax.ShapeDtypeStruct((8, RW), F32),
                   v, v, v, v, v],
        scratch_shapes=[pltpu.VMEM((n + 8, RW), F32), pltpu.VMEM((n + 8, RW), F32), pltpu.VMEM((n + 8, RW), F32),
                        pltpu.VMEM((8, RW), F32)],
        compiler_params=_cp("arbitrary", "arbitrary"),
    )(proj, proj, proj, h, h, dmix, cw, cb, wrg, wig, brg, big, lam, g_rec)


FC = 1536
TRF = 512


LC = 128


class _RowsBack:
    def __init__(self, before):
        row = lax.broadcasted_iota(jnp.int32, before.shape, 0)
        self.top1, self.top2 = row < 1, row < 2
        self.r1, self.r2 = pltpu.roll(before, 1, 0), pltpu.roll(before, 2, 0)

    def step(self, cur):
        r1, r2 = pltpu.roll(cur, 1, 0), pltpu.roll(cur, 2, 0)
        out = jnp.where(self.top1, self.r1, r1), jnp.where(self.top2, self.r2, r2)
        self.r1, self.r2 = r1, r2
        return out


def _up_act(h2, w_up, cw, cb):
    n = TRF
    nt = T // n
    pw = 256
    npc = FC // pw

    def body(h_ref, wg_ref, wu_ref, cwg_ref, cwu_ref, bg_ref, bu_ref, up_ref, a_ref, fa_ref, fb_ref, hx, gb0, gb1,
             ub0, ub1):
        i = pl.program_id(1)
        halo = h_ref[pl.ds(pl.multiple_of(jnp.maximum(i * n - 16, 0), 16), 16), :]
        hx[0:16, :] = jnp.where(i == 0, jnp.zeros_like(halo), halo)
        hx[16:, :] = h_ref[pl.ds(pl.multiple_of(i * n, n), n), :]
        gbufs, ubufs = (gb0, gb1), (ub0, ub1)

        def dots(c):
            hv = hx[...]
            gbufs[c % 2][...] = _dot(hv, wg_ref[:, c * pw:(c + 1) * pw])
            ubufs[c % 2][...] = _dot(hv, wu_ref[:, c * pw:(c + 1) * pw])

        def chain(c):
            gb, ub = gbufs[c % 2], ubufs[c % 2]
            up_ref[:, c * pw:(c + 1) * pw] = gb[16:, :]
            up_ref[:, FC + c * pw:FC + (c + 1) * pw] = ub[16:, :]
            rows8 = lambda v: jnp.broadcast_to(v, (8, LC))
            for sub in range(pw // LC):
                lc = slice(sub * LC, (sub + 1) * LC)
                cols = slice(c * pw + sub * LC, c * pw + (sub + 1) * LC)
                wg = [rows8(cwg_ref[kk:kk + 1, cols]) for kk in range(3)]
                wu = [rows8(cwu_ref[kk:kk + 1, cols]) for kk in range(3)]
                bg, bu = rows8(bg_ref[:, cols]), rows8(bu_ref[:, cols])
                g_back = _RowsBack(gb[pl.ds(8, 8), lc])
                u_back = _RowsBack(ub[pl.ds(8, 8), lc])
                for r in range(0, n, 16):
                    res, fa, fb = [], [], []
                    for rr in (16 + r, 24 + r):
                        g0, u0 = gb[pl.ds(rr, 8), lc], ub[pl.ds(rr, 8), lc]
                        g1, g2 = g_back.step(g0)
                        u1, u2 = u_back.step(u0)
                        ug = ((bg + g2 * wg[0]) + g1 * wg[1]) + g0 * wg[2]
                        uu = ((bu + u2 * wu[0]) + u1 * wu[1]) + u0 * wu[2]
                        gl, dgl = _gelu_and_grad(ug)
                        res.append(gl * uu)
                        fa.append(uu * dgl)
                        fb.append(gl)
                    a_ref[pl.ds(r, 16), cols] = jnp.concatenate(res, axis=0).astype(BF16)
                    fa_ref[pl.ds(r, 16), cols] = jnp.concatenate(fa, axis=0).astype(BF16)
                    fb_ref[pl.ds(r, 16), cols] = jnp.concatenate(fb, axis=0).astype(BF16)

        dots(0)
        for c in range(npc):
            if c + 1 < npc:
                dots(c + 1)
            chain(c)

    wsl = lambda o: _bs((None, D, FC), lambda j, i, o=o: (2 * j + o, 0, 0))
    wsp = lambda o: _bs((None, 8, FC), lambda j, i, o=o: (2 * j + o, 0, 0))
    bsp = lambda o: _bs((1, FC), lambda j, i, o=o: (0, 2 * j + o))
    return pl.pallas_call(
        body, name="up_act", grid=(2, nt),
        in_specs=[pl.BlockSpec((T, D), lambda j, i: (0, 0), pipeline_mode=pl.Buffered(1)), wsl(0), wsl(1),
                  wsp(0), wsp(1), bsp(0), bsp(1)],
        out_specs=[_bs((n, 2 * FC), lambda j, i: (i, j))] + [_bs((n, FC), lambda j, i: (i, j))] * 3,
        out_shape=[jax.ShapeDtypeStruct((T, 2 * DFF), F32)] + [jax.ShapeDtypeStruct((T, DFF), BF16)] * 3,
        scratch_shapes=[pltpu.VMEM((n + 16, D), BF16)] + [pltpu.VMEM((n + 16, pw), F32)] * 4,
        compiler_params=_cp("parallel", "arbitrary"),
    )(h2, w_up, w_up, cw, cw, cb, cb)


def _ffn_bwd(up_pre, fa, fb, dyb, w_down_t, cw, after=()):
    n = TRF
    hb = n // 8
    nt = T // n
    m = n + 8
    pw = 256
    npc = FC // pw

    def body(g_ref, gp_ref, u_ref, up_ref, fa_ref, fan_ref, fb_ref, fbn_ref, dy_ref, wd_ref, wg_ref, wu_ref, *rest):
        o_ref, dw_ref, db_ref, eg0, eu0, dug_s, duu_s, dyx, db0, db1 = rest[len(after):]
        i = pl.program_id(1)
        first, last = i == 0, i == nt - 1

        @pl.when(first)
        def _():
            dw_ref[...] = jnp.zeros_like(dw_ref)
            db_ref[...] = jnp.zeros_like(db_ref)

        tail = dy_ref[pl.ds(pl.multiple_of(jnp.minimum((i + 1) * n, T - 16), 16), 16), :]
        dyx[0:n, :] = dy_ref[pl.ds(pl.multiple_of(i * n, n), n), :]
        dyx[n:, :] = jnp.where(last, jnp.zeros_like(tail), tail)
        dbufs = (db0, db1)

        def dots(c):
            dbufs[c % 2][...] = _dot(dyx[...], wd_ref[:, c * pw:(c + 1) * pw])

        eg0[0:8, :] = jnp.where(first, 0.0, gp_ref[...])
        eg0[8:, :] = g_ref[0:8, :]
        eu0[0:8, :] = jnp.where(first, 0.0, up_ref[...])
        eu0[8:, :] = u_ref[0:8, :]

        def column(ci, dbuf, lc):
            cols = slice(ci * LC, (ci + 1) * LC)
            ucols = slice(FC + ci * LC, FC + (ci + 1) * LC)
            rows8 = lambda v: jnp.broadcast_to(v, (8, LC))
            wg = [rows8(wg_ref[kk:kk + 1, cols]) for kk in range(3)]
            wu = [rows8(wu_ref[kk:kk + 1, cols]) for kk in range(3)]
            zero = jnp.zeros((8, LC), F32)
            acc = [zero] * 8
            g_back, u_back = _RowsBack(eg0[pl.ds(0, 8), cols]), _RowsBack(eu0[pl.ds(0, 8), cols])
            for r in range(0, n + 16, 16):
                src_a, src_b, r16 = (fan_ref, fbn_ref, 0) if r == n else (fa_ref, fb_ref, r)
                fa16 = src_a[pl.ds(r16, 16), cols].astype(F32)
                fb16 = src_b[pl.ds(r16, 16), cols].astype(F32)
                for half in range(1 if r == n else 2):
                    rr = r + 8 * half
                    dv = dbuf[pl.ds(rr, 8), lc]
                    dug = dv * fa16[8 * half:8 * half + 8, :]
                    duu = dv * fb16[8 * half:8 * half + 8, :]
                    dug_s[pl.ds(rr, 8), :] = dug
                    duu_s[pl.ds(rr, 8), :] = duu
                    if rr < n:
                        g0, u0 = g_ref[pl.ds(rr, 8), cols], u_ref[pl.ds(rr, 8), cols]
                        g1, g2 = g_back.step(g0)
                        u1, u2 = u_back.step(u0)
                        gt, ut = (g2, g1, g0), (u2, u1, u0)
                        acc = [acc[0] + dug * gt[0], acc[1] + dug * gt[1], acc[2] + dug * gt[2],
                               acc[3] + duu * ut[0], acc[4] + duu * ut[1], acc[5] + duu * ut[2],
                               acc[6] + dug, acc[7] + duu]
            for r in range(0, n, 16):
                og, ou = [], []
                for rr in (r, r + 8):
                    og.append((dug_s[pl.ds(rr, 8), :] * wg[2] + dug_s[pl.ds(rr + 1, 8), :] * wg[1])
                              + dug_s[pl.ds(rr + 2, 8), :] * wg[0])
                    ou.append((duu_s[pl.ds(rr, 8), :] * wu[2] + duu_s[pl.ds(rr + 1, 8), :] * wu[1])
                              + duu_s[pl.ds(rr + 2, 8), :] * wu[0])
                o_ref[pl.ds(r, 16), cols] = jnp.concatenate(og, axis=0).astype(BF16)
                o_ref[pl.ds(r, 16), ucols] = jnp.concatenate(ou, axis=0).astype(BF16)
            for kk in range(3):
                dw_ref[kk:kk + 1, cols] += jnp.sum(acc[kk], axis=0, keepdims=True)
                dw_ref[kk:kk + 1, ucols] += jnp.sum(acc[3 + kk], axis=0, keepdims=True)
            db_ref[:, cols] += jnp.sum(acc[6], axis=0, keepdims=True)
            db_ref[:, ucols] += jnp.sum(acc[7], axis=0, keepdims=True)

        dots(0)
        for c in range(npc):
            if c + 1 < npc:
                dots(c + 1)
            for sub in range(pw // LC):
                column(c * (pw // LC) + sub, dbufs[c % 2], slice(sub * LC, (sub + 1) * LC))

    main = lambda o: _bs((n, FC), lambda j, i, o=o: (i, 2 * j + o))
    prev = lambda o: _bs((8, FC), lambda j, i, o=o: (jnp.maximum(i * hb - 1, 0), 2 * j + o))
    saved = _bs((n, FC), lambda j, i: (i, j))
    saved_next = _bs((16, FC), lambda j, i: (jnp.minimum((i + 1) * (n // 16), T // 16 - 1), j))
    wsp = lambda o: _bs((None, 8, FC), lambda j, i, o=o: (2 * j + o, 0, 0))
    return pl.pallas_call(
        body, name="ffn_bwd", grid=(2, nt),
        in_specs=[main(0), prev(0), main(1), prev(1), saved, saved_next, saved, saved_next,
                  pl.BlockSpec((T, D), lambda j, i: (0, 0), pipeline_mode=pl.Buffered(1)),
                  _bs((D, FC), lambda j, i: (0, j)), wsp(0), wsp(1)]
        + [pl.BlockSpec(memory_space=pl.ANY)] * len(after),
        out_specs=[_bs((n, 2 * FC), lambda j, i: (i, j)), _bs((8, 2 * FC), lambda j, i: (0, j)),
                   _bs((1, 2 * FC), lambda j, i: (0, j))],
        out_shape=[jax.ShapeDtypeStruct((T, 2 * DFF), BF16), jax.ShapeDtypeStruct((8, 2 * DFF), F32),
                   jax.ShapeDtypeStruct((1, 2 * DFF), F32)],
        scratch_shapes=[pltpu.VMEM((16, FC), F32)] * 2 + [pltpu.VMEM((m, LC), F32)] * 2
        + [pltpu.VMEM((n + 16, D), BF16)] + [pltpu.VMEM((n + 16, pw), F32)] * 2,
        compiler_params=_cp("parallel", "arbitrary"),
    )(up_pre, up_pre, up_pre, up_pre, fa, fa, fb, fb, dyb, w_down_t, cw, cw, *after)


def _down_loss(act, w_down, x1, target):
    tm, tn = 512, D

    def body(a_ref, b_ref, r_ref, t_ref, dy_ref, dyb_ref, l_ref):
        @pl.when((pl.program_id(0) == 0) & (pl.program_id(1) == 0))
        def _():
            l_ref[...] = jnp.zeros_like(l_ref)

        err = (r_ref[...] + _dot(a_ref[...], b_ref[...])) - t_ref[...]
        dy = err * (1.0 / D)
        dy_ref[...] = dy
        dyb_ref[...] = dy.astype(BF16)
        l_ref[...] += jnp.sum(0.5 * (err * err) * (1.0 / D))

    o_spec = _bs((tm, tn), lambda j, i: (i, j))
    return pl.pallas_call(
        body, name="down_loss", grid=(D // tn, T // tm),
        in_specs=[_bs((tm, DFF), lambda j, i: (i, 0)),
                  pl.BlockSpec((DFF, tn), lambda j, i: (0, j), pipeline_mode=pl.Buffered(1)), o_spec, o_spec],
        out_specs=[o_spec, o_spec, _bs((8, 128), lambda j, i: (0, 0))],
        out_shape=[jax.ShapeDtypeStruct((T, D), F32), jax.ShapeDtypeStruct((T, D), BF16),
                   jax.ShapeDtypeStruct((8, 128), F32)],
        compiler_params=_cp("arbitrary", "arbitrary"),
    )(act, w_down, x1, target)


def _block_diag(w):
    eye = jnp.eye(8, dtype=w.dtype)
    return (w[:, :, None, :] * eye[:, None, :, None]).reshape(RW, RW).astype(BF16)


def _diag_blocks(m):
    eye = jnp.eye(8, dtype=m.dtype)
    return (m.reshape(8, HD, 8, HD) * eye[:, None, :, None]).sum(axis=2)


def _local_step(x, pos_col, target, p, exch):
    qg, kg = jnp.tile(p["q_norm_g"], (1, 8)), jnp.tile(p["k_norm_g"], (1, 8))
    wrg, wig = _block_diag(p["w_rg"]), _block_diag(p["w_ig"])
    brg, big = p["b_rg"].reshape(1, RW), p["b_ig"].reshape(1, RW)

    h1 = _rms_fwd("rms1", x, p["g_mix"])
    p = {**p, **exch.wait_first(h1)}
    proj = _mm("mm_in", h1, p["w_in"], "nn", 1024, 640, stack=NCHIP, after=exch.start_rest(), a_full=True)
    q, k, cos_t, sin_t = _qk_prep(proj, pos_col, qg, kg)
    attn, lse = _attn_fwd(q, k, proj)
    mix = _attn_norm(attn, p["g_attn_out"])
    mix, hseq = _rec_fwd(proj, mix, p["rec_conv_w"], p["rec_conv_b"], wrg, wig, brg, big, p["lru_lambda"], p["g_rec_out"])
    rest = exch.wait_rest(mix)
    x1, h2 = _out_rms(mix, rest["w_out"], x, p["g_ffn"])
    up_pre, act, fa, fb = _up_act(h2, rest["w_up"], p["ffn_conv_w"], p["ffn_conv_b"])
    dy, dyb, loss_blk = _down_loss(act, rest["w_down"], x1, target)

    g = {}
    tok = exch.reduce_start("w_down", *_mm("wg_down", act, dyb, "tn", 512, 512, twin_bf16=True))
    dup, g["ffn_conv_w"], g["ffn_conv_b"] = _ffn_bwd(up_pre, fa, fb, dyb, rest["w_down"].T, p["ffn_conv_w"], tok)
    tok = exch.reduce_start("w_up", *_mm("wg_up", h2, dup, "tn", 512, 768, stack=NCHIP, twin_bf16=True, a_full=True))
    dx1, dx1b, g["g_ffn"] = _dgrad_rms_bwd("dg_up", dup, rest["w_up"], x1, p["g_ffn"], dy, True, after=tok)
    tok = exch.reduce_start("w_out", *_mm("wg_out", mix, dx1b, "tn", 512, 512, twin_bf16=True, a_full=True))
    dmix = _mm("dg_out", dx1b, rest["w_out"], "nt", 1024, 512, after=tok, a_full=True)
    do, delta, g["g_attn_out"] = _attn_out_bwd(attn, dmix, p["g_attn_out"])
    dq, dk, dv = _attn_bwd(q, k, proj, do, lse, delta)
    dqkv, dqg, dkg = _qk_bwd(proj, cos_t, sin_t, qg, kg, dq, dk, dv)
    (drec, xcb, dprb, dpib, g["rec_conv_w"], g["rec_conv_b"], dbr, dbi, dsp, g["g_rec_out"]) = _rec_bwd(
        proj, hseq, dmix, p["rec_conv_w"], p["rec_conv_b"], wrg, wig, brg, big, p["lru_lambda"], p["g_rec_out"])
    dproj = jnp.concatenate([dqkv, drec], axis=1)
    g["w_rg"] = _diag_blocks(_mm("wg_rg", xcb, dprb, "tn", 512, 512)).reshape(RW, HD)
    g["w_ig"] = _diag_blocks(_mm("wg_ig", xcb, dpib, "tn", 512, 512)).reshape(RW, HD)
    g["b_rg"], g["b_ig"] = dbr.reshape(8, HD), dbi.reshape(8, HD)
    g["lru_lambda"] = dsp
    g["q_norm_g"] = dqg.reshape(8, HD).sum(axis=0, keepdims=True)
    g["k_norm_g"] = dkg.reshape(8, HD).sum(axis=0, keepdims=True)
    tok = exch.reduce_start("w_in", *_mm("wg_in", h1, dproj, "tn", 512, 640, stack=NCHIP, twin_bf16=True, a_full=True))
    grad_x, g["g_mix"] = _dgrad_rms_bwd("dg_in", dproj, p["w_in"], x, p["g_mix"], dx1, False, after=tok)
    return loss_blk, grad_x, g


ANY = pl.BlockSpec(memory_space=pl.ANY)


def _mesh_pos():
    return lax.axis_index("x"), lax.axis_index("y"), lax.axis_index("c")


def _slot(px, py, perm):
    return 2 * py + px if perm else 2 * px + py


def _other_chips(x, y):
    return [(1 - x, y), (x, 1 - y), (1 - x, 1 - y)]


def _rcopy(src, dst, send, recv, k, to, kr=None):
    return pltpu.make_async_remote_copy(src_ref=src, dst_ref=dst, send_sem=send.at[k],
                                        recv_sem=recv.at[k if kr is None else kr], device_id=to, device_id_type=MESH)


def _cast_bf16(name, w, after=()):
    r, c = w.shape
    tr = 256

    def body(w_ref, *rest):
        rest[-1][...] = w_ref[...].astype(BF16)

    return pl.pallas_call(
        body, name=name, grid=(r // tr,), in_specs=[_bs((tr, c), lambda i: (i, 0))] + [ANY] * len(after),
        out_specs=_bs((tr, c), lambda i: (i, 0)), out_shape=jax.ShapeDtypeStruct((r, c), BF16),
        compiler_params=_cp("parallel"),
    )(w, *after)


def _sibling_fill(lands, perms):
    na = len(lands)

    def body(*refs):
        outs, (send, recv) = refs[na:2 * na], refs[2 * na:]
        x, y, c = _mesh_pos()
        cps = []
        for a in range(na):
            for j, (px, py) in enumerate(_other_chips(x, y)):
                mine = outs[a].at[_slot(px, py, perms[a]), c]
                cps.append(_rcopy(mine, mine, send, recv, 3 * a + j, (x, y, 1 - c)))
        for cp in cps:
            cp.start()
        for a in range(na):
            for j, (px, py) in enumerate(_other_chips(x, y)):
                got = outs[a].at[_slot(px, py, perms[a]), 1 - c]
                _rcopy(got, got, send, recv, 3 * a + j, (x, y, c)).wait_recv()
        for cp in cps:
            cp.wait_send()

    return pl.pallas_call(
        body, name="gather_fill", in_specs=[ANY] * na, out_specs=[ANY] * na,
        out_shape=[jax.ShapeDtypeStruct(a.shape, a.dtype) for a in lands],
        input_output_aliases={i: i for i in range(na)},
        scratch_shapes=[pltpu.SemaphoreType.DMA((3 * na,)), pltpu.SemaphoreType.DMA((3 * na,))],
    )(*lands)


HBM = pl.BlockSpec(memory_space=pltpu.HBM)
SEM = pl.BlockSpec(memory_space=pltpu.SEMAPHORE)
EFFECT = pltpu.SideEffectType.DATAFLOW_SIDE_EFFECTING


def _split_start(name, srcs, lands, plan, nsem):
    ns, nl = len(srcs), len(lands)

    def body(*refs):
        send, recv = refs[ns + nl], refs[ns + nl + 1]
        sends, _ = plan(refs[:ns], refs[ns:ns + nl], send, recv)
        for cp in sends:
            cp.start()
        refs[-1][...] = jnp.zeros((8, 128), F32)

    arrs = list(srcs) + list(lands)
    out = pl.pallas_call(
        body, name=name, in_specs=[HBM] * (ns + nl),
        out_specs=[SEM, SEM] + [HBM] * (ns + nl) + [pl.BlockSpec(memory_space=pltpu.VMEM)],
        out_shape=[pltpu.SemaphoreType.DMA((nsem,)), pltpu.SemaphoreType.DMA((nsem,))]
        + [pltpu.HBM(a.shape, a.dtype) for a in arrs] + [jax.ShapeDtypeStruct((8, 128), F32)],
        input_output_aliases={i: 2 + i for i in range(ns + nl)},
        compiler_params=pltpu.CompilerParams(has_side_effects=EFFECT),
    )(*[pltpu.with_memory_space_constraint(a, pltpu.HBM) for a in arrs])
    return out[0], out[1], out[2:2 + ns], out[2 + ns:2 + ns + nl], out[-1]


def _split_wait(name, send, recv, srcs, lands, plan, after):
    ns, nl = len(srcs), len(lands)

    def body(*refs):
        sends, recvs = plan(refs[:ns], refs[ns:ns + nl], refs[ns + nl], refs[ns + nl + 1])
        for cp in sends:
            cp.wait_send()
        for cp in recvs:
            cp.wait_recv()

    arrs = list(srcs) + list(lands)
    after = tuple(after) if isinstance(after, (tuple, list)) else (after,)
    out = pl.pallas_call(
        body, name=name, in_specs=[HBM] * (ns + nl) + [SEM, SEM] + [ANY] * len(after), out_specs=[HBM] * (ns + nl),
        out_shape=[pltpu.HBM(a.shape, a.dtype) for a in arrs],
        input_output_aliases={i: i for i in range(ns + nl)},
        compiler_params=pltpu.CompilerParams(has_side_effects=EFFECT),
    )(*arrs, send, recv, *after)
    return out[ns:]


def _gather_plan(perms):
    def plan(srcs, lands, send, recv):
        x, y, c = _mesh_pos()
        sends, recvs = [], []
        for a, perm in enumerate(perms):
            for j, (px, py) in enumerate(_other_chips(x, y)):
                for cc in (0, 1):
                    k = 6 * a + 2 * j + cc
                    sends.append(_rcopy(srcs[a].at[c], lands[a].at[_slot(x, y, perm), c], send, recv, k, (px, py, cc),
                                        kr=6 * a + 2 * j + c))
                    got = lands[a].at[_slot(px, py, perm), cc]
                    recvs.append(_rcopy(got, got, send, recv, k, (x, y, c)))
        return sends, recvs
    return plan


def _gather_half_plan(perms, halved):
    def plan(srcs, lands, send, recv):
        x, y, c = _mesh_pos()
        sends, recvs = [], []
        for a, perm in enumerate(perms):
            for j, (px, py) in enumerate(_other_chips(x, y)):
                k = 3 * a + j
                mine, theirs = _slot(x, y, perm), _slot(px, py, perm)
                if halved[a]:
                    sends.append(_rcopy(srcs[a].at[c], lands[a].at[mine, c], send, recv, k, (px, py, c)))
                    got = lands[a].at[theirs, c]
                else:
                    sends.append(_rcopy(srcs[a], lands[a].at[mine], send, recv, k, (px, py, c)))
                    got = lands[a].at[theirs]
                recvs.append(_rcopy(got, got, send, recv, k, (x, y, c)))
        return sends, recvs
    return plan


def _reduce_plan(perm):
    def plan(srcs, lands, send, recv):
        x, y, c = _mesh_pos()
        src, land = srcs[0], lands[0]
        sends = []
        for j, (px, py) in enumerate(_other_chips(x, y)):
            for hf in (0, 1):
                sends.append(_rcopy(src.at[_slot(px, py, perm), hf], land.at[2 * j + c], send, recv, 2 * j + hf,
                                    (px, py, hf), kr=2 * j + c))
        sends.append(_rcopy(src.at[_slot(x, y, perm), 1 - c], land.at[6], send, recv, 6, (x, y, 1 - c)))
        recvs = [_rcopy(land.at[i], land.at[i], send, recv, i, (x, y, c)) for i in range(7)]
        return sends, recvs
    return plan


def _sibling_share(rs):
    na = len(rs)

    def body(*refs):
        ins, outs, (send, recv) = refs[:na], refs[na:2 * na], refs[2 * na:]
        x, y, c = _mesh_pos()
        cps = [_rcopy(ins[a], outs[a], send, recv, a, (x, y, 1 - c)) for a in range(na)]
        for cp in cps:
            cp.start()
        for cp in cps:
            cp.wait()

    return pl.pallas_call(
        body, name="rs_share", in_specs=[ANY] * na, out_specs=[ANY] * na,
        out_shape=[jax.ShapeDtypeStruct(r.shape, F32) for r in rs],
        scratch_shapes=[pltpu.SemaphoreType.DMA((na,)), pltpu.SemaphoreType.DMA((na,))],
    )(*rs)


def _add_pieces(name, g, got, where):
    _, _, r2, cc = g.shape
    tr = 256 if r2 % 256 == 0 else 128

    def body(w_ref, g_ref, r_ref, o_ref):
        del w_ref
        acc = g_ref[...]
        for i in range(7):
            acc = acc + r_ref[i].astype(F32)
        o_ref[...] = acc

    return pl.pallas_call(
        body, name=name,
        grid_spec=pltpu.PrefetchScalarGridSpec(
            num_scalar_prefetch=1, grid=(r2 // tr,),
            in_specs=[_bs((None, None, tr, cc), lambda i, w_ref: (w_ref[0], w_ref[1], i, 0)),
                      _bs((7, tr, cc), lambda i, w_ref: (0, i, 0))],
            out_specs=_bs((tr, cc), lambda i, w_ref: (i, 0))),
        out_shape=jax.ShapeDtypeStruct((r2, cc), F32), compiler_params=_cp("parallel"),
    )(where, g, got)


def _adam_math(w, g, m, v):
    m = ADAM_B1 * m + (1.0 - ADAM_B1) * g
    v = ADAM_B2 * v + (1.0 - ADAM_B2) * (g * g)
    m_hat = m / (1.0 - ADAM_B1 ** ADAM_STEP)
    v_hat = v / (1.0 - ADAM_B2 ** ADAM_STEP)
    return -ADAM_LR * (m_hat / (jnp.sqrt(v_hat) + ADAM_EPS) + ADAM_WD * w), m, v


def _adam_big(name, w, g_mine, g_sib, m, v, c_arr):
    r, cols = w.shape
    tr = 256 if (r // 2) % 256 == 0 else 128
    per = r // 2 // tr

    def body(c_ref, w_ref, a_ref, b_ref, m_ref, v_ref, g_ref, d_ref, m2_ref, v2_ref):
        g = jnp.where(pl.program_id(0) == c_ref[0], a_ref[...], b_ref[...])
        g_ref[...] = g
        d_ref[...], m2_ref[...], v2_ref[...] = _adam_math(w_ref[...], g, m_ref[...], v_ref[...])

    spec = _bs((tr, cols), lambda h, i, c_ref: (h * per + i, 0))
    half = _bs((tr, cols), lambda h, i, c_ref: (i, 0))
    out = jax.ShapeDtypeStruct((r, cols), F32)
    return pl.pallas_call(
        body, name=name,
        grid_spec=pltpu.PrefetchScalarGridSpec(
            num_scalar_prefetch=1, grid=(2, per), in_specs=[spec, half, half, spec, spec], out_specs=[spec] * 4),
        out_shape=[out] * 4, compiler_params=_cp("parallel", "parallel"),
    )(c_arr, w, g_mine, g_sib, m, v)


_CLASS_SHAPE = {"a": (8, D), "b": (8, RW), "c": (8, 2 * DFF), "d": (1048, HD)}
_SMALL = (
    ("g_mix", "a", 0, 1, D), ("g_ffn", "a", 1, 1, D),
    ("rec_conv_w", "b", 0, 4, RW), ("rec_conv_b", "b", 4, 1, RW), ("lru_lambda", "b", 5, 1, RW),
    ("g_attn_out", "b", 6, 1, RW), ("g_rec_out", "b", 7, 1, RW),
    ("ffn_conv_w", "c", 0, 3, 2 * DFF), ("ffn_conv_b", "c", 3, 1, 2 * DFF),
    ("w_rg", "d", 0, RW, HD), ("w_ig", "d", RW, RW, HD), ("b_rg", "d", 2 * RW, 8, HD), ("b_ig", "d", 2 * RW + 8, 8, HD),
    ("q_norm_g", "d", 2 * RW + 16, 1, HD), ("k_norm_g", "d", 2 * RW + 17, 1, HD),
)
_LOSS_ROW = 2
_CLASSES = ("a", "b", "c", "d")
_CLASS_OWNER = {"a": 0, "b": 0, "c": 0, "d": 1}


def _small_allreduce(g, loss_blk):
    names = [s[0] for s in _SMALL]
    nin = len(names) + 1

    def body(*refs):
        ins = dict(zip(names, refs[:len(names)]))
        loss_ref = refs[len(names)]
        outs = dict(zip(_CLASSES, refs[nin:nin + 4]))
        pair = dict(zip(_CLASSES, refs[nin + 4:nin + 8]))
        quad = dict(zip(_CLASSES, refs[nin + 8:nin + 12]))
        send, recv = refs[nin + 12:]
        x, y, c = _mesh_pos()
        chip = 2 * x + y
        pair["a"][c] = jnp.zeros(_CLASS_SHAPE["a"], F32)
        pair["b"][c] = ins["rec_conv_w"][...]
        pair["c"][c] = ins["ffn_conv_w"][...]
        pair["d"][c, 2 * RW + 16:, :] = jnp.zeros((8, HD), F32)
        for name, k, r0, nr, _ in _SMALL:
            if name in ("rec_conv_w", "ffn_conv_w"):
                continue
            pair[k][c, r0:r0 + nr, :] = ins[name][...]
        pair["a"][c, _LOSS_ROW:_LOSS_ROW + 1, :] = jnp.broadcast_to(loss_ref[0:1, 0:1], (1, D))
        cps = [_rcopy(pair[k].at[c], pair[k].at[c], send, recv, ki, (x, y, 1 - c)) for ki, k in enumerate(_CLASSES)]
        for cp in cps:
            cp.start()
        for ki, k in enumerate(_CLASSES):
            _rcopy(pair[k].at[1 - c], pair[k].at[1 - c], send, recv, ki, (x, y, c)).wait_recv()
            quad[k][chip] = pair[k][0] + pair[k][1]
        for cp in cps:
            cp.wait_send()
        for ki, k in enumerate(_CLASSES):
            owner = _CLASS_OWNER[k]

            @pl.when(c == owner)
            def _(ki=ki, k=k):
                cps2 = [_rcopy(quad[k].at[chip], quad[k].at[chip], send, recv, 4 + 3 * ki + j, (px, py, c))
                        for j, (px, py) in enumerate(_other_chips(x, y))]
                for cp in cps2:
                    cp.start()
                for j, (px, py) in enumerate(_other_chips(x, y)):
                    got = quad[k].at[2 * px + py]
                    _rcopy(got, got, send, recv, 4 + 3 * ki + j, (x, y, c)).wait_recv()
                outs[k][...] = ((quad[k][0] + quad[k][1]) + quad[k][2]) + quad[k][3]
                share = _rcopy(outs[k], outs[k], send, recv, 16 + ki, (x, y, 1 - c))
                share.start()
                for cp in cps2:
                    cp.wait_send()
                share.wait_send()

        for ki, k in enumerate(_CLASSES):
            @pl.when(c != _CLASS_OWNER[k])
            def _(ki=ki, k=k):
                _rcopy(outs[k], outs[k], send, recv, 16 + ki, (x, y, c)).wait_recv()

    vm = pl.BlockSpec(memory_space=pltpu.VMEM)
    return pl.pallas_call(
        body, name="small_allreduce", in_specs=[vm] * nin, out_specs=[vm] * 4,
        out_shape=[jax.ShapeDtypeStruct(_CLASS_SHAPE[k], F32) for k in _CLASSES],
        scratch_shapes=[pltpu.VMEM((2,) + _CLASS_SHAPE[k], F32) for k in _CLASSES]
        + [pltpu.VMEM((NCHIP,) + _CLASS_SHAPE[k], F32) for k in _CLASSES]
        + [pltpu.SemaphoreType.DMA((20,)), pltpu.SemaphoreType.DMA((20,))],
        compiler_params=pltpu.CompilerParams(vmem_limit_bytes=VMEM_LIMIT),
    )(*[g[n] for n in names], loss_blk)


def _adam_small(red, w, m, v):
    names = [s[0] for s in _SMALL]
    n = len(names)

    def body(*refs):
        red_refs = dict(zip(_CLASSES, refs[:4]))
        w_refs, m_refs, v_refs = refs[4:4 + n], refs[4 + n:4 + 2 * n], refs[4 + 2 * n:4 + 3 * n]
        loss_ref = refs[4 + 3 * n]
        out_refs = refs[5 + 3 * n:]
        x, y, _ = _mesh_pos()
        chip = 2 * x + y
        loss_ref[...] = jnp.broadcast_to(red_refs["a"][_LOSS_ROW:_LOSS_ROW + 1, 0:1], loss_ref.shape)
        for pi, (name, k, r0, nr, width) in enumerate(_SMALL):
            gfull = red_refs[k][r0:r0 + nr, :]
            if name == "rec_conv_w":
                parts = [gfull[:, 128 * s:128 * (s + 1)] for s in range(NCHIP)]
                g = jnp.where(chip == 0, parts[0], jnp.where(chip == 1, parts[1], jnp.where(chip == 2, parts[2], parts[3])))
            elif name == "ffn_conv_w":
                parts = [gfull[:, FC * s:FC * (s + 1)] for s in range(NCHIP)]
                g = jnp.where(chip == 0, parts[0], jnp.where(chip == 1, parts[2], jnp.where(chip == 2, parts[1], parts[3])))
            elif name == "ffn_conv_b":
                g = jnp.concatenate([gfull[:, FC * s:FC * (s + 1)] for s in (0, 2, 1, 3)], axis=1)
            else:
                g = gfull
            d, m2, v2 = _adam_math(w_refs[pi][...], g, m_refs[pi][...], v_refs[pi][...])
            o = out_refs[4 * pi:4 * pi + 4]
            o[0][...], o[1][...], o[2][...], o[3][...] = g, d, m2, v2

    vm = pl.BlockSpec(memory_space=pltpu.VMEM)
    outs = [jax.ShapeDtypeStruct((1, 128), F32)]
    for name in names:
        outs += [jax.ShapeDtypeStruct(w[name].shape, F32)] * 4
    res = pl.pallas_call(
        body, name="adam_small", in_specs=[vm] * (4 + 3 * n), out_specs=[vm] * len(outs), out_shape=outs,
        compiler_params=pltpu.CompilerParams(vmem_limit_bytes=VMEM_LIMIT),
    )(*red, *[w[k] for k in names], *[m[k] for k in names], *[v[k] for k in names])
    return res[0], {name: res[1 + 4 * i:5 + 4 * i] for i, name in enumerate(names)}


_WEIGHTS = ("g_mix", "w_in", "q_norm_g", "k_norm_g", "rec_conv_w", "rec_conv_b", "w_rg", "b_rg", "w_ig", "b_ig",
            "lru_lambda", "g_attn_out", "g_rec_out", "w_out", "g_ffn", "w_up", "ffn_conv_w", "ffn_conv_b", "w_down")
_BIG = ("w_in", "w_out", "w_up", "w_down")
_BIG_PERM = {"w_in": False, "w_out": False, "w_up": True, "w_down": False}
_SMALL_2D = {"w_rg": (RW, HD), "w_ig": (RW, HD), "b_rg": (8, HD), "b_ig": (8, HD), "rec_conv_w": (4, 128),
             "ffn_conv_w": (3, FC)}


def _halves(a):
    r, c = a.shape
    return a.reshape(2, r // 2, c)


def kernel(x, positions, g_mix, w_in, q_norm_g, k_norm_g, rec_conv_w, rec_conv_b, w_rg, b_rg, w_ig, b_ig, lru_lambda, g_attn_out, g_rec_out, w_out, g_ffn, w_up, ffn_conv_w, ffn_conv_b, w_down, loss_target, m_g_mix, m_w_in, m_q_norm_g, m_k_norm_g, m_rec_conv_w, m_rec_conv_b, m_w_rg, m_b_rg, m_w_ig, m_b_ig, m_lru_lambda, m_g_attn_out, m_g_rec_out, m_w_out, m_g_ffn, m_w_up, m_ffn_conv_w, m_ffn_conv_b, m_w_down, v_g_mix, v_w_in, v_q_norm_g, v_k_norm_g, v_rec_conv_w, v_rec_conv_b, v_w_rg, v_b_rg, v_w_ig, v_b_ig, v_lru_lambda, v_g_attn_out, v_g_rec_out, v_w_out, v_g_ffn, v_w_up, v_ffn_conv_w, v_ffn_conv_b, v_w_down):
    given = dict(g_mix=g_mix, w_in=w_in, q_norm_g=q_norm_g, k_norm_g=k_norm_g, rec_conv_w=rec_conv_w, rec_conv_b=rec_conv_b, w_rg=w_rg, b_rg=b_rg, w_ig=w_ig, b_ig=b_ig, lru_lambda=lru_lambda, g_attn_out=g_attn_out, g_rec_out=g_rec_out, w_out=w_out, g_ffn=g_ffn, w_up=w_up, ffn_conv_w=ffn_conv_w, ffn_conv_b=ffn_conv_b, w_down=w_down)
    given_m = dict(g_mix=m_g_mix, w_in=m_w_in, q_norm_g=m_q_norm_g, k_norm_g=m_k_norm_g, rec_conv_w=m_rec_conv_w, rec_conv_b=m_rec_conv_b, w_rg=m_w_rg, b_rg=m_b_rg, w_ig=m_w_ig, b_ig=m_b_ig, lru_lambda=m_lru_lambda, g_attn_out=m_g_attn_out, g_rec_out=m_g_rec_out, w_out=m_w_out, g_ffn=m_g_ffn, w_up=m_w_up, ffn_conv_w=m_ffn_conv_w, ffn_conv_b=m_ffn_conv_b, w_down=m_w_down)
    given_v = dict(g_mix=v_g_mix, w_in=v_w_in, q_norm_g=v_q_norm_g, k_norm_g=v_k_norm_g, rec_conv_w=v_rec_conv_w, rec_conv_b=v_rec_conv_b, w_rg=v_w_rg, b_rg=v_b_rg, w_ig=v_w_ig, b_ig=v_b_ig, lru_lambda=v_lru_lambda, g_attn_out=v_g_attn_out, g_rec_out=v_g_rec_out, w_out=v_w_out, g_ffn=v_g_ffn, w_up=v_w_up, ffn_conv_w=v_ffn_conv_w, ffn_conv_b=v_ffn_conv_b, w_down=v_w_down)
    shapes = {n: a.shape for n, a in given.items()}

    def two_d(n, a):
        a = a[0]
        return a.reshape(_SMALL_2D[n]) if n in _SMALL_2D else (a if a.ndim == 2 else a[None])

    w = {n: two_d(n, a) for n, a in given.items()}
    m = {n: two_d(n, a) for n, a in given_m.items()}
    v = {n: two_d(n, a) for n, a in given_v.items()}
    cc = lax.axis_index("c").astype(jnp.int32)
    cx, cy = lax.axis_index("x").astype(jnp.int32), lax.axis_index("y").astype(jnp.int32)
    slot = {False: 2 * cx + cy, True: 2 * cy + cx}

    shards = {"w_in": _halves(_cast_bf16("cast_w_in", w["w_in"]))}
    first = [shards["w_in"], jnp.pad(w["ffn_conv_w"], ((0, 5), (0, 0))), jnp.pad(w["rec_conv_w"], ((0, 4), (0, 0)))]
    first_perm = [False, True, False]
    first_plan = _gather_half_plan(first_perm, [True, False, False])
    in_flight = _split_start(
        "gather_in_start", first,
        [lax.dynamic_update_slice(lax.empty((NCHIP,) + a.shape, a.dtype), a[None], (slot[pm],) + (0,) * a.ndim)
         for a, pm in zip(first, first_perm)], first_plan, 3 * len(first))
    for n in ("w_out", "w_up", "w_down"):
        shards[n] = _halves(_cast_bf16(f"cast_{n}", w[n], after=(in_flight[4],)))
    p = {n: w[n] for n in ("g_mix", "g_ffn", "q_norm_g", "k_norm_g", "rec_conv_b", "lru_lambda", "g_attn_out", "g_rec_out")}
    p.update(w_rg=w["w_rg"].reshape(8, HD, HD), w_ig=w["w_ig"].reshape(8, HD, HD), b_rg=w["b_rg"], b_ig=w["b_ig"],
             ffn_conv_b=jnp.concatenate([w["ffn_conv_b"][:, FC * s:FC * (s + 1)] for s in (0, 2, 1, 3)], axis=1))

    class Exchange:
        rest = ("w_out", "w_up", "w_down")
        order = []
        flight = {}

        def wait_first(self, after):
            send, recv, srcs, lands, _ = in_flight
            f_in, f_fcw, f_rcw = _split_wait("gather_in_wait", send, recv, srcs, lands, first_plan,
                                             (after,) + tuple(shards[n] for n in self.rest))
            (f_in,) = _sibling_fill([f_in], [False])
            return dict(w_in=f_in.reshape(NCHIP, D, INW // NCHIP), ffn_conv_w=f_fcw,
                        rec_conv_w=f_rcw.transpose(1, 0, 2).reshape(8, RW))

        def start_rest(self):
            srcs = [shards[n] for n in self.rest]
            lands = [lax.dynamic_update_slice(lax.empty((NCHIP,) + s.shape, BF16), s[None], (slot[_BIG_PERM[n]], 0, 0, 0))
                     for n, s in zip(self.rest, srcs)]
            plan = _gather_plan([_BIG_PERM[n] for n in self.rest])
            send, recv, srcs, lands, token = _split_start("gather_rest_start", srcs, lands, plan, 6 * len(srcs))
            self.flight["rest"] = (send, recv, srcs, lands, plan)
            return (token,)

        def wait_rest(self, after):
            send, recv, srcs, lands, plan = self.flight.pop("rest")
            f_out, f_up, f_down = _split_wait("gather_rest_wait", send, recv, srcs, lands, plan, after)
            return dict(w_out=f_out.reshape(D, D), w_up=f_up.reshape(NCHIP, D, FC), w_down=f_down.reshape(DFF, D))

        def reduce_start(self, name, g32, g16):
            r2, cols = shards[name].shape[1:]
            plan = _reduce_plan(_BIG_PERM[name])
            send, recv, srcs, lands, token = _split_start(
                f"reduce_{name}_start", [g16.reshape(NCHIP, 2, r2, cols)], [lax.empty((7, r2, cols), BF16)], plan, 7)
            self.flight[name] = (send, recv, srcs, lands, plan, g32.reshape(NCHIP, 2, r2, cols))
            self.order.append(name)
            return (token,)

        def finish(self, after):
            mine = {}
            for name in self.order:
                send, recv, srcs, lands, plan, g32 = self.flight.pop(name)
                (got,) = _split_wait(f"reduce_{name}_wait", send, recv, srcs, lands, plan, after)
                where = jnp.stack([slot[_BIG_PERM[name]], cc])
                mine[name] = after = _add_pieces(f"reduce_{name}_add", g32, got, where)
            theirs = dict(zip(_BIG, _sibling_share([mine[n] for n in _BIG])))
            return mine, theirs

    exch = Exchange()

    loss_blk, grad_x, g = _local_step(x[0], positions.reshape(T, 1), loss_target[0], p, exch)

    out_g, out_d, out_m, out_v = {}, {}, {}, {}
    red = _small_allreduce(g, loss_blk)
    loss_row, small_out = _adam_small(red, w, m, v)
    for n, (gn, dn, mn, vn) in small_out.items():
        out_g[n], out_d[n], out_m[n], out_v[n] = gn, dn, mn, vn

    mine, theirs = exch.finish(red[0])
    for n in _BIG:
        out_g[n], out_d[n], out_m[n], out_v[n] = _adam_big(f"adam_{n}", w[n], mine[n], theirs[n], m[n], v[n], cc.reshape(1))

    outs = [loss_row[0, 0], grad_x[None]]
    for group in (out_g, out_d, out_m, out_v):
        outs += [group[n].reshape(shapes[n]) for n in _WEIGHTS]
    return tuple(outs)
```

```python
import math

import jax
import jax.numpy as jnp
import numpy as np
from jax import lax
from jax.experimental import pallas as pl
from jax.experimental.pallas import tpu as pltpu

F32 = jnp.float32
BF16 = jnp.bfloat16

T = 4096
D = 1024
HD = 64
AW = 512
RW = 512
INW = 2560
DFF = 3072
NCHIP = 4
EPS = 1e-6
NEG = -1e30
LRU_C = 8.0
ROPE_THETA = 10000.0
BLK = 128
DILATIONS = (1, 4, 16)
ADAM_LR, ADAM_B1, ADAM_B2, ADAM_EPS, ADAM_WD, ADAM_STEP = 0.001, 0.9, 0.999, 1e-08, 0.01, 10
VMEM_LIMIT = 56 * 1024 * 1024
MESH = pl.DeviceIdType.MESH

NN = (((1,), (0,)), ((), ()))
NT = (((1,), (1,)), ((), ()))
TN = (((0,), (0,)), ((), ()))


def _cp(*sem):
    return pltpu.CompilerParams(dimension_semantics=sem, vmem_limit_bytes=VMEM_LIMIT)


def _bs(shape, fn):
    return pl.BlockSpec(shape, fn)


def _dot(a, b, dims=NN):
    return lax.dot_general(a, b, dims, preferred_element_type=F32)


_GC = math.sqrt(2.0 / math.pi)


def _gelu(x):
    return x * (0.5 + 0.5 * jnp.tanh(x * (_GC + (_GC * 0.044715) * (x * x))))


def _gelu_and_grad(x):
    x2 = x * x
    th = jnp.tanh(x * (_GC + (_GC * 0.044715) * x2))
    cdf = 0.5 + 0.5 * th
    dg = cdf + (x * (1.0 - th * th)) * ((0.5 * _GC) + (1.5 * 0.044715 * _GC) * x2)
    return x * cdf, dg


def _softplus(x):
    e = jnp.exp(-jnp.abs(x))
    u = 1.0 + e
    l1p = jnp.where(u == 1.0, e, jnp.log(u) * (e / (u - 1.0)))
    return jnp.maximum(x, 0.0) + l1p


def _segsum(z, e_bf16):
    hi = z.astype(BF16)
    lo = (z - hi.astype(F32)).astype(BF16)
    parts = []
    for c0 in range(0, z.shape[1], 128):
        parts.append(_dot(hi[:, c0:c0 + 128], e_bf16) + _dot(lo[:, c0:c0 + 128], e_bf16))
    return jnp.concatenate(parts, axis=1)


def _mm(name, a, b, mode, tm, tn, out_dtype=F32, stack=0, twin_bf16=False, after=(), a_full=False):
    if mode == "nn":
        (m, k), n = a.shape, (b.shape[1] if not stack else stack * b.shape[2])
        a_spec = _bs((tm, k), lambda j, i: (i, 0))
        if stack:
            per = b.shape[2] // tn
            b_spec = _bs((None, k, tn), lambda j, i: (j // per, 0, j % per))
        else:
            b_spec = _bs((k, tn), lambda j, i: (0, j))
    elif mode == "nt":
        (m, k), n = a.shape, b.shape[0]
        a_spec, b_spec = _bs((tm, k), lambda j, i: (i, 0)), _bs((tn, k), lambda j, i: (j, 0))
    else:
        (k, m), n = a.shape, b.shape[1]
        a_spec, b_spec = _bs((k, tm), lambda j, i: (0, i)), _bs((k, tn), lambda j, i: (0, j))
    assert m % tm == 0 and n % tn == 0
    o_spec = _bs((tm, tn), lambda j, i: (i, j))
    o_shape = (m, n)
    if mode == "tn" and stack:
        per = n // stack // tn
        o_spec = _bs((None, tm, tn), lambda j, i: (j // per, i, j % per))
        o_shape = (stack, m, n // stack)
    dims = {"nn": NN, "nt": NT, "tn": TN}[mode]
    if a_full:
        a_spec = pl.BlockSpec(a.shape, lambda j, i: (0, 0), pipeline_mode=pl.Buffered(1))

    def body(a_ref, b_ref, *rest):
        if a_full:
            mine = pl.ds(pl.multiple_of(pl.program_id(1) * tm, tm), tm)
            av = a_ref[:, mine] if mode == "tn" else a_ref[mine, :]
        else:
            av = a_ref[...]
        acc = _dot(av, b_ref[...], dims)
        outs = rest[len(after):]
        outs[0][...] = acc.astype(out_dtype)
        if twin_bf16:
            outs[1][...] = acc.astype(BF16)

    ins = (a, b) + tuple(after)
    specs = [a_spec, b_spec] + [pl.BlockSpec(memory_space=pl.ANY)] * len(after)
    shapes = [jax.ShapeDtypeStruct(o_shape, out_dtype)] + ([jax.ShapeDtypeStruct(o_shape, BF16)] if twin_bf16 else [])
    out = pl.pallas_call(
        body, name=name, grid=(n // tn, m // tm), in_specs=specs, out_specs=[o_spec] * len(shapes),
        out_shape=shapes, compiler_params=_cp("parallel", "parallel"),
    )(*ins)
    return tuple(out) if twin_bf16 else out[0]


def _rms_fwd(name, x, g):
    tr = 1024

    def body(x_ref, g_ref, o_ref):
        xv = x_ref[...]
        r = lax.rsqrt(jnp.mean(xv * xv, axis=-1, keepdims=True) + EPS)
        o_ref[...] = ((xv * r) * g_ref[...]).astype(BF16)

    return pl.pallas_call(
        body, name=name, grid=(T // tr,), in_specs=[_bs((tr, D), lambda i: (i, 0)), _bs((1, D), lambda i: (0, 0))],
        out_specs=_bs((tr, D), lambda i: (i, 0)), out_shape=jax.ShapeDtypeStruct((T, D), BF16),
        compiler_params=_cp("parallel"),
    )(x, g)


def _dgrad_rms_bwd(name, a, w, x, g, dres, want_bf16, after=()):
    tm = 512
    stack, _, cs = w.shape
    k = a.shape[1]

    def body(a_ref, w_ref, x_ref, g_ref, dr_ref, *rest):
        rest = rest[len(after):]
        dx_ref, dg_ref = rest[0], rest[-1]
        dyv = _dot(a_ref[:, 0:cs], w_ref[0], NT)
        for s in range(1, stack):
            dyv = dyv + _dot(a_ref[:, s * cs:(s + 1) * cs], w_ref[s], NT)
        xv = x_ref[...]
        r = lax.rsqrt(jnp.mean(xv * xv, axis=-1, keepdims=True) + EPS)
        gdy = g_ref[...] * dyv
        dx = r * gdy - xv * ((r * r * r) * jnp.mean(xv * gdy, axis=-1, keepdims=True)) + dr_ref[...]
        dx_ref[...] = dx
        if want_bf16:
            rest[1][...] = dx.astype(BF16)

        @pl.when(pl.program_id(0) == 0)
        def _():
            dg_ref[...] = jnp.zeros_like(dg_ref)

        dg_ref[...] += jnp.sum(dyv * (xv * r), axis=0, keepdims=True)

    row = _bs((tm, D), lambda i: (i, 0))
    vec = _bs((1, D), lambda i: (0, 0))
    outs = [jax.ShapeDtypeStruct((T, D), F32)] + ([jax.ShapeDtypeStruct((T, D), BF16)] if want_bf16 else [])
    return pl.pallas_call(
        body, name=name, grid=(T // tm,),
        in_specs=[_bs((tm, k), lambda i: (i, 0)),
                  pl.BlockSpec(w.shape, lambda i: (0, 0, 0), pipeline_mode=pl.Buffered(1)), row, vec, row]
        + [pl.BlockSpec(memory_space=pl.ANY)] * len(after),
        out_specs=[row] * len(outs) + [vec], out_shape=outs + [jax.ShapeDtypeStruct((1, D), F32)],
        compiler_params=_cp("arbitrary"),
    )(a, w, x, g, dres, *after)


def _out_rms(mix, w_out, x, g):
    tm = 512

    def body(a_ref, b_ref, r_ref, g_ref, x1_ref, h_ref):
        x1 = r_ref[...] + _dot(a_ref[...], b_ref[...])
        x1_ref[...] = x1
        r = lax.rsqrt(jnp.mean(x1 * x1, axis=-1, keepdims=True) + EPS)
        h_ref[...] = ((x1 * r) * g_ref[...]).astype(BF16)

    row = _bs((tm, D), lambda i: (i, 0))
    return pl.pallas_call(
        body, name="mm_out", grid=(T // tm,),
        in_specs=[row, pl.BlockSpec((D, D), lambda i: (0, 0), pipeline_mode=pl.Buffered(1)), row,
                  _bs((1, D), lambda i: (0, 0))],
        out_specs=[row, row], out_shape=[jax.ShapeDtypeStruct((T, D), F32), jax.ShapeDtypeStruct((T, D), BF16)],
        compiler_params=_cp("parallel"),
    )(mix, w_out, x, g)


def _head_ones():
    idx = np.arange(128) // HD
    return jnp.asarray((idx[:, None] == idx[None, :]).astype(np.float32), dtype=BF16)


def _freq_row():
    half = HD // 2
    inv = ROPE_THETA ** (-(np.arange(half, dtype=np.float64)) / half)
    return jnp.asarray(np.tile(inv, 4)[None, :], dtype=F32)


def _rot_tables(cos128, sin128):
    c = jnp.tile(cos128, (1, 4))
    s = jnp.tile(sin128, (1, 4))
    lane = lax.broadcasted_iota(jnp.int32, (1, AW), 1)
    first = (lane & 32) == 0
    return c, jnp.where(first, -s, s), first


def _swap_halves(y, first):
    return jnp.where(first, pltpu.roll(y, AW - 32, 1), pltpu.roll(y, 32, 1))


def _qk_prep(proj, pos_col, qg, kg):
    tr = 512

    def body(q_ref, k_ref, pos_ref, f_ref, qg_ref, kg_ref, e_ref, qo_ref, ko_ref, cos_ref, sin_ref):
        ang = pos_ref[...].astype(F32) * f_ref[...]
        cos_ref[...] = jnp.cos(ang)
        sin_ref[...] = jnp.sin(ang)
        c, s_signed, first = _rot_tables(cos_ref[...], sin_ref[...])
        e = e_ref[...]

        def norm_rot(xv, g, scale):
            r = lax.rsqrt(_segsum(xv * xv, e) * (1.0 / HD) + EPS)
            y = (xv * r) * g
            return (y * c + _swap_halves(y, first) * s_signed) * scale

        qo_ref[...] = norm_rot(q_ref[...], qg_ref[...], HD ** -0.5)
        ko_ref[...] = norm_rot(k_ref[...], kg_ref[...], 1.0)

    col = lambda j: _bs((tr, AW), lambda i, j=j: (i, j))
    vec = _bs((1, AW), lambda i: (0, 0))
    out = jax.ShapeDtypeStruct((T, AW), F32)
    tab = jax.ShapeDtypeStruct((T, 128), F32)
    tspec = _bs((tr, 128), lambda i: (i, 0))
    return pl.pallas_call(
        body, name="qk_prep", grid=(T // tr,),
        in_specs=[col(0), col(1), _bs((tr, 1), lambda i: (i, 0)), _bs((1, 128), lambda i: (0, 0)), vec, vec,
                  _bs((128, 128), lambda i: (0, 0))],
        out_specs=[col(0)] * 2 + [tspec] * 2, out_shape=[out, out, tab, tab], compiler_params=_cp("parallel"),
    )(proj, proj, pos_col, _freq_row(), qg, kg, _head_ones())


def _qk_bwd(proj, cos_t, sin_t, qg, kg, dq, dk, dv):
    tr = 512

    def body(q_ref, k_ref, cos_ref, sin_ref, qg_ref, kg_ref, e_ref, dq_ref, dk_ref, dv_ref, o_ref, dqg_ref, dkg_ref):
        i, j = pl.program_id(0), pl.program_id(1)

        @pl.when((i == 0) & (j == 0))
        def _():
            dqg_ref[...] = jnp.zeros_like(dqg_ref)
            dkg_ref[...] = jnp.zeros_like(dkg_ref)

        def norm_rot_bwd(x_ref, g_ref, dg_ref, d_ref, scale):
            c, s_signed, first = _rot_tables(cos_ref[...], sin_ref[...])
            e = e_ref[...]
            dout = d_ref[...] * scale
            dy = dout * c + _swap_halves(dout * s_signed, first)
            xv, g = x_ref[...], g_ref[...]
            r = lax.rsqrt(_segsum(xv * xv, e) * (1.0 / HD) + EPS)
            gdy = g * dy
            dx = r * gdy - xv * ((r * r * r) * (_segsum(xv * gdy, e) * (1.0 / HD)))
            o_ref[...] = dx.astype(BF16)
            dg_ref[...] += jnp.sum(dy * (xv * r), axis=0, keepdims=True)

        @pl.when(j == 0)
        def _():
            o_ref[...] = dv_ref[...].astype(BF16)

        @pl.when(j == 1)
        def _():
            norm_rot_bwd(q_ref, qg_ref, dqg_ref, dq_ref, HD ** -0.5)

        @pl.when(j == 2)
        def _():
            norm_rot_bwd(k_ref, kg_ref, dkg_ref, dk_ref, 1.0)

    col = lambda jj: _bs((tr, AW), lambda i, j, jj=jj: (i, jj))
    vec = _bs((1, AW), lambda i, j: (0, 0))
    piece = _bs((tr, AW), lambda i, j: (i, 0))
    return pl.pallas_call(
        body, name="qk_bwd", grid=(T // tr, 3),
        in_specs=[col(0), col(1), _bs((tr, 128), lambda i, j: (i, 0)), _bs((tr, 128), lambda i, j: (i, 0)), vec, vec,
                  _bs((128, 128), lambda i, j: (0, 0))] + [piece] * 3,
        out_specs=[_bs((tr, AW), lambda i, j: (i, (j + 2) % 3)), vec, vec],
        out_shape=[jax.ShapeDtypeStruct((T, 3 * AW), BF16), jax.ShapeDtypeStruct((1, AW), F32),
                   jax.ShapeDtypeStruct((1, AW), F32)],
        compiler_params=_cp("arbitrary", "arbitrary"),
    )(proj, proj, cos_t, sin_t, qg, kg, _head_ones(), dq, dk, dv)


RG = 256


def _stacked_band_mask():
    qi = lax.broadcasted_iota(jnp.int32, (2 * BLK, 2 * BLK), 0) & (BLK - 1)
    kj = lax.broadcasted_iota(jnp.int32, (2 * BLK, 2 * BLK), 1)
    rel = qi - kj + BLK
    return (rel >= 0) & (rel <= BLK), lax.broadcasted_iota(jnp.int32, (1, 2 * BLK), 1) >= BLK


def _natural_rows(r0, n_rows, d):
    if d == 1:
        return pl.ds(r0, n_rows)
    ln = T // d
    return pl.ds(r0 // ln + d * (r0 % ln), n_rows, stride=d)


def _regroup_into(dst, src_ref, d, pad, cast=True):
    def step(j, carry):
        r0 = pl.multiple_of(j * RG, RG)
        val = src_ref[_natural_rows(r0, RG, d), :]
        dst[pl.ds(pad + r0, RG), :] = val.astype(dst.dtype) if cast else val
        return carry
    lax.fori_loop(0, T // RG, step, 0)


def _stack_heads(x, h0):
    zero = jnp.zeros_like(x)
    return jnp.concatenate([jnp.where(h0, x, zero), jnp.where(h0, zero, x)], axis=0)


def _attn_fwd(q, k, proj):
    nblk = T // BLK

    def body(q_ref, k_ref, v_ref, a_ref, lse_ref, qs, ks, vs, o0, o1, o2, l0, l1, l2, sb0, sb1):
        band, cur_half = _stacked_band_mask()
        h0 = lax.broadcasted_iota(jnp.int32, (1, 128), 1) < HD
        ks[0:BLK, :] = jnp.zeros((BLK, 128), BF16)
        vs[0:BLK, :] = jnp.zeros((BLK, 128), BF16)
        for d, o_s, l_s in zip(DILATIONS, (o0, o1, o2), (l0, l1, l2)):
            nb = T // d // BLK
            _regroup_into(qs, q_ref, d, 0)
            _regroup_into(ks, k_ref, d, BLK)
            _regroup_into(vs, v_ref, d, BLK)

            def scores(b):
                r0 = pl.multiple_of(b * BLK, BLK)
                return _dot(_stack_heads(qs[pl.ds(r0, BLK), :], h0), ks[pl.ds(r0, 2 * BLK), :], NT)

            def finish(b, s_raw, d=d, nb=nb, o_s=o_s, l_s=l_s):
                r0 = pl.multiple_of(b * BLK, BLK)
                mask = band & (cur_half | ((b & (nb - 1)) > 0))
                s = jnp.where(mask, s_raw, NEG)
                m = jnp.max(s, axis=1, keepdims=True)
                p = jnp.exp(s - m)
                l = jnp.sum(p, axis=1, keepdims=True)
                o = _dot(p.astype(BF16), vs[pl.ds(r0, 2 * BLK), :]) / l
                lse = m + jnp.log(l)
                rows = _natural_rows(r0, BLK, d)
                o_s[rows, :] = jnp.where(h0, o[0:BLK, :], o[BLK:, :])
                l_s[rows, :] = jnp.where(h0, lse[0:BLK, :], lse[BLK:, :])

            sb0[...] = scores(0)

            def step(i, carry):
                b = 2 * i
                sb1[...] = scores(b + 1)
                finish(b, sb0[...])
                sb0[...] = scores(jnp.minimum(b + 2, nblk - 1))
                finish(b + 1, sb1[...])
                return carry

            lax.fori_loop(0, nblk // 2, step, 0)

        def merge(i, carry):
            r = pl.ds(pl.multiple_of(i * RG, RG), RG)
            la, lb, lc = l0[r, :], l1[r, :], l2[r, :]
            m = jnp.maximum(jnp.maximum(la, lb), lc)
            ea, eb, ec = jnp.exp(la - m), jnp.exp(lb - m), jnp.exp(lc - m)
            z = (ea + eb) + ec
            a_ref[r, :] = ((ea * o0[r, :] + eb * o1[r, :]) + ec * o2[r, :]) / z
            lse_ref[r, :] = m + jnp.log(z)
            return carry

        lax.fori_loop(0, T // RG, merge, 0)

    spec = lambda cb: _bs((T, 128), lambda p, cb=cb: (0, cb + p))
    out = jax.ShapeDtypeStruct((T, AW), F32)
    return pl.pallas_call(
        body, name="attn_fwd", grid=(AW // 128,), in_specs=[spec(0), spec(0), spec(8)], out_specs=[spec(0)] * 2,
        out_shape=[out] * 2,
        scratch_shapes=[pltpu.VMEM((T, 128), BF16), pltpu.VMEM((T + BLK, 128), BF16), pltpu.VMEM((T + BLK, 128), BF16)]
        + [pltpu.VMEM((T, 128), F32)] * 6 + [pltpu.VMEM((2 * BLK, 2 * BLK), F32)] * 2,
        compiler_params=_cp("parallel"),
    )(q, k, proj)


def _attn_bwd(q, k, proj, do, lse, delta):
    nblk = T // BLK

    def body(q_ref, k_ref, v_ref, do_ref, l_ref, dl_ref, dq_ref, dk_ref, dv_ref, qs, dos, ks, vs, ls, dls, dks, dvs,
             sa0, sa1, da0, da1):
        band, cur_half = _stacked_band_mask()
        h0 = lax.broadcasted_iota(jnp.int32, (1, 128), 1) < HD
        ks[0:BLK, :] = jnp.zeros((BLK, 128), BF16)
        vs[0:BLK, :] = jnp.zeros((BLK, 128), BF16)
        for d in DILATIONS:
            nb = T // d // BLK
            _regroup_into(qs, q_ref, d, 0)
            _regroup_into(dos, do_ref, d, 0)
            _regroup_into(ks, k_ref, d, BLK)
            _regroup_into(vs, v_ref, d, BLK)
            _regroup_into(ls, l_ref, d, 0, cast=False)
            _regroup_into(dls, dl_ref, d, 0, cast=False)
            dks[...] = jnp.zeros_like(dks)
            dvs[...] = jnp.zeros_like(dvs)

            def scores(b, s_buf, dp_buf):
                r0 = pl.multiple_of(b * BLK, BLK)
                win = pl.ds(r0, 2 * BLK)
                s_buf[...] = _dot(_stack_heads(qs[pl.ds(r0, BLK), :], h0), ks[win, :], NT)
                dp_buf[...] = _dot(_stack_heads(dos[pl.ds(r0, BLK), :], h0), vs[win, :], NT)

            def finish(b, s_buf, dp_buf, d=d, nb=nb):
                r0 = pl.multiple_of(b * BLK, BLK)
                mask = band & (cur_half | ((b & (nb - 1)) > 0))
                win = pl.ds(r0, 2 * BLK)
                lv, dlv = ls[pl.ds(r0, BLK), :], dls[pl.ds(r0, BLK), :]
                lse2 = jnp.concatenate([lv[:, 0:1], lv[:, HD:HD + 1]], axis=0)
                dl2 = jnp.concatenate([dlv[:, 0:1], dlv[:, HD:HD + 1]], axis=0)
                p = jnp.exp(jnp.where(mask, s_buf[...], NEG) - lse2)
                ds = p * (dp_buf[...] - dl2)
                pb, dsb = p.astype(BF16), ds.astype(BF16)
                dq2 = _dot(dsb, ks[win, :])
                dks[win, :] += _dot(dsb, _stack_heads(qs[pl.ds(r0, BLK), :], h0), TN)
                dvs[win, :] += _dot(pb, _stack_heads(dos[pl.ds(r0, BLK), :], h0), TN)
                rows = _natural_rows(r0, BLK, d)
                dq = jnp.where(h0, dq2[0:BLK, :], dq2[BLK:, :])
                dq_ref[rows, :] = dq if d == 1 else dq_ref[rows, :] + dq

            scores(0, sa0, da0)

            def step(i, carry):
                b = 2 * i
                scores(b + 1, sa1, da1)
                finish(b, sa0, da0)
                scores(jnp.minimum(b + 2, nblk - 1), sa0, da0)
                finish(b + 1, sa1, da1)
                return carry

            lax.fori_loop(0, nblk // 2, step, 0)

            def back(j, carry, d=d):
                r0 = pl.multiple_of(j * RG, RG)
                rows = _natural_rows(r0, RG, d)
                src = pl.ds(BLK + r0, RG)
                dk_ref[rows, :] = dks[src, :] if d == 1 else dk_ref[rows, :] + dks[src, :]
                dv_ref[rows, :] = dvs[src, :] if d == 1 else dv_ref[rows, :] + dvs[src, :]
                return carry

            lax.fori_loop(0, T // RG, back, 0)

    spec = lambda cb: _bs((T, 128), lambda p, cb=cb: (0, cb + p))
    ospec = _bs((T, 128), lambda p: (0, p))
    out = jax.ShapeDtypeStruct((T, AW), F32)
    return pl.pallas_call(
        body, name="attn_bwd", grid=(AW // 128,), in_specs=[spec(0), spec(0), spec(8), spec(0), spec(0), spec(0)],
        out_specs=[ospec] * 3, out_shape=[out] * 3,
        scratch_shapes=[pltpu.VMEM((T, 128), BF16), pltpu.VMEM((T, 128), BF16), pltpu.VMEM((T + BLK, 128), BF16),
                        pltpu.VMEM((T + BLK, 128), BF16), pltpu.VMEM((T, 128), F32), pltpu.VMEM((T, 128), F32),
                        pltpu.VMEM((T + BLK, 128), F32), pltpu.VMEM((T + BLK, 128), F32)]
        + [pltpu.VMEM((2 * BLK, 2 * BLK), F32)] * 4,
        compiler_params=_cp("parallel"),
    )(q, k, proj, do, lse, delta)


def _attn_norm(attn, g_attn):
    tr = 1024

    def body(a_ref, g_ref, mix_ref):
        attn = a_ref[...]
        r = lax.rsqrt(jnp.mean(attn * attn, axis=-1, keepdims=True) + EPS)
        mix_ref[...] = ((attn * r) * g_ref[...]).astype(BF16)

    row = _bs((tr, AW), lambda i: (i, 0))
    return pl.pallas_call(
        body, name="attn_norm", grid=(T // tr,), in_specs=[row, _bs((1, AW), lambda i: (0, 0))],
        out_specs=row, out_shape=jax.ShapeDtypeStruct((T, D), BF16), compiler_params=_cp("parallel"),
    )(attn, g_attn)


def _attn_out_bwd(attn, dmix, g_attn):
    tr = 512

    def body(a_ref, d_ref, g_ref, e_ref, do_ref, dl_ref, dg_ref):
        av, dyv = a_ref[...], d_ref[...]
        r = lax.rsqrt(jnp.mean(av * av, axis=-1, keepdims=True) + EPS)
        gdy = g_ref[...] * dyv
        da = r * gdy - av * ((r * r * r) * jnp.mean(av * gdy, axis=-1, keepdims=True))
        do_ref[...] = da
        dl_ref[...] = _segsum(da * av, e_ref[...])

        @pl.when(pl.program_id(0) == 0)
        def _():
            dg_ref[...] = jnp.zeros_like(dg_ref)

        dg_ref[...] += jnp.sum(dyv * (av * r), axis=0, keepdims=True)

    row = _bs((tr, AW), lambda i: (i, 0))
    vec = _bs((1, AW), lambda i: (0, 0))
    return pl.pallas_call(
        body, name="attn_out_bwd", grid=(T // tr,), in_specs=[row, row, vec, _bs((128, 128), lambda i: (0, 0))],
        out_specs=[row, row, vec],
        out_shape=[jax.ShapeDtypeStruct((T, AW), F32), jax.ShapeDtypeStruct((T, AW), F32),
                   jax.ShapeDtypeStruct((1, AW), F32)],
        compiler_params=_cp("arbitrary"),
    )(attn, dmix, g_attn, _head_ones())


TRR = 256


def _scan_fwd(a, u):
    n = a.shape[0]
    row = lax.broadcasted_iota(jnp.int32, (n, 1), 0)
    s = 1
    while s < n:
        keep = row >= s
        u = jnp.where(keep, a * pltpu.roll(u, s, 0) + u, u)
        a = jnp.where(keep, a * pltpu.roll(a, s, 0), a)
        s *= 2
    return a, u


def _scan_bwd(c, w):
    n = c.shape[0]
    row = lax.broadcasted_iota(jnp.int32, (n, 1), 0)
    s = 1
    while s < n:
        keep = row < n - s
        w = jnp.where(keep, c * pltpu.roll(w, n - s, 0) + w, w)
        c = jnp.where(keep, c * pltpu.roll(c, n - s, 0), c)
        s *= 2
    return w


def _gates(xc, wrg, wig, brg, big, sp):
    xcb = xc.astype(BF16)
    r = jax.nn.sigmoid(_dot(xcb, wrg) + brg)
    ig = jax.nn.sigmoid(_dot(xcb, wig) + big)
    la = (-LRU_C * r) * sp
    a = jnp.exp(la)
    mult = jnp.sqrt(-jnp.tanh(la) * (a * a + 1.0))
    return r, ig, a, mult


def _conv4(ext_ref, xr, cw_ref, cb_ref, n):
    y = cb_ref[...] + ext_ref[pl.ds(5, n), :] * cw_ref[0:1, :]
    y = y + ext_ref[pl.ds(6, n), :] * cw_ref[1:2, :]
    y = y + ext_ref[pl.ds(7, n), :] * cw_ref[2:3, :]
    return y + xr * cw_ref[3:4, :]


def _rec_fwd(proj, mix, cw, cb, wrg, wig, brg, big, lam, g_rec):
    n = TRR

    def body(xr_ref, gr_ref, cw_ref, cb_ref, wrg_ref, wig_ref, brg_ref, big_ref, lam_ref, g_ref, mix_in,
             mix_ref, h_ref, ext, hcar):
        del mix_in

        @pl.when(pl.program_id(0) == 0)
        def _():
            ext[0:8, :] = jnp.zeros((8, RW), F32)
            hcar[...] = jnp.zeros_like(hcar)

        xr = xr_ref[...]
        ext[8:, :] = xr
        xc = _conv4(ext, xr, cw_ref, cb_ref, n)
        ext[0:8, :] = xr[n - 8:, :]
        sp = _softplus(-lam_ref[...])
        _, ig, a, mult = _gates(xc, wrg_ref[...], wig_ref[...], brg_ref[...], big_ref[...], sp)
        a_s, u_s = _scan_fwd(a, mult * (ig * xc))
        h = u_s + a_s * hcar[7:8, :]
        h_ref[...] = h
        hcar[...] = h[n - 8:, :]
        pre = h * _gelu(gr_ref[...])
        r = lax.rsqrt(jnp.mean(pre * pre, axis=-1, keepdims=True) + EPS)
        mix_ref[...] = ((pre * r) * g_ref[...]).astype(BF16)

    vec = _bs((1, RW), lambda i: (0, 0))
    mat = _bs((RW, RW), lambda i: (0, 0))
    return pl.pallas_call(
        body, name="rec_fwd", grid=(T // n,),
        in_specs=[_bs((n, RW), lambda i: (i, 3)), _bs((n, RW), lambda i: (i, 4)), _bs((8, RW), lambda i: (0, 0)), vec,
                  mat, mat, vec, vec, vec, vec, pl.BlockSpec(memory_space=pl.ANY)],
        out_specs=[_bs((n, RW), lambda i: (i, 1)), _bs((n, RW), lambda i: (i, 0))],
        out_shape=[jax.ShapeDtypeStruct((T, D), BF16), jax.ShapeDtypeStruct((T, RW), F32)],
        scratch_shapes=[pltpu.VMEM((n + 8, RW), F32), pltpu.VMEM((8, RW), F32)],
        input_output_aliases={10: 0}, compiler_params=_cp("arbitrary"),
    )(proj, proj, cw, cb, wrg, wig, brg, big, lam, g_rec, mix)


def _rec_bwd(proj, h, dmix, cw, cb, wrg, wig, brg, big, lam, g_rec):
    n = TRR
    nt = T // n
    hb = n // 8

    def body(xr_ref, xh_ref, gr_ref, h_ref, hh_ref, dm_ref, cw_ref, cb_ref, wrg_ref, wig_ref, brg_ref, big_ref,
             lam_ref, g_ref, dp_ref, xc_ref, dr_ref, di_ref, dcw_ref, dcb_ref, dbr_ref, dbi_ref, dsp_ref,
             dg_ref, ext, exth, extd, adh):
        i, j = pl.program_id(0), pl.program_id(1)
        first_tile = i == nt - 1
        last_tile = i == 0

        @pl.when(j == 0)
        def _():
            @pl.when(last_tile)
            def _():
                for ref in (dcw_ref, dcb_ref, dbr_ref, dbi_ref, dsp_ref, dg_ref):
                    ref[...] = jnp.zeros_like(ref)
                extd[n:, :] = jnp.zeros((8, RW), F32)
                adh[...] = jnp.zeros_like(adh)

            row = lax.broadcasted_iota(jnp.int32, (n, 1), 0)
            xr = xr_ref[...]
            ext[0:8, :] = jnp.where(first_tile, 0.0, xh_ref[...])
            ext[8:, :] = xr
            xc = _conv4(ext, xr, cw_ref, cb_ref, n)
            sp = _softplus(-lam_ref[...])
            wrg, wig = wrg_ref[...], wig_ref[...]
            r, ig, a, mult = _gates(xc, wrg, wig, brg_ref[...], big_ref[...], sp)

            hv = h_ref[...]
            gl, dgl = _gelu_and_grad(gr_ref[...])
            pre = hv * gl
            dyv = dm_ref[...]
            rr = lax.rsqrt(jnp.mean(pre * pre, axis=-1, keepdims=True) + EPS)
            gdy = g_ref[...] * dyv
            dpre = rr * gdy - pre * ((rr * rr * rr) * jnp.mean(pre * gdy, axis=-1, keepdims=True))
            dg_ref[...] += jnp.sum(dyv * (pre * rr), axis=0, keepdims=True)
            dp_ref[:, RW:] = (dpre * hv * dgl).astype(BF16)

            is_last_row = row == n - 1
            w = dpre * gl + jnp.where(is_last_row, adh[0:1, :], 0.0)
            c = jnp.where(is_last_row, 0.0, pltpu.roll(a, n - 1, 0))
            dh = _scan_bwd(c, w)
            adh[...] = (a * dh)[0:8, :]

            exth[0:8, :] = jnp.where(first_tile, 0.0, hh_ref[...])
            exth[8:, :] = hv
            da = dh * exth[pl.ds(7, n), :]
            ixc = ig * xc
            dmult = dh * ixc
            dla = da * a - dmult * ((a * a) / mult)
            dsp_ref[...] += jnp.sum(dla * (-LRU_C * r), axis=0, keepdims=True)
            dpr = (dla * (-LRU_C * sp)) * (r * (1.0 - r))
            dpi = (dh * (mult * xc)) * (ig * (1.0 - ig))
            dprb, dpib = dpr.astype(BF16), dpi.astype(BF16)
            dxc = dh * (mult * ig) + _dot(dprb, wrg, NT) + _dot(dpib, wig, NT)
            dbr_ref[...] += jnp.sum(dpr, axis=0, keepdims=True)
            dbi_ref[...] += jnp.sum(dpi, axis=0, keepdims=True)
            xc_ref[...] = xc.astype(BF16)
            dr_ref[...] = dprb
            di_ref[...] = dpib

            extd[0:n, :] = dxc
            dxr = dxc * cw_ref[3:4, :] + extd[pl.ds(1, n), :] * cw_ref[2:3, :]
            dxr = dxr + extd[pl.ds(2, n), :] * cw_ref[1:2, :] + extd[pl.ds(3, n), :] * cw_ref[0:1, :]
            extd[n:, :] = dxc[0:8, :]
            dcb_ref[...] += jnp.sum(dxc, axis=0, keepdims=True)
            for kk in range(4):
                dcw_ref[kk:kk + 1, :] += jnp.sum(dxc * ext[pl.ds(5 + kk, n), :], axis=0, keepdims=True)

            @pl.when(first_tile)
            def _():
                dsp_ref[...] = dsp_ref[...] * (-jax.nn.sigmoid(-lam_ref[...]))

            dp_ref[:, 0:RW] = dxr.astype(BF16)

    vec = _bs((1, RW), lambda i, j: (0, 0))
    mat = _bs((RW, RW), lambda i, j: (0, 0))
    tile = lambda cblk: _bs((n, RW), lambda i, j, cblk=cblk: (nt - 1 - i, cblk))
    halo = lambda cblk: _bs((8, RW), lambda i, j, cblk=cblk: (jnp.maximum((nt - 1 - i) * hb - 1, 0), cblk))
    bt = jax.ShapeDtypeStruct((T, RW), BF16)
    v = jax.ShapeDtypeStruct((1, RW), F32)
    return pl.pallas_call(
        body, name="rec_bwd", grid=(nt, 1),
        in_specs=[tile(3), halo(3), tile(4), tile(0), halo(0), tile(1), _bs((8, RW), lambda i, j: (0, 0)), vec,
                  mat, mat, vec, vec, vec, vec],
        out_specs=[_bs((n, 2 * RW), lambda i, j: (nt - 1 - i, 0)), tile(0), tile(0), tile(0),
                   _bs((8, RW), lambda i, j: (0, 0)), vec, vec, vec, vec, vec],
        out_shape=[jax.ShapeDtypeStruct((T, 2 * RW), BF16), bt, bt, bt, jax.ShapeDtypeStruct((8, RW), F32),
                   v, v, v, v, v],
        scratch_shapes=[pltpu.VMEM((n + 8, RW), F32), pltpu.VMEM((n + 8, RW), F32), pltpu.VMEM((n + 8, RW), F32),
                        pltpu.VMEM((8, RW), F32)],
        compiler_params=_cp("arbitrary", "arbitrary"),
    )(proj, proj, proj, h, h, dmix, cw, cb, wrg, wig, brg, big, lam, g_rec)


FC = 1536
TRF = 512


LC = 128


class _RowsBack:
    def __init__(self, before):
        row = lax.broadcasted_iota(jnp.int32, before.shape, 0)
        self.top1, self.top2 = row < 1, row < 2
        self.r1, self.r2 = pltpu.roll(before, 1, 0), pltpu.roll(before, 2, 0)

    def step(self, cur):
        r1, r2 = pltpu.roll(cur, 1, 0), pltpu.roll(cur, 2, 0)
        out = jnp.where(self.top1, self.r1, r1), jnp.where(self.top2, self.r2, r2)
        self.r1, self.r2 = r1, r2
        return out


def _up_act(h2, w_up, cw, cb):
    n = TRF
    nt = T // n
    pw = 256
    npc = FC // pw

    def body(h_ref, wg_ref, wu_ref, cwg_ref, cwu_ref, bg_ref, bu_ref, up_ref, a_ref, fa_ref, fb_ref, hx, gb0, gb1,
             ub0, ub1):
        i = pl.program_id(1)
        halo = h_ref[pl.ds(pl.multiple_of(jnp.maximum(i * n - 16, 0), 16), 16), :]
        hx[0:16, :] = jnp.where(i == 0, jnp.zeros_like(halo), halo)
        hx[16:, :] = h_ref[pl.ds(pl.multiple_of(i * n, n), n), :]
        gbufs, ubufs = (gb0, gb1), (ub0, ub1)

        def dots(c):
            hv = hx[...]
            gbufs[c % 2][...] = _dot(hv, wg_ref[:, c * pw:(c + 1) * pw])
            ubufs[c % 2][...] = _dot(hv, wu_ref[:, c * pw:(c + 1) * pw])

        def chain(c):
            gb, ub = gbufs[c % 2], ubufs[c % 2]
            up_ref[:, c * pw:(c + 1) * pw] = gb[16:, :]
            up_ref[:, FC + c * pw:FC + (c + 1) * pw] = ub[16:, :]
            rows8 = lambda v: jnp.broadcast_to(v, (8, LC))
            for sub in range(pw // LC):
                lc = slice(sub * LC, (sub + 1) * LC)
                cols = slice(c * pw + sub * LC, c * pw + (sub + 1) * LC)
                wg = [rows8(cwg_ref[kk:kk + 1, cols]) for kk in range(3)]
                wu = [rows8(cwu_ref[kk:kk + 1, cols]) for kk in range(3)]
                bg, bu = rows8(bg_ref[:, cols]), rows8(bu_ref[:, cols])
                g_back = _RowsBack(gb[pl.ds(8, 8), lc])
                u_back = _RowsBack(ub[pl.ds(8, 8), lc])
                for r in range(0, n, 16):
                    res, fa, fb = [], [], []
                    for rr in (16 + r, 24 + r):
                        g0, u0 = gb[pl.ds(rr, 8), lc], ub[pl.ds(rr, 8), lc]
                        g1, g2 = g_back.step(g0)
                        u1, u2 = u_back.step(u0)
                        ug = ((bg + g2 * wg[0]) + g1 * wg[1]) + g0 * wg[2]
                        uu = ((bu + u2 * wu[0]) + u1 * wu[1]) + u0 * wu[2]
                        gl, dgl = _gelu_and_grad(ug)
                        res.append(gl * uu)
                        fa.append(uu * dgl)
                        fb.append(gl)
                    a_ref[pl.ds(r, 16), cols] = jnp.concatenate(res, axis=0).astype(BF16)
                    fa_ref[pl.ds(r, 16), cols] = jnp.concatenate(fa, axis=0).astype(BF16)
                    fb_ref[pl.ds(r, 16), cols] = jnp.concatenate(fb, axis=0).astype(BF16)

        dots(0)
        for c in range(npc):
            if c + 1 < npc:
                dots(c + 1)
            chain(c)

    wsl = lambda o: _bs((None, D, FC), lambda j, i, o=o: (2 * j + o, 0, 0))
    wsp = lambda o: _bs((None, 8, FC), lambda j, i, o=o: (2 * j + o, 0, 0))
    bsp = lambda o: _bs((1, FC), lambda j, i, o=o: (0, 2 * j + o))
    return pl.pallas_call(
        body, name="up_act", grid=(2, nt),
        in_specs=[pl.BlockSpec((T, D), lambda j, i: (0, 0), pipeline_mode=pl.Buffered(1)), wsl(0), wsl(1),
                  wsp(0), wsp(1), bsp(0), bsp(1)],
        out_specs=[_bs((n, 2 * FC), lambda j, i: (i, j))] + [_bs((n, FC), lambda j, i: (i, j))] * 3,
        out_shape=[jax.ShapeDtypeStruct((T, 2 * DFF), F32)] + [jax.ShapeDtypeStruct((T, DFF), BF16)] * 3,
        scratch_shapes=[pltpu.VMEM((n + 16, D), BF16)] + [pltpu.VMEM((n + 16, pw), F32)] * 4,
        compiler_params=_cp("parallel", "arbitrary"),
    )(h2, w_up, w_up, cw, cw, cb, cb)


def _ffn_bwd(up_pre, fa, fb, dyb, w_down_t, cw, after=()):
    n = TRF
    hb = n // 8
    nt = T // n
    m = n + 8
    pw = 256
    npc = FC // pw

    def body(g_ref, gp_ref, u_ref, up_ref, fa_ref, fan_ref, fb_ref, fbn_ref, dy_ref, wd_ref, wg_ref, wu_ref, *rest):
        o_ref, dw_ref, db_ref, eg0, eu0, dug_s, duu_s, dyx, db0, db1 = rest[len(after):]
        i = pl.program_id(1)
        first, last = i == 0, i == nt - 1

        @pl.when(first)
        def _():
            dw_ref[...] = jnp.zeros_like(dw_ref)
            db_ref[...] = jnp.zeros_like(db_ref)

        tail = dy_ref[pl.ds(pl.multiple_of(jnp.minimum((i + 1) * n, T - 16), 16), 16), :]
        dyx[0:n, :] = dy_ref[pl.ds(pl.multiple_of(i * n, n), n), :]
        dyx[n:, :] = jnp.where(last, jnp.zeros_like(tail), tail)
        dbufs = (db0, db1)

        def dots(c):
            dbufs[c % 2][...] = _dot(dyx[...], wd_ref[:, c * pw:(c + 1) * pw])

        eg0[0:8, :] = jnp.where(first, 0.0, gp_ref[...])
        eg0[8:, :] = g_ref[0:8, :]
        eu0[0:8, :] = jnp.where(first, 0.0, up_ref[...])
        eu0[8:, :] = u_ref[0:8, :]

        def column(ci, dbuf, lc):
            cols = slice(ci * LC, (ci + 1) * LC)
            ucols = slice(FC + ci * LC, FC + (ci + 1) * LC)
            rows8 = lambda v: jnp.broadcast_to(v, (8, LC))
            wg = [rows8(wg_ref[kk:kk + 1, cols]) for kk in range(3)]
            wu = [rows8(wu_ref[kk:kk + 1, cols]) for kk in range(3)]
            zero = jnp.zeros((8, LC), F32)
            acc = [zero] * 8
            g_back, u_back = _RowsBack(eg0[pl.ds(0, 8), cols]), _RowsBack(eu0[pl.ds(0, 8), cols])
            for r in range(0, n + 16, 16):
                src_a, src_b, r16 = (fan_ref, fbn_ref, 0) if r == n else (fa_ref, fb_ref, r)
                fa16 = src_a[pl.ds(r16, 16), cols].astype(F32)
                fb16 = src_b[pl.ds(r16, 16), cols].astype(F32)
                for half in range(1 if r == n else 2):
                    rr = r + 8 * half
                    dv = dbuf[pl.ds(rr, 8), lc]
                    dug = dv * fa16[8 * half:8 * half + 8, :]
                    duu = dv * fb16[8 * half:8 * half + 8, :]
                    dug_s[pl.ds(rr, 8), :] = dug
                    duu_s[pl.ds(rr, 8), :] = duu
                    if rr < n:
                        g0, u0 = g_ref[pl.ds(rr, 8), cols], u_ref[pl.ds(rr, 8), cols]
                        g1, g2 = g_back.step(g0)
                        u1, u2 = u_back.step(u0)
                        gt, ut = (g2, g1, g0), (u2, u1, u0)
                        acc = [acc[0] + dug * gt[0], acc[1] + dug * gt[1], acc[2] + dug * gt[2],
                               acc[3] + duu * ut[0], acc[4] + duu * ut[1], acc[5] + duu * ut[2],
                               acc[6] + dug, acc[7] + duu]
            for r in range(0, n, 16):
                og, ou = [], []
                for rr in (r, r + 8):
                    og.append((dug_s[pl.ds(rr, 8), :] * wg[2] + dug_s[pl.ds(rr + 1, 8), :] * wg[1])
                              + dug_s[pl.ds(rr + 2, 8), :] * wg[0])
                    ou.append((duu_s[pl.ds(rr, 8), :] * wu[2] + duu_s[pl.ds(rr + 1, 8), :] * wu[1])
                              + duu_s[pl.ds(rr + 2, 8), :] * wu[0])
                o_ref[pl.ds(r, 16), cols] = jnp.concatenate(og, axis=0).astype(BF16)
                o_ref[pl.ds(r, 16), ucols] = jnp.concatenate(ou, axis=0).astype(BF16)
            for kk in range(3):
                dw_ref[kk:kk + 1, cols] += jnp.sum(acc[kk], axis=0, keepdims=True)
                dw_ref[kk:kk + 1, ucols] += jnp.sum(acc[3 + kk], axis=0, keepdims=True)
            db_ref[:, cols] += jnp.sum(acc[6], axis=0, keepdims=True)
            db_ref[:, ucols] += jnp.sum(acc[7], axis=0, keepdims=True)

        dots(0)
        for c in range(npc):
            if c + 1 < npc:
                dots(c + 1)
            for sub in range(pw // LC):
                column(c * (pw // LC) + sub, dbufs[c % 2], slice(sub * LC, (sub + 1) * LC))

    main = lambda o: _bs((n, FC), lambda j, i, o=o: (i, 2 * j + o))
    prev = lambda o: _bs((8, FC), lambda j, i, o=o: (jnp.maximum(i * hb - 1, 0), 2 * j + o))
    saved = _bs((n, FC), lambda j, i: (i, j))
    saved_next = _bs((16, FC), lambda j, i: (jnp.minimum((i + 1) * (n // 16), T // 16 - 1), j))
    wsp = lambda o: _bs((None, 8, FC), lambda j, i, o=o: (2 * j + o, 0, 0))
    return pl.pallas_call(
        body, name="ffn_bwd", grid=(2, nt),
        in_specs=[main(0), prev(0), main(1), prev(1), saved, saved_next, saved, saved_next,
                  pl.BlockSpec((T, D), lambda j, i: (0, 0), pipeline_mode=pl.Buffered(1)),
                  _bs((D, FC), lambda j, i: (0, j)), wsp(0), wsp(1)]
        + [pl.BlockSpec(memory_space=pl.ANY)] * len(after),
        out_specs=[_bs((n, 2 * FC), lambda j, i: (i, j)), _bs((8, 2 * FC), lambda j, i: (0, j)),
                   _bs((1, 2 * FC), lambda j, i: (0, j))],
        out_shape=[jax.ShapeDtypeStruct((T, 2 * DFF), BF16), jax.ShapeDtypeStruct((8, 2 * DFF), F32),
                   jax.ShapeDtypeStruct((1, 2 * DFF), F32)],
        scratch_shapes=[pltpu.VMEM((16, FC), F32)] * 2 + [pltpu.VMEM((m, LC), F32)] * 2
        + [pltpu.VMEM((n + 16, D), BF16)] + [pltpu.VMEM((n + 16, pw), F32)] * 2,
        compiler_params=_cp("parallel", "arbitrary"),
    )(up_pre, up_pre, up_pre, up_pre, fa, fa, fb, fb, dyb, w_down_t, cw, cw, *after)


def _down_loss(act, w_down, x1, target):
    tm, tn = 512, D

    def body(a_ref, b_ref, r_ref, t_ref, dy_ref, dyb_ref, l_ref):
        @pl.when((pl.program_id(0) == 0) & (pl.program_id(1) == 0))
        def _():
            l_ref[...] = jnp.zeros_like(l_ref)

        err = (r_ref[...] + _dot(a_ref[...], b_ref[...])) - t_ref[...]
        dy = err * (1.0 / D)
        dy_ref[...] = dy
        dyb_ref[...] = dy.astype(BF16)
        l_ref[...] += jnp.sum(0.5 * (err * err) * (1.0 / D))

    o_spec = _bs((tm, tn), lambda j, i: (i, j))
    return pl.pallas_call(
        body, name="down_loss", grid=(D // tn, T // tm),
        in_specs=[_bs((tm, DFF), lambda j, i: (i, 0)),
                  pl.BlockSpec((DFF, tn), lambda j, i: (0, j), pipeline_mode=pl.Buffered(1)), o_spec, o_spec],
        out_specs=[o_spec, o_spec, _bs((8, 128), lambda j, i: (0, 0))],
        out_shape=[jax.ShapeDtypeStruct((T, D), F32), jax.ShapeDtypeStruct((T, D), BF16),
                   jax.ShapeDtypeStruct((8, 128), F32)],
        compiler_params=_cp("arbitrary", "arbitrary"),
    )(act, w_down, x1, target)


def _block_diag(w):
    eye = jnp.eye(8, dtype=w.dtype)
    return (w[:, :, None, :] * eye[:, None, :, None]).reshape(RW, RW).astype(BF16)


def _diag_blocks(m):
    eye = jnp.eye(8, dtype=m.dtype)
    return (m.reshape(8, HD, 8, HD) * eye[:, None, :, None]).sum(axis=2)


def _local_step(x, pos_col, target, p, exch):
    qg, kg = jnp.tile(p["q_norm_g"], (1, 8)), jnp.tile(p["k_norm_g"], (1, 8))
    wrg, wig = _block_diag(p["w_rg"]), _block_diag(p["w_ig"])
    brg, big = p["b_rg"].reshape(1, RW), p["b_ig"].reshape(1, RW)

    h1 = _rms_fwd("rms1", x, p["g_mix"])
    p = {**p, **exch.wait_first(h1)}
    proj = _mm("mm_in", h1, p["w_in"], "nn", 1024, 640, stack=NCHIP, after=exch.start_rest(), a_full=True)
    q, k, cos_t, sin_t = _qk_prep(proj, pos_col, qg, kg)
    attn, lse = _attn_fwd(q, k, proj)
    mix = _attn_norm(attn, p["g_attn_out"])
    mix, hseq = _rec_fwd(proj, mix, p["rec_conv_w"], p["rec_conv_b"], wrg, wig, brg, big, p["lru_lambda"], p["g_rec_out"])
    rest = exch.wait_rest(mix)
    x1, h2 = _out_rms(mix, rest["w_out"], x, p["g_ffn"])
    up_pre, act, fa, fb = _up_act(h2, rest["w_up"], p["ffn_conv_w"], p["ffn_conv_b"])
    dy, dyb, loss_blk = _down_loss(act, rest["w_down"], x1, target)

    g = {}
    tok = exch.reduce_start("w_down", *_mm("wg_down", act, dyb, "tn", 512, 512, twin_bf16=True))
    dup, g["ffn_conv_w"], g["ffn_conv_b"] = _ffn_bwd(up_pre, fa, fb, dyb, rest["w_down"].T, p["ffn_conv_w"], tok)
    tok = exch.reduce_start("w_up", *_mm("wg_up", h2, dup, "tn", 512, 768, stack=NCHIP, twin_bf16=True, a_full=True))
    dx1, dx1b, g["g_ffn"] = _dgrad_rms_bwd("dg_up", dup, rest["w_up"], x1, p["g_ffn"], dy, True, after=tok)
    tok = exch.reduce_start("w_out", *_mm("wg_out", mix, dx1b, "tn", 512, 512, twin_bf16=True, a_full=True))
    dmix = _mm("dg_out", dx1b, rest["w_out"], "nt", 1024, 512, after=tok, a_full=True)
    do, delta, g["g_attn_out"] = _attn_out_bwd(attn, dmix, p["g_attn_out"])
    dq, dk, dv = _attn_bwd(q, k, proj, do, lse, delta)
    dqkv, dqg, dkg = _qk_bwd(proj, cos_t, sin_t, qg, kg, dq, dk, dv)
    (drec, xcb, dprb, dpib, g["rec_conv_w"], g["rec_conv_b"], dbr, dbi, dsp, g["g_rec_out"]) = _rec_bwd(
        proj, hseq, dmix, p["rec_conv_w"], p["rec_conv_b"], wrg, wig, brg, big, p["lru_lambda"], p["g_rec_out"])
    dproj = jnp.concatenate([dqkv, drec], axis=1)
    g["w_rg"] = _diag_blocks(_mm("wg_rg", xcb, dprb, "tn", 512, 512)).reshape(RW, HD)
    g["w_ig"] = _diag_blocks(_mm("wg_ig", xcb, dpib, "tn", 512, 512)).reshape(RW, HD)
    g["b_rg"], g["b_ig"] = dbr.reshape(8, HD), dbi.reshape(8, HD)
    g["lru_lambda"] = dsp
    g["q_norm_g"] = dqg.reshape(8, HD).sum(axis=0, keepdims=True)
    g["k_norm_g"] = dkg.reshape(8, HD).sum(axis=0, keepdims=True)
    tok = exch.reduce_start("w_in", *_mm("wg_in", h1, dproj, "tn", 512, 640, stack=NCHIP, twin_bf16=True, a_full=True))
    grad_x, g["g_mix"] = _dgrad_rms_bwd("dg_in", dproj, p["w_in"], x, p["g_mix"], dx1, False, after=tok)
    return loss_blk, grad_x, g


ANY = pl.BlockSpec(memory_space=pl.ANY)


def _mesh_pos():
    return lax.axis_index("x"), lax.axis_index("y"), lax.axis_index("c")


def _slot(px, py, perm):
    return 2 * py + px if perm else 2 * px + py


def _other_chips(x, y):
    return [(1 - x, y), (x, 1 - y), (1 - x, 1 - y)]


def _rcopy(src, dst, send, recv, k, to, kr=None):
    return pltpu.make_async_remote_copy(src_ref=src, dst_ref=dst, send_sem=send.at[k],
                                        recv_sem=recv.at[k if kr is None else kr], device_id=to, device_id_type=MESH)


def _cast_bf16(name, w, after=()):
    r, c = w.shape
    tr = 256

    def body(w_ref, *rest):
        rest[-1][...] = w_ref[...].astype(BF16)

    return pl.pallas_call(
        body, name=name, grid=(r // tr,), in_specs=[_bs((tr, c), lambda i: (i, 0))] + [ANY] * len(after),
        out_specs=_bs((tr, c), lambda i: (i, 0)), out_shape=jax.ShapeDtypeStruct((r, c), BF16),
        compiler_params=_cp("parallel"),
    )(w, *after)


def _sibling_fill(lands, perms):
    na = len(lands)

    def body(*refs):
        outs, (send, recv) = refs[na:2 * na], refs[2 * na:]
        x, y, c = _mesh_pos()
        cps = []
        for a in range(na):
            for j, (px, py) in enumerate(_other_chips(x, y)):
                mine = outs[a].at[_slot(px, py, perms[a]), c]
                cps.append(_rcopy(mine, mine, send, recv, 3 * a + j, (x, y, 1 - c)))
        for cp in cps:
            cp.start()
        for a in range(na):
            for j, (px, py) in enumerate(_other_chips(x, y)):
                got = outs[a].at[_slot(px, py, perms[a]), 1 - c]
                _rcopy(got, got, send, recv, 3 * a + j, (x, y, c)).wait_recv()
        for cp in cps:
            cp.wait_send()

    return pl.pallas_call(
        body, name="gather_fill", in_specs=[ANY] * na, out_specs=[ANY] * na,
        out_shape=[jax.ShapeDtypeStruct(a.shape, a.dtype) for a in lands],
        input_output_aliases={i: i for i in range(na)},
        scratch_shapes=[pltpu.SemaphoreType.DMA((3 * na,)), pltpu.SemaphoreType.DMA((3 * na,))],
    )(*lands)


HBM = pl.BlockSpec(memory_space=pltpu.HBM)
SEM = pl.BlockSpec(memory_space=pltpu.SEMAPHORE)
EFFECT = pltpu.SideEffectType.DATAFLOW_SIDE_EFFECTING


def _split_start(name, srcs, lands, plan, nsem):
    ns, nl = len(srcs), len(lands)

    def body(*refs):
        send, recv = refs[ns + nl], refs[ns + nl + 1]
        sends, _ = plan(refs[:ns], refs[ns:ns + nl], send, recv)
        for cp in sends:
            cp.start()
        refs[-1][...] = jnp.zeros((8, 128), F32)

    arrs = list(srcs) + list(lands)
    out = pl.pallas_call(
        body, name=name, in_specs=[HBM] * (ns + nl),
        out_specs=[SEM, SEM] + [HBM] * (ns + nl) + [pl.BlockSpec(memory_space=pltpu.VMEM)],
        out_shape=[pltpu.SemaphoreType.DMA((nsem,)), pltpu.SemaphoreType.DMA((nsem,))]
        + [pltpu.HBM(a.shape, a.dtype) for a in arrs] + [jax.ShapeDtypeStruct((8, 128), F32)],
        input_output_aliases={i: 2 + i for i in range(ns + nl)},
        compiler_params=pltpu.CompilerParams(has_side_effects=EFFECT),
    )(*[pltpu.with_memory_space_constraint(a, pltpu.HBM) for a in arrs])
    return out[0], out[1], out[2:2 + ns], out[2 + ns:2 + ns + nl], out[-1]


def _split_wait(name, send, recv, srcs, lands, plan, after):
    ns, nl = len(srcs), len(lands)

    def body(*refs):
        sends, recvs = plan(refs[:ns], refs[ns:ns + nl], refs[ns + nl], refs[ns + nl + 1])
        for cp in sends:
            cp.wait_send()
        for cp in recvs:
            cp.wait_recv()

    arrs = list(srcs) + list(lands)
    after = tuple(after) if isinstance(after, (tuple, list)) else (after,)
    out = pl.pallas_call(
        body, name=name, in_specs=[HBM] * (ns + nl) + [SEM, SEM] + [ANY] * len(after), out_specs=[HBM] * (ns + nl),
        out_shape=[pltpu.HBM(a.shape, a.dtype) for a in arrs],
        input_output_aliases={i: i for i in range(ns + nl)},
        compiler_params=pltpu.CompilerParams(has_side_effects=EFFECT),
    )(*arrs, send, recv, *after)
    return out[ns:]


def _gather_plan(perms):
    def plan(srcs, lands, send, recv):
        x, y, c = _mesh_pos()
        sends, recvs = [], []
        for a, perm in enumerate(perms):
            for j, (px, py) in enumerate(_other_chips(x, y)):
                for cc in (0, 1):
                    k = 6 * a + 2 * j + cc
                    sends.append(_rcopy(srcs[a].at[c], lands[a].at[_slot(x, y, perm), c], send, recv, k, (px, py, cc),
                                        kr=6 * a + 2 * j + c))
                    got = lands[a].at[_slot(px, py, perm), cc]
                    recvs.append(_rcopy(got, got, send, recv, k, (x, y, c)))
        return sends, recvs
    return plan


def _gather_half_plan(perms, halved):
    def plan(srcs, lands, send, recv):
        x, y, c = _mesh_pos()
        sends, recvs = [], []
        for a, perm in enumerate(perms):
            for j, (px, py) in enumerate(_other_chips(x, y)):
                k = 3 * a + j
                mine, theirs = _slot(x, y, perm), _slot(px, py, perm)
                if halved[a]:
                    sends.append(_rcopy(srcs[a].at[c], lands[a].at[mine, c], send, recv, k, (px, py, c)))
                    got = lands[a].at[theirs, c]
                else:
                    sends.append(_rcopy(srcs[a], lands[a].at[mine], send, recv, k, (px, py, c)))
                    got = lands[a].at[theirs]
                recvs.append(_rcopy(got, got, send, recv, k, (x, y, c)))
        return sends, recvs
    return plan


def _reduce_plan(perm):
    def plan(srcs, lands, send, recv):
        x, y, c = _mesh_pos()
        src, land = srcs[0], lands[0]
        sends = []
        for j, (px, py) in enumerate(_other_chips(x, y)):
            for hf in (0, 1):
                sends.append(_rcopy(src.at[_slot(px, py, perm), hf], land.at[2 * j + c], send, recv, 2 * j + hf,
                                    (px, py, hf), kr=2 * j + c))
        sends.append(_rcopy(src.at[_slot(x, y, perm), 1 - c], land.at[6], send, recv, 6, (x, y, 1 - c)))
        recvs = [_rcopy(land.at[i], land.at[i], send, recv, i, (x, y, c)) for i in range(7)]
        return sends, recvs
    return plan


def _sibling_share(rs):
    na = len(rs)

    def body(*refs):
        ins, outs, (send, recv) = refs[:na], refs[na:2 * na], refs[2 * na:]
        x, y, c = _mesh_pos()
        cps = [_rcopy(ins[a], outs[a], send, recv, a, (x, y, 1 - c)) for a in range(na)]
        for cp in cps:
            cp.start()
        for cp in cps:
            cp.wait()

    return pl.pallas_call(
        body, name="rs_share", in_specs=[ANY] * na, out_specs=[ANY] * na,
        out_shape=[jax.ShapeDtypeStruct(r.shape, F32) for r in rs],
        scratch_shapes=[pltpu.SemaphoreType.DMA((na,)), pltpu.SemaphoreType.DMA((na,))],
    )(*rs)


def _add_pieces(name, g, got, where):
    _, _, r2, cc = g.shape
    tr = 256 if r2 % 256 == 0 else 128

    def body(w_ref, g_ref, r_ref, o_ref):
        del w_ref
        acc = g_ref[...]
        for i in range(7):
            acc = acc + r_ref[i].astype(F32)
        o_ref[...] = acc

    return pl.pallas_call(
        body, name=name,
        grid_spec=pltpu.PrefetchScalarGridSpec(
            num_scalar_prefetch=1, grid=(r2 // tr,),
            in_specs=[_bs((None, None, tr, cc), lambda i, w_ref: (w_ref[0], w_ref[1], i, 0)),
                      _bs((7, tr, cc), lambda i, w_ref: (0, i, 0))],
            out_specs=_bs((tr, cc), lambda i, w_ref: (i, 0))),
        out_shape=jax.ShapeDtypeStruct((r2, cc), F32), compiler_params=_cp("parallel"),
    )(where, g, got)


def _adam_math(w, g, m, v):
    m = ADAM_B1 * m + (1.0 - ADAM_B1) * g
    v = ADAM_B2 * v + (1.0 - ADAM_B2) * (g * g)
    m_hat = m / (1.0 - ADAM_B1 ** ADAM_STEP)
    v_hat = v / (1.0 - ADAM_B2 ** ADAM_STEP)
    return -ADAM_LR * (m_hat / (jnp.sqrt(v_hat) + ADAM_EPS) + ADAM_WD * w), m, v


def _adam_big(name, w, g_mine, g_sib, m, v, c_arr):
    r, cols = w.shape
    tr = 256 if (r // 2) % 256 == 0 else 128
    per = r // 2 // tr

    def body(c_ref, w_ref, a_ref, b_ref, m_ref, v_ref, g_ref, d_ref, m2_ref, v2_ref):
        g = jnp.where(pl.program_id(0) == c_ref[0], a_ref[...], b_ref[...])
        g_ref[...] = g
        d_ref[...], m2_ref[...], v2_ref[...] = _adam_math(w_ref[...], g, m_ref[...], v_ref[...])

    spec = _bs((tr, cols), lambda h, i, c_ref: (h * per + i, 0))
    half = _bs((tr, cols), lambda h, i, c_ref: (i, 0))
    out = jax.ShapeDtypeStruct((r, cols), F32)
    return pl.pallas_call(
        body, name=name,
        grid_spec=pltpu.PrefetchScalarGridSpec(
            num_scalar_prefetch=1, grid=(2, per), in_specs=[spec, half, half, spec, spec], out_specs=[spec] * 4),
        out_shape=[out] * 4, compiler_params=_cp("parallel", "parallel"),
    )(c_arr, w, g_mine, g_sib, m, v)


_CLASS_SHAPE = {"a": (8, D), "b": (8, RW), "c": (8, 2 * DFF), "d": (1048, HD)}
_SMALL = (
    ("g_mix", "a", 0, 1, D), ("g_ffn", "a", 1, 1, D),
    ("rec_conv_w", "b", 0, 4, RW), ("rec_conv_b", "b", 4, 1, RW), ("lru_lambda", "b", 5, 1, RW),
    ("g_attn_out", "b", 6, 1, RW), ("g_rec_out", "b", 7, 1, RW),
    ("ffn_conv_w", "c", 0, 3, 2 * DFF), ("ffn_conv_b", "c", 3, 1, 2 * DFF),
    ("w_rg", "d", 0, RW, HD), ("w_ig", "d", RW, RW, HD), ("b_rg", "d", 2 * RW, 8, HD), ("b_ig", "d", 2 * RW + 8, 8, HD),
    ("q_norm_g", "d", 2 * RW + 16, 1, HD), ("k_norm_g", "d", 2 * RW + 17, 1, HD),
)
_LOSS_ROW = 2
_CLASSES = ("a", "b", "c", "d")
_CLASS_OWNER = {"a": 0, "b": 0, "c": 0, "d": 1}


def _small_allreduce(g, loss_blk):
    names = [s[0] for s in _SMALL]
    nin = len(names) + 1

    def body(*refs):
        ins = dict(zip(names, refs[:len(names)]))
        loss_ref = refs[len(names)]
        outs = dict(zip(_CLASSES, refs[nin:nin + 4]))
        pair = dict(zip(_CLASSES, refs[nin + 4:nin + 8]))
        quad = dict(zip(_CLASSES, refs[nin + 8:nin + 12]))
        send, recv = refs[nin + 12:]
        x, y, c = _mesh_pos()
        chip = 2 * x + y
        pair["a"][c] = jnp.zeros(_CLASS_SHAPE["a"], F32)
        pair["b"][c] = ins["rec_conv_w"][...]
        pair["c"][c] = ins["ffn_conv_w"][...]
        pair["d"][c, 2 * RW + 16:, :] = jnp.zeros((8, HD), F32)
        for name, k, r0, nr, _ in _SMALL:
            if name in ("rec_conv_w", "ffn_conv_w"):
                continue
            pair[k][c, r0:r0 + nr, :] = ins[name][...]
        pair["a"][c, _LOSS_ROW:_LOSS_ROW + 1, :] = jnp.broadcast_to(loss_ref[0:1, 0:1], (1, D))
        cps = [_rcopy(pair[k].at[c], pair[k].at[c], send, recv, ki, (x, y, 1 - c)) for ki, k in enumerate(_CLASSES)]
        for cp in cps:
            cp.start()
        for ki, k in enumerate(_CLASSES):
            _rcopy(pair[k].at[1 - c], pair[k].at[1 - c], send, recv, ki, (x, y, c)).wait_recv()
            quad[k][chip] = pair[k][0] + pair[k][1]
        for cp in cps:
            cp.wait_send()
        for ki, k in enumerate(_CLASSES):
            owner = _CLASS_OWNER[k]

            @pl.when(c == owner)
            def _(ki=ki, k=k):
                cps2 = [_rcopy(quad[k].at[chip], quad[k].at[chip], send, recv, 4 + 3 * ki + j, (px, py, c))
                        for j, (px, py) in enumerate(_other_chips(x, y))]
                for cp in cps2:
                    cp.start()
                for j, (px, py) in enumerate(_other_chips(x, y)):
                    got = quad[k].at[2 * px + py]
                    _rcopy(got, got, send, recv, 4 + 3 * ki + j, (x, y, c)).wait_recv()
                outs[k][...] = ((quad[k][0] + quad[k][1]) + quad[k][2]) + quad[k][3]
                share = _rcopy(outs[k], outs[k], send, recv, 16 + ki, (x, y, 1 - c))
                share.start()
                for cp in cps2:
                    cp.wait_send()
                share.wait_send()

        for ki, k in enumerate(_CLASSES):
            @pl.when(c != _CLASS_OWNER[k])
            def _(ki=ki, k=k):
                _rcopy(outs[k], outs[k], send, recv, 16 + ki, (x, y, c)).wait_recv()

    vm = pl.BlockSpec(memory_space=pltpu.VMEM)
    return pl.pallas_call(
        body, name="small_allreduce", in_specs=[vm] * nin, out_specs=[vm] * 4,
        out_shape=[jax.ShapeDtypeStruct(_CLASS_SHAPE[k], F32) for k in _CLASSES],
        scratch_shapes=[pltpu.VMEM((2,) + _CLASS_SHAPE[k], F32) for k in _CLASSES]
        + [pltpu.VMEM((NCHIP,) + _CLASS_SHAPE[k], F32) for k in _CLASSES]
        + [pltpu.SemaphoreType.DMA((20,)), pltpu.SemaphoreType.DMA((20,))],
        compiler_params=pltpu.CompilerParams(vmem_limit_bytes=VMEM_LIMIT),
    )(*[g[n] for n in names], loss_blk)


def _adam_small(red, w, m, v):
    names = [s[0] for s in _SMALL]
    n = len(names)

    def body(*refs):
        red_refs = dict(zip(_CLASSES, refs[:4]))
        w_refs, m_refs, v_refs = refs[4:4 + n], refs[4 + n:4 + 2 * n], refs[4 + 2 * n:4 + 3 * n]
        loss_ref = refs[4 + 3 * n]
        out_refs = refs[5 + 3 * n:]
        x, y, _ = _mesh_pos()
        chip = 2 * x + y
        loss_ref[...] = jnp.broadcast_to(red_refs["a"][_LOSS_ROW:_LOSS_ROW + 1, 0:1], loss_ref.shape)
        for pi, (name, k, r0, nr, width) in enumerate(_SMALL):
            gfull = red_refs[k][r0:r0 + nr, :]
            if name == "rec_conv_w":
                parts = [gfull[:, 128 * s:128 * (s + 1)] for s in range(NCHIP)]
                g = jnp.where(chip == 0, parts[0], jnp.where(chip == 1, parts[1], jnp.where(chip == 2, parts[2], parts[3])))
            elif name == "ffn_conv_w":
                parts = [gfull[:, FC * s:FC * (s + 1)] for s in range(NCHIP)]
                g = jnp.where(chip == 0, parts[0], jnp.where(chip == 1, parts[2], jnp.where(chip == 2, parts[1], parts[3])))
            elif name == "ffn_conv_b":
                g = jnp.concatenate([gfull[:, FC * s:FC * (s + 1)] for s in (0, 2, 1, 3)], axis=1)
            else:
                g = gfull
            d, m2, v2 = _adam_math(w_refs[pi][...], g, m_refs[pi][...], v_refs[pi][...])
            o = out_refs[4 * pi:4 * pi + 4]
            o[0][...], o[1][...], o[2][...], o[3][...] = g, d, m2, v2

    vm = pl.BlockSpec(memory_space=pltpu.VMEM)
    outs = [jax.ShapeDtypeStruct((1, 128), F32)]
    for name in names:
        outs += [jax.ShapeDtypeStruct(w[name].shape, F32)] * 4
    res = pl.pallas_call(
        body, name="adam_small", in_specs=[vm] * (4 + 3 * n), out_specs=[vm] * len(outs), out_shape=outs,
        compiler_params=pltpu.CompilerParams(vmem_limit_bytes=VMEM_LIMIT),
    )(*red, *[w[k] for k in names], *[m[k] for k in names], *[v[k] for k in names])
    return res[0], {name: res[1 + 4 * i:5 + 4 * i] for i, name in enumerate(names)}


_WEIGHTS = ("g_mix", "w_in", "q_norm_g", "k_norm_g", "rec_conv_w", "rec_conv_b", "w_rg", "b_rg", "w_ig", "b_ig",
            "lru_lambda", "g_attn_out", "g_rec_out", "w_out", "g_ffn", "w_up", "ffn_conv_w", "ffn_conv_b", "w_down")
_BIG = ("w_in", "w_out", "w_up", "w_down")
_BIG_PERM = {"w_in": False, "w_out": False, "w_up": True, "w_down": False}
_SMALL_2D = {"w_rg": (RW, HD), "w_ig": (RW, HD), "b_rg": (8, HD), "b_ig": (8, HD), "rec_conv_w": (4, 128),
             "ffn_conv_w": (3, FC)}


def _halves(a):
    r, c = a.shape
    return a.reshape(2, r // 2, c)


def kernel(x, positions, g_mix, w_in, q_norm_g, k_norm_g, rec_conv_w, rec_conv_b, w_rg, b_rg, w_ig, b_ig, lru_lambda, g_attn_out, g_rec_out, w_out, g_ffn, w_up, ffn_conv_w, ffn_conv_b, w_down, loss_target, m_g_mix, m_w_in, m_q_norm_g, m_k_norm_g, m_rec_conv_w, m_rec_conv_b, m_w_rg, m_b_rg, m_w_ig, m_b_ig, m_lru_lambda, m_g_attn_out, m_g_rec_out, m_w_out, m_g_ffn, m_w_up, m_ffn_conv_w, m_ffn_conv_b, m_w_down, v_g_mix, v_w_in, v_q_norm_g, v_k_norm_g, v_rec_conv_w, v_rec_conv_b, v_w_rg, v_b_rg, v_w_ig, v_b_ig, v_lru_lambda, v_g_attn_out, v_g_rec_out, v_w_out, v_g_ffn, v_w_up, v_ffn_conv_w, v_ffn_conv_b, v_w_down):
    given = dict(g_mix=g_mix, w_in=w_in, q_norm_g=q_norm_g, k_norm_g=k_norm_g, rec_conv_w=rec_conv_w, rec_conv_b=rec_conv_b, w_rg=w_rg, b_rg=b_rg, w_ig=w_ig, b_ig=b_ig, lru_lambda=lru_lambda, g_attn_out=g_attn_out, g_rec_out=g_rec_out, w_out=w_out, g_ffn=g_ffn, w_up=w_up, ffn_conv_w=ffn_conv_w, ffn_conv_b=ffn_conv_b, w_down=w_down)
    given_m = dict(g_mix=m_g_mix, w_in=m_w_in, q_norm_g=m_q_norm_g, k_norm_g=m_k_norm_g, rec_conv_w=m_rec_conv_w, rec_conv_b=m_rec_conv_b, w_rg=m_w_rg, b_rg=m_b_rg, w_ig=m_w_ig, b_ig=m_b_ig, lru_lambda=m_lru_lambda, g_attn_out=m_g_attn_out, g_rec_out=m_g_rec_out, w_out=m_w_out, g_ffn=m_g_ffn, w_up=m_w_up, ffn_conv_w=m_ffn_conv_w, ffn_conv_b=m_ffn_conv_b, w_down=m_w_down)
    given_v = dict(g_mix=v_g_mix, w_in=v_w_in, q_norm_g=v_q_norm_g, k_norm_g=v_k_norm_g, rec_conv_w=v_rec_conv_w, rec_conv_b=v_rec_conv_b, w_rg=v_w_rg, b_rg=v_b_rg, w_ig=v_w_ig, b_ig=v_b_ig, lru_lambda=v_lru_lambda, g_attn_out=v_g_attn_out, g_rec_out=v_g_rec_out, w_out=v_w_out, g_ffn=v_g_ffn, w_up=v_w_up, ffn_conv_w=v_ffn_conv_w, ffn_conv_b=v_ffn_conv_b, w_down=v_w_down)
    shapes = {n: a.shape for n, a in given.items()}

    def two_d(n, a):
        a = a[0]
        return a.reshape(_SMALL_2D[n]) if n in _SMALL_2D else (a if a.ndim == 2 else a[None])

    w = {n: two_d(n, a) for n, a in given.items()}
    m = {n: two_d(n, a) for n, a in given_m.items()}
    v = {n: two_d(n, a) for n, a in given_v.items()}
    cc = lax.axis_index("c").astype(jnp.int32)
    cx, cy = lax.axis_index("x").astype(jnp.int32), lax.axis_index("y").astype(jnp.int32)
    slot = {False: 2 * cx + cy, True: 2 * cy + cx}

    shards = {"w_in": _halves(_cast_bf16("cast_w_in", w["w_in"]))}
    first = [shards["w_in"], jnp.pad(w["ffn_conv_w"], ((0, 5), (0, 0))), jnp.pad(w["rec_conv_w"], ((0, 4), (0, 0)))]
    first_perm = [False, True, False]
    first_plan = _gather_half_plan(first_perm, [True, False, False])
    in_flight = _split_start(
        "gather_in_start", first,
        [lax.dynamic_update_slice(lax.empty((NCHIP,) + a.shape, a.dtype), a[None], (slot[pm],) + (0,) * a.ndim)
         for a, pm in zip(first, first_perm)], first_plan, 3 * len(first))
    for n in ("w_out", "w_up", "w_down"):
        shards[n] = _halves(_cast_bf16(f"cast_{n}", w[n], after=(in_flight[4],)))
    p = {n: w[n] for n in ("g_mix", "g_ffn", "q_norm_g", "k_norm_g", "rec_conv_b", "lru_lambda", "g_attn_out", "g_rec_out")}
    p.update(w_rg=w["w_rg"].reshape(8, HD, HD), w_ig=w["w_ig"].reshape(8, HD, HD), b_rg=w["b_rg"], b_ig=w["b_ig"],
             ffn_conv_b=jnp.concatenate([w["ffn_conv_b"][:, FC * s:FC * (s + 1)] for s in (0, 2, 1, 3)], axis=1))

    class Exchange:
        rest = ("w_out", "w_up", "w_down")
        order = []
        flight = {}

        def wait_first(self, after):
            send, recv, srcs, lands, _ = in_flight
            f_in, f_fcw, f_rcw = _split_wait("gather_in_wait", send, recv, srcs, lands, first_plan,
                                             (after,) + tuple(shards[n] for n in self.rest))
            (f_in,) = _sibling_fill([f_in], [False])
            return dict(w_in=f_in.reshape(NCHIP, D, INW // NCHIP), ffn_conv_w=f_fcw,
                        rec_conv_w=f_rcw.transpose(1, 0, 2).reshape(8, RW))

        def start_rest(self):
            srcs = [shards[n] for n in self.rest]
            lands = [lax.dynamic_update_slice(lax.empty((NCHIP,) + s.shape, BF16), s[None], (slot[_BIG_PERM[n]], 0, 0, 0))
                     for n, s in zip(self.rest, srcs)]
            plan = _gather_plan([_BIG_PERM[n] for n in self.rest])
            send, recv, srcs, lands, token = _split_start("gather_rest_start", srcs, lands, plan, 6 * len(srcs))
            self.flight["rest"] = (send, recv, srcs, lands, plan)
            return (token,)

        def wait_rest(self, after):
            send, recv, srcs, lands, plan = self.flight.pop("rest")
            f_out, f_up, f_down = _split_wait("gather_rest_wait", send, recv, srcs, lands, plan, after)
            return dict(w_out=f_out.reshape(D, D), w_up=f_up.reshape(NCHIP, D, FC), w_down=f_down.reshape(DFF, D))

        def reduce_start(self, name, g32, g16):
            r2, cols = shards[name].shape[1:]
            plan = _reduce_plan(_BIG_PERM[name])
            send, recv, srcs, lands, token = _split_start(
                f"reduce_{name}_start", [g16.reshape(NCHIP, 2, r2, cols)], [lax.empty((7, r2, cols), BF16)], plan, 7)
            self.flight[name] = (send, recv, srcs, lands, plan, g32.reshape(NCHIP, 2, r2, cols))
            self.order.append(name)
            return (token,)

        def finish(self, after):
            mine = {}
            for name in self.order:
                send, recv, srcs, lands, plan, g32 = self.flight.pop(name)
                (got,) = _split_wait(f"reduce_{name}_wait", send, recv, srcs, lands, plan, after)
                where = jnp.stack([slot[_BIG_PERM[name]], cc])
                mine[name] = after = _add_pieces(f"reduce_{name}_add", g32, got, where)
            theirs = dict(zip(_BIG, _sibling_share([mine[n] for n in _BIG])))
            return mine, theirs

    exch = Exchange()

    loss_blk, grad_x, g = _local_step(x[0], positions.reshape(T, 1), loss_target[0], p, exch)

    out_g, out_d, out_m, out_v = {}, {}, {}, {}
    red = _small_allreduce(g, loss_blk)
    loss_row, small_out = _adam_small(red, w, m, v)
    for n, (gn, dn, mn, vn) in small_out.items():
        out_g[n], out_d[n], out_m[n], out_v[n] = gn, dn, mn, vn

    mine, theirs = exch.finish(red[0])
    for n in _BIG:
        out_g[n], out_d[n], out_m[n], out_v[n] = _adam_big(f"adam_{n}", w[n], mine[n], theirs[n], m[n], v[n], cc.reshape(1))

    outs = [loss_row[0, 0], grad_x[None]]
    for group in (out_g, out_d, out_m, out_v):
        outs += [group[n].reshape(shapes[n]) for n in _WEIGHTS]
    return tuple(outs)
```

```python
import math

import jax
import jax.numpy as jnp
import numpy as np
from jax import lax
from jax.experimental import pallas as pl
from jax.experimental.pallas import tpu as pltpu

F32 = jnp.float32
BF16 = jnp.bfloat16

T = 4096
D = 1024
HD = 64
AW = 512
RW = 512
INW = 2560
DFF = 3072
NCHIP = 4
EPS = 1e-6
NEG = -1e30
LRU_C = 8.0
ROPE_THETA = 10000.0
BLK = 128
DILATIONS = (1, 4, 16)
ADAM_LR, ADAM_B1, ADAM_B2, ADAM_EPS, ADAM_WD, ADAM_STEP = 0.001, 0.9, 0.999, 1e-08, 0.01, 10
VMEM_LIMIT = 56 * 1024 * 1024
MESH = pl.DeviceIdType.MESH

NN = (((1,), (0,)), ((), ()))
NT = (((1,), (1,)), ((), ()))
TN = (((0,), (0,)), ((), ()))


def _cp(*sem):
    return pltpu.CompilerParams(dimension_semantics=sem, vmem_limit_bytes=VMEM_LIMIT)


def _bs(shape, fn):
    return pl.BlockSpec(shape, fn)


def _dot(a, b, dims=NN):
    return lax.dot_general(a, b, dims, preferred_element_type=F32)


_GC = math.sqrt(2.0 / math.pi)


def _gelu(x):
    return x * (0.5 + 0.5 * jnp.tanh(x * (_GC + (_GC * 0.044715) * (x * x))))


def _gelu_and_grad(x):
    x2 = x * x
    th = jnp.tanh(x * (_GC + (_GC * 0.044715) * x2))
    cdf = 0.5 + 0.5 * th
    dg = cdf + (x * (1.0 - th * th)) * ((0.5 * _GC) + (1.5 * 0.044715 * _GC) * x2)
    return x * cdf, dg


def _softplus(x):
    e = jnp.exp(-jnp.abs(x))
    u = 1.0 + e
    l1p = jnp.where(u == 1.0, e, jnp.log(u) * (e / (u - 1.0)))
    return jnp.maximum(x, 0.0) + l1p


def _segsum(z, e_bf16):
    hi = z.astype(BF16)
    lo = (z - hi.astype(F32)).astype(BF16)
    parts = []
    for c0 in range(0, z.shape[1], 128):
        parts.append(_dot(hi[:, c0:c0 + 128], e_bf16) + _dot(lo[:, c0:c0 + 128], e_bf16))
    return jnp.concatenate(parts, axis=1)


def _mm(name, a, b, mode, tm, tn, out_dtype=F32, stack=0, twin_bf16=False, after=(), a_full=False):
    if mode == "nn":
        (m, k), n = a.shape, (b.shape[1] if not stack else stack * b.shape[2])
        a_spec = _bs((tm, k), lambda j, i: (i, 0))
        if stack:
            per = b.shape[2] // tn
            b_spec = _bs((None, k, tn), lambda j, i: (j // per, 0, j % per))
        else:
            b_spec = _bs((k, tn), lambda j, i: (0, j))
    elif mode == "nt":
        (m, k), n = a.shape, b.shape[0]
        a_spec, b_spec = _bs((tm, k), lambda j, i: (i, 0)), _bs((tn, k), lambda j, i: (j, 0))
    else:
        (k, m), n = a.shape, b.shape[1]
        a_spec, b_spec = _bs((k, tm), lambda j, i: (0, i)), _bs((k, tn), lambda j, i: (0, j))
    assert m % tm == 0 and n % tn == 0
    o_spec = _bs((tm, tn), lambda j, i: (i, j))
    o_shape = (m, n)
    if mode == "tn" and stack:
        per = n // stack // tn
        o_spec = _bs((None, tm, tn), lambda j, i: (j // per, i, j % per))
        o_shape = (stack, m, n // stack)
    dims = {"nn": NN, "nt": NT, "tn": TN}[mode]
    if a_full:
        a_spec = pl.BlockSpec(a.shape, lambda j, i: (0, 0), pipeline_mode=pl.Buffered(1))

    def body(a_ref, b_ref, *rest):
        if a_full:
            mine = pl.ds(pl.multiple_of(pl.program_id(1) * tm, tm), tm)
            av = a_ref[:, mine] if mode == "tn" else a_ref[mine, :]
        else:
            av = a_ref[...]
        acc = _dot(av, b_ref[...], dims)
        outs = rest[len(after):]
        outs[0][...] = acc.astype(out_dtype)
        if twin_bf16:
            outs[1][...] = acc.astype(BF16)

    ins = (a, b) + tuple(after)
    specs = [a_spec, b_spec] + [pl.BlockSpec(memory_space=pl.ANY)] * len(after)
    shapes = [jax.ShapeDtypeStruct(o_shape, out_dtype)] + ([jax.ShapeDtypeStruct(o_shape, BF16)] if twin_bf16 else [])
    out = pl.pallas_call(
        body, name=name, grid=(n // tn, m // tm), in_specs=specs, out_specs=[o_spec] * len(shapes),
        out_shape=shapes, compiler_params=_cp("parallel", "parallel"),
    )(*ins)
    return tuple(out) if twin_bf16 else out[0]


def _rms_fwd(name, x, g):
    tr = 1024

    def body(x_ref, g_ref, o_ref):
        xv = x_ref[...]
        r = lax.rsqrt(jnp.mean(xv * xv, axis=-1, keepdims=True) + EPS)
        o_ref[...] = ((xv * r) * g_ref[...]).astype(BF16)

    return pl.pallas_call(
        body, name=name, grid=(T // tr,), in_specs=[_bs((tr, D), lambda i: (i, 0)), _bs((1, D), lambda i: (0, 0))],
        out_specs=_bs((tr, D), lambda i: (i, 0)), out_shape=jax.ShapeDtypeStruct((T, D), BF16),
        compiler_params=_cp("parallel"),
    )(x, g)


def _dgrad_rms_bwd(name, a, w, x, g, dres, want_bf16, after=()):
    tm = 512
    stack, _, cs = w.shape
    k = a.shape[1]

    def body(a_ref, w_ref, x_ref, g_ref, dr_ref, *rest):
        rest = rest[len(after):]
        dx_ref, dg_ref = rest[0], rest[-1]
        dyv = _dot(a_ref[:, 0:cs], w_ref[0], NT)
        for s in range(1, stack):
            dyv = dyv + _dot(a_ref[:, s * cs:(s + 1) * cs], w_ref[s], NT)
        xv = x_ref[...]
        r = lax.rsqrt(jnp.mean(xv * xv, axis=-1, keepdims=True) + EPS)
        gdy = g_ref[...] * dyv
        dx = r * gdy - xv * ((r * r * r) * jnp.mean(xv * gdy, axis=-1, keepdims=True)) + dr_ref[...]
        dx_ref[...] = dx
        if want_bf16:
            rest[1][...] = dx.astype(BF16)

        @pl.when(pl.program_id(0) == 0)
        def _():
            dg_ref[...] = jnp.zeros_like(dg_ref)

        dg_ref[...] += jnp.sum(dyv * (xv * r), axis=0, keepdims=True)

    row = _bs((tm, D), lambda i: (i, 0))
    vec = _bs((1, D), lambda i: (0, 0))
    outs = [jax.ShapeDtypeStruct((T, D), F32)] + ([jax.ShapeDtypeStruct((T, D), BF16)] if want_bf16 else [])
    return pl.pallas_call(
        body, name=name, grid=(T // tm,),
        in_specs=[_bs((tm, k), lambda i: (i, 0)),
                  pl.BlockSpec(w.shape, lambda i: (0, 0, 0), pipeline_mode=pl.Buffered(1)), row, vec, row]
        + [pl.BlockSpec(memory_space=pl.ANY)] * len(after),
        out_specs=[row] * len(outs) + [vec], out_shape=outs + [jax.ShapeDtypeStruct((1, D), F32)],
        compiler_params=_cp("arbitrary"),
    )(a, w, x, g, dres, *after)


def _out_rms(mix, w_out, x, g):
    tm = 512

    def body(a_ref, b_ref, r_ref, g_ref, x1_ref, h_ref):
        x1 = r_ref[...] + _dot(a_ref[...], b_ref[...])
        x1_ref[...] = x1
        r = lax.rsqrt(jnp.mean(x1 * x1, axis=-1, keepdims=True) + EPS)
        h_ref[...] = ((x1 * r) * g_ref[...]).astype(BF16)

    row = _bs((tm, D), lambda i: (i, 0))
    return pl.pallas_call(
        body, name="mm_out", grid=(T // tm,),
        in_specs=[row, pl.BlockSpec((D, D), lambda i: (0, 0), pipeline_mode=pl.Buffered(1)), row,
                  _bs((1, D), lambda i: (0, 0))],
        out_specs=[row, row], out_shape=[jax.ShapeDtypeStruct((T, D), F32), jax.ShapeDtypeStruct((T, D), BF16)],
        compiler_params=_cp("parallel"),
    )(mix, w_out, x, g)


def _head_ones():
    idx = np.arange(128) // HD
    return jnp.asarray((idx[:, None] == idx[None, :]).astype(np.float32), dtype=BF16)


def _freq_row():
    half = HD // 2
    inv = ROPE_THETA ** (-(np.arange(half, dtype=np.float64)) / half)
    return jnp.asarray(np.tile(inv, 4)[None, :], dtype=F32)


def _rot_tables(cos128, sin128):
    c = jnp.tile(cos128, (1, 4))
    s = jnp.tile(sin128, (1, 4))
    lane = lax.broadcasted_iota(jnp.int32, (1, AW), 1)
    first = (lane & 32) == 0
    return c, jnp.where(first, -s, s), first


def _swap_halves(y, first):
    return jnp.where(first, pltpu.roll(y, AW - 32, 1), pltpu.roll(y, 32, 1))


def _qk_prep(proj, pos_col, qg, kg):
    tr = 512

    def body(q_ref, k_ref, pos_ref, f_ref, qg_ref, kg_ref, e_ref, qo_ref, ko_ref, cos_ref, sin_ref):
        ang = pos_ref[...].astype(F32) * f_ref[...]
        cos_ref[...] = jnp.cos(ang)
        sin_ref[...] = jnp.sin(ang)
        c, s_signed, first = _rot_tables(cos_ref[...], sin_ref[...])
        e = e_ref[...]

        def norm_rot(xv, g, scale):
            r = lax.rsqrt(_segsum(xv * xv, e) * (1.0 / HD) + EPS)
            y = (xv * r) * g
            return (y * c + _swap_halves(y, first) * s_signed) * scale

        qo_ref[...] = norm_rot(q_ref[...], qg_ref[...], HD ** -0.5)
        ko_ref[...] = norm_rot(k_ref[...], kg_ref[...], 1.0)

    col = lambda j: _bs((tr, AW), lambda i, j=j: (i, j))
    vec = _bs((1, AW), lambda i: (0, 0))
    out = jax.ShapeDtypeStruct((T, AW), F32)
    tab = jax.ShapeDtypeStruct((T, 128), F32)
    tspec = _bs((tr, 128), lambda i: (i, 0))
    return pl.pallas_call(
        body, name="qk_prep", grid=(T // tr,),
        in_specs=[col(0), col(1), _bs((tr, 1), lambda i: (i, 0)), _bs((1, 128), lambda i: (0, 0)), vec, vec,
                  _bs((128, 128), lambda i: (0, 0))],
        out_specs=[col(0)] * 2 + [tspec] * 2, out_shape=[out, out, tab, tab], compiler_params=_cp("parallel"),
    )(proj, proj, pos_col, _freq_row(), qg, kg, _head_ones())


def _qk_bwd(proj, cos_t, sin_t, qg, kg, dq, dk, dv):
    tr = 1024

    def body(q_ref, k_ref, cos_ref, sin_ref, qg_ref, kg_ref, e_ref, dq_ref, dk_ref, dv_ref, o_ref, dqg_ref, dkg_ref):
        i, j = pl.program_id(0), pl.program_id(1)

        @pl.when((i == 0) & (j == 0))
        def _():
            dqg_ref[...] = jnp.zeros_like(dqg_ref)
            dkg_ref[...] = jnp.zeros_like(dkg_ref)

        def norm_rot_bwd(x_ref, g_ref, dg_ref, d_ref, scale):
            c, s_signed, first = _rot_tables(cos_ref[...], sin_ref[...])
            e = e_ref[...]
            dout = d_ref[...] * scale
            dy = dout * c + _swap_halves(dout * s_signed, first)
            xv, g = x_ref[...], g_ref[...]
            r = lax.rsqrt(_segsum(xv * xv, e) * (1.0 / HD) + EPS)
            gdy = g * dy
            dx = r * gdy - xv * ((r * r * r) * (_segsum(xv * gdy, e) * (1.0 / HD)))
            o_ref[...] = dx.astype(BF16)
            dg_ref[...] += jnp.sum(dy * (xv * r), axis=0, keepdims=True)

        @pl.when(j == 0)
        def _():
            o_ref[...] = dv_ref[...].astype(BF16)

        @pl.when(j == 1)
        def _():
            norm_rot_bwd(q_ref, qg_ref, dqg_ref, dq_ref, HD ** -0.5)

        @pl.when(j == 2)
        def _():
            norm_rot_bwd(k_ref, kg_ref, dkg_ref, dk_ref, 1.0)

    col = lambda jj: _bs((tr, AW), lambda i, j, jj=jj: (i, jj))
    vec = _bs((1, AW), lambda i, j: (0, 0))
    piece = _bs((tr, AW), lambda i, j: (i, 0))
    return pl.pallas_call(
        body, name="qk_bwd", grid=(T // tr, 3),
        in_specs=[col(0), col(1), _bs((tr, 128), lambda i, j: (i, 0)), _bs((tr, 128), lambda i, j: (i, 0)), vec, vec,
                  _bs((128, 128), lambda i, j: (0, 0))] + [piece] * 3,
        out_specs=[_bs((tr, AW), lambda i, j: (i, (j + 2) % 3)), vec, vec],
        out_shape=[jax.ShapeDtypeStruct((T, 3 * AW), BF16), jax.ShapeDtypeStruct((1, AW), F32),
                   jax.ShapeDtypeStruct((1, AW), F32)],
        compiler_params=_cp("arbitrary", "arbitrary"),
    )(proj, proj, cos_t, sin_t, qg, kg, _head_ones(), dq, dk, dv)


RG = 256


def _stacked_band_mask():
    qi = lax.broadcasted_iota(jnp.int32, (2 * BLK, 2 * BLK), 0) & (BLK - 1)
    kj = lax.broadcasted_iota(jnp.int32, (2 * BLK, 2 * BLK), 1)
    rel = qi - kj + BLK
    return (rel >= 0) & (rel <= BLK), lax.broadcasted_iota(jnp.int32, (1, 2 * BLK), 1) >= BLK


def _natural_rows(r0, n_rows, d):
    if d == 1:
        return pl.ds(r0, n_rows)
    ln = T // d
    return pl.ds(r0 // ln + d * (r0 % ln), n_rows, stride=d)


def _regroup_into(dst, src_ref, d, pad, cast=True):
    def step(j, carry):
        r0 = pl.multiple_of(j * RG, RG)
        val = src_ref[_natural_rows(r0, RG, d), :]
        dst[pl.ds(pad + r0, RG), :] = val.astype(dst.dtype) if cast else val
        return carry
    lax.fori_loop(0, T // RG, step, 0)


def _stack_heads(x, h0):
    zero = jnp.zeros_like(x)
    return jnp.concatenate([jnp.where(h0, x, zero), jnp.where(h0, zero, x)], axis=0)


def _attn_fwd(q, k, proj):
    nblk = T // BLK

    def body(q_ref, k_ref, v_ref, a_ref, lse_ref, qs, ks, vs, o0, o1, o2, l0, l1, l2, sb0, sb1):
        band, cur_half = _stacked_band_mask()
        h0 = lax.broadcasted_iota(jnp.int32, (1, 128), 1) < HD
        ks[0:BLK, :] = jnp.zeros((BLK, 128), BF16)
        vs[0:BLK, :] = jnp.zeros((BLK, 128), BF16)
        for d, o_s, l_s in zip(DILATIONS, (o0, o1, o2), (l0, l1, l2)):
            nb = T // d // BLK
            _regroup_into(qs, q_ref, d, 0)
            _regroup_into(ks, k_ref, d, BLK)
            _regroup_into(vs, v_ref, d, BLK)

            def scores(b):
                r0 = pl.multiple_of(b * BLK, BLK)
                return _dot(_stack_heads(qs[pl.ds(r0, BLK), :], h0), ks[pl.ds(r0, 2 * BLK), :], NT)

            def finish(b, s_raw, d=d, nb=nb, o_s=o_s, l_s=l_s):
                r0 = pl.multiple_of(b * BLK, BLK)
                mask = band & (cur_half | ((b & (nb - 1)) > 0))
                s = jnp.where(mask, s_raw, NEG)
                m = jnp.max(s, axis=1, keepdims=True)
                p = jnp.exp(s - m)
                l = jnp.sum(p, axis=1, keepdims=True)
                o = _dot(p.astype(BF16), vs[pl.ds(r0, 2 * BLK), :]) / l
                lse = m + jnp.log(l)
                rows = _natural_rows(r0, BLK, d)
                o_s[rows, :] = jnp.where(h0, o[0:BLK, :], o[BLK:, :])
                l_s[rows, :] = jnp.where(h0, lse[0:BLK, :], lse[BLK:, :])

            sb0[...] = scores(0)

            def step(i, carry):
                b = 2 * i
                sb1[...] = scores(b + 1)
                finish(b, sb0[...])
                sb0[...] = scores(jnp.minimum(b + 2, nblk - 1))
                finish(b + 1, sb1[...])
                return carry

            lax.fori_loop(0, nblk // 2, step, 0)

        def merge(i, carry):
            r = pl.ds(pl.multiple_of(i * RG, RG), RG)
            la, lb, lc = l0[r, :], l1[r, :], l2[r, :]
            m = jnp.maximum(jnp.maximum(la, lb), lc)
            ea, eb, ec = jnp.exp(la - m), jnp.exp(lb - m), jnp.exp(lc - m)
            z = (ea + eb) + ec
            a_ref[r, :] = ((ea * o0[r, :] + eb * o1[r, :]) + ec * o2[r, :]) / z
            lse_ref[r, :] = m + jnp.log(z)
            return carry

        lax.fori_loop(0, T // RG, merge, 0)

    spec = lambda cb: _bs((T, 128), lambda p, cb=cb: (0, cb + p))
    out = jax.ShapeDtypeStruct((T, AW), F32)
    return pl.pallas_call(
        body, name="attn_fwd", grid=(AW // 128,), in_specs=[spec(0), spec(0), spec(8)], out_specs=[spec(0)] * 2,
        out_shape=[out] * 2,
        scratch_shapes=[pltpu.VMEM((T, 128), BF16), pltpu.VMEM((T + BLK, 128), BF16), pltpu.VMEM((T + BLK, 128), BF16)]
        + [pltpu.VMEM((T, 128), F32)] * 6 + [pltpu.VMEM((2 * BLK, 2 * BLK), F32)] * 2,
        compiler_params=_cp("parallel"),
    )(q, k, proj)


def _attn_bwd(q, k, proj, do, lse, delta):
    nblk = T // BLK

    def body(q_ref, k_ref, v_ref, do_ref, l_ref, dl_ref, dq_ref, dk_ref, dv_ref, qs, dos, ks, vs, ls, dls, dks, dvs,
             sa0, sa1, da0, da1):
        band, cur_half = _stacked_band_mask()
        h0 = lax.broadcasted_iota(jnp.int32, (1, 128), 1) < HD
        ks[0:BLK, :] = jnp.zeros((BLK, 128), BF16)
        vs[0:BLK, :] = jnp.zeros((BLK, 128), BF16)
        for d in DILATIONS:
            nb = T // d // BLK
            _regroup_into(qs, q_ref, d, 0)
            _regroup_into(dos, do_ref, d, 0)
            _regroup_into(ks, k_ref, d, BLK)
            _regroup_into(vs, v_ref, d, BLK)
            _regroup_into(ls, l_ref, d, 0, cast=False)
            _regroup_into(dls, dl_ref, d, 0, cast=False)
            dks[...] = jnp.zeros_like(dks)
            dvs[...] = jnp.zeros_like(dvs)

            def scores(b, s_buf, dp_buf):
                r0 = pl.multiple_of(b * BLK, BLK)
                win = pl.ds(r0, 2 * BLK)
                s_buf[...] = _dot(_stack_heads(qs[pl.ds(r0, BLK), :], h0), ks[win, :], NT)
                dp_buf[...] = _dot(_stack_heads(dos[pl.ds(r0, BLK), :], h0), vs[win, :], NT)

            def finish(b, s_buf, dp_buf, d=d, nb=nb):
                r0 = pl.multiple_of(b * BLK, BLK)
                mask = band & (cur_half | ((b & (nb - 1)) > 0))
                win = pl.ds(r0, 2 * BLK)
                lv, dlv = ls[pl.ds(r0, BLK), :], dls[pl.ds(r0, BLK), :]
                lse2 = jnp.concatenate([lv[:, 0:1], lv[:, HD:HD + 1]], axis=0)
                dl2 = jnp.concatenate([dlv[:, 0:1], dlv[:, HD:HD + 1]], axis=0)
                p = jnp.exp(jnp.where(mask, s_buf[...], NEG) - lse2)
                ds = p * (dp_buf[...] - dl2)
                pb, dsb = p.astype(BF16), ds.astype(BF16)
                dq2 = _dot(dsb, ks[win, :])
                dks[win, :] += _dot(dsb, _stack_heads(qs[pl.ds(r0, BLK), :], h0), TN)
                dvs[win, :] += _dot(pb, _stack_heads(dos[pl.ds(r0, BLK), :], h0), TN)
                rows = _natural_rows(r0, BLK, d)
                dq = jnp.where(h0, dq2[0:BLK, :], dq2[BLK:, :])
                dq_ref[rows, :] = dq if d == 1 else dq_ref[rows, :] + dq

            scores(0, sa0, da0)

            def step(i, carry):
                b = 2 * i
                scores(b + 1, sa1, da1)
                finish(b, sa0, da0)
                scores(jnp.minimum(b + 2, nblk - 1), sa0, da0)
                finish(b + 1, sa1, da1)
                return carry

            lax.fori_loop(0, nblk // 2, step, 0)

            def back(j, carry, d=d):
                r0 = pl.multiple_of(j * RG, RG)
                rows = _natural_rows(r0, RG, d)
                src = pl.ds(BLK + r0, RG)
                dk_ref[rows, :] = dks[src, :] if d == 1 else dk_ref[rows, :] + dks[src, :]
                dv_ref[rows, :] = dvs[src, :] if d == 1 else dv_ref[rows, :] + dvs[src, :]
                return carry

            lax.fori_loop(0, T // RG, back, 0)

    spec = lambda cb: _bs((T, 128), lambda p, cb=cb: (0, cb + p))
    ospec = _bs((T, 128), lambda p: (0, p))
    out = jax.ShapeDtypeStruct((T, AW), F32)
    return pl.pallas_call(
        body, name="attn_bwd", grid=(AW // 128,), in_specs=[spec(0), spec(0), spec(8), spec(0), spec(0), spec(0)],
        out_specs=[ospec] * 3, out_shape=[out] * 3,
        scratch_shapes=[pltpu.VMEM((T, 128), BF16), pltpu.VMEM((T, 128), BF16), pltpu.VMEM((T + BLK, 128), BF16),
                        pltpu.VMEM((T + BLK, 128), BF16), pltpu.VMEM((T, 128), F32), pltpu.VMEM((T, 128), F32),
                        pltpu.VMEM((T + BLK, 128), F32), pltpu.VMEM((T + BLK, 128), F32)]
        + [pltpu.VMEM((2 * BLK, 2 * BLK), F32)] * 4,
        compiler_params=_cp("parallel"),
    )(q, k, proj, do, lse, delta)


def _attn_norm(attn, g_attn):
    tr = 1024

    def body(a_ref, g_ref, mix_ref):
        attn = a_ref[...]
        r = lax.rsqrt(jnp.mean(attn * attn, axis=-1, keepdims=True) + EPS)
        mix_ref[...] = ((attn * r) * g_ref[...]).astype(BF16)

    row = _bs((tr, AW), lambda i: (i, 0))
    return pl.pallas_call(
        body, name="attn_norm", grid=(T // tr,), in_specs=[row, _bs((1, AW), lambda i: (0, 0))],
        out_specs=row, out_shape=jax.ShapeDtypeStruct((T, D), BF16), compiler_params=_cp("parallel"),
    )(attn, g_attn)


def _attn_out_bwd(attn, dmix, g_attn):
    tr = 1024

    def body(a_ref, d_ref, g_ref, e_ref, do_ref, dl_ref, dg_ref):
        av, dyv = a_ref[...], d_ref[...]
        r = lax.rsqrt(jnp.mean(av * av, axis=-1, keepdims=True) + EPS)
        gdy = g_ref[...] * dyv
        da = r * gdy - av * ((r * r * r) * jnp.mean(av * gdy, axis=-1, keepdims=True))
        do_ref[...] = da
        dl_ref[...] = _segsum(da * av, e_ref[...])

        @pl.when(pl.program_id(0) == 0)
        def _():
            dg_ref[...] = jnp.zeros_like(dg_ref)

        dg_ref[...] += jnp.sum(dyv * (av * r), axis=0, keepdims=True)

    row = _bs((tr, AW), lambda i: (i, 0))
    vec = _bs((1, AW), lambda i: (0, 0))
    return pl.pallas_call(
        body, name="attn_out_bwd", grid=(T // tr,), in_specs=[row, row, vec, _bs((128, 128), lambda i: (0, 0))],
        out_specs=[row, row, vec],
        out_shape=[jax.ShapeDtypeStruct((T, AW), F32), jax.ShapeDtypeStruct((T, AW), F32),
                   jax.ShapeDtypeStruct((1, AW), F32)],
        compiler_params=_cp("arbitrary"),
    )(attn, dmix, g_attn, _head_ones())


TRR = 256


def _scan_fwd(a, u):
    n = a.shape[0]
    row = lax.broadcasted_iota(jnp.int32, (n, 1), 0)
    s = 1
    while s < n:
        keep = row >= s
        u = jnp.where(keep, a * pltpu.roll(u, s, 0) + u, u)
        a = jnp.where(keep, a * pltpu.roll(a, s, 0), a)
        s *= 2
    return a, u


def _scan_bwd(c, w):
    n = c.shape[0]
    row = lax.broadcasted_iota(jnp.int32, (n, 1), 0)
    s = 1
    while s < n:
        keep = row < n - s
        w = jnp.where(keep, c * pltpu.roll(w, n - s, 0) + w, w)
        c = jnp.where(keep, c * pltpu.roll(c, n - s, 0), c)
        s *= 2
    return w


def _gates(xc, wrg, wig, brg, big, sp):
    xcb = xc.astype(BF16)
    r = jax.nn.sigmoid(_dot(xcb, wrg) + brg)
    ig = jax.nn.sigmoid(_dot(xcb, wig) + big)
    la = (-LRU_C * r) * sp
    a = jnp.exp(la)
    mult = jnp.sqrt(-jnp.tanh(la) * (a * a + 1.0))
    return r, ig, a, mult


def _conv4(ext_ref, xr, cw_ref, cb_ref, n):
    y = cb_ref[...] + ext_ref[pl.ds(5, n), :] * cw_ref[0:1, :]
    y = y + ext_ref[pl.ds(6, n), :] * cw_ref[1:2, :]
    y = y + ext_ref[pl.ds(7, n), :] * cw_ref[2:3, :]
    return y + xr * cw_ref[3:4, :]


def _rec_fwd(proj, mix, cw, cb, wrg, wig, brg, big, lam, g_rec):
    n = TRR

    def body(xr_ref, gr_ref, cw_ref, cb_ref, wrg_ref, wig_ref, brg_ref, big_ref, lam_ref, g_ref, mix_in,
             mix_ref, h_ref, ext, hcar):
        del mix_in

        @pl.when(pl.program_id(0) == 0)
        def _():
            ext[0:8, :] = jnp.zeros((8, RW), F32)
            hcar[...] = jnp.zeros_like(hcar)

        xr = xr_ref[...]
        ext[8:, :] = xr
        xc = _conv4(ext, xr, cw_ref, cb_ref, n)
        ext[0:8, :] = xr[n - 8:, :]
        sp = _softplus(-lam_ref[...])
        _, ig, a, mult = _gates(xc, wrg_ref[...], wig_ref[...], brg_ref[...], big_ref[...], sp)
        a_s, u_s = _scan_fwd(a, mult * (ig * xc))
        h = u_s + a_s * hcar[7:8, :]
        h_ref[...] = h
        hcar[...] = h[n - 8:, :]
        pre = h * _gelu(gr_ref[...])
        r = lax.rsqrt(jnp.mean(pre * pre, axis=-1, keepdims=True) + EPS)
        mix_ref[...] = ((pre * r) * g_ref[...]).astype(BF16)

    vec = _bs((1, RW), lambda i: (0, 0))
    mat = _bs((RW, RW), lambda i: (0, 0))
    return pl.pallas_call(
        body, name="rec_fwd", grid=(T // n,),
        in_specs=[_bs((n, RW), lambda i: (i, 3)), _bs((n, RW), lambda i: (i, 4)), _bs((8, RW), lambda i: (0, 0)), vec,
                  mat, mat, vec, vec, vec, vec, pl.BlockSpec(memory_space=pl.ANY)],
        out_specs=[_bs((n, RW), lambda i: (i, 1)), _bs((n, RW), lambda i: (i, 0))],
        out_shape=[jax.ShapeDtypeStruct((T, D), BF16), jax.ShapeDtypeStruct((T, RW), F32)],
        scratch_shapes=[pltpu.VMEM((n + 8, RW), F32), pltpu.VMEM((8, RW), F32)],
        input_output_aliases={10: 0}, compiler_params=_cp("arbitrary"),
    )(proj, proj, cw, cb, wrg, wig, brg, big, lam, g_rec, mix)


def _rec_bwd(proj, h, dmix, cw, cb, wrg, wig, brg, big, lam, g_rec):
    n = TRR
    nt = T // n
    hb = n // 8

    def body(xr_ref, xh_ref, gr_ref, h_ref, hh_ref, dm_ref, cw_ref, cb_ref, wrg_ref, wig_ref, brg_ref, big_ref,
             lam_ref, g_ref, dp_ref, xc_ref, dr_ref, di_ref, dcw_ref, dcb_ref, dbr_ref, dbi_ref, dsp_ref,
             dg_ref, ext, exth, extd, adh):
        i, j = pl.program_id(0), pl.program_id(1)
        first_tile = i == nt - 1
        last_tile = i == 0

        @pl.when(j == 0)
        def _():
            @pl.when(last_tile)
            def _():
                for ref in (dcw_ref, dcb_ref, dbr_ref, dbi_ref, dsp_ref, dg_ref):
                    ref[...] = jnp.zeros_like(ref)
                extd[n:, :] = jnp.zeros((8, RW), F32)
                adh[...] = jnp.zeros_like(adh)

            row = lax.broadcasted_iota(jnp.int32, (n, 1), 0)
            xr = xr_ref[...]
            ext[0:8, :] = jnp.where(first_tile, 0.0, xh_ref[...])
            ext[8:, :] = xr
            xc = _conv4(ext, xr, cw_ref, cb_ref, n)
            sp = _softplus(-lam_ref[...])
            wrg, wig = wrg_ref[...], wig_ref[...]
            r, ig, a, mult = _gates(xc, wrg, wig, brg_ref[...], big_ref[...], sp)

            hv = h_ref[...]
            gl, dgl = _gelu_and_grad(gr_ref[...])
            pre = hv * gl
            dyv = dm_ref[...]
            rr = lax.rsqrt(jnp.mean(pre * pre, axis=-1, keepdims=True) + EPS)
            gdy = g_ref[...] * dyv
            dpre = rr * gdy - pre * ((rr * rr * rr) * jnp.mean(pre * gdy, axis=-1, keepdims=True))
            dg_ref[...] += jnp.sum(dyv * (pre * rr), axis=0, keepdims=True)
            dp_ref[:, RW:] = (dpre * hv * dgl).astype(BF16)

            is_last_row = row == n - 1
            w = dpre * gl + jnp.where(is_last_row, adh[0:1, :], 0.0)
            c = jnp.where(is_last_row, 0.0, pltpu.roll(a, n - 1, 0))
            dh = _scan_bwd(c, w)
            adh[...] = (a * dh)[0:8, :]

            exth[0:8, :] = jnp.where(first_tile, 0.0, hh_ref[...])
            exth[8:, :] = hv
            da = dh * exth[pl.ds(7, n), :]
            ixc = ig * xc
            dmult = dh * ixc
            dla = da * a - dmult * ((a * a) / mult)
            dsp_ref[...] += jnp.sum(dla * (-LRU_C * r), axis=0, keepdims=True)
            dpr = (dla * (-LRU_C * sp)) * (r * (1.0 - r))
            dpi = (dh * (mult * xc)) * (ig * (1.0 - ig))
            dprb, dpib = dpr.astype(BF16), dpi.astype(BF16)
            dxc = dh * (mult * ig) + _dot(dprb, wrg, NT) + _dot(dpib, wig, NT)
            dbr_ref[...] += jnp.sum(dpr, axis=0, keepdims=True)
            dbi_ref[...] += jnp.sum(dpi, axis=0, keepdims=True)
            xc_ref[...] = xc.astype(BF16)
            dr_ref[...] = dprb
            di_ref[...] = dpib

            extd[0:n, :] = dxc
            dxr = dxc * cw_ref[3:4, :] + extd[pl.ds(1, n), :] * cw_ref[2:3, :]
            dxr = dxr + extd[pl.ds(2, n), :] * cw_ref[1:2, :] + extd[pl.ds(3, n), :] * cw_ref[0:1, :]
            extd[n:, :] = dxc[0:8, :]
            dcb_ref[...] += jnp.sum(dxc, axis=0, keepdims=True)
            for kk in range(4):
                dcw_ref[kk:kk + 1, :] += jnp.sum(dxc * ext[pl.ds(5 + kk, n), :], axis=0, keepdims=True)

            @pl.when(first_tile)
            def _():
                dsp_ref[...] = dsp_ref[...] * (-jax.nn.sigmoid(-lam_ref[...]))

            dp_ref[:, 0:RW] = dxr.astype(BF16)

    vec = _bs((1, RW), lambda i, j: (0, 0))
    mat = _bs((RW, RW), lambda i, j: (0, 0))
    tile = lambda cblk: _bs((n, RW), lambda i, j, cblk=cblk: (nt - 1 - i, cblk))
    halo = lambda cblk: _bs((8, RW), lambda i, j, cblk=cblk: (jnp.maximum((nt - 1 - i) * hb - 1, 0), cblk))
    bt = jax.ShapeDtypeStruct((T, RW), BF16)
    v = jax.ShapeDtypeStruct((1, RW), F32)
    return pl.pallas_call(
        body, name="rec_bwd", grid=(nt, 1),
        in_specs=[tile(3), halo(3), tile(4), tile(0), halo(0), tile(1), _bs((8, RW), lambda i, j: (0, 0)), vec,
                  mat, mat, vec, vec, vec, vec],
        out_specs=[_bs((n, 2 * RW), lambda i, j: (nt - 1 - i, 0)), tile(0), tile(0), tile(0),
                   _bs((8, RW), lambda i, j: (0, 0)), vec, vec, vec, vec, vec],
        out_shape=[jax.ShapeDtypeStruct((T, 2 * RW), BF16), bt, bt, bt, jax.ShapeDtypeStruct((8, RW), F32),
                   v, v, v, v, v],
        scratch_shapes=[pltpu.VMEM((n + 8, RW), F32), pltpu.VMEM((n + 8, RW), F32), pltpu.VMEM((n + 8, RW), F32),
                        pltpu.VMEM((8, RW), F32)],
        compiler_params=_cp("arbitrary", "arbitrary"),
    )(proj, proj, proj, h, h, dmix, cw, cb, wrg, wig, brg, big, lam, g_rec)


FC = 1536
TRF = 512


LC = 128


class _RowsBack:
    def __init__(self, before):
        row = lax.broadcasted_iota(jnp.int32, before.shape, 0)
        self.top1, self.top2 = row < 1, row < 2
        self.r1, self.r2 = pltpu.roll(before, 1, 0), pltpu.roll(before, 2, 0)

    def step(self, cur):
        r1, r2 = pltpu.roll(cur, 1, 0), pltpu.roll(cur, 2, 0)
        out = jnp.where(self.top1, self.r1, r1), jnp.where(self.top2, self.r2, r2)
        self.r1, self.r2 = r1, r2
        return out


def _up_act(h2, w_up, cw, cb):
    n = TRF
    nt = T // n
    pw = 256
    npc = FC // pw

    def body(h_ref, wg_ref, wu_ref, cwg_ref, cwu_ref, bg_ref, bu_ref, up_ref, a_ref, fa_ref, fb_ref, hx, gb0, gb1,
             ub0, ub1):
        i = pl.program_id(1)
        halo = h_ref[pl.ds(pl.multiple_of(jnp.maximum(i * n - 16, 0), 16), 16), :]
        hx[0:16, :] = jnp.where(i == 0, jnp.zeros_like(halo), halo)
        hx[16:, :] = h_ref[pl.ds(pl.multiple_of(i * n, n), n), :]
        gbufs, ubufs = (gb0, gb1), (ub0, ub1)

        def dots(c):
            hv = hx[...]
            gbufs[c % 2][...] = _dot(hv, wg_ref[:, c * pw:(c + 1) * pw])
            ubufs[c % 2][...] = _dot(hv, wu_ref[:, c * pw:(c + 1) * pw])

        def chain(c):
            gb, ub = gbufs[c % 2], ubufs[c % 2]
            up_ref[:, c * pw:(c + 1) * pw] = gb[16:, :]
            up_ref[:, FC + c * pw:FC + (c + 1) * pw] = ub[16:, :]
            rows8 = lambda v: jnp.broadcast_to(v, (8, LC))
            for sub in range(pw // LC):
                lc = slice(sub * LC, (sub + 1) * LC)
                cols = slice(c * pw + sub * LC, c * pw + (sub + 1) * LC)
                wg = [rows8(cwg_ref[kk:kk + 1, cols]) for kk in range(3)]
                wu = [rows8(cwu_ref[kk:kk + 1, cols]) for kk in range(3)]
                bg, bu = rows8(bg_ref[:, cols]), rows8(bu_ref[:, cols])
                g_back = _RowsBack(gb[pl.ds(8, 8), lc])
                u_back = _RowsBack(ub[pl.ds(8, 8), lc])
                for r in range(0, n, 16):
                    res, fa, fb = [], [], []
                    for rr in (16 + r, 24 + r):
                        g0, u0 = gb[pl.ds(rr, 8), lc], ub[pl.ds(rr, 8), lc]
                        g1, g2 = g_back.step(g0)
                        u1, u2 = u_back.step(u0)
                        ug = ((bg + g2 * wg[0]) + g1 * wg[1]) + g0 * wg[2]
                        uu = ((bu + u2 * wu[0]) + u1 * wu[1]) + u0 * wu[2]
                        gl, dgl = _gelu_and_grad(ug)
                        res.append(gl * uu)
                        fa.append(uu * dgl)
                        fb.append(gl)
                    a_ref[pl.ds(r, 16), cols] = jnp.concatenate(res, axis=0).astype(BF16)
                    fa_ref[pl.ds(r, 16), cols] = jnp.concatenate(fa, axis=0).astype(BF16)
                    fb_ref[pl.ds(r, 16), cols] = jnp.concatenate(fb, axis=0).astype(BF16)

        dots(0)
        for c in range(npc):
            if c + 1 < npc:
                dots(c + 1)
            chain(c)

    wsl = lambda o: _bs((None, D, FC), lambda j, i, o=o: (2 * j + o, 0, 0))
    wsp = lambda o: _bs((None, 8, FC), lambda j, i, o=o: (2 * j + o, 0, 0))
    bsp = lambda o: _bs((1, FC), lambda j, i, o=o: (0, 2 * j + o))
    return pl.pallas_call(
        body, name="up_act", grid=(2, nt),
        in_specs=[pl.BlockSpec((T, D), lambda j, i: (0, 0), pipeline_mode=pl.Buffered(1)), wsl(0), wsl(1),
                  wsp(0), wsp(1), bsp(0), bsp(1)],
        out_specs=[_bs((n, 2 * FC), lambda j, i: (i, j))] + [_bs((n, FC), lambda j, i: (i, j))] * 3,
        out_shape=[jax.ShapeDtypeStruct((T, 2 * DFF), F32)] + [jax.ShapeDtypeStruct((T, DFF), BF16)] * 3,
        scratch_shapes=[pltpu.VMEM((n + 16, D), BF16)] + [pltpu.VMEM((n + 16, pw), F32)] * 4,
        compiler_params=_cp("parallel", "arbitrary"),
    )(h2, w_up, w_up, cw, cw, cb, cb)


def _ffn_bwd(up_pre, fa, fb, dyb, w_down_t, cw, after=()):
    n = TRF
    hb = n // 8
    nt = T // n
    m = n + 8
    pw = 256
    npc = FC // pw

    def body(g_ref, gp_ref, u_ref, up_ref, fa_ref, fan_ref, fb_ref, fbn_ref, dy_ref, wd_ref, wg_ref, wu_ref, *rest):
        o_ref, dw_ref, db_ref, eg0, eu0, dug_s, duu_s, dyx, db0, db1 = rest[len(after):]
        i = pl.program_id(1)
        first, last = i == 0, i == nt - 1

        @pl.when(first)
        def _():
            dw_ref[...] = jnp.zeros_like(dw_ref)
            db_ref[...] = jnp.zeros_like(db_ref)

        tail = dy_ref[pl.ds(pl.multiple_of(jnp.minimum((i + 1) * n, T - 16), 16), 16), :]
        dyx[0:n, :] = dy_ref[pl.ds(pl.multiple_of(i * n, n), n), :]
        dyx[n:, :] = jnp.where(last, jnp.zeros_like(tail), tail)
        dbufs = (db0, db1)

        def dots(c):
            dbufs[c % 2][...] = _dot(dyx[...], wd_ref[:, c * pw:(c + 1) * pw])

        eg0[0:8, :] = jnp.where(first, 0.0, gp_ref[...])
        eg0[8:, :] = g_ref[0:8, :]
        eu0[0:8, :] = jnp.where(first, 0.0, up_ref[...])
        eu0[8:, :] = u_ref[0:8, :]

        def column(ci, dbuf, lc):
            cols = slice(ci * LC, (ci + 1) * LC)
            ucols = slice(FC + ci * LC, FC + (ci + 1) * LC)
            rows8 = lambda v: jnp.broadcast_to(v, (8, LC))
            wg = [rows8(wg_ref[kk:kk + 1, cols]) for kk in range(3)]
            wu = [rows8(wu_ref[kk:kk + 1, cols]) for kk in range(3)]
            zero = jnp.zeros((8, LC), F32)
            acc = [zero] * 8
            g_back, u_back = _RowsBack(eg0[pl.ds(0, 8), cols]), _RowsBack(eu0[pl.ds(0, 8), cols])
            for r in range(0, n + 16, 16):
                src_a, src_b, r16 = (fan_ref, fbn_ref, 0) if r == n else (fa_ref, fb_ref, r)
                fa16 = src_a[pl.ds(r16, 16), cols].astype(F32)
                fb16 = src_b[pl.ds(r16, 16), cols].astype(F32)
                for half in range(1 if r == n else 2):
                    rr = r + 8 * half
                    dv = dbuf[pl.ds(rr, 8), lc]
                    dug = dv * fa16[8 * half:8 * half + 8, :]
                    duu = dv * fb16[8 * half:8 * half + 8, :]
                    dug_s[pl.ds(rr, 8), :] = dug
                    duu_s[pl.ds(rr, 8), :] = duu
                    if rr < n:
                        g0, u0 = g_ref[pl.ds(rr, 8), cols], u_ref[pl.ds(rr, 8), cols]
                        g1, g2 = g_back.step(g0)
                        u1, u2 = u_back.step(u0)
                        gt, ut = (g2, g1, g0), (u2, u1, u0)
                        acc = [acc[0] + dug * gt[0], acc[1] + dug * gt[1], acc[2] + dug * gt[2],
                               acc[3] + duu * ut[0], acc[4] + duu * ut[1], acc[5] + duu * ut[2],
                               acc[6] + dug, acc[7] + duu]
            for r in range(0, n, 16):
                og, ou = [], []
                for rr in (r, r + 8):
                    og.append((dug_s[pl.ds(rr, 8), :] * wg[2] + dug_s[pl.ds(rr + 1, 8), :] * wg[1])
                              + dug_s[pl.ds(rr + 2, 8), :] * wg[0])
                    ou.append((duu_s[pl.ds(rr, 8), :] * wu[2] + duu_s[pl.ds(rr + 1, 8), :] * wu[1])
                              + duu_s[pl.ds(rr + 2, 8), :] * wu[0])
                o_ref[pl.ds(r, 16), cols] = jnp.concatenate(og, axis=0).astype(BF16)
                o_ref[pl.ds(r, 16), ucols] = jnp.concatenate(ou, axis=0).astype(BF16)
            for kk in range(3):
                dw_ref[kk:kk + 1, cols] += jnp.sum(acc[kk], axis=0, keepdims=True)
                dw_ref[kk:kk + 1, ucols] += jnp.sum(acc[3 + kk], axis=0, keepdims=True)
            db_ref[:, cols] += jnp.sum(acc[6], axis=0, keepdims=True)
            db_ref[:, ucols] += jnp.sum(acc[7], axis=0, keepdims=True)

        dots(0)
        for c in range(npc):
            if c + 1 < npc:
                dots(c + 1)
            for sub in range(pw // LC):
                column(c * (pw // LC) + sub, dbufs[c % 2], slice(sub * LC, (sub + 1) * LC))

    main = lambda o: _bs((n, FC), lambda j, i, o=o: (i, 2 * j + o))
    prev = lambda o: _bs((8, FC), lambda j, i, o=o: (jnp.maximum(i * hb - 1, 0), 2 * j + o))
    saved = _bs((n, FC), lambda j, i: (i, j))
    saved_next = _bs((16, FC), lambda j, i: (jnp.minimum((i + 1) * (n // 16), T // 16 - 1), j))
    wsp = lambda o: _bs((None, 8, FC), lambda j, i, o=o: (2 * j + o, 0, 0))
    return pl.pallas_call(
        body, name="ffn_bwd", grid=(2, nt),
        in_specs=[main(0), prev(0), main(1), prev(1), saved, saved_next, saved, saved_next,
                  pl.BlockSpec((T, D), lambda j, i: (0, 0), pipeline_mode=pl.Buffered(1)),
                  _bs((D, FC), lambda j, i: (0, j)), wsp(0), wsp(1)]
        + [pl.BlockSpec(memory_space=pl.ANY)] * len(after),
        out_specs=[_bs((n, 2 * FC), lambda j, i: (i, j)), _bs((8, 2 * FC), lambda j, i: (0, j)),
                   _bs((1, 2 * FC), lambda j, i: (0, j))],
        out_shape=[jax.ShapeDtypeStruct((T, 2 * DFF), BF16), jax.ShapeDtypeStruct((8, 2 * DFF), F32),
                   jax.ShapeDtypeStruct((1, 2 * DFF), F32)],
        scratch_shapes=[pltpu.VMEM((16, FC), F32)] * 2 + [pltpu.VMEM((m, LC), F32)] * 2
        + [pltpu.VMEM((n + 16, D), BF16)] + [pltpu.VMEM((n + 16, pw), F32)] * 2,
        compiler_params=_cp("parallel", "arbitrary"),
    )(up_pre, up_pre, up_pre, up_pre, fa, fa, fb, fb, dyb, w_down_t, cw, cw, *after)


def _down_loss(act, w_down, x1, target):
    tm, tn = 512, D

    def body(a_ref, b_ref, r_ref, t_ref, dy_ref, dyb_ref, l_ref):
        @pl.when((pl.program_id(0) == 0) & (pl.program_id(1) == 0))
        def _():
            l_ref[...] = jnp.zeros_like(l_ref)

        err = (r_ref[...] + _dot(a_ref[...], b_ref[...])) - t_ref[...]
        dy = err * (1.0 / D)
        dy_ref[...] = dy
        dyb_ref[...] = dy.astype(BF16)
        l_ref[...] += jnp.sum(0.5 * (err * err) * (1.0 / D))

    o_spec = _bs((tm, tn), lambda j, i: (i, j))
    return pl.pallas_call(
        body, name="down_loss", grid=(D // tn, T // tm),
        in_specs=[_bs((tm, DFF), lambda j, i: (i, 0)),
                  pl.BlockSpec((DFF, tn), lambda j, i: (0, j), pipeline_mode=pl.Buffered(1)), o_spec, o_spec],
        out_specs=[o_spec, o_spec, _bs((8, 128), lambda j, i: (0, 0))],
        out_shape=[jax.ShapeDtypeStruct((T, D), F32), jax.ShapeDtypeStruct((T, D), BF16),
                   jax.ShapeDtypeStruct((8, 128), F32)],
        compiler_params=_cp("arbitrary", "arbitrary"),
    )(act, w_down, x1, target)


def _block_diag(w):
    eye = jnp.eye(8, dtype=w.dtype)
    return (w[:, :, None, :] * eye[:, None, :, None]).reshape(RW, RW).astype(BF16)


def _diag_blocks(m):
    eye = jnp.eye(8, dtype=m.dtype)
    return (m.reshape(8, HD, 8, HD) * eye[:, None, :, None]).sum(axis=2)


def _local_step(x, pos_col, target, p, exch):
    qg, kg = jnp.tile(p["q_norm_g"], (1, 8)), jnp.tile(p["k_norm_g"], (1, 8))
    wrg, wig = _block_diag(p["w_rg"]), _block_diag(p["w_ig"])
    brg, big = p["b_rg"].reshape(1, RW), p["b_ig"].reshape(1, RW)

    h1 = _rms_fwd("rms1", x, p["g_mix"])
    p = {**p, **exch.wait_first(h1)}
    proj = _mm("mm_in", h1, p["w_in"], "nn", 1024, 640, stack=NCHIP, after=exch.start_rest(), a_full=True)
    q, k, cos_t, sin_t = _qk_prep(proj, pos_col, qg, kg)
    attn, lse = _attn_fwd(q, k, proj)
    mix = _attn_norm(attn, p["g_attn_out"])
    mix, hseq = _rec_fwd(proj, mix, p["rec_conv_w"], p["rec_conv_b"], wrg, wig, brg, big, p["lru_lambda"], p["g_rec_out"])
    rest = exch.wait_rest(mix)
    x1, h2 = _out_rms(mix, rest["w_out"], x, p["g_ffn"])
    up_pre, act, fa, fb = _up_act(h2, rest["w_up"], p["ffn_conv_w"], p["ffn_conv_b"])
    dy, dyb, loss_blk = _down_loss(act, rest["w_down"], x1, target)

    g = {}
    tok = exch.reduce_start("w_down", *_mm("wg_down", act, dyb, "tn", 512, 512, twin_bf16=True))
    dup, g["ffn_conv_w"], g["ffn_conv_b"] = _ffn_bwd(up_pre, fa, fb, dyb, rest["w_down"].T, p["ffn_conv_w"], tok)
    tok = exch.reduce_start("w_up", *_mm("wg_up", h2, dup, "tn", 512, 768, stack=NCHIP, twin_bf16=True, a_full=True))
    dx1, dx1b, g["g_ffn"] = _dgrad_rms_bwd("dg_up", dup, rest["w_up"], x1, p["g_ffn"], dy, True, after=tok)
    tok = exch.reduce_start("w_out", *_mm("wg_out", mix, dx1b, "tn", 512, 512, twin_bf16=True, a_full=True))
    dmix = _mm("dg_out", dx1b, rest["w_out"], "nt", 1024, 512, after=tok, a_full=True)
    do, delta, g["g_attn_out"] = _attn_out_bwd(attn, dmix, p["g_attn_out"])
    dq, dk, dv = _attn_bwd(q, k, proj, do, lse, delta)
    dqkv, dqg, dkg = _qk_bwd(proj, cos_t, sin_t, qg, kg, dq, dk, dv)
    (drec, xcb, dprb, dpib, g["rec_conv_w"], g["rec_conv_b"], dbr, dbi, dsp, g["g_rec_out"]) = _rec_bwd(
        proj, hseq, dmix, p["rec_conv_w"], p["rec_conv_b"], wrg, wig, brg, big, p["lru_lambda"], p["g_rec_out"])
    dproj = jnp.concatenate([dqkv, drec], axis=1)
    g["w_rg"] = _diag_blocks(_mm("wg_rg", xcb, dprb, "tn", 512, 512)).reshape(RW, HD)
    g["w_ig"] = _diag_blocks(_mm("wg_ig", xcb, dpib, "tn", 512, 512)).reshape(RW, HD)
    g["b_rg"], g["b_ig"] = dbr.reshape(8, HD), dbi.reshape(8, HD)
    g["lru_lambda"] = dsp
    g["q_norm_g"] = dqg.reshape(8, HD).sum(axis=0, keepdims=True)
    g["k_norm_g"] = dkg.reshape(8, HD).sum(axis=0, keepdims=True)
    tok = exch.reduce_start("w_in", *_mm("wg_in", h1, dproj, "tn", 512, 640, stack=NCHIP, twin_bf16=True, a_full=True))
    grad_x, g["g_mix"] = _dgrad_rms_bwd("dg_in", dproj, p["w_in"], x, p["g_mix"], dx1, False, after=tok)
    return loss_blk, grad_x, g


ANY = pl.BlockSpec(memory_space=pl.ANY)


def _mesh_pos():
    return lax.axis_index("x"), lax.axis_index("y"), lax.axis_index("c")


def _slot(px, py, perm):
    return 2 * py + px if perm else 2 * px + py


def _other_chips(x, y):
    return [(1 - x, y), (x, 1 - y), (1 - x, 1 - y)]


def _rcopy(src, dst, send, recv, k, to, kr=None):
    return pltpu.make_async_remote_copy(src_ref=src, dst_ref=dst, send_sem=send.at[k],
                                        recv_sem=recv.at[k if kr is None else kr], device_id=to, device_id_type=MESH)


def _cast_bf16(name, w, after=()):
    r, c = w.shape
    tr = 256

    def body(w_ref, *rest):
        rest[-1][...] = w_ref[...].astype(BF16)

    return pl.pallas_call(
        body, name=name, grid=(r // tr,), in_specs=[_bs((tr, c), lambda i: (i, 0))] + [ANY] * len(after),
        out_specs=_bs((tr, c), lambda i: (i, 0)), out_shape=jax.ShapeDtypeStruct((r, c), BF16),
        compiler_params=_cp("parallel"),
    )(w, *after)


def _sibling_fill(lands, perms):
    na = len(lands)

    def body(*refs):
        outs, (send, recv) = refs[na:2 * na], refs[2 * na:]
        x, y, c = _mesh_pos()
        cps = []
        for a in range(na):
            for j, (px, py) in enumerate(_other_chips(x, y)):
                mine = outs[a].at[_slot(px, py, perms[a]), c]
                cps.append(_rcopy(mine, mine, send, recv, 3 * a + j, (x, y, 1 - c)))
        for cp in cps:
            cp.start()
        for a in range(na):
            for j, (px, py) in enumerate(_other_chips(x, y)):
                got = outs[a].at[_slot(px, py, perms[a]), 1 - c]
                _rcopy(got, got, send, recv, 3 * a + j, (x, y, c)).wait_recv()
        for cp in cps:
            cp.wait_send()

    return pl.pallas_call(
        body, name="gather_fill", in_specs=[ANY] * na, out_specs=[ANY] * na,
        out_shape=[jax.ShapeDtypeStruct(a.shape, a.dtype) for a in lands],
        input_output_aliases={i: i for i in range(na)},
        scratch_shapes=[pltpu.SemaphoreType.DMA((3 * na,)), pltpu.SemaphoreType.DMA((3 * na,))],
    )(*lands)


HBM = pl.BlockSpec(memory_space=pltpu.HBM)
SEM = pl.BlockSpec(memory_space=pltpu.SEMAPHORE)
EFFECT = pltpu.SideEffectType.DATAFLOW_SIDE_EFFECTING


def _split_start(name, srcs, lands, plan, nsem):
    ns, nl = len(srcs), len(lands)

    def body(*refs):
        send, recv = refs[ns + nl], refs[ns + nl + 1]
        sends, _ = plan(refs[:ns], refs[ns:ns + nl], send, recv)
        for cp in sends:
            cp.start()
        refs[-1][...] = jnp.zeros((8, 128), F32)

    arrs = list(srcs) + list(lands)
    out = pl.pallas_call(
        body, name=name, in_specs=[HBM] * (ns + nl),
        out_specs=[SEM, SEM] + [HBM] * (ns + nl) + [pl.BlockSpec(memory_space=pltpu.VMEM)],
        out_shape=[pltpu.SemaphoreType.DMA((nsem,)), pltpu.SemaphoreType.DMA((nsem,))]
        + [pltpu.HBM(a.shape, a.dtype) for a in arrs] + [jax.ShapeDtypeStruct((8, 128), F32)],
        input_output_aliases={i: 2 + i for i in range(ns + nl)},
        compiler_params=pltpu.CompilerParams(has_side_effects=EFFECT),
    )(*[pltpu.with_memory_space_constraint(a, pltpu.HBM) for a in arrs])
    return out[0], out[1], out[2:2 + ns], out[2 + ns:2 + ns + nl], out[-1]


def _split_wait(name, send, recv, srcs, lands, plan, after):
    ns, nl = len(srcs), len(lands)

    def body(*refs):
        sends, recvs = plan(refs[:ns], refs[ns:ns + nl], refs[ns + nl], refs[ns + nl + 1])
        for cp in sends:
            cp.wait_send()
        for cp in recvs:
            cp.wait_recv()

    arrs = list(srcs) + list(lands)
    after = tuple(after) if isinstance(after, (tuple, list)) else (after,)
    out = pl.pallas_call(
        body, name=name, in_specs=[HBM] * (ns + nl) + [SEM, SEM] + [ANY] * len(after), out_specs=[HBM] * (ns + nl),
        out_shape=[pltpu.HBM(a.shape, a.dtype) for a in arrs],
        input_output_aliases={i: i for i in range(ns + nl)},
        compiler_params=pltpu.CompilerParams(has_side_effects=EFFECT),
    )(*arrs, send, recv, *after)
    return out[ns:]


def _gather_plan(perms):
    def plan(srcs, lands, send, recv):
        x, y, c = _mesh_pos()
        sends, recvs = [], []
        for a, perm in enumerate(perms):
            for j, (px, py) in enumerate(_other_chips(x, y)):
                for cc in (0, 1):
                    k = 6 * a + 2 * j + cc
                    sends.append(_rcopy(srcs[a].at[c], lands[a].at[_slot(x, y, perm), c], send, recv, k, (px, py, cc),
                                        kr=6 * a + 2 * j + c))
                    got = lands[a].at[_slot(px, py, perm), cc]
                    recvs.append(_rcopy(got, got, send, recv, k, (x, y, c)))
        return sends, recvs
    return plan


def _gather_half_plan(perms, halved):
    def plan(srcs, lands, send, recv):
        x, y, c = _mesh_pos()
        sends, recvs = [], []
        for a, perm in enumerate(perms):
            for j, (px, py) in enumerate(_other_chips(x, y)):
                k = 3 * a + j
                mine, theirs = _slot(x, y, perm), _slot(px, py, perm)
                if halved[a]:
                    sends.append(_rcopy(srcs[a].at[c], lands[a].at[mine, c], send, recv, k, (px, py, c)))
                    got = lands[a].at[theirs, c]
                else:
                    sends.append(_rcopy(srcs[a], lands[a].at[mine], send, recv, k, (px, py, c)))
                    got = lands[a].at[theirs]
                recvs.append(_rcopy(got, got, send, recv, k, (x, y, c)))
        return sends, recvs
    return plan


def _reduce_plan(perm):
    def plan(srcs, lands, send, recv):
        x, y, c = _mesh_pos()
        src, land = srcs[0], lands[0]
        sends = []
        for j, (px, py) in enumerate(_other_chips(x, y)):
            for hf in (0, 1):
                sends.append(_rcopy(src.at[_slot(px, py, perm), hf], land.at[2 * j + c], send, recv, 2 * j + hf,
                                    (px, py, hf), kr=2 * j + c))
        sends.append(_rcopy(src.at[_slot(x, y, perm), 1 - c], land.at[6], send, recv, 6, (x, y, 1 - c)))
        recvs = [_rcopy(land.at[i], land.at[i], send, recv, i, (x, y, c)) for i in range(7)]
        return sends, recvs
    return plan


def _sibling_share(rs):
    na = len(rs)

    def body(*refs):
        ins, outs, (send, recv) = refs[:na], refs[na:2 * na], refs[2 * na:]
        x, y, c = _mesh_pos()
        cps = [_rcopy(ins[a], outs[a], send, recv, a, (x, y, 1 - c)) for a in range(na)]
        for cp in cps:
            cp.start()
        for cp in cps:
            cp.wait()

    return pl.pallas_call(
        body, name="rs_share", in_specs=[ANY] * na, out_specs=[ANY] * na,
        out_shape=[jax.ShapeDtypeStruct(r.shape, F32) for r in rs],
        scratch_shapes=[pltpu.SemaphoreType.DMA((na,)), pltpu.SemaphoreType.DMA((na,))],
    )(*rs)


def _add_pieces(name, g, got, where):
    _, _, r2, cc = g.shape
    tr = 256 if r2 % 256 == 0 else 128

    def body(w_ref, g_ref, r_ref, o_ref):
        del w_ref
        acc = g_ref[...]
        for i in range(7):
            acc = acc + r_ref[i].astype(F32)
        o_ref[...] = acc

    return pl.pallas_call(
        body, name=name,
        grid_spec=pltpu.PrefetchScalarGridSpec(
            num_scalar_prefetch=1, grid=(r2 // tr,),
            in_specs=[_bs((None, None, tr, cc), lambda i, w_ref: (w_ref[0], w_ref[1], i, 0)),
                      _bs((7, tr, cc), lambda i, w_ref: (0, i, 0))],
            out_specs=_bs((tr, cc), lambda i, w_ref: (i, 0))),
        out_shape=jax.ShapeDtypeStruct((r2, cc), F32), compiler_params=_cp("parallel"),
    )(where, g, got)


def _adam_math(w, g, m, v):
    m = ADAM_B1 * m + (1.0 - ADAM_B1) * g
    v = ADAM_B2 * v + (1.0 - ADAM_B2) * (g * g)
    m_hat = m / (1.0 - ADAM_B1 ** ADAM_STEP)
    v_hat = v / (1.0 - ADAM_B2 ** ADAM_STEP)
    return -ADAM_LR * (m_hat / (jnp.sqrt(v_hat) + ADAM_EPS) + ADAM_WD * w), m, v


def _adam_big(name, w, g_mine, g_sib, m, v, c_arr):
    r, cols = w.shape
    tr = 256 if (r // 2) % 256 == 0 else 128
    per = r // 2 // tr

    def body(c_ref, w_ref, a_ref, b_ref, m_ref, v_ref, g_ref, d_ref, m2_ref, v2_ref):
        g = jnp.where(pl.program_id(0) == c_ref[0], a_ref[...], b_ref[...])
        g_ref[...] = g
        d_ref[...], m2_ref[...], v2_ref[...] = _adam_math(w_ref[...], g, m_ref[...], v_ref[...])

    spec = _bs((tr, cols), lambda h, i, c_ref: (h * per + i, 0))
    half = _bs((tr, cols), lambda h, i, c_ref: (i, 0))
    out = jax.ShapeDtypeStruct((r, cols), F32)
    return pl.pallas_call(
        body, name=name,
        grid_spec=pltpu.PrefetchScalarGridSpec(
            num_scalar_prefetch=1, grid=(2, per), in_specs=[spec, half, half, spec, spec], out_specs=[spec] * 4),
        out_shape=[out] * 4, compiler_params=_cp("parallel", "parallel"),
    )(c_arr, w, g_mine, g_sib, m, v)


_CLASS_SHAPE = {"a": (8, D), "b": (8, RW), "c": (8, 2 * DFF), "d": (1048, HD)}
_SMALL = (
    ("g_mix", "a", 0, 1, D), ("g_ffn", "a", 1, 1, D),
    ("rec_conv_w", "b", 0, 4, RW), ("rec_conv_b", "b", 4, 1, RW), ("lru_lambda", "b", 5, 1, RW),
    ("g_attn_out", "b", 6, 1, RW), ("g_rec_out", "b", 7, 1, RW),
    ("ffn_conv_w", "c", 0, 3, 2 * DFF), ("ffn_conv_b", "c", 3, 1, 2 * DFF),
    ("w_rg", "d", 0, RW, HD), ("w_ig", "d", RW, RW, HD), ("b_rg", "d", 2 * RW, 8, HD), ("b_ig", "d", 2 * RW + 8, 8, HD),
    ("q_norm_g", "d", 2 * RW + 16, 1, HD), ("k_norm_g", "d", 2 * RW + 17, 1, HD),
)
_LOSS_ROW = 2
_CLASSES = ("a", "b", "c", "d")
_CLASS_OWNER = {"a": 0, "b": 0, "c": 0, "d": 1}


def _small_allreduce(g, loss_blk):
    names = [s[0] for s in _SMALL]
    nin = len(names) + 1

    def body(*refs):
        ins = dict(zip(names, refs[:len(names)]))
        loss_ref = refs[len(names)]
        outs = dict(zip(_CLASSES, refs[nin:nin + 4]))
        pair = dict(zip(_CLASSES, refs[nin + 4:nin + 8]))
        quad = dict(zip(_CLASSES, refs[nin + 8:nin + 12]))
        send, recv = refs[nin + 12:]
        x, y, c = _mesh_pos()
        chip = 2 * x + y
        pair["a"][c] = jnp.zeros(_CLASS_SHAPE["a"], F32)
        pair["b"][c] = ins["rec_conv_w"][...]
        pair["c"][c] = ins["ffn_conv_w"][...]
        pair["d"][c, 2 * RW + 16:, :] = jnp.zeros((8, HD), F32)
        for name, k, r0, nr, _ in _SMALL:
            if name in ("rec_conv_w", "ffn_conv_w"):
                continue
            pair[k][c, r0:r0 + nr, :] = ins[name][...]
        pair["a"][c, _LOSS_ROW:_LOSS_ROW + 1, :] = jnp.broadcast_to(loss_ref[0:1, 0:1], (1, D))
        cps = [_rcopy(pair[k].at[c], pair[k].at[c], send, recv, ki, (x, y, 1 - c)) for ki, k in enumerate(_CLASSES)]
        for cp in cps:
            cp.start()
        for ki, k in enumerate(_CLASSES):
            _rcopy(pair[k].at[1 - c], pair[k].at[1 - c], send, recv, ki, (x, y, c)).wait_recv()
            quad[k][chip] = pair[k][0] + pair[k][1]
        for cp in cps:
            cp.wait_send()
        for ki, k in enumerate(_CLASSES):
            owner = _CLASS_OWNER[k]

            @pl.when(c == owner)
            def _(ki=ki, k=k):
                cps2 = [_rcopy(quad[k].at[chip], quad[k].at[chip], send, recv, 4 + 3 * ki + j, (px, py, c))
                        for j, (px, py) in enumerate(_other_chips(x, y))]
                for cp in cps2:
                    cp.start()
                for j, (px, py) in enumerate(_other_chips(x, y)):
                    got = quad[k].at[2 * px + py]
                    _rcopy(got, got, send, recv, 4 + 3 * ki + j, (x, y, c)).wait_recv()
                outs[k][...] = ((quad[k][0] + quad[k][1]) + quad[k][2]) + quad[k][3]
                share = _rcopy(outs[k], outs[k], send, recv, 16 + ki, (x, y, 1 - c))
                share.start()
                for cp in cps2:
                    cp.wait_send()
                share.wait_send()

        for ki, k in enumerate(_CLASSES):
            @pl.when(c != _CLASS_OWNER[k])
            def _(ki=ki, k=k):
                _rcopy(outs[k], outs[k], send, recv, 16 + ki, (x, y, c)).wait_recv()

    vm = pl.BlockSpec(memory_space=pltpu.VMEM)
    return pl.pallas_call(
        body, name="small_allreduce", in_specs=[vm] * nin, out_specs=[vm] * 4,
        out_shape=[jax.ShapeDtypeStruct(_CLASS_SHAPE[k], F32) for k in _CLASSES],
        scratch_shapes=[pltpu.VMEM((2,) + _CLASS_SHAPE[k], F32) for k in _CLASSES]
        + [pltpu.VMEM((NCHIP,) + _CLASS_SHAPE[k], F32) for k in _CLASSES]
        + [pltpu.SemaphoreType.DMA((20,)), pltpu.SemaphoreType.DMA((20,))],
        compiler_params=pltpu.CompilerParams(vmem_limit_bytes=VMEM_LIMIT),
    )(*[g[n] for n in names], loss_blk)


def _adam_small(red, w, m, v):
    names = [s[0] for s in _SMALL]
    n = len(names)

    def body(*refs):
        red_refs = dict(zip(_CLASSES, refs[:4]))
        w_refs, m_refs, v_refs = refs[4:4 + n], refs[4 + n:4 + 2 * n], refs[4 + 2 * n:4 + 3 * n]
        loss_ref = refs[4 + 3 * n]
        out_refs = refs[5 + 3 * n:]
        x, y, _ = _mesh_pos()
        chip = 2 * x + y
        loss_ref[...] = jnp.broadcast_to(red_refs["a"][_LOSS_ROW:_LOSS_ROW + 1, 0:1], loss_ref.shape)
        for pi, (name, k, r0, nr, width) in enumerate(_SMALL):
            gfull = red_refs[k][r0:r0 + nr, :]
            if name == "rec_conv_w":
                parts = [gfull[:, 128 * s:128 * (s + 1)] for s in range(NCHIP)]
                g = jnp.where(chip == 0, parts[0], jnp.where(chip == 1, parts[1], jnp.where(chip == 2, parts[2], parts[3])))
            elif name == "ffn_conv_w":
                parts = [gfull[:, FC * s:FC * (s + 1)] for s in range(NCHIP)]
                g = jnp.where(chip == 0, parts[0], jnp.where(chip == 1, parts[2], jnp.where(chip == 2, parts[1], parts[3])))
            elif name == "ffn_conv_b":
                g = jnp.concatenate([gfull[:, FC * s:FC * (s + 1)] for s in (0, 2, 1, 3)], axis=1)
            else:
                g = gfull
            d, m2, v2 = _adam_math(w_refs[pi][...], g, m_refs[pi][...], v_refs[pi][...])
            o = out_refs[4 * pi:4 * pi + 4]
            o[0][...], o[1][...], o[2][...], o[3][...] = g, d, m2, v2

    vm = pl.BlockSpec(memory_space=pltpu.VMEM)
    outs = [jax.ShapeDtypeStruct((1, 128), F32)]
    for name in names:
        outs += [jax.ShapeDtypeStruct(w[name].shape, F32)] * 4
    res = pl.pallas_call(
        body, name="adam_small", in_specs=[vm] * (4 + 3 * n), out_specs=[vm] * len(outs), out_shape=outs,
        compiler_params=pltpu.CompilerParams(vmem_limit_bytes=VMEM_LIMIT),
    )(*red, *[w[k] for k in names], *[m[k] for k in names], *[v[k] for k in names])
    return res[0], {name: res[1 + 4 * i:5 + 4 * i] for i, name in enumerate(names)}


_WEIGHTS = ("g_mix", "w_in", "q_norm_g", "k_norm_g", "rec_conv_w", "rec_conv_b", "w_rg", "b_rg", "w_ig", "b_ig",
            "lru_lambda", "g_attn_out", "g_rec_out", "w_out", "g_ffn", "w_up", "ffn_conv_w", "ffn_conv_b", "w_down")
_BIG = ("w_in", "w_out", "w_up", "w_down")
_BIG_PERM = {"w_in": False, "w_out": False, "w_up": True, "w_down": False}
_SMALL_2D = {"w_rg": (RW, HD), "w_ig": (RW, HD), "b_rg": (8, HD), "b_ig": (8, HD), "rec_conv_w": (4, 128),
             "ffn_conv_w": (3, FC)}


def _halves(a):
    r, c = a.shape
    return a.reshape(2, r // 2, c)


def kernel(x, positions, g_mix, w_in, q_norm_g, k_norm_g, rec_conv_w, rec_conv_b, w_rg, b_rg, w_ig, b_ig, lru_lambda, g_attn_out, g_rec_out, w_out, g_ffn, w_up, ffn_conv_w, ffn_conv_b, w_down, loss_target, m_g_mix, m_w_in, m_q_norm_g, m_k_norm_g, m_rec_conv_w, m_rec_conv_b, m_w_rg, m_b_rg, m_w_ig, m_b_ig, m_lru_lambda, m_g_attn_out, m_g_rec_out, m_w_out, m_g_ffn, m_w_up, m_ffn_conv_w, m_ffn_conv_b, m_w_down, v_g_mix, v_w_in, v_q_norm_g, v_k_norm_g, v_rec_conv_w, v_rec_conv_b, v_w_rg, v_b_rg, v_w_ig, v_b_ig, v_lru_lambda, v_g_attn_out, v_g_rec_out, v_w_out, v_g_ffn, v_w_up, v_ffn_conv_w, v_ffn_conv_b, v_w_down):
    given = dict(g_mix=g_mix, w_in=w_in, q_norm_g=q_norm_g, k_norm_g=k_norm_g, rec_conv_w=rec_conv_w, rec_conv_b=rec_conv_b, w_rg=w_rg, b_rg=b_rg, w_ig=w_ig, b_ig=b_ig, lru_lambda=lru_lambda, g_attn_out=g_attn_out, g_rec_out=g_rec_out, w_out=w_out, g_ffn=g_ffn, w_up=w_up, ffn_conv_w=ffn_conv_w, ffn_conv_b=ffn_conv_b, w_down=w_down)
    given_m = dict(g_mix=m_g_mix, w_in=m_w_in, q_norm_g=m_q_norm_g, k_norm_g=m_k_norm_g, rec_conv_w=m_rec_conv_w, rec_conv_b=m_rec_conv_b, w_rg=m_w_rg, b_rg=m_b_rg, w_ig=m_w_ig, b_ig=m_b_ig, lru_lambda=m_lru_lambda, g_attn_out=m_g_attn_out, g_rec_out=m_g_rec_out, w_out=m_w_out, g_ffn=m_g_ffn, w_up=m_w_up, ffn_conv_w=m_ffn_conv_w, ffn_conv_b=m_ffn_conv_b, w_down=m_w_down)
    given_v = dict(g_mix=v_g_mix, w_in=v_w_in, q_norm_g=v_q_norm_g, k_norm_g=v_k_norm_g, rec_conv_w=v_rec_conv_w, rec_conv_b=v_rec_conv_b, w_rg=v_w_rg, b_rg=v_b_rg, w_ig=v_w_ig, b_ig=v_b_ig, lru_lambda=v_lru_lambda, g_attn_out=v_g_attn_out, g_rec_out=v_g_rec_out, w_out=v_w_out, g_ffn=v_g_ffn, w_up=v_w_up, ffn_conv_w=v_ffn_conv_w, ffn_conv_b=v_ffn_conv_b, w_down=v_w_down)
    shapes = {n: a.shape for n, a in given.items()}

    def two_d(n, a):
        a = a[0]
        return a.reshape(_SMALL_2D[n]) if n in _SMALL_2D else (a if a.ndim == 2 else a[None])

    w = {n: two_d(n, a) for n, a in given.items()}
    m = {n: two_d(n, a) for n, a in given_m.items()}
    v = {n: two_d(n, a) for n, a in given_v.items()}
    cc = lax.axis_index("c").astype(jnp.int32)
    cx, cy = lax.axis_index("x").astype(jnp.int32), lax.axis_index("y").astype(jnp.int32)
    slot = {False: 2 * cx + cy, True: 2 * cy + cx}

    shards = {"w_in": _halves(_cast_bf16("cast_w_in", w["w_in"]))}
    first = [shards["w_in"], jnp.pad(w["ffn_conv_w"], ((0, 5), (0, 0))), jnp.pad(w["rec_conv_w"], ((0, 4), (0, 0)))]
    first_perm = [False, True, False]
    first_plan = _gather_half_plan(first_perm, [True, False, False])
    in_flight = _split_start(
        "gather_in_start", first,
        [lax.dynamic_update_slice(lax.empty((NCHIP,) + a.shape, a.dtype), a[None], (slot[pm],) + (0,) * a.ndim)
         for a, pm in zip(first, first_perm)], first_plan, 3 * len(first))
    for n in ("w_out", "w_up", "w_down"):
        shards[n] = _halves(_cast_bf16(f"cast_{n}", w[n], after=(in_flight[4],)))
    p = {n: w[n] for n in ("g_mix", "g_ffn", "q_norm_g", "k_norm_g", "rec_conv_b", "lru_lambda", "g_attn_out", "g_rec_out")}
    p.update(w_rg=w["w_rg"].reshape(8, HD, HD), w_ig=w["w_ig"].reshape(8, HD, HD), b_rg=w["b_rg"], b_ig=w["b_ig"],
             ffn_conv_b=jnp.concatenate([w["ffn_conv_b"][:, FC * s:FC * (s + 1)] for s in (0, 2, 1, 3)], axis=1))

    class Exchange:
        rest = ("w_out", "w_up", "w_down")
        order = []
        flight = {}

        def wait_first(self, after):
            send, recv, srcs, lands, _ = in_flight
            f_in, f_fcw, f_rcw = _split_wait("gather_in_wait", send, recv, srcs, lands, first_plan,
                                             (after,) + tuple(shards[n] for n in self.rest))
            (f_in,) = _sibling_fill([f_in], [False])
            return dict(w_in=f_in.reshape(NCHIP, D, INW // NCHIP), ffn_conv_w=f_fcw,
                        rec_conv_w=f_rcw.transpose(1, 0, 2).reshape(8, RW))

        def start_rest(self):
            srcs = [shards[n] for n in self.rest]
            lands = [lax.dynamic_update_slice(lax.empty((NCHIP,) + s.shape, BF16), s[None], (slot[_BIG_PERM[n]], 0, 0, 0))
                     for n, s in zip(self.rest, srcs)]
            plan = _gather_plan([_BIG_PERM[n] for n in self.rest])
            send, recv, srcs, lands, token = _split_start("gather_rest_start", srcs, lands, plan, 6 * len(srcs))
            self.flight["rest"] = (send, recv, srcs, lands, plan)
            return (token,)

        def wait_rest(self, after):
            send, recv, srcs, lands, plan = self.flight.pop("rest")
            f_out, f_up, f_down = _split_wait("gather_rest_wait", send, recv, srcs, lands, plan, after)
            return dict(w_out=f_out.reshape(D, D), w_up=f_up.reshape(NCHIP, D, FC), w_down=f_down.reshape(DFF, D))

        def reduce_start(self, name, g32, g16):
            r2, cols = shards[name].shape[1:]
            plan = _reduce_plan(_BIG_PERM[name])
            send, recv, srcs, lands, token = _split_start(
                f"reduce_{name}_start", [g16.reshape(NCHIP, 2, r2, cols)], [lax.empty((7, r2, cols), BF16)], plan, 7)
            self.flight[name] = (send, recv, srcs, lands, plan, g32.reshape(NCHIP, 2, r2, cols))
            self.order.append(name)
            return (token,)

        def finish(self, after):
            mine = {}
            for name in self.order:
                send, recv, srcs, lands, plan, g32 = self.flight.pop(name)
                (got,) = _split_wait(f"reduce_{name}_wait", send, recv, srcs, lands, plan, after)
                where = jnp.stack([slot[_BIG_PERM[name]], cc])
                mine[name] = after = _add_pieces(f"reduce_{name}_add", g32, got, where)
            theirs = dict(zip(_BIG, _sibling_share([mine[n] for n in _BIG])))
            return mine, theirs

    exch = Exchange()

    loss_blk, grad_x, g = _local_step(x[0], positions.reshape(T, 1), loss_target[0], p, exch)

    out_g, out_d, out_m, out_v = {}, {}, {}, {}
    red = _small_allreduce(g, loss_blk)
    loss_row, small_out = _adam_small(red, w, m, v)
    for n, (gn, dn, mn, vn) in small_out.items():
        out_g[n], out_d[n], out_m[n], out_v[n] = gn, dn, mn, vn

    mine, theirs = exch.finish(red[0])
    for n in _BIG:
        out_g[n], out_d[n], out_m[n], out_v[n] = _adam_big(f"adam_{n}", w[n], mine[n], theirs[n], m[n], v[n], cc.reshape(1))

    outs = [loss_row[0, 0], grad_x[None]]
    for group in (out_g, out_d, out_m, out_v):
        outs += [group[n].reshape(shapes[n]) for n in _WEIGHTS]
    return tuple(outs)
```
